```python
import jax, jax.numpy as jnp
from jax import lax
import numpy as np

D_MODEL = 1024
BATCH = 4
SEQ = 4096
DEPTH = 1
DEC_BATCH = 128
DEC_SEQ = 8
PAST_LEN = 8192
PAGE_SIZE = 128

MIX_W = D_MODEL
ATT_W = MIX_W // 2
MLSTM_W = MIX_W - ATT_W
HEAD_DIM = 64
N_HEADS = ATT_W // HEAD_DIM
N_KV_HEADS = 2
GQA_GROUP = N_HEADS // N_KV_HEADS
KV_W = N_KV_HEADS * HEAD_DIM
WINDOW = 128
BLOCK = WINDOW
M_HEADS = 4
M_HEAD_DIM = MLSTM_W // M_HEADS
CONV_W = 4
N_META = 16
META_PAD = BLOCK - N_META
D_FF = -(-8 * D_MODEL // (3 * 256)) * 256
IN_SPLITS = (ATT_W, KV_W, KV_W, MLSTM_W, MLSTM_W, MLSTM_W, M_HEADS, M_HEADS)
IN_W = sum(IN_SPLITS)
ALIBI_SLOPES = tuple(2.0 ** (-8.0 * (h + 1) / N_HEADS) for h in range(N_HEADS))
DEEPNORM_ALPHA = (2.0 * DEPTH) ** 0.25
DEEPNORM_BETA = (8.0 * DEPTH) ** -0.25
EPS = 1e-5

kernel_name = 'hymba_swa_sink_mlstm_deepnorm_step'


def layer_norm(x, g, b):
    xf = x.astype(jnp.float32)
    mu = jnp.mean(xf, axis=-1, keepdims=True)
    var = jnp.mean(jnp.square(xf - mu), axis=-1, keepdims=True)
    y = (xf - mu) * lax.rsqrt(var + EPS) * g.astype(jnp.float32) + b.astype(jnp.float32)
    return y.astype(x.dtype)


def head_norm(h, g):
    hf = h.astype(jnp.float32)
    mu = jnp.mean(hf, axis=-1, keepdims=True)
    var = jnp.mean(jnp.square(hf - mu), axis=-1, keepdims=True)
    y = ((hf - mu) * lax.rsqrt(var + EPS)).reshape(h.shape[:-2] + (-1,))
    return (y * g.astype(jnp.float32)).astype(h.dtype)


def split_proj(x, w_in):
    z = x @ w_in
    bounds = np.cumsum(IN_SPLITS)[:-1].tolist()
    return jnp.split(z, bounds, axis=-1)


def causal_conv(ext, w_conv, b_conv):
    l = ext.shape[1] - (CONV_W - 1)
    y = b_conv
    for j in range(CONV_W):
        y = y + ext[:, j:j + l] * w_conv[j]
    return y


def swa_attend(q, k, v, key_valid, sinks):
    f32 = jnp.float32
    lq, lk = q.shape[2], k.shape[2]
    s = jnp.einsum('nbqkgd,nbskd->nbkgqs', q.astype(f32), k.astype(f32)) * (HEAD_DIM ** -0.5)
    dist = WINDOW + jnp.arange(lq)[:, None] - jnp.arange(lk)[None, :]
    band = (dist >= 0) & (dist < WINDOW)
    slopes = jnp.asarray(ALIBI_SLOPES, f32).reshape(N_KV_HEADS, GQA_GROUP)
    alibi = -slopes[:, :, None, None] * dist.astype(f32)
    mask = band[None] & key_valid[:, None, :]
    s = jnp.where(mask[None, :, None, None], s + alibi, -jnp.inf)
    sink = sinks.astype(f32).reshape(N_KV_HEADS, GQA_GROUP)[None, None, :, :, None, None]
    mx = jnp.maximum(jnp.max(s, axis=-1, keepdims=True), sink)
    p = jnp.exp(s - mx)
    p = p / (jnp.sum(p, axis=-1, keepdims=True) + jnp.exp(sink - mx))
    return jnp.einsum('nbkgqs,nbskd->nbqkgd', p, v.astype(f32)).astype(v.dtype)


def mlstm_inputs(c_act, vm, i_pre, f_pre, w_mq, w_mk, b_i, b_f):
    n, l, _ = c_act.shape
    ca = c_act.reshape(n, l, M_HEADS, M_HEAD_DIM)
    q = jnp.einsum('nlhd,hde->nhle', ca, w_mq)
    k = jnp.einsum('nlhd,hde->nhle', ca, w_mk) * (M_HEAD_DIM ** -0.5)
    v = vm.reshape(n, l, M_HEADS, M_HEAD_DIM).transpose(0, 2, 1, 3)
    logi = (i_pre + b_i).astype(jnp.float32).transpose(0, 2, 1)
    logf = jax.nn.log_sigmoid((f_pre + b_f).astype(jnp.float32)).transpose(0, 2, 1)
    return q, k, v, logi, logf


def mlstm_chunk(C_s, n_s, m_s, q, k, v, logi, logf):
    f32 = jnp.float32
    C_s, n_s, m_s = C_s.astype(f32), n_s.astype(f32), m_s.astype(f32)
    q, k, v = q.astype(f32), k.astype(f32), v.astype(f32)
    l = q.shape[-2]
    b = jnp.cumsum(logf, axis=-1)
    causal = jnp.tril(jnp.ones((l, l), bool))
    dmat = jnp.where(causal, b[..., :, None] - b[..., None, :] + logi[..., None, :], -jnp.inf)
    m_inter = b + m_s[..., None]
    m_t = jnp.maximum(m_inter, jnp.max(dmat, axis=-1))
    w_inter = jnp.exp(m_inter - m_t)
    sc = jnp.einsum('nhtk,nhsk->nhts', q, k) * jnp.exp(dmat - m_t[..., None])
    num = w_inter[..., None] * jnp.einsum('nhvk,nhtk->nhtv', C_s, q) + jnp.einsum('nhts,nhsv->nhtv', sc, v)
    den = w_inter * jnp.einsum('nhk,nhtk->nht', n_s, q) + jnp.sum(sc, axis=-1)
    h = num / jnp.maximum(jnp.abs(den), jnp.exp(-m_t))[..., None]
    m_end = m_t[..., -1]
    decay = jnp.exp(b[..., -1] + m_s - m_end)
    wk = jnp.exp(b[..., -1:] - b + logi - m_end[..., None])
    C_new = decay[..., None, None] * C_s + jnp.einsum('nhs,nhsv,nhsk->nhvk', wk, v, k)
    n_new = decay[..., None] * n_s + jnp.einsum('nhs,nhsk->nhk', wk, k)
    return h, (C_new, n_new, m_end)


def mix_and_ffn(x, att, h_m, o_pre, g_attn, g_mlstm, w_out, ln1_g, ln1_b, w_gate, w_up, w_down, ln2_g, ln2_b):
    o = jax.nn.sigmoid(o_pre).reshape(h_m.shape)
    y = jnp.concatenate([head_norm(att, g_attn), head_norm(o * h_m, g_mlstm)], axis=-1)
    x1 = layer_norm(DEEPNORM_ALPHA * x + y @ w_out, ln1_g, ln1_b)
    ffn = (jax.nn.silu(x1 @ w_gate) * (x1 @ w_up)) @ w_down
    return layer_norm(DEEPNORM_ALPHA * x1 + ffn, ln2_g, ln2_b)


def prompt_layer(x, valid, key_valid, w_in, w_conv, b_conv, w_mq, w_mk, b_i, b_f, attn_sinks,
                 g_attn, g_mlstm, w_out, ln1_g, ln1_b, w_gate, w_up, w_down, ln2_g, ln2_b):
    n, p, _ = x.shape
    nb = p // BLOCK
    q, k, v, c, vm, o_pre, i_pre, f_pre = split_proj(x, w_in)
    k = k.reshape(n, p, N_KV_HEADS, HEAD_DIM)
    v = v.reshape(n, p, N_KV_HEADS, HEAD_DIM)

    def band_blocks(t):
        tp = jnp.pad(t, ((0, 0), (BLOCK, 0), (0, 0), (0, 0))).reshape(n, nb + 1, BLOCK, N_KV_HEADS, HEAD_DIM)
        return jnp.concatenate([tp[:, :-1], tp[:, 1:]], axis=2)

    att = swa_attend(q.reshape(n, nb, BLOCK, N_KV_HEADS, GQA_GROUP, HEAD_DIM),
                     band_blocks(k), band_blocks(v), key_valid, attn_sinks).reshape(n, p, N_HEADS, HEAD_DIM)

    c = jnp.where(valid[None, :, None], c, jnp.zeros_like(c))
    c_act = jax.nn.silu(causal_conv(jnp.pad(c, ((0, 0), (CONV_W - 1, 0), (0, 0))), w_conv, b_conv))
    qm, km, vmh, logi, logf = mlstm_inputs(c_act, vm, i_pre, f_pre, w_mq, w_mk, b_i, b_f)
    logi = jnp.where(valid, logi, -jnp.inf)
    logf = jnp.where(valid, logf, 0.0)

    def chunks(t):
        return jnp.moveaxis(t.reshape(t.shape[:2] + (nb, BLOCK) + t.shape[3:]), 2, 0)

    f32 = jnp.float32
    init = (jnp.zeros((n, M_HEADS, M_HEAD_DIM, M_HEAD_DIM), f32),
            jnp.zeros((n, M_HEADS, M_HEAD_DIM), f32),
            jnp.zeros((n, M_HEADS), f32))

    def step(carry, xs):
        h, new = mlstm_chunk(carry[0], carry[1], carry[2], xs[0], xs[1], xs[2], xs[3], xs[4])
        return new, h

    (C_f, n_f, m_f), hs = lax.scan(step, init, (chunks(qm), chunks(km), chunks(vmh), chunks(logi), chunks(logf)))
    h_m = jnp.moveaxis(hs, 0, 2).reshape(n, M_HEADS, p, M_HEAD_DIM).transpose(0, 2, 1, 3).astype(x.dtype)
    y = mix_and_ffn(x, att, h_m, o_pre, g_attn, g_mlstm, w_out, ln1_g, ln1_b, w_gate, w_up, w_down, ln2_g, ln2_b)
    return y, (k[:, -WINDOW:], v[:, -WINDOW:], c[:, -(CONV_W - 1):], C_f, n_f, m_f)


def sample_layer(x, cache_k, cache_v, state_conv, state_C, state_n, state_m, w_in, w_conv, b_conv, w_mq, w_mk,
                 b_i, b_f, attn_sinks, g_attn, g_mlstm, w_out, ln1_g, ln1_b, w_gate, w_up, w_down, ln2_g, ln2_b):
    n, l, _ = x.shape
    q, k, v, c, vm, o_pre, i_pre, f_pre = split_proj(x, w_in)
    k_all = jnp.concatenate([cache_k, k.reshape(n, l, N_KV_HEADS, HEAD_DIM)], axis=1)
    v_all = jnp.concatenate([cache_v, v.reshape(n, l, N_KV_HEADS, HEAD_DIM)], axis=1)
    key_valid = jnp.ones((1, WINDOW + l), bool)
    att = swa_attend(q.reshape(n, 1, l, N_KV_HEADS, GQA_GROUP, HEAD_DIM), k_all[:, None], v_all[:, None],
                     key_valid, attn_sinks).reshape(n, l, N_HEADS, HEAD_DIM)

    ext = jnp.concatenate([state_conv, c], axis=1)
    c_act = jax.nn.silu(causal_conv(ext, w_conv, b_conv))
    qm, km, vmh, logi, logf = mlstm_inputs(c_act, vm, i_pre, f_pre, w_mq, w_mk, b_i, b_f)
    h, (C_f, n_f, m_f) = mlstm_chunk(state_C, state_n, state_m, qm, km, vmh, logi, logf)
    h_m = h.transpose(0, 2, 1, 3).astype(x.dtype)
    y = mix_and_ffn(x, att, h_m, o_pre, g_attn, g_mlstm, w_out, ln1_g, ln1_b, w_gate, w_up, w_down, ln2_g, ln2_b)
    return y, (k_all[:, -WINDOW:], v_all[:, -WINDOW:], ext[:, -(CONV_W - 1):], C_f, n_f, m_f)


def setup_inputs(seed: int = 0) -> dict:
    key = jax.random.key(seed)
    ks = jax.random.split(key, 32)
    f32 = jnp.float32

    def nrm(k, shape, scale):
        return jax.random.normal(k, shape, f32) * scale

    return {
        'x_prompt': nrm(ks[0], (BATCH, SEQ, D_MODEL), 1.0),
        'x_sample': nrm(ks[1], (DEC_BATCH, DEC_SEQ, D_MODEL), 1.0),
        'cache_k': nrm(ks[2], (DEPTH, DEC_BATCH, WINDOW, N_KV_HEADS, HEAD_DIM), 1.0),
        'cache_v': nrm(ks[3], (DEPTH, DEC_BATCH, WINDOW, N_KV_HEADS, HEAD_DIM), 1.0),
        'state_conv': nrm(ks[4], (DEPTH, DEC_BATCH, CONV_W - 1, MLSTM_W), 1.0),
        'state_C': nrm(ks[5], (DEPTH, DEC_BATCH, M_HEADS, M_HEAD_DIM, M_HEAD_DIM), 0.1),
        'state_n': nrm(ks[6], (DEPTH, DEC_BATCH, M_HEADS, M_HEAD_DIM), 0.1),
        'state_m': nrm(ks[7], (DEPTH, DEC_BATCH, M_HEADS), 1.0),
        'meta_tokens': nrm(ks[8], (N_META, D_MODEL), 1.0),
        'w_in': nrm(ks[9], (DEPTH, D_MODEL, IN_W), D_MODEL ** -0.5),
        'w_conv': nrm(ks[10], (DEPTH, CONV_W, MLSTM_W), CONV_W ** -0.5),
        'b_conv': nrm(ks[11], (DEPTH, MLSTM_W), 0.02),
        'w_mq': nrm(ks[12], (DEPTH, M_HEADS, M_HEAD_DIM, M_HEAD_DIM), M_HEAD_DIM ** -0.5),
        'w_mk': nrm(ks[13], (DEPTH, M_HEADS, M_HEAD_DIM, M_HEAD_DIM), M_HEAD_DIM ** -0.5),
        'b_i': nrm(ks[14], (DEPTH, M_HEADS), 0.1),
        'b_f': jnp.broadcast_to(jnp.linspace(3.0, 6.0, M_HEADS, dtype=f32), (DEPTH, M_HEADS)) + nrm(ks[15], (DEPTH, M_HEADS), 0.01),
        'attn_sinks': nrm(ks[16], (DEPTH, N_HEADS), 0.5),
        'g_attn': 1.0 + nrm(ks[17], (DEPTH, ATT_W), 0.02),
        'g_mlstm': 1.0 + nrm(ks[18], (DEPTH, MLSTM_W), 0.02),
        'w_out': nrm(ks[19], (DEPTH, MIX_W, D_MODEL), MIX_W ** -0.5 * DEEPNORM_BETA),
        'ln1_g': 1.0 + nrm(ks[20], (DEPTH, D_MODEL), 0.02),
        'ln1_b': nrm(ks[21], (DEPTH, D_MODEL), 0.02),
        'w_gate': nrm(ks[22], (DEPTH, D_MODEL, D_FF), D_MODEL ** -0.5),
        'w_up': nrm(ks[23], (DEPTH, D_MODEL, D_FF), D_MODEL ** -0.5),
        'w_down': nrm(ks[24], (DEPTH, D_FF, D_MODEL), D_FF ** -0.5 * DEEPNORM_BETA),
        'ln2_g': 1.0 + nrm(ks[25], (DEPTH, D_MODEL), 0.02),
        'ln2_b': nrm(ks[26], (DEPTH, D_MODEL), 0.02),
    }


def reference(x_prompt, x_sample, cache_k, cache_v, state_conv, state_C, state_n, state_m, meta_tokens,
              w_in, w_conv, b_conv, w_mq, w_mk, b_i, b_f, attn_sinks, g_attn, g_mlstm, w_out,
              ln1_g, ln1_b, w_gate, w_up, w_down, ln2_g, ln2_b):
    b, s, d = x_prompt.shape
    p = s + BLOCK
    meta = jnp.broadcast_to(meta_tokens[None].astype(x_prompt.dtype), (b, N_META, d))
    xp = jnp.concatenate([jnp.zeros((b, META_PAD, d), x_prompt.dtype), meta, x_prompt], axis=1)
    valid = jnp.arange(p) >= META_PAD
    nb = p // BLOCK
    key_pos = (jnp.arange(nb)[:, None] - 1) * BLOCK + jnp.arange(2 * BLOCK)[None, :]
    key_valid = key_pos >= META_PAD
    xs = x_sample
    p_states, s_states = [], []
    for layer in range(DEPTH):
        lw = (w_in[layer], w_conv[layer], b_conv[layer], w_mq[layer], w_mk[layer], b_i[layer], b_f[layer],
              attn_sinks[layer], g_attn[layer], g_mlstm[layer], w_out[layer], ln1_g[layer], ln1_b[layer],
              w_gate[layer], w_up[layer], w_down[layer], ln2_g[layer], ln2_b[layer])
        xp, ps = prompt_layer(xp, valid, key_valid, *lw)
        xs, ss = sample_layer(xs, cache_k[layer], cache_v[layer], state_conv[layer], state_C[layer],
                              state_n[layer], state_m[layer], *lw)
        p_states.append(ps)
        s_states.append(ss)
    pk, pv, pconv, pC, pn, pm = [jnp.stack(t) for t in zip(*p_states)]
    sk, sv, sconv, sC, sn, sm = [jnp.stack(t) for t in zip(*s_states)]
    y_prompt = xp[:, BLOCK:]
    return (y_prompt, xs, pk, pv, pconv, pC, pn, pm, sk, sv, sconv, sC, sn, sm)
```

```python
import numpy as np
import jax
import jax.numpy as jnp
from jax import lax
from jax.experimental import pallas as pl
from jax.experimental.pallas import tpu as pltpu

F32 = jnp.float32
BF16 = jnp.bfloat16

D_MODEL = 1024
ATT_W = 512
MLSTM_W = 512
HEAD_DIM = 64
N_HEADS = 8
KV_W = 128
WINDOW = 128
BLOCK = 128
M_HEADS = 4
M_HEAD_DIM = 128
CONV_W = 4
N_META = 16
META_PAD = BLOCK - N_META
D_FF = 2816
IN_MAIN = 2304
IN_GATES = 2 * M_HEADS
IN_PAD = IN_MAIN + 128
DEPTH = 1
ALIBI_SLOPES = tuple(2.0 ** (-8.0 * (h + 1) / N_HEADS) for h in range(N_HEADS))
DEEPNORM_ALPHA = (2.0 * DEPTH) ** 0.25
EPS = 1e-5
SEQ_TILE = 16
POST_ROWS = 256
VMEM_LIMIT = 56 * 1024 * 1024

_ATT_PERM = np.concatenate(
    [np.concatenate([np.arange(r * HEAD_DIM, (r + 1) * HEAD_DIM),
                     np.arange((r + 4) * HEAD_DIM, (r + 5) * HEAD_DIM)]) for r in range(4)])


def _dot(a, b):
    return jnp.dot(a, b, preferred_element_type=F32)


def _dot_nt(a, b):
    return lax.dot_general(a, b, (((1,), (1,)), ((), ())), preferred_element_type=F32)


def _dot_tn(a, b):
    return lax.dot_general(a, b, (((0,), (0,)), ((), ())), preferred_element_type=F32)


def _split2(x):
    hi = x.astype(BF16)
    lo = (x - hi.astype(F32)).astype(BF16)
    return hi, lo


def _sigmoid(x):
    return 1.0 / (1.0 + jnp.exp(-x))


def _log_sigmoid(x):
    return -(jnp.maximum(-x, 0.0) + jnp.log1p(jnp.exp(-jnp.abs(x))))


def _layer_norm(x, g, b):
    mu = jnp.mean(x, axis=-1, keepdims=True)
    xc = x - mu
    var = jnp.mean(xc * xc, axis=-1, keepdims=True)
    return xc * lax.rsqrt(var + EPS) * g + b


def _attn_head_norm(att, bavg, g):
    hi, lo = _split2(att)
    mu = _dot(hi, bavg) + _dot(lo, bavg)
    xc = att - mu
    var = _dot((xc * xc).astype(BF16), bavg)
    return xc * lax.rsqrt(var + EPS) * g


def _mlstm_head_norm(hm, g):
    mu = jnp.mean(hm, axis=-1, keepdims=True)
    xc = hm - mu
    var = jnp.mean(xc * xc, axis=-1, keepdims=True)
    return xc * lax.rsqrt(var + EPS) * g


def _softmax_parts(parts, sink):
    sp = [jnp.where(m, s * (HEAD_DIM ** -0.5) + a, -jnp.inf) for s, m, a in parts]
    mx = sink
    for s in sp:
        mx = jnp.maximum(mx, jnp.max(s, axis=1, keepdims=True))
    ps = [jnp.exp(s - mx) for s in sp]
    den = jnp.exp(sink - mx)
    for p in ps:
        den = den + jnp.sum(p, axis=1, keepdims=True)
    return ps, 1.0 / den


def _mlstm_intra(qm_b, km_b, vh_b, mask, b_c, b_r, li_r, m_prev):
    dmat = jnp.where(mask, b_c - b_r + li_r, -jnp.inf)
    m_inter = b_c + m_prev
    m_t = jnp.maximum(m_inter, jnp.max(dmat, axis=1, keepdims=True))
    w_inter = jnp.exp(m_inter - m_t)
    sc = _dot_nt(qm_b, km_b) * jnp.exp(dmat - m_t)
    num = _dot(sc.astype(BF16), vh_b)
    den = jnp.sum(sc, axis=1, keepdims=True)
    return m_t, w_inter, num, den


def _conv_silu(window, wconv_ref, bconv_ref):
    acc = bconv_ref[...]
    for j in range(CONV_W):
        acc = acc + window(j) * wconv_ref[j:j + 1, :]
    return acc * _sigmoid(acc)


def _gate_columns(gates, gb_ref, row_valid, ltri, extra=None):
    lane = lax.broadcasted_iota(jnp.int32, (BLOCK, 128), 1)
    gb = gates + gb_ref[...]
    is_i = lane < M_HEADS
    is_f = (lane >= M_HEADS) & (lane < 2 * M_HEADS)
    logf = _log_sigmoid(gb)
    if row_valid is not None:
        lf = jnp.where(is_f & row_valid, logf, 0.0)
        li = jnp.where(row_valid, gb, -jnp.inf)
    else:
        lf = jnp.where(is_f, logf, 0.0)
        li = gb
    hi, lo = _split2(lf)
    bc = _dot(ltri, hi) + _dot(ltri, lo)
    x = jnp.where(is_i, li, bc)
    if extra is None:
        return x, None
    return x, _dot(extra, hi) + _dot(extra, lo)


def _prompt_kernel(x_ref, meta_ref, w_in_ref, wconv_ref, bconv_ref, wqk_ref, gb_ref, sinks_ref,
                   gattn_ref, gml_ref, bavg_ref,
                   ymix_ref, pk_ref, pv_ref, pconv_ref, pc_ref, pn_ref, pm_ref,
                   kplo, kphi, vplo, vphi, cbuf, ct, nst, mst):
    j = pl.program_id(1)
    last = pl.num_programs(1) - 1

    @pl.when(j == 0)
    def _init():
        zb = jnp.zeros((BLOCK, KV_W), BF16)
        kplo[...] = zb
        kphi[...] = zb
        vplo[...] = zb
        vphi[...] = zb
        cbuf[0:8, :] = jnp.zeros((8, MLSTM_W), F32)
        ct[...] = jnp.zeros_like(ct)
        nst[...] = jnp.zeros_like(nst)
        mst[...] = jnp.zeros_like(mst)

    xb = jnp.where(j == 0, meta_ref[...], x_ref[...])
    z = _dot(xb.astype(BF16), w_in_ref[...])

    row = lax.broadcasted_iota(jnp.int32, (BLOCK, 1), 0)
    row_valid = (row + j * BLOCK) >= META_PAD
    lane = lax.broadcasted_iota(jnp.int32, (BLOCK, 128), 1)
    lo_half = lane < HEAD_DIM

    kb = z[:, ATT_W:ATT_W + KV_W].astype(BF16)
    vb = z[:, ATT_W + KV_W:ATT_W + 2 * KV_W].astype(BF16)
    zero_b = jnp.zeros_like(kb)
    k_lo, k_hi = jnp.where(lo_half, kb, zero_b), jnp.where(lo_half, zero_b, kb)
    v_lo, v_hi = jnp.where(lo_half, vb, zero_b), jnp.where(lo_half, zero_b, vb)
    kmask = jnp.concatenate([kplo[...], k_lo, kphi[...], k_hi], axis=0)
    vmask = jnp.concatenate([vplo[...], v_lo, vphi[...], v_hi], axis=0)
    q4 = jnp.concatenate([z[:, r * 128:(r + 1) * 128] for r in range(4)], axis=0).astype(BF16)
    s_all = _dot_nt(q4, kmask)

    qi = lax.broadcasted_iota(jnp.int32, (BLOCK, 2 * BLOCK), 0)
    ks = lax.broadcasted_iota(jnp.int32, (BLOCK, 2 * BLOCK), 1)
    dist = WINDOW + qi - ks
    key_pos = (j - 1) * BLOCK + ks
    att_mask = (dist >= 0) & (dist < WINDOW) & (key_pos >= META_PAD)
    distf = dist.astype(F32)

    p_rows, inv = [], {}
    for r in range(4):
        pr = []
        for c in range(2):
            h = r + 4 * c
            s_blk = s_all[r * 128:(r + 1) * 128, c * 256:(c + 1) * 256]
            (p,), inv[h] = _softmax_parts([(s_blk, att_mask, -ALIBI_SLOPES[h] * distf)], sinks_ref[h])
            pr.append(p.astype(BF16))
        p_rows.append(jnp.concatenate(pr, axis=1))
    o_all = _dot(jnp.concatenate(p_rows, axis=0), vmask)
    att = jnp.concatenate(
        [o_all[r * 128:(r + 1) * 128] * jnp.where(lo_half, inv[r], inv[r + 4]) for r in range(4)], axis=1)
    y_att = _attn_head_norm(att, bavg_ref[...], gattn_ref[...])

    kplo[...] = k_lo
    kphi[...] = k_hi
    vplo[...] = v_lo
    vphi[...] = v_hi

    c_in = jnp.where(row_valid, z[:, 768:1280], 0.0)
    cbuf[8:8 + BLOCK, :] = c_in
    c_act = _conv_silu(lambda jj: cbuf[5 + jj:5 + jj + BLOCK, :], wconv_ref, bconv_ref)
    cbuf[0:8, :] = cbuf[BLOCK:BLOCK + 8, :]

    ti = lax.broadcasted_iota(jnp.int32, (BLOCK, BLOCK), 0)
    si = lax.broadcasted_iota(jnp.int32, (BLOCK, BLOCK), 1)
    tri = si <= ti
    ltri = jnp.where(tri, 1.0, 0.0).astype(BF16)
    xg, _ = _gate_columns(z[:, IN_MAIN:IN_PAD], gb_ref, row_valid, ltri)
    xgt = xg.T

    y_m = []
    for h in range(M_HEADS):
        sl = slice(h * 128, (h + 1) * 128)
        qk = _dot(c_act[:, sl].astype(BF16), wqk_ref[h])
        qm_b = qk[:, :128].astype(BF16)
        km = qk[:, 128:] * (M_HEAD_DIM ** -0.5)
        km_b = km.astype(BF16)
        vh = z[:, 1280 + h * 128:1280 + (h + 1) * 128]
        b_c, li_c = xg[:, 4 + h:5 + h], xg[:, h:h + 1]
        b_r, li_r = xgt[4 + h:5 + h, :], xgt[h:h + 1, :]
        m_prev = mst[h:h + 1, 0:1]
        m_t, w_inter, num, den = _mlstm_intra(qm_b, km_b, vh.astype(BF16), tri, b_c, b_r, li_r, m_prev)
        c_old = ct[h]
        n_old = nst[h:h + 1, :]
        num = num + w_inter * _dot_nt(qm_b, c_old.astype(BF16))
        den = den + w_inter * jnp.sum(qk[:, :128] * n_old, axis=1, keepdims=True)
        hh = num / jnp.maximum(jnp.abs(den), jnp.exp(-m_t))

        m_end = m_t[BLOCK - 1:BLOCK, :]
        b_last = b_c[BLOCK - 1:BLOCK, :]
        decay = jnp.exp(b_last + m_prev - m_end)
        wk = jnp.exp(b_last - b_c + li_c - m_end)
        ct[h] = decay * c_old + _dot_tn((vh * wk).astype(BF16), km_b)
        nst[h:h + 1, :] = decay * n_old + jnp.sum(km * wk, axis=0, keepdims=True)
        mst[h:h + 1, :] = jnp.broadcast_to(m_end, (1, 128))

        hm = _sigmoid(z[:, 1792 + h * 128:1792 + (h + 1) * 128]) * hh
        y_m.append(_mlstm_head_norm(hm, gml_ref[:, sl]))

    ymix_ref[...] = jnp.concatenate([y_att] + y_m, axis=1).astype(BF16)

    @pl.when(j == last)
    def _final():
        pk_ref[...] = z[:, ATT_W:ATT_W + KV_W]
        pv_ref[...] = z[:, ATT_W + KV_W:ATT_W + 2 * KV_W]
        pconv_ref[...] = cbuf[0:8, :]
        pc_ref[...] = ct[...]
        pn_ref[...] = nst[...]
        pm_ref[...] = mst[...]


def _sample_kernel(xs_ref, ck_ref, cv_ref, sconv_ref, sc_ref, sn_ref, sm_ref,
                   w_in_ref, wconv_ref, bconv_ref, wqk_ref, gb_ref, sinks_ref,
                   gattn_ref, gml_ref, bavg_ref,
                   ymix_ref, sk_ref, sv_ref, scv_ref, sco_ref, sno_ref, smo_ref, cext):
    ns, L = SEQ_TILE, 8
    xs = xs_ref[...].reshape(ns * L, D_MODEL)
    z = _dot(xs.astype(BF16), w_in_ref[...])

    lane = lax.broadcasted_iota(jnp.int32, (BLOCK, 128), 1)
    lo_half = lane < HEAD_DIM
    lo3 = lax.broadcasted_iota(jnp.int32, (ns, L, 128), 2) < HEAD_DIM
    ti = lax.broadcasted_iota(jnp.int32, (BLOCK, BLOCK), 0)
    si = lax.broadcasted_iota(jnp.int32, (BLOCK, BLOCK), 1)
    same_seq = (ti >> 3) == (si >> 3)
    tl, sl_ = ti & 7, si & 7
    causal_seq = same_seq & (sl_ <= tl)

    k_new = z[:, ATT_W:ATT_W + KV_W]
    v_new = z[:, ATT_W + KV_W:ATT_W + 2 * KV_W]
    kb, vb = k_new.astype(BF16), v_new.astype(BF16)
    zero_b = jnp.zeros_like(kb)
    kn_mask = jnp.concatenate([jnp.where(lo_half, kb, zero_b), jnp.where(lo_half, zero_b, kb)], axis=0)
    vn_mask = jnp.concatenate([jnp.where(lo_half, vb, zero_b), jnp.where(lo_half, zero_b, vb)], axis=0)
    qg = [z[:, r * 128:(r + 1) * 128] for r in range(4)]
    s_new = _dot_nt(jnp.concatenate(qg, axis=0).astype(BF16), kn_mask)
    qg3 = [q.reshape(ns, L, 128) for q in qg]
    q8 = jnp.concatenate([jnp.where(lo3, q, 0.0) for q in qg3] + [jnp.where(lo3, 0.0, q) for q in qg3],
                         axis=1).astype(BF16)
    s_cache = jnp.einsum('nqd,nkd->nqk', q8, ck_ref[...].astype(BF16), preferred_element_type=F32)

    mask_c = si > tl
    dist_c = (WINDOW + tl - si).astype(F32)
    dist_n = (tl - sl_).astype(F32)
    pc_parts, pn_rows, inv = {}, [], {}
    for r in range(4):
        pn = []
        for c in range(2):
            h = r + 4 * c
            sc_c = s_cache[:, c * 32 + r * L:c * 32 + (r + 1) * L, :].reshape(BLOCK, 128)
            sc_n = s_new[r * 128:(r + 1) * 128, c * 128:(c + 1) * 128]
            (p_c, p_n), inv[h] = _softmax_parts(
                [(sc_c, mask_c, -ALIBI_SLOPES[h] * dist_c), (sc_n, causal_seq, -ALIBI_SLOPES[h] * dist_n)],
                sinks_ref[h])
            pc_parts[(c, r)] = p_c.reshape(ns, L, 128)
            pn.append(p_n.astype(BF16))
        pn_rows.append(jnp.concatenate(pn, axis=1))
    p8 = jnp.concatenate([pc_parts[(c, r)] for c in range(2) for r in range(4)], axis=1).astype(BF16)
    o_cache = jnp.einsum('nqk,nkd->nqd', p8, cv_ref[...].astype(BF16), preferred_element_type=F32)
    o_new = _dot(jnp.concatenate(pn_rows, axis=0), vn_mask)
    groups = []
    for r in range(4):
        oa = o_cache[:, r * L:(r + 1) * L, :].reshape(BLOCK, 128)
        ob = o_cache[:, 32 + r * L:32 + (r + 1) * L, :].reshape(BLOCK, 128)
        o = jnp.where(lo_half, oa, ob) + o_new[r * 128:(r + 1) * 128]
        groups.append(o * jnp.where(lo_half, inv[r], inv[r + 4]))
    y_att = _attn_head_norm(jnp.concatenate(groups, axis=1), bavg_ref[...], gattn_ref[...])

    sk_ref[:, 0:WINDOW - L, :] = ck_ref[:, L:WINDOW, :]
    sk_ref[:, WINDOW - L:WINDOW, :] = k_new.reshape(ns, L, 128)
    sv_ref[:, 0:WINDOW - L, :] = cv_ref[:, L:WINDOW, :]
    sv_ref[:, WINDOW - L:WINDOW, :] = v_new.reshape(ns, L, 128)

    c3 = z[:, 768:1280].reshape(ns, L, MLSTM_W)
    cext[:, 0:8, :] = sconv_ref[...]
    cext[:, 8:16, :] = c3
    scv_ref[...] = c3
    c_act = _conv_silu(lambda jj: cext[:, 5 + jj:5 + jj + L, :], wconv_ref, bconv_ref)
    c_act = c_act.reshape(BLOCK, MLSTM_W)

    lblk = jnp.where(causal_seq, 1.0, 0.0).astype(BF16)
    lseq = jnp.where(same_seq, 1.0, 0.0).astype(BF16)
    xg, blast = _gate_columns(z[:, IN_MAIN:IN_PAD], gb_ref, None, lblk, extra=lseq)
    xgt = xg.T
    m_rep = jnp.broadcast_to(sm_ref[...], (ns, L, 128)).reshape(BLOCK, 128)
    lane3 = lax.broadcasted_iota(jnp.int32, (ns, 1, 128), 2)
    m_out = jnp.zeros((ns, 1, 128), F32)

    y_m = []
    for h in range(M_HEADS):
        sl = slice(h * 128, (h + 1) * 128)
        qk = _dot(c_act[:, sl].astype(BF16), wqk_ref[h])
        qm = qk[:, :128]
        qm_b = qm.astype(BF16)
        km = qk[:, 128:] * (M_HEAD_DIM ** -0.5)
        km_b = km.astype(BF16)
        vh = z[:, 1280 + h * 128:1280 + (h + 1) * 128]
        b_c, li_c = xg[:, 4 + h:5 + h], xg[:, h:h + 1]
        b_r, li_r = xgt[4 + h:5 + h, :], xgt[h:h + 1, :]
        bl_c = blast[:, 4 + h:5 + h]
        m_prev = m_rep[:, h:h + 1]
        m_t, w_inter, num, den = _mlstm_intra(qm_b, km_b, vh.astype(BF16), causal_seq, b_c, b_r, li_r, m_prev)

        c_old = sc_ref[:, h]
        cq = _dot_nt(qm_b, c_old.reshape(ns * 128, 128).astype(BF16))
        num_inter = jnp.concatenate([cq[g * L:(g + 1) * L, g * 128:(g + 1) * 128] for g in range(ns)], axis=0)
        n_old = sn_ref[:, h:h + 1, :]
        n_rep = jnp.broadcast_to(n_old, (ns, L, 128)).reshape(BLOCK, 128)
        num = num + w_inter * num_inter
        den = den + w_inter * jnp.sum(qm * n_rep, axis=1, keepdims=True)
        hh = num / jnp.maximum(jnp.abs(den), jnp.exp(-m_t))

        m_end3 = jnp.broadcast_to(m_t, (BLOCK, 128)).reshape(ns, L, 128)[:, L - 1:L, :]
        m_end = jnp.broadcast_to(m_end3, (ns, L, 128)).reshape(BLOCK, 128)[:, 0:1]
        decay = jnp.exp(bl_c + m_prev - m_end)
        wk = jnp.exp(bl_c - b_c + li_c - m_end)
        k_big = jnp.concatenate(
            [jnp.where((ti >> 3) == g, km_b, jnp.zeros_like(km_b)) for g in range(ns)], axis=1)
        c_up = _dot_tn((vh * wk).astype(BF16), k_big)
        c_up3 = jnp.stack([c_up[:, g * 128:(g + 1) * 128] for g in range(ns)], axis=0)
        decay3 = jnp.broadcast_to(decay, (BLOCK, 128)).reshape(ns, L, 128)[:, 0:1, :]
        sco_ref[:, h] = decay3 * c_old + c_up3
        sno_ref[:, h:h + 1, :] = decay3 * n_old + jnp.sum((km * wk).reshape(ns, L, 128), axis=1, keepdims=True)
        m_out = jnp.where(lane3 == h, m_end3, m_out)

        hm = _sigmoid(z[:, 1792 + h * 128:1792 + (h + 1) * 128]) * hh
        y_m.append(_mlstm_head_norm(hm, gml_ref[:, sl]))

    smo_ref[...] = m_out
    ymix_ref[...] = jnp.concatenate([y_att] + y_m, axis=1).astype(BF16)


def _post_kernel(x_ref, y_ref, wout_ref, ln1g_ref, ln1b_ref, wg_ref, wu_ref, wd_ref, ln2g_ref, ln2b_ref, o_ref):
    x = x_ref[...]
    x1 = _layer_norm(DEEPNORM_ALPHA * x + _dot(y_ref[...], wout_ref[...]), ln1g_ref[...], ln1b_ref[...])
    x1b = x1.astype(BF16)
    g = _dot(x1b, wg_ref[...])
    u = _dot(x1b, wu_ref[...])
    hid = (g * _sigmoid(g) * u).astype(BF16)
    o_ref[...] = _layer_norm(DEEPNORM_ALPHA * x1 + _dot(hid, wd_ref[...]), ln2g_ref[...], ln2b_ref[...])


def _const_spec(shape):
    nd = len(shape)
    return pl.BlockSpec(shape, lambda *_: (0,) * nd, pipeline_mode=pl.Buffered(1))


def _mixer_weight_specs():
    return [
        _const_spec((D_MODEL, IN_PAD)),
        _const_spec((CONV_W, MLSTM_W)),
        _const_spec((1, MLSTM_W)),
        _const_spec((M_HEADS, M_HEAD_DIM, 2 * M_HEAD_DIM)),
        _const_spec((1, 128)),
        pl.BlockSpec(memory_space=pltpu.SMEM),
        _const_spec((1, ATT_W)),
        _const_spec((1, MLSTM_W)),
        _const_spec((ATT_W, ATT_W)),
    ]


def _prompt_mixer(x_prompt, meta_blk, mixer_w):
    b, s, d = x_prompt.shape
    nb = s // BLOCK + 1
    xmap = lambda bi, j: (bi, jnp.maximum(j - 1, 0), 0)
    bmap3 = lambda bi, j: (bi, 0, 0)
    out_shape = (
        jax.ShapeDtypeStruct((b, s, d), BF16),
        jax.ShapeDtypeStruct((b, BLOCK, KV_W), F32),
        jax.ShapeDtypeStruct((b, BLOCK, KV_W), F32),
        jax.ShapeDtypeStruct((b, 8, MLSTM_W), F32),
        jax.ShapeDtypeStruct((b, M_HEADS, M_HEAD_DIM, M_HEAD_DIM), F32),
        jax.ShapeDtypeStruct((b, 8, 128), F32),
        jax.ShapeDtypeStruct((b, 8, 128), F32),
    )
    out_specs = (
        pl.BlockSpec((None, BLOCK, d), xmap),
        pl.BlockSpec((None, BLOCK, KV_W), bmap3),
        pl.BlockSpec((None, BLOCK, KV_W), bmap3),
        pl.BlockSpec((None, 8, MLSTM_W), bmap3),
        pl.BlockSpec((None, M_HEADS, M_HEAD_DIM, M_HEAD_DIM), lambda bi, j: (bi, 0, 0, 0)),
        pl.BlockSpec((None, 8, 128), bmap3),
        pl.BlockSpec((None, 8, 128), bmap3),
    )
    return pl.pallas_call(
        _prompt_kernel,
        grid=(b, nb),
        in_specs=[pl.BlockSpec((None, BLOCK, d), xmap), _const_spec((BLOCK, d))] + _mixer_weight_specs(),
        out_specs=out_specs,
        out_shape=out_shape,
        scratch_shapes=[pltpu.VMEM((BLOCK, KV_W), BF16)] * 4 + [
            pltpu.VMEM((BLOCK + 8, MLSTM_W), F32),
            pltpu.VMEM((M_HEADS, M_HEAD_DIM, M_HEAD_DIM), F32),
            pltpu.VMEM((8, 128), F32),
            pltpu.VMEM((8, 128), F32),
        ],
        compiler_params=pltpu.CompilerParams(
            dimension_semantics=("arbitrary", "arbitrary"), vmem_limit_bytes=VMEM_LIMIT),
        name="prompt_mixer",
    )(x_prompt, meta_blk, *mixer_w)


def _sample_mixer(x_sample, ck, cv, sconv8, s_c, s_n, s_m3, mixer_w):
    n, l, d = x_sample.shape
    t = SEQ_TILE
    m3 = lambda i: (i, 0, 0)
    m4 = lambda i: (i, 0, 0, 0)
    out_shape = (
        jax.ShapeDtypeStruct((n * l, d), BF16),
        jax.ShapeDtypeStruct((n, WINDOW, KV_W), F32),
        jax.ShapeDtypeStruct((n, WINDOW, KV_W), F32),
        jax.ShapeDtypeStruct((n, l, MLSTM_W), F32),
        jax.ShapeDtypeStruct((n, M_HEADS, M_HEAD_DIM, M_HEAD_DIM), F32),
        jax.ShapeDtypeStruct((n, M_HEADS, M_HEAD_DIM), F32),
        jax.ShapeDtypeStruct((n, 1, 128), F32),
    )
    out_specs = (
        pl.BlockSpec((t * l, d), lambda i: (i, 0)),
        pl.BlockSpec((t, WINDOW, KV_W), m3),
        pl.BlockSpec((t, WINDOW, KV_W), m3),
        pl.BlockSpec((t, l, MLSTM_W), m3),
        pl.BlockSpec((t, M_HEADS, M_HEAD_DIM, M_HEAD_DIM), m4),
        pl.BlockSpec((t, M_HEADS, M_HEAD_DIM), m3),
        pl.BlockSpec((t, 1, 128), m3),
    )
    in_specs = [
        pl.BlockSpec((t, l, d), m3),
        pl.BlockSpec((t, WINDOW, KV_W), m3),
        pl.BlockSpec((t, WINDOW, KV_W), m3),
        pl.BlockSpec((t, 8, MLSTM_W), m3),
        pl.BlockSpec((t, M_HEADS, M_HEAD_DIM, M_HEAD_DIM), m4),
        pl.BlockSpec((t, M_HEADS, M_HEAD_DIM), m3),
        pl.BlockSpec((t, 1, 128), m3),
    ] + _mixer_weight_specs()
    return pl.pallas_call(
        _sample_kernel,
        grid=(n // t,),
        in_specs=in_specs,
        out_specs=out_specs,
        out_shape=out_shape,
        scratch_shapes=[pltpu.VMEM((t, 16, MLSTM_W), F32)],
        compiler_params=pltpu.CompilerParams(
            dimension_semantics=("arbitrary",), vmem_limit_bytes=VMEM_LIMIT),
        name="sample_mixer",
    )(x_sample, ck, cv, sconv8, s_c, s_n, s_m3, *mixer_w)


def _post(x_rows, y_rows, post_w):
    rows, d = x_rows.shape
    tm = min(POST_ROWS, rows)
    rmap = lambda i: (i, 0)
    in_specs = [
        pl.BlockSpec((tm, d), rmap),
        pl.BlockSpec((tm, d), rmap),
        _const_spec((d, d)),
        _const_spec((1, d)), _const_spec((1, d)),
        _const_spec((d, D_FF)), _const_spec((d, D_FF)), _const_spec((D_FF, d)),
        _const_spec((1, d)), _const_spec((1, d)),
    ]
    return pl.pallas_call(
        _post_kernel,
        grid=(rows // tm,),
        in_specs=in_specs,
        out_specs=pl.BlockSpec((tm, d), rmap),
        out_shape=jax.ShapeDtypeStruct((rows, d), F32),
        compiler_params=pltpu.CompilerParams(
            dimension_semantics=("arbitrary",), vmem_limit_bytes=VMEM_LIMIT),
        name="post_ffn",
    )(x_rows, y_rows, *post_w)


def kernel(x_prompt, x_sample, cache_k, cache_v, state_conv, state_C, state_n, state_m, meta_tokens,
           w_in, w_conv, b_conv, w_mq, w_mk, b_i, b_f, attn_sinks, g_attn, g_mlstm, w_out,
           ln1_g, ln1_b, w_gate, w_up, w_down, ln2_g, ln2_b):
    b, s, d = x_prompt.shape
    n, l, _ = x_sample.shape

    w0 = w_in[0]
    w_in_p = jnp.concatenate(
        [w0[:, :ATT_W][:, _ATT_PERM], w0[:, ATT_W:IN_MAIN], w0[:, IN_MAIN:],
         jnp.zeros((d, IN_PAD - IN_MAIN - IN_GATES), w0.dtype)], axis=1).astype(BF16)
    wqk = jnp.concatenate([w_mq[0], w_mk[0]], axis=-1).astype(BF16)
    gb = jnp.concatenate([b_i[0], b_f[0], jnp.zeros((128 - IN_GATES,), F32)]).reshape(1, 128)
    seg = np.arange(ATT_W) // HEAD_DIM
    bavg = jnp.asarray((seg[:, None] == seg[None, :]) / HEAD_DIM, BF16)
    mixer_w = (w_in_p, w_conv[0], b_conv[0].reshape(1, MLSTM_W), wqk, gb, attn_sinks[0],
               g_attn[0][_ATT_PERM].reshape(1, ATT_W), g_mlstm[0].reshape(1, MLSTM_W), bavg)
    w_out_p = jnp.concatenate([w_out[0][:ATT_W][_ATT_PERM], w_out[0][ATT_W:]], axis=0).astype(BF16)
    post_w = (w_out_p, ln1_g[0].reshape(1, d), ln1_b[0].reshape(1, d),
              w_gate[0].astype(BF16), w_up[0].astype(BF16), w_down[0].astype(BF16),
              ln2_g[0].reshape(1, d), ln2_b[0].reshape(1, d))

    meta_blk = jnp.concatenate([jnp.zeros((META_PAD, d), x_prompt.dtype), meta_tokens.astype(x_prompt.dtype)], axis=0)
    ymix_p, pk, pv, pconv8, p_c, pn8, pm8 = _prompt_mixer(x_prompt, meta_blk, mixer_w)

    sconv8 = jnp.pad(state_conv[0], ((0, 0), (8 - (CONV_W - 1), 0), (0, 0)))
    s_m3 = jnp.pad(state_m[0], ((0, 0), (0, 128 - M_HEADS))).reshape(n, 1, 128)
    ymix_s, sk, sv, scv, s_c, s_n, s_mo = _sample_mixer(
        x_sample, cache_k[0].reshape(n, WINDOW, KV_W), cache_v[0].reshape(n, WINDOW, KV_W),
        sconv8, state_C[0], state_n[0], s_m3, mixer_w)

    y_prompt = _post(x_prompt.reshape(b * s, d), ymix_p.reshape(b * s, d), post_w).reshape(b, s, d)
    y_sample = _post(x_sample.reshape(n * l, d), ymix_s, post_w).reshape(n, l, d)

    kv5 = lambda a: a.reshape(1, a.shape[0], WINDOW, 2, HEAD_DIM)
    return (y_prompt, y_sample,
            kv5(pk), kv5(pv), pconv8[None, :, 8 - (CONV_W - 1):, :], p_c[None],
            pn8[None, :, :M_HEADS, :], pm8[None, :, :M_HEADS, 0],
            kv5(sk), kv5(sv), scv[None, :, l - (CONV_W - 1):, :], s_c[None], s_n[None],
            s_mo[None, :, 0, :M_HEADS])
```

```python
import functools

import numpy as np
import jax
import jax.numpy as jnp
from jax import lax
from jax.experimental import pallas as pl
from jax.experimental.pallas import tpu as pltpu

F32 = jnp.float32
BF16 = jnp.bfloat16

D_MODEL = 1024
ATT_W = 512
MLSTM_W = 512
HEAD_DIM = 64
N_HEADS = 8
KV_W = 128
WINDOW = 128
BLOCK = 128
M_HEADS = 4
M_HEAD_DIM = 128
CONV_W = 4
N_META = 16
META_PAD = BLOCK - N_META
D_FF = 2816
IN_MAIN = 2304
IN_GATES = 2 * M_HEADS
IN_PAD = IN_MAIN + 128
DEPTH = 1
ALIBI_SLOPES = tuple(2.0 ** (-8.0 * (h + 1) / N_HEADS) for h in range(N_HEADS))
DEEPNORM_ALPHA = (2.0 * DEPTH) ** 0.25
EPS = 1e-5
SEQ_TILE = 16
POST_ROWS = 256
FF_CHUNK = 256
PROJ_CHUNK = 256
VMEM_LIMIT = 56 * 1024 * 1024

_ATT_PERM = np.concatenate(
    [np.concatenate([np.arange(r * HEAD_DIM, (r + 1) * HEAD_DIM),
                     np.arange((r + 4) * HEAD_DIM, (r + 5) * HEAD_DIM)]) for r in range(4)])


def _dot(a, b):
    return jnp.dot(a, b, preferred_element_type=F32)


def _dot_nt(a, b):
    return lax.dot_general(a, b, (((1,), (1,)), ((), ())), preferred_element_type=F32)


def _dot_tn(a, b):
    return lax.dot_general(a, b, (((0,), (0,)), ((), ())), preferred_element_type=F32)


def _split2(x):
    hi = x.astype(BF16)
    lo = (x - hi.astype(F32)).astype(BF16)
    return hi, lo


def _sigmoid(x):
    return 1.0 / (1.0 + jnp.exp(-x))


def _log_sigmoid(x):
    return -(jnp.maximum(-x, 0.0) + jnp.log1p(jnp.exp(-jnp.abs(x))))


def _layer_norm(x, g, b):
    mu = jnp.mean(x, axis=-1, keepdims=True)
    xc = x - mu
    var = jnp.mean(xc * xc, axis=-1, keepdims=True)
    return xc * lax.rsqrt(var + EPS) * g + b


def _attn_head_norm(att, bavg, g):
    hi, lo = _split2(att)
    mu = _dot(hi, bavg) + _dot(lo, bavg)
    xc = att - mu
    var = _dot((xc * xc).astype(BF16), bavg)
    return xc * lax.rsqrt(var + EPS) * g


def _mlstm_head_norm(hm, g):
    mu = jnp.mean(hm, axis=-1, keepdims=True)
    xc = hm - mu
    var = jnp.mean(xc * xc, axis=-1, keepdims=True)
    return xc * lax.rsqrt(var + EPS) * g


def _softmax_parts(parts, sink):
    sp = [jnp.where(m, s * (HEAD_DIM ** -0.5) + a, -jnp.inf) for s, m, a in parts]
    mx = sink
    for s in sp:
        mx = jnp.maximum(mx, jnp.max(s, axis=1, keepdims=True))
    ps = [jnp.exp(s - mx) for s in sp]
    den = jnp.exp(sink - mx)
    for p in ps:
        den = den + jnp.sum(p, axis=1, keepdims=True)
    return ps, 1.0 / den


def _mlstm_intra(qm_b, km_b, vh_b, mask, b_c, b_r, li_r, m_prev):
    dmat = jnp.where(mask, b_c - b_r + li_r, -jnp.inf)
    m_inter = b_c + m_prev
    m_t = jnp.maximum(m_inter, jnp.max(dmat, axis=1, keepdims=True))
    w_inter = jnp.exp(m_inter - m_t)
    sc = _dot_nt(qm_b, km_b) * jnp.exp(dmat - m_t)
    num = _dot(sc.astype(BF16), vh_b)
    den = jnp.sum(sc, axis=1, keepdims=True)
    return m_t, w_inter, num, den


def _conv_silu(window, wconv_ref, bconv_ref):
    acc = bconv_ref[...]
    for j in range(CONV_W):
        acc = acc + window(j) * wconv_ref[j:j + 1, :]
    return acc * _sigmoid(acc)


def _gate_columns(gates, gb_ref, row_valid, ltri, extra=None):
    lane = lax.broadcasted_iota(jnp.int32, (BLOCK, 128), 1)
    gb = gates + gb_ref[...]
    is_i = lane < M_HEADS
    is_f = (lane >= M_HEADS) & (lane < 2 * M_HEADS)
    logf = _log_sigmoid(gb)
    if row_valid is not None:
        lf = jnp.where(is_f & row_valid, logf, 0.0)
        li = jnp.where(row_valid, gb, -jnp.inf)
    else:
        lf = jnp.where(is_f, logf, 0.0)
        li = gb
    hi, lo = _split2(lf)
    bc = _dot(ltri, hi) + _dot(ltri, lo)
    x = jnp.where(is_i, li, bc)
    if extra is None:
        return x, None
    return x, _dot(extra, hi) + _dot(extra, lo)


class _FfnStages:
    N_CHUNKS = D_FF // FF_CHUNK

    def __init__(self, x, ymix, wout_ref, ln1g_ref, ln1b_ref, wg_ref, wu_ref, wd_ref, ln2g_ref, ln2b_ref):
        self.x, self.ymix = x, ymix
        self.wout_ref, self.ln1g_ref, self.ln1b_ref = wout_ref, ln1g_ref, ln1b_ref
        self.wg_ref, self.wu_ref, self.wd_ref = wg_ref, wu_ref, wd_ref
        self.ln2g_ref, self.ln2b_ref = ln2g_ref, ln2b_ref

    def head(self):
        self.x1 = _layer_norm(DEEPNORM_ALPHA * self.x + _dot(self.ymix, self.wout_ref[...]),
                              self.ln1g_ref[...], self.ln1b_ref[...])
        self.x1b = self.x1.astype(BF16)
        self.acc = DEEPNORM_ALPHA * self.x1

    def _gate_up(self, c):
        cs = slice(c * FF_CHUNK, (c + 1) * FF_CHUNK)
        return _dot(self.x1b, self.wg_ref[:, cs]), _dot(self.x1b, self.wu_ref[:, cs])

    def chunk(self, c):
        if c == 0:
            self.gu = self._gate_up(0)
        g, u = self.gu
        if c + 1 < self.N_CHUNKS:
            self.gu = self._gate_up(c + 1)
        hid = (g * _sigmoid(g) * u).astype(BF16)
        self.acc = self.acc + _dot(hid, self.wd_ref[c * FF_CHUNK:(c + 1) * FF_CHUNK, :])

    def tail(self):
        return _layer_norm(self.acc, self.ln2g_ref[...], self.ln2b_ref[...])

    def run(self):
        self.head()
        for c in range(self.N_CHUNKS):
            self.chunk(c)
        return self.tail()


def _prompt_kernel(xin_ref, xres_ref, meta_ref, w_in_ref, wconv_ref, bconv_ref, wq_ref, wkt_ref, gb_ref,
                   sinks_ref, gattn_ref, gml_ref, bavg_ref,
                   wout_ref, ln1g_ref, ln1b_ref, wg_ref, wu_ref, wd_ref, ln2g_ref, ln2b_ref,
                   y_ref, pk_ref, pv_ref, pconv_ref, pct_ref, pm_ref,
                   zcur, znext, kvlast, kplo, kphi, vplo, vphi, cbuf, ctn, mst, yprev, *, nblk):
    t = pl.program_id(0)
    j_in = lax.rem(t, nblk)
    j = lax.rem(t + (nblk - 1), nblk)

    @pl.when(t == 0)
    def _init_pipeline():
        zcur[...] = jnp.zeros_like(zcur)
        yprev[...] = jnp.zeros_like(yprev)

    @pl.when((j == 0) | (t == 0))
    def _init_state():
        zb = jnp.zeros((BLOCK, KV_W), BF16)
        kplo[...] = zb
        kphi[...] = zb
        vplo[...] = zb
        vphi[...] = zb
        cbuf[0:8, :] = jnp.zeros((8, MLSTM_W), F32)
        ctn[...] = jnp.zeros_like(ctn)
        mst[...] = jnp.zeros_like(mst)

    ffn = _FfnStages(xres_ref[...], yprev[...], wout_ref, ln1g_ref, ln1b_ref, wg_ref, wu_ref, wd_ref,
                     ln2g_ref, ln2b_ref)
    xb_b = jnp.where(j_in == 0, meta_ref[...], xin_ref[...]).astype(BF16)

    def proj(n):
        cs = slice(n * PROJ_CHUNK, min((n + 1) * PROJ_CHUNK, IN_PAD))
        znext[:, cs] = _dot(xb_b, w_in_ref[:, cs])

    n_proj = -(-IN_PAD // PROJ_CHUNK)
    proj_it = iter(range(n_proj))
    ffn_it = iter(range(ffn.N_CHUNKS))

    def fill_proj(n=1):
        for c in [c for _, c in zip(range(n), proj_it)]:
            proj(c)

    def fill_ffn(n=1):
        for c in [c for _, c in zip(range(n), ffn_it)]:
            ffn.chunk(c)

    row = lax.broadcasted_iota(jnp.int32, (BLOCK, 1), 0)
    row_valid = (row + j * BLOCK) >= META_PAD
    lane = lax.broadcasted_iota(jnp.int32, (BLOCK, 128), 1)
    lo_half = lane < HEAD_DIM

    kf = zcur[:, ATT_W:ATT_W + KV_W]
    vf = zcur[:, ATT_W + KV_W:ATT_W + 2 * KV_W]
    kb, vb = kf.astype(BF16), vf.astype(BF16)
    zero_b = jnp.zeros_like(kb)
    k_lo, k_hi = jnp.where(lo_half, kb, zero_b), jnp.where(lo_half, zero_b, kb)
    v_lo, v_hi = jnp.where(lo_half, vb, zero_b), jnp.where(lo_half, zero_b, vb)
    kmask = jnp.concatenate([kplo[...], k_lo, kphi[...], k_hi], axis=0)
    vmask = jnp.concatenate([vplo[...], v_lo, vphi[...], v_hi], axis=0)
    q4 = jnp.concatenate([zcur[:, r * 128:(r + 1) * 128] for r in range(4)], axis=0).astype(BF16)
    s_all = _dot_nt(q4, kmask)
    kvlast[...] = zcur[:, ATT_W:ATT_W + 2 * KV_W]
    fill_proj(2)
    ffn.head()

    c_in = jnp.where(row_valid, zcur[:, 768:1280], 0.0)
    cbuf[8:8 + BLOCK, :] = c_in
    c_act = _conv_silu(lambda jj: cbuf[5 + jj:5 + jj + BLOCK, :], wconv_ref, bconv_ref)
    cbuf[0:8, :] = cbuf[BLOCK:BLOCK + 8, :]
    ca_b = [c_act[:, h * 128:(h + 1) * 128].astype(BF16) for h in range(M_HEADS)]
    qm_b = [_dot(ca_b[h], wq_ref[h]).astype(BF16) for h in range(M_HEADS)]
    kmt = [_dot_nt(wkt_ref[h], ca_b[h]) * (M_HEAD_DIM ** -0.5) for h in range(M_HEADS)]
    fill_proj(1)

    ti = lax.broadcasted_iota(jnp.int32, (BLOCK, BLOCK), 0)
    si = lax.broadcasted_iota(jnp.int32, (BLOCK, BLOCK), 1)
    tri = si <= ti
    ltri = jnp.where(tri, 1.0, 0.0).astype(BF16)
    xg, _ = _gate_columns(zcur[:, IN_MAIN:IN_PAD], gb_ref, row_valid, ltri)
    xgt = xg.T
    fill_ffn(1)

    qi = lax.broadcasted_iota(jnp.int32, (BLOCK, 2 * BLOCK), 0)
    ks = lax.broadcasted_iota(jnp.int32, (BLOCK, 2 * BLOCK), 1)
    dist = WINDOW + qi - ks
    key_pos = (j - 1) * BLOCK + ks
    att_mask = (dist >= 0) & (dist < WINDOW) & (key_pos >= META_PAD)
    distf = dist.astype(F32)
    p_rows, inv = [], {}
    for r in range(4):
        pr = []
        for c in range(2):
            h = r + 4 * c
            s_blk = s_all[r * 128:(r + 1) * 128, c * 256:(c + 1) * 256]
            (p,), inv[h] = _softmax_parts([(s_blk, att_mask, -ALIBI_SLOPES[h] * distf)], sinks_ref[h])
            pr.append(p.astype(BF16))
        p_rows.append(jnp.concatenate(pr, axis=1))
        fill_ffn(2)
    o_all = _dot(jnp.concatenate(p_rows, axis=0), vmask)
    kplo[...] = k_lo
    kphi[...] = k_hi
    vplo[...] = v_lo
    vphi[...] = v_hi

    ones_col = jnp.where(lane == 0, 1.0, 0.0).astype(BF16)
    hs = range(M_HEADS)
    b_c = [xg[:, 4 + h:5 + h] for h in hs]
    b_r = [xgt[4 + h:5 + h, :] for h in hs]
    li_r = [xgt[h:h + 1, :] for h in hs]
    m_prev = [mst[h:h + 1, 0:1] for h in hs]
    c_old = [ctn[h] for h in hs]
    v_ext = [jnp.concatenate([zcur[:, 1280 + h * 128:1280 + (h + 1) * 128].astype(BF16), ones_col], axis=1)
             for h in hs]
    s_qk = [_dot(qm_b[h], kmt[h].astype(BF16)) for h in hs]
    inter = [_dot(qm_b[h], c_old[h].astype(BF16)) for h in hs]
    fill_ffn(1)
    dmat = [jnp.where(tri, b_c[h] - b_r[h] + li_r[h], -jnp.inf) for h in hs]
    m_inter = [b_c[h] + m_prev[h] for h in hs]
    m_t = [jnp.maximum(m_inter[h], jnp.max(dmat[h], axis=1, keepdims=True)) for h in hs]
    w_inter = [jnp.exp(m_inter[h] - m_t[h]) for h in hs]
    sc_b = [(s_qk[h] * jnp.exp(dmat[h] - m_t[h])).astype(BF16) for h in hs]
    fill_ffn(ffn.N_CHUNKS)
    nd = [_dot(sc_b[h], v_ext[h]) + w_inter[h] * inter[h] for h in hs]

    att = jnp.concatenate(
        [o_all[r * 128:(r + 1) * 128] * jnp.where(lo_half, inv[r], inv[r + 4]) for r in range(4)], axis=1)
    y_att = _attn_head_norm(att, bavg_ref[...], gattn_ref[...])
    y_ref[...] = ffn.tail()
    fill_proj(1)

    m_end = [m_t[h][BLOCK - 1:BLOCK, :] for h in hs]
    b_last = [b_c[h][BLOCK - 1:BLOCK, :] for h in hs]
    for h in hs:
        decay = jnp.exp(b_last[h] + m_prev[h] - m_end[h])
        wk_r = jnp.exp(b_last[h] - b_r[h] + li_r[h] - m_end[h])
        ctn[h] = decay * c_old[h] + _dot((kmt[h] * wk_r).astype(BF16), v_ext[h])
        mst[h:h + 1, :] = jnp.broadcast_to(m_end[h], (1, 128))
    fill_proj(2)

    y_m = []
    for h in hs:
        hh = nd[h][:, :128] / jnp.maximum(jnp.abs(nd[h][:, 128:129]), jnp.exp(-m_t[h]))
        hm = _sigmoid(zcur[:, 1792 + h * 128:1792 + (h + 1) * 128]) * hh
        y_m.append(_mlstm_head_norm(hm, gml_ref[:, h * 128:(h + 1) * 128]))
        fill_proj(1)

    fill_proj(n_proj)
    yprev[...] = jnp.concatenate([y_att] + y_m, axis=1).astype(BF16)
    zcur[...] = znext[...]

    @pl.when(j == nblk - 1)
    def _final():
        pk_ref[...] = kvlast[:, 0:KV_W]
        pv_ref[...] = kvlast[:, KV_W:2 * KV_W]
        pconv_ref[...] = cbuf[0:8, :]
        pm_ref[...] = mst[...]
        for h in range(M_HEADS):
            pct_ref[h] = ctn[h].T


def _sample_kernel(xs_ref, ck_ref, cv_ref, sconv_ref, sc_ref, sn_ref, sm_ref,
                   w_in_ref, wconv_ref, bconv_ref, wqk_ref, gb_ref, sinks_ref,
                   gattn_ref, gml_ref, bavg_ref,
                   ymix_ref, sk_ref, sv_ref, scv_ref, sco_ref, sno_ref, smo_ref, cext):
    ns, L = SEQ_TILE, 8
    xs = xs_ref[...].reshape(ns * L, D_MODEL)
    z = _dot(xs.astype(BF16), w_in_ref[...])

    lane = lax.broadcasted_iota(jnp.int32, (BLOCK, 128), 1)
    lo_half = lane < HEAD_DIM
    lo3 = lax.broadcasted_iota(jnp.int32, (ns, L, 128), 2) < HEAD_DIM
    ti = lax.broadcasted_iota(jnp.int32, (BLOCK, BLOCK), 0)
    si = lax.broadcasted_iota(jnp.int32, (BLOCK, BLOCK), 1)
    same_seq = (ti >> 3) == (si >> 3)
    tl, sl_ = ti & 7, si & 7
    causal_seq = same_seq & (sl_ <= tl)

    k_new = z[:, ATT_W:ATT_W + KV_W]
    v_new = z[:, ATT_W + KV_W:ATT_W + 2 * KV_W]
    kb, vb = k_new.astype(BF16), v_new.astype(BF16)
    zero_b = jnp.zeros_like(kb)
    kn_mask = jnp.concatenate([jnp.where(lo_half, kb, zero_b), jnp.where(lo_half, zero_b, kb)], axis=0)
    vn_mask = jnp.concatenate([jnp.where(lo_half, vb, zero_b), jnp.where(lo_half, zero_b, vb)], axis=0)
    qg = [z[:, r * 128:(r + 1) * 128] for r in range(4)]
    s_new = _dot_nt(jnp.concatenate(qg, axis=0).astype(BF16), kn_mask)
    qg3 = [q.reshape(ns, L, 128) for q in qg]
    q8 = jnp.concatenate([jnp.where(lo3, q, 0.0) for q in qg3] + [jnp.where(lo3, 0.0, q) for q in qg3],
                         axis=1).astype(BF16)
    s_cache = jnp.einsum('nqd,nkd->nqk', q8, ck_ref[...].astype(BF16), preferred_element_type=F32)

    mask_c = si > tl
    dist_c = (WINDOW + tl - si).astype(F32)
    dist_n = (tl - sl_).astype(F32)
    pc_parts, pn_rows, inv = {}, [], {}
    for r in range(4):
        pn = []
        for c in range(2):
            h = r + 4 * c
            sc_c = s_cache[:, c * 32 + r * L:c * 32 + (r + 1) * L, :].reshape(BLOCK, 128)
            sc_n = s_new[r * 128:(r + 1) * 128, c * 128:(c + 1) * 128]
            (p_c, p_n), inv[h] = _softmax_parts(
                [(sc_c, mask_c, -ALIBI_SLOPES[h] * dist_c), (sc_n, causal_seq, -ALIBI_SLOPES[h] * dist_n)],
                sinks_ref[h])
            pc_parts[(c, r)] = p_c.reshape(ns, L, 128)
            pn.append(p_n.astype(BF16))
        pn_rows.append(jnp.concatenate(pn, axis=1))
    p8 = jnp.concatenate([pc_parts[(c, r)] for c in range(2) for r in range(4)], axis=1).astype(BF16)
    o_cache = jnp.einsum('nqk,nkd->nqd', p8, cv_ref[...].astype(BF16), preferred_element_type=F32)
    o_new = _dot(jnp.concatenate(pn_rows, axis=0), vn_mask)
    groups = []
    for r in range(4):
        oa = o_cache[:, r * L:(r + 1) * L, :].reshape(BLOCK, 128)
        ob = o_cache[:, 32 + r * L:32 + (r + 1) * L, :].reshape(BLOCK, 128)
        o = jnp.where(lo_half, oa, ob) + o_new[r * 128:(r + 1) * 128]
        groups.append(o * jnp.where(lo_half, inv[r], inv[r + 4]))
    y_att = _attn_head_norm(jnp.concatenate(groups, axis=1), bavg_ref[...], gattn_ref[...])

    sk_ref[:, 0:WINDOW - L, :] = ck_ref[:, L:WINDOW, :]
    sk_ref[:, WINDOW - L:WINDOW, :] = k_new.reshape(ns, L, 128)
    sv_ref[:, 0:WINDOW - L, :] = cv_ref[:, L:WINDOW, :]
    sv_ref[:, WINDOW - L:WINDOW, :] = v_new.reshape(ns, L, 128)

    c3 = z[:, 768:1280].reshape(ns, L, MLSTM_W)
    cext[:, 0:8, :] = sconv_ref[...]
    cext[:, 8:16, :] = c3
    scv_ref[...] = c3
    c_act = _conv_silu(lambda jj: cext[:, 5 + jj:5 + jj + L, :], wconv_ref, bconv_ref)
    c_act = c_act.reshape(BLOCK, MLSTM_W)

    lblk = jnp.where(causal_seq, 1.0, 0.0).astype(BF16)
    lseq = jnp.where(same_seq, 1.0, 0.0).astype(BF16)
    xg, blast = _gate_columns(z[:, IN_MAIN:IN_PAD], gb_ref, None, lblk, extra=lseq)
    xgt = xg.T
    m_rep = jnp.broadcast_to(sm_ref[...], (ns, L, 128)).reshape(BLOCK, 128)
    lane3 = lax.broadcasted_iota(jnp.int32, (ns, 1, 128), 2)
    m_out = jnp.zeros((ns, 1, 128), F32)

    y_m = []
    for h in range(M_HEADS):
        sl = slice(h * 128, (h + 1) * 128)
        qk = _dot(c_act[:, sl].astype(BF16), wqk_ref[h])
        qm = qk[:, :128]
        qm_b = qm.astype(BF16)
        km = qk[:, 128:] * (M_HEAD_DIM ** -0.5)
        km_b = km.astype(BF16)
        vh = z[:, 1280 + h * 128:1280 + (h + 1) * 128]
        b_c, li_c = xg[:, 4 + h:5 + h], xg[:, h:h + 1]
        b_r, li_r = xgt[4 + h:5 + h, :], xgt[h:h + 1, :]
        bl_c = blast[:, 4 + h:5 + h]
        m_prev = m_rep[:, h:h + 1]
        m_t, w_inter, num, den = _mlstm_intra(qm_b, km_b, vh.astype(BF16), causal_seq, b_c, b_r, li_r, m_prev)

        c_old = sc_ref[:, h]
        cq = _dot_nt(qm_b, c_old.reshape(ns * 128, 128).astype(BF16))
        num_inter = jnp.concatenate([cq[g * L:(g + 1) * L, g * 128:(g + 1) * 128] for g in range(ns)], axis=0)
        n_old = sn_ref[:, h:h + 1, :]
        n_rep = jnp.broadcast_to(n_old, (ns, L, 128)).reshape(BLOCK, 128)
        num = num + w_inter * num_inter
        den = den + w_inter * jnp.sum(qm * n_rep, axis=1, keepdims=True)
        hh = num / jnp.maximum(jnp.abs(den), jnp.exp(-m_t))

        m_end3 = jnp.broadcast_to(m_t, (BLOCK, 128)).reshape(ns, L, 128)[:, L - 1:L, :]
        m_end = jnp.broadcast_to(m_end3, (ns, L, 128)).reshape(BLOCK, 128)[:, 0:1]
        decay = jnp.exp(bl_c + m_prev - m_end)
        wk = jnp.exp(bl_c - b_c + li_c - m_end)
        k_big = jnp.concatenate(
            [jnp.where((ti >> 3) == g, km_b, jnp.zeros_like(km_b)) for g in range(ns)], axis=1)
        c_up = _dot_tn((vh * wk).astype(BF16), k_big)
        c_up3 = jnp.stack([c_up[:, g * 128:(g + 1) * 128] for g in range(ns)], axis=0)
        decay3 = jnp.broadcast_to(decay, (BLOCK, 128)).reshape(ns, L, 128)[:, 0:1, :]
        sco_ref[:, h] = decay3 * c_old + c_up3
        sno_ref[:, h:h + 1, :] = decay3 * n_old + jnp.sum((km * wk).reshape(ns, L, 128), axis=1, keepdims=True)
        m_out = jnp.where(lane3 == h, m_end3, m_out)

        hm = _sigmoid(z[:, 1792 + h * 128:1792 + (h + 1) * 128]) * hh
        y_m.append(_mlstm_head_norm(hm, gml_ref[:, sl]))

    smo_ref[...] = m_out
    ymix_ref[...] = jnp.concatenate([y_att] + y_m, axis=1).astype(BF16)


def _post_kernel(x_ref, y_ref, wout_ref, ln1g_ref, ln1b_ref, wg_ref, wu_ref, wd_ref, ln2g_ref, ln2b_ref, o_ref):
    o_ref[...] = _FfnStages(x_ref[...], y_ref[...], wout_ref, ln1g_ref, ln1b_ref, wg_ref, wu_ref, wd_ref,
                            ln2g_ref, ln2b_ref).run()


def _const_spec(shape):
    nd = len(shape)
    return pl.BlockSpec(shape, lambda *_: (0,) * nd, pipeline_mode=pl.Buffered(1))


def _mixer_weight_specs(qk_specs):
    return [
        _const_spec((D_MODEL, IN_PAD)),
        _const_spec((CONV_W, MLSTM_W)),
        _const_spec((1, MLSTM_W)),
    ] + qk_specs + [
        _const_spec((1, 128)),
        pl.BlockSpec(memory_space=pltpu.SMEM),
        _const_spec((1, ATT_W)),
        _const_spec((1, MLSTM_W)),
        _const_spec((ATT_W, ATT_W)),
    ]


def _post_weight_specs(d):
    return [
        _const_spec((d, d)),
        _const_spec((1, d)), _const_spec((1, d)),
        _const_spec((d, D_FF)), _const_spec((d, D_FF)), _const_spec((D_FF, d)),
        _const_spec((1, d)), _const_spec((1, d)),
    ]


def _prompt_layer(x_prompt, meta_blk, mixer_w, post_w):
    b, s, d = x_prompt.shape
    nblk = s // BLOCK + 1
    steps = b * nblk + 2

    def block_map(lag):
        def index(t):
            tb = jnp.maximum(t - lag, 0)
            return (jnp.minimum(tb // nblk, b - 1), jnp.maximum(tb % nblk - 1, 0), 0)
        return index

    def smap(t):
        return (jnp.minimum(jnp.maximum(t - 1, 0) // nblk, b - 1), 0, 0)

    out_shape = (
        jax.ShapeDtypeStruct((b, s, d), F32),
        jax.ShapeDtypeStruct((b, BLOCK, KV_W), F32),
        jax.ShapeDtypeStruct((b, BLOCK, KV_W), F32),
        jax.ShapeDtypeStruct((b, 8, MLSTM_W), F32),
        jax.ShapeDtypeStruct((b, M_HEADS, 2 * M_HEAD_DIM, M_HEAD_DIM), F32),
        jax.ShapeDtypeStruct((b, 8, 128), F32),
    )
    out_specs = (
        pl.BlockSpec((None, BLOCK, d), block_map(2)),
        pl.BlockSpec((None, BLOCK, KV_W), smap),
        pl.BlockSpec((None, BLOCK, KV_W), smap),
        pl.BlockSpec((None, 8, MLSTM_W), smap),
        pl.BlockSpec((None, M_HEADS, 2 * M_HEAD_DIM, M_HEAD_DIM), lambda t: smap(t) + (0,)),
        pl.BlockSpec((None, 8, 128), smap),
    )
    qk_specs = [_const_spec((M_HEADS, M_HEAD_DIM, M_HEAD_DIM))] * 2
    return pl.pallas_call(
        functools.partial(_prompt_kernel, nblk=nblk),
        grid=(steps,),
        in_specs=[pl.BlockSpec((None, BLOCK, d), block_map(0)), pl.BlockSpec((None, BLOCK, d), block_map(2)),
                  _const_spec((BLOCK, d))]
        + _mixer_weight_specs(qk_specs) + _post_weight_specs(d),
        out_specs=out_specs,
        out_shape=out_shape,
        scratch_shapes=[
            pltpu.VMEM((BLOCK, IN_PAD), F32),
            pltpu.VMEM((BLOCK, IN_PAD), F32),
            pltpu.VMEM((BLOCK, 2 * KV_W), F32),
        ] + [pltpu.VMEM((BLOCK, KV_W), BF16)] * 4 + [
            pltpu.VMEM((BLOCK + 8, MLSTM_W), F32),
            pltpu.VMEM((M_HEADS, M_HEAD_DIM, 2 * M_HEAD_DIM), F32),
            pltpu.VMEM((8, 128), F32),
            pltpu.VMEM((BLOCK, d), BF16),
        ],
        compiler_params=pltpu.CompilerParams(
            dimension_semantics=("arbitrary",), vmem_limit_bytes=VMEM_LIMIT),
        name="prompt_layer",
    )(x_prompt, x_prompt, meta_blk, *mixer_w, *post_w)


def _sample_mixer(x_sample, ck, cv, sconv8, s_c, s_n, s_m3, mixer_w):
    n, l, d = x_sample.shape
    t = SEQ_TILE
    m3 = lambda i: (i, 0, 0)
    m4 = lambda i: (i, 0, 0, 0)
    out_shape = (
        jax.ShapeDtypeStruct((n * l, d), BF16),
        jax.ShapeDtypeStruct((n, WINDOW, KV_W), F32),
        jax.ShapeDtypeStruct((n, WINDOW, KV_W), F32),
        jax.ShapeDtypeStruct((n, l, MLSTM_W), F32),
        jax.ShapeDtypeStruct((n, M_HEADS, M_HEAD_DIM, M_HEAD_DIM), F32),
        jax.ShapeDtypeStruct((n, M_HEADS, M_HEAD_DIM), F32),
        jax.ShapeDtypeStruct((n, 1, 128), F32),
    )
    out_specs = (
        pl.BlockSpec((t * l, d), lambda i: (i, 0)),
        pl.BlockSpec((t, WINDOW, KV_W), m3),
        pl.BlockSpec((t, WINDOW, KV_W), m3),
        pl.BlockSpec((t, l, MLSTM_W), m3),
        pl.BlockSpec((t, M_HEADS, M_HEAD_DIM, M_HEAD_DIM), m4),
        pl.BlockSpec((t, M_HEADS, M_HEAD_DIM), m3),
        pl.BlockSpec((t, 1, 128), m3),
    )
    in_specs = [
        pl.BlockSpec((t, l, d), m3),
        pl.BlockSpec((t, WINDOW, KV_W), m3),
        pl.BlockSpec((t, WINDOW, KV_W), m3),
        pl.BlockSpec((t, 8, MLSTM_W), m3),
        pl.BlockSpec((t, M_HEADS, M_HEAD_DIM, M_HEAD_DIM), m4),
        pl.BlockSpec((t, M_HEADS, M_HEAD_DIM), m3),
        pl.BlockSpec((t, 1, 128), m3),
    ] + _mixer_weight_specs([_const_spec((M_HEADS, M_HEAD_DIM, 2 * M_HEAD_DIM))])
    return pl.pallas_call(
        _sample_kernel,
        grid=(n // t,),
        in_specs=in_specs,
        out_specs=out_specs,
        out_shape=out_shape,
        scratch_shapes=[pltpu.VMEM((t, 16, MLSTM_W), F32)],
        compiler_params=pltpu.CompilerParams(
            dimension_semantics=("arbitrary",), vmem_limit_bytes=VMEM_LIMIT),
        name="sample_mixer",
    )(x_sample, ck, cv, sconv8, s_c, s_n, s_m3, *mixer_w)


def _post(x_rows, y_rows, post_w):
    rows, d = x_rows.shape
    tm = min(POST_ROWS, rows)
    rmap = lambda i: (i, 0)
    in_specs = [pl.BlockSpec((tm, d), rmap), pl.BlockSpec((tm, d), rmap)] + _post_weight_specs(d)
    return pl.pallas_call(
        _post_kernel,
        grid=(rows // tm,),
        in_specs=in_specs,
        out_specs=pl.BlockSpec((tm, d), rmap),
        out_shape=jax.ShapeDtypeStruct((rows, d), F32),
        compiler_params=pltpu.CompilerParams(
            dimension_semantics=("arbitrary",), vmem_limit_bytes=VMEM_LIMIT),
        name="post_ffn",
    )(x_rows, y_rows, *post_w)


def kernel(x_prompt, x_sample, cache_k, cache_v, state_conv, state_C, state_n, state_m, meta_tokens,
           w_in, w_conv, b_conv, w_mq, w_mk, b_i, b_f, attn_sinks, g_attn, g_mlstm, w_out,
           ln1_g, ln1_b, w_gate, w_up, w_down, ln2_g, ln2_b):
    b, s, d = x_prompt.shape
    n, l, _ = x_sample.shape

    w0 = w_in[0]
    w_in_p = jnp.concatenate(
        [w0[:, :ATT_W][:, _ATT_PERM], w0[:, ATT_W:IN_MAIN], w0[:, IN_MAIN:],
         jnp.zeros((d, IN_PAD - IN_MAIN - IN_GATES), w0.dtype)], axis=1).astype(BF16)
    wqk = jnp.concatenate([w_mq[0], w_mk[0]], axis=-1).astype(BF16)
    gb = jnp.concatenate([b_i[0], b_f[0], jnp.zeros((128 - IN_GATES,), F32)]).reshape(1, 128)
    seg = np.arange(ATT_W) // HEAD_DIM
    bavg = jnp.asarray((seg[:, None] == seg[None, :]) / HEAD_DIM, BF16)
    mixer_head = (w_in_p, w_conv[0], b_conv[0].reshape(1, MLSTM_W))
    mixer_tail = (gb, attn_sinks[0], g_attn[0][_ATT_PERM].reshape(1, ATT_W), g_mlstm[0].reshape(1, MLSTM_W), bavg)
    wq_b = w_mq[0].astype(BF16)
    wkt_b = jnp.swapaxes(w_mk[0], 1, 2).astype(BF16)
    w_out_p = jnp.concatenate([w_out[0][:ATT_W][_ATT_PERM], w_out[0][ATT_W:]], axis=0).astype(BF16)
    post_w = (w_out_p, ln1_g[0].reshape(1, d), ln1_b[0].reshape(1, d),
              w_gate[0].astype(BF16), w_up[0].astype(BF16), w_down[0].astype(BF16),
              ln2_g[0].reshape(1, d), ln2_b[0].reshape(1, d))

    meta_blk = jnp.concatenate([jnp.zeros((META_PAD, d), x_prompt.dtype), meta_tokens.astype(x_prompt.dtype)], axis=0)
    y_prompt, pk, pv, pconv8, pct, pm8 = _prompt_layer(
        x_prompt, meta_blk, mixer_head + (wq_b, wkt_b) + mixer_tail, post_w)

    mixer_w = mixer_head + (wqk,) + mixer_tail
    sconv8 = jnp.pad(state_conv[0], ((0, 0), (8 - (CONV_W - 1), 0), (0, 0)))
    s_m3 = jnp.pad(state_m[0], ((0, 0), (0, 128 - M_HEADS))).reshape(n, 1, 128)
    ymix_s, sk, sv, scv, s_c, s_n, s_mo = _sample_mixer(
        x_sample, cache_k[0].reshape(n, WINDOW, KV_W), cache_v[0].reshape(n, WINDOW, KV_W),
        sconv8, state_C[0], state_n[0], s_m3, mixer_w)

    y_sample = _post(x_sample.reshape(n * l, d), ymix_s, post_w).reshape(n, l, d)

    kv5 = lambda a: a.reshape(1, a.shape[0], WINDOW, 2, HEAD_DIM)
    return (y_prompt, y_sample,
            kv5(pk), kv5(pv), pconv8[None, :, 8 - (CONV_W - 1):, :], pct[None, :, :, :M_HEAD_DIM, :],
            pct[None, :, :, M_HEAD_DIM, :], pm8[None, :, :M_HEADS, 0],
            kv5(sk), kv5(sv), scv[None, :, l - (CONV_W - 1):, :], s_c[None], s_n[None],
            s_mo[None, :, 0, :M_HEADS])
```

```python
import functools

import numpy as np
import jax
import jax.numpy as jnp
from jax import lax
from jax.experimental import pallas as pl
from jax.experimental.pallas import tpu as pltpu

F32 = jnp.float32
BF16 = jnp.bfloat16

D_MODEL = 1024
ATT_W = 512
MLSTM_W = 512
HEAD_DIM = 64
N_HEADS = 8
KV_W = 128
WINDOW = 128
BLOCK = 128
M_HEADS = 4
M_HEAD_DIM = 128
CONV_W = 4
N_META = 16
META_PAD = BLOCK - N_META
D_FF = 2816
IN_MAIN = 2304
IN_GATES = 2 * M_HEADS
IN_PAD = IN_MAIN + 128
DEPTH = 1
ALIBI_SLOPES = tuple(2.0 ** (-8.0 * (h + 1) / N_HEADS) for h in range(N_HEADS))
DEEPNORM_ALPHA = (2.0 * DEPTH) ** 0.25
EPS = 1e-5
SEQ_TILE = 16
POST_ROWS = 256
FF_CHUNK = 256
PROJ_CHUNK = 256
VMEM_LIMIT = 56 * 1024 * 1024

_ATT_PERM = np.concatenate(
    [np.concatenate([np.arange(r * HEAD_DIM, (r + 1) * HEAD_DIM),
                     np.arange((r + 4) * HEAD_DIM, (r + 5) * HEAD_DIM)]) for r in range(4)])
_IN_COLS = np.concatenate([_ATT_PERM, np.arange(ATT_W, IN_MAIN + IN_GATES),
                           np.zeros(IN_PAD - IN_MAIN - IN_GATES, np.int64)])


def _dot(a, b):
    return jnp.dot(a, b, preferred_element_type=F32)


def _dot_nt(a, b):
    return lax.dot_general(a, b, (((1,), (1,)), ((), ())), preferred_element_type=F32)


def _dot_tn(a, b):
    return lax.dot_general(a, b, (((0,), (0,)), ((), ())), preferred_element_type=F32)


def _split2(x):
    hi = x.astype(BF16)
    lo = (x - hi.astype(F32)).astype(BF16)
    return hi, lo


def _sigmoid(x):
    return 1.0 / (1.0 + jnp.exp(-x))


def _log_sigmoid(x):
    return -(jnp.maximum(-x, 0.0) + jnp.log1p(jnp.exp(-jnp.abs(x))))


def _layer_norm(x, g, b):
    mu = jnp.mean(x, axis=-1, keepdims=True)
    xc = x - mu
    var = jnp.mean(xc * xc, axis=-1, keepdims=True)
    return xc * lax.rsqrt(var + EPS) * g + b


def _attn_head_norm(att, g):
    lo_half = lax.broadcasted_iota(jnp.int32, (att.shape[0], 128), 1) < HEAD_DIM

    def seg_mean(x):
        s_all = jnp.sum(x, axis=1, keepdims=True)
        s_lo = jnp.sum(jnp.where(lo_half, x, 0.0), axis=1, keepdims=True)
        return jnp.where(lo_half, s_lo, s_all - s_lo) * (1.0 / HEAD_DIM)

    out = []
    for grp in range(att.shape[1] // 128):
        x = att[:, grp * 128:(grp + 1) * 128]
        xc = x - seg_mean(x)
        out.append(xc * lax.rsqrt(seg_mean(xc * xc) + EPS))
    return jnp.concatenate(out, axis=1) * g


def _mlstm_head_norm(hm, g):
    mu = jnp.mean(hm, axis=-1, keepdims=True)
    xc = hm - mu
    var = jnp.mean(xc * xc, axis=-1, keepdims=True)
    return xc * lax.rsqrt(var + EPS) * g


def _softmax_parts(parts, sink):
    sp = [jnp.where(m, s * (HEAD_DIM ** -0.5) + a, -jnp.inf) for s, m, a in parts]
    mx = sink
    for s in sp:
        mx = jnp.maximum(mx, jnp.max(s, axis=1, keepdims=True))
    ps = [jnp.exp(s - mx) for s in sp]
    den = jnp.exp(sink - mx)
    for p in ps:
        den = den + jnp.sum(p, axis=1, keepdims=True)
    return ps, 1.0 / den


def _mlstm_intra(qm_b, km_b, vh_b, mask, b_c, b_r, li_r, m_prev):
    dmat = jnp.where(mask, b_c - b_r + li_r, -jnp.inf)
    m_inter = b_c + m_prev
    m_t = jnp.maximum(m_inter, jnp.max(dmat, axis=1, keepdims=True))
    w_inter = jnp.exp(m_inter - m_t)
    sc = _dot_nt(qm_b, km_b) * jnp.exp(dmat - m_t)
    num = _dot(sc.astype(BF16), vh_b)
    den = jnp.sum(sc, axis=1, keepdims=True)
    return m_t, w_inter, num, den


def _conv_silu(window, wconv_ref, bconv_ref):
    acc = bconv_ref[...]
    for j in range(CONV_W):
        acc = acc + window(j) * wconv_ref[j:j + 1, :]
    return acc * _sigmoid(acc)


def _gate_columns(gates, gb_ref, row_valid, ltri, extra=None):
    lane = lax.broadcasted_iota(jnp.int32, gates.shape, 1)
    gb = gates + gb_ref[...]
    is_i = lane < M_HEADS
    is_f = (lane >= M_HEADS) & (lane < 2 * M_HEADS)
    logf = _log_sigmoid(gb)
    if row_valid is not None:
        lf = jnp.where(is_f & row_valid, logf, 0.0)
        li = jnp.where(row_valid, gb, -jnp.inf)
    else:
        lf = jnp.where(is_f, logf, 0.0)
        li = gb
    hi, lo = _split2(lf)
    bc = _dot(ltri, hi) + _dot(ltri, lo)
    x = jnp.where(is_i, li, bc)
    if extra is None:
        return x, None
    return x, _dot(extra, hi) + _dot(extra, lo)


class _FfnStages:
    N_CHUNKS = D_FF // FF_CHUNK

    def __init__(self, x, ymix, wout_ref, ln1g_ref, ln1b_ref, wg_ref, wu_ref, wd_ref, ln2g_ref, ln2b_ref):
        self.x, self.ymix = x, ymix
        self.wout_ref, self.ln1g_ref, self.ln1b_ref = wout_ref, ln1g_ref, ln1b_ref
        self.wg_ref, self.wu_ref, self.wd_ref = wg_ref, wu_ref, wd_ref
        self.ln2g_ref, self.ln2b_ref = ln2g_ref, ln2b_ref

    def head(self):
        self.x1 = _layer_norm(DEEPNORM_ALPHA * self.x + _dot(self.ymix, self.wout_ref[...]),
                              self.ln1g_ref[...], self.ln1b_ref[...])
        self.x1b = self.x1.astype(BF16)
        self.acc = DEEPNORM_ALPHA * self.x1

    def _gate_up(self, c):
        cs = slice(c * FF_CHUNK, (c + 1) * FF_CHUNK)
        return _dot(self.x1b, self.wg_ref[:, cs]), _dot(self.x1b, self.wu_ref[:, cs])

    def chunk(self, c):
        if c == 0:
            self.gu = self._gate_up(0)
        g, u = self.gu
        if c + 1 < self.N_CHUNKS:
            self.gu = self._gate_up(c + 1)
        hid = (g * _sigmoid(g) * u).astype(BF16)
        self.acc = self.acc + _dot(hid, self.wd_ref[c * FF_CHUNK:(c + 1) * FF_CHUNK, :])

    def tail(self):
        return _layer_norm(self.acc, self.ln2g_ref[...], self.ln2b_ref[...])

    def run(self):
        self.head()
        for c in range(self.N_CHUNKS):
            self.chunk(c)
        return self.tail()


def _mix_blocks(zc, nb, pad_rows, thr_first, state, cbuf, wconv_ref, bconv_ref, wq_ref, wkt_ref, gb_ref,
                sinks_ref, gattn_ref, gml_ref, hook):
    rows = nb * BLOCK
    kp_lo, kp_hi, vp_lo, vp_hi, c_state, m_state = state
    c_state, m_state = list(c_state), list(m_state)
    lane = lax.broadcasted_iota(jnp.int32, (BLOCK, 128), 1)
    lo_half = lane < HEAD_DIM
    row_valid = None
    if pad_rows:
        row_valid = lax.broadcasted_iota(jnp.int32, (rows, 1), 0) >= pad_rows

    s_all, vmasks = [], []
    for blk in range(nb):
        rs = slice(blk * BLOCK, (blk + 1) * BLOCK)
        kb = zc[rs, ATT_W:ATT_W + KV_W].astype(BF16)
        vb = zc[rs, ATT_W + KV_W:ATT_W + 2 * KV_W].astype(BF16)
        zero_b = jnp.zeros_like(kb)
        k_lo, k_hi = jnp.where(lo_half, kb, zero_b), jnp.where(lo_half, zero_b, kb)
        v_lo, v_hi = jnp.where(lo_half, vb, zero_b), jnp.where(lo_half, zero_b, vb)
        kmask = jnp.concatenate([kp_lo, k_lo, kp_hi, k_hi], axis=0)
        vmasks.append(jnp.concatenate([vp_lo, v_lo, vp_hi, v_hi], axis=0))
        q4 = jnp.concatenate([zc[rs, r * 128:(r + 1) * 128] for r in range(4)], axis=0).astype(BF16)
        s_all.append(_dot_nt(q4, kmask))
        kp_lo, kp_hi, vp_lo, vp_hi = k_lo, k_hi, v_lo, v_hi
    hook("scores")

    c_in = zc[:, 768:1280]
    if row_valid is not None:
        c_in = jnp.where(row_valid, c_in, 0.0)
    cbuf[8:8 + rows, :] = c_in
    c_act = _conv_silu(lambda jj: cbuf[5 + jj:5 + jj + rows, :], wconv_ref, bconv_ref)
    cbuf[0:8, :] = cbuf[rows:rows + 8, :]
    hs = range(M_HEADS)
    ca_b = [c_act[:, h * 128:(h + 1) * 128].astype(BF16) for h in hs]
    qm_b = [_dot(ca_b[h], wq_ref[h]).astype(BF16) for h in hs]
    kmt_f = [_dot_nt(wkt_ref[h], ca_b[h]) * (M_HEAD_DIM ** -0.5) for h in hs]
    hook("front")

    ti = lax.broadcasted_iota(jnp.int32, (rows, rows), 0)
    si = lax.broadcasted_iota(jnp.int32, (rows, rows), 1)
    ltri = jnp.where((si <= ti) & ((si >> 7) == (ti >> 7)), 1.0, 0.0).astype(BF16)
    xg, _ = _gate_columns(zc[:, IN_MAIN:IN_PAD], gb_ref, row_valid, ltri)
    xgt = xg.T
    hook("gates")

    qi = lax.broadcasted_iota(jnp.int32, (BLOCK, 2 * BLOCK), 0)
    ks = lax.broadcasted_iota(jnp.int32, (BLOCK, 2 * BLOCK), 1)
    dist = WINDOW + qi - ks
    band = (dist >= 0) & (dist < WINDOW)
    distf = dist.astype(F32)
    att_rows = []
    for blk in range(nb):
        att_mask = band & (ks >= thr_first) if (blk == 0 and thr_first is not None) else band
        p_rows, inv = [], {}
        for r in range(4):
            pr = []
            for c in range(2):
                h = r + 4 * c
                s_blk = s_all[blk][r * 128:(r + 1) * 128, c * 256:(c + 1) * 256]
                (p,), inv[h] = _softmax_parts([(s_blk, att_mask, -ALIBI_SLOPES[h] * distf)], sinks_ref[h])
                pr.append(p.astype(BF16))
            p_rows.append(jnp.concatenate(pr, axis=1))
            hook("softmax")
        o_all = _dot(jnp.concatenate(p_rows, axis=0), vmasks[blk])
        att_rows.append(jnp.concatenate(
            [o_all[r * 128:(r + 1) * 128] * jnp.where(lo_half, inv[r], inv[r + 4]) for r in range(4)], axis=1))
        hook("pv")

    ones_col = jnp.where(lane == 0, 1.0, 0.0).astype(BF16)
    tb = lax.broadcasted_iota(jnp.int32, (BLOCK, BLOCK), 0)
    sb = lax.broadcasted_iota(jnp.int32, (BLOCK, BLOCK), 1)
    tri = sb <= tb
    hh_rows = [[] for _ in hs]
    for blk in range(nb):
        rs = slice(blk * BLOCK, (blk + 1) * BLOCK)
        b_c = [xg[rs, 4 + h:5 + h] for h in hs]
        b_r = [xgt[4 + h:5 + h, rs] for h in hs]
        li_r = [xgt[h:h + 1, rs] for h in hs]
        kmt = [kmt_f[h][:, rs] for h in hs]
        v_ext = [jnp.concatenate([zc[rs, 1280 + h * 128:1280 + (h + 1) * 128].astype(BF16), ones_col], axis=1)
                 for h in hs]
        s_qk = [_dot(qm_b[h][rs], kmt[h].astype(BF16)) for h in hs]
        inter = [_dot(qm_b[h][rs], c_state[h].astype(BF16)) for h in hs]
        hook("mlstm_a")
        dmat = [jnp.where(tri, b_c[h] - b_r[h] + li_r[h], -jnp.inf) for h in hs]
        m_inter = [b_c[h] + m_state[h] for h in hs]
        m_t = [jnp.maximum(m_inter[h], jnp.max(dmat[h], axis=1, keepdims=True)) for h in hs]
        w_inter = [jnp.exp(m_inter[h] - m_t[h]) for h in hs]
        sc_b = [(s_qk[h] * jnp.exp(dmat[h] - m_t[h])).astype(BF16) for h in hs]
        hook("mlstm_b")
        nd = [_dot(sc_b[h], v_ext[h]) + w_inter[h] * inter[h] for h in hs]
        for h in hs:
            m_end = m_t[h][BLOCK - 1:BLOCK, :]
            b_last = b_c[h][BLOCK - 1:BLOCK, :]
            decay = jnp.exp(b_last + m_state[h] - m_end)
            wk_r = jnp.exp(b_last - b_r[h] + li_r[h] - m_end)
            c_state[h] = decay * c_state[h] + _dot((kmt[h] * wk_r).astype(BF16), v_ext[h])
            m_state[h] = m_end
            hh_rows[h].append(nd[h][:, :128] / jnp.maximum(jnp.abs(nd[h][:, 128:129]), jnp.exp(-m_t[h])))
        hook("mlstm_c")

    y_att = _attn_head_norm(jnp.concatenate(att_rows, axis=0), gattn_ref[...])
    hook("attnorm")
    y_m = []
    for h in hs:
        hm = _sigmoid(zc[:, 1792 + h * 128:1792 + (h + 1) * 128]) * jnp.concatenate(hh_rows[h], axis=0)
        y_m.append(_mlstm_head_norm(hm, gml_ref[:, h * 128:(h + 1) * 128]))
        hook("headnorm")
    ymix = jnp.concatenate([y_att] + y_m, axis=1).astype(BF16)
    return ymix, (kp_lo, kp_hi, vp_lo, vp_hi, c_state, m_state)


def _meta_kernel(meta_ref, w_in_ref, wconv_ref, bconv_ref, wq_ref, wkt_ref, gb_ref, sinks_ref, gattn_ref,
                 gml_ref, kplo_o, kphi_o, vplo_o, vphi_o, cb_o, ctn_o, mst_o, zc, cbuf):
    zc[...] = _dot(meta_ref[...].astype(BF16), w_in_ref[...])
    cbuf[0:8, :] = jnp.zeros((8, MLSTM_W), F32)
    zb = jnp.zeros((BLOCK, KV_W), BF16)
    state = (zb, zb, zb, zb, [jnp.zeros((M_HEAD_DIM, 2 * M_HEAD_DIM), F32)] * M_HEADS,
             [jnp.zeros((1, 1), F32)] * M_HEADS)
    _, (kp_lo, kp_hi, vp_lo, vp_hi, c_state, m_state) = _mix_blocks(
        zc, 1, META_PAD, BLOCK + META_PAD, state, cbuf, wconv_ref, bconv_ref, wq_ref, wkt_ref, gb_ref,
        sinks_ref, gattn_ref, gml_ref, lambda name: None)
    kplo_o[...] = kp_lo
    kphi_o[...] = kp_hi
    vplo_o[...] = vp_lo
    vphi_o[...] = vp_hi
    cb_o[...] = cbuf[0:8, :]
    for h in range(M_HEADS):
        ctn_o[h] = c_state[h]
        mst_o[h:h + 1, :] = jnp.broadcast_to(m_state[h], (1, 128))
    mst_o[M_HEADS:8, :] = jnp.zeros((8 - M_HEADS, 128), F32)


def _pair_kernel(xin_ref, xres_ref, kplo_i, kphi_i, vplo_i, vphi_i, cb_i, ctn_i, mst_i,
                 w_in_ref, wconv_ref, bconv_ref, wq_ref, wkt_ref, gb_ref, sinks_ref, gattn_ref, gml_ref,
                 wout_ref, ln1g_ref, ln1b_ref, wg_ref, wu_ref, wd_ref, ln2g_ref, ln2b_ref,
                 y_ref, pk_ref, pv_ref, pconv_ref, pct_ref, pm_ref,
                 zcur, znext, kvlast, kplo, kphi, vplo, vphi, cbuf, ctn, mst, yprev, *, npair):
    t = pl.program_id(0)
    p = lax.rem(t + (npair - 1), npair)

    @pl.when(t == 0)
    def _init_pipeline():
        zcur[...] = jnp.zeros_like(zcur)
        yprev[...] = jnp.zeros_like(yprev)

    @pl.when((p == 0) | (t == 0))
    def _load_meta_state():
        kplo[...] = kplo_i[...]
        kphi[...] = kphi_i[...]
        vplo[...] = vplo_i[...]
        vphi[...] = vphi_i[...]
        cbuf[0:8, :] = cb_i[...]
        ctn[...] = ctn_i[...]
        mst[...] = mst_i[...]

    ffn = _FfnStages(xres_ref[...], yprev[...], wout_ref, ln1g_ref, ln1b_ref, wg_ref, wu_ref, wd_ref,
                     ln2g_ref, ln2b_ref)
    xb_b = xin_ref[...].astype(BF16)
    n_proj = -(-IN_PAD // PROJ_CHUNK)

    def proj(n):
        cs = slice(n * PROJ_CHUNK, min((n + 1) * PROJ_CHUNK, IN_PAD))
        znext[:, cs] = _dot(xb_b, w_in_ref[:, cs])

    proj_it, ffn_it = iter(range(n_proj)), iter(range(ffn.N_CHUNKS))

    def fill_proj(n):
        for c in [c for _, c in zip(range(n), proj_it)]:
            proj(c)

    def fill_ffn(n):
        for c in [c for _, c in zip(range(n), ffn_it)]:
            ffn.chunk(c)

    out = {}

    def hook(name):
        if name == "scores":
            fill_proj(2)
            ffn.head()
        elif name == "front":
            fill_proj(1)
        elif name == "gates":
            fill_ffn(1)
        elif name == "softmax":
            fill_ffn(1)
        elif name in ("mlstm_a", "mlstm_b"):
            fill_ffn(1)
        elif name == "mlstm_c":
            fill_proj(1)
        elif name == "attnorm":
            fill_ffn(ffn.N_CHUNKS)
            out["y"] = ffn.tail()
            fill_proj(1)
        elif name == "headnorm":
            fill_proj(1)

    state = (kplo[...], kphi[...], vplo[...], vphi[...], [ctn[h] for h in range(M_HEADS)],
             [mst[h:h + 1, 0:1] for h in range(M_HEADS)])
    thr_first = jnp.where(p == 0, META_PAD, 0)
    kvlast[...] = zcur[BLOCK:2 * BLOCK, ATT_W:ATT_W + 2 * KV_W]
    ymix, (kp_lo, kp_hi, vp_lo, vp_hi, c_state, m_state) = _mix_blocks(
        zcur, 2, 0, thr_first, state, cbuf, wconv_ref, bconv_ref, wq_ref, wkt_ref, gb_ref,
        sinks_ref, gattn_ref, gml_ref, hook)
    fill_ffn(ffn.N_CHUNKS)
    fill_proj(n_proj)
    y_ref[...] = out["y"]
    kplo[...] = kp_lo
    kphi[...] = kp_hi
    vplo[...] = vp_lo
    vphi[...] = vp_hi
    for h in range(M_HEADS):
        ctn[h] = c_state[h]
        mst[h:h + 1, :] = jnp.broadcast_to(m_state[h], (1, 128))
    yprev[...] = ymix
    zcur[...] = znext[...]

    @pl.when(p == npair - 1)
    def _final():
        pk_ref[...] = kvlast[:, 0:KV_W]
        pv_ref[...] = kvlast[:, KV_W:2 * KV_W]
        pconv_ref[...] = cbuf[0:8, :]
        pm_ref[...] = mst[...]
        for h in range(M_HEADS):
            pct_ref[h] = ctn[h].T


def _sample_kernel(xs_ref, ck_ref, cv_ref, sconv_ref, sc_ref, sn_ref, sm_ref,
                   w_in_ref, wconv_ref, bconv_ref, wqk_ref, gb_ref, sinks_ref,
                   gattn_ref, gml_ref,
                   ymix_ref, sk_ref, sv_ref, scv_ref, sco_ref, sno_ref, smo_ref, cext):
    ns, L = SEQ_TILE, 8
    xs = xs_ref[...].reshape(ns * L, D_MODEL)
    z = _dot(xs.astype(BF16), w_in_ref[...])

    lane = lax.broadcasted_iota(jnp.int32, (BLOCK, 128), 1)
    lo_half = lane < HEAD_DIM
    lo3 = lax.broadcasted_iota(jnp.int32, (ns, L, 128), 2) < HEAD_DIM
    ti = lax.broadcasted_iota(jnp.int32, (BLOCK, BLOCK), 0)
    si = lax.broadcasted_iota(jnp.int32, (BLOCK, BLOCK), 1)
    same_seq = (ti >> 3) == (si >> 3)
    tl, sl_ = ti & 7, si & 7
    causal_seq = same_seq & (sl_ <= tl)

    k_new = z[:, ATT_W:ATT_W + KV_W]
    v_new = z[:, ATT_W + KV_W:ATT_W + 2 * KV_W]
    kb, vb = k_new.astype(BF16), v_new.astype(BF16)
    zero_b = jnp.zeros_like(kb)
    kn_mask = jnp.concatenate([jnp.where(lo_half, kb, zero_b), jnp.where(lo_half, zero_b, kb)], axis=0)
    vn_mask = jnp.concatenate([jnp.where(lo_half, vb, zero_b), jnp.where(lo_half, zero_b, vb)], axis=0)
    qg = [z[:, r * 128:(r + 1) * 128] for r in range(4)]
    s_new = _dot_nt(jnp.concatenate(qg, axis=0).astype(BF16), kn_mask)
    qg3 = [q.reshape(ns, L, 128) for q in qg]
    q8 = jnp.concatenate([jnp.where(lo3, q, 0.0) for q in qg3] + [jnp.where(lo3, 0.0, q) for q in qg3],
                         axis=1).astype(BF16)
    s_cache = jnp.einsum('nqd,nkd->nqk', q8, ck_ref[...].astype(BF16), preferred_element_type=F32)

    mask_c = si > tl
    dist_c = (WINDOW + tl - si).astype(F32)
    dist_n = (tl - sl_).astype(F32)
    pc_parts, pn_rows, inv = {}, [], {}
    for r in range(4):
        pn = []
        for c in range(2):
            h = r + 4 * c
            sc_c = s_cache[:, c * 32 + r * L:c * 32 + (r + 1) * L, :].reshape(BLOCK, 128)
            sc_n = s_new[r * 128:(r + 1) * 128, c * 128:(c + 1) * 128]
            (p_c, p_n), inv[h] = _softmax_parts(
                [(sc_c, mask_c, -ALIBI_SLOPES[h] * dist_c), (sc_n, causal_seq, -ALIBI_SLOPES[h] * dist_n)],
                sinks_ref[h])
            pc_parts[(c, r)] = p_c.reshape(ns, L, 128)
            pn.append(p_n.astype(BF16))
        pn_rows.append(jnp.concatenate(pn, axis=1))
    p8 = jnp.concatenate([pc_parts[(c, r)] for c in range(2) for r in range(4)], axis=1).astype(BF16)
    o_cache = jnp.einsum('nqk,nkd->nqd', p8, cv_ref[...].astype(BF16), preferred_element_type=F32)
    o_new = _dot(jnp.concatenate(pn_rows, axis=0), vn_mask)
    groups = []
    for r in range(4):
        oa = o_cache[:, r * L:(r + 1) * L, :].reshape(BLOCK, 128)
        ob = o_cache[:, 32 + r * L:32 + (r + 1) * L, :].reshape(BLOCK, 128)
        o = jnp.where(lo_half, oa, ob) + o_new[r * 128:(r + 1) * 128]
        groups.append(o * jnp.where(lo_half, inv[r], inv[r + 4]))
    y_att = _attn_head_norm(jnp.concatenate(groups, axis=1), gattn_ref[...])

    sk_ref[:, 0:WINDOW - L, :] = ck_ref[:, L:WINDOW, :]
    sk_ref[:, WINDOW - L:WINDOW, :] = k_new.reshape(ns, L, 128)
    sv_ref[:, 0:WINDOW - L, :] = cv_ref[:, L:WINDOW, :]
    sv_ref[:, WINDOW - L:WINDOW, :] = v_new.reshape(ns, L, 128)

    c3 = z[:, 768:1280].reshape(ns, L, MLSTM_W)
    cext[:, 0:8, :] = sconv_ref[...]
    cext[:, 8:16, :] = c3
    scv_ref[...] = c3
    c_act = _conv_silu(lambda jj: cext[:, 5 + jj:5 + jj + L, :], wconv_ref, bconv_ref)
    c_act = c_act.reshape(BLOCK, MLSTM_W)

    lblk = jnp.where(causal_seq, 1.0, 0.0).astype(BF16)
    lseq = jnp.where(same_seq, 1.0, 0.0).astype(BF16)
    xg, blast = _gate_columns(z[:, IN_MAIN:IN_PAD], gb_ref, None, lblk, extra=lseq)
    xgt = xg.T
    m_rep = jnp.broadcast_to(sm_ref[...], (ns, L, 128)).reshape(BLOCK, 128)
    lane3 = lax.broadcasted_iota(jnp.int32, (ns, 1, 128), 2)
    m_out = jnp.zeros((ns, 1, 128), F32)

    y_m = []
    for h in range(M_HEADS):
        sl = slice(h * 128, (h + 1) * 128)
        qk = _dot(c_act[:, sl].astype(BF16), wqk_ref[h])
        qm = qk[:, :128]
        qm_b = qm.astype(BF16)
        km = qk[:, 128:] * (M_HEAD_DIM ** -0.5)
        km_b = km.astype(BF16)
        vh = z[:, 1280 + h * 128:1280 + (h + 1) * 128]
        b_c, li_c = xg[:, 4 + h:5 + h], xg[:, h:h + 1]
        b_r, li_r = xgt[4 + h:5 + h, :], xgt[h:h + 1, :]
        bl_c = blast[:, 4 + h:5 + h]
        m_prev = m_rep[:, h:h + 1]
        m_t, w_inter, num, den = _mlstm_intra(qm_b, km_b, vh.astype(BF16), causal_seq, b_c, b_r, li_r, m_prev)

        c_old = sc_ref[:, h]
        cq = _dot_nt(qm_b, c_old.reshape(ns * 128, 128).astype(BF16))
        num_inter = jnp.concatenate([cq[g * L:(g + 1) * L, g * 128:(g + 1) * 128] for g in range(ns)], axis=0)
        n_old = sn_ref[:, h:h + 1, :]
        n_rep = jnp.broadcast_to(n_old, (ns, L, 128)).reshape(BLOCK, 128)
        num = num + w_inter * num_inter
        den = den + w_inter * jnp.sum(qm * n_rep, axis=1, keepdims=True)
        hh = num / jnp.maximum(jnp.abs(den), jnp.exp(-m_t))

        m_end3 = jnp.broadcast_to(m_t, (BLOCK, 128)).reshape(ns, L, 128)[:, L - 1:L, :]
        m_end = jnp.broadcast_to(m_end3, (ns, L, 128)).reshape(BLOCK, 128)[:, 0:1]
        decay = jnp.exp(bl_c + m_prev - m_end)
        wk = jnp.exp(bl_c - b_c + li_c - m_end)
        k_big = jnp.concatenate(
            [jnp.where((ti >> 3) == g, km_b, jnp.zeros_like(km_b)) for g in range(ns)], axis=1)
        c_up = _dot_tn((vh * wk).astype(BF16), k_big)
        c_up3 = jnp.stack([c_up[:, g * 128:(g + 1) * 128] for g in range(ns)], axis=0)
        decay3 = jnp.broadcast_to(decay, (BLOCK, 128)).reshape(ns, L, 128)[:, 0:1, :]
        sco_ref[:, h] = decay3 * c_old + c_up3
        sno_ref[:, h:h + 1, :] = decay3 * n_old + jnp.sum((km * wk).reshape(ns, L, 128), axis=1, keepdims=True)
        m_out = jnp.where(lane3 == h, m_end3, m_out)

        hm = _sigmoid(z[:, 1792 + h * 128:1792 + (h + 1) * 128]) * hh
        y_m.append(_mlstm_head_norm(hm, gml_ref[:, sl]))

    smo_ref[...] = m_out
    ymix_ref[...] = jnp.concatenate([y_att] + y_m, axis=1).astype(BF16)


def _post_kernel(x_ref, y_ref, wout_ref, ln1g_ref, ln1b_ref, wg_ref, wu_ref, wd_ref, ln2g_ref, ln2b_ref, o_ref):
    o_ref[...] = _FfnStages(x_ref[...], y_ref[...], wout_ref, ln1g_ref, ln1b_ref, wg_ref, wu_ref, wd_ref,
                            ln2g_ref, ln2b_ref).run()


def _const_spec(shape):
    nd = len(shape)
    return pl.BlockSpec(shape, lambda *_: (0,) * nd, pipeline_mode=pl.Buffered(1))


def _mixer_weight_specs(qk_specs):
    return [
        _const_spec((D_MODEL, IN_PAD)),
        _const_spec((CONV_W, MLSTM_W)),
        _const_spec((1, MLSTM_W)),
    ] + qk_specs + [
        _const_spec((1, 128)),
        pl.BlockSpec(memory_space=pltpu.SMEM),
        _const_spec((1, ATT_W)),
        _const_spec((1, MLSTM_W)),
    ]


def _post_weight_specs(d):
    return [
        _const_spec((d, d)),
        _const_spec((1, d)), _const_spec((1, d)),
        _const_spec((d, D_FF)), _const_spec((d, D_FF)), _const_spec((D_FF, d)),
        _const_spec((1, d)), _const_spec((1, d)),
    ]


def _prompt_layer(x_prompt, meta_blk, mixer_w, post_w):
    b, s, d = x_prompt.shape
    qk_specs = [_const_spec((M_HEADS, M_HEAD_DIM, M_HEAD_DIM))] * 2
    state_shapes = [((BLOCK, KV_W), BF16)] * 4 + [
        ((8, MLSTM_W), F32), ((M_HEADS, M_HEAD_DIM, 2 * M_HEAD_DIM), F32), ((8, 128), F32)]
    meta_state = pl.pallas_call(
        _meta_kernel,
        grid=(1,),
        in_specs=[_const_spec((BLOCK, d))] + _mixer_weight_specs(qk_specs),
        out_specs=tuple(_const_spec(shape) for shape, _ in state_shapes),
        out_shape=tuple(jax.ShapeDtypeStruct(shape, dt) for shape, dt in state_shapes),
        scratch_shapes=[pltpu.VMEM((BLOCK, IN_PAD), F32), pltpu.VMEM((BLOCK + 8, MLSTM_W), F32)],
        compiler_params=pltpu.CompilerParams(
            dimension_semantics=("arbitrary",), vmem_limit_bytes=VMEM_LIMIT),
        name="meta_state",
    )(meta_blk, *mixer_w)

    rows = 2 * BLOCK
    npair = s // rows
    total = b * npair
    steps = total + 2

    def block_map(lag):
        def index(t):
            tb = jnp.clip(t - lag, 0, total - 1)
            return (tb // npair, tb % npair, 0)
        return index

    def smap(t):
        return (jnp.clip(t - 1, 0, total - 1) // npair, 0, 0)

    out_shape = (
        jax.ShapeDtypeStruct((b, s, d), F32),
        jax.ShapeDtypeStruct((b, BLOCK, KV_W), F32),
        jax.ShapeDtypeStruct((b, BLOCK, KV_W), F32),
        jax.ShapeDtypeStruct((b, 8, MLSTM_W), F32),
        jax.ShapeDtypeStruct((b, M_HEADS, 2 * M_HEAD_DIM, M_HEAD_DIM), F32),
        jax.ShapeDtypeStruct((b, 8, 128), F32),
    )
    out_specs = (
        pl.BlockSpec((None, rows, d), block_map(2)),
        pl.BlockSpec((None, BLOCK, KV_W), smap),
        pl.BlockSpec((None, BLOCK, KV_W), smap),
        pl.BlockSpec((None, 8, MLSTM_W), smap),
        pl.BlockSpec((None, M_HEADS, 2 * M_HEAD_DIM, M_HEAD_DIM), lambda t: smap(t) + (0,)),
        pl.BlockSpec((None, 8, 128), smap),
    )
    return pl.pallas_call(
        functools.partial(_pair_kernel, npair=npair),
        grid=(steps,),
        in_specs=[pl.BlockSpec((None, rows, d), block_map(0)), pl.BlockSpec((None, rows, d), block_map(2))]
        + [_const_spec(shape) for shape, _ in state_shapes]
        + _mixer_weight_specs(qk_specs) + _post_weight_specs(d),
        out_specs=out_specs,
        out_shape=out_shape,
        scratch_shapes=[
            pltpu.VMEM((rows, IN_PAD), F32),
            pltpu.VMEM((rows, IN_PAD), F32),
            pltpu.VMEM((BLOCK, 2 * KV_W), F32),
        ] + [pltpu.VMEM((BLOCK, KV_W), BF16)] * 4 + [
            pltpu.VMEM((rows + 8, MLSTM_W), F32),
            pltpu.VMEM((M_HEADS, M_HEAD_DIM, 2 * M_HEAD_DIM), F32),
            pltpu.VMEM((8, 128), F32),
            pltpu.VMEM((rows, d), BF16),
        ],
        compiler_params=pltpu.CompilerParams(
            dimension_semantics=("arbitrary",), vmem_limit_bytes=VMEM_LIMIT),
        name="prompt_layer",
    )(x_prompt, x_prompt, *meta_state, *mixer_w, *post_w)


def _sample_mixer(x_sample, ck, cv, sconv8, s_c, s_n, s_m3, mixer_w):
    n, l, d = x_sample.shape
    t = SEQ_TILE
    m3 = lambda i: (i, 0, 0)
    m4 = lambda i: (i, 0, 0, 0)
    out_shape = (
        jax.ShapeDtypeStruct((n * l, d), BF16),
        jax.ShapeDtypeStruct((n, WINDOW, KV_W), F32),
        jax.ShapeDtypeStruct((n, WINDOW, KV_W), F32),
        jax.ShapeDtypeStruct((n, l, MLSTM_W), F32),
        jax.ShapeDtypeStruct((n, M_HEADS, M_HEAD_DIM, M_HEAD_DIM), F32),
        jax.ShapeDtypeStruct((n, M_HEADS, M_HEAD_DIM), F32),
        jax.ShapeDtypeStruct((n, 1, 128), F32),
    )
    out_specs = (
        pl.BlockSpec((t * l, d), lambda i: (i, 0)),
        pl.BlockSpec((t, WINDOW, KV_W), m3),
        pl.BlockSpec((t, WINDOW, KV_W), m3),
        pl.BlockSpec((t, l, MLSTM_W), m3),
        pl.BlockSpec((t, M_HEADS, M_HEAD_DIM, M_HEAD_DIM), m4),
        pl.BlockSpec((t, M_HEADS, M_HEAD_DIM), m3),
        pl.BlockSpec((t, 1, 128), m3),
    )
    in_specs = [
        pl.BlockSpec((t, l, d), m3),
        pl.BlockSpec((t, WINDOW, KV_W), m3),
        pl.BlockSpec((t, WINDOW, KV_W), m3),
        pl.BlockSpec((t, 8, MLSTM_W), m3),
        pl.BlockSpec((t, M_HEADS, M_HEAD_DIM, M_HEAD_DIM), m4),
        pl.BlockSpec((t, M_HEADS, M_HEAD_DIM), m3),
        pl.BlockSpec((t, 1, 128), m3),
    ] + _mixer_weight_specs([_const_spec((M_HEADS, M_HEAD_DIM, 2 * M_HEAD_DIM))])
    return pl.pallas_call(
        _sample_kernel,
        grid=(n // t,),
        in_specs=in_specs,
        out_specs=out_specs,
        out_shape=out_shape,
        scratch_shapes=[pltpu.VMEM((t, 16, MLSTM_W), F32)],
        compiler_params=pltpu.CompilerParams(
            dimension_semantics=("arbitrary",), vmem_limit_bytes=VMEM_LIMIT),
        name="sample_mixer",
    )(x_sample, ck, cv, sconv8, s_c, s_n, s_m3, *mixer_w)


def _post(x_rows, y_rows, post_w):
    rows, d = x_rows.shape
    tm = min(POST_ROWS, rows)
    rmap = lambda i: (i, 0)
    in_specs = [pl.BlockSpec((tm, d), rmap), pl.BlockSpec((tm, d), rmap)] + _post_weight_specs(d)
    return pl.pallas_call(
        _post_kernel,
        grid=(rows // tm,),
        in_specs=in_specs,
        out_specs=pl.BlockSpec((tm, d), rmap),
        out_shape=jax.ShapeDtypeStruct((rows, d), F32),
        compiler_params=pltpu.CompilerParams(
            dimension_semantics=("arbitrary",), vmem_limit_bytes=VMEM_LIMIT),
        name="post_ffn",
    )(x_rows, y_rows, *post_w)


def kernel(x_prompt, x_sample, cache_k, cache_v, state_conv, state_C, state_n, state_m, meta_tokens,
           w_in, w_conv, b_conv, w_mq, w_mk, b_i, b_f, attn_sinks, g_attn, g_mlstm, w_out,
           ln1_g, ln1_b, w_gate, w_up, w_down, ln2_g, ln2_b):
    b, s, d = x_prompt.shape
    n, l, _ = x_sample.shape

    w_in_p = w_in[0][:, _IN_COLS].astype(BF16)
    wqk = jnp.concatenate([w_mq[0], w_mk[0]], axis=-1).astype(BF16)
    gb = jnp.concatenate([b_i[0], b_f[0], jnp.zeros((128 - IN_GATES,), F32)]).reshape(1, 128)
    mixer_head = (w_in_p, w_conv[0], b_conv[0].reshape(1, MLSTM_W))
    mixer_tail = (gb, attn_sinks[0], g_attn[0][_ATT_PERM].reshape(1, ATT_W), g_mlstm[0].reshape(1, MLSTM_W))
    wq_b = w_mq[0].astype(BF16)
    wkt_b = jnp.swapaxes(w_mk[0], 1, 2).astype(BF16)
    w_out_p = jnp.concatenate([w_out[0][:ATT_W][_ATT_PERM], w_out[0][ATT_W:]], axis=0).astype(BF16)
    post_w = (w_out_p, ln1_g[0].reshape(1, d), ln1_b[0].reshape(1, d),
              w_gate[0].astype(BF16), w_up[0].astype(BF16), w_down[0].astype(BF16),
              ln2_g[0].reshape(1, d), ln2_b[0].reshape(1, d))

    meta_blk = jnp.concatenate([jnp.zeros((META_PAD, d), x_prompt.dtype), meta_tokens.astype(x_prompt.dtype)], axis=0)
    y_prompt, pk, pv, pconv8, pct, pm8 = _prompt_layer(
        x_prompt, meta_blk, mixer_head + (wq_b, wkt_b) + mixer_tail, post_w)

    mixer_w = mixer_head + (wqk,) + mixer_tail
    sconv8 = jnp.pad(state_conv[0], ((0, 0), (8 - (CONV_W - 1), 0), (0, 0)))
    s_m3 = jnp.pad(state_m[0], ((0, 0), (0, 128 - M_HEADS))).reshape(n, 1, 128)
    ymix_s, sk, sv, scv, s_c, s_n, s_mo = _sample_mixer(
        x_sample, cache_k[0].reshape(n, WINDOW, KV_W), cache_v[0].reshape(n, WINDOW, KV_W),
        sconv8, state_C[0], state_n[0], s_m3, mixer_w)

    y_sample = _post(x_sample.reshape(n * l, d), ymix_s, post_w).reshape(n, l, d)

    kv5 = lambda a: a.reshape(1, a.shape[0], WINDOW, 2, HEAD_DIM)
    return (y_prompt, y_sample,
            kv5(pk), kv5(pv), pconv8[None, :, 8 - (CONV_W - 1):, :], pct[None, :, :, :M_HEAD_DIM, :],
            pct[None, :, :, M_HEAD_DIM, :], pm8[None, :, :M_HEADS, 0],
            kv5(sk), kv5(sv), scv[None, :, l - (CONV_W - 1):, :], s_c[None], s_n[None],
            s_mo[None, :, 0, :M_HEADS])
```

```python
import functools

import numpy as np
import jax
import jax.numpy as jnp
from jax import lax
from jax.experimental import pallas as pl
from jax.experimental.pallas import tpu as pltpu

F32 = jnp.float32
BF16 = jnp.bfloat16

D_MODEL = 1024
ATT_W = 512
MLSTM_W = 512
HEAD_DIM = 64
N_HEADS = 8
KV_W = 128
WINDOW = 128
BLOCK = 128
M_HEADS = 4
M_HEAD_DIM = 128
CONV_W = 4
N_META = 16
META_PAD = BLOCK - N_META
D_FF = 2816
IN_MAIN = 2304
IN_GATES = 2 * M_HEADS
IN_PAD = IN_MAIN + 128
DEPTH = 1
ALIBI_SLOPES = tuple(2.0 ** (-8.0 * (h + 1) / N_HEADS) for h in range(N_HEADS))
DEEPNORM_ALPHA = (2.0 * DEPTH) ** 0.25
EPS = 1e-5
SEQ_TILE = 16
POST_ROWS = 256
FF_CHUNK = 256
FF_LOOKAHEAD = 2
PROJ_CHUNK = 256
VMEM_LIMIT = 56 * 1024 * 1024

_ATT_PERM = np.concatenate(
    [np.concatenate([np.arange(r * HEAD_DIM, (r + 1) * HEAD_DIM),
                     np.arange((r + 4) * HEAD_DIM, (r + 5) * HEAD_DIM)]) for r in range(4)])


def _dot(a, b):
    return jnp.dot(a, b, preferred_element_type=F32)


def _dot_nt(a, b):
    return lax.dot_general(a, b, (((1,), (1,)), ((), ())), preferred_element_type=F32)


def _dot_tn(a, b):
    return lax.dot_general(a, b, (((0,), (0,)), ((), ())), preferred_element_type=F32)


def _split2(x):
    hi = x.astype(BF16)
    lo = (x - hi.astype(F32)).astype(BF16)
    return hi, lo


def _sigmoid(x):
    return 1.0 / (1.0 + jnp.exp(-x))


def _log_sigmoid(x):
    return -(jnp.maximum(-x, 0.0) + jnp.log1p(jnp.exp(-jnp.abs(x))))


def _layer_norm(x, g, b):
    mu = jnp.mean(x, axis=-1, keepdims=True)
    xc = x - mu
    var = jnp.mean(xc * xc, axis=-1, keepdims=True)
    return xc * lax.rsqrt(var + EPS) * g + b


def _attn_head_norm(att, g):
    lo_half = lax.broadcasted_iota(jnp.int32, (att.shape[0], 128), 1) < HEAD_DIM

    def seg_mean(x):
        s_all = jnp.sum(x, axis=1, keepdims=True)
        s_lo = jnp.sum(jnp.where(lo_half, x, 0.0), axis=1, keepdims=True)
        return jnp.where(lo_half, s_lo, s_all - s_lo) * (1.0 / HEAD_DIM)

    out = []
    for grp in range(att.shape[1] // 128):
        x = att[:, grp * 128:(grp + 1) * 128]
        xc = x - seg_mean(x)
        out.append(xc * lax.rsqrt(seg_mean(xc * xc) + EPS))
    return jnp.concatenate(out, axis=1) * g


def _mlstm_head_norm(hm, g):
    mu = jnp.mean(hm, axis=-1, keepdims=True)
    xc = hm - mu
    var = jnp.mean(xc * xc, axis=-1, keepdims=True)
    return xc * lax.rsqrt(var + EPS) * g


def _softmax_parts(parts, sink):
    sp = [jnp.where(m, s * (HEAD_DIM ** -0.5) + a, -jnp.inf) for s, m, a in parts]
    mx = sink
    for s in sp:
        mx = jnp.maximum(mx, jnp.max(s, axis=1, keepdims=True))
    ps = [jnp.exp(s - mx) for s in sp]
    den = jnp.exp(sink - mx)
    for p in ps:
        den = den + jnp.sum(p, axis=1, keepdims=True)
    return ps, 1.0 / den


def _mlstm_intra(qm_b, km_b, vh_b, mask, b_c, b_r, li_r, m_prev):
    dmat = jnp.where(mask, b_c - b_r + li_r, -jnp.inf)
    m_inter = b_c + m_prev
    m_t = jnp.maximum(m_inter, jnp.max(dmat, axis=1, keepdims=True))
    w_inter = jnp.exp(m_inter - m_t)
    sc = _dot_nt(qm_b, km_b) * jnp.exp(dmat - m_t)
    num = _dot(sc.astype(BF16), vh_b)
    den = jnp.sum(sc, axis=1, keepdims=True)
    return m_t, w_inter, num, den


def _conv_silu(window, wconv_ref, bconv_ref):
    acc = bconv_ref[...]
    for j in range(CONV_W):
        acc = acc + window(j) * wconv_ref[j:j + 1, :]
    return acc * _sigmoid(acc)


def _gate_columns(gates, gb_ref, row_valid, ltri, extra=None):
    lane = lax.broadcasted_iota(jnp.int32, gates.shape, 1)
    gb = gates + gb_ref[...]
    is_i = lane < M_HEADS
    is_f = (lane >= M_HEADS) & (lane < 2 * M_HEADS)
    logf = _log_sigmoid(gb)
    if row_valid is not None:
        lf = jnp.where(is_f & row_valid, logf, 0.0)
        li = jnp.where(row_valid, gb, -jnp.inf)
    else:
        lf = jnp.where(is_f, logf, 0.0)
        li = gb
    hi, lo = _split2(lf)
    bc = _dot(ltri, hi) + _dot(ltri, lo)
    x = jnp.where(is_i, li, bc)
    if extra is None:
        return x, None
    return x, _dot(extra, hi) + _dot(extra, lo)


class _FfnStages:
    N_CHUNKS = D_FF // FF_CHUNK

    def __init__(self, x, ymix, wout_ref, ln1g_ref, ln1b_ref, wg_ref, wu_ref, wd_ref, ln2g_ref, ln2b_ref):
        self.x, self.ymix = x, ymix
        self.wout_ref, self.ln1g_ref, self.ln1b_ref = wout_ref, ln1g_ref, ln1b_ref
        self.wg_ref, self.wu_ref, self.wd_ref = wg_ref, wu_ref, wd_ref
        self.ln2g_ref, self.ln2b_ref = ln2g_ref, ln2b_ref

    def head(self):
        self.x1 = _layer_norm(DEEPNORM_ALPHA * self.x + _dot(self.ymix, self.wout_ref[...]),
                              self.ln1g_ref[...], self.ln1b_ref[...])
        self.x1b = self.x1.astype(BF16)
        self.acc = DEEPNORM_ALPHA * self.x1

    def _gate_up(self, c):
        cs = slice(c * FF_CHUNK, (c + 1) * FF_CHUNK)
        return _dot(self.x1b, self.wg_ref[:, cs]), _dot(self.x1b, self.wu_ref[:, cs])

    def chunk(self, c):
        if c == 0:
            self.gu = {}
        for n in range(c, min(c + FF_LOOKAHEAD, self.N_CHUNKS - 1) + 1):
            if n not in self.gu:
                self.gu[n] = self._gate_up(n)
        g, u = self.gu.pop(c)
        hid = (g * _sigmoid(g) * u).astype(BF16)
        self.acc = self.acc + _dot(hid, self.wd_ref[c * FF_CHUNK:(c + 1) * FF_CHUNK, :])

    def tail(self):
        return _layer_norm(self.acc, self.ln2g_ref[...], self.ln2b_ref[...])

    def run(self):
        self.head()
        for c in range(self.N_CHUNKS):
            self.chunk(c)
        return self.tail()


def _mix_blocks(zc, nb, pad_rows, thr_first, state, cbuf, wconv_ref, bconv_ref, wq_ref, wkt_ref, gb_ref,
                sinks_ref, gattn_ref, gml_ref, hook):
    rows = nb * BLOCK
    kp_lo, kp_hi, vp_lo, vp_hi, c_state, m_state = state
    c_state, m_state = list(c_state), list(m_state)
    lane = lax.broadcasted_iota(jnp.int32, (BLOCK, 128), 1)
    lo_half = lane < HEAD_DIM
    row_valid = None
    if pad_rows:
        row_valid = lax.broadcasted_iota(jnp.int32, (rows, 1), 0) >= pad_rows

    s_all, vmasks = [], []
    for blk in range(nb):
        rs = slice(blk * BLOCK, (blk + 1) * BLOCK)
        kb = zc[rs, ATT_W:ATT_W + KV_W].astype(BF16)
        vb = zc[rs, ATT_W + KV_W:ATT_W + 2 * KV_W].astype(BF16)
        zero_b = jnp.zeros_like(kb)
        k_lo, k_hi = jnp.where(lo_half, kb, zero_b), jnp.where(lo_half, zero_b, kb)
        v_lo, v_hi = jnp.where(lo_half, vb, zero_b), jnp.where(lo_half, zero_b, vb)
        kmask = jnp.concatenate([kp_lo, k_lo, kp_hi, k_hi], axis=0)
        vmasks.append(jnp.concatenate([vp_lo, v_lo, vp_hi, v_hi], axis=0))
        q4 = jnp.concatenate([zc[rs, r * 128:(r + 1) * 128] for r in range(4)], axis=0).astype(BF16)
        s_all.append(_dot_nt(q4, kmask))
        kp_lo, kp_hi, vp_lo, vp_hi = k_lo, k_hi, v_lo, v_hi
    hook("scores")

    c_in = zc[:, 768:1280]
    if row_valid is not None:
        c_in = jnp.where(row_valid, c_in, 0.0)
    cbuf[8:8 + rows, :] = c_in
    c_act = _conv_silu(lambda jj: cbuf[5 + jj:5 + jj + rows, :], wconv_ref, bconv_ref)
    cbuf[0:8, :] = cbuf[rows:rows + 8, :]
    hs = range(M_HEADS)
    ca_b = [c_act[:, h * 128:(h + 1) * 128].astype(BF16) for h in hs]
    qm_b = [_dot(ca_b[h], wq_ref[h]).astype(BF16) for h in hs]
    kmt_f = [_dot_nt(wkt_ref[h], ca_b[h]) * (M_HEAD_DIM ** -0.5) for h in hs]
    hook("front")

    ti = lax.broadcasted_iota(jnp.int32, (rows, rows), 0)
    si = lax.broadcasted_iota(jnp.int32, (rows, rows), 1)
    ltri = jnp.where((si <= ti) & ((si >> 7) == (ti >> 7)), 1.0, 0.0).astype(BF16)
    xg, _ = _gate_columns(zc[:, IN_MAIN:IN_PAD], gb_ref, row_valid, ltri)
    xgt = xg.T
    hook("gates")

    qi = lax.broadcasted_iota(jnp.int32, (BLOCK, BLOCK), 0)
    ks = lax.broadcasted_iota(jnp.int32, (BLOCK, BLOCK), 1)
    own = ks <= qi
    distf = jnp.where(own, qi - ks, WINDOW + qi - ks).astype(F32)
    zero_p = jnp.zeros((BLOCK, BLOCK), BF16)
    att_rows = []
    for blk in range(nb):
        live = None
        if blk == 0 and thr_first is not None:
            live = jnp.where(own, ks + BLOCK, ks) >= thr_first
        p_rows, inv = [], {}
        for r in range(4):
            pr = []
            for c in range(2):
                h = r + 4 * c
                s_blk = s_all[blk][r * 128:(r + 1) * 128, c * 256:(c + 1) * 256]
                sp = jnp.where(own, s_blk[:, BLOCK:], s_blk[:, :BLOCK]) * (HEAD_DIM ** -0.5) \
                    - ALIBI_SLOPES[h] * distf
                if live is not None:
                    sp = jnp.where(live, sp, -jnp.inf)
                sink = sinks_ref[h]
                mx = jnp.maximum(jnp.max(sp, axis=1, keepdims=True), sink)
                p = jnp.exp(sp - mx)
                inv[h] = 1.0 / (jnp.sum(p, axis=1, keepdims=True) + jnp.exp(sink - mx))
                p_b = p.astype(BF16)
                pr += [jnp.where(own, zero_p, p_b), jnp.where(own, p_b, zero_p)]
            p_rows.append(jnp.concatenate(pr, axis=1))
            hook("softmax")
        o_all = _dot(jnp.concatenate(p_rows, axis=0), vmasks[blk])
        att_rows.append(jnp.concatenate(
            [o_all[r * 128:(r + 1) * 128] * jnp.where(lo_half, inv[r], inv[r + 4]) for r in range(4)], axis=1))
        hook("pv")

    ones_col = jnp.where(lane == 0, 1.0, 0.0).astype(BF16)
    tb = lax.broadcasted_iota(jnp.int32, (BLOCK, BLOCK), 0)
    sb = lax.broadcasted_iota(jnp.int32, (BLOCK, BLOCK), 1)
    tri = sb <= tb
    hh_rows = [[] for _ in hs]
    for blk in range(nb):
        rs = slice(blk * BLOCK, (blk + 1) * BLOCK)
        b_c = [xg[rs, 4 + h:5 + h] for h in hs]
        b_r = [xgt[4 + h:5 + h, rs] for h in hs]
        li_r = [xgt[h:h + 1, rs] for h in hs]
        kmt = [kmt_f[h][:, rs] for h in hs]
        v_ext = [jnp.concatenate([zc[rs, 1280 + h * 128:1280 + (h + 1) * 128].astype(BF16), ones_col], axis=1)
                 for h in hs]
        s_qk = [_dot(qm_b[h][rs], kmt[h].astype(BF16)) for h in hs]
        inter = [_dot(qm_b[h][rs], c_state[h].astype(BF16)) for h in hs]
        hook("mlstm_a")
        dmat = [jnp.where(tri, b_c[h] - b_r[h] + li_r[h], -jnp.inf) for h in hs]
        m_inter = [b_c[h] + m_state[h] for h in hs]
        m_t = [jnp.maximum(m_inter[h], jnp.max(dmat[h], axis=1, keepdims=True)) for h in hs]
        w_inter = [jnp.exp(m_inter[h] - m_t[h]) for h in hs]
        sc_b = [(s_qk[h] * jnp.exp(dmat[h] - m_t[h])).astype(BF16) for h in hs]
        hook("mlstm_b")
        nd = [_dot(sc_b[h], v_ext[h]) + w_inter[h] * inter[h] for h in hs]
        for h in hs:
            m_end = m_t[h][BLOCK - 1:BLOCK, :]
            b_last = b_c[h][BLOCK - 1:BLOCK, :]
            decay = jnp.exp(b_last + m_state[h] - m_end)
            wk_r = jnp.exp(b_last - b_r[h] + li_r[h] - m_end)
            c_state[h] = decay * c_state[h] + _dot((kmt[h] * wk_r).astype(BF16), v_ext[h])
            m_state[h] = m_end
            hh_rows[h].append(nd[h][:, :128] / jnp.maximum(jnp.abs(nd[h][:, 128:129]), jnp.exp(-m_t[h])))
        hook("mlstm_c")

    y_att = _attn_head_norm(jnp.concatenate(att_rows, axis=0), gattn_ref[...])
    hook("attnorm")
    y_m = []
    for h in hs:
        hm = _sigmoid(zc[:, 1792 + h * 128:1792 + (h + 1) * 128]) * jnp.concatenate(hh_rows[h], axis=0)
        y_m.append(_mlstm_head_norm(hm, gml_ref[:, h * 128:(h + 1) * 128]))
        hook("headnorm")
    ymix = jnp.concatenate([y_att] + y_m, axis=1).astype(BF16)
    return ymix, (kp_lo, kp_hi, vp_lo, vp_hi, c_state, m_state)


def _meta_kernel(meta_ref, w_in_ref, wconv_ref, bconv_ref, wq_ref, wkt_ref, gb_ref, sinks_ref, gattn_ref,
                 gml_ref, kplo_o, kphi_o, vplo_o, vphi_o, cb_o, ctn_o, mst_o, zc, cbuf):
    zc[...] = _dot(meta_ref[...].astype(BF16), w_in_ref[...])
    cbuf[0:8, :] = jnp.zeros((8, MLSTM_W), F32)
    zb = jnp.zeros((BLOCK, KV_W), BF16)
    state = (zb, zb, zb, zb, [jnp.zeros((M_HEAD_DIM, 2 * M_HEAD_DIM), F32)] * M_HEADS,
             [jnp.zeros((1, 1), F32)] * M_HEADS)
    _, (kp_lo, kp_hi, vp_lo, vp_hi, c_state, m_state) = _mix_blocks(
        zc, 1, META_PAD, BLOCK + META_PAD, state, cbuf, wconv_ref, bconv_ref, wq_ref, wkt_ref, gb_ref,
        sinks_ref, gattn_ref, gml_ref, lambda name: None)
    kplo_o[...] = kp_lo
    kphi_o[...] = kp_hi
    vplo_o[...] = vp_lo
    vphi_o[...] = vp_hi
    cb_o[...] = cbuf[0:8, :]
    for h in range(M_HEADS):
        ctn_o[h] = c_state[h]
        mst_o[h:h + 1, :] = jnp.broadcast_to(m_state[h], (1, 128))
    mst_o[M_HEADS:8, :] = jnp.zeros((8 - M_HEADS, 128), F32)


def _pair_kernel(xin_ref, xres_ref, kplo_i, kphi_i, vplo_i, vphi_i, cb_i, ctn_i, mst_i,
                 w_in_ref, wconv_ref, bconv_ref, wq_ref, wkt_ref, gb_ref, sinks_ref, gattn_ref, gml_ref,
                 wout_ref, ln1g_ref, ln1b_ref, wg_ref, wu_ref, wd_ref, ln2g_ref, ln2b_ref,
                 y_ref, pk_ref, pv_ref, pconv_ref, pct_ref, pm_ref,
                 zcur, znext, kvlast, kplo, kphi, vplo, vphi, cbuf, ctn, mst, yprev, *, npair):
    t = pl.program_id(0)
    p = lax.rem(t + (npair - 1), npair)

    @pl.when(t == 0)
    def _init_pipeline():
        zcur[...] = jnp.zeros_like(zcur)
        yprev[...] = jnp.zeros_like(yprev)

    @pl.when((p == 0) | (t == 0))
    def _load_meta_state():
        kplo[...] = kplo_i[...]
        kphi[...] = kphi_i[...]
        vplo[...] = vplo_i[...]
        vphi[...] = vphi_i[...]
        cbuf[0:8, :] = cb_i[...]
        ctn[...] = ctn_i[...]
        mst[...] = mst_i[...]

    ffn = _FfnStages(xres_ref[...], yprev[...], wout_ref, ln1g_ref, ln1b_ref, wg_ref, wu_ref, wd_ref,
                     ln2g_ref, ln2b_ref)
    xb_b = xin_ref[...].astype(BF16)
    n_proj = -(-IN_PAD // PROJ_CHUNK)

    def proj(n):
        cs = slice(n * PROJ_CHUNK, min((n + 1) * PROJ_CHUNK, IN_PAD))
        znext[:, cs] = _dot(xb_b, w_in_ref[:, cs])

    proj_it, ffn_it = iter(range(n_proj)), iter(range(ffn.N_CHUNKS))

    def fill_proj(n):
        for c in [c for _, c in zip(range(n), proj_it)]:
            proj(c)

    def fill_ffn(n):
        for c in [c for _, c in zip(range(n), ffn_it)]:
            ffn.chunk(c)

    out = {}
    seen = {"softmax": 0}

    def hook(name):
        if name == "scores":
            fill_proj(2)
            ffn.head()
        elif name == "front":
            fill_proj(1)
        elif name == "gates":
            fill_ffn(1)
        elif name == "softmax":
            seen["softmax"] += 1
            if seen["softmax"] % 2 == 0:
                fill_ffn(1)
        elif name in ("pv", "mlstm_a", "mlstm_b"):
            fill_ffn(1)
        elif name == "mlstm_c":
            fill_proj(1)
        elif name == "attnorm":
            fill_ffn(ffn.N_CHUNKS)
            out["y"] = ffn.tail()
            fill_proj(1)
        elif name == "headnorm":
            fill_proj(1)

    state = (kplo[...], kphi[...], vplo[...], vphi[...], [ctn[h] for h in range(M_HEADS)],
             [mst[h:h + 1, 0:1] for h in range(M_HEADS)])
    thr_first = jnp.where(p == 0, META_PAD, 0)
    kvlast[...] = zcur[BLOCK:2 * BLOCK, ATT_W:ATT_W + 2 * KV_W]
    ymix, (kp_lo, kp_hi, vp_lo, vp_hi, c_state, m_state) = _mix_blocks(
        zcur, 2, 0, thr_first, state, cbuf, wconv_ref, bconv_ref, wq_ref, wkt_ref, gb_ref,
        sinks_ref, gattn_ref, gml_ref, hook)
    fill_ffn(ffn.N_CHUNKS)
    fill_proj(n_proj)
    y_ref[...] = out["y"]
    kplo[...] = kp_lo
    kphi[...] = kp_hi
    vplo[...] = vp_lo
    vphi[...] = vp_hi
    for h in range(M_HEADS):
        ctn[h] = c_state[h]
        mst[h:h + 1, :] = jnp.broadcast_to(m_state[h], (1, 128))
    yprev[...] = ymix
    zcur[...] = znext[...]

    @pl.when(p == npair - 1)
    def _final():
        pk_ref[...] = kvlast[:, 0:KV_W]
        pv_ref[...] = kvlast[:, KV_W:2 * KV_W]
        pconv_ref[...] = cbuf[0:8, :]
        pm_ref[...] = mst[...]
        for h in range(M_HEADS):
            pct_ref[h] = ctn[h].T


def _sample_kernel(xs_ref, ck_ref, cv_ref, sconv_ref, sc_ref, sn_ref, sm_ref,
                   w_in_ref, wconv_ref, bconv_ref, wqk_ref, gb_ref, sinks_ref,
                   gattn_ref, gml_ref,
                   ymix_ref, sk_ref, sv_ref, scv_ref, sco_ref, sno_ref, smo_ref, cext):
    ns, L = SEQ_TILE, 8
    xs = xs_ref[...].reshape(ns * L, D_MODEL)
    z = _dot(xs.astype(BF16), w_in_ref[...])

    lane = lax.broadcasted_iota(jnp.int32, (BLOCK, 128), 1)
    lo_half = lane < HEAD_DIM
    lo3 = lax.broadcasted_iota(jnp.int32, (ns, L, 128), 2) < HEAD_DIM
    ti = lax.broadcasted_iota(jnp.int32, (BLOCK, BLOCK), 0)
    si = lax.broadcasted_iota(jnp.int32, (BLOCK, BLOCK), 1)
    same_seq = (ti >> 3) == (si >> 3)
    tl, sl_ = ti & 7, si & 7
    causal_seq = same_seq & (sl_ <= tl)

    k_new = z[:, ATT_W:ATT_W + KV_W]
    v_new = z[:, ATT_W + KV_W:ATT_W + 2 * KV_W]
    kb, vb = k_new.astype(BF16), v_new.astype(BF16)
    zero_b = jnp.zeros_like(kb)
    kn_mask = jnp.concatenate([jnp.where(lo_half, kb, zero_b), jnp.where(lo_half, zero_b, kb)], axis=0)
    vn_mask = jnp.concatenate([jnp.where(lo_half, vb, zero_b), jnp.where(lo_half, zero_b, vb)], axis=0)
    qg = [z[:, r * 128:(r + 1) * 128] for r in range(4)]
    s_new = _dot_nt(jnp.concatenate(qg, axis=0).astype(BF16), kn_mask)
    qg3 = [q.reshape(ns, L, 128) for q in qg]
    q8 = jnp.concatenate([jnp.where(lo3, q, 0.0) for q in qg3] + [jnp.where(lo3, 0.0, q) for q in qg3],
                         axis=1).astype(BF16)
    s_cache = jnp.einsum('nqd,nkd->nqk', q8, ck_ref[...].astype(BF16), preferred_element_type=F32)

    mask_c = si > tl
    dist_c = (WINDOW + tl - si).astype(F32)
    dist_n = (tl - sl_).astype(F32)
    pc_parts, pn_rows, inv = {}, [], {}
    for r in range(4):
        pn = []
        for c in range(2):
            h = r + 4 * c
            sc_c = s_cache[:, c * 32 + r * L:c * 32 + (r + 1) * L, :].reshape(BLOCK, 128)
            sc_n = s_new[r * 128:(r + 1) * 128, c * 128:(c + 1) * 128]
            (p_c, p_n), inv[h] = _softmax_parts(
                [(sc_c, mask_c, -ALIBI_SLOPES[h] * dist_c), (sc_n, causal_seq, -ALIBI_SLOPES[h] * dist_n)],
                sinks_ref[h])
            pc_parts[(c, r)] = p_c.reshape(ns, L, 128)
            pn.append(p_n.astype(BF16))
        pn_rows.append(jnp.concatenate(pn, axis=1))
    p8 = jnp.concatenate([pc_parts[(c, r)] for c in range(2) for r in range(4)], axis=1).astype(BF16)
    o_cache = jnp.einsum('nqk,nkd->nqd', p8, cv_ref[...].astype(BF16), preferred_element_type=F32)
    o_new = _dot(jnp.concatenate(pn_rows, axis=0), vn_mask)
    groups = []
    for r in range(4):
        oa = o_cache[:, r * L:(r + 1) * L, :].reshape(BLOCK, 128)
        ob = o_cache[:, 32 + r * L:32 + (r + 1) * L, :].reshape(BLOCK, 128)
        o = jnp.where(lo_half, oa, ob) + o_new[r * 128:(r + 1) * 128]
        groups.append(o * jnp.where(lo_half, inv[r], inv[r + 4]))
    y_att = _attn_head_norm(jnp.concatenate(groups, axis=1), gattn_ref[...])

    sk_ref[:, 0:WINDOW - L, :] = ck_ref[:, L:WINDOW, :]
    sk_ref[:, WINDOW - L:WINDOW, :] = k_new.reshape(ns, L, 128)
    sv_ref[:, 0:WINDOW - L, :] = cv_ref[:, L:WINDOW, :]
    sv_ref[:, WINDOW - L:WINDOW, :] = v_new.reshape(ns, L, 128)

    c3 = z[:, 768:1280].reshape(ns, L, MLSTM_W)
    cext[:, 0:8, :] = sconv_ref[...]
    cext[:, 8:16, :] = c3
    scv_ref[...] = c3
    c_act = _conv_silu(lambda jj: cext[:, 5 + jj:5 + jj + L, :], wconv_ref, bconv_ref)
    c_act = c_act.reshape(BLOCK, MLSTM_W)

    lblk = jnp.where(causal_seq, 1.0, 0.0).astype(BF16)
    lseq = jnp.where(same_seq, 1.0, 0.0).astype(BF16)
    xg, blast = _gate_columns(z[:, IN_MAIN:IN_PAD], gb_ref, None, lblk, extra=lseq)
    xgt = xg.T
    m_rep = jnp.broadcast_to(sm_ref[...], (ns, L, 128)).reshape(BLOCK, 128)
    lane3 = lax.broadcasted_iota(jnp.int32, (ns, 1, 128), 2)
    m_out = jnp.zeros((ns, 1, 128), F32)

    y_m = []
    for h in range(M_HEADS):
        sl = slice(h * 128, (h + 1) * 128)
        qk = _dot(c_act[:, sl].astype(BF16), wqk_ref[h])
        qm = qk[:, :128]
        qm_b = qm.astype(BF16)
        km = qk[:, 128:] * (M_HEAD_DIM ** -0.5)
        km_b = km.astype(BF16)
        vh = z[:, 1280 + h * 128:1280 + (h + 1) * 128]
        b_c, li_c = xg[:, 4 + h:5 + h], xg[:, h:h + 1]
        b_r, li_r = xgt[4 + h:5 + h, :], xgt[h:h + 1, :]
        bl_c = blast[:, 4 + h:5 + h]
        m_prev = m_rep[:, h:h + 1]
        m_t, w_inter, num, den = _mlstm_intra(qm_b, km_b, vh.astype(BF16), causal_seq, b_c, b_r, li_r, m_prev)

        c_old = sc_ref[:, h]
        cq = _dot_nt(qm_b, c_old.reshape(ns * 128, 128).astype(BF16))
        num_inter = jnp.concatenate([cq[g * L:(g + 1) * L, g * 128:(g + 1) * 128] for g in range(ns)], axis=0)
        n_old = sn_ref[:, h:h + 1, :]
        n_rep = jnp.broadcast_to(n_old, (ns, L, 128)).reshape(BLOCK, 128)
        num = num + w_inter * num_inter
        den = den + w_inter * jnp.sum(qm * n_rep, axis=1, keepdims=True)
        hh = num / jnp.maximum(jnp.abs(den), jnp.exp(-m_t))

        m_end3 = jnp.broadcast_to(m_t, (BLOCK, 128)).reshape(ns, L, 128)[:, L - 1:L, :]
        m_end = jnp.broadcast_to(m_end3, (ns, L, 128)).reshape(BLOCK, 128)[:, 0:1]
        decay = jnp.exp(bl_c + m_prev - m_end)
        wk = jnp.exp(bl_c - b_c + li_c - m_end)
        k_big = jnp.concatenate(
            [jnp.where((ti >> 3) == g, km_b, jnp.zeros_like(km_b)) for g in range(ns)], axis=1)
        c_up = _dot_tn((vh * wk).astype(BF16), k_big)
        c_up3 = jnp.stack([c_up[:, g * 128:(g + 1) * 128] for g in range(ns)], axis=0)
        decay3 = jnp.broadcast_to(decay, (BLOCK, 128)).reshape(ns, L, 128)[:, 0:1, :]
        sco_ref[:, h] = decay3 * c_old + c_up3
        sno_ref[:, h:h + 1, :] = decay3 * n_old + jnp.sum((km * wk).reshape(ns, L, 128), axis=1, keepdims=True)
        m_out = jnp.where(lane3 == h, m_end3, m_out)

        hm = _sigmoid(z[:, 1792 + h * 128:1792 + (h + 1) * 128]) * hh
        y_m.append(_mlstm_head_norm(hm, gml_ref[:, sl]))

    smo_ref[...] = m_out
    ymix_ref[...] = jnp.concatenate([y_att] + y_m, axis=1).astype(BF16)


def _post_kernel(x_ref, y_ref, wout_ref, ln1g_ref, ln1b_ref, wg_ref, wu_ref, wd_ref, ln2g_ref, ln2b_ref, o_ref):
    o_ref[...] = _FfnStages(x_ref[...], y_ref[...], wout_ref, ln1g_ref, ln1b_ref, wg_ref, wu_ref, wd_ref,
                            ln2g_ref, ln2b_ref).run()


def _const_spec(shape):
    nd = len(shape)
    return pl.BlockSpec(shape, lambda *_: (0,) * nd, pipeline_mode=pl.Buffered(1))


def _mixer_weight_specs(qk_specs):
    return [
        _const_spec((D_MODEL, IN_PAD)),
        _const_spec((CONV_W, MLSTM_W)),
        _const_spec((1, MLSTM_W)),
    ] + qk_specs + [
        _const_spec((1, 128)),
        pl.BlockSpec(memory_space=pltpu.SMEM),
        _const_spec((1, ATT_W)),
        _const_spec((1, MLSTM_W)),
    ]


def _post_weight_specs(d):
    return [
        _const_spec((d, d)),
        _const_spec((1, d)), _const_spec((1, d)),
        _const_spec((d, D_FF)), _const_spec((d, D_FF)), _const_spec((D_FF, d)),
        _const_spec((1, d)), _const_spec((1, d)),
    ]


def _prompt_layer(x_prompt, meta_blk, mixer_w, post_w):
    b, s, d = x_prompt.shape
    qk_specs = [_const_spec((M_HEADS, M_HEAD_DIM, M_HEAD_DIM))] * 2
    state_shapes = [((BLOCK, KV_W), BF16)] * 4 + [
        ((8, MLSTM_W), F32), ((M_HEADS, M_HEAD_DIM, 2 * M_HEAD_DIM), F32), ((8, 128), F32)]
    meta_state = pl.pallas_call(
        _meta_kernel,
        grid=(1,),
        in_specs=[_const_spec((BLOCK, d))] + _mixer_weight_specs(qk_specs),
        out_specs=tuple(_const_spec(shape) for shape, _ in state_shapes),
        out_shape=tuple(jax.ShapeDtypeStruct(shape, dt) for shape, dt in state_shapes),
        scratch_shapes=[pltpu.VMEM((BLOCK, IN_PAD), F32), pltpu.VMEM((BLOCK + 8, MLSTM_W), F32)],
        compiler_params=pltpu.CompilerParams(
            dimension_semantics=("arbitrary",), vmem_limit_bytes=VMEM_LIMIT),
        name="meta_state",
    )(meta_blk, *mixer_w)

    rows = 2 * BLOCK
    npair = s // rows
    total = b * npair
    steps = total + 2

    def block_map(lag):
        def index(t):
            tb = jnp.clip(t - lag, 0, total - 1)
            return (tb // npair, tb % npair, 0)
        return index

    def smap(t):
        return (jnp.clip(t - 1, 0, total - 1) // npair, 0, 0)

    out_shape = (
        jax.ShapeDtypeStruct((b, s, d), F32),
        jax.ShapeDtypeStruct((b, BLOCK, KV_W), F32),
        jax.ShapeDtypeStruct((b, BLOCK, KV_W), F32),
        jax.ShapeDtypeStruct((b, 8, MLSTM_W), F32),
        jax.ShapeDtypeStruct((b, M_HEADS, 2 * M_HEAD_DIM, M_HEAD_DIM), F32),
        jax.ShapeDtypeStruct((b, 8, 128), F32),
    )
    out_specs = (
        pl.BlockSpec((None, rows, d), block_map(2)),
        pl.BlockSpec((None, BLOCK, KV_W), smap),
        pl.BlockSpec((None, BLOCK, KV_W), smap),
        pl.BlockSpec((None, 8, MLSTM_W), smap),
        pl.BlockSpec((None, M_HEADS, 2 * M_HEAD_DIM, M_HEAD_DIM), lambda t: smap(t) + (0,)),
        pl.BlockSpec((None, 8, 128), smap),
    )
    return pl.pallas_call(
        functools.partial(_pair_kernel, npair=npair),
        grid=(steps,),
        in_specs=[pl.BlockSpec((None, rows, d), block_map(0)), pl.BlockSpec((None, rows, d), block_map(2))]
        + [_const_spec(shape) for shape, _ in state_shapes]
        + _mixer_weight_specs(qk_specs) + _post_weight_specs(d),
        out_specs=out_specs,
        out_shape=out_shape,
        scratch_shapes=[
            pltpu.VMEM((rows, IN_PAD), F32),
            pltpu.VMEM((rows, IN_PAD), F32),
            pltpu.VMEM((BLOCK, 2 * KV_W), F32),
        ] + [pltpu.VMEM((BLOCK, KV_W), BF16)] * 4 + [
            pltpu.VMEM((rows + 8, MLSTM_W), F32),
            pltpu.VMEM((M_HEADS, M_HEAD_DIM, 2 * M_HEAD_DIM), F32),
            pltpu.VMEM((8, 128), F32),
            pltpu.VMEM((rows, d), BF16),
        ],
        compiler_params=pltpu.CompilerParams(
            dimension_semantics=("arbitrary",), vmem_limit_bytes=VMEM_LIMIT),
        name="prompt_layer",
    )(x_prompt, x_prompt, *meta_state, *mixer_w, *post_w)


def _sample_mixer(x_sample, ck, cv, sconv8, s_c, s_n, s_m3, mixer_w):
    n, l, d = x_sample.shape
    t = SEQ_TILE
    m3 = lambda i: (i, 0, 0)
    m4 = lambda i: (i, 0, 0, 0)
    out_shape = (
        jax.ShapeDtypeStruct((n * l, d), BF16),
        jax.ShapeDtypeStruct((n, WINDOW, KV_W), F32),
        jax.ShapeDtypeStruct((n, WINDOW, KV_W), F32),
        jax.ShapeDtypeStruct((n, l, MLSTM_W), F32),
        jax.ShapeDtypeStruct((n, M_HEADS, M_HEAD_DIM, M_HEAD_DIM), F32),
        jax.ShapeDtypeStruct((n, M_HEADS, M_HEAD_DIM), F32),
        jax.ShapeDtypeStruct((n, 1, 128), F32),
    )
    out_specs = (
        pl.BlockSpec((t * l, d), lambda i: (i, 0)),
        pl.BlockSpec((t, WINDOW, KV_W), m3),
        pl.BlockSpec((t, WINDOW, KV_W), m3),
        pl.BlockSpec((t, l, MLSTM_W), m3),
        pl.BlockSpec((t, M_HEADS, M_HEAD_DIM, M_HEAD_DIM), m4),
        pl.BlockSpec((t, M_HEADS, M_HEAD_DIM), m3),
        pl.BlockSpec((t, 1, 128), m3),
    )
    in_specs = [
        pl.BlockSpec((t, l, d), m3),
        pl.BlockSpec((t, WINDOW, KV_W), m3),
        pl.BlockSpec((t, WINDOW, KV_W), m3),
        pl.BlockSpec((t, 8, MLSTM_W), m3),
        pl.BlockSpec((t, M_HEADS, M_HEAD_DIM, M_HEAD_DIM), m4),
        pl.BlockSpec((t, M_HEADS, M_HEAD_DIM), m3),
        pl.BlockSpec((t, 1, 128), m3),
    ] + _mixer_weight_specs([_const_spec((M_HEADS, M_HEAD_DIM, 2 * M_HEAD_DIM))])
    return pl.pallas_call(
        _sample_kernel,
        grid=(n // t,),
        in_specs=in_specs,
        out_specs=out_specs,
        out_shape=out_shape,
        scratch_shapes=[pltpu.VMEM((t, 16, MLSTM_W), F32)],
        compiler_params=pltpu.CompilerParams(
            dimension_semantics=("arbitrary",), vmem_limit_bytes=VMEM_LIMIT),
        name="sample_mixer",
    )(x_sample, ck, cv, sconv8, s_c, s_n, s_m3, *mixer_w)


def _post(x_rows, y_rows, post_w):
    rows, d = x_rows.shape
    tm = min(POST_ROWS, rows)
    rmap = lambda i: (i, 0)
    in_specs = [pl.BlockSpec((tm, d), rmap), pl.BlockSpec((tm, d), rmap)] + _post_weight_specs(d)
    return pl.pallas_call(
        _post_kernel,
        grid=(rows // tm,),
        in_specs=in_specs,
        out_specs=pl.BlockSpec((tm, d), rmap),
        out_shape=jax.ShapeDtypeStruct((rows, d), F32),
        compiler_params=pltpu.CompilerParams(
            dimension_semantics=("arbitrary",), vmem_limit_bytes=VMEM_LIMIT),
        name="post_ffn",
    )(x_rows, y_rows, *post_w)


def kernel(x_prompt, x_sample, cache_k, cache_v, state_conv, state_C, state_n, state_m, meta_tokens,
           w_in, w_conv, b_conv, w_mq, w_mk, b_i, b_f, attn_sinks, g_attn, g_mlstm, w_out,
           ln1_g, ln1_b, w_gate, w_up, w_down, ln2_g, ln2_b):
    b, s, d = x_prompt.shape
    n, l, _ = x_sample.shape

    w0 = w_in[0]
    wq_perm = w0[:, :ATT_W].reshape(d, 2, 4, HEAD_DIM).transpose(0, 2, 1, 3).reshape(d, ATT_W)
    w_in_p = jnp.concatenate(
        [wq_perm, w0[:, ATT_W:], jnp.zeros((d, IN_PAD - IN_MAIN - IN_GATES), w0.dtype)], axis=1).astype(BF16)
    wqk = jnp.concatenate([w_mq[0], w_mk[0]], axis=-1).astype(BF16)
    gb = jnp.concatenate([b_i[0], b_f[0], jnp.zeros((128 - IN_GATES,), F32)]).reshape(1, 128)
    mixer_head = (w_in_p, w_conv[0], b_conv[0].reshape(1, MLSTM_W))
    mixer_tail = (gb, attn_sinks[0], g_attn[0][_ATT_PERM].reshape(1, ATT_W), g_mlstm[0].reshape(1, MLSTM_W))
    wq_b = w_mq[0].astype(BF16)
    wkt_b = jnp.swapaxes(w_mk[0], 1, 2).astype(BF16)
    w_out_p = jnp.concatenate([w_out[0][:ATT_W][_ATT_PERM], w_out[0][ATT_W:]], axis=0).astype(BF16)
    post_w = (w_out_p, ln1_g[0].reshape(1, d), ln1_b[0].reshape(1, d),
              w_gate[0].astype(BF16), w_up[0].astype(BF16), w_down[0].astype(BF16),
              ln2_g[0].reshape(1, d), ln2_b[0].reshape(1, d))

    meta_blk = jnp.concatenate([jnp.zeros((META_PAD, d), x_prompt.dtype), meta_tokens.astype(x_prompt.dtype)], axis=0)
    y_prompt, pk, pv, pconv8, pct, pm8 = _prompt_layer(
        x_prompt, meta_blk, mixer_head + (wq_b, wkt_b) + mixer_tail, post_w)

    mixer_w = mixer_head + (wqk,) + mixer_tail
    sconv8 = jnp.pad(state_conv[0], ((0, 0), (8 - (CONV_W - 1), 0), (0, 0)))
    s_m3 = jnp.pad(state_m[0], ((0, 0), (0, 128 - M_HEADS))).reshape(n, 1, 128)
    ymix_s, sk, sv, scv, s_c, s_n, s_mo = _sample_mixer(
        x_sample, cache_k[0].reshape(n, WINDOW, KV_W), cache_v[0].reshape(n, WINDOW, KV_W),
        sconv8, state_C[0], state_n[0], s_m3, mixer_w)

    y_sample = _post(x_sample.reshape(n * l, d), ymix_s, post_w).reshape(n, l, d)

    kv5 = lambda a: a.reshape(1, a.shape[0], WINDOW, 2, HEAD_DIM)
    return (y_prompt, y_sample,
            kv5(pk), kv5(pv), pconv8[None, :, 8 - (CONV_W - 1):, :], pct[None, :, :, :M_HEAD_DIM, :],
            pct[None, :, :, M_HEAD_DIM, :], pm8[None, :, :M_HEADS, 0],
            kv5(sk), kv5(sv), scv[None, :, l - (CONV_W - 1):, :], s_c[None], s_n[None],
            s_mo[None, :, 0, :M_HEADS])
```

```python
import functools

import jax
import jax.numpy as jnp
from jax import lax
from jax.experimental import pallas as pl
from jax.experimental.pallas import tpu as pltpu

F32 = jnp.float32
BF16 = jnp.bfloat16

D_MODEL = 1024
ATT_W = 512
MLSTM_W = 512
HEAD_DIM = 64
N_HEADS = 8
KV_W = 128
WINDOW = 128
BLOCK = 128
M_HEADS = 4
M_HEAD_DIM = 128
CONV_W = 4
N_META = 16
META_PAD = BLOCK - N_META
D_FF = 2816
IN_MAIN = 2304
IN_GATES = 2 * M_HEADS
IN_PAD = IN_MAIN + 128
DEPTH = 1
ALIBI_SLOPES = tuple(2.0 ** (-8.0 * (h + 1) / N_HEADS) for h in range(N_HEADS))
DEEPNORM_ALPHA = (2.0 * DEPTH) ** 0.25
EPS = 1e-5
SEQ_TILE = 16
POST_ROWS = 256
STEP_BLOCKS = 2
FF_CHUNK = 256
FF_LOOKAHEAD = 2
PROJ_CHUNK = 256
VMEM_LIMIT = 56 * 1024 * 1024


def _pair_heads(a, axis):
    shape = a.shape
    a = a.reshape(shape[:axis] + (2, 4, HEAD_DIM) + shape[axis + 1:])
    return jnp.swapaxes(a, axis, axis + 1).reshape(shape)


def _dot(a, b):
    return jnp.dot(a, b, preferred_element_type=F32)


def _dot_nt(a, b):
    return lax.dot_general(a, b, (((1,), (1,)), ((), ())), preferred_element_type=F32)


def _dot_tn(a, b):
    return lax.dot_general(a, b, (((0,), (0,)), ((), ())), preferred_element_type=F32)


def _split2(x):
    hi = x.astype(BF16)
    lo = (x - hi.astype(F32)).astype(BF16)
    return hi, lo


def _sigmoid(x):
    return 1.0 / (1.0 + jnp.exp(-x))


def _log_sigmoid(x):
    return -(jnp.maximum(-x, 0.0) + jnp.log1p(jnp.exp(-jnp.abs(x))))


def _layer_norm(x, g, b):
    mu = jnp.mean(x, axis=-1, keepdims=True)
    xc = x - mu
    var = jnp.mean(xc * xc, axis=-1, keepdims=True)
    return xc * lax.rsqrt(var + EPS) * g + b


def _attn_head_norm(att, g):
    lo_half = lax.broadcasted_iota(jnp.int32, (att.shape[0], 128), 1) < HEAD_DIM

    def seg_mean(x):
        s_all = jnp.sum(x, axis=1, keepdims=True)
        s_lo = jnp.sum(jnp.where(lo_half, x, 0.0), axis=1, keepdims=True)
        return jnp.where(lo_half, s_lo, s_all - s_lo) * (1.0 / HEAD_DIM)

    out = []
    for grp in range(att.shape[1] // 128):
        x = att[:, grp * 128:(grp + 1) * 128]
        xc = x - seg_mean(x)
        out.append(xc * lax.rsqrt(seg_mean(xc * xc) + EPS))
    return jnp.concatenate(out, axis=1) * g


def _mlstm_head_norm(hm, g):
    mu = jnp.mean(hm, axis=-1, keepdims=True)
    xc = hm - mu
    var = jnp.mean(xc * xc, axis=-1, keepdims=True)
    return xc * lax.rsqrt(var + EPS) * g


def _softmax_parts(parts, sink):
    sp = [jnp.where(m, s * (HEAD_DIM ** -0.5) + a, -jnp.inf) for s, m, a in parts]
    mx = sink
    for s in sp:
        mx = jnp.maximum(mx, jnp.max(s, axis=1, keepdims=True))
    ps = [jnp.exp(s - mx) for s in sp]
    den = jnp.exp(sink - mx)
    for p in ps:
        den = den + jnp.sum(p, axis=1, keepdims=True)
    return ps, 1.0 / den


def _mlstm_intra(qm_b, km_b, vh_b, mask, b_c, b_r, li_r, m_prev):
    dmat = jnp.where(mask, b_c - b_r + li_r, -jnp.inf)
    m_inter = b_c + m_prev
    m_t = jnp.maximum(m_inter, jnp.max(dmat, axis=1, keepdims=True))
    w_inter = jnp.exp(m_inter - m_t)
    sc = _dot_nt(qm_b, km_b) * jnp.exp(dmat - m_t)
    num = _dot(sc.astype(BF16), vh_b)
    den = jnp.sum(sc, axis=1, keepdims=True)
    return m_t, w_inter, num, den


def _conv_silu(window, wconv_ref, bconv_ref):
    acc = bconv_ref[...]
    for j in range(CONV_W):
        acc = acc + window(j) * wconv_ref[j:j + 1, :]
    return acc * _sigmoid(acc)


def _gate_columns(gates, gb_ref, row_valid, ltri, extra=None):
    lane = lax.broadcasted_iota(jnp.int32, gates.shape, 1)
    gb = gates + gb_ref[...]
    is_i = lane < M_HEADS
    is_f = (lane >= M_HEADS) & (lane < 2 * M_HEADS)
    logf = _log_sigmoid(gb)
    if row_valid is not None:
        lf = jnp.where(is_f & row_valid, logf, 0.0)
        li = jnp.where(row_valid, gb, -jnp.inf)
    else:
        lf = jnp.where(is_f, logf, 0.0)
        li = gb
    hi, lo = _split2(lf)
    bc = _dot(ltri, hi) + _dot(ltri, lo)
    x = jnp.where(is_i, li, bc)
    if extra is None:
        return x, None
    return x, _dot(extra, hi) + _dot(extra, lo)


class _FfnStages:
    N_CHUNKS = D_FF // FF_CHUNK

    def __init__(self, x, ymix, wout_ref, ln1g_ref, ln1b_ref, wg_ref, wu_ref, wd_ref, ln2g_ref, ln2b_ref):
        self.x, self.ymix = x, ymix
        self.wout_ref, self.ln1g_ref, self.ln1b_ref = wout_ref, ln1g_ref, ln1b_ref
        self.wg_ref, self.wu_ref, self.wd_ref = wg_ref, wu_ref, wd_ref
        self.ln2g_ref, self.ln2b_ref = ln2g_ref, ln2b_ref

    def head(self):
        self.x1 = _layer_norm(DEEPNORM_ALPHA * self.x + _dot(self.ymix, self.wout_ref[...]),
                              self.ln1g_ref[...], self.ln1b_ref[...])
        self.x1b = self.x1.astype(BF16)
        self.acc = DEEPNORM_ALPHA * self.x1

    def _gate_up(self, c):
        cs = slice(c * FF_CHUNK, (c + 1) * FF_CHUNK)
        return _dot(self.x1b, self.wg_ref[:, cs]), _dot(self.x1b, self.wu_ref[:, cs])

    def chunk(self, c):
        if c == 0:
            self.gu = {}
        for n in range(c, min(c + FF_LOOKAHEAD, self.N_CHUNKS - 1) + 1):
            if n not in self.gu:
                self.gu[n] = self._gate_up(n)
        g, u = self.gu.pop(c)
        hid = (g * _sigmoid(g) * u).astype(BF16)
        self.acc = self.acc + _dot(hid, self.wd_ref[c * FF_CHUNK:(c + 1) * FF_CHUNK, :])

    def tail(self):
        return _layer_norm(self.acc, self.ln2g_ref[...], self.ln2b_ref[...])

    def run(self):
        self.head()
        for c in range(self.N_CHUNKS):
            self.chunk(c)
        return self.tail()


def _mix_blocks(zc, nb, pad_rows, thr_first, state, cbuf, wconv_ref, bconv_ref, wq_ref, wkt_ref, gb_ref,
                sinks_ref, gattn_ref, gml_ref, hook):
    rows = nb * BLOCK
    kp_lo, kp_hi, vp_lo, vp_hi, c_state, m_state = state
    c_state, m_state = list(c_state), list(m_state)
    lane = lax.broadcasted_iota(jnp.int32, (BLOCK, 128), 1)
    lo_half = lane < HEAD_DIM
    row_valid = None
    if pad_rows:
        row_valid = lax.broadcasted_iota(jnp.int32, (rows, 1), 0) >= pad_rows

    s_all, vmasks = [], []
    for blk in range(nb):
        rs = slice(blk * BLOCK, (blk + 1) * BLOCK)
        kb = zc[rs, ATT_W:ATT_W + KV_W].astype(BF16)
        vb = zc[rs, ATT_W + KV_W:ATT_W + 2 * KV_W].astype(BF16)
        zero_b = jnp.zeros_like(kb)
        k_lo, k_hi = jnp.where(lo_half, kb, zero_b), jnp.where(lo_half, zero_b, kb)
        v_lo, v_hi = jnp.where(lo_half, vb, zero_b), jnp.where(lo_half, zero_b, vb)
        kmask = jnp.concatenate([kp_lo, k_lo, kp_hi, k_hi], axis=0)
        vmasks.append(jnp.concatenate([vp_lo, v_lo, vp_hi, v_hi], axis=0))
        q4 = jnp.concatenate([zc[rs, r * 128:(r + 1) * 128] for r in range(4)], axis=0).astype(BF16)
        s_all.append(_dot_nt(q4, kmask))
        kp_lo, kp_hi, vp_lo, vp_hi = k_lo, k_hi, v_lo, v_hi
    hook("scores")

    c_in = zc[:, 768:1280]
    if row_valid is not None:
        c_in = jnp.where(row_valid, c_in, 0.0)
    cbuf[8:8 + rows, :] = c_in
    c_act = _conv_silu(lambda jj: cbuf[5 + jj:5 + jj + rows, :], wconv_ref, bconv_ref)
    cbuf[0:8, :] = cbuf[rows:rows + 8, :]
    hs = range(M_HEADS)
    ca_b = [c_act[:, h * 128:(h + 1) * 128].astype(BF16) for h in hs]
    qm_b = [_dot(ca_b[h], wq_ref[h]).astype(BF16) for h in hs]
    kmt_f = [_dot_nt(wkt_ref[h], ca_b[h]) * (M_HEAD_DIM ** -0.5) for h in hs]
    hook("front")

    ti = lax.broadcasted_iota(jnp.int32, (rows, rows), 0)
    si = lax.broadcasted_iota(jnp.int32, (rows, rows), 1)
    ltri = jnp.where((si <= ti) & ((si >> 7) == (ti >> 7)), 1.0, 0.0).astype(BF16)
    xg, _ = _gate_columns(zc[:, IN_MAIN:IN_PAD], gb_ref, row_valid, ltri)
    xgt = xg.T
    hook("gates")

    qi = lax.broadcasted_iota(jnp.int32, (BLOCK, BLOCK), 0)
    ks = lax.broadcasted_iota(jnp.int32, (BLOCK, BLOCK), 1)
    own = ks <= qi
    distf = jnp.where(own, qi - ks, WINDOW + qi - ks).astype(F32)
    zero_p = jnp.zeros((BLOCK, BLOCK), BF16)
    att_rows = []
    for blk in range(nb):
        live = None
        if blk == 0 and thr_first is not None:
            live = jnp.where(own, ks + BLOCK, ks) >= thr_first
        p_rows, inv = [], {}
        for r in range(4):
            pr = []
            for c in range(2):
                h = r + 4 * c
                s_blk = s_all[blk][r * 128:(r + 1) * 128, c * 256:(c + 1) * 256]
                sp = jnp.where(own, s_blk[:, BLOCK:], s_blk[:, :BLOCK]) * (HEAD_DIM ** -0.5) \
                    - ALIBI_SLOPES[h] * distf
                if live is not None:
                    sp = jnp.where(live, sp, -jnp.inf)
                sink = sinks_ref[h]
                mx = jnp.maximum(jnp.max(sp, axis=1, keepdims=True), sink)
                p = jnp.exp(sp - mx)
                inv[h] = 1.0 / (jnp.sum(p, axis=1, keepdims=True) + jnp.exp(sink - mx))
                p_b = p.astype(BF16)
                pr += [jnp.where(own, zero_p, p_b), jnp.where(own, p_b, zero_p)]
            p_rows.append(jnp.concatenate(pr, axis=1))
            hook("softmax", blk * 4 + r)
        o_all = _dot(jnp.concatenate(p_rows, axis=0), vmasks[blk])
        att_rows.append(jnp.concatenate(
            [o_all[r * 128:(r + 1) * 128] * jnp.where(lo_half, inv[r], inv[r + 4]) for r in range(4)], axis=1))
        hook("pv", blk)

    ones_col = jnp.where(lane == 0, 1.0, 0.0).astype(BF16)
    tb = lax.broadcasted_iota(jnp.int32, (BLOCK, BLOCK), 0)
    sb = lax.broadcasted_iota(jnp.int32, (BLOCK, BLOCK), 1)
    tri = sb <= tb
    hh_rows = [[] for _ in hs]
    for blk in range(nb):
        rs = slice(blk * BLOCK, (blk + 1) * BLOCK)
        b_c = [xg[rs, 4 + h:5 + h] for h in hs]
        b_r = [xgt[4 + h:5 + h, rs] for h in hs]
        li_r = [xgt[h:h + 1, rs] for h in hs]
        kmt = [kmt_f[h][:, rs] for h in hs]
        v_ext = [jnp.concatenate([zc[rs, 1280 + h * 128:1280 + (h + 1) * 128].astype(BF16), ones_col], axis=1)
                 for h in hs]
        s_qk = [_dot(qm_b[h][rs], kmt[h].astype(BF16)) for h in hs]
        inter = [_dot(qm_b[h][rs], c_state[h].astype(BF16)) for h in hs]
        hook("mlstm_a", blk)
        dmat = [jnp.where(tri, b_c[h] - b_r[h] + li_r[h], -jnp.inf) for h in hs]
        m_inter = [b_c[h] + m_state[h] for h in hs]
        m_t = [jnp.maximum(m_inter[h], jnp.max(dmat[h], axis=1, keepdims=True)) for h in hs]
        w_inter = [jnp.exp(m_inter[h] - m_t[h]) for h in hs]
        sc_b = [(s_qk[h] * jnp.exp(dmat[h] - m_t[h])).astype(BF16) for h in hs]
        hook("mlstm_b", blk)
        nd = [_dot(sc_b[h], v_ext[h]) + w_inter[h] * inter[h] for h in hs]
        for h in hs:
            m_end = m_t[h][BLOCK - 1:BLOCK, :]
            b_last = b_c[h][BLOCK - 1:BLOCK, :]
            decay = jnp.exp(b_last + m_state[h] - m_end)
            wk_r = jnp.exp(b_last - b_r[h] + li_r[h] - m_end)
            c_state[h] = decay * c_state[h] + _dot((kmt[h] * wk_r).astype(BF16), v_ext[h])
            m_state[h] = m_end
            hh_rows[h].append(nd[h][:, :128] / jnp.maximum(jnp.abs(nd[h][:, 128:129]), jnp.exp(-m_t[h])))
        hook("mlstm_c", blk)

    y_att = _attn_head_norm(jnp.concatenate(att_rows, axis=0), gattn_ref[...])
    hook("attnorm")
    y_m = []
    for h in hs:
        hm = _sigmoid(zc[:, 1792 + h * 128:1792 + (h + 1) * 128]) * jnp.concatenate(hh_rows[h], axis=0)
        y_m.append(_mlstm_head_norm(hm, gml_ref[:, h * 128:(h + 1) * 128]))
        hook("headnorm", h)
    ymix = jnp.concatenate([y_att] + y_m, axis=1).astype(BF16)
    return ymix, (kp_lo, kp_hi, vp_lo, vp_hi, c_state, m_state)


def _meta_kernel(meta_ref, w_in_ref, wconv_ref, bconv_ref, wq_ref, wkt_ref, gb_ref, sinks_ref, gattn_ref,
                 gml_ref, kplo_o, kphi_o, vplo_o, vphi_o, cb_o, ctn_o, mst_o, zc, cbuf):
    zc[...] = _dot(meta_ref[...].astype(BF16), w_in_ref[...])
    cbuf[0:8, :] = jnp.zeros((8, MLSTM_W), F32)
    zb = jnp.zeros((BLOCK, KV_W), BF16)
    state = (zb, zb, zb, zb, [jnp.zeros((M_HEAD_DIM, 2 * M_HEAD_DIM), F32)] * M_HEADS,
             [jnp.zeros((1, 1), F32)] * M_HEADS)
    _, (kp_lo, kp_hi, vp_lo, vp_hi, c_state, m_state) = _mix_blocks(
        zc, 1, META_PAD, BLOCK + META_PAD, state, cbuf, wconv_ref, bconv_ref, wq_ref, wkt_ref, gb_ref,
        sinks_ref, gattn_ref, gml_ref, lambda name, index=0: None)
    kplo_o[...] = kp_lo
    kphi_o[...] = kp_hi
    vplo_o[...] = vp_lo
    vphi_o[...] = vp_hi
    cb_o[...] = cbuf[0:8, :]
    for h in range(M_HEADS):
        ctn_o[h] = c_state[h]
        mst_o[h:h + 1, :] = jnp.broadcast_to(m_state[h], (1, 128))
    mst_o[M_HEADS:8, :] = jnp.zeros((8 - M_HEADS, 128), F32)


def _pair_kernel(xin_ref, xres_ref, kplo_i, kphi_i, vplo_i, vphi_i, cb_i, ctn_i, mst_i,
                 w_in_ref, wconv_ref, bconv_ref, wq_ref, wkt_ref, gb_ref, sinks_ref, gattn_ref, gml_ref,
                 wout_ref, ln1g_ref, ln1b_ref, wg_ref, wu_ref, wd_ref, ln2g_ref, ln2b_ref,
                 y_ref, pk_ref, pv_ref, pconv_ref, pct_ref, pm_ref,
                 zcur, kvlast, kplo, kphi, vplo, vphi, cbuf, ctn, mst, yprev, *, npair):
    t = pl.program_id(0)
    p = lax.rem(t + (npair - 1), npair)

    @pl.when(t == 0)
    def _init_pipeline():
        zcur[...] = jnp.zeros_like(zcur)
        yprev[...] = jnp.zeros_like(yprev)

    @pl.when((p == 0) | (t == 0))
    def _load_meta_state():
        kplo[...] = kplo_i[...]
        kphi[...] = kphi_i[...]
        vplo[...] = vplo_i[...]
        vphi[...] = vphi_i[...]
        cbuf[0:8, :] = cb_i[...]
        ctn[...] = ctn_i[...]
        mst[...] = mst_i[...]

    ffn = _FfnStages(xres_ref[...], yprev[...], wout_ref, ln1g_ref, ln1b_ref, wg_ref, wu_ref, wd_ref,
                     ln2g_ref, ln2b_ref)
    xb_b = xin_ref[...].astype(BF16)

    def proj(*chunks):
        for n in chunks:
            cs = slice(n * PROJ_CHUNK, min((n + 1) * PROJ_CHUNK, IN_PAD))
            zcur[:, cs] = _dot(xb_b, w_in_ref[:, cs])

    assert (ATT_W + 2 * KV_W, IN_MAIN, IN_PAD) == (3 * PROJ_CHUNK, 9 * PROJ_CHUNK, 9 * PROJ_CHUNK + 128)
    ffn_it = iter(range(ffn.N_CHUNKS))

    def fill_ffn(n):
        for c in [c for _, c in zip(range(n), ffn_it)]:
            ffn.chunk(c)

    out = {}

    def hook(name, index=0):
        if name == "scores":
            proj(0, 1, 2)
            ffn.head()
        elif name == "front":
            proj(3, 4)
        elif name == "gates":
            proj(9)
        elif name == "softmax":
            if index % 2 == 1:
                fill_ffn(1)
        elif name in ("pv", "mlstm_b", "mlstm_c"):
            fill_ffn(1)
        elif name == "mlstm_a":
            fill_ffn(1)
            if index == STEP_BLOCKS - 1:
                proj(5, 6)
        elif name == "attnorm":
            fill_ffn(ffn.N_CHUNKS)
            out["y"] = ffn.tail()
        elif name == "headnorm" and index % 2 == 1:
            proj(7 + index // 2)

    state = (kplo[...], kphi[...], vplo[...], vphi[...], [ctn[h] for h in range(M_HEADS)],
             [mst[h:h + 1, 0:1] for h in range(M_HEADS)])
    thr_first = jnp.where(p == 0, META_PAD, 0)
    kvlast[...] = zcur[(STEP_BLOCKS - 1) * BLOCK:STEP_BLOCKS * BLOCK, ATT_W:ATT_W + 2 * KV_W]
    ymix, (kp_lo, kp_hi, vp_lo, vp_hi, c_state, m_state) = _mix_blocks(
        zcur, STEP_BLOCKS, 0, thr_first, state, cbuf, wconv_ref, bconv_ref, wq_ref, wkt_ref, gb_ref,
        sinks_ref, gattn_ref, gml_ref, hook)
    y_ref[...] = out["y"]
    kplo[...] = kp_lo
    kphi[...] = kp_hi
    vplo[...] = vp_lo
    vphi[...] = vp_hi
    for h in range(M_HEADS):
        ctn[h] = c_state[h]
        mst[h:h + 1, :] = jnp.broadcast_to(m_state[h], (1, 128))
    yprev[...] = ymix

    @pl.when(p == npair - 1)
    def _final():
        pk_ref[...] = kvlast[:, 0:KV_W]
        pv_ref[...] = kvlast[:, KV_W:2 * KV_W]
        pconv_ref[...] = cbuf[0:8, :]
        pm_ref[...] = mst[...]
        for h in range(M_HEADS):
            pct_ref[h] = ctn[h].T


def _sample_kernel(xs_ref, ck_ref, cv_ref, sconv_ref, sc_ref, sn_ref, sm_ref,
                   w_in_ref, wconv_ref, bconv_ref, wqk_ref, gb_ref, sinks_ref,
                   gattn_ref, gml_ref,
                   ymix_ref, sk_ref, sv_ref, scv_ref, sco_ref, sno_ref, smo_ref, cext):
    ns, L = SEQ_TILE, 8
    xs = xs_ref[...].reshape(ns * L, D_MODEL)
    z = _dot(xs.astype(BF16), w_in_ref[...])

    lane = lax.broadcasted_iota(jnp.int32, (BLOCK, 128), 1)
    lo_half = lane < HEAD_DIM
    lo3 = lax.broadcasted_iota(jnp.int32, (ns, L, 128), 2) < HEAD_DIM
    ti = lax.broadcasted_iota(jnp.int32, (BLOCK, BLOCK), 0)
    si = lax.broadcasted_iota(jnp.int32, (BLOCK, BLOCK), 1)
    same_seq = (ti >> 3) == (si >> 3)
    tl, sl_ = ti & 7, si & 7
    causal_seq = same_seq & (sl_ <= tl)

    k_new = z[:, ATT_W:ATT_W + KV_W]
    v_new = z[:, ATT_W + KV_W:ATT_W + 2 * KV_W]
    kb, vb = k_new.astype(BF16), v_new.astype(BF16)
    zero_b = jnp.zeros_like(kb)
    kn_mask = jnp.concatenate([jnp.where(lo_half, kb, zero_b), jnp.where(lo_half, zero_b, kb)], axis=0)
    vn_mask = jnp.concatenate([jnp.where(lo_half, vb, zero_b), jnp.where(lo_half, zero_b, vb)], axis=0)
    qg = [z[:, r * 128:(r + 1) * 128] for r in range(4)]
    s_new = _dot_nt(jnp.concatenate(qg, axis=0).astype(BF16), kn_mask)
    qg3 = [q.reshape(ns, L, 128) for q in qg]
    q8 = jnp.concatenate([jnp.where(lo3, q, 0.0) for q in qg3] + [jnp.where(lo3, 0.0, q) for q in qg3],
                         axis=1).astype(BF16)
    s_cache = jnp.einsum('nqd,nkd->nqk', q8, ck_ref[...].astype(BF16), preferred_element_type=F32)

    mask_c = si > tl
    dist_c = (WINDOW + tl - si).astype(F32)
    dist_n = (tl - sl_).astype(F32)
    pc_parts, pn_rows, inv = {}, [], {}
    for r in range(4):
        pn = []
        for c in range(2):
            h = r + 4 * c
            sc_c = s_cache[:, c * 32 + r * L:c * 32 + (r + 1) * L, :].reshape(BLOCK, 128)
            sc_n = s_new[r * 128:(r + 1) * 128, c * 128:(c + 1) * 128]
            (p_c, p_n), inv[h] = _softmax_parts(
                [(sc_c, mask_c, -ALIBI_SLOPES[h] * dist_c), (sc_n, causal_seq, -ALIBI_SLOPES[h] * dist_n)],
                sinks_ref[h])
            pc_parts[(c, r)] = p_c.reshape(ns, L, 128)
            pn.append(p_n.astype(BF16))
        pn_rows.append(jnp.concatenate(pn, axis=1))
    p8 = jnp.concatenate([pc_parts[(c, r)] for c in range(2) for r in range(4)], axis=1).astype(BF16)
    o_cache = jnp.einsum('nqk,nkd->nqd', p8, cv_ref[...].astype(BF16), preferred_element_type=F32)
    o_new = _dot(jnp.concatenate(pn_rows, axis=0), vn_mask)
    groups = []
    for r in range(4):
        oa = o_cache[:, r * L:(r + 1) * L, :].reshape(BLOCK, 128)
        ob = o_cache[:, 32 + r * L:32 + (r + 1) * L, :].reshape(BLOCK, 128)
        o = jnp.where(lo_half, oa, ob) + o_new[r * 128:(r + 1) * 128]
        groups.append(o * jnp.where(lo_half, inv[r], inv[r + 4]))
    y_att = _attn_head_norm(jnp.concatenate(groups, axis=1), gattn_ref[...])

    sk_ref[:, 0:WINDOW - L, :] = ck_ref[:, L:WINDOW, :]
    sk_ref[:, WINDOW - L:WINDOW, :] = k_new.reshape(ns, L, 128)
    sv_ref[:, 0:WINDOW - L, :] = cv_ref[:, L:WINDOW, :]
    sv_ref[:, WINDOW - L:WINDOW, :] = v_new.reshape(ns, L, 128)

    c3 = z[:, 768:1280].reshape(ns, L, MLSTM_W)
    cext[:, 0:8, :] = sconv_ref[...]
    cext[:, 8:16, :] = c3
    scv_ref[...] = c3
    c_act = _conv_silu(lambda jj: cext[:, 5 + jj:5 + jj + L, :], wconv_ref, bconv_ref)
    c_act = c_act.reshape(BLOCK, MLSTM_W)

    lblk = jnp.where(causal_seq, 1.0, 0.0).astype(BF16)
    lseq = jnp.where(same_seq, 1.0, 0.0).astype(BF16)
    xg, blast = _gate_columns(z[:, IN_MAIN:IN_PAD], gb_ref, None, lblk, extra=lseq)
    xgt = xg.T
    m_rep = jnp.broadcast_to(sm_ref[...], (ns, L, 128)).reshape(BLOCK, 128)
    lane3 = lax.broadcasted_iota(jnp.int32, (ns, 1, 128), 2)
    m_out = jnp.zeros((ns, 1, 128), F32)

    y_m = []
    for h in range(M_HEADS):
        sl = slice(h * 128, (h + 1) * 128)
        qk = _dot(c_act[:, sl].astype(BF16), wqk_ref[h])
        qm = qk[:, :128]
        qm_b = qm.astype(BF16)
        km = qk[:, 128:] * (M_HEAD_DIM ** -0.5)
        km_b = km.astype(BF16)
        vh = z[:, 1280 + h * 128:1280 + (h + 1) * 128]
        b_c, li_c = xg[:, 4 + h:5 + h], xg[:, h:h + 1]
        b_r, li_r = xgt[4 + h:5 + h, :], xgt[h:h + 1, :]
        bl_c = blast[:, 4 + h:5 + h]
        m_prev = m_rep[:, h:h + 1]
        m_t, w_inter, num, den = _mlstm_intra(qm_b, km_b, vh.astype(BF16), causal_seq, b_c, b_r, li_r, m_prev)

        c_old = sc_ref[:, h]
        cq = _dot_nt(qm_b, c_old.reshape(ns * 128, 128).astype(BF16))
        num_inter = jnp.concatenate([cq[g * L:(g + 1) * L, g * 128:(g + 1) * 128] for g in range(ns)], axis=0)
        n_old = sn_ref[:, h:h + 1, :]
        n_rep = jnp.broadcast_to(n_old, (ns, L, 128)).reshape(BLOCK, 128)
        num = num + w_inter * num_inter
        den = den + w_inter * jnp.sum(qm * n_rep, axis=1, keepdims=True)
        hh = num / jnp.maximum(jnp.abs(den), jnp.exp(-m_t))

        m_end3 = jnp.broadcast_to(m_t, (BLOCK, 128)).reshape(ns, L, 128)[:, L - 1:L, :]
        m_end = jnp.broadcast_to(m_end3, (ns, L, 128)).reshape(BLOCK, 128)[:, 0:1]
        decay = jnp.exp(bl_c + m_prev - m_end)
        wk = jnp.exp(bl_c - b_c + li_c - m_end)
        k_big = jnp.concatenate(
            [jnp.where((ti >> 3) == g, km_b, jnp.zeros_like(km_b)) for g in range(ns)], axis=1)
        c_up = _dot_tn((vh * wk).astype(BF16), k_big)
        c_up3 = jnp.stack([c_up[:, g * 128:(g + 1) * 128] for g in range(ns)], axis=0)
        decay3 = jnp.broadcast_to(decay, (BLOCK, 128)).reshape(ns, L, 128)[:, 0:1, :]
        sco_ref[:, h] = decay3 * c_old + c_up3
        sno_ref[:, h:h + 1, :] = decay3 * n_old + jnp.sum((km * wk).reshape(ns, L, 128), axis=1, keepdims=True)
        m_out = jnp.where(lane3 == h, m_end3, m_out)

        hm = _sigmoid(z[:, 1792 + h * 128:1792 + (h + 1) * 128]) * hh
        y_m.append(_mlstm_head_norm(hm, gml_ref[:, sl]))

    smo_ref[...] = m_out
    ymix_ref[...] = jnp.concatenate([y_att] + y_m, axis=1).astype(BF16)


def _post_kernel(x_ref, y_ref, wout_ref, ln1g_ref, ln1b_ref, wg_ref, wu_ref, wd_ref, ln2g_ref, ln2b_ref, o_ref):
    o_ref[...] = _FfnStages(x_ref[...], y_ref[...], wout_ref, ln1g_ref, ln1b_ref, wg_ref, wu_ref, wd_ref,
                            ln2g_ref, ln2b_ref).run()


def _const_spec(shape):
    nd = len(shape)
    return pl.BlockSpec(shape, lambda *_: (0,) * nd, pipeline_mode=pl.Buffered(1))


def _mixer_weight_specs(qk_specs):
    return [
        _const_spec((D_MODEL, IN_PAD)),
        _const_spec((CONV_W, MLSTM_W)),
        _const_spec((1, MLSTM_W)),
    ] + qk_specs + [
        _const_spec((1, 128)),
        pl.BlockSpec(memory_space=pltpu.SMEM),
        _const_spec((1, ATT_W)),
        _const_spec((1, MLSTM_W)),
    ]


def _post_weight_specs(d):
    return [
        _const_spec((d, d)),
        _const_spec((1, d)), _const_spec((1, d)),
        _const_spec((d, D_FF)), _const_spec((d, D_FF)), _const_spec((D_FF, d)),
        _const_spec((1, d)), _const_spec((1, d)),
    ]


def _prompt_layer(x_prompt, meta_blk, mixer_w, post_w):
    b, s, d = x_prompt.shape
    qk_specs = [_const_spec((M_HEADS, M_HEAD_DIM, M_HEAD_DIM))] * 2
    state_shapes = [((BLOCK, KV_W), BF16)] * 4 + [
        ((8, MLSTM_W), F32), ((M_HEADS, M_HEAD_DIM, 2 * M_HEAD_DIM), F32), ((8, 128), F32)]
    meta_state = pl.pallas_call(
        _meta_kernel,
        grid=(1,),
        in_specs=[_const_spec((BLOCK, d))] + _mixer_weight_specs(qk_specs),
        out_specs=tuple(_const_spec(shape) for shape, _ in state_shapes),
        out_shape=tuple(jax.ShapeDtypeStruct(shape, dt) for shape, dt in state_shapes),
        scratch_shapes=[pltpu.VMEM((BLOCK, IN_PAD), F32), pltpu.VMEM((BLOCK + 8, MLSTM_W), F32)],
        compiler_params=pltpu.CompilerParams(
            dimension_semantics=("arbitrary",), vmem_limit_bytes=VMEM_LIMIT),
        name="meta_state",
    )(meta_blk, *mixer_w)

    rows = STEP_BLOCKS * BLOCK
    npair = s // rows
    total = b * npair
    steps = total + 2

    def block_map(lag):
        def index(t):
            tb = jnp.clip(t - lag, 0, total - 1)
            return (tb // npair, tb % npair, 0)
        return index

    def smap(t):
        return (jnp.clip(t - 1, 0, total - 1) // npair, 0, 0)

    out_shape = (
        jax.ShapeDtypeStruct((b, s, d), F32),
        jax.ShapeDtypeStruct((b, BLOCK, KV_W), F32),
        jax.ShapeDtypeStruct((b, BLOCK, KV_W), F32),
        jax.ShapeDtypeStruct((b, 8, MLSTM_W), F32),
        jax.ShapeDtypeStruct((b, M_HEADS, 2 * M_HEAD_DIM, M_HEAD_DIM), F32),
        jax.ShapeDtypeStruct((b, 8, 128), F32),
    )
    out_specs = (
        pl.BlockSpec((None, rows, d), block_map(2)),
        pl.BlockSpec((None, BLOCK, KV_W), smap),
        pl.BlockSpec((None, BLOCK, KV_W), smap),
        pl.BlockSpec((None, 8, MLSTM_W), smap),
        pl.BlockSpec((None, M_HEADS, 2 * M_HEAD_DIM, M_HEAD_DIM), lambda t: smap(t) + (0,)),
        pl.BlockSpec((None, 8, 128), smap),
    )
    return pl.pallas_call(
        functools.partial(_pair_kernel, npair=npair),
        grid=(steps,),
        in_specs=[pl.BlockSpec((None, rows, d), block_map(0)), pl.BlockSpec((None, rows, d), block_map(2))]
        + [_const_spec(shape) for shape, _ in state_shapes]
        + _mixer_weight_specs(qk_specs) + _post_weight_specs(d),
        out_specs=out_specs,
        out_shape=out_shape,
        scratch_shapes=[
            pltpu.VMEM((rows, IN_PAD), F32),
            pltpu.VMEM((BLOCK, 2 * KV_W), F32),
        ] + [pltpu.VMEM((BLOCK, KV_W), BF16)] * 4 + [
            pltpu.VMEM((rows + 8, MLSTM_W), F32),
            pltpu.VMEM((M_HEADS, M_HEAD_DIM, 2 * M_HEAD_DIM), F32),
            pltpu.VMEM((8, 128), F32),
            pltpu.VMEM((rows, d), BF16),
        ],
        compiler_params=pltpu.CompilerParams(
            dimension_semantics=("arbitrary",), vmem_limit_bytes=VMEM_LIMIT),
        name="prompt_layer",
    )(x_prompt, x_prompt, *meta_state, *mixer_w, *post_w)


def _sample_mixer(x_sample, ck, cv, sconv8, s_c, s_n, s_m3, mixer_w):
    n, l, d = x_sample.shape
    t = SEQ_TILE
    m3 = lambda i: (i, 0, 0)
    m4 = lambda i: (i, 0, 0, 0)
    out_shape = (
        jax.ShapeDtypeStruct((n * l, d), BF16),
        jax.ShapeDtypeStruct((n, WINDOW, KV_W), F32),
        jax.ShapeDtypeStruct((n, WINDOW, KV_W), F32),
        jax.ShapeDtypeStruct((n, l, MLSTM_W), F32),
        jax.ShapeDtypeStruct((n, M_HEADS, M_HEAD_DIM, M_HEAD_DIM), F32),
        jax.ShapeDtypeStruct((n, M_HEADS, M_HEAD_DIM), F32),
        jax.ShapeDtypeStruct((n, 1, 128), F32),
    )
    out_specs = (
        pl.BlockSpec((t * l, d), lambda i: (i, 0)),
        pl.BlockSpec((t, WINDOW, KV_W), m3),
        pl.BlockSpec((t, WINDOW, KV_W), m3),
        pl.BlockSpec((t, l, MLSTM_W), m3),
        pl.BlockSpec((t, M_HEADS, M_HEAD_DIM, M_HEAD_DIM), m4),
        pl.BlockSpec((t, M_HEADS, M_HEAD_DIM), m3),
        pl.BlockSpec((t, 1, 128), m3),
    )
    in_specs = [
        pl.BlockSpec((t, l, d), m3),
        pl.BlockSpec((t, WINDOW, KV_W), m3),
        pl.BlockSpec((t, WINDOW, KV_W), m3),
        pl.BlockSpec((t, 8, MLSTM_W), m3),
        pl.BlockSpec((t, M_HEADS, M_HEAD_DIM, M_HEAD_DIM), m4),
        pl.BlockSpec((t, M_HEADS, M_HEAD_DIM), m3),
        pl.BlockSpec((t, 1, 128), m3),
    ] + _mixer_weight_specs([_const_spec((M_HEADS, M_HEAD_DIM, 2 * M_HEAD_DIM))])
    return pl.pallas_call(
        _sample_kernel,
        grid=(n // t,),
        in_specs=in_specs,
        out_specs=out_specs,
        out_shape=out_shape,
        scratch_shapes=[pltpu.VMEM((t, 16, MLSTM_W), F32)],
        compiler_params=pltpu.CompilerParams(
            dimension_semantics=("arbitrary",), vmem_limit_bytes=VMEM_LIMIT),
        name="sample_mixer",
    )(x_sample, ck, cv, sconv8, s_c, s_n, s_m3, *mixer_w)


def _post(x_rows, y_rows, post_w):
    rows, d = x_rows.shape
    tm = min(POST_ROWS, rows)
    rmap = lambda i: (i, 0)
    in_specs = [pl.BlockSpec((tm, d), rmap), pl.BlockSpec((tm, d), rmap)] + _post_weight_specs(d)
    return pl.pallas_call(
        _post_kernel,
        grid=(rows // tm,),
        in_specs=in_specs,
        out_specs=pl.BlockSpec((tm, d), rmap),
        out_shape=jax.ShapeDtypeStruct((rows, d), F32),
        compiler_params=pltpu.CompilerParams(
            dimension_semantics=("arbitrary",), vmem_limit_bytes=VMEM_LIMIT),
        name="post_ffn",
    )(x_rows, y_rows, *post_w)


def kernel(x_prompt, x_sample, cache_k, cache_v, state_conv, state_C, state_n, state_m, meta_tokens,
           w_in, w_conv, b_conv, w_mq, w_mk, b_i, b_f, attn_sinks, g_attn, g_mlstm, w_out,
           ln1_g, ln1_b, w_gate, w_up, w_down, ln2_g, ln2_b):
    b, s, d = x_prompt.shape
    n, l, _ = x_sample.shape

    w0 = w_in[0]
    w_in_p = jnp.concatenate(
        [_pair_heads(w0[:, :ATT_W], 1), w0[:, ATT_W:], jnp.zeros((d, IN_PAD - IN_MAIN - IN_GATES), w0.dtype)],
        axis=1).astype(BF16)
    wqk = jnp.concatenate([w_mq[0], w_mk[0]], axis=-1).astype(BF16)
    gb = jnp.concatenate([b_i[0], b_f[0], jnp.zeros((128 - IN_GATES,), F32)]).reshape(1, 128)
    mixer_head = (w_in_p, w_conv[0], b_conv[0].reshape(1, MLSTM_W))
    mixer_tail = (gb, attn_sinks[0], _pair_heads(g_attn[0].reshape(1, ATT_W), 1), g_mlstm[0].reshape(1, MLSTM_W))
    wq_b = w_mq[0].astype(BF16)
    wkt_b = jnp.swapaxes(w_mk[0], 1, 2).astype(BF16)
    w_out_p = jnp.concatenate([_pair_heads(w_out[0][:ATT_W], 0), w_out[0][ATT_W:]], axis=0).astype(BF16)
    post_w = (w_out_p, ln1_g[0].reshape(1, d), ln1_b[0].reshape(1, d),
              w_gate[0].astype(BF16), w_up[0].astype(BF16), w_down[0].astype(BF16),
              ln2_g[0].reshape(1, d), ln2_b[0].reshape(1, d))

    meta_blk = jnp.concatenate([jnp.zeros((META_PAD, d), x_prompt.dtype), meta_tokens.astype(x_prompt.dtype)], axis=0)
    y_prompt, pk, pv, pconv8, pct, pm8 = _prompt_layer(
        x_prompt, meta_blk, mixer_head + (wq_b, wkt_b) + mixer_tail, post_w)

    mixer_w = mixer_head + (wqk,) + mixer_tail
    sconv8 = jnp.pad(state_conv[0], ((0, 0), (8 - (CONV_W - 1), 0), (0, 0)))
    s_m3 = jnp.pad(state_m[0], ((0, 0), (0, 128 - M_HEADS))).reshape(n, 1, 128)
    ymix_s, sk, sv, scv, s_c, s_n, s_mo = _sample_mixer(
        x_sample, cache_k[0].reshape(n, WINDOW, KV_W), cache_v[0].reshape(n, WINDOW, KV_W),
        sconv8, state_C[0], state_n[0], s_m3, mixer_w)

    y_sample = _post(x_sample.reshape(n * l, d), ymix_s, post_w).reshape(n, l, d)

    kv5 = lambda a: a.reshape(1, a.shape[0], WINDOW, 2, HEAD_DIM)
    return (y_prompt, y_sample,
            kv5(pk), kv5(pv), pconv8[None, :, 8 - (CONV_W - 1):, :], pct[None, :, :, :M_HEAD_DIM, :],
            pct[None, :, :, M_HEAD_DIM, :], pm8[None, :, :M_HEADS, 0],
            kv5(sk), kv5(sv), scv[None, :, l - (CONV_W - 1):, :], s_c[None], s_n[None],
            s_mo[None, :, 0, :M_HEADS])
```

```python
import functools

import jax
import jax.numpy as jnp
from jax import lax
from jax.experimental import pallas as pl
from jax.experimental.pallas import tpu as pltpu

F32 = jnp.float32
BF16 = jnp.bfloat16

D_MODEL = 1024
ATT_W = 512
MLSTM_W = 512
HEAD_DIM = 64
N_HEADS = 8
KV_W = 128
WINDOW = 128
BLOCK = 128
M_HEADS = 4
M_HEAD_DIM = 128
CONV_W = 4
N_META = 16
META_PAD = BLOCK - N_META
D_FF = 2816
IN_MAIN = 2304
IN_GATES = 2 * M_HEADS
IN_PAD = IN_MAIN + 128
DEPTH = 1
ALIBI_SLOPES = tuple(2.0 ** (-8.0 * (h + 1) / N_HEADS) for h in range(N_HEADS))
DEEPNORM_ALPHA = (2.0 * DEPTH) ** 0.25
EPS = 1e-5
SEQ_TILE = 16
POST_ROWS = 512
STEP_BLOCKS = 2
FF_CHUNK = 256
FF_LOOKAHEAD = 2
PROJ_CHUNK = 256
VMEM_LIMIT = 56 * 1024 * 1024


def _pair_heads(a, axis):
    shape = a.shape
    a = a.reshape(shape[:axis] + (2, 4, HEAD_DIM) + shape[axis + 1:])
    return jnp.swapaxes(a, axis, axis + 1).reshape(shape)


def _dot(a, b):
    return jnp.dot(a, b, preferred_element_type=F32)


def _dot_nt(a, b):
    return lax.dot_general(a, b, (((1,), (1,)), ((), ())), preferred_element_type=F32)


def _dot_tn(a, b):
    return lax.dot_general(a, b, (((0,), (0,)), ((), ())), preferred_element_type=F32)


def _split2(x):
    hi = x.astype(BF16)
    lo = (x - hi.astype(F32)).astype(BF16)
    return hi, lo


def _sigmoid(x):
    return 1.0 / (1.0 + jnp.exp(-x))


def _log_sigmoid(x):
    return -(jnp.maximum(-x, 0.0) + jnp.log1p(jnp.exp(-jnp.abs(x))))


def _layer_norm(x, g, b):
    mu = jnp.mean(x, axis=-1, keepdims=True)
    xc = x - mu
    var = jnp.mean(xc * xc, axis=-1, keepdims=True)
    return xc * lax.rsqrt(var + EPS) * g + b


def _attn_head_norm(att, g):
    lo_half = lax.broadcasted_iota(jnp.int32, (att.shape[0], 128), 1) < HEAD_DIM

    def seg_mean(x):
        s_all = jnp.sum(x, axis=1, keepdims=True)
        s_lo = jnp.sum(jnp.where(lo_half, x, 0.0), axis=1, keepdims=True)
        return jnp.where(lo_half, s_lo, s_all - s_lo) * (1.0 / HEAD_DIM)

    out = []
    for grp in range(att.shape[1] // 128):
        x = att[:, grp * 128:(grp + 1) * 128]
        xc = x - seg_mean(x)
        out.append(xc * lax.rsqrt(seg_mean(xc * xc) + EPS))
    return jnp.concatenate(out, axis=1) * g


def _mlstm_head_norm(hm, g):
    mu = jnp.mean(hm, axis=-1, keepdims=True)
    xc = hm - mu
    var = jnp.mean(xc * xc, axis=-1, keepdims=True)
    return xc * lax.rsqrt(var + EPS) * g


def _softmax_parts(parts, sink):
    sp = [jnp.where(m, s * (HEAD_DIM ** -0.5) + a, -jnp.inf) for s, m, a in parts]
    mx = sink
    for s in sp:
        mx = jnp.maximum(mx, jnp.max(s, axis=1, keepdims=True))
    ps = [jnp.exp(s - mx) for s in sp]
    den = jnp.exp(sink - mx)
    for p in ps:
        den = den + jnp.sum(p, axis=1, keepdims=True)
    return ps, 1.0 / den


def _conv_silu(window, wconv_ref, bconv_ref):
    acc = bconv_ref[...]
    for j in range(CONV_W):
        acc = acc + window(j) * wconv_ref[j:j + 1, :]
    return acc * _sigmoid(acc)


def _gate_columns(gates, gb_ref, row_valid, ltri, extra=None):
    lane = lax.broadcasted_iota(jnp.int32, gates.shape, 1)
    gb = gates + gb_ref[...]
    is_i = lane < M_HEADS
    is_f = (lane >= M_HEADS) & (lane < 2 * M_HEADS)
    logf = _log_sigmoid(gb)
    if row_valid is not None:
        lf = jnp.where(is_f & row_valid, logf, 0.0)
        li = jnp.where(row_valid, gb, -jnp.inf)
    else:
        lf = jnp.where(is_f, logf, 0.0)
        li = gb
    hi, lo = _split2(lf)
    bc = _dot(ltri, hi) + _dot(ltri, lo)
    x = jnp.where(is_i, li, bc)
    if extra is None:
        return x, None
    return x, _dot(extra, hi) + _dot(extra, lo)


class _FfnStages:
    N_CHUNKS = D_FF // FF_CHUNK

    def __init__(self, x, ymix, wout_ref, ln1g_ref, ln1b_ref, wg_ref, wu_ref, wd_ref, ln2g_ref, ln2b_ref):
        self.x, self.ymix = x, ymix
        self.wout_ref, self.ln1g_ref, self.ln1b_ref = wout_ref, ln1g_ref, ln1b_ref
        self.wg_ref, self.wu_ref, self.wd_ref = wg_ref, wu_ref, wd_ref
        self.ln2g_ref, self.ln2b_ref = ln2g_ref, ln2b_ref

    def head(self):
        self.x1 = _layer_norm(DEEPNORM_ALPHA * self.x + _dot(self.ymix, self.wout_ref[...]),
                              self.ln1g_ref[...], self.ln1b_ref[...])
        self.x1b = self.x1.astype(BF16)
        self.acc = DEEPNORM_ALPHA * self.x1

    def _gate_up(self, c):
        cs = slice(c * FF_CHUNK, (c + 1) * FF_CHUNK)
        return _dot(self.x1b, self.wg_ref[:, cs]), _dot(self.x1b, self.wu_ref[:, cs])

    def chunk(self, c):
        if c == 0:
            self.gu = {}
        for n in range(c, min(c + FF_LOOKAHEAD, self.N_CHUNKS - 1) + 1):
            if n not in self.gu:
                self.gu[n] = self._gate_up(n)
        g, u = self.gu.pop(c)
        hid = (g * _sigmoid(g) * u).astype(BF16)
        self.acc = self.acc + _dot(hid, self.wd_ref[c * FF_CHUNK:(c + 1) * FF_CHUNK, :])

    def tail(self):
        return _layer_norm(self.acc, self.ln2g_ref[...], self.ln2b_ref[...])

    def run(self):
        self.head()
        for c in range(self.N_CHUNKS):
            self.chunk(c)
        return self.tail()


def _mix_blocks(zc, nb, pad_rows, thr_first, state, cbuf, wconv_ref, bconv_ref, wq_ref, wkt_ref, gb_ref,
                sinks_ref, gattn_ref, gml_ref, hook):
    rows = nb * BLOCK
    kp_lo, kp_hi, vp_lo, vp_hi, c_state, m_state = state
    c_state, m_state = list(c_state), list(m_state)
    lane = lax.broadcasted_iota(jnp.int32, (BLOCK, 128), 1)
    lo_half = lane < HEAD_DIM
    row_valid = None
    if pad_rows:
        row_valid = lax.broadcasted_iota(jnp.int32, (rows, 1), 0) >= pad_rows

    s_all, vmasks = [], []
    for blk in range(nb):
        rs = slice(blk * BLOCK, (blk + 1) * BLOCK)
        kb = zc[rs, ATT_W:ATT_W + KV_W].astype(BF16)
        vb = zc[rs, ATT_W + KV_W:ATT_W + 2 * KV_W].astype(BF16)
        zero_b = jnp.zeros_like(kb)
        k_lo, k_hi = jnp.where(lo_half, kb, zero_b), jnp.where(lo_half, zero_b, kb)
        v_lo, v_hi = jnp.where(lo_half, vb, zero_b), jnp.where(lo_half, zero_b, vb)
        kmask = jnp.concatenate([kp_lo, k_lo, kp_hi, k_hi], axis=0)
        vmasks.append(jnp.concatenate([vp_lo, v_lo, vp_hi, v_hi], axis=0))
        q4 = jnp.concatenate([zc[rs, r * 128:(r + 1) * 128] for r in range(4)], axis=0).astype(BF16)
        s_all.append(_dot_nt(q4, kmask))
        kp_lo, kp_hi, vp_lo, vp_hi = k_lo, k_hi, v_lo, v_hi
    hook("scores")

    c_in = zc[:, 768:1280]
    if row_valid is not None:
        c_in = jnp.where(row_valid, c_in, 0.0)
    cbuf[8:8 + rows, :] = c_in
    c_act = _conv_silu(lambda jj: cbuf[5 + jj:5 + jj + rows, :], wconv_ref, bconv_ref)
    cbuf[0:8, :] = cbuf[rows:rows + 8, :]
    hs = range(M_HEADS)
    ca_b = [c_act[:, h * 128:(h + 1) * 128].astype(BF16) for h in hs]
    qm_b = [_dot(ca_b[h], wq_ref[h]).astype(BF16) for h in hs]
    kmt_f = [_dot_nt(wkt_ref[h], ca_b[h]) * (M_HEAD_DIM ** -0.5) for h in hs]
    hook("front")

    ti = lax.broadcasted_iota(jnp.int32, (rows, rows), 0)
    si = lax.broadcasted_iota(jnp.int32, (rows, rows), 1)
    ltri = jnp.where((si <= ti) & ((si >> 7) == (ti >> 7)), 1.0, 0.0).astype(BF16)
    xg, _ = _gate_columns(zc[:, IN_MAIN:IN_PAD], gb_ref, row_valid, ltri)
    xgt = xg.T
    hook("gates")

    qi = lax.broadcasted_iota(jnp.int32, (BLOCK, BLOCK), 0)
    ks = lax.broadcasted_iota(jnp.int32, (BLOCK, BLOCK), 1)
    own = ks <= qi
    distf = jnp.where(own, qi - ks, WINDOW + qi - ks).astype(F32)
    zero_p = jnp.zeros((BLOCK, BLOCK), BF16)
    att_rows = []
    for blk in range(nb):
        live = None
        if blk == 0 and thr_first is not None:
            live = jnp.where(own, ks + BLOCK, ks) >= thr_first
        p_rows, inv = [], {}
        for r in range(4):
            pr = []
            for c in range(2):
                h = r + 4 * c
                s_blk = s_all[blk][r * 128:(r + 1) * 128, c * 256:(c + 1) * 256]
                sp = jnp.where(own, s_blk[:, BLOCK:], s_blk[:, :BLOCK]) * (HEAD_DIM ** -0.5) \
                    - ALIBI_SLOPES[h] * distf
                if live is not None:
                    sp = jnp.where(live, sp, -jnp.inf)
                sink = sinks_ref[h]
                mx = jnp.maximum(jnp.max(sp, axis=1, keepdims=True), sink)
                p = jnp.exp(sp - mx)
                inv[h] = 1.0 / (jnp.sum(p, axis=1, keepdims=True) + jnp.exp(sink - mx))
                p_b = p.astype(BF16)
                pr += [jnp.where(own, zero_p, p_b), jnp.where(own, p_b, zero_p)]
            p_rows.append(jnp.concatenate(pr, axis=1))
            hook("softmax", blk * 4 + r)
        o_all = _dot(jnp.concatenate(p_rows, axis=0), vmasks[blk])
        att_rows.append(jnp.concatenate(
            [o_all[r * 128:(r + 1) * 128] * jnp.where(lo_half, inv[r], inv[r + 4]) for r in range(4)], axis=1))
        hook("pv", blk)

    ones_col = jnp.where(lane == 0, 1.0, 0.0).astype(BF16)
    tb = lax.broadcasted_iota(jnp.int32, (BLOCK, BLOCK), 0)
    sb = lax.broadcasted_iota(jnp.int32, (BLOCK, BLOCK), 1)
    tri = sb <= tb
    hh_rows = [[] for _ in hs]
    for blk in range(nb):
        rs = slice(blk * BLOCK, (blk + 1) * BLOCK)
        b_c = [xg[rs, 4 + h:5 + h] for h in hs]
        b_r = [xgt[4 + h:5 + h, rs] for h in hs]
        li_r = [xgt[h:h + 1, rs] for h in hs]
        kmt = [kmt_f[h][:, rs] for h in hs]
        v_ext = [jnp.concatenate([zc[rs, 1280 + h * 128:1280 + (h + 1) * 128].astype(BF16), ones_col], axis=1)
                 for h in hs]
        s_qk = [_dot(qm_b[h][rs], kmt[h].astype(BF16)) for h in hs]
        inter = [_dot(qm_b[h][rs], c_state[h].astype(BF16)) for h in hs]
        hook("mlstm_a", blk)
        dmat = [jnp.where(tri, b_c[h] - b_r[h] + li_r[h], -jnp.inf) for h in hs]
        m_inter = [b_c[h] + m_state[h] for h in hs]
        m_t = [jnp.maximum(m_inter[h], jnp.max(dmat[h], axis=1, keepdims=True)) for h in hs]
        w_inter = [jnp.exp(m_inter[h] - m_t[h]) for h in hs]
        sc_b = [(s_qk[h] * jnp.exp(dmat[h] - m_t[h])).astype(BF16) for h in hs]
        hook("mlstm_b", blk)
        nd = [_dot(sc_b[h], v_ext[h]) + w_inter[h] * inter[h] for h in hs]
        for h in hs:
            m_end = m_t[h][BLOCK - 1:BLOCK, :]
            b_last = b_c[h][BLOCK - 1:BLOCK, :]
            decay = jnp.exp(b_last + m_state[h] - m_end)
            wk_r = jnp.exp(b_last - b_r[h] + li_r[h] - m_end)
            c_state[h] = decay * c_state[h] + _dot((kmt[h] * wk_r).astype(BF16), v_ext[h])
            m_state[h] = m_end
            hh_rows[h].append(nd[h][:, :128] / jnp.maximum(jnp.abs(nd[h][:, 128:129]), jnp.exp(-m_t[h])))
        hook("mlstm_c", blk)

    y_att = _attn_head_norm(jnp.concatenate(att_rows, axis=0), gattn_ref[...])
    hook("attnorm")
    y_m = []
    for h in hs:
        hm = _sigmoid(zc[:, 1792 + h * 128:1792 + (h + 1) * 128]) * jnp.concatenate(hh_rows[h], axis=0)
        y_m.append(_mlstm_head_norm(hm, gml_ref[:, h * 128:(h + 1) * 128]))
        hook("headnorm", h)
    ymix = jnp.concatenate([y_att] + y_m, axis=1).astype(BF16)
    return ymix, (kp_lo, kp_hi, vp_lo, vp_hi, c_state, m_state)


def _meta_kernel(meta_ref, w_in_ref, wconv_ref, bconv_ref, wq_ref, wkt_ref, gb_ref, sinks_ref, gattn_ref,
                 gml_ref, kplo_o, kphi_o, vplo_o, vphi_o, cb_o, ctn_o, mst_o, zc, cbuf):
    zc[...] = _dot(meta_ref[...].astype(BF16), w_in_ref[...])
    cbuf[0:8, :] = jnp.zeros((8, MLSTM_W), F32)
    zb = jnp.zeros((BLOCK, KV_W), BF16)
    state = (zb, zb, zb, zb, [jnp.zeros((M_HEAD_DIM, 2 * M_HEAD_DIM), F32)] * M_HEADS,
             [jnp.zeros((1, 1), F32)] * M_HEADS)
    _, (kp_lo, kp_hi, vp_lo, vp_hi, c_state, m_state) = _mix_blocks(
        zc, 1, META_PAD, BLOCK + META_PAD, state, cbuf, wconv_ref, bconv_ref, wq_ref, wkt_ref, gb_ref,
        sinks_ref, gattn_ref, gml_ref, lambda name, index=0: None)
    kplo_o[...] = kp_lo
    kphi_o[...] = kp_hi
    vplo_o[...] = vp_lo
    vphi_o[...] = vp_hi
    cb_o[...] = cbuf[0:8, :]
    for h in range(M_HEADS):
        ctn_o[h] = c_state[h]
        mst_o[h:h + 1, :] = jnp.broadcast_to(m_state[h], (1, 128))
    mst_o[M_HEADS:8, :] = jnp.zeros((8 - M_HEADS, 128), F32)


def _pair_kernel(xin_ref, xres_ref, kplo_i, kphi_i, vplo_i, vphi_i, cb_i, ctn_i, mst_i,
                 w_in_ref, wconv_ref, bconv_ref, wq_ref, wkt_ref, gb_ref, sinks_ref, gattn_ref, gml_ref,
                 wout_ref, ln1g_ref, ln1b_ref, wg_ref, wu_ref, wd_ref, ln2g_ref, ln2b_ref,
                 y_ref, pk_ref, pv_ref, pconv_ref, pct_ref, pm_ref,
                 zcur, znext, kvlast, kplo, kphi, vplo, vphi, cbuf, ctn, mst, yprev, *, npair):
    t = pl.program_id(0)
    p = lax.rem(t + (npair - 1), npair)

    @pl.when(t == 0)
    def _init_pipeline():
        zcur[...] = jnp.zeros_like(zcur)
        yprev[...] = jnp.zeros_like(yprev)

    @pl.when((p == 0) | (t == 0))
    def _load_meta_state():
        kplo[...] = kplo_i[...]
        kphi[...] = kphi_i[...]
        vplo[...] = vplo_i[...]
        vphi[...] = vphi_i[...]
        cbuf[0:8, :] = cb_i[...]
        ctn[...] = ctn_i[...]
        mst[...] = mst_i[...]

    ffn = _FfnStages(xres_ref[...], yprev[...], wout_ref, ln1g_ref, ln1b_ref, wg_ref, wu_ref, wd_ref,
                     ln2g_ref, ln2b_ref)
    xb_b = xin_ref[...].astype(BF16)

    n_proj = -(-IN_PAD // PROJ_CHUNK)

    def proj(n):
        cs = slice(n * PROJ_CHUNK, min((n + 1) * PROJ_CHUNK, IN_PAD))
        znext[:, cs] = _dot(xb_b, w_in_ref[:, cs])

    proj_it, ffn_it = iter(range(n_proj)), iter(range(ffn.N_CHUNKS))

    def fill_proj(n):
        for c in [c for _, c in zip(range(n), proj_it)]:
            proj(c)

    def fill_ffn(n):
        for c in [c for _, c in zip(range(n), ffn_it)]:
            ffn.chunk(c)

    out = {}

    def hook(name, index=0):
        if name == "scores":
            fill_proj(2)
            ffn.head()
        elif name == "front":
            fill_proj(1)
        elif name == "gates":
            fill_ffn(1)
        elif name == "softmax":
            if index % 2 == 1:
                fill_ffn(1)
        elif name in ("pv", "mlstm_a", "mlstm_b"):
            fill_ffn(1)
        elif name == "mlstm_c":
            fill_proj(1)
        elif name == "attnorm":
            fill_ffn(ffn.N_CHUNKS)
            out["y"] = ffn.tail()
            fill_proj(1)
        elif name == "headnorm":
            fill_proj(1)

    state = (kplo[...], kphi[...], vplo[...], vphi[...], [ctn[h] for h in range(M_HEADS)],
             [mst[h:h + 1, 0:1] for h in range(M_HEADS)])
    thr_first = jnp.where(p == 0, META_PAD, 0)
    kvlast[...] = zcur[(STEP_BLOCKS - 1) * BLOCK:STEP_BLOCKS * BLOCK, ATT_W:ATT_W + 2 * KV_W]
    ymix, (kp_lo, kp_hi, vp_lo, vp_hi, c_state, m_state) = _mix_blocks(
        zcur, STEP_BLOCKS, 0, thr_first, state, cbuf, wconv_ref, bconv_ref, wq_ref, wkt_ref, gb_ref,
        sinks_ref, gattn_ref, gml_ref, hook)
    fill_proj(n_proj)
    y_ref[...] = out["y"]
    kplo[...] = kp_lo
    kphi[...] = kp_hi
    vplo[...] = vp_lo
    vphi[...] = vp_hi
    for h in range(M_HEADS):
        ctn[h] = c_state[h]
        mst[h:h + 1, :] = jnp.broadcast_to(m_state[h], (1, 128))
    yprev[...] = ymix
    zcur[...] = znext[...]

    @pl.when(p == npair - 1)
    def _final():
        pk_ref[...] = kvlast[:, 0:KV_W]
        pv_ref[...] = kvlast[:, KV_W:2 * KV_W]
        pconv_ref[...] = cbuf[0:8, :]
        pm_ref[...] = mst[...]
        for h in range(M_HEADS):
            pct_ref[h] = ctn[h].T


def _sample_kernel(xs_ref, ck_ref, cv_ref, sconv_ref, sc_ref, sn_ref, sm_ref,
                   w_in_ref, wconv_ref, bconv_ref, wqk_ref, gb_ref, sinks_ref,
                   gattn_ref, gml_ref,
                   ymix_ref, sk_ref, sv_ref, scv_ref, sco_ref, sno_ref, smo_ref, cext):
    ns, L = SEQ_TILE, 8
    xs = xs_ref[...].reshape(ns * L, D_MODEL)
    z = _dot(xs.astype(BF16), w_in_ref[...])

    lane = lax.broadcasted_iota(jnp.int32, (BLOCK, 128), 1)
    lo_half = lane < HEAD_DIM
    lo3 = lax.broadcasted_iota(jnp.int32, (ns, L, 128), 2) < HEAD_DIM
    ti = lax.broadcasted_iota(jnp.int32, (BLOCK, BLOCK), 0)
    si = lax.broadcasted_iota(jnp.int32, (BLOCK, BLOCK), 1)
    same_seq = (ti >> 3) == (si >> 3)
    tl, sl_ = ti & 7, si & 7
    causal_seq = same_seq & (sl_ <= tl)

    k_new = z[:, ATT_W:ATT_W + KV_W]
    v_new = z[:, ATT_W + KV_W:ATT_W + 2 * KV_W]
    kb, vb = k_new.astype(BF16), v_new.astype(BF16)
    zero_b = jnp.zeros_like(kb)
    kn_mask = jnp.concatenate([jnp.where(lo_half, kb, zero_b), jnp.where(lo_half, zero_b, kb)], axis=0)
    vn_mask = jnp.concatenate([jnp.where(lo_half, vb, zero_b), jnp.where(lo_half, zero_b, vb)], axis=0)
    qg = [z[:, r * 128:(r + 1) * 128] for r in range(4)]
    s_new = _dot_nt(jnp.concatenate(qg, axis=0).astype(BF16), kn_mask)
    qg3 = [q.reshape(ns, L, 128) for q in qg]
    q8 = jnp.concatenate([jnp.where(lo3, q, 0.0) for q in qg3] + [jnp.where(lo3, 0.0, q) for q in qg3],
                         axis=1).astype(BF16)
    s_cache = jnp.einsum('nqd,nkd->nqk', q8, ck_ref[...].astype(BF16), preferred_element_type=F32)

    c3 = z[:, 768:1280].reshape(ns, L, MLSTM_W)
    cext[:, 0:8, :] = sconv_ref[...]
    cext[:, 8:16, :] = c3
    scv_ref[...] = c3
    c_act = _conv_silu(lambda jj: cext[:, 5 + jj:5 + jj + L, :], wconv_ref, bconv_ref)
    c_act = c_act.reshape(BLOCK, MLSTM_W)
    lblk = jnp.where(causal_seq, 1.0, 0.0).astype(BF16)
    lseq = jnp.where(same_seq, 1.0, 0.0).astype(BF16)
    xg, blast = _gate_columns(z[:, IN_MAIN:IN_PAD], gb_ref, None, lblk, extra=lseq)
    xgt = xg.T
    m_rep = jnp.broadcast_to(sm_ref[...], (ns, L, 128)).reshape(BLOCK, 128)
    hs = range(M_HEADS)
    qk = [_dot(c_act[:, h * 128:(h + 1) * 128].astype(BF16), wqk_ref[h]) for h in hs]
    qm = [qk[h][:, :128] for h in hs]
    qm_b = [qm[h].astype(BF16) for h in hs]
    km = [qk[h][:, 128:] * (M_HEAD_DIM ** -0.5) for h in hs]
    km_b = [km[h].astype(BF16) for h in hs]
    vh_b = [z[:, 1280 + h * 128:1280 + (h + 1) * 128].astype(BF16) for h in hs]
    c_old = [sc_ref[:, h] for h in hs]
    s_qk = [_dot_nt(qm_b[h], km_b[h]) for h in hs]
    cq = [_dot_nt(qm_b[h], c_old[h].reshape(ns * 128, 128).astype(BF16)) for h in hs]

    mask_c = si > tl
    dist_c = (WINDOW + tl - si).astype(F32)
    dist_n = (tl - sl_).astype(F32)
    pc_parts, pn_rows, inv = {}, [], {}
    for r in range(4):
        pn = []
        for c in range(2):
            h = r + 4 * c
            sc_c = s_cache[:, c * 32 + r * L:c * 32 + (r + 1) * L, :].reshape(BLOCK, 128)
            sc_n = s_new[r * 128:(r + 1) * 128, c * 128:(c + 1) * 128]
            (p_c, p_n), inv[h] = _softmax_parts(
                [(sc_c, mask_c, -ALIBI_SLOPES[h] * dist_c), (sc_n, causal_seq, -ALIBI_SLOPES[h] * dist_n)],
                sinks_ref[h])
            pc_parts[(c, r)] = p_c.reshape(ns, L, 128)
            pn.append(p_n.astype(BF16))
        pn_rows.append(jnp.concatenate(pn, axis=1))
    p8 = jnp.concatenate([pc_parts[(c, r)] for c in range(2) for r in range(4)], axis=1).astype(BF16)
    o_cache = jnp.einsum('nqk,nkd->nqd', p8, cv_ref[...].astype(BF16), preferred_element_type=F32)
    o_new = _dot(jnp.concatenate(pn_rows, axis=0), vn_mask)
    groups = []
    for r in range(4):
        oa = o_cache[:, r * L:(r + 1) * L, :].reshape(BLOCK, 128)
        ob = o_cache[:, 32 + r * L:32 + (r + 1) * L, :].reshape(BLOCK, 128)
        o = jnp.where(lo_half, oa, ob) + o_new[r * 128:(r + 1) * 128]
        groups.append(o * jnp.where(lo_half, inv[r], inv[r + 4]))
    y_att = _attn_head_norm(jnp.concatenate(groups, axis=1), gattn_ref[...])

    sk_ref[:, 0:WINDOW - L, :] = ck_ref[:, L:WINDOW, :]
    sk_ref[:, WINDOW - L:WINDOW, :] = k_new.reshape(ns, L, 128)
    sv_ref[:, 0:WINDOW - L, :] = cv_ref[:, L:WINDOW, :]
    sv_ref[:, WINDOW - L:WINDOW, :] = v_new.reshape(ns, L, 128)

    b_c = [xg[:, 4 + h:5 + h] for h in hs]
    li_c = [xg[:, h:h + 1] for h in hs]
    b_r = [xgt[4 + h:5 + h, :] for h in hs]
    li_r = [xgt[h:h + 1, :] for h in hs]
    bl_c = [blast[:, 4 + h:5 + h] for h in hs]
    m_prev = [m_rep[:, h:h + 1] for h in hs]
    dmat = [jnp.where(causal_seq, b_c[h] - b_r[h] + li_r[h], -jnp.inf) for h in hs]
    m_inter = [b_c[h] + m_prev[h] for h in hs]
    m_t = [jnp.maximum(m_inter[h], jnp.max(dmat[h], axis=1, keepdims=True)) for h in hs]
    w_inter = [jnp.exp(m_inter[h] - m_t[h]) for h in hs]
    sc = [s_qk[h] * jnp.exp(dmat[h] - m_t[h]) for h in hs]
    num_intra = [_dot(sc[h].astype(BF16), vh_b[h]) for h in hs]

    m_end3 = [jnp.broadcast_to(m_t[h], (BLOCK, 128)).reshape(ns, L, 128)[:, L - 1:L, :] for h in hs]
    m_end = [jnp.broadcast_to(m_end3[h], (ns, L, 128)).reshape(BLOCK, 128)[:, 0:1] for h in hs]
    decay = [jnp.exp(bl_c[h] + m_prev[h] - m_end[h]) for h in hs]
    kw = [km[h] * jnp.exp(bl_c[h] - b_c[h] + li_c[h] - m_end[h]) for h in hs]
    seq_of_row = ti >> 3
    lane3 = lax.broadcasted_iota(jnp.int32, (ns, 1, 128), 2)
    m_out = jnp.zeros((ns, 1, 128), F32)
    for h in hs:
        kw_b = kw[h].astype(BF16)
        k_big = jnp.concatenate(
            [jnp.where(seq_of_row == g, kw_b, jnp.zeros_like(kw_b)) for g in range(ns)], axis=1)
        c_up = _dot_tn(vh_b[h], k_big)
        c_up3 = jnp.stack([c_up[:, g * 128:(g + 1) * 128] for g in range(ns)], axis=0)
        decay3 = jnp.broadcast_to(decay[h], (BLOCK, 128)).reshape(ns, L, 128)[:, 0:1, :]
        n_old = sn_ref[:, h:h + 1, :]
        sco_ref[:, h] = decay3 * c_old[h] + c_up3
        sno_ref[:, h:h + 1, :] = decay3 * n_old + jnp.sum(kw[h].reshape(ns, L, 128), axis=1, keepdims=True)
        m_out = jnp.where(lane3 == h, m_end3[h], m_out)
    smo_ref[...] = m_out

    y_m = []
    for h in hs:
        num_inter = jnp.concatenate([cq[h][g * L:(g + 1) * L, g * 128:(g + 1) * 128] for g in range(ns)], axis=0)
        n_rep = jnp.broadcast_to(sn_ref[:, h:h + 1, :], (ns, L, 128)).reshape(BLOCK, 128)
        num = num_intra[h] + w_inter[h] * num_inter
        den = jnp.sum(sc[h], axis=1, keepdims=True) + w_inter[h] * jnp.sum(qm[h] * n_rep, axis=1, keepdims=True)
        hh = num / jnp.maximum(jnp.abs(den), jnp.exp(-m_t[h]))
        hm = _sigmoid(z[:, 1792 + h * 128:1792 + (h + 1) * 128]) * hh
        y_m.append(_mlstm_head_norm(hm, gml_ref[:, h * 128:(h + 1) * 128]))

    ymix_ref[...] = jnp.concatenate([y_att] + y_m, axis=1).astype(BF16)


def _post_kernel(x_ref, y_ref, wout_ref, ln1g_ref, ln1b_ref, wg_ref, wu_ref, wd_ref, ln2g_ref, ln2b_ref, o_ref):
    o_ref[...] = _FfnStages(x_ref[...], y_ref[...], wout_ref, ln1g_ref, ln1b_ref, wg_ref, wu_ref, wd_ref,
                            ln2g_ref, ln2b_ref).run()


def _const_spec(shape):
    nd = len(shape)
    return pl.BlockSpec(shape, lambda *_: (0,) * nd, pipeline_mode=pl.Buffered(1))


def _mixer_weight_specs(qk_specs):
    return [
        _const_spec((D_MODEL, IN_PAD)),
        _const_spec((CONV_W, MLSTM_W)),
        _const_spec((1, MLSTM_W)),
    ] + qk_specs + [
        _const_spec((1, 128)),
        pl.BlockSpec(memory_space=pltpu.SMEM),
        _const_spec((1, ATT_W)),
        _const_spec((1, MLSTM_W)),
    ]


def _post_weight_specs(d):
    return [
        _const_spec((d, d)),
        _const_spec((1, d)), _const_spec((1, d)),
        _const_spec((d, D_FF)), _const_spec((d, D_FF)), _const_spec((D_FF, d)),
        _const_spec((1, d)), _const_spec((1, d)),
    ]


def _prompt_layer(x_prompt, meta_blk, mixer_w, post_w):
    b, s, d = x_prompt.shape
    qk_specs = [_const_spec((M_HEADS, M_HEAD_DIM, M_HEAD_DIM))] * 2
    state_shapes = [((BLOCK, KV_W), BF16)] * 4 + [
        ((8, MLSTM_W), F32), ((M_HEADS, M_HEAD_DIM, 2 * M_HEAD_DIM), F32), ((8, 128), F32)]
    meta_state = pl.pallas_call(
        _meta_kernel,
        grid=(1,),
        in_specs=[_const_spec((BLOCK, d))] + _mixer_weight_specs(qk_specs),
        out_specs=tuple(_const_spec(shape) for shape, _ in state_shapes),
        out_shape=tuple(jax.ShapeDtypeStruct(shape, dt) for shape, dt in state_shapes),
        scratch_shapes=[pltpu.VMEM((BLOCK, IN_PAD), F32), pltpu.VMEM((BLOCK + 8, MLSTM_W), F32)],
        compiler_params=pltpu.CompilerParams(
            dimension_semantics=("arbitrary",), vmem_limit_bytes=VMEM_LIMIT),
        name="meta_state",
    )(meta_blk, *mixer_w)

    rows = STEP_BLOCKS * BLOCK
    npair = s // rows
    total = b * npair
    steps = total + 2

    def block_map(lag):
        def index(t):
            tb = jnp.clip(t - lag, 0, total - 1)
            return (tb // npair, tb % npair, 0)
        return index

    def smap(t):
        return (jnp.clip(t - 1, 0, total - 1) // npair, 0, 0)

    out_shape = (
        jax.ShapeDtypeStruct((b, s, d), F32),
        jax.ShapeDtypeStruct((b, BLOCK, KV_W), F32),
        jax.ShapeDtypeStruct((b, BLOCK, KV_W), F32),
        jax.ShapeDtypeStruct((b, 8, MLSTM_W), F32),
        jax.ShapeDtypeStruct((b, M_HEADS, 2 * M_HEAD_DIM, M_HEAD_DIM), F32),
        jax.ShapeDtypeStruct((b, 8, 128), F32),
    )
    out_specs = (
        pl.BlockSpec((None, rows, d), block_map(2)),
        pl.BlockSpec((None, BLOCK, KV_W), smap),
        pl.BlockSpec((None, BLOCK, KV_W), smap),
        pl.BlockSpec((None, 8, MLSTM_W), smap),
        pl.BlockSpec((None, M_HEADS, 2 * M_HEAD_DIM, M_HEAD_DIM), lambda t: smap(t) + (0,)),
        pl.BlockSpec((None, 8, 128), smap),
    )
    return pl.pallas_call(
        functools.partial(_pair_kernel, npair=npair),
        grid=(steps,),
        in_specs=[pl.BlockSpec((None, rows, d), block_map(0)), pl.BlockSpec((None, rows, d), block_map(2))]
        + [_const_spec(shape) for shape, _ in state_shapes]
        + _mixer_weight_specs(qk_specs) + _post_weight_specs(d),
        out_specs=out_specs,
        out_shape=out_shape,
        scratch_shapes=[
            pltpu.VMEM((rows, IN_PAD), F32),
            pltpu.VMEM((rows, IN_PAD), F32),
            pltpu.VMEM((BLOCK, 2 * KV_W), F32),
        ] + [pltpu.VMEM((BLOCK, KV_W), BF16)] * 4 + [
            pltpu.VMEM((rows + 8, MLSTM_W), F32),
            pltpu.VMEM((M_HEADS, M_HEAD_DIM, 2 * M_HEAD_DIM), F32),
            pltpu.VMEM((8, 128), F32),
            pltpu.VMEM((rows, d), BF16),
        ],
        compiler_params=pltpu.CompilerParams(
            dimension_semantics=("arbitrary",), vmem_limit_bytes=VMEM_LIMIT),
        name="prompt_layer",
    )(x_prompt, x_prompt, *meta_state, *mixer_w, *post_w)


def _sample_mixer(x_sample, ck, cv, sconv8, s_c, s_n, s_m3, mixer_w):
    n, l, d = x_sample.shape
    t = SEQ_TILE
    m3 = lambda i: (i, 0, 0)
    m4 = lambda i: (i, 0, 0, 0)
    out_shape = (
        jax.ShapeDtypeStruct((n * l, d), BF16),
        jax.ShapeDtypeStruct((n, WINDOW, KV_W), F32),
        jax.ShapeDtypeStruct((n, WINDOW, KV_W), F32),
        jax.ShapeDtypeStruct((n, l, MLSTM_W), F32),
        jax.ShapeDtypeStruct((n, M_HEADS, M_HEAD_DIM, M_HEAD_DIM), F32),
        jax.ShapeDtypeStruct((n, M_HEADS, M_HEAD_DIM), F32),
        jax.ShapeDtypeStruct((n, 1, 128), F32),
    )
    out_specs = (
        pl.BlockSpec((t * l, d), lambda i: (i, 0)),
        pl.BlockSpec((t, WINDOW, KV_W), m3),
        pl.BlockSpec((t, WINDOW, KV_W), m3),
        pl.BlockSpec((t, l, MLSTM_W), m3),
        pl.BlockSpec((t, M_HEADS, M_HEAD_DIM, M_HEAD_DIM), m4),
        pl.BlockSpec((t, M_HEADS, M_HEAD_DIM), m3),
        pl.BlockSpec((t, 1, 128), m3),
    )
    in_specs = [
        pl.BlockSpec((t, l, d), m3),
        pl.BlockSpec((t, WINDOW, KV_W), m3),
        pl.BlockSpec((t, WINDOW, KV_W), m3),
        pl.BlockSpec((t, 8, MLSTM_W), m3),
        pl.BlockSpec((t, M_HEADS, M_HEAD_DIM, M_HEAD_DIM), m4),
        pl.BlockSpec((t, M_HEADS, M_HEAD_DIM), m3),
        pl.BlockSpec((t, 1, 128), m3),
    ] + _mixer_weight_specs([_const_spec((M_HEADS, M_HEAD_DIM, 2 * M_HEAD_DIM))])
    return pl.pallas_call(
        _sample_kernel,
        grid=(n // t,),
        in_specs=in_specs,
        out_specs=out_specs,
        out_shape=out_shape,
        scratch_shapes=[pltpu.VMEM((t, 16, MLSTM_W), F32)],
        compiler_params=pltpu.CompilerParams(
            dimension_semantics=("arbitrary",), vmem_limit_bytes=VMEM_LIMIT),
        name="sample_mixer",
    )(x_sample, ck, cv, sconv8, s_c, s_n, s_m3, *mixer_w)


def _post(x_rows, y_rows, post_w):
    rows, d = x_rows.shape
    tm = min(POST_ROWS, rows)
    rmap = lambda i: (i, 0)
    in_specs = [pl.BlockSpec((tm, d), rmap), pl.BlockSpec((tm, d), rmap)] + _post_weight_specs(d)
    return pl.pallas_call(
        _post_kernel,
        grid=(rows // tm,),
        in_specs=in_specs,
        out_specs=pl.BlockSpec((tm, d), rmap),
        out_shape=jax.ShapeDtypeStruct((rows, d), F32),
        compiler_params=pltpu.CompilerParams(
            dimension_semantics=("arbitrary",), vmem_limit_bytes=VMEM_LIMIT),
        name="post_ffn",
    )(x_rows, y_rows, *post_w)


def kernel(x_prompt, x_sample, cache_k, cache_v, state_conv, state_C, state_n, state_m, meta_tokens,
           w_in, w_conv, b_conv, w_mq, w_mk, b_i, b_f, attn_sinks, g_attn, g_mlstm, w_out,
           ln1_g, ln1_b, w_gate, w_up, w_down, ln2_g, ln2_b):
    b, s, d = x_prompt.shape
    n, l, _ = x_sample.shape

    w0 = w_in[0]
    w_in_p = jnp.concatenate(
        [_pair_heads(w0[:, :ATT_W], 1), w0[:, ATT_W:], jnp.zeros((d, IN_PAD - IN_MAIN - IN_GATES), w0.dtype)],
        axis=1).astype(BF16)
    wqk = jnp.concatenate([w_mq[0], w_mk[0]], axis=-1).astype(BF16)
    gb = jnp.concatenate([b_i[0], b_f[0], jnp.zeros((128 - IN_GATES,), F32)]).reshape(1, 128)
    mixer_head = (w_in_p, w_conv[0], b_conv[0].reshape(1, MLSTM_W))
    mixer_tail = (gb, attn_sinks[0], _pair_heads(g_attn[0].reshape(1, ATT_W), 1), g_mlstm[0].reshape(1, MLSTM_W))
    wq_b = w_mq[0].astype(BF16)
    wkt_b = jnp.swapaxes(w_mk[0], 1, 2).astype(BF16)
    w_out_p = jnp.concatenate([_pair_heads(w_out[0][:ATT_W], 0), w_out[0][ATT_W:]], axis=0).astype(BF16)
    post_w = (w_out_p, ln1_g[0].reshape(1, d), ln1_b[0].reshape(1, d),
              w_gate[0].astype(BF16), w_up[0].astype(BF16), w_down[0].astype(BF16),
              ln2_g[0].reshape(1, d), ln2_b[0].reshape(1, d))

    meta_blk = jnp.concatenate([jnp.zeros((META_PAD, d), x_prompt.dtype), meta_tokens.astype(x_prompt.dtype)], axis=0)
    y_prompt, pk, pv, pconv8, pct, pm8 = _prompt_layer(
        x_prompt, meta_blk, mixer_head + (wq_b, wkt_b) + mixer_tail, post_w)

    mixer_w = mixer_head + (wqk,) + mixer_tail
    sconv8 = jnp.pad(state_conv[0], ((0, 0), (8 - (CONV_W - 1), 0), (0, 0)))
    s_m3 = jnp.pad(state_m[0], ((0, 0), (0, 128 - M_HEADS))).reshape(n, 1, 128)
    ymix_s, sk, sv, scv, s_c, s_n, s_mo = _sample_mixer(
        x_sample, cache_k[0].reshape(n, WINDOW, KV_W), cache_v[0].reshape(n, WINDOW, KV_W),
        sconv8, state_C[0], state_n[0], s_m3, mixer_w)

    y_sample = _post(x_sample.reshape(n * l, d), ymix_s, post_w).reshape(n, l, d)

    kv5 = lambda a: a.reshape(1, a.shape[0], WINDOW, 2, HEAD_DIM)
    return (y_prompt, y_sample,
            kv5(pk), kv5(pv), pconv8[None, :, 8 - (CONV_W - 1):, :], pct[None, :, :, :M_HEAD_DIM, :],
            pct[None, :, :, M_HEAD_DIM, :], pm8[None, :, :M_HEADS, 0],
            kv5(sk), kv5(sv), scv[None, :, l - (CONV_W - 1):, :], s_c[None], s_n[None],
            s_mo[None, :, 0, :M_HEADS])
```

```python
import functools

import jax
import jax.numpy as jnp
from jax import lax
from jax.experimental import pallas as pl
from jax.experimental.pallas import tpu as pltpu

F32 = jnp.float32
BF16 = jnp.bfloat16

D_MODEL = 1024
ATT_W = 512
MLSTM_W = 512
HEAD_DIM = 64
N_HEADS = 8
KV_W = 128
WINDOW = 128
BLOCK = 128
M_HEADS = 4
M_HEAD_DIM = 128
CONV_W = 4
N_META = 16
META_PAD = BLOCK - N_META
D_FF = 2816
IN_MAIN = 2304
IN_GATES = 2 * M_HEADS
IN_PAD = IN_MAIN + 128
DEPTH = 1
ALIBI_SLOPES = tuple(2.0 ** (-8.0 * (h + 1) / N_HEADS) for h in range(N_HEADS))
DEEPNORM_ALPHA = (2.0 * DEPTH) ** 0.25
EPS = 1e-5
SEQ_TILE = 16
POST_ROWS = 512
STEP_BLOCKS = 2
FF_CHUNK = 256
FF_LOOKAHEAD = 2
PROJ_CHUNK = 256
VMEM_LIMIT = 56 * 1024 * 1024


def _pair_heads(a, axis):
    shape = a.shape
    a = a.reshape(shape[:axis] + (2, 4, HEAD_DIM) + shape[axis + 1:])
    return jnp.swapaxes(a, axis, axis + 1).reshape(shape)


def _dot(a, b):
    return jnp.dot(a, b, preferred_element_type=F32)


def _dot_nt(a, b):
    return lax.dot_general(a, b, (((1,), (1,)), ((), ())), preferred_element_type=F32)


def _dot_tn(a, b):
    return lax.dot_general(a, b, (((0,), (0,)), ((), ())), preferred_element_type=F32)


def _split2(x):
    hi = x.astype(BF16)
    lo = (x - hi.astype(F32)).astype(BF16)
    return hi, lo


def _sigmoid(x):
    return 1.0 / (1.0 + jnp.exp(-x))


def _log_sigmoid(x):
    return -(jnp.maximum(-x, 0.0) + jnp.log1p(jnp.exp(-jnp.abs(x))))


def _layer_norm(x, g, b):
    mu = jnp.mean(x, axis=-1, keepdims=True)
    xc = x - mu
    var = jnp.mean(xc * xc, axis=-1, keepdims=True)
    return xc * lax.rsqrt(var + EPS) * g + b


def _attn_head_norm(att, g):
    lo_half = lax.broadcasted_iota(jnp.int32, (att.shape[0], 128), 1) < HEAD_DIM

    def seg_mean(x):
        s_all = jnp.sum(x, axis=1, keepdims=True)
        s_lo = jnp.sum(jnp.where(lo_half, x, 0.0), axis=1, keepdims=True)
        return jnp.where(lo_half, s_lo, s_all - s_lo) * (1.0 / HEAD_DIM)

    out = []
    for grp in range(att.shape[1] // 128):
        x = att[:, grp * 128:(grp + 1) * 128]
        xc = x - seg_mean(x)
        out.append(xc * lax.rsqrt(seg_mean(xc * xc) + EPS))
    return jnp.concatenate(out, axis=1) * g


def _mlstm_head_norm(hm, g):
    mu = jnp.mean(hm, axis=-1, keepdims=True)
    xc = hm - mu
    var = jnp.mean(xc * xc, axis=-1, keepdims=True)
    return xc * lax.rsqrt(var + EPS) * g


def _softmax_parts(parts, sink):
    sp = [jnp.where(m, s * (HEAD_DIM ** -0.5) + a, -jnp.inf) for s, m, a in parts]
    mx = sink
    for s in sp:
        mx = jnp.maximum(mx, jnp.max(s, axis=1, keepdims=True))
    ps = [jnp.exp(s - mx) for s in sp]
    den = jnp.exp(sink - mx)
    for p in ps:
        den = den + jnp.sum(p, axis=1, keepdims=True)
    return ps, 1.0 / den


def _conv_silu(window, wconv_ref, bconv_ref):
    acc = bconv_ref[...]
    for j in range(CONV_W):
        acc = acc + window(j) * wconv_ref[j:j + 1, :]
    return acc * _sigmoid(acc)


def _gate_columns(gates, gb_ref, row_valid, ltri, extra=None):
    lane = lax.broadcasted_iota(jnp.int32, gates.shape, 1)
    gb = gates + gb_ref[...]
    is_i = lane < M_HEADS
    is_f = (lane >= M_HEADS) & (lane < 2 * M_HEADS)
    logf = _log_sigmoid(gb)
    if row_valid is not None:
        lf = jnp.where(is_f & row_valid, logf, 0.0)
        li = jnp.where(row_valid, gb, -jnp.inf)
    else:
        lf = jnp.where(is_f, logf, 0.0)
        li = gb
    hi, lo = _split2(lf)
    bc = _dot(ltri, hi) + _dot(ltri, lo)
    x = jnp.where(is_i, li, bc)
    if extra is None:
        return x, None
    return x, _dot(extra, hi) + _dot(extra, lo)


class _FfnStages:
    N_CHUNKS = D_FF // FF_CHUNK

    def __init__(self, x, ymix, wout_ref, ln1g_ref, ln1b_ref, wg_ref, wu_ref, wd_ref, ln2g_ref, ln2b_ref):
        self.x, self.ymix = x, ymix
        self.wout_ref, self.ln1g_ref, self.ln1b_ref = wout_ref, ln1g_ref, ln1b_ref
        self.wg_ref, self.wu_ref, self.wd_ref = wg_ref, wu_ref, wd_ref
        self.ln2g_ref, self.ln2b_ref = ln2g_ref, ln2b_ref

    def head(self):
        self.x1 = _layer_norm(DEEPNORM_ALPHA * self.x + _dot(self.ymix, self.wout_ref[...]),
                              self.ln1g_ref[...], self.ln1b_ref[...])
        self.x1b = self.x1.astype(BF16)
        self.acc = DEEPNORM_ALPHA * self.x1

    def _gate_up(self, c):
        cs = slice(c * FF_CHUNK, (c + 1) * FF_CHUNK)
        return _dot(self.x1b, self.wg_ref[:, cs]), _dot(self.x1b, self.wu_ref[:, cs])

    def chunk(self, c):
        if c == 0:
            self.gu = {}
        for n in range(c, min(c + FF_LOOKAHEAD, self.N_CHUNKS - 1) + 1):
            if n not in self.gu:
                self.gu[n] = self._gate_up(n)
        g, u = self.gu.pop(c)
        hid = (g * _sigmoid(g) * u).astype(BF16)
        self.acc = self.acc + _dot(hid, self.wd_ref[c * FF_CHUNK:(c + 1) * FF_CHUNK, :])

    def tail(self):
        return _layer_norm(self.acc, self.ln2g_ref[...], self.ln2b_ref[...])

    def run(self):
        self.head()
        for c in range(self.N_CHUNKS):
            self.chunk(c)
        return self.tail()


def _mix_blocks(zc, nb, pad_rows, thr_first, state, cbuf, wconv_ref, bconv_ref, wq_ref, wkt_ref, gb_ref,
                sinks_ref, gattn_ref, gml_ref, hook):
    rows = nb * BLOCK
    kp_lo, kp_hi, vp_lo, vp_hi, c_state, m_state = state
    c_state, m_state = list(c_state), list(m_state)
    lane = lax.broadcasted_iota(jnp.int32, (BLOCK, 128), 1)
    lo_half = lane < HEAD_DIM
    row_valid = None
    if pad_rows:
        row_valid = lax.broadcasted_iota(jnp.int32, (rows, 1), 0) >= pad_rows

    s_all, vmasks = [], []
    for blk in range(nb):
        rs = slice(blk * BLOCK, (blk + 1) * BLOCK)
        kb = zc[rs, ATT_W:ATT_W + KV_W].astype(BF16)
        vb = zc[rs, ATT_W + KV_W:ATT_W + 2 * KV_W].astype(BF16)
        zero_b = jnp.zeros_like(kb)
        k_lo, k_hi = jnp.where(lo_half, kb, zero_b), jnp.where(lo_half, zero_b, kb)
        v_lo, v_hi = jnp.where(lo_half, vb, zero_b), jnp.where(lo_half, zero_b, vb)
        kmask = jnp.concatenate([kp_lo, k_lo, kp_hi, k_hi], axis=0)
        vmasks.append(jnp.concatenate([vp_lo, v_lo, vp_hi, v_hi], axis=0))
        q4 = jnp.concatenate([zc[rs, r * 128:(r + 1) * 128] for r in range(4)], axis=0).astype(BF16)
        s_all.append(_dot_nt(q4, kmask))
        kp_lo, kp_hi, vp_lo, vp_hi = k_lo, k_hi, v_lo, v_hi
    hook("scores")

    c_in = zc[:, 768:1280]
    if row_valid is not None:
        c_in = jnp.where(row_valid, c_in, 0.0)
    cbuf[8:8 + rows, :] = c_in
    c_act = _conv_silu(lambda jj: cbuf[5 + jj:5 + jj + rows, :], wconv_ref, bconv_ref)
    cbuf[0:8, :] = cbuf[rows:rows + 8, :]
    hs = range(M_HEADS)
    ca_b = [c_act[:, h * 128:(h + 1) * 128].astype(BF16) for h in hs]
    qm_b = [_dot(ca_b[h], wq_ref[h]).astype(BF16) for h in hs]
    kmt_f = [_dot_nt(wkt_ref[h], ca_b[h]) * (M_HEAD_DIM ** -0.5) for h in hs]
    hook("front")

    ti = lax.broadcasted_iota(jnp.int32, (rows, rows), 0)
    si = lax.broadcasted_iota(jnp.int32, (rows, rows), 1)
    ltri = jnp.where((si <= ti) & ((si >> 7) == (ti >> 7)), 1.0, 0.0).astype(BF16)
    xg, _ = _gate_columns(zc[:, IN_MAIN:IN_PAD], gb_ref, row_valid, ltri)
    xgt = xg.T
    hook("gates")

    qi = lax.broadcasted_iota(jnp.int32, (BLOCK, BLOCK), 0)
    ks = lax.broadcasted_iota(jnp.int32, (BLOCK, BLOCK), 1)
    own = ks <= qi
    distf = jnp.where(own, qi - ks, WINDOW + qi - ks).astype(F32)
    zero_p = jnp.zeros((BLOCK, BLOCK), BF16)
    att_rows = []
    for blk in range(nb):
        live = None
        if blk == 0 and thr_first is not None:
            live = jnp.where(own, ks + BLOCK, ks) >= thr_first
        p_rows, inv = [], {}
        for r in range(4):
            pr = []
            for c in range(2):
                h = r + 4 * c
                s_blk = s_all[blk][r * 128:(r + 1) * 128, c * 256:(c + 1) * 256]
                sp = jnp.where(own, s_blk[:, BLOCK:], s_blk[:, :BLOCK]) * (HEAD_DIM ** -0.5) \
                    - ALIBI_SLOPES[h] * distf
                if live is not None:
                    sp = jnp.where(live, sp, -jnp.inf)
                sink = sinks_ref[h]
                mx = jnp.maximum(jnp.max(sp, axis=1, keepdims=True), sink)
                p = jnp.exp(sp - mx)
                inv[h] = 1.0 / (jnp.sum(p, axis=1, keepdims=True) + jnp.exp(sink - mx))
                p_b = p.astype(BF16)
                pr += [jnp.where(own, zero_p, p_b), jnp.where(own, p_b, zero_p)]
            p_rows.append(jnp.concatenate(pr, axis=1))
            hook("softmax", blk * 4 + r)
        o_all = _dot(jnp.concatenate(p_rows, axis=0), vmasks[blk])
        att_rows.append(jnp.concatenate(
            [o_all[r * 128:(r + 1) * 128] * jnp.where(lo_half, inv[r], inv[r + 4]) for r in range(4)], axis=1))
        hook("pv", blk)

    ones_col = jnp.where(lane == 0, 1.0, 0.0).astype(BF16)
    tb = lax.broadcasted_iota(jnp.int32, (BLOCK, BLOCK), 0)
    sb = lax.broadcasted_iota(jnp.int32, (BLOCK, BLOCK), 1)
    tri = sb <= tb
    hh_rows = [[] for _ in hs]
    for blk in range(nb):
        rs = slice(blk * BLOCK, (blk + 1) * BLOCK)
        b_c = [xg[rs, 4 + h:5 + h] for h in hs]
        b_r = [xgt[4 + h:5 + h, rs] for h in hs]
        li_r = [xgt[h:h + 1, rs] for h in hs]
        kmt = [kmt_f[h][:, rs] for h in hs]
        v_ext = [jnp.concatenate([zc[rs, 1280 + h * 128:1280 + (h + 1) * 128].astype(BF16), ones_col], axis=1)
                 for h in hs]
        s_qk = [_dot(qm_b[h][rs], kmt[h].astype(BF16)) for h in hs]
        inter = [_dot(qm_b[h][rs], c_state[h].astype(BF16)) for h in hs]
        hook("mlstm_a", blk)
        dmat = [jnp.where(tri, b_c[h] - b_r[h] + li_r[h], -jnp.inf) for h in hs]
        m_inter = [b_c[h] + m_state[h] for h in hs]
        m_t = [jnp.maximum(m_inter[h], jnp.max(dmat[h], axis=1, keepdims=True)) for h in hs]
        w_inter = [jnp.exp(m_inter[h] - m_t[h]) for h in hs]
        sc_b = [(s_qk[h] * jnp.exp(dmat[h] - m_t[h])).astype(BF16) for h in hs]
        hook("mlstm_b", blk)
        nd = [_dot(sc_b[h], v_ext[h]) + w_inter[h] * inter[h] for h in hs]
        for h in hs:
            m_end = m_t[h][BLOCK - 1:BLOCK, :]
            b_last = b_c[h][BLOCK - 1:BLOCK, :]
            decay = jnp.exp(b_last + m_state[h] - m_end)
            wk_r = jnp.exp(b_last - b_r[h] + li_r[h] - m_end)
            c_state[h] = decay * c_state[h] + _dot((kmt[h] * wk_r).astype(BF16), v_ext[h])
            m_state[h] = m_end
            hh_rows[h].append(nd[h][:, :128] / jnp.maximum(jnp.abs(nd[h][:, 128:129]), jnp.exp(-m_t[h])))
        hook("mlstm_c", blk)

    y_att = _attn_head_norm(jnp.concatenate(att_rows, axis=0), gattn_ref[...])
    hook("attnorm")
    y_m = []
    for h in hs:
        hm = _sigmoid(zc[:, 1792 + h * 128:1792 + (h + 1) * 128]) * jnp.concatenate(hh_rows[h], axis=0)
        y_m.append(_mlstm_head_norm(hm, gml_ref[:, h * 128:(h + 1) * 128]))
        hook("headnorm", h)
    ymix = jnp.concatenate([y_att] + y_m, axis=1).astype(BF16)
    return ymix, (kp_lo, kp_hi, vp_lo, vp_hi, c_state, m_state)


def _meta_kernel(meta_ref, w_in_ref, wconv_ref, bconv_ref, wq_ref, wkt_ref, gb_ref, sinks_ref, gattn_ref,
                 gml_ref, kplo_o, kphi_o, vplo_o, vphi_o, cb_o, ctn_o, mst_o, zc, cbuf):
    zc[...] = _dot(meta_ref[...].astype(BF16), w_in_ref[...])
    cbuf[0:8, :] = jnp.zeros((8, MLSTM_W), F32)
    zb = jnp.zeros((BLOCK, KV_W), BF16)
    state = (zb, zb, zb, zb, [jnp.zeros((M_HEAD_DIM, 2 * M_HEAD_DIM), F32)] * M_HEADS,
             [jnp.zeros((1, 1), F32)] * M_HEADS)
    _, (kp_lo, kp_hi, vp_lo, vp_hi, c_state, m_state) = _mix_blocks(
        zc, 1, META_PAD, BLOCK + META_PAD, state, cbuf, wconv_ref, bconv_ref, wq_ref, wkt_ref, gb_ref,
        sinks_ref, gattn_ref, gml_ref, lambda name, index=0: None)
    kplo_o[...] = kp_lo
    kphi_o[...] = kp_hi
    vplo_o[...] = vp_lo
    vphi_o[...] = vp_hi
    cb_o[...] = cbuf[0:8, :]
    for h in range(M_HEADS):
        ctn_o[h] = c_state[h]
        mst_o[h:h + 1, :] = jnp.broadcast_to(m_state[h], (1, 128))
    mst_o[M_HEADS:8, :] = jnp.zeros((8 - M_HEADS, 128), F32)


def _pair_kernel(xin_ref, xres_ref, kplo_i, kphi_i, vplo_i, vphi_i, cb_i, ctn_i, mst_i,
                 w_in_ref, wconv_ref, bconv_ref, wq_ref, wkt_ref, gb_ref, sinks_ref, gattn_ref, gml_ref,
                 wout_ref, ln1g_ref, ln1b_ref, wg_ref, wu_ref, wd_ref, ln2g_ref, ln2b_ref,
                 y_ref, pk_ref, pv_ref, pconv_ref, pc_ref, pn_ref, pm_ref,
                 zcur, znext, kvlast, kplo, kphi, vplo, vphi, cbuf, ctn, mst, yprev, *, npair):
    t = pl.program_id(0)
    p = lax.rem(t + (npair - 1), npair)

    @pl.when(t == 0)
    def _init_pipeline():
        zcur[...] = jnp.zeros_like(zcur)
        yprev[...] = jnp.zeros_like(yprev)

    @pl.when((p == 0) | (t == 0))
    def _load_meta_state():
        kplo[...] = kplo_i[...]
        kphi[...] = kphi_i[...]
        vplo[...] = vplo_i[...]
        vphi[...] = vphi_i[...]
        cbuf[0:8, :] = cb_i[...]
        ctn[...] = ctn_i[...]
        mst[...] = mst_i[...]

    ffn = _FfnStages(xres_ref[...], yprev[...], wout_ref, ln1g_ref, ln1b_ref, wg_ref, wu_ref, wd_ref,
                     ln2g_ref, ln2b_ref)
    xb_b = xin_ref[...].astype(BF16)

    n_proj = -(-IN_PAD // PROJ_CHUNK)

    def proj(n):
        cs = slice(n * PROJ_CHUNK, min((n + 1) * PROJ_CHUNK, IN_PAD))
        znext[:, cs] = _dot(xb_b, w_in_ref[:, cs])

    proj_it, ffn_it = iter(range(n_proj)), iter(range(ffn.N_CHUNKS))

    def fill_proj(n):
        for c in [c for _, c in zip(range(n), proj_it)]:
            proj(c)

    def fill_ffn(n):
        for c in [c for _, c in zip(range(n), ffn_it)]:
            ffn.chunk(c)

    out = {}

    def hook(name, index=0):
        if name == "scores":
            fill_proj(2)
            ffn.head()
        elif name == "front":
            fill_proj(1)
        elif name == "gates":
            fill_ffn(1)
        elif name == "softmax":
            if index % 2 == 1:
                fill_ffn(1)
        elif name in ("pv", "mlstm_a", "mlstm_b"):
            fill_ffn(1)
        elif name == "mlstm_c":
            fill_proj(1)
        elif name == "attnorm":
            fill_ffn(ffn.N_CHUNKS)
            out["y"] = ffn.tail()
            fill_proj(1)
        elif name == "headnorm":
            fill_proj(1)

    state = (kplo[...], kphi[...], vplo[...], vphi[...], [ctn[h] for h in range(M_HEADS)],
             [mst[h:h + 1, 0:1] for h in range(M_HEADS)])
    thr_first = jnp.where(p == 0, META_PAD, 0)
    kvlast[...] = zcur[(STEP_BLOCKS - 1) * BLOCK:STEP_BLOCKS * BLOCK, ATT_W:ATT_W + 2 * KV_W]
    ymix, (kp_lo, kp_hi, vp_lo, vp_hi, c_state, m_state) = _mix_blocks(
        zcur, STEP_BLOCKS, 0, thr_first, state, cbuf, wconv_ref, bconv_ref, wq_ref, wkt_ref, gb_ref,
        sinks_ref, gattn_ref, gml_ref, hook)
    fill_proj(n_proj)
    y_ref[...] = out["y"]
    kplo[...] = kp_lo
    kphi[...] = kp_hi
    vplo[...] = vp_lo
    vphi[...] = vp_hi
    for h in range(M_HEADS):
        ctn[h] = c_state[h]
        mst[h:h + 1, :] = jnp.broadcast_to(m_state[h], (1, 128))
    yprev[...] = ymix
    zcur[...] = znext[...]

    @pl.when(p == npair - 1)
    def _final():
        pk_ref[...] = kvlast[:, 0:KV_W]
        pv_ref[...] = kvlast[:, KV_W:2 * KV_W]
        pconv_ref[...] = cbuf[8 - (CONV_W - 1):8, :]
        pm_ref[...] = mst[...]
        for h in range(M_HEADS):
            c_n = ctn[h].T
            pc_ref[h] = c_n[0:M_HEAD_DIM, :]
            pn_ref[h:h + 1, :] = c_n[M_HEAD_DIM:M_HEAD_DIM + 1, :]


def _sample_kernel(xs_ref, ck_ref, cv_ref, sconv_ref, sc_ref, sn_ref, sm_ref,
                   w_in_ref, wconv_ref, bconv_ref, wqk_ref, gb_ref, sinks_ref,
                   gattn_ref, gml_ref,
                   ymix_ref, sk_ref, sv_ref, scv_ref, sco_ref, sno_ref, smo_ref, cext):
    ns, L = SEQ_TILE, 8
    xs = xs_ref[...].reshape(ns * L, D_MODEL)
    z = _dot(xs.astype(BF16), w_in_ref[...])

    lane = lax.broadcasted_iota(jnp.int32, (BLOCK, 128), 1)
    lo_half = lane < HEAD_DIM
    lo3 = lax.broadcasted_iota(jnp.int32, (ns, L, 128), 2) < HEAD_DIM
    ti = lax.broadcasted_iota(jnp.int32, (BLOCK, BLOCK), 0)
    si = lax.broadcasted_iota(jnp.int32, (BLOCK, BLOCK), 1)
    same_seq = (ti >> 3) == (si >> 3)
    tl, sl_ = ti & 7, si & 7
    causal_seq = same_seq & (sl_ <= tl)

    k_new = z[:, ATT_W:ATT_W + KV_W]
    v_new = z[:, ATT_W + KV_W:ATT_W + 2 * KV_W]
    kb, vb = k_new.astype(BF16), v_new.astype(BF16)
    zero_b = jnp.zeros_like(kb)
    kn_mask = jnp.concatenate([jnp.where(lo_half, kb, zero_b), jnp.where(lo_half, zero_b, kb)], axis=0)
    vn_mask = jnp.concatenate([jnp.where(lo_half, vb, zero_b), jnp.where(lo_half, zero_b, vb)], axis=0)
    qg = [z[:, r * 128:(r + 1) * 128] for r in range(4)]
    s_new = _dot_nt(jnp.concatenate(qg, axis=0).astype(BF16), kn_mask)
    qg3 = [q.reshape(ns, L, 128) for q in qg]
    q8 = jnp.concatenate([jnp.where(lo3, q, 0.0) for q in qg3] + [jnp.where(lo3, 0.0, q) for q in qg3],
                         axis=1).astype(BF16)
    s_cache = jnp.einsum('nqd,ndk->nqk', q8, ck_ref[...].astype(BF16), preferred_element_type=F32)

    c3 = z[:, 768:1280].reshape(ns, L, MLSTM_W)
    cext[:, 8 - (CONV_W - 1):8, :] = sconv_ref[...]
    cext[:, 8:16, :] = c3
    scv_ref[...] = c3[:, L - (CONV_W - 1):L, :]
    c_act = _conv_silu(lambda jj: cext[:, 5 + jj:5 + jj + L, :], wconv_ref, bconv_ref)
    c_act = c_act.reshape(BLOCK, MLSTM_W)
    lblk = jnp.where(causal_seq, 1.0, 0.0).astype(BF16)
    lseq = jnp.where(same_seq, 1.0, 0.0).astype(BF16)
    xg, blast = _gate_columns(z[:, IN_MAIN:IN_PAD], gb_ref, None, lblk, extra=lseq)
    xgt = xg.T
    m_rep = jnp.broadcast_to(sm_ref[...], (ns, L, 128)).reshape(BLOCK, 128)
    hs = range(M_HEADS)
    qk = [_dot(c_act[:, h * 128:(h + 1) * 128].astype(BF16), wqk_ref[h]) for h in hs]
    qm = [qk[h][:, :128] for h in hs]
    qm_b = [qm[h].astype(BF16) for h in hs]
    km = [qk[h][:, 128:] * (M_HEAD_DIM ** -0.5) for h in hs]
    km_b = [km[h].astype(BF16) for h in hs]
    vh_b = [z[:, 1280 + h * 128:1280 + (h + 1) * 128].astype(BF16) for h in hs]
    c_old = [sc_ref[:, h] for h in hs]
    s_qk = [_dot_nt(qm_b[h], km_b[h]) for h in hs]
    cq = [_dot_nt(qm_b[h], c_old[h].reshape(ns * 128, 128).astype(BF16)) for h in hs]

    mask_c = si > tl
    dist_c = (WINDOW + tl - si).astype(F32)
    dist_n = (tl - sl_).astype(F32)
    pc_parts, pn_rows, inv = {}, [], {}
    for r in range(4):
        pn = []
        for c in range(2):
            h = r + 4 * c
            sc_c = s_cache[:, c * 32 + r * L:c * 32 + (r + 1) * L, :].reshape(BLOCK, 128)
            sc_n = s_new[r * 128:(r + 1) * 128, c * 128:(c + 1) * 128]
            (p_c, p_n), inv[h] = _softmax_parts(
                [(sc_c, mask_c, -ALIBI_SLOPES[h] * dist_c), (sc_n, causal_seq, -ALIBI_SLOPES[h] * dist_n)],
                sinks_ref[h])
            pc_parts[(c, r)] = p_c.reshape(ns, L, 128)
            pn.append(p_n.astype(BF16))
        pn_rows.append(jnp.concatenate(pn, axis=1))
    p8 = jnp.concatenate([pc_parts[(c, r)] for c in range(2) for r in range(4)], axis=1).astype(BF16)
    o_cache = jnp.einsum('nqk,ndk->nqd', p8, cv_ref[...].astype(BF16), preferred_element_type=F32)
    o_new = _dot(jnp.concatenate(pn_rows, axis=0), vn_mask)
    groups = []
    for r in range(4):
        oa = o_cache[:, r * L:(r + 1) * L, :].reshape(BLOCK, 128)
        ob = o_cache[:, 32 + r * L:32 + (r + 1) * L, :].reshape(BLOCK, 128)
        o = jnp.where(lo_half, oa, ob) + o_new[r * 128:(r + 1) * 128]
        groups.append(o * jnp.where(lo_half, inv[r], inv[r + 4]))
    y_att = _attn_head_norm(jnp.concatenate(groups, axis=1), gattn_ref[...])

    sk_ref[:, 0:WINDOW - L, :] = jnp.swapaxes(ck_ref[...], 1, 2)[:, L:WINDOW, :]
    sk_ref[:, WINDOW - L:WINDOW, :] = k_new.reshape(ns, L, 128)
    sv_ref[:, 0:WINDOW - L, :] = jnp.swapaxes(cv_ref[...], 1, 2)[:, L:WINDOW, :]
    sv_ref[:, WINDOW - L:WINDOW, :] = v_new.reshape(ns, L, 128)

    b_c = [xg[:, 4 + h:5 + h] for h in hs]
    li_c = [xg[:, h:h + 1] for h in hs]
    b_r = [xgt[4 + h:5 + h, :] for h in hs]
    li_r = [xgt[h:h + 1, :] for h in hs]
    bl_c = [blast[:, 4 + h:5 + h] for h in hs]
    m_prev = [m_rep[:, h:h + 1] for h in hs]
    dmat = [jnp.where(causal_seq, b_c[h] - b_r[h] + li_r[h], -jnp.inf) for h in hs]
    m_inter = [b_c[h] + m_prev[h] for h in hs]
    m_t = [jnp.maximum(m_inter[h], jnp.max(dmat[h], axis=1, keepdims=True)) for h in hs]
    w_inter = [jnp.exp(m_inter[h] - m_t[h]) for h in hs]
    sc = [s_qk[h] * jnp.exp(dmat[h] - m_t[h]) for h in hs]
    num_intra = [_dot(sc[h].astype(BF16), vh_b[h]) for h in hs]

    m_end3 = [jnp.broadcast_to(m_t[h], (BLOCK, 128)).reshape(ns, L, 128)[:, L - 1:L, :] for h in hs]
    m_end = [jnp.broadcast_to(m_end3[h], (ns, L, 128)).reshape(BLOCK, 128)[:, 0:1] for h in hs]
    decay = [jnp.exp(bl_c[h] + m_prev[h] - m_end[h]) for h in hs]
    kw = [km[h] * jnp.exp(bl_c[h] - b_c[h] + li_c[h] - m_end[h]) for h in hs]
    seq_of_row = ti >> 3
    lane3 = lax.broadcasted_iota(jnp.int32, (ns, 1, 128), 2)
    m_out = jnp.zeros((ns, 1, 128), F32)
    for h in hs:
        kw_b = kw[h].astype(BF16)
        k_big = jnp.concatenate(
            [jnp.where(seq_of_row == g, kw_b, jnp.zeros_like(kw_b)) for g in range(ns)], axis=1)
        c_up = _dot_tn(vh_b[h], k_big)
        c_up3 = jnp.stack([c_up[:, g * 128:(g + 1) * 128] for g in range(ns)], axis=0)
        decay3 = jnp.broadcast_to(decay[h], (BLOCK, 128)).reshape(ns, L, 128)[:, 0:1, :]
        n_old = sn_ref[:, h:h + 1, :]
        sco_ref[:, h] = decay3 * c_old[h] + c_up3
        sno_ref[:, h:h + 1, :] = decay3 * n_old + jnp.sum(kw[h].reshape(ns, L, 128), axis=1, keepdims=True)
        m_out = jnp.where(lane3 == h, m_end3[h], m_out)
    smo_ref[...] = m_out

    y_m = []
    for h in hs:
        num_inter = jnp.concatenate([cq[h][g * L:(g + 1) * L, g * 128:(g + 1) * 128] for g in range(ns)], axis=0)
        n_rep = jnp.broadcast_to(sn_ref[:, h:h + 1, :], (ns, L, 128)).reshape(BLOCK, 128)
        num = num_intra[h] + w_inter[h] * num_inter
        den = jnp.sum(sc[h], axis=1, keepdims=True) + w_inter[h] * jnp.sum(qm[h] * n_rep, axis=1, keepdims=True)
        hh = num / jnp.maximum(jnp.abs(den), jnp.exp(-m_t[h]))
        hm = _sigmoid(z[:, 1792 + h * 128:1792 + (h + 1) * 128]) * hh
        y_m.append(_mlstm_head_norm(hm, gml_ref[:, h * 128:(h + 1) * 128]))

    ymix_ref[...] = jnp.concatenate([y_att] + y_m, axis=1).astype(BF16)


def _post_kernel(x_ref, y_ref, wout_ref, ln1g_ref, ln1b_ref, wg_ref, wu_ref, wd_ref, ln2g_ref, ln2b_ref, o_ref):
    o_ref[...] = _FfnStages(x_ref[...], y_ref[...], wout_ref, ln1g_ref, ln1b_ref, wg_ref, wu_ref, wd_ref,
                            ln2g_ref, ln2b_ref).run()


def _const_spec(shape):
    nd = len(shape)
    return pl.BlockSpec(shape, lambda *_: (0,) * nd, pipeline_mode=pl.Buffered(1))


def _mixer_weight_specs(qk_specs):
    return [
        _const_spec((D_MODEL, IN_PAD)),
        _const_spec((CONV_W, MLSTM_W)),
        _const_spec((1, MLSTM_W)),
    ] + qk_specs + [
        _const_spec((1, 128)),
        pl.BlockSpec(memory_space=pltpu.SMEM),
        _const_spec((1, ATT_W)),
        _const_spec((1, MLSTM_W)),
    ]


def _post_weight_specs(d):
    return [
        _const_spec((d, d)),
        _const_spec((1, d)), _const_spec((1, d)),
        _const_spec((d, D_FF)), _const_spec((d, D_FF)), _const_spec((D_FF, d)),
        _const_spec((1, d)), _const_spec((1, d)),
    ]


def _prompt_layer(x_prompt, meta_blk, mixer_w, post_w):
    b, s, d = x_prompt.shape
    qk_specs = [_const_spec((M_HEADS, M_HEAD_DIM, M_HEAD_DIM))] * 2
    state_shapes = [((BLOCK, KV_W), BF16)] * 4 + [
        ((8, MLSTM_W), F32), ((M_HEADS, M_HEAD_DIM, 2 * M_HEAD_DIM), F32), ((8, 128), F32)]
    meta_state = pl.pallas_call(
        _meta_kernel,
        grid=(1,),
        in_specs=[_const_spec((BLOCK, d))] + _mixer_weight_specs(qk_specs),
        out_specs=tuple(_const_spec(shape) for shape, _ in state_shapes),
        out_shape=tuple(jax.ShapeDtypeStruct(shape, dt) for shape, dt in state_shapes),
        scratch_shapes=[pltpu.VMEM((BLOCK, IN_PAD), F32), pltpu.VMEM((BLOCK + 8, MLSTM_W), F32)],
        compiler_params=pltpu.CompilerParams(
            dimension_semantics=("arbitrary",), vmem_limit_bytes=VMEM_LIMIT),
        name="meta_state",
    )(meta_blk, *mixer_w)

    rows = STEP_BLOCKS * BLOCK
    npair = s // rows
    total = b * npair
    steps = total + 2

    def block_map(lag):
        def index(t):
            tb = jnp.clip(t - lag, 0, total - 1)
            return (tb // npair, tb % npair, 0)
        return index

    def smap(t):
        return (jnp.clip(t - 1, 0, total - 1) // npair, 0, 0)

    out_shape = (
        jax.ShapeDtypeStruct((b, s, d), F32),
        jax.ShapeDtypeStruct((b, BLOCK, KV_W), F32),
        jax.ShapeDtypeStruct((b, BLOCK, KV_W), F32),
        jax.ShapeDtypeStruct((b, CONV_W - 1, MLSTM_W), F32),
        jax.ShapeDtypeStruct((b, M_HEADS, M_HEAD_DIM, M_HEAD_DIM), F32),
        jax.ShapeDtypeStruct((b, M_HEADS, M_HEAD_DIM), F32),
        jax.ShapeDtypeStruct((b, 8, 128), F32),
    )
    out_specs = (
        pl.BlockSpec((None, rows, d), block_map(2)),
        pl.BlockSpec((None, BLOCK, KV_W), smap),
        pl.BlockSpec((None, BLOCK, KV_W), smap),
        pl.BlockSpec((None, CONV_W - 1, MLSTM_W), smap),
        pl.BlockSpec((None, M_HEADS, M_HEAD_DIM, M_HEAD_DIM), lambda t: smap(t) + (0,)),
        pl.BlockSpec((None, M_HEADS, M_HEAD_DIM), smap),
        pl.BlockSpec((None, 8, 128), smap),
    )
    return pl.pallas_call(
        functools.partial(_pair_kernel, npair=npair),
        grid=(steps,),
        in_specs=[pl.BlockSpec((None, rows, d), block_map(0)), pl.BlockSpec((None, rows, d), block_map(2))]
        + [_const_spec(shape) for shape, _ in state_shapes]
        + _mixer_weight_specs(qk_specs) + _post_weight_specs(d),
        out_specs=out_specs,
        out_shape=out_shape,
        scratch_shapes=[
            pltpu.VMEM((rows, IN_PAD), F32),
            pltpu.VMEM((rows, IN_PAD), F32),
            pltpu.VMEM((BLOCK, 2 * KV_W), F32),
        ] + [pltpu.VMEM((BLOCK, KV_W), BF16)] * 4 + [
            pltpu.VMEM((rows + 8, MLSTM_W), F32),
            pltpu.VMEM((M_HEADS, M_HEAD_DIM, 2 * M_HEAD_DIM), F32),
            pltpu.VMEM((8, 128), F32),
            pltpu.VMEM((rows, d), BF16),
        ],
        compiler_params=pltpu.CompilerParams(
            dimension_semantics=("arbitrary",), vmem_limit_bytes=VMEM_LIMIT),
        name="prompt_layer",
    )(x_prompt, x_prompt, *meta_state, *mixer_w, *post_w)


def _sample_mixer(x_sample, ck, cv, sconv8, s_c, s_n, s_m3, mixer_w):
    n, l, d = x_sample.shape
    t = SEQ_TILE
    m3 = lambda i: (i, 0, 0)
    m4 = lambda i: (i, 0, 0, 0)
    out_shape = (
        jax.ShapeDtypeStruct((n * l, d), BF16),
        jax.ShapeDtypeStruct((n, WINDOW, KV_W), F32),
        jax.ShapeDtypeStruct((n, WINDOW, KV_W), F32),
        jax.ShapeDtypeStruct((n, CONV_W - 1, MLSTM_W), F32),
        jax.ShapeDtypeStruct((n, M_HEADS, M_HEAD_DIM, M_HEAD_DIM), F32),
        jax.ShapeDtypeStruct((n, M_HEADS, M_HEAD_DIM), F32),
        jax.ShapeDtypeStruct((n, 1, 128), F32),
    )
    out_specs = (
        pl.BlockSpec((t * l, d), lambda i: (i, 0)),
        pl.BlockSpec((t, WINDOW, KV_W), m3),
        pl.BlockSpec((t, WINDOW, KV_W), m3),
        pl.BlockSpec((t, CONV_W - 1, MLSTM_W), m3),
        pl.BlockSpec((t, M_HEADS, M_HEAD_DIM, M_HEAD_DIM), m4),
        pl.BlockSpec((t, M_HEADS, M_HEAD_DIM), m3),
        pl.BlockSpec((t, 1, 128), m3),
    )
    in_specs = [
        pl.BlockSpec((t, l, d), m3),
        pl.BlockSpec((t, WINDOW, KV_W), m3),
        pl.BlockSpec((t, WINDOW, KV_W), m3),
        pl.BlockSpec((t, CONV_W - 1, MLSTM_W), m3),
        pl.BlockSpec((t, M_HEADS, M_HEAD_DIM, M_HEAD_DIM), m4),
        pl.BlockSpec((t, M_HEADS, M_HEAD_DIM), m3),
        pl.BlockSpec((t, 1, 128), m3),
    ] + _mixer_weight_specs([_const_spec((M_HEADS, M_HEAD_DIM, 2 * M_HEAD_DIM))])
    return pl.pallas_call(
        _sample_kernel,
        grid=(n // t,),
        in_specs=in_specs,
        out_specs=out_specs,
        out_shape=out_shape,
        scratch_shapes=[pltpu.VMEM((t, 16, MLSTM_W), F32)],
        compiler_params=pltpu.CompilerParams(
            dimension_semantics=("arbitrary",), vmem_limit_bytes=VMEM_LIMIT),
        name="sample_mixer",
    )(x_sample, ck, cv, sconv8, s_c, s_n, s_m3, *mixer_w)


def _post(x_rows, y_rows, post_w):
    rows, d = x_rows.shape
    tm = min(POST_ROWS, rows)
    rmap = lambda i: (i, 0)
    in_specs = [pl.BlockSpec((tm, d), rmap), pl.BlockSpec((tm, d), rmap)] + _post_weight_specs(d)
    return pl.pallas_call(
        _post_kernel,
        grid=(rows // tm,),
        in_specs=in_specs,
        out_specs=pl.BlockSpec((tm, d), rmap),
        out_shape=jax.ShapeDtypeStruct((rows, d), F32),
        compiler_params=pltpu.CompilerParams(
            dimension_semantics=("arbitrary",), vmem_limit_bytes=VMEM_LIMIT),
        name="post_ffn",
    )(x_rows, y_rows, *post_w)


def kernel(x_prompt, x_sample, cache_k, cache_v, state_conv, state_C, state_n, state_m, meta_tokens,
           w_in, w_conv, b_conv, w_mq, w_mk, b_i, b_f, attn_sinks, g_attn, g_mlstm, w_out,
           ln1_g, ln1_b, w_gate, w_up, w_down, ln2_g, ln2_b):
    b, s, d = x_prompt.shape
    n, l, _ = x_sample.shape

    w0 = w_in[0]
    w_in_p = jnp.concatenate(
        [_pair_heads(w0[:, :ATT_W], 1), w0[:, ATT_W:], jnp.zeros((d, IN_PAD - IN_MAIN - IN_GATES), w0.dtype)],
        axis=1).astype(BF16)
    wqk = jnp.concatenate([w_mq[0], w_mk[0]], axis=-1).astype(BF16)
    gb = jnp.concatenate([b_i[0], b_f[0], jnp.zeros((128 - IN_GATES,), F32)]).reshape(1, 128)
    mixer_head = (w_in_p, w_conv[0], b_conv[0].reshape(1, MLSTM_W))
    mixer_tail = (gb, attn_sinks[0], _pair_heads(g_attn[0].reshape(1, ATT_W), 1), g_mlstm[0].reshape(1, MLSTM_W))
    wq_b = w_mq[0].astype(BF16)
    wkt_b = jnp.swapaxes(w_mk[0], 1, 2).astype(BF16)
    w_out_p = jnp.concatenate([_pair_heads(w_out[0][:ATT_W], 0), w_out[0][ATT_W:]], axis=0).astype(BF16)
    post_w = (w_out_p, ln1_g[0].reshape(1, d), ln1_b[0].reshape(1, d),
              w_gate[0].astype(BF16), w_up[0].astype(BF16), w_down[0].astype(BF16),
              ln2_g[0].reshape(1, d), ln2_b[0].reshape(1, d))

    meta_blk = jnp.concatenate([jnp.zeros((META_PAD, d), x_prompt.dtype), meta_tokens.astype(x_prompt.dtype)], axis=0)
    y_prompt, pk, pv, pconv, p_c, p_n, pm8 = _prompt_layer(
        x_prompt, meta_blk, mixer_head + (wq_b, wkt_b) + mixer_tail, post_w)

    mixer_w = mixer_head + (wqk,) + mixer_tail
    key_minor = lambda c: c[0].transpose(0, 2, 3, 1).reshape(n, KV_W, WINDOW)
    s_m3 = jnp.pad(state_m[0], ((0, 0), (0, 128 - M_HEADS))).reshape(n, 1, 128)
    ymix_s, sk, sv, scv, s_c, s_n, s_mo = _sample_mixer(
        x_sample, key_minor(cache_k), key_minor(cache_v), state_conv[0], state_C[0], state_n[0], s_m3, mixer_w)

    y_sample = _post(x_sample.reshape(n * l, d), ymix_s, post_w).reshape(n, l, d)

    kv5 = lambda a: a.reshape(1, a.shape[0], WINDOW, 2, HEAD_DIM)
    return (y_prompt, y_sample,
            kv5(pk), kv5(pv), pconv[None], p_c[None], p_n[None], pm8[None, :, :M_HEADS, 0],
            kv5(sk), kv5(sv), scv[None], s_c[None], s_n[None], s_mo[None, :, 0, :M_HEADS])
```

```python
import functools

import jax
import jax.numpy as jnp
from jax import lax
from jax.experimental import pallas as pl
from jax.experimental.pallas import tpu as pltpu

F32 = jnp.float32
BF16 = jnp.bfloat16

D_MODEL = 1024
ATT_W = 512
MLSTM_W = 512
HEAD_DIM = 64
N_HEADS = 8
KV_W = 128
WINDOW = 128
BLOCK = 128
M_HEADS = 4
M_HEAD_DIM = 128
CONV_W = 4
N_META = 16
META_PAD = BLOCK - N_META
D_FF = 2816
IN_MAIN = 2304
IN_GATES = 2 * M_HEADS
IN_PAD = IN_MAIN + 128
DEPTH = 1
ALIBI_SLOPES = tuple(2.0 ** (-8.0 * (h + 1) / N_HEADS) for h in range(N_HEADS))
DEEPNORM_ALPHA = (2.0 * DEPTH) ** 0.25
EPS = 1e-5
SEQ_TILE = 16
POST_ROWS = 512
STEP_BLOCKS = 2
FF_CHUNK = 256
FF_LOOKAHEAD = 2
PROJ_CHUNK = 256
VMEM_LIMIT = 56 * 1024 * 1024


def _pair_heads(a, axis):
    shape = a.shape
    a = a.reshape(shape[:axis] + (2, 4, HEAD_DIM) + shape[axis + 1:])
    return jnp.swapaxes(a, axis, axis + 1).reshape(shape)


def _dot(a, b):
    return jnp.dot(a, b, preferred_element_type=F32)


def _dot_nt(a, b):
    return lax.dot_general(a, b, (((1,), (1,)), ((), ())), preferred_element_type=F32)


def _dot_tn(a, b):
    return lax.dot_general(a, b, (((0,), (0,)), ((), ())), preferred_element_type=F32)


def _split2(x):
    hi = x.astype(BF16)
    lo = (x - hi.astype(F32)).astype(BF16)
    return hi, lo


def _sigmoid(x):
    return 1.0 / (1.0 + jnp.exp(-x))


def _log_sigmoid(x):
    return -(jnp.maximum(-x, 0.0) + jnp.log1p(jnp.exp(-jnp.abs(x))))


def _layer_norm(x, g, b):
    mu = jnp.mean(x, axis=-1, keepdims=True)
    xc = x - mu
    var = jnp.mean(xc * xc, axis=-1, keepdims=True)
    return xc * lax.rsqrt(var + EPS) * g + b


def _attn_head_norm(att, g):
    lo_half = lax.broadcasted_iota(jnp.int32, (att.shape[0], 128), 1) < HEAD_DIM

    def seg_mean(x):
        s_all = jnp.sum(x, axis=1, keepdims=True)
        s_lo = jnp.sum(jnp.where(lo_half, x, 0.0), axis=1, keepdims=True)
        return jnp.where(lo_half, s_lo, s_all - s_lo) * (1.0 / HEAD_DIM)

    out = []
    for grp in range(att.shape[1] // 128):
        x = att[:, grp * 128:(grp + 1) * 128]
        xc = x - seg_mean(x)
        out.append(xc * lax.rsqrt(seg_mean(xc * xc) + EPS))
    return jnp.concatenate(out, axis=1) * g


def _mlstm_head_norm(hm, g):
    mu = jnp.mean(hm, axis=-1, keepdims=True)
    xc = hm - mu
    var = jnp.mean(xc * xc, axis=-1, keepdims=True)
    return xc * lax.rsqrt(var + EPS) * g


def _softmax_parts(parts, sink):
    sp = [jnp.where(m, s * (HEAD_DIM ** -0.5) + a, -jnp.inf) for s, m, a in parts]
    mx = sink
    for s in sp:
        mx = jnp.maximum(mx, jnp.max(s, axis=1, keepdims=True))
    ps = [jnp.exp(s - mx) for s in sp]
    den = jnp.exp(sink - mx)
    for p in ps:
        den = den + jnp.sum(p, axis=1, keepdims=True)
    return ps, 1.0 / den


def _conv_silu(window, wconv_ref, bconv_ref):
    acc = bconv_ref[...]
    for j in range(CONV_W):
        acc = acc + window(j) * wconv_ref[j:j + 1, :]
    return acc * _sigmoid(acc)


def _gate_columns(gates, gb_ref, row_valid, ltri, extra=None):
    lane = lax.broadcasted_iota(jnp.int32, gates.shape, 1)
    gb = gates + gb_ref[...]
    is_i = lane < M_HEADS
    is_f = (lane >= M_HEADS) & (lane < 2 * M_HEADS)
    logf = _log_sigmoid(gb)
    if row_valid is not None:
        lf = jnp.where(is_f & row_valid, logf, 0.0)
        li = jnp.where(row_valid, gb, -jnp.inf)
    else:
        lf = jnp.where(is_f, logf, 0.0)
        li = gb
    hi_lo = jnp.concatenate(_split2(lf), axis=1)
    bc = _dot(ltri, hi_lo)
    x = jnp.where(is_i, li, bc[:, :128] + bc[:, 128:])
    if extra is None:
        return x, None
    be = _dot(extra, hi_lo)
    return x, be[:, :128] + be[:, 128:]


class _FfnStages:
    N_CHUNKS = D_FF // FF_CHUNK

    def __init__(self, x, ymix, wout_ref, ln1g_ref, ln1b_ref, wg_ref, wu_ref, wd_ref, ln2g_ref, ln2b_ref):
        self.x, self.ymix = x, ymix
        self.wout_ref, self.ln1g_ref, self.ln1b_ref = wout_ref, ln1g_ref, ln1b_ref
        self.wg_ref, self.wu_ref, self.wd_ref = wg_ref, wu_ref, wd_ref
        self.ln2g_ref, self.ln2b_ref = ln2g_ref, ln2b_ref

    def head(self):
        self.x1 = _layer_norm(DEEPNORM_ALPHA * self.x + _dot(self.ymix, self.wout_ref[...]),
                              self.ln1g_ref[...], self.ln1b_ref[...])
        self.x1b = self.x1.astype(BF16)
        self.acc = DEEPNORM_ALPHA * self.x1

    def _gate_up(self, c):
        cs = slice(c * FF_CHUNK, (c + 1) * FF_CHUNK)
        return _dot(self.x1b, self.wg_ref[:, cs]), _dot(self.x1b, self.wu_ref[:, cs])

    def chunk(self, c):
        if c == 0:
            self.gu = {}
        for n in range(c, min(c + FF_LOOKAHEAD, self.N_CHUNKS - 1) + 1):
            if n not in self.gu:
                self.gu[n] = self._gate_up(n)
        g, u = self.gu.pop(c)
        hid = (g * _sigmoid(g) * u).astype(BF16)
        self.acc = self.acc + _dot(hid, self.wd_ref[c * FF_CHUNK:(c + 1) * FF_CHUNK, :])

    def tail(self):
        return _layer_norm(self.acc, self.ln2g_ref[...], self.ln2b_ref[...])

    def run(self):
        self.head()
        for c in range(self.N_CHUNKS):
            self.chunk(c)
        return self.tail()


def _mix_blocks(zc, nb, pad_rows, thr_first, state, cbuf, wconv_ref, bconv_ref, wqk_ref, gb_ref,
                sinks_ref, gattn_ref, gml_ref, hook):
    rows = nb * BLOCK
    kp_lo, kp_hi, vp_lo, vp_hi, c_state, m_state = state
    c_state, m_state = list(c_state), list(m_state)
    lane = lax.broadcasted_iota(jnp.int32, (BLOCK, 128), 1)
    lo_half = lane < HEAD_DIM
    row_valid = None
    if pad_rows:
        row_valid = lax.broadcasted_iota(jnp.int32, (rows, 1), 0) >= pad_rows

    s_all, vmasks = [], []
    for blk in range(nb):
        rs = slice(blk * BLOCK, (blk + 1) * BLOCK)
        kb = zc[rs, ATT_W:ATT_W + KV_W].astype(BF16)
        vb = zc[rs, ATT_W + KV_W:ATT_W + 2 * KV_W].astype(BF16)
        zero_b = jnp.zeros_like(kb)
        k_lo, k_hi = jnp.where(lo_half, kb, zero_b), jnp.where(lo_half, zero_b, kb)
        v_lo, v_hi = jnp.where(lo_half, vb, zero_b), jnp.where(lo_half, zero_b, vb)
        kmask = jnp.concatenate([kp_lo, k_lo, kp_hi, k_hi], axis=0)
        vmasks.append(jnp.concatenate([vp_lo, v_lo, vp_hi, v_hi], axis=0))
        q4 = jnp.concatenate([zc[rs, r * 128:(r + 1) * 128] for r in range(4)], axis=0).astype(BF16)
        s_all.append(_dot_nt(q4, kmask))
        kp_lo, kp_hi, vp_lo, vp_hi = k_lo, k_hi, v_lo, v_hi
    hook("scores")

    c_in = zc[:, 768:1280]
    if row_valid is not None:
        c_in = jnp.where(row_valid, c_in, 0.0)
    cbuf[8:8 + rows, :] = c_in
    c_act = _conv_silu(lambda jj: cbuf[5 + jj:5 + jj + rows, :], wconv_ref, bconv_ref)
    cbuf[0:8, :] = cbuf[rows:rows + 8, :]
    hs = range(M_HEADS)
    ca_b = [c_act[:, h * 128:(h + 1) * 128].astype(BF16) for h in hs]
    qk = [_dot(ca_b[h], wqk_ref[h]) for h in hs]
    qm_b = [qk[h][:, :M_HEAD_DIM].astype(BF16) for h in hs]
    kmt_f = [(qk[h][:, M_HEAD_DIM:] * (M_HEAD_DIM ** -0.5)).T for h in hs]
    hook("front")

    ti = lax.broadcasted_iota(jnp.int32, (rows, rows), 0)
    si = lax.broadcasted_iota(jnp.int32, (rows, rows), 1)
    ltri = jnp.where((si <= ti) & ((si >> 7) == (ti >> 7)), 1.0, 0.0).astype(BF16)
    xg, _ = _gate_columns(zc[:, IN_MAIN:IN_PAD], gb_ref, row_valid, ltri)
    xgt = xg.T
    hook("gates")

    qi = lax.broadcasted_iota(jnp.int32, (BLOCK, BLOCK), 0)
    ks = lax.broadcasted_iota(jnp.int32, (BLOCK, BLOCK), 1)
    own = ks <= qi
    distf = jnp.where(own, qi - ks, WINDOW + qi - ks).astype(F32)
    zero_p = jnp.zeros((BLOCK, BLOCK), BF16)
    att_rows = []
    for blk in range(nb):
        live = None
        if blk == 0 and thr_first is not None:
            live = jnp.where(own, ks + BLOCK, ks) >= thr_first
        p_rows, inv = [], {}
        for r in range(4):
            pr = []
            for c in range(2):
                h = r + 4 * c
                s_blk = s_all[blk][r * 128:(r + 1) * 128, c * 256:(c + 1) * 256]
                sp = jnp.where(own, s_blk[:, BLOCK:], s_blk[:, :BLOCK]) * (HEAD_DIM ** -0.5) \
                    - ALIBI_SLOPES[h] * distf
                if live is not None:
                    sp = jnp.where(live, sp, -jnp.inf)
                sink = sinks_ref[h]
                mx = jnp.maximum(jnp.max(sp, axis=1, keepdims=True), sink)
                p = jnp.exp(sp - mx)
                inv[h] = 1.0 / (jnp.sum(p, axis=1, keepdims=True) + jnp.exp(sink - mx))
                p_b = p.astype(BF16)
                pr += [jnp.where(own, zero_p, p_b), jnp.where(own, p_b, zero_p)]
            p_rows.append(jnp.concatenate(pr, axis=1))
            hook("softmax", blk * 4 + r)
        o_all = _dot(jnp.concatenate(p_rows, axis=0), vmasks[blk])
        att_rows.append(jnp.concatenate(
            [o_all[r * 128:(r + 1) * 128] * jnp.where(lo_half, inv[r], inv[r + 4]) for r in range(4)], axis=1))
        hook("pv", blk)

    ones_col = jnp.where(lane == 0, 1.0, 0.0).astype(BF16)
    tb = lax.broadcasted_iota(jnp.int32, (BLOCK, BLOCK), 0)
    sb = lax.broadcasted_iota(jnp.int32, (BLOCK, BLOCK), 1)
    tri = sb <= tb
    hh_rows = [[] for _ in hs]
    for blk in range(nb):
        rs = slice(blk * BLOCK, (blk + 1) * BLOCK)
        b_c = [xg[rs, 4 + h:5 + h] for h in hs]
        b_r = [xgt[4 + h:5 + h, rs] for h in hs]
        li_r = [xgt[h:h + 1, rs] for h in hs]
        kmt = [kmt_f[h][:, rs] for h in hs]
        v_ext = [jnp.concatenate([zc[rs, 1280 + h * 128:1280 + (h + 1) * 128].astype(BF16), ones_col], axis=1)
                 for h in hs]
        s_qk = [_dot(qm_b[h][rs], kmt[h].astype(BF16)) for h in hs]
        inter = [_dot(qm_b[h][rs], c_state[h].astype(BF16)) for h in hs]
        hook("mlstm_a", blk)
        dmat = [jnp.where(tri, b_c[h] - b_r[h] + li_r[h], -jnp.inf) for h in hs]
        m_inter = [b_c[h] + m_state[h] for h in hs]
        m_t = [jnp.maximum(m_inter[h], jnp.max(dmat[h], axis=1, keepdims=True)) for h in hs]
        w_inter = [jnp.exp(m_inter[h] - m_t[h]) for h in hs]
        sc_b = [(s_qk[h] * jnp.exp(dmat[h] - m_t[h])).astype(BF16) for h in hs]
        hook("mlstm_b", blk)
        nd = [_dot(sc_b[h], v_ext[h]) + w_inter[h] * inter[h] for h in hs]
        for h in hs:
            m_end = m_t[h][BLOCK - 1:BLOCK, :]
            b_last = b_c[h][BLOCK - 1:BLOCK, :]
            decay = jnp.exp(b_last + m_state[h] - m_end)
            wk_r = jnp.exp(b_last - b_r[h] + li_r[h] - m_end)
            c_state[h] = decay * c_state[h] + _dot((kmt[h] * wk_r).astype(BF16), v_ext[h])
            m_state[h] = m_end
            hh_rows[h].append(nd[h][:, :128] / jnp.maximum(jnp.abs(nd[h][:, 128:129]), jnp.exp(-m_t[h])))
        hook("mlstm_c", blk)

    y_att = _attn_head_norm(jnp.concatenate(att_rows, axis=0), gattn_ref[...])
    hook("attnorm")
    y_m = []
    for h in hs:
        hm = _sigmoid(zc[:, 1792 + h * 128:1792 + (h + 1) * 128]) * jnp.concatenate(hh_rows[h], axis=0)
        y_m.append(_mlstm_head_norm(hm, gml_ref[:, h * 128:(h + 1) * 128]))
        hook("headnorm", h)
    ymix = jnp.concatenate([y_att] + y_m, axis=1).astype(BF16)
    return ymix, (kp_lo, kp_hi, vp_lo, vp_hi, c_state, m_state)


def _project(xb, w_in_refs, lo=0, hi=IN_PAD):
    bounds = (0, ATT_W, IN_MAIN, IN_PAD)
    parts = []
    for ref, start, stop in zip(w_in_refs, bounds[:-1], bounds[1:]):
        a, b = max(lo, start), min(hi, stop)
        if a < b:
            parts.append(_dot(xb, ref[:, a - start:b - start]))
    return parts[0] if len(parts) == 1 else jnp.concatenate(parts, axis=1)


def _meta_kernel(meta_ref, w_inq_ref, w_inr_ref, w_ing_ref, wconv_ref, bconv_ref, wqk_ref, gb_ref,
                 sinks_ref, gattn_ref,
                 gml_ref, kplo_o, kphi_o, vplo_o, vphi_o, cb_o, ctn_o, mst_o, zc, cbuf):
    zc[...] = _project(meta_ref[...].astype(BF16), (w_inq_ref, w_inr_ref, w_ing_ref))
    cbuf[0:8, :] = jnp.zeros((8, MLSTM_W), F32)
    zb = jnp.zeros((BLOCK, KV_W), BF16)
    state = (zb, zb, zb, zb, [jnp.zeros((M_HEAD_DIM, 2 * M_HEAD_DIM), F32)] * M_HEADS,
             [jnp.zeros((1, 1), F32)] * M_HEADS)
    _, (kp_lo, kp_hi, vp_lo, vp_hi, c_state, m_state) = _mix_blocks(
        zc, 1, META_PAD, BLOCK + META_PAD, state, cbuf, wconv_ref, bconv_ref, wqk_ref, gb_ref,
        sinks_ref, gattn_ref, gml_ref, lambda name, index=0: None)
    kplo_o[...] = kp_lo
    kphi_o[...] = kp_hi
    vplo_o[...] = vp_lo
    vphi_o[...] = vp_hi
    cb_o[...] = cbuf[0:8, :]
    for h in range(M_HEADS):
        ctn_o[h] = c_state[h]
        mst_o[h:h + 1, :] = jnp.broadcast_to(m_state[h], (1, 128))
    mst_o[M_HEADS:8, :] = jnp.zeros((8 - M_HEADS, 128), F32)


def _pair_kernel(xin_ref, xres_ref, kplo_i, kphi_i, vplo_i, vphi_i, cb_i, ctn_i, mst_i,
                 w_inq_ref, w_inr_ref, w_ing_ref, wconv_ref, bconv_ref, wqk_ref, gb_ref, sinks_ref,
                 gattn_ref, gml_ref,
                 wout_ref, ln1g_ref, ln1b_ref, wg_ref, wu_ref, wd_ref, ln2g_ref, ln2b_ref,
                 y_ref, pk_ref, pv_ref, pconv_ref, pc_ref, pn_ref, pm_ref,
                 zcur, znext, kvlast, kplo, kphi, vplo, vphi, cbuf, ctn, mst, yprev, *, npair):
    t = pl.program_id(0)
    p = lax.rem(t + (npair - 1), npair)

    @pl.when(t == 0)
    def _init_pipeline():
        zcur[...] = jnp.zeros_like(zcur)
        yprev[...] = jnp.zeros_like(yprev)

    @pl.when((p == 0) | (t == 0))
    def _load_meta_state():
        kplo[...] = kplo_i[...]
        kphi[...] = kphi_i[...]
        vplo[...] = vplo_i[...]
        vphi[...] = vphi_i[...]
        cbuf[0:8, :] = cb_i[...]
        ctn[...] = ctn_i[...]
        mst[...] = mst_i[...]

    ffn = _FfnStages(xres_ref[...], yprev[...], wout_ref, ln1g_ref, ln1b_ref, wg_ref, wu_ref, wd_ref,
                     ln2g_ref, ln2b_ref)
    xb_b = xin_ref[...].astype(BF16)

    n_proj = -(-IN_PAD // PROJ_CHUNK)

    def proj(n):
        lo, hi = n * PROJ_CHUNK, min((n + 1) * PROJ_CHUNK, IN_PAD)
        znext[:, lo:hi] = _project(xb_b, (w_inq_ref, w_inr_ref, w_ing_ref), lo, hi)

    proj_it, ffn_it = iter(range(n_proj)), iter(range(ffn.N_CHUNKS))

    def fill_proj(n):
        for c in [c for _, c in zip(range(n), proj_it)]:
            proj(c)

    def fill_ffn(n):
        for c in [c for _, c in zip(range(n), ffn_it)]:
            ffn.chunk(c)

    out = {}

    def hook(name, index=0):
        if name == "scores":
            fill_proj(2)
            ffn.head()
        elif name == "front":
            fill_proj(1)
        elif name == "gates":
            fill_ffn(1)
        elif name == "softmax":
            if index % 2 == 1:
                fill_ffn(1)
        elif name in ("pv", "mlstm_a", "mlstm_b"):
            fill_ffn(1)
        elif name == "mlstm_c":
            fill_proj(1)
        elif name == "attnorm":
            fill_ffn(ffn.N_CHUNKS)
            out["y"] = ffn.tail()
            fill_proj(1)
        elif name == "headnorm":
            fill_proj(1)

    state = (kplo[...], kphi[...], vplo[...], vphi[...], [ctn[h] for h in range(M_HEADS)],
             [mst[h:h + 1, 0:1] for h in range(M_HEADS)])
    thr_first = jnp.where(p == 0, META_PAD, 0)
    kvlast[...] = zcur[(STEP_BLOCKS - 1) * BLOCK:STEP_BLOCKS * BLOCK, ATT_W:ATT_W + 2 * KV_W]
    ymix, (kp_lo, kp_hi, vp_lo, vp_hi, c_state, m_state) = _mix_blocks(
        zcur, STEP_BLOCKS, 0, thr_first, state, cbuf, wconv_ref, bconv_ref, wqk_ref, gb_ref,
        sinks_ref, gattn_ref, gml_ref, hook)
    fill_proj(n_proj)
    y_ref[...] = out["y"]
    kplo[...] = kp_lo
    kphi[...] = kp_hi
    vplo[...] = vp_lo
    vphi[...] = vp_hi
    for h in range(M_HEADS):
        ctn[h] = c_state[h]
        mst[h:h + 1, :] = jnp.broadcast_to(m_state[h], (1, 128))
    yprev[...] = ymix
    zcur[...] = znext[...]

    @pl.when(p == npair - 1)
    def _final():
        pk_ref[...] = kvlast[:, 0:KV_W]
        pv_ref[...] = kvlast[:, KV_W:2 * KV_W]
        pconv_ref[...] = cbuf[8 - (CONV_W - 1):8, :]
        pm_ref[...] = mst[...]
        for h in range(M_HEADS):
            c_n = ctn[h].T
            pc_ref[h] = c_n[0:M_HEAD_DIM, :]
            pn_ref[h:h + 1, :] = c_n[M_HEAD_DIM:M_HEAD_DIM + 1, :]


def _sample_kernel(xs_ref, ck_ref, cv_ref, sconv_ref, sc_ref, sn_ref, sm_ref,
                   w_inq_ref, w_inr_ref, w_ing_ref, wconv_ref, bconv_ref, wqk_ref, gb_ref, sinks_ref,
                   gattn_ref, gml_ref,
                   ymix_ref, sk_ref, sv_ref, scv_ref, sco_ref, sno_ref, smo_ref, cext):
    ns, L = SEQ_TILE, 8
    xs = xs_ref[...].reshape(ns * L, D_MODEL)
    z = _project(xs.astype(BF16), (w_inq_ref, w_inr_ref, w_ing_ref))

    lane = lax.broadcasted_iota(jnp.int32, (BLOCK, 128), 1)
    lo_half = lane < HEAD_DIM
    lo3 = lax.broadcasted_iota(jnp.int32, (ns, L, 128), 2) < HEAD_DIM
    ti = lax.broadcasted_iota(jnp.int32, (BLOCK, BLOCK), 0)
    si = lax.broadcasted_iota(jnp.int32, (BLOCK, BLOCK), 1)
    same_seq = (ti >> 3) == (si >> 3)
    tl, sl_ = ti & 7, si & 7
    causal_seq = same_seq & (sl_ <= tl)

    k_new = z[:, ATT_W:ATT_W + KV_W]
    v_new = z[:, ATT_W + KV_W:ATT_W + 2 * KV_W]
    kb, vb = k_new.astype(BF16), v_new.astype(BF16)
    zero_b = jnp.zeros_like(kb)
    kn_mask = jnp.concatenate([jnp.where(lo_half, kb, zero_b), jnp.where(lo_half, zero_b, kb)], axis=0)
    vn_mask = jnp.concatenate([jnp.where(lo_half, vb, zero_b), jnp.where(lo_half, zero_b, vb)], axis=0)
    qg = [z[:, r * 128:(r + 1) * 128] for r in range(4)]
    s_new = _dot_nt(jnp.concatenate(qg, axis=0).astype(BF16), kn_mask)
    qg3 = [q.reshape(ns, L, 128) for q in qg]
    q8 = jnp.concatenate([jnp.where(lo3, q, 0.0) for q in qg3] + [jnp.where(lo3, 0.0, q) for q in qg3],
                         axis=1).astype(BF16)
    s_cache = jnp.einsum('nqd,ndk->nqk', q8, ck_ref[...].astype(BF16), preferred_element_type=F32)

    c3 = z[:, 768:1280].reshape(ns, L, MLSTM_W)
    cext[:, 8 - (CONV_W - 1):8, :] = sconv_ref[...]
    cext[:, 8:16, :] = c3
    scv_ref[...] = c3[:, L - (CONV_W - 1):L, :]
    c_act = _conv_silu(lambda jj: cext[:, 5 + jj:5 + jj + L, :], wconv_ref, bconv_ref)
    c_act = c_act.reshape(BLOCK, MLSTM_W)
    lblk = jnp.where(causal_seq, 1.0, 0.0).astype(BF16)
    lseq = jnp.where(same_seq, 1.0, 0.0).astype(BF16)
    xg, blast = _gate_columns(z[:, IN_MAIN:IN_PAD], gb_ref, None, lblk, extra=lseq)
    xgt = xg.T
    m_rep = jnp.broadcast_to(sm_ref[...], (ns, L, 128)).reshape(BLOCK, 128)
    hs = range(M_HEADS)
    qk = [_dot(c_act[:, h * 128:(h + 1) * 128].astype(BF16), wqk_ref[h]) for h in hs]
    qm = [qk[h][:, :128] for h in hs]
    qm_b = [qm[h].astype(BF16) for h in hs]
    km = [qk[h][:, 128:] * (M_HEAD_DIM ** -0.5) for h in hs]
    km_b = [km[h].astype(BF16) for h in hs]
    vh_b = [z[:, 1280 + h * 128:1280 + (h + 1) * 128].astype(BF16) for h in hs]
    c_old = [sc_ref[:, h] for h in hs]
    s_qk = [_dot_nt(qm_b[h], km_b[h]) for h in hs]
    cq = [_dot_nt(qm_b[h], c_old[h].reshape(ns * 128, 128).astype(BF16)) for h in hs]

    mask_c = si > tl
    dist_c = (WINDOW + tl - si).astype(F32)
    dist_n = (tl - sl_).astype(F32)
    pc_parts, pn_rows, inv = {}, [], {}
    for r in range(4):
        pn = []
        for c in range(2):
            h = r + 4 * c
            sc_c = s_cache[:, c * 32 + r * L:c * 32 + (r + 1) * L, :].reshape(BLOCK, 128)
            sc_n = s_new[r * 128:(r + 1) * 128, c * 128:(c + 1) * 128]
            (p_c, p_n), inv[h] = _softmax_parts(
                [(sc_c, mask_c, -ALIBI_SLOPES[h] * dist_c), (sc_n, causal_seq, -ALIBI_SLOPES[h] * dist_n)],
                sinks_ref[h])
            pc_parts[(c, r)] = p_c.reshape(ns, L, 128)
            pn.append(p_n.astype(BF16))
        pn_rows.append(jnp.concatenate(pn, axis=1))
    p8 = jnp.concatenate([pc_parts[(c, r)] for c in range(2) for r in range(4)], axis=1).astype(BF16)
    o_cache = jnp.einsum('nqk,ndk->nqd', p8, cv_ref[...].astype(BF16), preferred_element_type=F32)
    o_new = _dot(jnp.concatenate(pn_rows, axis=0), vn_mask)
    groups = []
    for r in range(4):
        oa = o_cache[:, r * L:(r + 1) * L, :].reshape(BLOCK, 128)
        ob = o_cache[:, 32 + r * L:32 + (r + 1) * L, :].reshape(BLOCK, 128)
        o = jnp.where(lo_half, oa, ob) + o_new[r * 128:(r + 1) * 128]
        groups.append(o * jnp.where(lo_half, inv[r], inv[r + 4]))
    y_att = _attn_head_norm(jnp.concatenate(groups, axis=1), gattn_ref[...])

    sk_ref[:, 0:WINDOW - L, :] = jnp.swapaxes(ck_ref[...], 1, 2)[:, L:WINDOW, :]
    sk_ref[:, WINDOW - L:WINDOW, :] = k_new.reshape(ns, L, 128)
    sv_ref[:, 0:WINDOW - L, :] = jnp.swapaxes(cv_ref[...], 1, 2)[:, L:WINDOW, :]
    sv_ref[:, WINDOW - L:WINDOW, :] = v_new.reshape(ns, L, 128)

    b_c = [xg[:, 4 + h:5 + h] for h in hs]
    li_c = [xg[:, h:h + 1] for h in hs]
    b_r = [xgt[4 + h:5 + h, :] for h in hs]
    li_r = [xgt[h:h + 1, :] for h in hs]
    bl_c = [blast[:, 4 + h:5 + h] for h in hs]
    m_prev = [m_rep[:, h:h + 1] for h in hs]
    dmat = [jnp.where(causal_seq, b_c[h] - b_r[h] + li_r[h], -jnp.inf) for h in hs]
    m_inter = [b_c[h] + m_prev[h] for h in hs]
    m_t = [jnp.maximum(m_inter[h], jnp.max(dmat[h], axis=1, keepdims=True)) for h in hs]
    w_inter = [jnp.exp(m_inter[h] - m_t[h]) for h in hs]
    sc = [s_qk[h] * jnp.exp(dmat[h] - m_t[h]) for h in hs]
    num_intra = [_dot(sc[h].astype(BF16), vh_b[h]) for h in hs]

    m_end3 = [jnp.broadcast_to(m_t[h], (BLOCK, 128)).reshape(ns, L, 128)[:, L - 1:L, :] for h in hs]
    m_end = [jnp.broadcast_to(m_end3[h], (ns, L, 128)).reshape(BLOCK, 128)[:, 0:1] for h in hs]
    decay = [jnp.exp(bl_c[h] + m_prev[h] - m_end[h]) for h in hs]
    kw = [km[h] * jnp.exp(bl_c[h] - b_c[h] + li_c[h] - m_end[h]) for h in hs]
    seq_of_row = ti >> 3
    lane3 = lax.broadcasted_iota(jnp.int32, (ns, 1, 128), 2)
    m_out = jnp.zeros((ns, 1, 128), F32)
    for h in hs:
        kw_b = kw[h].astype(BF16)
        k_big = jnp.concatenate(
            [jnp.where(seq_of_row == g, kw_b, jnp.zeros_like(kw_b)) for g in range(ns)], axis=1)
        c_up = _dot_tn(vh_b[h], k_big)
        c_up3 = jnp.stack([c_up[:, g * 128:(g + 1) * 128] for g in range(ns)], axis=0)
        decay3 = jnp.broadcast_to(decay[h], (BLOCK, 128)).reshape(ns, L, 128)[:, 0:1, :]
        n_old = sn_ref[:, h:h + 1, :]
        sco_ref[:, h] = decay3 * c_old[h] + c_up3
        sno_ref[:, h:h + 1, :] = decay3 * n_old + jnp.sum(kw[h].reshape(ns, L, 128), axis=1, keepdims=True)
        m_out = jnp.where(lane3 == h, m_end3[h], m_out)
    smo_ref[...] = m_out

    y_m = []
    for h in hs:
        num_inter = jnp.concatenate([cq[h][g * L:(g + 1) * L, g * 128:(g + 1) * 128] for g in range(ns)], axis=0)
        n_rep = jnp.broadcast_to(sn_ref[:, h:h + 1, :], (ns, L, 128)).reshape(BLOCK, 128)
        num = num_intra[h] + w_inter[h] * num_inter
        den = jnp.sum(sc[h], axis=1, keepdims=True) + w_inter[h] * jnp.sum(qm[h] * n_rep, axis=1, keepdims=True)
        hh = num / jnp.maximum(jnp.abs(den), jnp.exp(-m_t[h]))
        hm = _sigmoid(z[:, 1792 + h * 128:1792 + (h + 1) * 128]) * hh
        y_m.append(_mlstm_head_norm(hm, gml_ref[:, h * 128:(h + 1) * 128]))

    ymix_ref[...] = jnp.concatenate([y_att] + y_m, axis=1).astype(BF16)


def _post_kernel(x_ref, y_ref, wout_ref, ln1g_ref, ln1b_ref, wg_ref, wu_ref, wd_ref, ln2g_ref, ln2b_ref, o_ref):
    o_ref[...] = _FfnStages(x_ref[...], y_ref[...], wout_ref, ln1g_ref, ln1b_ref, wg_ref, wu_ref, wd_ref,
                            ln2g_ref, ln2b_ref).run()


def _const_spec(shape):
    nd = len(shape)
    return pl.BlockSpec(shape, lambda *_: (0,) * nd, pipeline_mode=pl.Buffered(1))


def _mixer_weight_specs():
    return [
        _const_spec((D_MODEL, ATT_W)),
        _const_spec((D_MODEL, IN_MAIN - ATT_W)),
        _const_spec((D_MODEL, IN_PAD - IN_MAIN)),
        _const_spec((CONV_W, MLSTM_W)),
        _const_spec((1, MLSTM_W)),
        _const_spec((M_HEADS, M_HEAD_DIM, 2 * M_HEAD_DIM)),
        _const_spec((1, 128)),
        pl.BlockSpec(memory_space=pltpu.SMEM),
        _const_spec((1, ATT_W)),
        _const_spec((1, MLSTM_W)),
    ]


def _post_weight_specs(d):
    return [
        _const_spec((d, d)),
        _const_spec((1, d)), _const_spec((1, d)),
        _const_spec((d, D_FF)), _const_spec((d, D_FF)), _const_spec((D_FF, d)),
        _const_spec((1, d)), _const_spec((1, d)),
    ]


def _prompt_layer(x_prompt, meta_blk, mixer_w, post_w):
    b, s, d = x_prompt.shape
    state_shapes = [((BLOCK, KV_W), BF16)] * 4 + [
        ((8, MLSTM_W), F32), ((M_HEADS, M_HEAD_DIM, 2 * M_HEAD_DIM), F32), ((8, 128), F32)]
    meta_state = pl.pallas_call(
        _meta_kernel,
        grid=(1,),
        in_specs=[_const_spec((BLOCK, d))] + _mixer_weight_specs(),
        out_specs=tuple(_const_spec(shape) for shape, _ in state_shapes),
        out_shape=tuple(jax.ShapeDtypeStruct(shape, dt) for shape, dt in state_shapes),
        scratch_shapes=[pltpu.VMEM((BLOCK, IN_PAD), F32), pltpu.VMEM((BLOCK + 8, MLSTM_W), F32)],
        compiler_params=pltpu.CompilerParams(
            dimension_semantics=("arbitrary",), vmem_limit_bytes=VMEM_LIMIT),
        name="meta_state",
    )(meta_blk, *mixer_w)

    rows = STEP_BLOCKS * BLOCK
    npair = s // rows
    total = b * npair
    steps = total + 2

    def block_map(lag):
        def index(t):
            tb = jnp.clip(t - lag, 0, total - 1)
            return (tb // npair, tb % npair, 0)
        return index

    def smap(t):
        return (jnp.clip(t - 1, 0, total - 1) // npair, 0, 0)

    out_shape = (
        jax.ShapeDtypeStruct((b, s, d), F32),
        jax.ShapeDtypeStruct((b, BLOCK, KV_W), F32),
        jax.ShapeDtypeStruct((b, BLOCK, KV_W), F32),
        jax.ShapeDtypeStruct((b, CONV_W - 1, MLSTM_W), F32),
        jax.ShapeDtypeStruct((b, M_HEADS, M_HEAD_DIM, M_HEAD_DIM), F32),
        jax.ShapeDtypeStruct((b, M_HEADS, M_HEAD_DIM), F32),
        jax.ShapeDtypeStruct((b, 8, 128), F32),
    )
    out_specs = (
        pl.BlockSpec((None, rows, d), block_map(2)),
        pl.BlockSpec((None, BLOCK, KV_W), smap),
        pl.BlockSpec((None, BLOCK, KV_W), smap),
        pl.BlockSpec((None, CONV_W - 1, MLSTM_W), smap),
        pl.BlockSpec((None, M_HEADS, M_HEAD_DIM, M_HEAD_DIM), lambda t: smap(t) + (0,)),
        pl.BlockSpec((None, M_HEADS, M_HEAD_DIM), smap),
        pl.BlockSpec((None, 8, 128), smap),
    )
    return pl.pallas_call(
        functools.partial(_pair_kernel, npair=npair),
        grid=(steps,),
        in_specs=[pl.BlockSpec((None, rows, d), block_map(0)), pl.BlockSpec((None, rows, d), block_map(2))]
        + [_const_spec(shape) for shape, _ in state_shapes]
        + _mixer_weight_specs() + _post_weight_specs(d),
        out_specs=out_specs,
        out_shape=out_shape,
        scratch_shapes=[
            pltpu.VMEM((rows, IN_PAD), F32),
            pltpu.VMEM((rows, IN_PAD), F32),
            pltpu.VMEM((BLOCK, 2 * KV_W), F32),
        ] + [pltpu.VMEM((BLOCK, KV_W), BF16)] * 4 + [
            pltpu.VMEM((rows + 8, MLSTM_W), F32),
            pltpu.VMEM((M_HEADS, M_HEAD_DIM, 2 * M_HEAD_DIM), F32),
            pltpu.VMEM((8, 128), F32),
            pltpu.VMEM((rows, d), BF16),
        ],
        compiler_params=pltpu.CompilerParams(
            dimension_semantics=("arbitrary",), vmem_limit_bytes=VMEM_LIMIT),
        name="prompt_layer",
    )(x_prompt, x_prompt, *meta_state, *mixer_w, *post_w)


def _sample_mixer(x_sample, ck, cv, sconv8, s_c, s_n, s_m3, mixer_w):
    n, l, d = x_sample.shape
    t = SEQ_TILE
    m3 = lambda i: (i, 0, 0)
    m4 = lambda i: (i, 0, 0, 0)
    out_shape = (
        jax.ShapeDtypeStruct((n * l, d), BF16),
        jax.ShapeDtypeStruct((n, WINDOW, KV_W), F32),
        jax.ShapeDtypeStruct((n, WINDOW, KV_W), F32),
        jax.ShapeDtypeStruct((n, CONV_W - 1, MLSTM_W), F32),
        jax.ShapeDtypeStruct((n, M_HEADS, M_HEAD_DIM, M_HEAD_DIM), F32),
        jax.ShapeDtypeStruct((n, M_HEADS, M_HEAD_DIM), F32),
        jax.ShapeDtypeStruct((n, 1, 128), F32),
    )
    out_specs = (
        pl.BlockSpec((t * l, d), lambda i: (i, 0)),
        pl.BlockSpec((t, WINDOW, KV_W), m3),
        pl.BlockSpec((t, WINDOW, KV_W), m3),
        pl.BlockSpec((t, CONV_W - 1, MLSTM_W), m3),
        pl.BlockSpec((t, M_HEADS, M_HEAD_DIM, M_HEAD_DIM), m4),
        pl.BlockSpec((t, M_HEADS, M_HEAD_DIM), m3),
        pl.BlockSpec((t, 1, 128), m3),
    )
    in_specs = [
        pl.BlockSpec((t, l, d), m3),
        pl.BlockSpec((t, WINDOW, KV_W), m3),
        pl.BlockSpec((t, WINDOW, KV_W), m3),
        pl.BlockSpec((t, CONV_W - 1, MLSTM_W), m3),
        pl.BlockSpec((t, M_HEADS, M_HEAD_DIM, M_HEAD_DIM), m4),
        pl.BlockSpec((t, M_HEADS, M_HEAD_DIM), m3),
        pl.BlockSpec((t, 1, 128), m3),
    ] + _mixer_weight_specs()
    return pl.pallas_call(
        _sample_kernel,
        grid=(n // t,),
        in_specs=in_specs,
        out_specs=out_specs,
        out_shape=out_shape,
        scratch_shapes=[pltpu.VMEM((t, 16, MLSTM_W), F32)],
        compiler_params=pltpu.CompilerParams(
            dimension_semantics=("arbitrary",), vmem_limit_bytes=VMEM_LIMIT),
        name="sample_mixer",
    )(x_sample, ck, cv, sconv8, s_c, s_n, s_m3, *mixer_w)


def _post(x_rows, y_rows, post_w):
    rows, d = x_rows.shape
    tm = min(POST_ROWS, rows)
    rmap = lambda i: (i, 0)
    in_specs = [pl.BlockSpec((tm, d), rmap), pl.BlockSpec((tm, d), rmap)] + _post_weight_specs(d)
    return pl.pallas_call(
        _post_kernel,
        grid=(rows // tm,),
        in_specs=in_specs,
        out_specs=pl.BlockSpec((tm, d), rmap),
        out_shape=jax.ShapeDtypeStruct((rows, d), F32),
        compiler_params=pltpu.CompilerParams(
            dimension_semantics=("arbitrary",), vmem_limit_bytes=VMEM_LIMIT),
        name="post_ffn",
    )(x_rows, y_rows, *post_w)


def kernel(x_prompt, x_sample, cache_k, cache_v, state_conv, state_C, state_n, state_m, meta_tokens,
           w_in, w_conv, b_conv, w_mq, w_mk, b_i, b_f, attn_sinks, g_attn, g_mlstm, w_out,
           ln1_g, ln1_b, w_gate, w_up, w_down, ln2_g, ln2_b):
    b, s, d = x_prompt.shape
    n, l, _ = x_sample.shape

    w0 = w_in[0]
    w_in_parts = (_pair_heads(w0[:, :ATT_W], 1).astype(BF16), w0[:, ATT_W:IN_MAIN].astype(BF16),
                  jnp.pad(w0[:, IN_MAIN:], ((0, 0), (0, IN_PAD - IN_MAIN - IN_GATES))).astype(BF16))
    wqk = jnp.concatenate([w_mq[0], w_mk[0]], axis=-1).astype(BF16)
    gb = jnp.concatenate([b_i[0], b_f[0], jnp.zeros((128 - IN_GATES,), F32)]).reshape(1, 128)
    mixer_w = w_in_parts + (w_conv[0], b_conv[0].reshape(1, MLSTM_W), wqk, gb, attn_sinks[0],
                            _pair_heads(g_attn[0].reshape(1, ATT_W), 1), g_mlstm[0].reshape(1, MLSTM_W))
    w_out_p = jnp.concatenate([_pair_heads(w_out[0][:ATT_W], 0), w_out[0][ATT_W:]], axis=0).astype(BF16)
    post_w = (w_out_p, ln1_g[0].reshape(1, d), ln1_b[0].reshape(1, d),
              w_gate[0].astype(BF16), w_up[0].astype(BF16), w_down[0].astype(BF16),
              ln2_g[0].reshape(1, d), ln2_b[0].reshape(1, d))

    meta_blk = jnp.concatenate([jnp.zeros((META_PAD, d), x_prompt.dtype), meta_tokens.astype(x_prompt.dtype)], axis=0)
    y_prompt, pk, pv, pconv, p_c, p_n, pm8 = _prompt_layer(x_prompt, meta_blk, mixer_w, post_w)

    key_minor = lambda c: c[0].transpose(0, 2, 3, 1).reshape(n, KV_W, WINDOW)
    s_m3 = jnp.pad(state_m[0], ((0, 0), (0, 128 - M_HEADS))).reshape(n, 1, 128)
    ymix_s, sk, sv, scv, s_c, s_n, s_mo = _sample_mixer(
        x_sample, key_minor(cache_k), key_minor(cache_v), state_conv[0], state_C[0], state_n[0], s_m3, mixer_w)

    y_sample = _post(x_sample.reshape(n * l, d), ymix_s, post_w).reshape(n, l, d)

    kv5 = lambda a: a.reshape(1, a.shape[0], WINDOW, 2, HEAD_DIM)
    return (y_prompt, y_sample,
            kv5(pk), kv5(pv), pconv[None], p_c[None], p_n[None], pm8[None, :, :M_HEADS, 0],
            kv5(sk), kv5(sv), scv[None], s_c[None], s_n[None], s_mo[None, :, 0, :M_HEADS])
```

```python
import functools

import jax
import jax.numpy as jnp
from jax import lax
from jax.experimental import pallas as pl
from jax.experimental.pallas import tpu as pltpu

F32 = jnp.float32
BF16 = jnp.bfloat16

D_MODEL = 1024
ATT_W = 512
MLSTM_W = 512
HEAD_DIM = 64
N_HEADS = 8
KV_W = 128
WINDOW = 128
BLOCK = 128
M_HEADS = 4
M_HEAD_DIM = 128
CONV_W = 4
N_META = 16
META_PAD = BLOCK - N_META
D_FF = 2816
IN_MAIN = 2304
IN_GATES = 2 * M_HEADS
IN_PAD = IN_MAIN + 128
DEPTH = 1
ALIBI_SLOPES = tuple(2.0 ** (-8.0 * (h + 1) / N_HEADS) for h in range(N_HEADS))
DEEPNORM_ALPHA = (2.0 * DEPTH) ** 0.25
EPS = 1e-5
SEQ_TILE = 16
POST_ROWS = 512
STEP_BLOCKS = 2
FF_CHUNK = 256
FF_LOOKAHEAD = 2
PROJ_CHUNK = 256
VMEM_LIMIT = 56 * 1024 * 1024


def _pair_heads(a, axis):
    shape = a.shape
    a = a.reshape(shape[:axis] + (2, 4, HEAD_DIM) + shape[axis + 1:])
    return jnp.swapaxes(a, axis, axis + 1).reshape(shape)


def _dot(a, b):
    return jnp.dot(a, b, preferred_element_type=F32)


def _dot_nt(a, b):
    return lax.dot_general(a, b, (((1,), (1,)), ((), ())), preferred_element_type=F32)


def _dot_tn(a, b):
    return lax.dot_general(a, b, (((0,), (0,)), ((), ())), preferred_element_type=F32)


def _split2(x):
    hi = x.astype(BF16)
    lo = (x - hi.astype(F32)).astype(BF16)
    return hi, lo


def _sigmoid(x):
    return 1.0 / (1.0 + jnp.exp(-x))


def _log_sigmoid(x):
    return -(jnp.maximum(-x, 0.0) + jnp.log1p(jnp.exp(-jnp.abs(x))))


def _layer_norm(x, g, b):
    mu = jnp.mean(x, axis=-1, keepdims=True)
    xc = x - mu
    var = jnp.mean(xc * xc, axis=-1, keepdims=True)
    return xc * lax.rsqrt(var + EPS) * g + b


def _attn_head_norm(att, g):
    lo_half = lax.broadcasted_iota(jnp.int32, (att.shape[0], 128), 1) < HEAD_DIM

    def seg_mean(x):
        s_all = jnp.sum(x, axis=1, keepdims=True)
        s_lo = jnp.sum(jnp.where(lo_half, x, 0.0), axis=1, keepdims=True)
        return jnp.where(lo_half, s_lo, s_all - s_lo) * (1.0 / HEAD_DIM)

    out = []
    for grp in range(att.shape[1] // 128):
        x = att[:, grp * 128:(grp + 1) * 128]
        xc = x - seg_mean(x)
        out.append(xc * lax.rsqrt(seg_mean(xc * xc) + EPS))
    return jnp.concatenate(out, axis=1) * g


def _mlstm_head_norm(hm, g):
    mu = jnp.mean(hm, axis=-1, keepdims=True)
    xc = hm - mu
    var = jnp.mean(xc * xc, axis=-1, keepdims=True)
    return xc * lax.rsqrt(var + EPS) * g


def _softmax_parts(parts, sink):
    sp = [jnp.where(m, s * (HEAD_DIM ** -0.5) + a, -jnp.inf) for s, m, a in parts]
    mx = sink
    for s in sp:
        mx = jnp.maximum(mx, jnp.max(s, axis=1, keepdims=True))
    ps = [jnp.exp(s - mx) for s in sp]
    den = jnp.exp(sink - mx)
    for p in ps:
        den = den + jnp.sum(p, axis=1, keepdims=True)
    return ps, 1.0 / den


def _conv_silu(window, wconv_ref, bconv_ref):
    acc = bconv_ref[...]
    for j in range(CONV_W):
        acc = acc + window(j) * wconv_ref[j:j + 1, :]
    return acc * _sigmoid(acc)


def _gate_columns(gates, gb_ref, row_valid, ltri, extra=None):
    lane = lax.broadcasted_iota(jnp.int32, gates.shape, 1)
    gb = gates + gb_ref[...]
    is_i = lane < M_HEADS
    is_f = (lane >= M_HEADS) & (lane < 2 * M_HEADS)
    logf = _log_sigmoid(gb)
    if row_valid is not None:
        lf = jnp.where(is_f & row_valid, logf, 0.0)
        li = jnp.where(row_valid, gb, -jnp.inf)
    else:
        lf = jnp.where(is_f, logf, 0.0)
        li = gb
    hi_lo = jnp.concatenate(_split2(lf), axis=1)
    bc = _dot(ltri, hi_lo)
    x = jnp.where(is_i, li, bc[:, :128] + bc[:, 128:])
    if extra is None:
        return x, None
    be = _dot(extra, hi_lo)
    return x, be[:, :128] + be[:, 128:]


class _FfnStages:
    N_CHUNKS = D_FF // FF_CHUNK

    def __init__(self, x, ymix, wout_ref, ln1g_ref, ln1b_ref, wg_ref, wu_ref, wd_ref, ln2g_ref, ln2b_ref):
        self.x, self.ymix = x, ymix
        self.wout_ref, self.ln1g_ref, self.ln1b_ref = wout_ref, ln1g_ref, ln1b_ref
        self.wg_ref, self.wu_ref, self.wd_ref = wg_ref, wu_ref, wd_ref
        self.ln2g_ref, self.ln2b_ref = ln2g_ref, ln2b_ref

    def head(self):
        self.x1 = _layer_norm(DEEPNORM_ALPHA * self.x + _dot(self.ymix, self.wout_ref[...]),
                              self.ln1g_ref[...], self.ln1b_ref[...])
        self.x1b = self.x1.astype(BF16)
        self.acc = DEEPNORM_ALPHA * self.x1

    def _gate_up(self, c):
        cs = slice(c * FF_CHUNK, (c + 1) * FF_CHUNK)
        return _dot(self.x1b, self.wg_ref[:, cs]), _dot(self.x1b, self.wu_ref[:, cs])

    def chunk(self, c):
        if c == 0:
            self.gu = {}
        for n in range(c, min(c + FF_LOOKAHEAD, self.N_CHUNKS - 1) + 1):
            if n not in self.gu:
                self.gu[n] = self._gate_up(n)
        g, u = self.gu.pop(c)
        hid = (g * _sigmoid(g) * u).astype(BF16)
        self.acc = self.acc + _dot(hid, self.wd_ref[c * FF_CHUNK:(c + 1) * FF_CHUNK, :])

    def tail(self):
        return _layer_norm(self.acc, self.ln2g_ref[...], self.ln2b_ref[...])

    def run(self):
        self.head()
        for c in range(self.N_CHUNKS):
            self.chunk(c)
        return self.tail()


def _mix_blocks(zc, nb, pad_rows, thr_first, state, cbuf, wconv_ref, bconv_ref, wqk_ref, gb_ref,
                sinks_ref, gattn_ref, gml_ref, hook):
    rows = nb * BLOCK
    kp_lo, kp_hi, vp_lo, vp_hi, c_state, m_state = state
    c_state, m_state = list(c_state), list(m_state)
    lane = lax.broadcasted_iota(jnp.int32, (BLOCK, 128), 1)
    lo_half = lane < HEAD_DIM
    row_valid = None
    if pad_rows:
        row_valid = lax.broadcasted_iota(jnp.int32, (rows, 1), 0) >= pad_rows

    s_all, vmasks = [], []
    for blk in range(nb):
        rs = slice(blk * BLOCK, (blk + 1) * BLOCK)
        kb = zc[rs, ATT_W:ATT_W + KV_W].astype(BF16)
        vb = zc[rs, ATT_W + KV_W:ATT_W + 2 * KV_W].astype(BF16)
        zero_b = jnp.zeros_like(kb)
        k_lo, k_hi = jnp.where(lo_half, kb, zero_b), jnp.where(lo_half, zero_b, kb)
        v_lo, v_hi = jnp.where(lo_half, vb, zero_b), jnp.where(lo_half, zero_b, vb)
        kmask = jnp.concatenate([kp_lo, k_lo, kp_hi, k_hi], axis=0)
        vmasks.append(jnp.concatenate([vp_lo, v_lo, vp_hi, v_hi], axis=0))
        q4 = jnp.concatenate([zc[rs, r * 128:(r + 1) * 128] for r in range(4)], axis=0).astype(BF16)
        s_all.append(_dot_nt(q4, kmask))
        kp_lo, kp_hi, vp_lo, vp_hi = k_lo, k_hi, v_lo, v_hi
    hook("scores")

    c_in = zc[:, 768:1280]
    if row_valid is not None:
        c_in = jnp.where(row_valid, c_in, 0.0)
    cbuf[8:8 + rows, :] = c_in
    c_act = _conv_silu(lambda jj: cbuf[5 + jj:5 + jj + rows, :], wconv_ref, bconv_ref)
    cbuf[0:8, :] = cbuf[rows:rows + 8, :]
    hs = range(M_HEADS)
    ca_b = [c_act[:, h * 128:(h + 1) * 128].astype(BF16) for h in hs]
    qk = [_dot(ca_b[h], wqk_ref[h]) for h in hs]
    qm_b = [qk[h][:, :M_HEAD_DIM].astype(BF16) for h in hs]
    kmt_f = [(qk[h][:, M_HEAD_DIM:] * (M_HEAD_DIM ** -0.5)).T for h in hs]
    hook("front")

    ti = lax.broadcasted_iota(jnp.int32, (rows, rows), 0)
    si = lax.broadcasted_iota(jnp.int32, (rows, rows), 1)
    ltri = jnp.where((si <= ti) & ((si >> 7) == (ti >> 7)), 1.0, 0.0).astype(BF16)
    xg, _ = _gate_columns(zc[:, IN_MAIN:IN_PAD], gb_ref, row_valid, ltri)
    xgt = xg.T
    hook("gates")

    qi = lax.broadcasted_iota(jnp.int32, (BLOCK, BLOCK), 0)
    ks = lax.broadcasted_iota(jnp.int32, (BLOCK, BLOCK), 1)
    own = ks <= qi
    distf = jnp.where(own, qi - ks, WINDOW + qi - ks).astype(F32)
    zero_p = jnp.zeros((BLOCK, BLOCK), BF16)
    att_rows = []
    for blk in range(nb):
        live = None
        if blk == 0 and thr_first is not None:
            live = jnp.where(own, ks + BLOCK, ks) >= thr_first
        p_rows, inv = [], {}
        for r in range(4):
            pr = []
            for c in range(2):
                h = r + 4 * c
                s_blk = s_all[blk][r * 128:(r + 1) * 128, c * 256:(c + 1) * 256]
                sp = jnp.where(own, s_blk[:, BLOCK:], s_blk[:, :BLOCK]) * (HEAD_DIM ** -0.5) \
                    - ALIBI_SLOPES[h] * distf
                if live is not None:
                    sp = jnp.where(live, sp, -jnp.inf)
                sink = sinks_ref[h]
                mx = jnp.maximum(jnp.max(sp, axis=1, keepdims=True), sink)
                p = jnp.exp(sp - mx)
                inv[h] = 1.0 / (jnp.sum(p, axis=1, keepdims=True) + jnp.exp(sink - mx))
                p_b = p.astype(BF16)
                pr += [jnp.where(own, zero_p, p_b), jnp.where(own, p_b, zero_p)]
            p_rows.append(jnp.concatenate(pr, axis=1))
            hook("softmax", blk * 4 + r)
        o_all = _dot(jnp.concatenate(p_rows, axis=0), vmasks[blk])
        att_rows.append(jnp.concatenate(
            [o_all[r * 128:(r + 1) * 128] * jnp.where(lo_half, inv[r], inv[r + 4]) for r in range(4)], axis=1))
        hook("pv", blk)

    ones_col = jnp.where(lane == 0, 1.0, 0.0).astype(BF16)
    tb = lax.broadcasted_iota(jnp.int32, (BLOCK, BLOCK), 0)
    sb = lax.broadcasted_iota(jnp.int32, (BLOCK, BLOCK), 1)
    tri = sb <= tb
    hh_rows = [[] for _ in hs]
    for blk in range(nb):
        rs = slice(blk * BLOCK, (blk + 1) * BLOCK)
        b_c = [xg[rs, 4 + h:5 + h] for h in hs]
        b_r = [xgt[4 + h:5 + h, rs] for h in hs]
        li_r = [xgt[h:h + 1, rs] for h in hs]
        kmt = [kmt_f[h][:, rs] for h in hs]
        v_ext = [jnp.concatenate([zc[rs, 1280 + h * 128:1280 + (h + 1) * 128].astype(BF16), ones_col], axis=1)
                 for h in hs]
        s_qk = [_dot(qm_b[h][rs], kmt[h].astype(BF16)) for h in hs]
        inter = [_dot(qm_b[h][rs], c_state[h].astype(BF16)) for h in hs]
        hook("mlstm_a", blk)
        dmat = [jnp.where(tri, b_c[h] - b_r[h] + li_r[h], -jnp.inf) for h in hs]
        m_inter = [b_c[h] + m_state[h] for h in hs]
        m_t = [jnp.maximum(m_inter[h], jnp.max(dmat[h], axis=1, keepdims=True)) for h in hs]
        w_inter = [jnp.exp(m_inter[h] - m_t[h]) for h in hs]
        sc_b = [(s_qk[h] * jnp.exp(dmat[h] - m_t[h])).astype(BF16) for h in hs]
        hook("mlstm_b", blk)
        nd = [_dot(sc_b[h], v_ext[h]) + w_inter[h] * inter[h] for h in hs]
        for h in hs:
            m_end = m_t[h][BLOCK - 1:BLOCK, :]
            b_last = b_c[h][BLOCK - 1:BLOCK, :]
            decay = jnp.exp(b_last + m_state[h] - m_end)
            wk_r = jnp.exp(b_last - b_r[h] + li_r[h] - m_end)
            c_state[h] = decay * c_state[h] + _dot((kmt[h] * wk_r).astype(BF16), v_ext[h])
            m_state[h] = m_end
            hh_rows[h].append(nd[h][:, :128] / jnp.maximum(jnp.abs(nd[h][:, 128:129]), jnp.exp(-m_t[h])))
        hook("mlstm_c", blk)

    y_att = _attn_head_norm(jnp.concatenate(att_rows, axis=0), gattn_ref[...])
    hook("attnorm")
    y_m = []
    for h in hs:
        hm = _sigmoid(zc[:, 1792 + h * 128:1792 + (h + 1) * 128]) * jnp.concatenate(hh_rows[h], axis=0)
        y_m.append(_mlstm_head_norm(hm, gml_ref[:, h * 128:(h + 1) * 128]))
        hook("headnorm", h)
    ymix = jnp.concatenate([y_att] + y_m, axis=1).astype(BF16)
    return ymix, (kp_lo, kp_hi, vp_lo, vp_hi, c_state, m_state)


def _project(xb, w_in_refs, lo=0, hi=IN_PAD):
    bounds = (0, ATT_W, IN_MAIN, IN_PAD)
    parts = []
    for ref, start, stop in zip(w_in_refs, bounds[:-1], bounds[1:]):
        a, b = max(lo, start), min(hi, stop)
        if a < b:
            parts.append(_dot(xb, ref[:, a - start:b - start]))
    return parts[0] if len(parts) == 1 else jnp.concatenate(parts, axis=1)


def _meta_state(meta_ref, w_inq_ref, w_inr_ref, w_ing_ref, wconv_ref, bconv_ref, wqk_ref, gb_ref,
                sinks_ref, gattn_ref,
                gml_ref, kplo_o, kphi_o, vplo_o, vphi_o, cb_o, ctn_o, mst_o, zc, cbuf):
    zc[...] = _project(meta_ref[...].astype(BF16), (w_inq_ref, w_inr_ref, w_ing_ref))
    cbuf[0:8, :] = jnp.zeros((8, MLSTM_W), F32)
    zb = jnp.zeros((BLOCK, KV_W), BF16)
    state = (zb, zb, zb, zb, [jnp.zeros((M_HEAD_DIM, 2 * M_HEAD_DIM), F32)] * M_HEADS,
             [jnp.zeros((1, 1), F32)] * M_HEADS)
    _, (kp_lo, kp_hi, vp_lo, vp_hi, c_state, m_state) = _mix_blocks(
        zc, 1, META_PAD, BLOCK + META_PAD, state, cbuf, wconv_ref, bconv_ref, wqk_ref, gb_ref,
        sinks_ref, gattn_ref, gml_ref, lambda name, index=0: None)
    kplo_o[...] = kp_lo
    kphi_o[...] = kp_hi
    vplo_o[...] = vp_lo
    vphi_o[...] = vp_hi
    cb_o[...] = cbuf[0:8, :]
    for h in range(M_HEADS):
        ctn_o[h] = c_state[h]
        mst_o[h:h + 1, :] = jnp.broadcast_to(m_state[h], (1, 128))
    mst_o[M_HEADS:8, :] = jnp.zeros((8 - M_HEADS, 128), F32)


def _pair_kernel(xin_ref, xres_ref, meta_ref,
                 w_inq_ref, w_inr_ref, w_ing_ref, wconv_ref, bconv_ref, wqk_ref, gb_ref, sinks_ref,
                 gattn_ref, gml_ref,
                 wout_ref, ln1g_ref, ln1b_ref, wg_ref, wu_ref, wd_ref, ln2g_ref, ln2b_ref,
                 y_ref, pk_ref, pv_ref, pconv_ref, pc_ref, pn_ref, pm_ref,
                 kplo_i, kphi_i, vplo_i, vphi_i, cb_i, ctn_i, mst_i,
                 zcur, znext, kvlast, kplo, kphi, vplo, vphi, cbuf, ctn, mst, yprev, *, npair):
    t = pl.program_id(0)
    p = lax.rem(t + (npair - 1), npair)

    @pl.when(t == 0)
    def _init_pipeline():
        zcur[...] = jnp.zeros_like(zcur)
        yprev[...] = jnp.zeros_like(yprev)
        _meta_state(meta_ref, w_inq_ref, w_inr_ref, w_ing_ref, wconv_ref, bconv_ref, wqk_ref, gb_ref,
                    sinks_ref, gattn_ref, gml_ref, kplo_i, kphi_i, vplo_i, vphi_i, cb_i, ctn_i, mst_i,
                    znext.at[pl.ds(0, BLOCK)], cbuf)

    @pl.when((p == 0) | (t == 0))
    def _load_meta_state():
        kplo[...] = kplo_i[...]
        kphi[...] = kphi_i[...]
        vplo[...] = vplo_i[...]
        vphi[...] = vphi_i[...]
        cbuf[0:8, :] = cb_i[...]
        ctn[...] = ctn_i[...]
        mst[...] = mst_i[...]

    ffn = _FfnStages(xres_ref[...], yprev[...], wout_ref, ln1g_ref, ln1b_ref, wg_ref, wu_ref, wd_ref,
                     ln2g_ref, ln2b_ref)
    xb_b = xin_ref[...].astype(BF16)

    n_proj = -(-IN_PAD // PROJ_CHUNK)

    def proj(n):
        lo, hi = n * PROJ_CHUNK, min((n + 1) * PROJ_CHUNK, IN_PAD)
        znext[:, lo:hi] = _project(xb_b, (w_inq_ref, w_inr_ref, w_ing_ref), lo, hi)

    proj_it, ffn_it = iter(range(n_proj)), iter(range(ffn.N_CHUNKS))

    def fill_proj(n):
        for c in [c for _, c in zip(range(n), proj_it)]:
            proj(c)

    def fill_ffn(n):
        for c in [c for _, c in zip(range(n), ffn_it)]:
            ffn.chunk(c)

    out = {}

    def hook(name, index=0):
        if name == "scores":
            fill_proj(2)
            ffn.head()
        elif name == "front":
            fill_proj(1)
        elif name == "gates":
            fill_ffn(1)
        elif name == "softmax":
            if index % 2 == 1:
                fill_ffn(1)
        elif name in ("pv", "mlstm_a", "mlstm_b"):
            fill_ffn(1)
        elif name == "mlstm_c":
            fill_proj(1)
        elif name == "attnorm":
            fill_ffn(ffn.N_CHUNKS)
            out["y"] = ffn.tail()
            fill_proj(1)
        elif name == "headnorm":
            fill_proj(1)

    state = (kplo[...], kphi[...], vplo[...], vphi[...], [ctn[h] for h in range(M_HEADS)],
             [mst[h:h + 1, 0:1] for h in range(M_HEADS)])
    thr_first = jnp.where(p == 0, META_PAD, 0)
    kvlast[...] = zcur[(STEP_BLOCKS - 1) * BLOCK:STEP_BLOCKS * BLOCK, ATT_W:ATT_W + 2 * KV_W]
    ymix, (kp_lo, kp_hi, vp_lo, vp_hi, c_state, m_state) = _mix_blocks(
        zcur, STEP_BLOCKS, 0, thr_first, state, cbuf, wconv_ref, bconv_ref, wqk_ref, gb_ref,
        sinks_ref, gattn_ref, gml_ref, hook)
    fill_proj(n_proj)
    y_ref[...] = out["y"]
    kplo[...] = kp_lo
    kphi[...] = kp_hi
    vplo[...] = vp_lo
    vphi[...] = vp_hi
    for h in range(M_HEADS):
        ctn[h] = c_state[h]
        mst[h:h + 1, :] = jnp.broadcast_to(m_state[h], (1, 128))
    yprev[...] = ymix
    zcur[...] = znext[...]

    @pl.when(p == npair - 1)
    def _final():
        pk_ref[...] = kvlast[:, 0:KV_W]
        pv_ref[...] = kvlast[:, KV_W:2 * KV_W]
        pconv_ref[...] = cbuf[8 - (CONV_W - 1):8, :]
        pm_ref[...] = mst[...]
        for h in range(M_HEADS):
            c_n = ctn[h].T
            pc_ref[h] = c_n[0:M_HEAD_DIM, :]
            pn_ref[h:h + 1, :] = c_n[M_HEAD_DIM:M_HEAD_DIM + 1, :]


def _sample_kernel(xs_ref, ck_ref, cv_ref, sconv_ref, sc_ref, sn_ref, sm_ref,
                   w_inq_ref, w_inr_ref, w_ing_ref, wconv_ref, bconv_ref, wqk_ref, gb_ref, sinks_ref,
                   gattn_ref, gml_ref,
                   ymix_ref, sk_ref, sv_ref, scv_ref, sco_ref, sno_ref, smo_ref, cext):
    ns, L = SEQ_TILE, 8
    xs = xs_ref[...].reshape(ns * L, D_MODEL)
    z = _project(xs.astype(BF16), (w_inq_ref, w_inr_ref, w_ing_ref))

    lane = lax.broadcasted_iota(jnp.int32, (BLOCK, 128), 1)
    lo_half = lane < HEAD_DIM
    lo3 = lax.broadcasted_iota(jnp.int32, (ns, L, 128), 2) < HEAD_DIM
    ti = lax.broadcasted_iota(jnp.int32, (BLOCK, BLOCK), 0)
    si = lax.broadcasted_iota(jnp.int32, (BLOCK, BLOCK), 1)
    same_seq = (ti >> 3) == (si >> 3)
    tl, sl_ = ti & 7, si & 7
    causal_seq = same_seq & (sl_ <= tl)

    k_new = z[:, ATT_W:ATT_W + KV_W]
    v_new = z[:, ATT_W + KV_W:ATT_W + 2 * KV_W]
    kb, vb = k_new.astype(BF16), v_new.astype(BF16)
    zero_b = jnp.zeros_like(kb)
    kn_mask = jnp.concatenate([jnp.where(lo_half, kb, zero_b), jnp.where(lo_half, zero_b, kb)], axis=0)
    vn_mask = jnp.concatenate([jnp.where(lo_half, vb, zero_b), jnp.where(lo_half, zero_b, vb)], axis=0)
    qg = [z[:, r * 128:(r + 1) * 128] for r in range(4)]
    s_new = _dot_nt(jnp.concatenate(qg, axis=0).astype(BF16), kn_mask)
    qg3 = [q.reshape(ns, L, 128) for q in qg]
    q8 = jnp.concatenate([jnp.where(lo3, q, 0.0) for q in qg3] + [jnp.where(lo3, 0.0, q) for q in qg3],
                         axis=1).astype(BF16)
    s_cache = jnp.einsum('nqd,ndk->nqk', q8, ck_ref[...].astype(BF16), preferred_element_type=F32)

    c3 = z[:, 768:1280].reshape(ns, L, MLSTM_W)
    cext[:, 8 - (CONV_W - 1):8, :] = sconv_ref[...]
    cext[:, 8:16, :] = c3
    scv_ref[...] = c3[:, L - (CONV_W - 1):L, :]
    c_act = _conv_silu(lambda jj: cext[:, 5 + jj:5 + jj + L, :], wconv_ref, bconv_ref)
    c_act = c_act.reshape(BLOCK, MLSTM_W)
    lblk = jnp.where(causal_seq, 1.0, 0.0).astype(BF16)
    lseq = jnp.where(same_seq, 1.0, 0.0).astype(BF16)
    xg, blast = _gate_columns(z[:, IN_MAIN:IN_PAD], gb_ref, None, lblk, extra=lseq)
    xgt = xg.T
    m_rep = jnp.broadcast_to(sm_ref[...], (ns, L, 128)).reshape(BLOCK, 128)
    hs = range(M_HEADS)
    qk = [_dot(c_act[:, h * 128:(h + 1) * 128].astype(BF16), wqk_ref[h]) for h in hs]
    qm = [qk[h][:, :128] for h in hs]
    qm_b = [qm[h].astype(BF16) for h in hs]
    km = [qk[h][:, 128:] * (M_HEAD_DIM ** -0.5) for h in hs]
    km_b = [km[h].astype(BF16) for h in hs]
    vh_b = [z[:, 1280 + h * 128:1280 + (h + 1) * 128].astype(BF16) for h in hs]
    c_old = [sc_ref[:, h] for h in hs]
    s_qk = [_dot_nt(qm_b[h], km_b[h]) for h in hs]
    cq = [_dot_nt(qm_b[h], c_old[h].reshape(ns * 128, 128).astype(BF16)) for h in hs]

    mask_c = si > tl
    dist_c = (WINDOW + tl - si).astype(F32)
    dist_n = (tl - sl_).astype(F32)
    pc_parts, pn_rows, inv = {}, [], {}
    for r in range(4):
        pn = []
        for c in range(2):
            h = r + 4 * c
            sc_c = s_cache[:, c * 32 + r * L:c * 32 + (r + 1) * L, :].reshape(BLOCK, 128)
            sc_n = s_new[r * 128:(r + 1) * 128, c * 128:(c + 1) * 128]
            (p_c, p_n), inv[h] = _softmax_parts(
                [(sc_c, mask_c, -ALIBI_SLOPES[h] * dist_c), (sc_n, causal_seq, -ALIBI_SLOPES[h] * dist_n)],
                sinks_ref[h])
            pc_parts[(c, r)] = p_c.reshape(ns, L, 128)
            pn.append(p_n.astype(BF16))
        pn_rows.append(jnp.concatenate(pn, axis=1))
    p8 = jnp.concatenate([pc_parts[(c, r)] for c in range(2) for r in range(4)], axis=1).astype(BF16)
    o_cache = jnp.einsum('nqk,ndk->nqd', p8, cv_ref[...].astype(BF16), preferred_element_type=F32)
    o_new = _dot(jnp.concatenate(pn_rows, axis=0), vn_mask)
    groups = []
    for r in range(4):
        oa = o_cache[:, r * L:(r + 1) * L, :].reshape(BLOCK, 128)
        ob = o_cache[:, 32 + r * L:32 + (r + 1) * L, :].reshape(BLOCK, 128)
        o = jnp.where(lo_half, oa, ob) + o_new[r * 128:(r + 1) * 128]
        groups.append(o * jnp.where(lo_half, inv[r], inv[r + 4]))
    y_att = _attn_head_norm(jnp.concatenate(groups, axis=1), gattn_ref[...])

    sk_ref[:, 0:WINDOW - L, :] = jnp.swapaxes(ck_ref[...], 1, 2)[:, L:WINDOW, :]
    sk_ref[:, WINDOW - L:WINDOW, :] = k_new.reshape(ns, L, 128)
    sv_ref[:, 0:WINDOW - L, :] = jnp.swapaxes(cv_ref[...], 1, 2)[:, L:WINDOW, :]
    sv_ref[:, WINDOW - L:WINDOW, :] = v_new.reshape(ns, L, 128)

    b_c = [xg[:, 4 + h:5 + h] for h in hs]
    li_c = [xg[:, h:h + 1] for h in hs]
    b_r = [xgt[4 + h:5 + h, :] for h in hs]
    li_r = [xgt[h:h + 1, :] for h in hs]
    bl_c = [blast[:, 4 + h:5 + h] for h in hs]
    m_prev = [m_rep[:, h:h + 1] for h in hs]
    dmat = [jnp.where(causal_seq, b_c[h] - b_r[h] + li_r[h], -jnp.inf) for h in hs]
    m_inter = [b_c[h] + m_prev[h] for h in hs]
    m_t = [jnp.maximum(m_inter[h], jnp.max(dmat[h], axis=1, keepdims=True)) for h in hs]
    w_inter = [jnp.exp(m_inter[h] - m_t[h]) for h in hs]
    sc = [s_qk[h] * jnp.exp(dmat[h] - m_t[h]) for h in hs]
    num_intra = [_dot(sc[h].astype(BF16), vh_b[h]) for h in hs]

    m_end3 = [jnp.broadcast_to(m_t[h], (BLOCK, 128)).reshape(ns, L, 128)[:, L - 1:L, :] for h in hs]
    m_end = [jnp.broadcast_to(m_end3[h], (ns, L, 128)).reshape(BLOCK, 128)[:, 0:1] for h in hs]
    decay = [jnp.exp(bl_c[h] + m_prev[h] - m_end[h]) for h in hs]
    kw = [km[h] * jnp.exp(bl_c[h] - b_c[h] + li_c[h] - m_end[h]) for h in hs]
    seq_of_row = ti >> 3
    lane3 = lax.broadcasted_iota(jnp.int32, (ns, 1, 128), 2)
    m_out = jnp.zeros((ns, 1, 128), F32)
    for h in hs:
        kw_b = kw[h].astype(BF16)
        k_big = jnp.concatenate(
            [jnp.where(seq_of_row == g, kw_b, jnp.zeros_like(kw_b)) for g in range(ns)], axis=1)
        c_up = _dot_tn(vh_b[h], k_big)
        c_up3 = jnp.stack([c_up[:, g * 128:(g + 1) * 128] for g in range(ns)], axis=0)
        decay3 = jnp.broadcast_to(decay[h], (BLOCK, 128)).reshape(ns, L, 128)[:, 0:1, :]
        n_old = sn_ref[:, h:h + 1, :]
        sco_ref[:, h] = decay3 * c_old[h] + c_up3
        sno_ref[:, h:h + 1, :] = decay3 * n_old + jnp.sum(kw[h].reshape(ns, L, 128), axis=1, keepdims=True)
        m_out = jnp.where(lane3 == h, m_end3[h], m_out)
    smo_ref[...] = m_out

    y_m = []
    for h in hs:
        num_inter = jnp.concatenate([cq[h][g * L:(g + 1) * L, g * 128:(g + 1) * 128] for g in range(ns)], axis=0)
        n_rep = jnp.broadcast_to(sn_ref[:, h:h + 1, :], (ns, L, 128)).reshape(BLOCK, 128)
        num = num_intra[h] + w_inter[h] * num_inter
        den = jnp.sum(sc[h], axis=1, keepdims=True) + w_inter[h] * jnp.sum(qm[h] * n_rep, axis=1, keepdims=True)
        hh = num / jnp.maximum(jnp.abs(den), jnp.exp(-m_t[h]))
        hm = _sigmoid(z[:, 1792 + h * 128:1792 + (h + 1) * 128]) * hh
        y_m.append(_mlstm_head_norm(hm, gml_ref[:, h * 128:(h + 1) * 128]))

    ymix_ref[...] = jnp.concatenate([y_att] + y_m, axis=1).astype(BF16)


def _post_kernel(x_ref, y_ref, wout_ref, ln1g_ref, ln1b_ref, wg_ref, wu_ref, wd_ref, ln2g_ref, ln2b_ref, o_ref):
    o_ref[...] = _FfnStages(x_ref[...], y_ref[...], wout_ref, ln1g_ref, ln1b_ref, wg_ref, wu_ref, wd_ref,
                            ln2g_ref, ln2b_ref).run()


def _const_spec(shape):
    nd = len(shape)
    return pl.BlockSpec(shape, lambda *_: (0,) * nd, pipeline_mode=pl.Buffered(1))


def _mixer_weight_specs():
    return [
        _const_spec((D_MODEL, ATT_W)),
        _const_spec((D_MODEL, IN_MAIN - ATT_W)),
        _const_spec((D_MODEL, IN_PAD - IN_MAIN)),
        _const_spec((CONV_W, MLSTM_W)),
        _const_spec((1, MLSTM_W)),
        _const_spec((M_HEADS, M_HEAD_DIM, 2 * M_HEAD_DIM)),
        _const_spec((1, 128)),
        pl.BlockSpec(memory_space=pltpu.SMEM),
        _const_spec((1, ATT_W)),
        _const_spec((1, MLSTM_W)),
    ]


def _post_weight_specs(d):
    return [
        _const_spec((d, d)),
        _const_spec((1, d)), _const_spec((1, d)),
        _const_spec((d, D_FF)), _const_spec((d, D_FF)), _const_spec((D_FF, d)),
        _const_spec((1, d)), _const_spec((1, d)),
    ]


def _prompt_layer(x_prompt, meta_blk, mixer_w, post_w):
    b, s, d = x_prompt.shape
    state_shapes = [((BLOCK, KV_W), BF16)] * 4 + [
        ((8, MLSTM_W), F32), ((M_HEADS, M_HEAD_DIM, 2 * M_HEAD_DIM), F32), ((8, 128), F32)]
    rows = STEP_BLOCKS * BLOCK
    npair = s // rows
    total = b * npair
    steps = total + 2

    def block_map(lag):
        def index(t):
            tb = jnp.clip(t - lag, 0, total - 1)
            return (tb // npair, tb % npair, 0)
        return index

    def smap(t):
        return (jnp.clip(t - 1, 0, total - 1) // npair, 0, 0)

    out_shape = (
        jax.ShapeDtypeStruct((b, s, d), F32),
        jax.ShapeDtypeStruct((b, BLOCK, KV_W), F32),
        jax.ShapeDtypeStruct((b, BLOCK, KV_W), F32),
        jax.ShapeDtypeStruct((b, CONV_W - 1, MLSTM_W), F32),
        jax.ShapeDtypeStruct((b, M_HEADS, M_HEAD_DIM, M_HEAD_DIM), F32),
        jax.ShapeDtypeStruct((b, M_HEADS, M_HEAD_DIM), F32),
        jax.ShapeDtypeStruct((b, 8, 128), F32),
    )
    out_specs = (
        pl.BlockSpec((None, rows, d), block_map(2)),
        pl.BlockSpec((None, BLOCK, KV_W), smap),
        pl.BlockSpec((None, BLOCK, KV_W), smap),
        pl.BlockSpec((None, CONV_W - 1, MLSTM_W), smap),
        pl.BlockSpec((None, M_HEADS, M_HEAD_DIM, M_HEAD_DIM), lambda t: smap(t) + (0,)),
        pl.BlockSpec((None, M_HEADS, M_HEAD_DIM), smap),
        pl.BlockSpec((None, 8, 128), smap),
    )
    return pl.pallas_call(
        functools.partial(_pair_kernel, npair=npair),
        grid=(steps,),
        in_specs=[pl.BlockSpec((None, rows, d), block_map(0)), pl.BlockSpec((None, rows, d), block_map(2))]
        + [_const_spec((BLOCK, d))]
        + _mixer_weight_specs() + _post_weight_specs(d),
        out_specs=out_specs,
        out_shape=out_shape,
        scratch_shapes=[pltpu.VMEM(shape, dt) for shape, dt in state_shapes] + [
            pltpu.VMEM((rows, IN_PAD), F32),
            pltpu.VMEM((rows, IN_PAD), F32),
            pltpu.VMEM((BLOCK, 2 * KV_W), F32),
        ] + [pltpu.VMEM((BLOCK, KV_W), BF16)] * 4 + [
            pltpu.VMEM((rows + 8, MLSTM_W), F32),
            pltpu.VMEM((M_HEADS, M_HEAD_DIM, 2 * M_HEAD_DIM), F32),
            pltpu.VMEM((8, 128), F32),
            pltpu.VMEM((rows, d), BF16),
        ],
        compiler_params=pltpu.CompilerParams(
            dimension_semantics=("arbitrary",), vmem_limit_bytes=VMEM_LIMIT),
        name="prompt_layer",
    )(x_prompt, x_prompt, meta_blk, *mixer_w, *post_w)


def _sample_mixer(x_sample, ck, cv, sconv8, s_c, s_n, s_m3, mixer_w):
    n, l, d = x_sample.shape
    t = SEQ_TILE
    m3 = lambda i: (i, 0, 0)
    m4 = lambda i: (i, 0, 0, 0)
    out_shape = (
        jax.ShapeDtypeStruct((n * l, d), BF16),
        jax.ShapeDtypeStruct((n, WINDOW, KV_W), F32),
        jax.ShapeDtypeStruct((n, WINDOW, KV_W), F32),
        jax.ShapeDtypeStruct((n, CONV_W - 1, MLSTM_W), F32),
        jax.ShapeDtypeStruct((n, M_HEADS, M_HEAD_DIM, M_HEAD_DIM), F32),
        jax.ShapeDtypeStruct((n, M_HEADS, M_HEAD_DIM), F32),
        jax.ShapeDtypeStruct((n, 1, 128), F32),
    )
    out_specs = (
        pl.BlockSpec((t * l, d), lambda i: (i, 0)),
        pl.BlockSpec((t, WINDOW, KV_W), m3),
        pl.BlockSpec((t, WINDOW, KV_W), m3),
        pl.BlockSpec((t, CONV_W - 1, MLSTM_W), m3),
        pl.BlockSpec((t, M_HEADS, M_HEAD_DIM, M_HEAD_DIM), m4),
        pl.BlockSpec((t, M_HEADS, M_HEAD_DIM), m3),
        pl.BlockSpec((t, 1, 128), m3),
    )
    in_specs = [
        pl.BlockSpec((t, l, d), m3),
        pl.BlockSpec((t, WINDOW, KV_W), m3),
        pl.BlockSpec((t, WINDOW, KV_W), m3),
        pl.BlockSpec((t, CONV_W - 1, MLSTM_W), m3),
        pl.BlockSpec((t, M_HEADS, M_HEAD_DIM, M_HEAD_DIM), m4),
        pl.BlockSpec((t, M_HEADS, M_HEAD_DIM), m3),
        pl.BlockSpec((t, 1, 128), m3),
    ] + _mixer_weight_specs()
    return pl.pallas_call(
        _sample_kernel,
        grid=(n // t,),
        in_specs=in_specs,
        out_specs=out_specs,
        out_shape=out_shape,
        scratch_shapes=[pltpu.VMEM((t, 16, MLSTM_W), F32)],
        compiler_params=pltpu.CompilerParams(
            dimension_semantics=("arbitrary",), vmem_limit_bytes=VMEM_LIMIT),
        name="sample_mixer",
    )(x_sample, ck, cv, sconv8, s_c, s_n, s_m3, *mixer_w)


def _post(x_rows, y_rows, post_w):
    rows, d = x_rows.shape
    tm = min(POST_ROWS, rows)
    rmap = lambda i: (i, 0)
    in_specs = [pl.BlockSpec((tm, d), rmap), pl.BlockSpec((tm, d), rmap)] + _post_weight_specs(d)
    return pl.pallas_call(
        _post_kernel,
        grid=(rows // tm,),
        in_specs=in_specs,
        out_specs=pl.BlockSpec((tm, d), rmap),
        out_shape=jax.ShapeDtypeStruct((rows, d), F32),
        compiler_params=pltpu.CompilerParams(
            dimension_semantics=("arbitrary",), vmem_limit_bytes=VMEM_LIMIT),
        name="post_ffn",
    )(x_rows, y_rows, *post_w)


def kernel(x_prompt, x_sample, cache_k, cache_v, state_conv, state_C, state_n, state_m, meta_tokens,
           w_in, w_conv, b_conv, w_mq, w_mk, b_i, b_f, attn_sinks, g_attn, g_mlstm, w_out,
           ln1_g, ln1_b, w_gate, w_up, w_down, ln2_g, ln2_b):
    b, s, d = x_prompt.shape
    n, l, _ = x_sample.shape

    w0 = w_in[0]
    w_in_parts = (_pair_heads(w0[:, :ATT_W], 1).astype(BF16), w0[:, ATT_W:IN_MAIN].astype(BF16),
                  jnp.pad(w0[:, IN_MAIN:], ((0, 0), (0, IN_PAD - IN_MAIN - IN_GATES))).astype(BF16))
    wqk = jnp.concatenate([w_mq[0], w_mk[0]], axis=-1).astype(BF16)
    gb = jnp.concatenate([b_i[0], b_f[0], jnp.zeros((128 - IN_GATES,), F32)]).reshape(1, 128)
    mixer_w = w_in_parts + (w_conv[0], b_conv[0].reshape(1, MLSTM_W), wqk, gb, attn_sinks[0],
                            _pair_heads(g_attn[0].reshape(1, ATT_W), 1), g_mlstm[0].reshape(1, MLSTM_W))
    w_out_p = jnp.concatenate([_pair_heads(w_out[0][:ATT_W], 0), w_out[0][ATT_W:]], axis=0).astype(BF16)
    post_w = (w_out_p, ln1_g[0].reshape(1, d), ln1_b[0].reshape(1, d),
              w_gate[0].astype(BF16), w_up[0].astype(BF16), w_down[0].astype(BF16),
              ln2_g[0].reshape(1, d), ln2_b[0].reshape(1, d))

    meta_blk = jnp.concatenate([jnp.zeros((META_PAD, d), x_prompt.dtype), meta_tokens.astype(x_prompt.dtype)], axis=0)
    y_prompt, pk, pv, pconv, p_c, p_n, pm8 = _prompt_layer(x_prompt, meta_blk, mixer_w, post_w)

    key_minor = lambda c: c[0].transpose(0, 2, 3, 1).reshape(n, KV_W, WINDOW)
    s_m3 = jnp.pad(state_m[0], ((0, 0), (0, 128 - M_HEADS))).reshape(n, 1, 128)
    ymix_s, sk, sv, scv, s_c, s_n, s_mo = _sample_mixer(
        x_sample, key_minor(cache_k), key_minor(cache_v), state_conv[0], state_C[0], state_n[0], s_m3, mixer_w)

    y_sample = _post(x_sample.reshape(n * l, d), ymix_s, post_w).reshape(n, l, d)

    kv5 = lambda a: a.reshape(1, a.shape[0], WINDOW, 2, HEAD_DIM)
    return (y_prompt, y_sample,
            kv5(pk), kv5(pv), pconv[None], p_c[None], p_n[None], pm8[None, :, :M_HEADS, 0],
            kv5(sk), kv5(sv), scv[None], s_c[None], s_n[None], s_mo[None, :, 0, :M_HEADS])
```

```python
import functools

import jax
import jax.numpy as jnp
from jax import lax
from jax.experimental import pallas as pl
from jax.experimental.pallas import tpu as pltpu

F32 = jnp.float32
BF16 = jnp.bfloat16

D_MODEL = 1024
ATT_W = 512
MLSTM_W = 512
HEAD_DIM = 64
N_HEADS = 8
KV_W = 128
WINDOW = 128
BLOCK = 128
M_HEADS = 4
M_HEAD_DIM = 128
CONV_W = 4
N_META = 16
META_PAD = BLOCK - N_META
D_FF = 2816
IN_MAIN = 2304
IN_GATES = 2 * M_HEADS
IN_PAD = IN_MAIN + 128
DEPTH = 1
ALIBI_SLOPES = tuple(2.0 ** (-8.0 * (h + 1) / N_HEADS) for h in range(N_HEADS))
DEEPNORM_ALPHA = (2.0 * DEPTH) ** 0.25
EPS = 1e-5
SEQ_TILE = 16
STEP_BLOCKS = 2
FF_CHUNK = 256
FF_LOOKAHEAD = 2
PROJ_CHUNK = 256
VMEM_LIMIT = 56 * 1024 * 1024


def _pair_heads(a, axis):
    shape = a.shape
    a = a.reshape(shape[:axis] + (2, 4, HEAD_DIM) + shape[axis + 1:])
    return jnp.swapaxes(a, axis, axis + 1).reshape(shape)


def _dot(a, b):
    return jnp.dot(a, b, preferred_element_type=F32)


def _dot_nt(a, b):
    return lax.dot_general(a, b, (((1,), (1,)), ((), ())), preferred_element_type=F32)


def _dot_tn(a, b):
    return lax.dot_general(a, b, (((0,), (0,)), ((), ())), preferred_element_type=F32)


def _split2(x):
    hi = x.astype(BF16)
    lo = (x - hi.astype(F32)).astype(BF16)
    return hi, lo


def _sigmoid(x):
    return 1.0 / (1.0 + jnp.exp(-x))


def _log_sigmoid(x):
    return -(jnp.maximum(-x, 0.0) + jnp.log1p(jnp.exp(-jnp.abs(x))))


def _layer_norm(x, g, b):
    mu = jnp.mean(x, axis=-1, keepdims=True)
    xc = x - mu
    var = jnp.mean(xc * xc, axis=-1, keepdims=True)
    return xc * lax.rsqrt(var + EPS) * g + b


def _attn_head_norm(att, g):
    lo_half = lax.broadcasted_iota(jnp.int32, (att.shape[0], 128), 1) < HEAD_DIM

    def seg_mean(x):
        s_all = jnp.sum(x, axis=1, keepdims=True)
        s_lo = jnp.sum(jnp.where(lo_half, x, 0.0), axis=1, keepdims=True)
        return jnp.where(lo_half, s_lo, s_all - s_lo) * (1.0 / HEAD_DIM)

    out = []
    for grp in range(att.shape[1] // 128):
        x = att[:, grp * 128:(grp + 1) * 128]
        xc = x - seg_mean(x)
        out.append(xc * lax.rsqrt(seg_mean(xc * xc) + EPS))
    return jnp.concatenate(out, axis=1) * g


def _mlstm_head_norm(hm, g):
    mu = jnp.mean(hm, axis=-1, keepdims=True)
    xc = hm - mu
    var = jnp.mean(xc * xc, axis=-1, keepdims=True)
    return xc * lax.rsqrt(var + EPS) * g


def _softmax_parts(parts, sink):
    sp = [jnp.where(m, s * (HEAD_DIM ** -0.5) + a, -jnp.inf) for s, m, a in parts]
    mx = sink
    for s in sp:
        mx = jnp.maximum(mx, jnp.max(s, axis=1, keepdims=True))
    ps = [jnp.exp(s - mx) for s in sp]
    den = jnp.exp(sink - mx)
    for p in ps:
        den = den + jnp.sum(p, axis=1, keepdims=True)
    return ps, 1.0 / den


def _conv_silu(window, wconv_ref, bconv_ref):
    acc = bconv_ref[...]
    for j in range(CONV_W):
        acc = acc + window(j) * wconv_ref[j:j + 1, :]
    return acc * _sigmoid(acc)


def _gate_columns(gates, gb_ref, row_valid, ltri, extra=None):
    lane = lax.broadcasted_iota(jnp.int32, gates.shape, 1)
    gb = gates + gb_ref[...]
    is_i = lane < M_HEADS
    is_f = (lane >= M_HEADS) & (lane < 2 * M_HEADS)
    logf = _log_sigmoid(gb)
    if row_valid is not None:
        lf = jnp.where(is_f & row_valid, logf, 0.0)
        li = jnp.where(row_valid, gb, -jnp.inf)
    else:
        lf = jnp.where(is_f, logf, 0.0)
        li = gb
    hi_lo = jnp.concatenate(_split2(lf), axis=1)
    bc = _dot(ltri, hi_lo)
    x = jnp.where(is_i, li, bc[:, :128] + bc[:, 128:])
    if extra is None:
        return x, None
    be = _dot(extra, hi_lo)
    return x, be[:, :128] + be[:, 128:]


class _FfnStages:
    N_CHUNKS = D_FF // FF_CHUNK

    def __init__(self, x, ymix, wout_ref, ln1g_ref, ln1b_ref, wg_ref, wu_ref, wd_ref, ln2g_ref, ln2b_ref):
        self.x, self.ymix = x, ymix
        self.wout_ref, self.ln1g_ref, self.ln1b_ref = wout_ref, ln1g_ref, ln1b_ref
        self.wg_ref, self.wu_ref, self.wd_ref = wg_ref, wu_ref, wd_ref
        self.ln2g_ref, self.ln2b_ref = ln2g_ref, ln2b_ref

    def head(self):
        self.x1 = _layer_norm(DEEPNORM_ALPHA * self.x + _dot(self.ymix, self.wout_ref[...]),
                              self.ln1g_ref[...], self.ln1b_ref[...])
        self.x1b = self.x1.astype(BF16)
        self.acc = DEEPNORM_ALPHA * self.x1

    def _gate_up(self, c):
        cs = slice(c * FF_CHUNK, (c + 1) * FF_CHUNK)
        return _dot(self.x1b, self.wg_ref[:, cs]), _dot(self.x1b, self.wu_ref[:, cs])

    def chunk(self, c):
        if c == 0:
            self.gu = {}
        for n in range(c, min(c + FF_LOOKAHEAD, self.N_CHUNKS - 1) + 1):
            if n not in self.gu:
                self.gu[n] = self._gate_up(n)
        g, u = self.gu.pop(c)
        hid = (g * _sigmoid(g) * u).astype(BF16)
        self.acc = self.acc + _dot(hid, self.wd_ref[c * FF_CHUNK:(c + 1) * FF_CHUNK, :])

    def tail(self):
        return _layer_norm(self.acc, self.ln2g_ref[...], self.ln2b_ref[...])

    def run(self):
        self.head()
        for c in range(self.N_CHUNKS):
            self.chunk(c)
        return self.tail()


def _mix_blocks(zc, nb, pad_rows, thr_first, state, cbuf, wconv_ref, bconv_ref, wqk_ref, gb_ref,
                sinks_ref, gattn_ref, gml_ref, hook):
    rows = nb * BLOCK
    kp_lo, kp_hi, vp_lo, vp_hi, c_state, m_state = state
    c_state, m_state = list(c_state), list(m_state)
    lane = lax.broadcasted_iota(jnp.int32, (BLOCK, 128), 1)
    lo_half = lane < HEAD_DIM
    row_valid = None
    if pad_rows:
        row_valid = lax.broadcasted_iota(jnp.int32, (rows, 1), 0) >= pad_rows

    s_all, vmasks = [], []
    for blk in range(nb):
        rs = slice(blk * BLOCK, (blk + 1) * BLOCK)
        kb = zc[rs, ATT_W:ATT_W + KV_W].astype(BF16)
        vb = zc[rs, ATT_W + KV_W:ATT_W + 2 * KV_W].astype(BF16)
        zero_b = jnp.zeros_like(kb)
        k_lo, k_hi = jnp.where(lo_half, kb, zero_b), jnp.where(lo_half, zero_b, kb)
        v_lo, v_hi = jnp.where(lo_half, vb, zero_b), jnp.where(lo_half, zero_b, vb)
        kmask = jnp.concatenate([kp_lo, k_lo, kp_hi, k_hi], axis=0)
        vmasks.append(jnp.concatenate([vp_lo, v_lo, vp_hi, v_hi], axis=0))
        q4 = jnp.concatenate([zc[rs, r * 128:(r + 1) * 128] for r in range(4)], axis=0).astype(BF16)
        s_all.append(_dot_nt(q4, kmask))
        kp_lo, kp_hi, vp_lo, vp_hi = k_lo, k_hi, v_lo, v_hi
    hook("scores")

    c_in = zc[:, 768:1280]
    if row_valid is not None:
        c_in = jnp.where(row_valid, c_in, 0.0)
    cbuf[8:8 + rows, :] = c_in
    c_act = _conv_silu(lambda jj: cbuf[5 + jj:5 + jj + rows, :], wconv_ref, bconv_ref)
    cbuf[0:8, :] = cbuf[rows:rows + 8, :]
    hs = range(M_HEADS)
    ca_b = [c_act[:, h * 128:(h + 1) * 128].astype(BF16) for h in hs]
    qk = [_dot(ca_b[h], wqk_ref[h]) for h in hs]
    qm_b = [qk[h][:, :M_HEAD_DIM].astype(BF16) for h in hs]
    kmt_f = [(qk[h][:, M_HEAD_DIM:] * (M_HEAD_DIM ** -0.5)).T for h in hs]
    hook("front")

    ti = lax.broadcasted_iota(jnp.int32, (rows, rows), 0)
    si = lax.broadcasted_iota(jnp.int32, (rows, rows), 1)
    ltri = jnp.where((si <= ti) & ((si >> 7) == (ti >> 7)), 1.0, 0.0).astype(BF16)
    xg, _ = _gate_columns(zc[:, IN_MAIN:IN_PAD], gb_ref, row_valid, ltri)
    xgt = xg.T
    hook("gates")

    qi = lax.broadcasted_iota(jnp.int32, (BLOCK, BLOCK), 0)
    ks = lax.broadcasted_iota(jnp.int32, (BLOCK, BLOCK), 1)
    own = ks <= qi
    distf = jnp.where(own, qi - ks, WINDOW + qi - ks).astype(F32)
    zero_p = jnp.zeros((BLOCK, BLOCK), BF16)
    att_rows = []
    for blk in range(nb):
        live = None
        if blk == 0 and thr_first is not None:
            live = jnp.where(own, ks + BLOCK, ks) >= thr_first
        p_rows, inv = [], {}
        for r in range(4):
            pr = []
            for c in range(2):
                h = r + 4 * c
                s_blk = s_all[blk][r * 128:(r + 1) * 128, c * 256:(c + 1) * 256]
                sp = jnp.where(own, s_blk[:, BLOCK:], s_blk[:, :BLOCK]) * (HEAD_DIM ** -0.5) \
                    - ALIBI_SLOPES[h] * distf
                if live is not None:
                    sp = jnp.where(live, sp, -jnp.inf)
                sink = sinks_ref[h]
                mx = jnp.maximum(jnp.max(sp, axis=1, keepdims=True), sink)
                p = jnp.exp(sp - mx)
                inv[h] = 1.0 / (jnp.sum(p, axis=1, keepdims=True) + jnp.exp(sink - mx))
                p_b = p.astype(BF16)
                pr += [jnp.where(own, zero_p, p_b), jnp.where(own, p_b, zero_p)]
            p_rows.append(jnp.concatenate(pr, axis=1))
            hook("softmax", blk * 4 + r)
        o_all = _dot(jnp.concatenate(p_rows, axis=0), vmasks[blk])
        att_rows.append(jnp.concatenate(
            [o_all[r * 128:(r + 1) * 128] * jnp.where(lo_half, inv[r], inv[r + 4]) for r in range(4)], axis=1))
        hook("pv", blk)

    ones_col = jnp.where(lane == 0, 1.0, 0.0).astype(BF16)
    tb = lax.broadcasted_iota(jnp.int32, (BLOCK, BLOCK), 0)
    sb = lax.broadcasted_iota(jnp.int32, (BLOCK, BLOCK), 1)
    tri = sb <= tb
    hh_rows = [[] for _ in hs]
    for blk in range(nb):
        rs = slice(blk * BLOCK, (blk + 1) * BLOCK)
        b_c = [xg[rs, 4 + h:5 + h] for h in hs]
        b_r = [xgt[4 + h:5 + h, rs] for h in hs]
        li_r = [xgt[h:h + 1, rs] for h in hs]
        kmt = [kmt_f[h][:, rs] for h in hs]
        v_ext = [jnp.concatenate([zc[rs, 1280 + h * 128:1280 + (h + 1) * 128].astype(BF16), ones_col], axis=1)
                 for h in hs]
        s_qk = [_dot(qm_b[h][rs], kmt[h].astype(BF16)) for h in hs]
        inter = [_dot(qm_b[h][rs], c_state[h].astype(BF16)) for h in hs]
        hook("mlstm_a", blk)
        dmat = [jnp.where(tri, b_c[h] - b_r[h] + li_r[h], -jnp.inf) for h in hs]
        m_inter = [b_c[h] + m_state[h] for h in hs]
        m_t = [jnp.maximum(m_inter[h], jnp.max(dmat[h], axis=1, keepdims=True)) for h in hs]
        w_inter = [jnp.exp(m_inter[h] - m_t[h]) for h in hs]
        sc_b = [(s_qk[h] * jnp.exp(dmat[h] - m_t[h])).astype(BF16) for h in hs]
        hook("mlstm_b", blk)
        nd = [_dot(sc_b[h], v_ext[h]) + w_inter[h] * inter[h] for h in hs]
        for h in hs:
            m_end = m_t[h][BLOCK - 1:BLOCK, :]
            b_last = b_c[h][BLOCK - 1:BLOCK, :]
            decay = jnp.exp(b_last + m_state[h] - m_end)
            wk_r = jnp.exp(b_last - b_r[h] + li_r[h] - m_end)
            c_state[h] = decay * c_state[h] + _dot((kmt[h] * wk_r).astype(BF16), v_ext[h])
            m_state[h] = m_end
            hh_rows[h].append(nd[h][:, :128] / jnp.maximum(jnp.abs(nd[h][:, 128:129]), jnp.exp(-m_t[h])))
        hook("mlstm_c", blk)

    y_att = _attn_head_norm(jnp.concatenate(att_rows, axis=0), gattn_ref[...])
    hook("attnorm")
    y_m = []
    for h in hs:
        hm = _sigmoid(zc[:, 1792 + h * 128:1792 + (h + 1) * 128]) * jnp.concatenate(hh_rows[h], axis=0)
        y_m.append(_mlstm_head_norm(hm, gml_ref[:, h * 128:(h + 1) * 128]))
        hook("headnorm", h)
    ymix = jnp.concatenate([y_att] + y_m, axis=1).astype(BF16)
    return ymix, (kp_lo, kp_hi, vp_lo, vp_hi, c_state, m_state)


def _project(xb, w_in_refs, lo=0, hi=IN_PAD):
    bounds = (0, ATT_W, IN_MAIN, IN_PAD)
    parts = []
    for ref, start, stop in zip(w_in_refs, bounds[:-1], bounds[1:]):
        a, b = max(lo, start), min(hi, stop)
        if a < b:
            parts.append(_dot(xb, ref[:, a - start:b - start]))
    return parts[0] if len(parts) == 1 else jnp.concatenate(parts, axis=1)


def _meta_state(meta_ref, w_inq_ref, w_inr_ref, w_ing_ref, wconv_ref, bconv_ref, wqk_ref, gb_ref,
                sinks_ref, gattn_ref,
                gml_ref, kplo_o, kphi_o, vplo_o, vphi_o, cb_o, ctn_o, mst_o, zc, cbuf):
    zc[...] = _project(meta_ref[...].astype(BF16), (w_inq_ref, w_inr_ref, w_ing_ref))
    cbuf[0:8, :] = jnp.zeros((8, MLSTM_W), F32)
    zb = jnp.zeros((BLOCK, KV_W), BF16)
    state = (zb, zb, zb, zb, [jnp.zeros((M_HEAD_DIM, 2 * M_HEAD_DIM), F32)] * M_HEADS,
             [jnp.zeros((1, 1), F32)] * M_HEADS)
    _, (kp_lo, kp_hi, vp_lo, vp_hi, c_state, m_state) = _mix_blocks(
        zc, 1, META_PAD, BLOCK + META_PAD, state, cbuf, wconv_ref, bconv_ref, wqk_ref, gb_ref,
        sinks_ref, gattn_ref, gml_ref, lambda name, index=0: None)
    kplo_o[...] = kp_lo
    kphi_o[...] = kp_hi
    vplo_o[...] = vp_lo
    vphi_o[...] = vp_hi
    cb_o[...] = cbuf[0:8, :]
    for h in range(M_HEADS):
        ctn_o[h] = c_state[h]
        mst_o[h:h + 1, :] = jnp.broadcast_to(m_state[h], (1, 128))
    mst_o[M_HEADS:8, :] = jnp.zeros((8 - M_HEADS, 128), F32)


def _pair_kernel(xin_ref, xres_ref, meta_ref, xs_ref, ymix_s_ref,
                 w_inq_ref, w_inr_ref, w_ing_ref, wconv_ref, bconv_ref, wqk_ref, gb_ref, sinks_ref,
                 gattn_ref, gml_ref,
                 wout_ref, ln1g_ref, ln1b_ref, wg_ref, wu_ref, wd_ref, ln2g_ref, ln2b_ref,
                 y_ref, ys_ref, pk_ref, pv_ref, pconv_ref, pc_ref, pn_ref, pm_ref,
                 kplo_i, kphi_i, vplo_i, vphi_i, cb_i, ctn_i, mst_i,
                 zcur, znext, kvlast, kplo, kphi, vplo, vphi, cbuf, ctn, mst, yprev, *, npair, n_pair_steps):
    t = pl.program_id(0)
    p = lax.rem(t + (npair - 1), npair)

    @pl.when(t == 0)
    def _init_pipeline():
        zcur[...] = jnp.zeros_like(zcur)
        yprev[...] = jnp.zeros_like(yprev)
        _meta_state(meta_ref, w_inq_ref, w_inr_ref, w_ing_ref, wconv_ref, bconv_ref, wqk_ref, gb_ref,
                    sinks_ref, gattn_ref, gml_ref, kplo_i, kphi_i, vplo_i, vphi_i, cb_i, ctn_i, mst_i,
                    znext.at[pl.ds(0, BLOCK)], cbuf)

    @pl.when((p == 0) | (t == 0))
    def _load_meta_state():
        kplo[...] = kplo_i[...]
        kphi[...] = kphi_i[...]
        vplo[...] = vplo_i[...]
        vphi[...] = vphi_i[...]
        cbuf[0:8, :] = cb_i[...]
        ctn[...] = ctn_i[...]
        mst[...] = mst_i[...]

    ffn_w = (wout_ref, ln1g_ref, ln1b_ref, wg_ref, wu_ref, wd_ref, ln2g_ref, ln2b_ref)

    @pl.when(t < n_pair_steps)
    def _pair_step():
        ffn = _FfnStages(xres_ref[...], yprev[...], *ffn_w)
        xb_b = xin_ref[...].astype(BF16)

        n_proj = -(-IN_PAD // PROJ_CHUNK)

        def proj(n):
            lo, hi = n * PROJ_CHUNK, min((n + 1) * PROJ_CHUNK, IN_PAD)
            znext[:, lo:hi] = _project(xb_b, (w_inq_ref, w_inr_ref, w_ing_ref), lo, hi)

        proj_it, ffn_it = iter(range(n_proj)), iter(range(ffn.N_CHUNKS))

        def fill_proj(n):
            for c in [c for _, c in zip(range(n), proj_it)]:
                proj(c)

        def fill_ffn(n):
            for c in [c for _, c in zip(range(n), ffn_it)]:
                ffn.chunk(c)

        out = {}

        def hook(name, index=0):
            if name == "scores":
                fill_proj(2)
                ffn.head()
            elif name == "front":
                fill_proj(1)
            elif name == "gates":
                fill_ffn(1)
            elif name == "softmax":
                if index % 2 == 1:
                    fill_ffn(1)
            elif name in ("pv", "mlstm_a", "mlstm_b"):
                fill_ffn(1)
            elif name == "mlstm_c":
                fill_proj(1)
            elif name == "attnorm":
                fill_ffn(ffn.N_CHUNKS)
                out["y"] = ffn.tail()
                fill_proj(1)
            elif name == "headnorm":
                fill_proj(1)

        state = (kplo[...], kphi[...], vplo[...], vphi[...], [ctn[h] for h in range(M_HEADS)],
                 [mst[h:h + 1, 0:1] for h in range(M_HEADS)])
        thr_first = jnp.where(p == 0, META_PAD, 0)
        kvlast[...] = zcur[(STEP_BLOCKS - 1) * BLOCK:STEP_BLOCKS * BLOCK, ATT_W:ATT_W + 2 * KV_W]
        ymix, (kp_lo, kp_hi, vp_lo, vp_hi, c_state, m_state) = _mix_blocks(
            zcur, STEP_BLOCKS, 0, thr_first, state, cbuf, wconv_ref, bconv_ref, wqk_ref, gb_ref,
            sinks_ref, gattn_ref, gml_ref, hook)
        fill_proj(n_proj)
        y_ref[...] = out["y"]
        kplo[...] = kp_lo
        kphi[...] = kp_hi
        vplo[...] = vp_lo
        vphi[...] = vp_hi
        for h in range(M_HEADS):
            ctn[h] = c_state[h]
            mst[h:h + 1, :] = jnp.broadcast_to(m_state[h], (1, 128))
        yprev[...] = ymix
        zcur[...] = znext[...]

    @pl.when(t >= n_pair_steps)
    def _sample_ffn_step():
        ys_ref[...] = _FfnStages(xs_ref[...], ymix_s_ref[...], *ffn_w).run()

    @pl.when((p == npair - 1) & (t < n_pair_steps))
    def _final():
        pk_ref[...] = kvlast[:, 0:KV_W]
        pv_ref[...] = kvlast[:, KV_W:2 * KV_W]
        pconv_ref[...] = cbuf[8 - (CONV_W - 1):8, :]
        pm_ref[...] = mst[...]
        for h in range(M_HEADS):
            c_n = ctn[h].T
            pc_ref[h] = c_n[0:M_HEAD_DIM, :]
            pn_ref[h:h + 1, :] = c_n[M_HEAD_DIM:M_HEAD_DIM + 1, :]


def _sample_kernel(xs_ref, ck_ref, cv_ref, sconv_ref, sc_ref, sn_ref, sm_ref,
                   w_inq_ref, w_inr_ref, w_ing_ref, wconv_ref, bconv_ref, wqk_ref, gb_ref, sinks_ref,
                   gattn_ref, gml_ref,
                   ymix_ref, sk_ref, sv_ref, scv_ref, sco_ref, sno_ref, smo_ref, cext):
    ns, L = SEQ_TILE, 8
    xs = xs_ref[...].reshape(ns * L, D_MODEL)
    z = _project(xs.astype(BF16), (w_inq_ref, w_inr_ref, w_ing_ref))

    lane = lax.broadcasted_iota(jnp.int32, (BLOCK, 128), 1)
    lo_half = lane < HEAD_DIM
    lo3 = lax.broadcasted_iota(jnp.int32, (ns, L, 128), 2) < HEAD_DIM
    ti = lax.broadcasted_iota(jnp.int32, (BLOCK, BLOCK), 0)
    si = lax.broadcasted_iota(jnp.int32, (BLOCK, BLOCK), 1)
    same_seq = (ti >> 3) == (si >> 3)
    tl, sl_ = ti & 7, si & 7
    causal_seq = same_seq & (sl_ <= tl)

    k_new = z[:, ATT_W:ATT_W + KV_W]
    v_new = z[:, ATT_W + KV_W:ATT_W + 2 * KV_W]
    kb, vb = k_new.astype(BF16), v_new.astype(BF16)
    zero_b = jnp.zeros_like(kb)
    kn_mask = jnp.concatenate([jnp.where(lo_half, kb, zero_b), jnp.where(lo_half, zero_b, kb)], axis=0)
    vn_mask = jnp.concatenate([jnp.where(lo_half, vb, zero_b), jnp.where(lo_half, zero_b, vb)], axis=0)
    qg = [z[:, r * 128:(r + 1) * 128] for r in range(4)]
    s_new = _dot_nt(jnp.concatenate(qg, axis=0).astype(BF16), kn_mask)
    qg3 = [q.reshape(ns, L, 128) for q in qg]
    q8 = jnp.concatenate([jnp.where(lo3, q, 0.0) for q in qg3] + [jnp.where(lo3, 0.0, q) for q in qg3],
                         axis=1).astype(BF16)
    s_cache = jnp.einsum('nqd,ndk->nqk', q8, ck_ref[...].astype(BF16), preferred_element_type=F32)

    c3 = z[:, 768:1280].reshape(ns, L, MLSTM_W)
    cext[:, 8 - (CONV_W - 1):8, :] = sconv_ref[...]
    cext[:, 8:16, :] = c3
    scv_ref[...] = c3[:, L - (CONV_W - 1):L, :]
    c_act = _conv_silu(lambda jj: cext[:, 5 + jj:5 + jj + L, :], wconv_ref, bconv_ref)
    c_act = c_act.reshape(BLOCK, MLSTM_W)
    lblk = jnp.where(causal_seq, 1.0, 0.0).astype(BF16)
    lseq = jnp.where(same_seq, 1.0, 0.0).astype(BF16)
    xg, blast = _gate_columns(z[:, IN_MAIN:IN_PAD], gb_ref, None, lblk, extra=lseq)
    xgt = xg.T
    m_rep = jnp.broadcast_to(sm_ref[...], (ns, L, 128)).reshape(BLOCK, 128)
    hs = range(M_HEADS)
    qk = [_dot(c_act[:, h * 128:(h + 1) * 128].astype(BF16), wqk_ref[h]) for h in hs]
    qm = [qk[h][:, :128] for h in hs]
    qm_b = [qm[h].astype(BF16) for h in hs]
    km = [qk[h][:, 128:] * (M_HEAD_DIM ** -0.5) for h in hs]
    km_b = [km[h].astype(BF16) for h in hs]
    vh_b = [z[:, 1280 + h * 128:1280 + (h + 1) * 128].astype(BF16) for h in hs]
    c_old = [sc_ref[:, h] for h in hs]
    s_qk = [_dot_nt(qm_b[h], km_b[h]) for h in hs]
    cq = [_dot_nt(qm_b[h], c_old[h].reshape(ns * 128, 128).astype(BF16)) for h in hs]

    mask_c = si > tl
    dist_c = (WINDOW + tl - si).astype(F32)
    dist_n = (tl - sl_).astype(F32)
    pc_parts, pn_rows, inv = {}, [], {}
    for r in range(4):
        pn = []
        for c in range(2):
            h = r + 4 * c
            sc_c = s_cache[:, c * 32 + r * L:c * 32 + (r + 1) * L, :].reshape(BLOCK, 128)
            sc_n = s_new[r * 128:(r + 1) * 128, c * 128:(c + 1) * 128]
            (p_c, p_n), inv[h] = _softmax_parts(
                [(sc_c, mask_c, -ALIBI_SLOPES[h] * dist_c), (sc_n, causal_seq, -ALIBI_SLOPES[h] * dist_n)],
                sinks_ref[h])
            pc_parts[(c, r)] = p_c.reshape(ns, L, 128)
            pn.append(p_n.astype(BF16))
        pn_rows.append(jnp.concatenate(pn, axis=1))
    p8 = jnp.concatenate([pc_parts[(c, r)] for c in range(2) for r in range(4)], axis=1).astype(BF16)
    o_cache = jnp.einsum('nqk,ndk->nqd', p8, cv_ref[...].astype(BF16), preferred_element_type=F32)
    o_new = _dot(jnp.concatenate(pn_rows, axis=0), vn_mask)
    groups = []
    for r in range(4):
        oa = o_cache[:, r * L:(r + 1) * L, :].reshape(BLOCK, 128)
        ob = o_cache[:, 32 + r * L:32 + (r + 1) * L, :].reshape(BLOCK, 128)
        o = jnp.where(lo_half, oa, ob) + o_new[r * 128:(r + 1) * 128]
        groups.append(o * jnp.where(lo_half, inv[r], inv[r + 4]))
    y_att = _attn_head_norm(jnp.concatenate(groups, axis=1), gattn_ref[...])

    sk_ref[:, 0:WINDOW - L, :] = jnp.swapaxes(ck_ref[...], 1, 2)[:, L:WINDOW, :]
    sk_ref[:, WINDOW - L:WINDOW, :] = k_new.reshape(ns, L, 128)
    sv_ref[:, 0:WINDOW - L, :] = jnp.swapaxes(cv_ref[...], 1, 2)[:, L:WINDOW, :]
    sv_ref[:, WINDOW - L:WINDOW, :] = v_new.reshape(ns, L, 128)

    b_c = [xg[:, 4 + h:5 + h] for h in hs]
    li_c = [xg[:, h:h + 1] for h in hs]
    b_r = [xgt[4 + h:5 + h, :] for h in hs]
    li_r = [xgt[h:h + 1, :] for h in hs]
    bl_c = [blast[:, 4 + h:5 + h] for h in hs]
    m_prev = [m_rep[:, h:h + 1] for h in hs]
    dmat = [jnp.where(causal_seq, b_c[h] - b_r[h] + li_r[h], -jnp.inf) for h in hs]
    m_inter = [b_c[h] + m_prev[h] for h in hs]
    m_t = [jnp.maximum(m_inter[h], jnp.max(dmat[h], axis=1, keepdims=True)) for h in hs]
    w_inter = [jnp.exp(m_inter[h] - m_t[h]) for h in hs]
    sc = [s_qk[h] * jnp.exp(dmat[h] - m_t[h]) for h in hs]
    num_intra = [_dot(sc[h].astype(BF16), vh_b[h]) for h in hs]

    m_end3 = [jnp.broadcast_to(m_t[h], (BLOCK, 128)).reshape(ns, L, 128)[:, L - 1:L, :] for h in hs]
    m_end = [jnp.broadcast_to(m_end3[h], (ns, L, 128)).reshape(BLOCK, 128)[:, 0:1] for h in hs]
    decay = [jnp.exp(bl_c[h] + m_prev[h] - m_end[h]) for h in hs]
    kw = [km[h] * jnp.exp(bl_c[h] - b_c[h] + li_c[h] - m_end[h]) for h in hs]
    seq_of_row = ti >> 3
    lane3 = lax.broadcasted_iota(jnp.int32, (ns, 1, 128), 2)
    m_out = jnp.zeros((ns, 1, 128), F32)
    for h in hs:
        kw_b = kw[h].astype(BF16)
        k_big = jnp.concatenate(
            [jnp.where(seq_of_row == g, kw_b, jnp.zeros_like(kw_b)) for g in range(ns)], axis=1)
        c_up = _dot_tn(vh_b[h], k_big)
        c_up3 = jnp.stack([c_up[:, g * 128:(g + 1) * 128] for g in range(ns)], axis=0)
        decay3 = jnp.broadcast_to(decay[h], (BLOCK, 128)).reshape(ns, L, 128)[:, 0:1, :]
        n_old = sn_ref[:, h:h + 1, :]
        sco_ref[:, h] = decay3 * c_old[h] + c_up3
        sno_ref[:, h:h + 1, :] = decay3 * n_old + jnp.sum(kw[h].reshape(ns, L, 128), axis=1, keepdims=True)
        m_out = jnp.where(lane3 == h, m_end3[h], m_out)
    smo_ref[...] = m_out

    y_m = []
    for h in hs:
        num_inter = jnp.concatenate([cq[h][g * L:(g + 1) * L, g * 128:(g + 1) * 128] for g in range(ns)], axis=0)
        n_rep = jnp.broadcast_to(sn_ref[:, h:h + 1, :], (ns, L, 128)).reshape(BLOCK, 128)
        num = num_intra[h] + w_inter[h] * num_inter
        den = jnp.sum(sc[h], axis=1, keepdims=True) + w_inter[h] * jnp.sum(qm[h] * n_rep, axis=1, keepdims=True)
        hh = num / jnp.maximum(jnp.abs(den), jnp.exp(-m_t[h]))
        hm = _sigmoid(z[:, 1792 + h * 128:1792 + (h + 1) * 128]) * hh
        y_m.append(_mlstm_head_norm(hm, gml_ref[:, h * 128:(h + 1) * 128]))

    ymix_ref[...] = jnp.concatenate([y_att] + y_m, axis=1).astype(BF16)


def _const_spec(shape):
    nd = len(shape)
    return pl.BlockSpec(shape, lambda *_: (0,) * nd, pipeline_mode=pl.Buffered(1))


def _mixer_weight_specs():
    return [
        _const_spec((D_MODEL, ATT_W)),
        _const_spec((D_MODEL, IN_MAIN - ATT_W)),
        _const_spec((D_MODEL, IN_PAD - IN_MAIN)),
        _const_spec((CONV_W, MLSTM_W)),
        _const_spec((1, MLSTM_W)),
        _const_spec((M_HEADS, M_HEAD_DIM, 2 * M_HEAD_DIM)),
        _const_spec((1, 128)),
        pl.BlockSpec(memory_space=pltpu.SMEM),
        _const_spec((1, ATT_W)),
        _const_spec((1, MLSTM_W)),
    ]


def _post_weight_specs(d):
    return [
        _const_spec((d, d)),
        _const_spec((1, d)), _const_spec((1, d)),
        _const_spec((d, D_FF)), _const_spec((d, D_FF)), _const_spec((D_FF, d)),
        _const_spec((1, d)), _const_spec((1, d)),
    ]


def _prompt_layer(x_prompt, meta_blk, x_sample_rows, ymix_sample, mixer_w, post_w):
    b, s, d = x_prompt.shape
    state_shapes = [((BLOCK, KV_W), BF16)] * 4 + [
        ((8, MLSTM_W), F32), ((M_HEADS, M_HEAD_DIM, 2 * M_HEAD_DIM), F32), ((8, 128), F32)]
    rows = STEP_BLOCKS * BLOCK
    npair = s // rows
    total = b * npair
    n_pair_steps = total + 2
    n_sample_steps = x_sample_rows.shape[0] // rows
    steps = n_pair_steps + n_sample_steps

    def block_map(lag):
        def index(t):
            tb = jnp.clip(t - lag, 0, total - 1)
            return (tb // npair, tb % npair, 0)
        return index

    def smap(t):
        return (jnp.clip(t - 1, 0, total - 1) // npair, 0, 0)

    def sample_map(t):
        return (jnp.clip(t - n_pair_steps, 0, n_sample_steps - 1), 0)

    out_shape = (
        jax.ShapeDtypeStruct((b, s, d), F32),
        jax.ShapeDtypeStruct(x_sample_rows.shape, F32),
        jax.ShapeDtypeStruct((b, BLOCK, KV_W), F32),
        jax.ShapeDtypeStruct((b, BLOCK, KV_W), F32),
        jax.ShapeDtypeStruct((b, CONV_W - 1, MLSTM_W), F32),
        jax.ShapeDtypeStruct((b, M_HEADS, M_HEAD_DIM, M_HEAD_DIM), F32),
        jax.ShapeDtypeStruct((b, M_HEADS, M_HEAD_DIM), F32),
        jax.ShapeDtypeStruct((b, 8, 128), F32),
    )
    out_specs = (
        pl.BlockSpec((None, rows, d), block_map(2)),
        pl.BlockSpec((rows, d), sample_map),
        pl.BlockSpec((None, BLOCK, KV_W), smap),
        pl.BlockSpec((None, BLOCK, KV_W), smap),
        pl.BlockSpec((None, CONV_W - 1, MLSTM_W), smap),
        pl.BlockSpec((None, M_HEADS, M_HEAD_DIM, M_HEAD_DIM), lambda t: smap(t) + (0,)),
        pl.BlockSpec((None, M_HEADS, M_HEAD_DIM), smap),
        pl.BlockSpec((None, 8, 128), smap),
    )
    return pl.pallas_call(
        functools.partial(_pair_kernel, npair=npair, n_pair_steps=n_pair_steps),
        grid=(steps,),
        in_specs=[pl.BlockSpec((None, rows, d), block_map(0)), pl.BlockSpec((None, rows, d), block_map(2)),
                  _const_spec((BLOCK, d)), pl.BlockSpec((rows, d), sample_map), pl.BlockSpec((rows, d), sample_map)]
        + _mixer_weight_specs() + _post_weight_specs(d),
        out_specs=out_specs,
        out_shape=out_shape,
        scratch_shapes=[pltpu.VMEM(shape, dt) for shape, dt in state_shapes] + [
            pltpu.VMEM((rows, IN_PAD), F32),
            pltpu.VMEM((rows, IN_PAD), F32),
            pltpu.VMEM((BLOCK, 2 * KV_W), F32),
        ] + [pltpu.VMEM((BLOCK, KV_W), BF16)] * 4 + [
            pltpu.VMEM((rows + 8, MLSTM_W), F32),
            pltpu.VMEM((M_HEADS, M_HEAD_DIM, 2 * M_HEAD_DIM), F32),
            pltpu.VMEM((8, 128), F32),
            pltpu.VMEM((rows, d), BF16),
        ],
        compiler_params=pltpu.CompilerParams(
            dimension_semantics=("arbitrary",), vmem_limit_bytes=VMEM_LIMIT),
        name="prompt_layer",
    )(x_prompt, x_prompt, meta_blk, x_sample_rows, ymix_sample, *mixer_w, *post_w)


def _sample_mixer(x_sample, ck, cv, sconv8, s_c, s_n, s_m3, mixer_w):
    n, l, d = x_sample.shape
    t = SEQ_TILE
    m3 = lambda i: (i, 0, 0)
    m4 = lambda i: (i, 0, 0, 0)
    out_shape = (
        jax.ShapeDtypeStruct((n * l, d), BF16),
        jax.ShapeDtypeStruct((n, WINDOW, KV_W), F32),
        jax.ShapeDtypeStruct((n, WINDOW, KV_W), F32),
        jax.ShapeDtypeStruct((n, CONV_W - 1, MLSTM_W), F32),
        jax.ShapeDtypeStruct((n, M_HEADS, M_HEAD_DIM, M_HEAD_DIM), F32),
        jax.ShapeDtypeStruct((n, M_HEADS, M_HEAD_DIM), F32),
        jax.ShapeDtypeStruct((n, 1, 128), F32),
    )
    out_specs = (
        pl.BlockSpec((t * l, d), lambda i: (i, 0)),
        pl.BlockSpec((t, WINDOW, KV_W), m3),
        pl.BlockSpec((t, WINDOW, KV_W), m3),
        pl.BlockSpec((t, CONV_W - 1, MLSTM_W), m3),
        pl.BlockSpec((t, M_HEADS, M_HEAD_DIM, M_HEAD_DIM), m4),
        pl.BlockSpec((t, M_HEADS, M_HEAD_DIM), m3),
        pl.BlockSpec((t, 1, 128), m3),
    )
    in_specs = [
        pl.BlockSpec((t, l, d), m3),
        pl.BlockSpec((t, WINDOW, KV_W), m3),
        pl.BlockSpec((t, WINDOW, KV_W), m3),
        pl.BlockSpec((t, CONV_W - 1, MLSTM_W), m3),
        pl.BlockSpec((t, M_HEADS, M_HEAD_DIM, M_HEAD_DIM), m4),
        pl.BlockSpec((t, M_HEADS, M_HEAD_DIM), m3),
        pl.BlockSpec((t, 1, 128), m3),
    ] + _mixer_weight_specs()
    return pl.pallas_call(
        _sample_kernel,
        grid=(n // t,),
        in_specs=in_specs,
        out_specs=out_specs,
        out_shape=out_shape,
        scratch_shapes=[pltpu.VMEM((t, 16, MLSTM_W), F32)],
        compiler_params=pltpu.CompilerParams(
            dimension_semantics=("arbitrary",), vmem_limit_bytes=VMEM_LIMIT),
        name="sample_mixer",
    )(x_sample, ck, cv, sconv8, s_c, s_n, s_m3, *mixer_w)


def kernel(x_prompt, x_sample, cache_k, cache_v, state_conv, state_C, state_n, state_m, meta_tokens,
           w_in, w_conv, b_conv, w_mq, w_mk, b_i, b_f, attn_sinks, g_attn, g_mlstm, w_out,
           ln1_g, ln1_b, w_gate, w_up, w_down, ln2_g, ln2_b):
    b, s, d = x_prompt.shape
    n, l, _ = x_sample.shape

    w0 = w_in[0]
    w_in_parts = (_pair_heads(w0[:, :ATT_W], 1).astype(BF16), w0[:, ATT_W:IN_MAIN].astype(BF16),
                  jnp.pad(w0[:, IN_MAIN:], ((0, 0), (0, IN_PAD - IN_MAIN - IN_GATES))).astype(BF16))
    wqk = jnp.concatenate([w_mq[0], w_mk[0]], axis=-1).astype(BF16)
    gb = jnp.concatenate([b_i[0], b_f[0], jnp.zeros((128 - IN_GATES,), F32)]).reshape(1, 128)
    mixer_w = w_in_parts + (w_conv[0], b_conv[0].reshape(1, MLSTM_W), wqk, gb, attn_sinks[0],
                            _pair_heads(g_attn[0].reshape(1, ATT_W), 1), g_mlstm[0].reshape(1, MLSTM_W))
    w_out_p = jnp.concatenate([_pair_heads(w_out[0][:ATT_W], 0), w_out[0][ATT_W:]], axis=0).astype(BF16)
    post_w = (w_out_p, ln1_g[0].reshape(1, d), ln1_b[0].reshape(1, d),
              w_gate[0].astype(BF16), w_up[0].astype(BF16), w_down[0].astype(BF16),
              ln2_g[0].reshape(1, d), ln2_b[0].reshape(1, d))

    key_minor = lambda c: c[0].transpose(0, 2, 3, 1).reshape(n, KV_W, WINDOW)
    s_m3 = jnp.pad(state_m[0], ((0, 0), (0, 128 - M_HEADS))).reshape(n, 1, 128)
    ymix_s, sk, sv, scv, s_c, s_n, s_mo = _sample_mixer(
        x_sample, key_minor(cache_k), key_minor(cache_v), state_conv[0], state_C[0], state_n[0], s_m3, mixer_w)

    meta_blk = jnp.concatenate([jnp.zeros((META_PAD, d), x_prompt.dtype), meta_tokens.astype(x_prompt.dtype)], axis=0)
    y_prompt, y_sample, pk, pv, pconv, p_c, p_n, pm8 = _prompt_layer(
        x_prompt, meta_blk, x_sample.reshape(n * l, d), ymix_s, mixer_w, post_w)

    kv5 = lambda a: a.reshape(1, a.shape[0], WINDOW, 2, HEAD_DIM)
    return (y_prompt, y_sample.reshape(n, l, d),
            kv5(pk), kv5(pv), pconv[None], p_c[None], p_n[None], pm8[None, :, :M_HEADS, 0],
            kv5(sk), kv5(sv), scv[None], s_c[None], s_n[None], s_mo[None, :, 0, :M_HEADS])
```

```python
import functools

import jax
import jax.numpy as jnp
from jax import lax
from jax.experimental import pallas as pl
from jax.experimental.pallas import tpu as pltpu

F32 = jnp.float32
BF16 = jnp.bfloat16

D_MODEL = 1024
ATT_W = 512
MLSTM_W = 512
HEAD_DIM = 64
N_HEADS = 8
KV_W = 128
WINDOW = 128
BLOCK = 128
M_HEADS = 4
M_HEAD_DIM = 128
CONV_W = 4
N_META = 16
META_PAD = BLOCK - N_META
D_FF = 2816
IN_MAIN = 2304
IN_GATES = 2 * M_HEADS
IN_PAD = IN_MAIN + 128
DEPTH = 1
ALIBI_SLOPES = tuple(2.0 ** (-8.0 * (h + 1) / N_HEADS) for h in range(N_HEADS))
DEEPNORM_ALPHA = (2.0 * DEPTH) ** 0.25
EPS = 1e-5
SEQ_TILE = 16
STEP_BLOCKS = 2
FF_CHUNK = 256
FF_LOOKAHEAD = 3
PROJ_CHUNK = 256
VMEM_LIMIT = 56 * 1024 * 1024


def _pair_heads(a, axis):
    shape = a.shape
    a = a.reshape(shape[:axis] + (2, 4, HEAD_DIM) + shape[axis + 1:])
    return jnp.swapaxes(a, axis, axis + 1).reshape(shape)


def _dot(a, b):
    return jnp.dot(a, b, preferred_element_type=F32)


def _dot_nt(a, b):
    return lax.dot_general(a, b, (((1,), (1,)), ((), ())), preferred_element_type=F32)


def _dot_tn(a, b):
    return lax.dot_general(a, b, (((0,), (0,)), ((), ())), preferred_element_type=F32)


def _split2(x):
    hi = x.astype(BF16)
    lo = (x - hi.astype(F32)).astype(BF16)
    return hi, lo


def _sigmoid(x):
    return 1.0 / (1.0 + jnp.exp(-x))


def _log_sigmoid(x):
    return -(jnp.maximum(-x, 0.0) + jnp.log1p(jnp.exp(-jnp.abs(x))))


def _layer_norm(x, g, b):
    mu = jnp.mean(x, axis=-1, keepdims=True)
    xc = x - mu
    var = jnp.mean(xc * xc, axis=-1, keepdims=True)
    return xc * lax.rsqrt(var + EPS) * g + b


def _attn_head_norm(att, g):
    lo_half = lax.broadcasted_iota(jnp.int32, (att.shape[0], 128), 1) < HEAD_DIM

    def seg_mean(x):
        s_all = jnp.sum(x, axis=1, keepdims=True)
        s_lo = jnp.sum(jnp.where(lo_half, x, 0.0), axis=1, keepdims=True)
        return jnp.where(lo_half, s_lo, s_all - s_lo) * (1.0 / HEAD_DIM)

    out = []
    for grp in range(att.shape[1] // 128):
        x = att[:, grp * 128:(grp + 1) * 128]
        xc = x - seg_mean(x)
        out.append(xc * lax.rsqrt(seg_mean(xc * xc) + EPS))
    return jnp.concatenate(out, axis=1) * g


def _mlstm_head_norm(hm, g):
    mu = jnp.mean(hm, axis=-1, keepdims=True)
    xc = hm - mu
    var = jnp.mean(xc * xc, axis=-1, keepdims=True)
    return xc * lax.rsqrt(var + EPS) * g


def _softmax_parts(parts, sink):
    sp = [jnp.where(m, s * (HEAD_DIM ** -0.5) + a, -jnp.inf) for s, m, a in parts]
    mx = sink
    for s in sp:
        mx = jnp.maximum(mx, jnp.max(s, axis=1, keepdims=True))
    ps = [jnp.exp(s - mx) for s in sp]
    den = jnp.exp(sink - mx)
    for p in ps:
        den = den + jnp.sum(p, axis=1, keepdims=True)
    return ps, 1.0 / den


def _conv_silu(window, wconv_ref, bconv_ref):
    acc = bconv_ref[...]
    for j in range(CONV_W):
        acc = acc + window(j) * wconv_ref[j:j + 1, :]
    return acc * _sigmoid(acc)


def _gate_columns(gates, gb_ref, row_valid, ltri, extra=None):
    lane = lax.broadcasted_iota(jnp.int32, gates.shape, 1)
    gb = gates + gb_ref[...]
    is_i = lane < M_HEADS
    is_f = (lane >= M_HEADS) & (lane < 2 * M_HEADS)
    logf = _log_sigmoid(gb)
    if row_valid is not None:
        lf = jnp.where(is_f & row_valid, logf, 0.0)
        li = jnp.where(row_valid, gb, -jnp.inf)
    else:
        lf = jnp.where(is_f, logf, 0.0)
        li = gb
    hi_lo = jnp.concatenate(_split2(lf), axis=1)
    bc = _dot(ltri, hi_lo)
    x = jnp.where(is_i, li, bc[:, :128] + bc[:, 128:])
    if extra is None:
        return x, None
    be = _dot(extra, hi_lo)
    return x, be[:, :128] + be[:, 128:]


class _FfnStages:
    N_CHUNKS = D_FF // FF_CHUNK

    def __init__(self, x, ymix, wout_ref, ln1g_ref, ln1b_ref, wg_ref, wu_ref, wd_ref, ln2g_ref, ln2b_ref):
        self.x, self.ymix = x, ymix
        self.wout_ref, self.ln1g_ref, self.ln1b_ref = wout_ref, ln1g_ref, ln1b_ref
        self.wg_ref, self.wu_ref, self.wd_ref = wg_ref, wu_ref, wd_ref
        self.ln2g_ref, self.ln2b_ref = ln2g_ref, ln2b_ref

    def head(self):
        self.x1 = _layer_norm(DEEPNORM_ALPHA * self.x + _dot(self.ymix, self.wout_ref[...]),
                              self.ln1g_ref[...], self.ln1b_ref[...])
        self.x1b = self.x1.astype(BF16)
        self.acc = DEEPNORM_ALPHA * self.x1

    def _gate_up(self, c):
        cs = slice(c * FF_CHUNK, (c + 1) * FF_CHUNK)
        return _dot(self.x1b, self.wg_ref[:, cs]), _dot(self.x1b, self.wu_ref[:, cs])

    def chunk(self, c):
        if c == 0:
            self.gu = {}
        for n in range(c, min(c + FF_LOOKAHEAD, self.N_CHUNKS - 1) + 1):
            if n not in self.gu:
                self.gu[n] = self._gate_up(n)
        g, u = self.gu.pop(c)
        hid = (g * _sigmoid(g) * u).astype(BF16)
        self.acc = self.acc + _dot(hid, self.wd_ref[c * FF_CHUNK:(c + 1) * FF_CHUNK, :])

    def tail(self):
        return _layer_norm(self.acc, self.ln2g_ref[...], self.ln2b_ref[...])

    def run(self):
        self.head()
        for c in range(self.N_CHUNKS):
            self.chunk(c)
        return self.tail()


def _mix_blocks(zc, nb, pad_rows, thr_first, state, cbuf, wconv_ref, bconv_ref, wqk_ref, gb_ref,
                sinks_ref, gattn_ref, gml_ref, hook):
    rows = nb * BLOCK
    kp_lo, kp_hi, vp_lo, vp_hi, c_state, m_state = state
    c_state, m_state = list(c_state), list(m_state)
    lane = lax.broadcasted_iota(jnp.int32, (BLOCK, 128), 1)
    lo_half = lane < HEAD_DIM
    row_valid = None
    if pad_rows:
        row_valid = lax.broadcasted_iota(jnp.int32, (rows, 1), 0) >= pad_rows

    s_all, vmasks = [], []
    for blk in range(nb):
        rs = slice(blk * BLOCK, (blk + 1) * BLOCK)
        kb = zc[rs, ATT_W:ATT_W + KV_W].astype(BF16)
        vb = zc[rs, ATT_W + KV_W:ATT_W + 2 * KV_W].astype(BF16)
        zero_b = jnp.zeros_like(kb)
        k_lo, k_hi = jnp.where(lo_half, kb, zero_b), jnp.where(lo_half, zero_b, kb)
        v_lo, v_hi = jnp.where(lo_half, vb, zero_b), jnp.where(lo_half, zero_b, vb)
        kmask = jnp.concatenate([kp_lo, k_lo, kp_hi, k_hi], axis=0)
        vmasks.append(jnp.concatenate([vp_lo, v_lo, vp_hi, v_hi], axis=0))
        q4 = jnp.concatenate([zc[rs, r * 128:(r + 1) * 128] for r in range(4)], axis=0).astype(BF16)
        s_all.append(_dot_nt(q4, kmask))
        kp_lo, kp_hi, vp_lo, vp_hi = k_lo, k_hi, v_lo, v_hi
    hook("scores")

    c_in = zc[:, 768:1280]
    if row_valid is not None:
        c_in = jnp.where(row_valid, c_in, 0.0)
    cbuf[8:8 + rows, :] = c_in
    c_act = _conv_silu(lambda jj: cbuf[5 + jj:5 + jj + rows, :], wconv_ref, bconv_ref)
    cbuf[0:8, :] = cbuf[rows:rows + 8, :]
    hs = range(M_HEADS)
    ca_b = [c_act[:, h * 128:(h + 1) * 128].astype(BF16) for h in hs]
    qk = [_dot(ca_b[h], wqk_ref[h]) for h in hs]
    qm_b = [qk[h][:, :M_HEAD_DIM].astype(BF16) for h in hs]
    kmt_f = [(qk[h][:, M_HEAD_DIM:] * (M_HEAD_DIM ** -0.5)).T for h in hs]
    hook("front")

    ti = lax.broadcasted_iota(jnp.int32, (rows, rows), 0)
    si = lax.broadcasted_iota(jnp.int32, (rows, rows), 1)
    ltri = jnp.where((si <= ti) & ((si >> 7) == (ti >> 7)), 1.0, 0.0).astype(BF16)
    xg, _ = _gate_columns(zc[:, IN_MAIN:IN_PAD], gb_ref, row_valid, ltri)
    xgt = xg.T
    hook("gates")

    qi = lax.broadcasted_iota(jnp.int32, (BLOCK, BLOCK), 0)
    ks = lax.broadcasted_iota(jnp.int32, (BLOCK, BLOCK), 1)
    own = ks <= qi
    distf = jnp.where(own, qi - ks, WINDOW + qi - ks).astype(F32)
    zero_p = jnp.zeros((BLOCK, BLOCK), BF16)
    att_rows = []
    for blk in range(nb):
        live = None
        if blk == 0 and thr_first is not None:
            live = jnp.where(own, ks + BLOCK, ks) >= thr_first
        p_rows, inv = [], {}
        for r in range(4):
            pr = []
            for c in range(2):
                h = r + 4 * c
                s_blk = s_all[blk][r * 128:(r + 1) * 128, c * 256:(c + 1) * 256]
                sp = jnp.where(own, s_blk[:, BLOCK:], s_blk[:, :BLOCK]) * (HEAD_DIM ** -0.5) \
                    - ALIBI_SLOPES[h] * distf
                if live is not None:
                    sp = jnp.where(live, sp, -jnp.inf)
                sink = sinks_ref[h]
                mx = jnp.maximum(jnp.max(sp, axis=1, keepdims=True), sink)
                p = jnp.exp(sp - mx)
                inv[h] = 1.0 / (jnp.sum(p, axis=1, keepdims=True) + jnp.exp(sink - mx))
                p_b = p.astype(BF16)
                pr += [jnp.where(own, zero_p, p_b), jnp.where(own, p_b, zero_p)]
            p_rows.append(jnp.concatenate(pr, axis=1))
            hook("softmax", blk * 4 + r)
        o_all = _dot(jnp.concatenate(p_rows, axis=0), vmasks[blk])
        att_rows.append(jnp.concatenate(
            [o_all[r * 128:(r + 1) * 128] * jnp.where(lo_half, inv[r], inv[r + 4]) for r in range(4)], axis=1))
        hook("pv", blk)

    ones_col = jnp.where(lane == 0, 1.0, 0.0).astype(BF16)
    tb = lax.broadcasted_iota(jnp.int32, (BLOCK, BLOCK), 0)
    sb = lax.broadcasted_iota(jnp.int32, (BLOCK, BLOCK), 1)
    tri = sb <= tb
    hh_rows = [[] for _ in hs]
    for blk in range(nb):
        rs = slice(blk * BLOCK, (blk + 1) * BLOCK)
        b_c = [xg[rs, 4 + h:5 + h] for h in hs]
        b_r = [xgt[4 + h:5 + h, rs] for h in hs]
        li_r = [xgt[h:h + 1, rs] for h in hs]
        kmt = [kmt_f[h][:, rs] for h in hs]
        v_ext = [jnp.concatenate([zc[rs, 1280 + h * 128:1280 + (h + 1) * 128].astype(BF16), ones_col], axis=1)
                 for h in hs]
        s_qk = [_dot(qm_b[h][rs], kmt[h].astype(BF16)) for h in hs]
        inter = [_dot(qm_b[h][rs], c_state[h].astype(BF16)) for h in hs]
        hook("mlstm_a", blk)
        dmat = [jnp.where(tri, b_c[h] - b_r[h] + li_r[h], -jnp.inf) for h in hs]
        m_inter = [b_c[h] + m_state[h] for h in hs]
        m_t = [jnp.maximum(m_inter[h], jnp.max(dmat[h], axis=1, keepdims=True)) for h in hs]
        w_inter = [jnp.exp(m_inter[h] - m_t[h]) for h in hs]
        sc_b = [(s_qk[h] * jnp.exp(dmat[h] - m_t[h])).astype(BF16) for h in hs]
        hook("mlstm_b", blk)
        nd = [_dot(sc_b[h], v_ext[h]) + w_inter[h] * inter[h] for h in hs]
        for h in hs:
            m_end = m_t[h][BLOCK - 1:BLOCK, :]
            b_last = b_c[h][BLOCK - 1:BLOCK, :]
            decay = jnp.exp(b_last + m_state[h] - m_end)
            wk_r = jnp.exp(b_last - b_r[h] + li_r[h] - m_end)
            c_state[h] = decay * c_state[h] + _dot((kmt[h] * wk_r).astype(BF16), v_ext[h])
            m_state[h] = m_end
            hh_rows[h].append(nd[h][:, :128] / jnp.maximum(jnp.abs(nd[h][:, 128:129]), jnp.exp(-m_t[h])))
        hook("mlstm_c", blk)

    y_att = _attn_head_norm(jnp.concatenate(att_rows, axis=0), gattn_ref[...])
    hook("attnorm")
    y_m = []
    for h in hs:
        hm = _sigmoid(zc[:, 1792 + h * 128:1792 + (h + 1) * 128]) * jnp.concatenate(hh_rows[h], axis=0)
        y_m.append(_mlstm_head_norm(hm, gml_ref[:, h * 128:(h + 1) * 128]))
        hook("headnorm", h)
    ymix = jnp.concatenate([y_att] + y_m, axis=1).astype(BF16)
    return ymix, (kp_lo, kp_hi, vp_lo, vp_hi, c_state, m_state)


def _project(xb, w_in_refs, lo=0, hi=IN_PAD):
    bounds = (0, ATT_W, IN_MAIN, IN_PAD)
    parts = []
    for ref, start, stop in zip(w_in_refs, bounds[:-1], bounds[1:]):
        a, b = max(lo, start), min(hi, stop)
        if a < b:
            parts.append(_dot(xb, ref[:, a - start:b - start]))
    return parts[0] if len(parts) == 1 else jnp.concatenate(parts, axis=1)


def _meta_state(meta_ref, w_inq_ref, w_inr_ref, w_ing_ref, wconv_ref, bconv_ref, wqk_ref, gb_ref,
                sinks_ref, gattn_ref,
                gml_ref, kplo_o, kphi_o, vplo_o, vphi_o, cb_o, ctn_o, mst_o, zc, cbuf):
    zc[...] = _project(meta_ref[...].astype(BF16), (w_inq_ref, w_inr_ref, w_ing_ref))
    cbuf[0:8, :] = jnp.zeros((8, MLSTM_W), F32)
    zb = jnp.zeros((BLOCK, KV_W), BF16)
    state = (zb, zb, zb, zb, [jnp.zeros((M_HEAD_DIM, 2 * M_HEAD_DIM), F32)] * M_HEADS,
             [jnp.zeros((1, 1), F32)] * M_HEADS)
    _, (kp_lo, kp_hi, vp_lo, vp_hi, c_state, m_state) = _mix_blocks(
        zc, 1, META_PAD, BLOCK + META_PAD, state, cbuf, wconv_ref, bconv_ref, wqk_ref, gb_ref,
        sinks_ref, gattn_ref, gml_ref, lambda name, index=0: None)
    kplo_o[...] = kp_lo
    kphi_o[...] = kp_hi
    vplo_o[...] = vp_lo
    vphi_o[...] = vp_hi
    cb_o[...] = cbuf[0:8, :]
    for h in range(M_HEADS):
        ctn_o[h] = c_state[h]
        mst_o[h:h + 1, :] = jnp.broadcast_to(m_state[h], (1, 128))
    mst_o[M_HEADS:8, :] = jnp.zeros((8 - M_HEADS, 128), F32)


def _pair_kernel(xin_ref, xres_ref, meta_ref, xs_ref, ymix_s_ref,
                 w_inq_ref, w_inr_ref, w_ing_ref, wconv_ref, bconv_ref, wqk_ref, gb_ref, sinks_ref,
                 gattn_ref, gml_ref,
                 wout_ref, ln1g_ref, ln1b_ref, wg_ref, wu_ref, wd_ref, ln2g_ref, ln2b_ref,
                 y_ref, ys_ref, pk_ref, pv_ref, pconv_ref, pc_ref, pn_ref, pm_ref,
                 kplo_i, kphi_i, vplo_i, vphi_i, cb_i, ctn_i, mst_i,
                 zcur, znext, kvlast, kplo, kphi, vplo, vphi, cbuf, ctn, mst, yprev, *, npair, n_pair_steps):
    t = pl.program_id(0)
    p = lax.rem(t + (npair - 1), npair)

    @pl.when(t == 0)
    def _init_pipeline():
        zcur[...] = jnp.zeros_like(zcur)
        yprev[...] = jnp.zeros_like(yprev)
        _meta_state(meta_ref, w_inq_ref, w_inr_ref, w_ing_ref, wconv_ref, bconv_ref, wqk_ref, gb_ref,
                    sinks_ref, gattn_ref, gml_ref, kplo_i, kphi_i, vplo_i, vphi_i, cb_i, ctn_i, mst_i,
                    znext.at[pl.ds(0, BLOCK)], cbuf)

    @pl.when((p == 0) | (t == 0))
    def _load_meta_state():
        kplo[...] = kplo_i[...]
        kphi[...] = kphi_i[...]
        vplo[...] = vplo_i[...]
        vphi[...] = vphi_i[...]
        cbuf[0:8, :] = cb_i[...]
        ctn[...] = ctn_i[...]
        mst[...] = mst_i[...]

    ffn_w = (wout_ref, ln1g_ref, ln1b_ref, wg_ref, wu_ref, wd_ref, ln2g_ref, ln2b_ref)

    @pl.when(t < n_pair_steps)
    def _pair_step():
        ffn = _FfnStages(xres_ref[...], yprev[...], *ffn_w)
        xb_b = xin_ref[...].astype(BF16)

        n_proj = -(-IN_PAD // PROJ_CHUNK)

        def proj(n):
            lo, hi = n * PROJ_CHUNK, min((n + 1) * PROJ_CHUNK, IN_PAD)
            znext[:, lo:hi] = _project(xb_b, (w_inq_ref, w_inr_ref, w_ing_ref), lo, hi)

        proj_it, ffn_it = iter(range(n_proj)), iter(range(ffn.N_CHUNKS))

        def fill_proj(n):
            for c in [c for _, c in zip(range(n), proj_it)]:
                proj(c)

        def fill_ffn(n):
            for c in [c for _, c in zip(range(n), ffn_it)]:
                ffn.chunk(c)

        out = {}

        def hook(name, index=0):
            if name == "scores":
                fill_proj(2)
                ffn.head()
            elif name == "front":
                fill_proj(1)
            elif name == "gates":
                fill_ffn(1)
            elif name == "softmax":
                if index % 2 == 1:
                    fill_ffn(1)
            elif name in ("pv", "mlstm_a", "mlstm_b"):
                fill_ffn(1)
            elif name == "mlstm_c":
                fill_proj(1)
            elif name == "attnorm":
                fill_ffn(ffn.N_CHUNKS)
                out["y"] = ffn.tail()
                fill_proj(1)
            elif name == "headnorm":
                fill_proj(1)

        state = (kplo[...], kphi[...], vplo[...], vphi[...], [ctn[h] for h in range(M_HEADS)],
                 [mst[h:h + 1, 0:1] for h in range(M_HEADS)])
        thr_first = jnp.where(p == 0, META_PAD, 0)
        kvlast[...] = zcur[(STEP_BLOCKS - 1) * BLOCK:STEP_BLOCKS * BLOCK, ATT_W:ATT_W + 2 * KV_W]
        ymix, (kp_lo, kp_hi, vp_lo, vp_hi, c_state, m_state) = _mix_blocks(
            zcur, STEP_BLOCKS, 0, thr_first, state, cbuf, wconv_ref, bconv_ref, wqk_ref, gb_ref,
            sinks_ref, gattn_ref, gml_ref, hook)
        fill_proj(n_proj)
        y_ref[...] = out["y"]
        kplo[...] = kp_lo
        kphi[...] = kp_hi
        vplo[...] = vp_lo
        vphi[...] = vp_hi
        for h in range(M_HEADS):
            ctn[h] = c_state[h]
            mst[h:h + 1, :] = jnp.broadcast_to(m_state[h], (1, 128))
        yprev[...] = ymix
        zcur[...] = znext[...]

    @pl.when(t >= n_pair_steps)
    def _sample_ffn_step():
        ys_ref[...] = _FfnStages(xs_ref[...], ymix_s_ref[...], *ffn_w).run()

    @pl.when((p == npair - 1) & (t < n_pair_steps))
    def _final():
        pk_ref[...] = kvlast[:, 0:KV_W].T
        pv_ref[...] = kvlast[:, KV_W:2 * KV_W].T
        pconv_ref[...] = cbuf[8 - (CONV_W - 1):8, :]
        pm_ref[...] = mst[...]
        for h in range(M_HEADS):
            c_n = ctn[h].T
            pc_ref[h] = c_n[0:M_HEAD_DIM, :]
            pn_ref[h:h + 1, :] = c_n[M_HEAD_DIM:M_HEAD_DIM + 1, :]


def _sample_kernel(xs_ref, ck_ref, cv_ref, sconv_ref, sc_ref, sn_ref, sm_ref,
                   w_inq_ref, w_inr_ref, w_ing_ref, wconv_ref, bconv_ref, wqk_ref, gb_ref, sinks_ref,
                   gattn_ref, gml_ref,
                   ymix_ref, sk_ref, sv_ref, scv_ref, sco_ref, sno_ref, smo_ref, cext):
    ns, L = SEQ_TILE, 8
    xs = xs_ref[...].reshape(ns * L, D_MODEL)
    z = _project(xs.astype(BF16), (w_inq_ref, w_inr_ref, w_ing_ref))

    lane = lax.broadcasted_iota(jnp.int32, (BLOCK, 128), 1)
    lo_half = lane < HEAD_DIM
    lo3 = lax.broadcasted_iota(jnp.int32, (ns, L, 128), 2) < HEAD_DIM
    ti = lax.broadcasted_iota(jnp.int32, (BLOCK, BLOCK), 0)
    si = lax.broadcasted_iota(jnp.int32, (BLOCK, BLOCK), 1)
    same_seq = (ti >> 3) == (si >> 3)
    tl, sl_ = ti & 7, si & 7
    causal_seq = same_seq & (sl_ <= tl)

    k_new = z[:, ATT_W:ATT_W + KV_W]
    v_new = z[:, ATT_W + KV_W:ATT_W + 2 * KV_W]
    kb, vb = k_new.astype(BF16), v_new.astype(BF16)
    zero_b = jnp.zeros_like(kb)
    kn_mask = jnp.concatenate([jnp.where(lo_half, kb, zero_b), jnp.where(lo_half, zero_b, kb)], axis=0)
    vn_mask = jnp.concatenate([jnp.where(lo_half, vb, zero_b), jnp.where(lo_half, zero_b, vb)], axis=0)
    qg = [z[:, r * 128:(r + 1) * 128] for r in range(4)]
    s_new = _dot_nt(jnp.concatenate(qg, axis=0).astype(BF16), kn_mask)
    qg3 = [q.reshape(ns, L, 128) for q in qg]
    q8 = jnp.concatenate([jnp.where(lo3, q, 0.0) for q in qg3] + [jnp.where(lo3, 0.0, q) for q in qg3],
                         axis=1).astype(BF16)
    s_cache = jnp.einsum('nqd,ndk->nqk', q8, ck_ref[...].astype(BF16), preferred_element_type=F32)

    c3 = z[:, 768:1280].reshape(ns, L, MLSTM_W)
    cext[:, 8 - (CONV_W - 1):8, :] = sconv_ref[...]
    cext[:, 8:16, :] = c3
    scv_ref[...] = c3[:, L - (CONV_W - 1):L, :]
    c_act = _conv_silu(lambda jj: cext[:, 5 + jj:5 + jj + L, :], wconv_ref, bconv_ref)
    c_act = c_act.reshape(BLOCK, MLSTM_W)
    lblk = jnp.where(causal_seq, 1.0, 0.0).astype(BF16)
    lseq = jnp.where(same_seq, 1.0, 0.0).astype(BF16)
    xg, blast = _gate_columns(z[:, IN_MAIN:IN_PAD], gb_ref, None, lblk, extra=lseq)
    xgt = xg.T
    m_rep = jnp.broadcast_to(sm_ref[...], (ns, L, 128)).reshape(BLOCK, 128)
    hs = range(M_HEADS)
    qk = [_dot(c_act[:, h * 128:(h + 1) * 128].astype(BF16), wqk_ref[h]) for h in hs]
    qm = [qk[h][:, :128] for h in hs]
    qm_b = [qm[h].astype(BF16) for h in hs]
    km = [qk[h][:, 128:] * (M_HEAD_DIM ** -0.5) for h in hs]
    km_b = [km[h].astype(BF16) for h in hs]
    vh_b = [z[:, 1280 + h * 128:1280 + (h + 1) * 128].astype(BF16) for h in hs]
    c_old = [sc_ref[:, h] for h in hs]
    s_qk = [_dot_nt(qm_b[h], km_b[h]) for h in hs]
    cq = [_dot_nt(qm_b[h], c_old[h].reshape(ns * 128, 128).astype(BF16)) for h in hs]

    mask_c = si > tl
    dist_c = (WINDOW + tl - si).astype(F32)
    dist_n = (tl - sl_).astype(F32)
    pc_parts, pn_rows, inv = {}, [], {}
    for r in range(4):
        pn = []
        for c in range(2):
            h = r + 4 * c
            sc_c = s_cache[:, c * 32 + r * L:c * 32 + (r + 1) * L, :].reshape(BLOCK, 128)
            sc_n = s_new[r * 128:(r + 1) * 128, c * 128:(c + 1) * 128]
            (p_c, p_n), inv[h] = _softmax_parts(
                [(sc_c, mask_c, -ALIBI_SLOPES[h] * dist_c), (sc_n, causal_seq, -ALIBI_SLOPES[h] * dist_n)],
                sinks_ref[h])
            pc_parts[(c, r)] = p_c.reshape(ns, L, 128)
            pn.append(p_n.astype(BF16))
        pn_rows.append(jnp.concatenate(pn, axis=1))
    p8 = jnp.concatenate([pc_parts[(c, r)] for c in range(2) for r in range(4)], axis=1).astype(BF16)
    o_cache = jnp.einsum('nqk,ndk->nqd', p8, cv_ref[...].astype(BF16), preferred_element_type=F32)
    o_new = _dot(jnp.concatenate(pn_rows, axis=0), vn_mask)
    groups = []
    for r in range(4):
        oa = o_cache[:, r * L:(r + 1) * L, :].reshape(BLOCK, 128)
        ob = o_cache[:, 32 + r * L:32 + (r + 1) * L, :].reshape(BLOCK, 128)
        o = jnp.where(lo_half, oa, ob) + o_new[r * 128:(r + 1) * 128]
        groups.append(o * jnp.where(lo_half, inv[r], inv[r + 4]))
    y_att = _attn_head_norm(jnp.concatenate(groups, axis=1), gattn_ref[...])

    sk_ref[:, 0:WINDOW - L, :] = jnp.swapaxes(ck_ref[...], 1, 2)[:, L:WINDOW, :]
    sk_ref[:, WINDOW - L:WINDOW, :] = k_new.reshape(ns, L, 128)
    sv_ref[:, 0:WINDOW - L, :] = jnp.swapaxes(cv_ref[...], 1, 2)[:, L:WINDOW, :]
    sv_ref[:, WINDOW - L:WINDOW, :] = v_new.reshape(ns, L, 128)

    b_c = [xg[:, 4 + h:5 + h] for h in hs]
    li_c = [xg[:, h:h + 1] for h in hs]
    b_r = [xgt[4 + h:5 + h, :] for h in hs]
    li_r = [xgt[h:h + 1, :] for h in hs]
    bl_c = [blast[:, 4 + h:5 + h] for h in hs]
    m_prev = [m_rep[:, h:h + 1] for h in hs]
    dmat = [jnp.where(causal_seq, b_c[h] - b_r[h] + li_r[h], -jnp.inf) for h in hs]
    m_inter = [b_c[h] + m_prev[h] for h in hs]
    m_t = [jnp.maximum(m_inter[h], jnp.max(dmat[h], axis=1, keepdims=True)) for h in hs]
    w_inter = [jnp.exp(m_inter[h] - m_t[h]) for h in hs]
    sc = [s_qk[h] * jnp.exp(dmat[h] - m_t[h]) for h in hs]
    num_intra = [_dot(sc[h].astype(BF16), vh_b[h]) for h in hs]

    m_end3 = [jnp.broadcast_to(m_t[h], (BLOCK, 128)).reshape(ns, L, 128)[:, L - 1:L, :] for h in hs]
    m_end = [jnp.broadcast_to(m_end3[h], (ns, L, 128)).reshape(BLOCK, 128)[:, 0:1] for h in hs]
    decay = [jnp.exp(bl_c[h] + m_prev[h] - m_end[h]) for h in hs]
    kw = [km[h] * jnp.exp(bl_c[h] - b_c[h] + li_c[h] - m_end[h]) for h in hs]
    seq_of_row = ti >> 3
    lane3 = lax.broadcasted_iota(jnp.int32, (ns, 1, 128), 2)
    m_out = jnp.zeros((ns, 1, 128), F32)
    for h in hs:
        kw_b = kw[h].astype(BF16)
        k_big = jnp.concatenate(
            [jnp.where(seq_of_row == g, kw_b, jnp.zeros_like(kw_b)) for g in range(ns)], axis=1)
        c_up = _dot_tn(vh_b[h], k_big)
        c_up3 = jnp.stack([c_up[:, g * 128:(g + 1) * 128] for g in range(ns)], axis=0)
        decay3 = jnp.broadcast_to(decay[h], (BLOCK, 128)).reshape(ns, L, 128)[:, 0:1, :]
        n_old = sn_ref[:, h:h + 1, :]
        sco_ref[:, h] = decay3 * c_old[h] + c_up3
        sno_ref[:, h:h + 1, :] = decay3 * n_old + jnp.sum(kw[h].reshape(ns, L, 128), axis=1, keepdims=True)
        m_out = jnp.where(lane3 == h, m_end3[h], m_out)
    smo_ref[...] = m_out

    y_m = []
    for h in hs:
        num_inter = jnp.concatenate([cq[h][g * L:(g + 1) * L, g * 128:(g + 1) * 128] for g in range(ns)], axis=0)
        n_rep = jnp.broadcast_to(sn_ref[:, h:h + 1, :], (ns, L, 128)).reshape(BLOCK, 128)
        num = num_intra[h] + w_inter[h] * num_inter
        den = jnp.sum(sc[h], axis=1, keepdims=True) + w_inter[h] * jnp.sum(qm[h] * n_rep, axis=1, keepdims=True)
        hh = num / jnp.maximum(jnp.abs(den), jnp.exp(-m_t[h]))
        hm = _sigmoid(z[:, 1792 + h * 128:1792 + (h + 1) * 128]) * hh
        y_m.append(_mlstm_head_norm(hm, gml_ref[:, h * 128:(h + 1) * 128]))

    ymix_ref[...] = jnp.concatenate([y_att] + y_m, axis=1).astype(BF16)


def _const_spec(shape):
    nd = len(shape)
    return pl.BlockSpec(shape, lambda *_: (0,) * nd, pipeline_mode=pl.Buffered(1))


def _mixer_weight_specs():
    return [
        _const_spec((D_MODEL, ATT_W)),
        _const_spec((D_MODEL, IN_MAIN - ATT_W)),
        _const_spec((D_MODEL, IN_PAD - IN_MAIN)),
        _const_spec((CONV_W, MLSTM_W)),
        _const_spec((1, MLSTM_W)),
        _const_spec((M_HEADS, M_HEAD_DIM, 2 * M_HEAD_DIM)),
        _const_spec((1, 128)),
        pl.BlockSpec(memory_space=pltpu.SMEM),
        _const_spec((1, ATT_W)),
        _const_spec((1, MLSTM_W)),
    ]


def _post_weight_specs(d):
    return [
        _const_spec((d, d)),
        _const_spec((1, d)), _const_spec((1, d)),
        _const_spec((d, D_FF)), _const_spec((d, D_FF)), _const_spec((D_FF, d)),
        _const_spec((1, d)), _const_spec((1, d)),
    ]


def _prompt_layer(x_prompt, meta_blk, x_sample_rows, ymix_sample, mixer_w, post_w):
    b, s, d = x_prompt.shape
    state_shapes = [((BLOCK, KV_W), BF16)] * 4 + [
        ((8, MLSTM_W), F32), ((M_HEADS, M_HEAD_DIM, 2 * M_HEAD_DIM), F32), ((8, 128), F32)]
    rows = STEP_BLOCKS * BLOCK
    npair = s // rows
    total = b * npair
    n_pair_steps = total + 2
    n_sample_steps = x_sample_rows.shape[0] // rows
    steps = n_pair_steps + n_sample_steps

    def block_map(lag):
        def index(t):
            tb = jnp.clip(t - lag, 0, total - 1)
            return (tb // npair, tb % npair, 0)
        return index

    def smap(t):
        return (jnp.clip(t - 1, 0, total - 1) // npair, 0, 0)

    def sample_map(t):
        return (jnp.clip(t - n_pair_steps, 0, n_sample_steps - 1), 0)

    out_shape = (
        jax.ShapeDtypeStruct((b, s, d), F32),
        jax.ShapeDtypeStruct(x_sample_rows.shape, F32),
        jax.ShapeDtypeStruct((b, BLOCK, KV_W), F32),
        jax.ShapeDtypeStruct((b, BLOCK, KV_W), F32),
        jax.ShapeDtypeStruct((b, CONV_W - 1, MLSTM_W), F32),
        jax.ShapeDtypeStruct((b, M_HEADS, M_HEAD_DIM, M_HEAD_DIM), F32),
        jax.ShapeDtypeStruct((b, M_HEADS, M_HEAD_DIM), F32),
        jax.ShapeDtypeStruct((b, 8, 128), F32),
    )
    out_specs = (
        pl.BlockSpec((None, rows, d), block_map(2)),
        pl.BlockSpec((rows, d), sample_map),
        pl.BlockSpec((None, BLOCK, KV_W), smap),
        pl.BlockSpec((None, BLOCK, KV_W), smap),
        pl.BlockSpec((None, CONV_W - 1, MLSTM_W), smap),
        pl.BlockSpec((None, M_HEADS, M_HEAD_DIM, M_HEAD_DIM), lambda t: smap(t) + (0,)),
        pl.BlockSpec((None, M_HEADS, M_HEAD_DIM), smap),
        pl.BlockSpec((None, 8, 128), smap),
    )
    return pl.pallas_call(
        functools.partial(_pair_kernel, npair=npair, n_pair_steps=n_pair_steps),
        grid=(steps,),
        in_specs=[pl.BlockSpec((None, rows, d), block_map(0)), pl.BlockSpec((None, rows, d), block_map(2)),
                  _const_spec((BLOCK, d)), pl.BlockSpec((rows, d), sample_map), pl.BlockSpec((rows, d), sample_map)]
        + _mixer_weight_specs() + _post_weight_specs(d),
        out_specs=out_specs,
        out_shape=out_shape,
        scratch_shapes=[pltpu.VMEM(shape, dt) for shape, dt in state_shapes] + [
            pltpu.VMEM((rows, IN_PAD), F32),
            pltpu.VMEM((rows, IN_PAD), F32),
            pltpu.VMEM((BLOCK, 2 * KV_W), F32),
        ] + [pltpu.VMEM((BLOCK, KV_W), BF16)] * 4 + [
            pltpu.VMEM((rows + 8, MLSTM_W), F32),
            pltpu.VMEM((M_HEADS, M_HEAD_DIM, 2 * M_HEAD_DIM), F32),
            pltpu.VMEM((8, 128), F32),
            pltpu.VMEM((rows, d), BF16),
        ],
        compiler_params=pltpu.CompilerParams(
            dimension_semantics=("arbitrary",), vmem_limit_bytes=VMEM_LIMIT),
        name="prompt_layer",
    )(x_prompt, x_prompt, meta_blk, x_sample_rows, ymix_sample, *mixer_w, *post_w)


def _sample_mixer(x_sample, ck, cv, sconv8, s_c, s_n, s_m3, mixer_w):
    n, l, d = x_sample.shape
    t = SEQ_TILE
    m3 = lambda i: (i, 0, 0)
    m4 = lambda i: (i, 0, 0, 0)
    out_shape = (
        jax.ShapeDtypeStruct((n * l, d), BF16),
        jax.ShapeDtypeStruct((n, WINDOW, KV_W), F32),
        jax.ShapeDtypeStruct((n, WINDOW, KV_W), F32),
        jax.ShapeDtypeStruct((n, CONV_W - 1, MLSTM_W), F32),
        jax.ShapeDtypeStruct((n, M_HEADS, M_HEAD_DIM, M_HEAD_DIM), F32),
        jax.ShapeDtypeStruct((n, M_HEADS, M_HEAD_DIM), F32),
        jax.ShapeDtypeStruct((n, 1, 128), F32),
    )
    out_specs = (
        pl.BlockSpec((t * l, d), lambda i: (i, 0)),
        pl.BlockSpec((t, WINDOW, KV_W), m3),
        pl.BlockSpec((t, WINDOW, KV_W), m3),
        pl.BlockSpec((t, CONV_W - 1, MLSTM_W), m3),
        pl.BlockSpec((t, M_HEADS, M_HEAD_DIM, M_HEAD_DIM), m4),
        pl.BlockSpec((t, M_HEADS, M_HEAD_DIM), m3),
        pl.BlockSpec((t, 1, 128), m3),
    )
    in_specs = [
        pl.BlockSpec((t, l, d), m3),
        pl.BlockSpec((t, WINDOW, KV_W), m3),
        pl.BlockSpec((t, WINDOW, KV_W), m3),
        pl.BlockSpec((t, CONV_W - 1, MLSTM_W), m3),
        pl.BlockSpec((t, M_HEADS, M_HEAD_DIM, M_HEAD_DIM), m4),
        pl.BlockSpec((t, M_HEADS, M_HEAD_DIM), m3),
        pl.BlockSpec((t, 1, 128), m3),
    ] + _mixer_weight_specs()
    return pl.pallas_call(
        _sample_kernel,
        grid=(n // t,),
        in_specs=in_specs,
        out_specs=out_specs,
        out_shape=out_shape,
        scratch_shapes=[pltpu.VMEM((t, 16, MLSTM_W), F32)],
        compiler_params=pltpu.CompilerParams(
            dimension_semantics=("arbitrary",), vmem_limit_bytes=VMEM_LIMIT),
        name="sample_mixer",
    )(x_sample, ck, cv, sconv8, s_c, s_n, s_m3, *mixer_w)


def kernel(x_prompt, x_sample, cache_k, cache_v, state_conv, state_C, state_n, state_m, meta_tokens,
           w_in, w_conv, b_conv, w_mq, w_mk, b_i, b_f, attn_sinks, g_attn, g_mlstm, w_out,
           ln1_g, ln1_b, w_gate, w_up, w_down, ln2_g, ln2_b):
    b, s, d = x_prompt.shape
    n, l, _ = x_sample.shape

    w0 = w_in[0]
    w_in_parts = (_pair_heads(w0[:, :ATT_W], 1).astype(BF16), w0[:, ATT_W:IN_MAIN].astype(BF16),
                  jnp.pad(w0[:, IN_MAIN:], ((0, 0), (0, IN_PAD - IN_MAIN - IN_GATES))).astype(BF16))
    wqk = jnp.concatenate([w_mq[0], w_mk[0]], axis=-1).astype(BF16)
    gb = jnp.concatenate([b_i[0], b_f[0], jnp.zeros((128 - IN_GATES,), F32)]).reshape(1, 128)
    mixer_w = w_in_parts + (w_conv[0], b_conv[0].reshape(1, MLSTM_W), wqk, gb, attn_sinks[0],
                            _pair_heads(g_attn[0].reshape(1, ATT_W), 1), g_mlstm[0].reshape(1, MLSTM_W))
    w_out_p = jnp.concatenate([_pair_heads(w_out[0][:ATT_W], 0), w_out[0][ATT_W:]], axis=0).astype(BF16)
    post_w = (w_out_p, ln1_g[0].reshape(1, d), ln1_b[0].reshape(1, d),
              w_gate[0].astype(BF16), w_up[0].astype(BF16), w_down[0].astype(BF16),
              ln2_g[0].reshape(1, d), ln2_b[0].reshape(1, d))

    key_minor = lambda c: c[0].transpose(0, 2, 3, 1).reshape(n, KV_W, WINDOW)
    s_m3 = jnp.pad(state_m[0], ((0, 0), (0, 128 - M_HEADS))).reshape(n, 1, 128)
    ymix_s, sk, sv, scv, s_c, s_n, s_mo = _sample_mixer(
        x_sample, key_minor(cache_k), key_minor(cache_v), state_conv[0], state_C[0], state_n[0], s_m3, mixer_w)

    meta_blk = jnp.concatenate([jnp.zeros((META_PAD, d), x_prompt.dtype), meta_tokens.astype(x_prompt.dtype)], axis=0)
    y_prompt, y_sample, pk, pv, pconv, p_c, p_n, pm8 = _prompt_layer(
        x_prompt, meta_blk, x_sample.reshape(n * l, d), ymix_s, mixer_w, post_w)

    kv5 = lambda a: a.reshape(1, a.shape[0], WINDOW, 2, HEAD_DIM)
    kv5_key_minor = lambda a: a.reshape(a.shape[0], 2, HEAD_DIM, WINDOW).transpose(0, 3, 1, 2)[None]
    return (y_prompt, y_sample.reshape(n, l, d),
            kv5_key_minor(pk), kv5_key_minor(pv), pconv[None], p_c[None], p_n[None], pm8[None, :, :M_HEADS, 0],
            kv5(sk), kv5(sv), scv[None], s_c[None], s_n[None], s_mo[None, :, 0, :M_HEADS])
```

```python
import functools

import jax
import jax.numpy as jnp
from jax import lax
from jax.experimental import pallas as pl
from jax.experimental.pallas import tpu as pltpu

F32 = jnp.float32
BF16 = jnp.bfloat16

D_MODEL = 1024
ATT_W = 512
MLSTM_W = 512
HEAD_DIM = 64
N_HEADS = 8
KV_W = 128
WINDOW = 128
BLOCK = 128
M_HEADS = 4
M_HEAD_DIM = 128
CONV_W = 4
N_META = 16
META_PAD = BLOCK - N_META
D_FF = 2816
IN_MAIN = 2304
IN_GATES = 2 * M_HEADS
IN_PAD = IN_MAIN + 128
DEPTH = 1
ALIBI_SLOPES = tuple(2.0 ** (-8.0 * (h + 1) / N_HEADS) for h in range(N_HEADS))
DEEPNORM_ALPHA = (2.0 * DEPTH) ** 0.25
EPS = 1e-5
SEQ_TILE = 16
STEP_BLOCKS = 2
FF_CHUNK = 256
FF_LOOKAHEAD = 3
PROJ_CHUNK = 256
WEIGHT_ROWS = 128
VMEM_LIMIT = 56 * 1024 * 1024


def _pair_heads(a, axis):
    shape = a.shape
    a = a.reshape(shape[:axis] + (2, 4, HEAD_DIM) + shape[axis + 1:])
    return jnp.swapaxes(a, axis, axis + 1).reshape(shape)


def _dot(a, b):
    return jnp.dot(a, b, preferred_element_type=F32)


def _dot_nt(a, b):
    return lax.dot_general(a, b, (((1,), (1,)), ((), ())), preferred_element_type=F32)


def _dot_tn(a, b):
    return lax.dot_general(a, b, (((0,), (0,)), ((), ())), preferred_element_type=F32)


def _split2(x):
    hi = x.astype(BF16)
    lo = (x - hi.astype(F32)).astype(BF16)
    return hi, lo


def _sigmoid(x):
    return 1.0 / (1.0 + jnp.exp(-x))


def _log_sigmoid(x):
    return -(jnp.maximum(-x, 0.0) + jnp.log1p(jnp.exp(-jnp.abs(x))))


def _layer_norm(x, g, b):
    mu = jnp.mean(x, axis=-1, keepdims=True)
    xc = x - mu
    var = jnp.mean(xc * xc, axis=-1, keepdims=True)
    return xc * lax.rsqrt(var + EPS) * g + b


def _attn_head_norm(att, g):
    lo_half = lax.broadcasted_iota(jnp.int32, (att.shape[0], 128), 1) < HEAD_DIM

    def seg_mean(x):
        s_all = jnp.sum(x, axis=1, keepdims=True)
        s_lo = jnp.sum(jnp.where(lo_half, x, 0.0), axis=1, keepdims=True)
        return jnp.where(lo_half, s_lo, s_all - s_lo) * (1.0 / HEAD_DIM)

    out = []
    for grp in range(att.shape[1] // 128):
        x = att[:, grp * 128:(grp + 1) * 128]
        xc = x - seg_mean(x)
        out.append(xc * lax.rsqrt(seg_mean(xc * xc) + EPS))
    return jnp.concatenate(out, axis=1) * g


def _mlstm_head_norm(hm, g):
    mu = jnp.mean(hm, axis=-1, keepdims=True)
    xc = hm - mu
    var = jnp.mean(xc * xc, axis=-1, keepdims=True)
    return xc * lax.rsqrt(var + EPS) * g


def _softmax_parts(parts, sink):
    sp = [jnp.where(m, s * (HEAD_DIM ** -0.5) + a, -jnp.inf) for s, m, a in parts]
    mx = sink
    for s in sp:
        mx = jnp.maximum(mx, jnp.max(s, axis=1, keepdims=True))
    ps = [jnp.exp(s - mx) for s in sp]
    den = jnp.exp(sink - mx)
    for p in ps:
        den = den + jnp.sum(p, axis=1, keepdims=True)
    return ps, 1.0 / den


def _conv_silu(window, wconv_ref, bconv_ref):
    acc = bconv_ref[...]
    for j in range(CONV_W):
        acc = acc + window(j) * wconv_ref[j:j + 1, :]
    return acc * _sigmoid(acc)


def _gate_columns(gates, gb_ref, row_valid, ltri, extra=None):
    lane = lax.broadcasted_iota(jnp.int32, gates.shape, 1)
    gb = gates + gb_ref[...]
    is_i = lane < M_HEADS
    is_f = (lane >= M_HEADS) & (lane < 2 * M_HEADS)
    logf = _log_sigmoid(gb)
    if row_valid is not None:
        lf = jnp.where(is_f & row_valid, logf, 0.0)
        li = jnp.where(row_valid, gb, -jnp.inf)
    else:
        lf = jnp.where(is_f, logf, 0.0)
        li = gb
    hi_lo = jnp.concatenate(_split2(lf), axis=1)
    bc = _dot(ltri, hi_lo)
    x = jnp.where(is_i, li, bc[:, :128] + bc[:, 128:])
    if extra is None:
        return x, None
    be = _dot(extra, hi_lo)
    return x, be[:, :128] + be[:, 128:]


class _FfnStages:
    N_CHUNKS = D_FF // FF_CHUNK

    def __init__(self, x, ymix, wout_ref, ln1g_ref, ln1b_ref, wg_ref, wu_ref, wd_ref, ln2g_ref, ln2b_ref):
        self.x, self.ymix = x, ymix
        self.wout_ref, self.ln1g_ref, self.ln1b_ref = wout_ref, ln1g_ref, ln1b_ref
        self.wg_ref, self.wu_ref, self.wd_ref = wg_ref, wu_ref, wd_ref
        self.ln2g_ref, self.ln2b_ref = ln2g_ref, ln2b_ref

    def head(self):
        self.x1 = _layer_norm(DEEPNORM_ALPHA * self.x + _dot(self.ymix, self.wout_ref[...]),
                              self.ln1g_ref[...], self.ln1b_ref[...])
        self.x1b = self.x1.astype(BF16)
        self.acc = DEEPNORM_ALPHA * self.x1

    def _gate_up(self, c):
        cs = slice(c * FF_CHUNK, (c + 1) * FF_CHUNK)
        return _dot(self.x1b, self.wg_ref[:, cs]), _dot(self.x1b, self.wu_ref[:, cs])

    def chunk(self, c):
        if c == 0:
            self.gu = {}
        for n in range(c, min(c + FF_LOOKAHEAD, self.N_CHUNKS - 1) + 1):
            if n not in self.gu:
                self.gu[n] = self._gate_up(n)
        g, u = self.gu.pop(c)
        hid = (g * _sigmoid(g) * u).astype(BF16)
        self.acc = self.acc + _dot(hid, self.wd_ref[c * FF_CHUNK:(c + 1) * FF_CHUNK, :])

    def tail(self):
        return _layer_norm(self.acc, self.ln2g_ref[...], self.ln2b_ref[...])

    def run(self):
        self.head()
        for c in range(self.N_CHUNKS):
            self.chunk(c)
        return self.tail()


def _mix_blocks(zc, nb, pad_rows, thr_first, state, cbuf, wconv_ref, bconv_ref, wqk_ref, gb_ref,
                sinks_ref, gattn_ref, gml_ref, hook):
    rows = nb * BLOCK
    kp_lo, kp_hi, vp_lo, vp_hi, c_state, m_state = state
    c_state, m_state = list(c_state), list(m_state)
    lane = lax.broadcasted_iota(jnp.int32, (BLOCK, 128), 1)
    lo_half = lane < HEAD_DIM
    row_valid = None
    if pad_rows:
        row_valid = lax.broadcasted_iota(jnp.int32, (rows, 1), 0) >= pad_rows

    s_all, vmasks = [], []
    for blk in range(nb):
        rs = slice(blk * BLOCK, (blk + 1) * BLOCK)
        kb = zc[rs, ATT_W:ATT_W + KV_W].astype(BF16)
        vb = zc[rs, ATT_W + KV_W:ATT_W + 2 * KV_W].astype(BF16)
        zero_b = jnp.zeros_like(kb)
        k_lo, k_hi = jnp.where(lo_half, kb, zero_b), jnp.where(lo_half, zero_b, kb)
        v_lo, v_hi = jnp.where(lo_half, vb, zero_b), jnp.where(lo_half, zero_b, vb)
        kmask = jnp.concatenate([kp_lo, k_lo, kp_hi, k_hi], axis=0)
        vmasks.append(jnp.concatenate([vp_lo, v_lo, vp_hi, v_hi], axis=0))
        q4 = jnp.concatenate([zc[rs, r * 128:(r + 1) * 128] for r in range(4)], axis=0).astype(BF16)
        s_all.append(_dot_nt(q4, kmask))
        kp_lo, kp_hi, vp_lo, vp_hi = k_lo, k_hi, v_lo, v_hi
    hook("scores")

    c_in = zc[:, 768:1280]
    if row_valid is not None:
        c_in = jnp.where(row_valid, c_in, 0.0)
    cbuf[8:8 + rows, :] = c_in
    c_act = _conv_silu(lambda jj: cbuf[5 + jj:5 + jj + rows, :], wconv_ref, bconv_ref)
    cbuf[0:8, :] = cbuf[rows:rows + 8, :]
    hs = range(M_HEADS)
    ca_b = [c_act[:, h * 128:(h + 1) * 128].astype(BF16) for h in hs]
    qk = [_dot(ca_b[h], wqk_ref[h]) for h in hs]
    qm_b = [qk[h][:, :M_HEAD_DIM].astype(BF16) for h in hs]
    kmt_f = [(qk[h][:, M_HEAD_DIM:] * (M_HEAD_DIM ** -0.5)).T for h in hs]
    hook("front")

    ti = lax.broadcasted_iota(jnp.int32, (rows, rows), 0)
    si = lax.broadcasted_iota(jnp.int32, (rows, rows), 1)
    ltri = jnp.where((si <= ti) & ((si >> 7) == (ti >> 7)), 1.0, 0.0).astype(BF16)
    xg, _ = _gate_columns(zc[:, IN_MAIN:IN_PAD], gb_ref, row_valid, ltri)
    xgt = xg.T
    hook("gates")

    qi = lax.broadcasted_iota(jnp.int32, (BLOCK, BLOCK), 0)
    ks = lax.broadcasted_iota(jnp.int32, (BLOCK, BLOCK), 1)
    own = ks <= qi
    distf = jnp.where(own, qi - ks, WINDOW + qi - ks).astype(F32)
    zero_p = jnp.zeros((BLOCK, BLOCK), BF16)
    att_rows = []
    for blk in range(nb):
        live = None
        if blk == 0 and thr_first is not None:
            live = jnp.where(own, ks + BLOCK, ks) >= thr_first
        p_rows, inv = [], {}
        for r in range(4):
            pr = []
            for c in range(2):
                h = r + 4 * c
                s_blk = s_all[blk][r * 128:(r + 1) * 128, c * 256:(c + 1) * 256]
                sp = jnp.where(own, s_blk[:, BLOCK:], s_blk[:, :BLOCK]) * (HEAD_DIM ** -0.5) \
                    - ALIBI_SLOPES[h] * distf
                if live is not None:
                    sp = jnp.where(live, sp, -jnp.inf)
                sink = sinks_ref[h]
                mx = jnp.maximum(jnp.max(sp, axis=1, keepdims=True), sink)
                p = jnp.exp(sp - mx)
                inv[h] = 1.0 / (jnp.sum(p, axis=1, keepdims=True) + jnp.exp(sink - mx))
                p_b = p.astype(BF16)
                pr += [jnp.where(own, zero_p, p_b), jnp.where(own, p_b, zero_p)]
            p_rows.append(jnp.concatenate(pr, axis=1))
            hook("softmax", blk * 4 + r)
        o_all = _dot(jnp.concatenate(p_rows, axis=0), vmasks[blk])
        att_rows.append(jnp.concatenate(
            [o_all[r * 128:(r + 1) * 128] * jnp.where(lo_half, inv[r], inv[r + 4]) for r in range(4)], axis=1))
        hook("pv", blk)

    ones_col = jnp.where(lane == 0, 1.0, 0.0).astype(BF16)
    tb = lax.broadcasted_iota(jnp.int32, (BLOCK, BLOCK), 0)
    sb = lax.broadcasted_iota(jnp.int32, (BLOCK, BLOCK), 1)
    tri = sb <= tb
    hh_rows = [[] for _ in hs]
    for blk in range(nb):
        rs = slice(blk * BLOCK, (blk + 1) * BLOCK)
        b_c = [xg[rs, 4 + h:5 + h] for h in hs]
        b_r = [xgt[4 + h:5 + h, rs] for h in hs]
        li_r = [xgt[h:h + 1, rs] for h in hs]
        kmt = [kmt_f[h][:, rs] for h in hs]
        v_ext = [jnp.concatenate([zc[rs, 1280 + h * 128:1280 + (h + 1) * 128].astype(BF16), ones_col], axis=1)
                 for h in hs]
        s_qk = [_dot(qm_b[h][rs], kmt[h].astype(BF16)) for h in hs]
        inter = [_dot(qm_b[h][rs], c_state[h].astype(BF16)) for h in hs]
        hook("mlstm_a", blk)
        dmat = [jnp.where(tri, b_c[h] - b_r[h] + li_r[h], -jnp.inf) for h in hs]
        m_inter = [b_c[h] + m_state[h] for h in hs]
        m_t = [jnp.maximum(m_inter[h], jnp.max(dmat[h], axis=1, keepdims=True)) for h in hs]
        w_inter = [jnp.exp(m_inter[h] - m_t[h]) for h in hs]
        sc_b = [(s_qk[h] * jnp.exp(dmat[h] - m_t[h])).astype(BF16) for h in hs]
        hook("mlstm_b", blk)
        nd = [_dot(sc_b[h], v_ext[h]) + w_inter[h] * inter[h] for h in hs]
        for h in hs:
            m_end = m_t[h][BLOCK - 1:BLOCK, :]
            b_last = b_c[h][BLOCK - 1:BLOCK, :]
            decay = jnp.exp(b_last + m_state[h] - m_end)
            wk_r = jnp.exp(b_last - b_r[h] + li_r[h] - m_end)
            c_state[h] = decay * c_state[h] + _dot((kmt[h] * wk_r).astype(BF16), v_ext[h])
            m_state[h] = m_end
            hh_rows[h].append(nd[h][:, :128] / jnp.maximum(jnp.abs(nd[h][:, 128:129]), jnp.exp(-m_t[h])))
        hook("mlstm_c", blk)

    y_att = _attn_head_norm(jnp.concatenate(att_rows, axis=0), gattn_ref[...])
    hook("attnorm")
    y_m = []
    for h in hs:
        hm = _sigmoid(zc[:, 1792 + h * 128:1792 + (h + 1) * 128]) * jnp.concatenate(hh_rows[h], axis=0)
        y_m.append(_mlstm_head_norm(hm, gml_ref[:, h * 128:(h + 1) * 128]))
        hook("headnorm", h)
    ymix = jnp.concatenate([y_att] + y_m, axis=1).astype(BF16)
    return ymix, (kp_lo, kp_hi, vp_lo, vp_hi, c_state, m_state)


def _project(xb, w_in_refs, lo=0, hi=IN_PAD):
    bounds = (0, ATT_W, IN_MAIN, IN_PAD)
    parts = []
    for ref, start, stop in zip(w_in_refs, bounds[:-1], bounds[1:]):
        a, b = max(lo, start), min(hi, stop)
        if a < b:
            parts.append(_dot(xb, ref[:, a - start:b - start]))
    return parts[0] if len(parts) == 1 else jnp.concatenate(parts, axis=1)


def _meta_state(meta_ref, w_inq_ref, w_inr_ref, w_ing_ref, wconv_ref, bconv_ref, wqk_ref, gb_ref,
                sinks_ref, gattn_ref,
                gml_ref, kplo_o, kphi_o, vplo_o, vphi_o, cb_o, ctn_o, mst_o, zc, cbuf):
    zc[...] = _project(meta_ref[...].astype(BF16), (w_inq_ref, w_inr_ref, w_ing_ref))
    cbuf[0:8, :] = jnp.zeros((8, MLSTM_W), F32)
    zb = jnp.zeros((BLOCK, KV_W), BF16)
    state = (zb, zb, zb, zb, [jnp.zeros((M_HEAD_DIM, 2 * M_HEAD_DIM), F32)] * M_HEADS,
             [jnp.zeros((1, 1), F32)] * M_HEADS)
    _, (kp_lo, kp_hi, vp_lo, vp_hi, c_state, m_state) = _mix_blocks(
        zc, 1, META_PAD, BLOCK + META_PAD, state, cbuf, wconv_ref, bconv_ref, wqk_ref, gb_ref,
        sinks_ref, gattn_ref, gml_ref, lambda name, index=0: None)
    kplo_o[...] = kp_lo
    kphi_o[...] = kp_hi
    vplo_o[...] = vp_lo
    vphi_o[...] = vp_hi
    cb_o[...] = cbuf[0:8, :]
    for h in range(M_HEADS):
        ctn_o[h] = c_state[h]
        mst_o[h:h + 1, :] = jnp.broadcast_to(m_state[h], (1, 128))
    mst_o[M_HEADS:8, :] = jnp.zeros((8 - M_HEADS, 128), F32)


def _load_ffn_weights(wout_hbm, wg_hbm, wu_hbm, wd_hbm, wout_v, wg_v, wu_v, wd_v, stage, sems):
    jobs = []
    for src, dst, nrows in ((wg_hbm, wg_v, D_MODEL), (wu_hbm, wu_v, D_MODEL), (wd_hbm, wd_v, D_FF)):
        jobs += [(src, r, WEIGHT_ROWS, dst, r) for r in range(0, nrows, WEIGHT_ROWS)]
    for h in range(N_HEADS):
        jobs.append((wout_hbm, h * HEAD_DIM, HEAD_DIM, wout_v, ((h % 4) * 2 + h // 4) * HEAD_DIM))
    jobs += [(wout_hbm, r, WEIGHT_ROWS, wout_v, r) for r in range(ATT_W, D_MODEL, WEIGHT_ROWS)]

    def copy(i):
        src, r0, nrows, _, _ = jobs[i]
        cols = src.shape[1]
        return pltpu.make_async_copy(src.at[pl.ds(r0, nrows), :],
                                     stage.at[i % 2, pl.ds(0, nrows), pl.ds(0, cols)], sems.at[i % 2])

    copy(0).start()
    for i, (src, _, nrows, dst, d0) in enumerate(jobs):
        if i + 1 < len(jobs):
            copy(i + 1).start()
        copy(i).wait()
        dst[d0:d0 + nrows, :] = stage[i % 2, 0:nrows, 0:src.shape[1]].astype(BF16)


def _pair_kernel(xin_ref, xres_ref, meta_ref, xs_ref, ymix_s_ref,
                 w_inq_ref, w_inr_ref, w_ing_ref, wconv_ref, bconv_ref, wqk_ref, gb_ref, sinks_ref,
                 gattn_ref, gml_ref,
                 wout_hbm, ln1g_ref, ln1b_ref, wg_hbm, wu_hbm, wd_hbm, ln2g_ref, ln2b_ref,
                 y_ref, ys_ref, pk_ref, pv_ref, pconv_ref, pc_ref, pn_ref, pm_ref,
                 wout_v, wg_v, wu_v, wd_v, wstage, wsems,
                 kplo_i, kphi_i, vplo_i, vphi_i, cb_i, ctn_i, mst_i,
                 zcur, znext, kvlast, kplo, kphi, vplo, vphi, cbuf, ctn, mst, yprev, *, npair, n_pair_steps):
    t = pl.program_id(0)
    p = lax.rem(t + (npair - 1), npair)

    @pl.when(t == 0)
    def _init_pipeline():
        _load_ffn_weights(wout_hbm, wg_hbm, wu_hbm, wd_hbm, wout_v, wg_v, wu_v, wd_v, wstage, wsems)
        zcur[...] = jnp.zeros_like(zcur)
        yprev[...] = jnp.zeros_like(yprev)
        _meta_state(meta_ref, w_inq_ref, w_inr_ref, w_ing_ref, wconv_ref, bconv_ref, wqk_ref, gb_ref,
                    sinks_ref, gattn_ref, gml_ref, kplo_i, kphi_i, vplo_i, vphi_i, cb_i, ctn_i, mst_i,
                    znext.at[pl.ds(0, BLOCK)], cbuf)

    @pl.when((p == 0) | (t == 0))
    def _load_meta_state():
        kplo[...] = kplo_i[...]
        kphi[...] = kphi_i[...]
        vplo[...] = vplo_i[...]
        vphi[...] = vphi_i[...]
        cbuf[0:8, :] = cb_i[...]
        ctn[...] = ctn_i[...]
        mst[...] = mst_i[...]

    ffn_w = (wout_v, ln1g_ref, ln1b_ref, wg_v, wu_v, wd_v, ln2g_ref, ln2b_ref)

    @pl.when(t < n_pair_steps)
    def _pair_step():
        ffn = _FfnStages(xres_ref[...], yprev[...], *ffn_w)
        xb_b = xin_ref[...].astype(BF16)

        n_proj = -(-IN_PAD // PROJ_CHUNK)

        def proj(n):
            lo, hi = n * PROJ_CHUNK, min((n + 1) * PROJ_CHUNK, IN_PAD)
            znext[:, lo:hi] = _project(xb_b, (w_inq_ref, w_inr_ref, w_ing_ref), lo, hi)

        proj_it, ffn_it = iter(range(n_proj)), iter(range(ffn.N_CHUNKS))

        def fill_proj(n):
            for c in [c for _, c in zip(range(n), proj_it)]:
                proj(c)

        def fill_ffn(n):
            for c in [c for _, c in zip(range(n), ffn_it)]:
                ffn.chunk(c)

        out = {}

        def hook(name, index=0):
            if name == "scores":
                fill_proj(2)
                ffn.head()
            elif name == "front":
                fill_proj(1)
            elif name == "gates":
                fill_ffn(1)
            elif name == "softmax":
                if index % 2 == 1:
                    fill_ffn(1)
            elif name in ("pv", "mlstm_a", "mlstm_b"):
                fill_ffn(1)
            elif name == "mlstm_c":
                fill_proj(1)
            elif name == "attnorm":
                fill_ffn(ffn.N_CHUNKS)
                out["y"] = ffn.tail()
                fill_proj(1)
            elif name == "headnorm":
                fill_proj(1)

        state = (kplo[...], kphi[...], vplo[...], vphi[...], [ctn[h] for h in range(M_HEADS)],
                 [mst[h:h + 1, 0:1] for h in range(M_HEADS)])
        thr_first = jnp.where(p == 0, META_PAD, 0)
        kvlast[...] = zcur[(STEP_BLOCKS - 1) * BLOCK:STEP_BLOCKS * BLOCK, ATT_W:ATT_W + 2 * KV_W]
        ymix, (kp_lo, kp_hi, vp_lo, vp_hi, c_state, m_state) = _mix_blocks(
            zcur, STEP_BLOCKS, 0, thr_first, state, cbuf, wconv_ref, bconv_ref, wqk_ref, gb_ref,
            sinks_ref, gattn_ref, gml_ref, hook)
        fill_proj(n_proj)
        y_ref[...] = out["y"]
        kplo[...] = kp_lo
        kphi[...] = kp_hi
        vplo[...] = vp_lo
        vphi[...] = vp_hi
        for h in range(M_HEADS):
            ctn[h] = c_state[h]
            mst[h:h + 1, :] = jnp.broadcast_to(m_state[h], (1, 128))
        yprev[...] = ymix
        zcur[...] = znext[...]

    @pl.when(t >= n_pair_steps)
    def _sample_ffn_step():
        ys_ref[...] = _FfnStages(xs_ref[...], ymix_s_ref[...], *ffn_w).run()

    @pl.when((p == npair - 1) & (t < n_pair_steps))
    def _final():
        pk_ref[...] = kvlast[:, 0:KV_W].T
        pv_ref[...] = kvlast[:, KV_W:2 * KV_W].T
        pconv_ref[...] = cbuf[8 - (CONV_W - 1):8, :]
        pm_ref[...] = mst[...]
        for h in range(M_HEADS):
            c_n = ctn[h].T
            pc_ref[h] = c_n[0:M_HEAD_DIM, :]
            pn_ref[h:h + 1, :] = c_n[M_HEAD_DIM:M_HEAD_DIM + 1, :]


def _sample_kernel(xs_ref, ck_ref, cv_ref, sconv_ref, sc_ref, sn_ref, sm_ref,
                   w_inq_ref, w_inr_ref, w_ing_ref, wconv_ref, bconv_ref, wqk_ref, gb_ref, sinks_ref,
                   gattn_ref, gml_ref,
                   ymix_ref, sk_ref, sv_ref, scv_ref, sco_ref, sno_ref, smo_ref, cext):
    ns, L = SEQ_TILE, 8
    xs = xs_ref[...].reshape(ns * L, D_MODEL)
    z = _project(xs.astype(BF16), (w_inq_ref, w_inr_ref, w_ing_ref))

    lane = lax.broadcasted_iota(jnp.int32, (BLOCK, 128), 1)
    lo_half = lane < HEAD_DIM
    lo3 = lax.broadcasted_iota(jnp.int32, (ns, L, 128), 2) < HEAD_DIM
    ti = lax.broadcasted_iota(jnp.int32, (BLOCK, BLOCK), 0)
    si = lax.broadcasted_iota(jnp.int32, (BLOCK, BLOCK), 1)
    same_seq = (ti >> 3) == (si >> 3)
    tl, sl_ = ti & 7, si & 7
    causal_seq = same_seq & (sl_ <= tl)

    k_new = z[:, ATT_W:ATT_W + KV_W]
    v_new = z[:, ATT_W + KV_W:ATT_W + 2 * KV_W]
    kb, vb = k_new.astype(BF16), v_new.astype(BF16)
    zero_b = jnp.zeros_like(kb)
    kn_mask = jnp.concatenate([jnp.where(lo_half, kb, zero_b), jnp.where(lo_half, zero_b, kb)], axis=0)
    vn_mask = jnp.concatenate([jnp.where(lo_half, vb, zero_b), jnp.where(lo_half, zero_b, vb)], axis=0)
    qg = [z[:, r * 128:(r + 1) * 128] for r in range(4)]
    s_new = _dot_nt(jnp.concatenate(qg, axis=0).astype(BF16), kn_mask)
    qg3 = [q.reshape(ns, L, 128) for q in qg]
    q8 = jnp.concatenate([jnp.where(lo3, q, 0.0) for q in qg3] + [jnp.where(lo3, 0.0, q) for q in qg3],
                         axis=1).astype(BF16)
    s_cache = jnp.einsum('nqd,ndk->nqk', q8, ck_ref[...].astype(BF16), preferred_element_type=F32)

    c3 = z[:, 768:1280].reshape(ns, L, MLSTM_W)
    cext[:, 8 - (CONV_W - 1):8, :] = sconv_ref[...]
    cext[:, 8:16, :] = c3
    scv_ref[...] = c3[:, L - (CONV_W - 1):L, :]
    c_act = _conv_silu(lambda jj: cext[:, 5 + jj:5 + jj + L, :], wconv_ref, bconv_ref)
    c_act = c_act.reshape(BLOCK, MLSTM_W)
    lblk = jnp.where(causal_seq, 1.0, 0.0).astype(BF16)
    lseq = jnp.where(same_seq, 1.0, 0.0).astype(BF16)
    xg, blast = _gate_columns(z[:, IN_MAIN:IN_PAD], gb_ref, None, lblk, extra=lseq)
    xgt = xg.T
    m_rep = jnp.broadcast_to(sm_ref[...], (ns, L, 128)).reshape(BLOCK, 128)
    hs = range(M_HEADS)
    qk = [_dot(c_act[:, h * 128:(h + 1) * 128].astype(BF16), wqk_ref[h]) for h in hs]
    qm = [qk[h][:, :128] for h in hs]
    qm_b = [qm[h].astype(BF16) for h in hs]
    km = [qk[h][:, 128:] * (M_HEAD_DIM ** -0.5) for h in hs]
    km_b = [km[h].astype(BF16) for h in hs]
    vh_b = [z[:, 1280 + h * 128:1280 + (h + 1) * 128].astype(BF16) for h in hs]
    c_old = [sc_ref[:, h] for h in hs]
    s_qk = [_dot_nt(qm_b[h], km_b[h]) for h in hs]
    cq = [_dot_nt(qm_b[h], c_old[h].reshape(ns * 128, 128).astype(BF16)) for h in hs]

    mask_c = si > tl
    dist_c = (WINDOW + tl - si).astype(F32)
    dist_n = (tl - sl_).astype(F32)
    pc_parts, pn_rows, inv = {}, [], {}
    for r in range(4):
        pn = []
        for c in range(2):
            h = r + 4 * c
            sc_c = s_cache[:, c * 32 + r * L:c * 32 + (r + 1) * L, :].reshape(BLOCK, 128)
            sc_n = s_new[r * 128:(r + 1) * 128, c * 128:(c + 1) * 128]
            (p_c, p_n), inv[h] = _softmax_parts(
                [(sc_c, mask_c, -ALIBI_SLOPES[h] * dist_c), (sc_n, causal_seq, -ALIBI_SLOPES[h] * dist_n)],
                sinks_ref[h])
            pc_parts[(c, r)] = p_c.reshape(ns, L, 128)
            pn.append(p_n.astype(BF16))
        pn_rows.append(jnp.concatenate(pn, axis=1))
    p8 = jnp.concatenate([pc_parts[(c, r)] for c in range(2) for r in range(4)], axis=1).astype(BF16)
    o_cache = jnp.einsum('nqk,ndk->nqd', p8, cv_ref[...].astype(BF16), preferred_element_type=F32)
    o_new = _dot(jnp.concatenate(pn_rows, axis=0), vn_mask)
    groups = []
    for r in range(4):
        oa = o_cache[:, r * L:(r + 1) * L, :].reshape(BLOCK, 128)
        ob = o_cache[:, 32 + r * L:32 + (r + 1) * L, :].reshape(BLOCK, 128)
        o = jnp.where(lo_half, oa, ob) + o_new[r * 128:(r + 1) * 128]
        groups.append(o * jnp.where(lo_half, inv[r], inv[r + 4]))
    y_att = _attn_head_norm(jnp.concatenate(groups, axis=1), gattn_ref[...])

    sk_ref[:, 0:WINDOW - L, :] = jnp.swapaxes(ck_ref[...], 1, 2)[:, L:WINDOW, :]
    sk_ref[:, WINDOW - L:WINDOW, :] = k_new.reshape(ns, L, 128)
    sv_ref[:, 0:WINDOW - L, :] = jnp.swapaxes(cv_ref[...], 1, 2)[:, L:WINDOW, :]
    sv_ref[:, WINDOW - L:WINDOW, :] = v_new.reshape(ns, L, 128)

    b_c = [xg[:, 4 + h:5 + h] for h in hs]
    li_c = [xg[:, h:h + 1] for h in hs]
    b_r = [xgt[4 + h:5 + h, :] for h in hs]
    li_r = [xgt[h:h + 1, :] for h in hs]
    bl_c = [blast[:, 4 + h:5 + h] for h in hs]
    m_prev = [m_rep[:, h:h + 1] for h in hs]
    dmat = [jnp.where(causal_seq, b_c[h] - b_r[h] + li_r[h], -jnp.inf) for h in hs]
    m_inter = [b_c[h] + m_prev[h] for h in hs]
    m_t = [jnp.maximum(m_inter[h], jnp.max(dmat[h], axis=1, keepdims=True)) for h in hs]
    w_inter = [jnp.exp(m_inter[h] - m_t[h]) for h in hs]
    sc = [s_qk[h] * jnp.exp(dmat[h] - m_t[h]) for h in hs]
    num_intra = [_dot(sc[h].astype(BF16), vh_b[h]) for h in hs]

    m_end3 = [jnp.broadcast_to(m_t[h], (BLOCK, 128)).reshape(ns, L, 128)[:, L - 1:L, :] for h in hs]
    m_end = [jnp.broadcast_to(m_end3[h], (ns, L, 128)).reshape(BLOCK, 128)[:, 0:1] for h in hs]
    decay = [jnp.exp(bl_c[h] + m_prev[h] - m_end[h]) for h in hs]
    kw = [km[h] * jnp.exp(bl_c[h] - b_c[h] + li_c[h] - m_end[h]) for h in hs]
    seq_of_row = ti >> 3
    lane3 = lax.broadcasted_iota(jnp.int32, (ns, 1, 128), 2)
    m_out = jnp.zeros((ns, 1, 128), F32)
    for h in hs:
        kw_b = kw[h].astype(BF16)
        k_big = jnp.concatenate(
            [jnp.where(seq_of_row == g, kw_b, jnp.zeros_like(kw_b)) for g in range(ns)], axis=1)
        c_up = _dot_tn(vh_b[h], k_big)
        c_up3 = jnp.stack([c_up[:, g * 128:(g + 1) * 128] for g in range(ns)], axis=0)
        decay3 = jnp.broadcast_to(decay[h], (BLOCK, 128)).reshape(ns, L, 128)[:, 0:1, :]
        n_old = sn_ref[:, h:h + 1, :]
        sco_ref[:, h] = decay3 * c_old[h] + c_up3
        sno_ref[:, h:h + 1, :] = decay3 * n_old + jnp.sum(kw[h].reshape(ns, L, 128), axis=1, keepdims=True)
        m_out = jnp.where(lane3 == h, m_end3[h], m_out)
    smo_ref[...] = m_out

    y_m = []
    for h in hs:
        num_inter = jnp.concatenate([cq[h][g * L:(g + 1) * L, g * 128:(g + 1) * 128] for g in range(ns)], axis=0)
        n_rep = jnp.broadcast_to(sn_ref[:, h:h + 1, :], (ns, L, 128)).reshape(BLOCK, 128)
        num = num_intra[h] + w_inter[h] * num_inter
        den = jnp.sum(sc[h], axis=1, keepdims=True) + w_inter[h] * jnp.sum(qm[h] * n_rep, axis=1, keepdims=True)
        hh = num / jnp.maximum(jnp.abs(den), jnp.exp(-m_t[h]))
        hm = _sigmoid(z[:, 1792 + h * 128:1792 + (h + 1) * 128]) * hh
        y_m.append(_mlstm_head_norm(hm, gml_ref[:, h * 128:(h + 1) * 128]))

    ymix_ref[...] = jnp.concatenate([y_att] + y_m, axis=1).astype(BF16)


def _const_spec(shape):
    nd = len(shape)
    return pl.BlockSpec(shape, lambda *_: (0,) * nd, pipeline_mode=pl.Buffered(1))


def _mixer_weight_specs():
    return [
        _const_spec((D_MODEL, ATT_W)),
        _const_spec((D_MODEL, IN_MAIN - ATT_W)),
        _const_spec((D_MODEL, IN_PAD - IN_MAIN)),
        _const_spec((CONV_W, MLSTM_W)),
        _const_spec((1, MLSTM_W)),
        _const_spec((M_HEADS, M_HEAD_DIM, 2 * M_HEAD_DIM)),
        _const_spec((1, 128)),
        pl.BlockSpec(memory_space=pltpu.SMEM),
        _const_spec((1, ATT_W)),
        _const_spec((1, MLSTM_W)),
    ]


def _post_weight_specs(d):
    hbm = pl.BlockSpec(memory_space=pl.ANY)
    return [hbm, _const_spec((1, d)), _const_spec((1, d)), hbm, hbm, hbm, _const_spec((1, d)), _const_spec((1, d))]


def _prompt_layer(x_prompt, meta_blk, x_sample_rows, ymix_sample, mixer_w, post_w):
    b, s, d = x_prompt.shape
    state_shapes = [((BLOCK, KV_W), BF16)] * 4 + [
        ((8, MLSTM_W), F32), ((M_HEADS, M_HEAD_DIM, 2 * M_HEAD_DIM), F32), ((8, 128), F32)]
    rows = STEP_BLOCKS * BLOCK
    npair = s // rows
    total = b * npair
    n_pair_steps = total + 2
    n_sample_steps = x_sample_rows.shape[0] // rows
    steps = n_pair_steps + n_sample_steps

    def block_map(lag):
        def index(t):
            tb = jnp.clip(t - lag, 0, total - 1)
            return (tb // npair, tb % npair, 0)
        return index

    def smap(t):
        return (jnp.clip(t - 1, 0, total - 1) // npair, 0, 0)

    def sample_map(t):
        return (jnp.clip(t - n_pair_steps, 0, n_sample_steps - 1), 0)

    out_shape = (
        jax.ShapeDtypeStruct((b, s, d), F32),
        jax.ShapeDtypeStruct(x_sample_rows.shape, F32),
        jax.ShapeDtypeStruct((b, BLOCK, KV_W), F32),
        jax.ShapeDtypeStruct((b, BLOCK, KV_W), F32),
        jax.ShapeDtypeStruct((b, CONV_W - 1, MLSTM_W), F32),
        jax.ShapeDtypeStruct((b, M_HEADS, M_HEAD_DIM, M_HEAD_DIM), F32),
        jax.ShapeDtypeStruct((b, M_HEADS, M_HEAD_DIM), F32),
        jax.ShapeDtypeStruct((b, 8, 128), F32),
    )
    out_specs = (
        pl.BlockSpec((None, rows, d), block_map(2)),
        pl.BlockSpec((rows, d), sample_map),
        pl.BlockSpec((None, BLOCK, KV_W), smap),
        pl.BlockSpec((None, BLOCK, KV_W), smap),
        pl.BlockSpec((None, CONV_W - 1, MLSTM_W), smap),
        pl.BlockSpec((None, M_HEADS, M_HEAD_DIM, M_HEAD_DIM), lambda t: smap(t) + (0,)),
        pl.BlockSpec((None, M_HEADS, M_HEAD_DIM), smap),
        pl.BlockSpec((None, 8, 128), smap),
    )
    return pl.pallas_call(
        functools.partial(_pair_kernel, npair=npair, n_pair_steps=n_pair_steps),
        grid=(steps,),
        in_specs=[pl.BlockSpec((None, rows, d), block_map(0)), pl.BlockSpec((None, rows, d), block_map(2)),
                  _const_spec((BLOCK, d)), pl.BlockSpec((rows, d), sample_map), pl.BlockSpec((rows, d), sample_map)]
        + _mixer_weight_specs() + _post_weight_specs(d),
        out_specs=out_specs,
        out_shape=out_shape,
        scratch_shapes=[
            pltpu.VMEM((d, d), BF16), pltpu.VMEM((d, D_FF), BF16), pltpu.VMEM((d, D_FF), BF16),
            pltpu.VMEM((D_FF, d), BF16),
            pltpu.VMEM((2, WEIGHT_ROWS, D_FF), F32), pltpu.SemaphoreType.DMA((2,)),
        ] + [pltpu.VMEM(shape, dt) for shape, dt in state_shapes] + [
            pltpu.VMEM((rows, IN_PAD), F32),
            pltpu.VMEM((rows, IN_PAD), F32),
            pltpu.VMEM((BLOCK, 2 * KV_W), F32),
        ] + [pltpu.VMEM((BLOCK, KV_W), BF16)] * 4 + [
            pltpu.VMEM((rows + 8, MLSTM_W), F32),
            pltpu.VMEM((M_HEADS, M_HEAD_DIM, 2 * M_HEAD_DIM), F32),
            pltpu.VMEM((8, 128), F32),
            pltpu.VMEM((rows, d), BF16),
        ],
        compiler_params=pltpu.CompilerParams(
            dimension_semantics=("arbitrary",), vmem_limit_bytes=VMEM_LIMIT),
        name="prompt_layer",
    )(x_prompt, x_prompt, meta_blk, x_sample_rows, ymix_sample, *mixer_w, *post_w)


def _sample_mixer(x_sample, ck, cv, sconv8, s_c, s_n, s_m3, mixer_w):
    n, l, d = x_sample.shape
    t = SEQ_TILE
    m3 = lambda i: (i, 0, 0)
    m4 = lambda i: (i, 0, 0, 0)
    out_shape = (
        jax.ShapeDtypeStruct((n * l, d), BF16),
        jax.ShapeDtypeStruct((n, WINDOW, KV_W), F32),
        jax.ShapeDtypeStruct((n, WINDOW, KV_W), F32),
        jax.ShapeDtypeStruct((n, CONV_W - 1, MLSTM_W), F32),
        jax.ShapeDtypeStruct((n, M_HEADS, M_HEAD_DIM, M_HEAD_DIM), F32),
        jax.ShapeDtypeStruct((n, M_HEADS, M_HEAD_DIM), F32),
        jax.ShapeDtypeStruct((n, 1, 128), F32),
    )
    out_specs = (
        pl.BlockSpec((t * l, d), lambda i: (i, 0)),
        pl.BlockSpec((t, WINDOW, KV_W), m3),
        pl.BlockSpec((t, WINDOW, KV_W), m3),
        pl.BlockSpec((t, CONV_W - 1, MLSTM_W), m3),
        pl.BlockSpec((t, M_HEADS, M_HEAD_DIM, M_HEAD_DIM), m4),
        pl.BlockSpec((t, M_HEADS, M_HEAD_DIM), m3),
        pl.BlockSpec((t, 1, 128), m3),
    )
    in_specs = [
        pl.BlockSpec((t, l, d), m3),
        pl.BlockSpec((t, WINDOW, KV_W), m3),
        pl.BlockSpec((t, WINDOW, KV_W), m3),
        pl.BlockSpec((t, CONV_W - 1, MLSTM_W), m3),
        pl.BlockSpec((t, M_HEADS, M_HEAD_DIM, M_HEAD_DIM), m4),
        pl.BlockSpec((t, M_HEADS, M_HEAD_DIM), m3),
        pl.BlockSpec((t, 1, 128), m3),
    ] + _mixer_weight_specs()
    return pl.pallas_call(
        _sample_kernel,
        grid=(n // t,),
        in_specs=in_specs,
        out_specs=out_specs,
        out_shape=out_shape,
        scratch_shapes=[pltpu.VMEM((t, 16, MLSTM_W), F32)],
        compiler_params=pltpu.CompilerParams(
            dimension_semantics=("arbitrary",), vmem_limit_bytes=VMEM_LIMIT),
        name="sample_mixer",
    )(x_sample, ck, cv, sconv8, s_c, s_n, s_m3, *mixer_w)


def kernel(x_prompt, x_sample, cache_k, cache_v, state_conv, state_C, state_n, state_m, meta_tokens,
           w_in, w_conv, b_conv, w_mq, w_mk, b_i, b_f, attn_sinks, g_attn, g_mlstm, w_out,
           ln1_g, ln1_b, w_gate, w_up, w_down, ln2_g, ln2_b):
    b, s, d = x_prompt.shape
    n, l, _ = x_sample.shape

    w0 = w_in[0]
    w_in_parts = (_pair_heads(w0[:, :ATT_W], 1).astype(BF16), w0[:, ATT_W:IN_MAIN].astype(BF16),
                  jnp.pad(w0[:, IN_MAIN:], ((0, 0), (0, IN_PAD - IN_MAIN - IN_GATES))).astype(BF16))
    wqk = jnp.concatenate([w_mq[0], w_mk[0]], axis=-1).astype(BF16)
    gb = jnp.concatenate([b_i[0], b_f[0], jnp.zeros((128 - IN_GATES,), F32)]).reshape(1, 128)
    mixer_w = w_in_parts + (w_conv[0], b_conv[0].reshape(1, MLSTM_W), wqk, gb, attn_sinks[0],
                            _pair_heads(g_attn[0].reshape(1, ATT_W), 1), g_mlstm[0].reshape(1, MLSTM_W))
    post_w = (w_out[0], ln1_g[0].reshape(1, d), ln1_b[0].reshape(1, d), w_gate[0], w_up[0], w_down[0],
              ln2_g[0].reshape(1, d), ln2_b[0].reshape(1, d))

    key_minor = lambda c: c[0].transpose(0, 2, 3, 1).reshape(n, KV_W, WINDOW)
    s_m3 = jnp.pad(state_m[0], ((0, 0), (0, 128 - M_HEADS))).reshape(n, 1, 128)
    ymix_s, sk, sv, scv, s_c, s_n, s_mo = _sample_mixer(
        x_sample, key_minor(cache_k), key_minor(cache_v), state_conv[0], state_C[0], state_n[0], s_m3, mixer_w)

    meta_blk = jnp.concatenate([jnp.zeros((META_PAD, d), x_prompt.dtype), meta_tokens.astype(x_prompt.dtype)], axis=0)
    y_prompt, y_sample, pk, pv, pconv, p_c, p_n, pm8 = _prompt_layer(
        x_prompt, meta_blk, x_sample.reshape(n * l, d), ymix_s, mixer_w, post_w)

    kv5 = lambda a: a.reshape(1, a.shape[0], WINDOW, 2, HEAD_DIM)
    kv5_key_minor = lambda a: a.reshape(a.shape[0], 2, HEAD_DIM, WINDOW).transpose(0, 3, 1, 2)[None]
    return (y_prompt, y_sample.reshape(n, l, d),
            kv5_key_minor(pk), kv5_key_minor(pv), pconv[None], p_c[None], p_n[None], pm8[None, :, :M_HEADS, 0],
            kv5(sk), kv5(sv), scv[None], s_c[None], s_n[None], s_mo[None, :, 0, :M_HEADS])
```

```python
import functools

import jax
import jax.numpy as jnp
from jax import lax
from jax.experimental import pallas as pl
from jax.experimental.pallas import tpu as pltpu

F32 = jnp.float32
BF16 = jnp.bfloat16

D_MODEL = 1024
ATT_W = 512
MLSTM_W = 512
HEAD_DIM = 64
N_HEADS = 8
KV_W = 128
WINDOW = 128
BLOCK = 128
M_HEADS = 4
M_HEAD_DIM = 128
CONV_W = 4
N_META = 16
META_PAD = BLOCK - N_META
D_FF = 2816
IN_MAIN = 2304
IN_GATES = 2 * M_HEADS
IN_PAD = IN_MAIN + 128
DEPTH = 1
ALIBI_SLOPES = tuple(2.0 ** (-8.0 * (h + 1) / N_HEADS) for h in range(N_HEADS))
DEEPNORM_ALPHA = (2.0 * DEPTH) ** 0.25
EPS = 1e-5
SEQ_TILE = 16
STEP_BLOCKS = 2
FF_CHUNK = 256
FF_LOOKAHEAD = 3
PROJ_CHUNK = 256
WEIGHT_ROWS = 128
WEIGHT_SLOTS = 4
VMEM_LIMIT = 56 * 1024 * 1024


def _pair_heads(a, axis):
    shape = a.shape
    a = a.reshape(shape[:axis] + (2, 4, HEAD_DIM) + shape[axis + 1:])
    return jnp.swapaxes(a, axis, axis + 1).reshape(shape)


def _dot(a, b):
    return jnp.dot(a, b, preferred_element_type=F32)


def _dot_nt(a, b):
    return lax.dot_general(a, b, (((1,), (1,)), ((), ())), preferred_element_type=F32)


def _dot_tn(a, b):
    return lax.dot_general(a, b, (((0,), (0,)), ((), ())), preferred_element_type=F32)


def _split2(x):
    hi = x.astype(BF16)
    lo = (x - hi.astype(F32)).astype(BF16)
    return hi, lo


def _sigmoid(x):
    return 1.0 / (1.0 + jnp.exp(-x))


def _log_sigmoid(x):
    return -(jnp.maximum(-x, 0.0) + jnp.log1p(jnp.exp(-jnp.abs(x))))


def _layer_norm(x, g, b):
    mu = jnp.mean(x, axis=-1, keepdims=True)
    xc = x - mu
    var = jnp.mean(xc * xc, axis=-1, keepdims=True)
    return xc * lax.rsqrt(var + EPS) * g + b


def _attn_head_norm(att, g):
    lo_half = lax.broadcasted_iota(jnp.int32, (att.shape[0], 128), 1) < HEAD_DIM

    def seg_mean(x):
        s_all = jnp.sum(x, axis=1, keepdims=True)
        s_lo = jnp.sum(jnp.where(lo_half, x, 0.0), axis=1, keepdims=True)
        return jnp.where(lo_half, s_lo, s_all - s_lo) * (1.0 / HEAD_DIM)

    out = []
    for grp in range(att.shape[1] // 128):
        x = att[:, grp * 128:(grp + 1) * 128]
        xc = x - seg_mean(x)
        out.append(xc * lax.rsqrt(seg_mean(xc * xc) + EPS))
    return jnp.concatenate(out, axis=1) * g


def _mlstm_head_norm(hm, g):
    mu = jnp.mean(hm, axis=-1, keepdims=True)
    xc = hm - mu
    var = jnp.mean(xc * xc, axis=-1, keepdims=True)
    return xc * lax.rsqrt(var + EPS) * g


def _softmax_parts(parts, sink):
    sp = [jnp.where(m, s * (HEAD_DIM ** -0.5) + a, -jnp.inf) for s, m, a in parts]
    mx = sink
    for s in sp:
        mx = jnp.maximum(mx, jnp.max(s, axis=1, keepdims=True))
    ps = [jnp.exp(s - mx) for s in sp]
    den = jnp.exp(sink - mx)
    for p in ps:
        den = den + jnp.sum(p, axis=1, keepdims=True)
    return ps, 1.0 / den


def _conv_silu(window, wconv_ref, bconv_ref):
    acc = bconv_ref[...]
    for j in range(CONV_W):
        acc = acc + window(j) * wconv_ref[j:j + 1, :]
    return acc * _sigmoid(acc)


def _gate_columns(gates, gb_ref, row_valid, ltri, extra=None):
    lane = lax.broadcasted_iota(jnp.int32, gates.shape, 1)
    gb = gates + gb_ref[...]
    is_i = lane < M_HEADS
    is_f = (lane >= M_HEADS) & (lane < 2 * M_HEADS)
    logf = _log_sigmoid(gb)
    if row_valid is not None:
        lf = jnp.where(is_f & row_valid, logf, 0.0)
        li = jnp.where(row_valid, gb, -jnp.inf)
    else:
        lf = jnp.where(is_f, logf, 0.0)
        li = gb
    hi_lo = jnp.concatenate(_split2(lf), axis=1)
    bc = _dot(ltri, hi_lo)
    x = jnp.where(is_i, li, bc[:, :128] + bc[:, 128:])
    if extra is None:
        return x, None
    be = _dot(extra, hi_lo)
    return x, be[:, :128] + be[:, 128:]


class _FfnStages:
    N_CHUNKS = D_FF // FF_CHUNK

    def __init__(self, x, ymix, wout_ref, ln1g_ref, ln1b_ref, wg_ref, wu_ref, wd_ref, ln2g_ref, ln2b_ref):
        self.x, self.ymix = x, ymix
        self.wout_ref, self.ln1g_ref, self.ln1b_ref = wout_ref, ln1g_ref, ln1b_ref
        self.wg_ref, self.wu_ref, self.wd_ref = wg_ref, wu_ref, wd_ref
        self.ln2g_ref, self.ln2b_ref = ln2g_ref, ln2b_ref

    def head(self):
        self.x1 = _layer_norm(DEEPNORM_ALPHA * self.x + _dot(self.ymix, self.wout_ref[...]),
                              self.ln1g_ref[...], self.ln1b_ref[...])
        self.x1b = self.x1.astype(BF16)
        self.acc = DEEPNORM_ALPHA * self.x1

    def _gate_up(self, c):
        cs = slice(c * FF_CHUNK, (c + 1) * FF_CHUNK)
        return _dot(self.x1b, self.wg_ref[:, cs]), _dot(self.x1b, self.wu_ref[:, cs])

    def chunk(self, c):
        if c == 0:
            self.gu = {}
        for n in range(c, min(c + FF_LOOKAHEAD, self.N_CHUNKS - 1) + 1):
            if n not in self.gu:
                self.gu[n] = self._gate_up(n)
        g, u = self.gu.pop(c)
        hid = (g * _sigmoid(g) * u).astype(BF16)
        self.acc = self.acc + _dot(hid, self.wd_ref[c * FF_CHUNK:(c + 1) * FF_CHUNK, :])

    def tail(self):
        return _layer_norm(self.acc, self.ln2g_ref[...], self.ln2b_ref[...])

    def run(self):
        self.head()
        for c in range(self.N_CHUNKS):
            self.chunk(c)
        return self.tail()


def _mix_blocks(zc, nb, pad_rows, thr_first, state, cbuf, wconv_ref, bconv_ref, wqk_ref, gb_ref,
                sinks_ref, gattn_ref, gml_ref, hook):
    rows = nb * BLOCK
    kp_lo, kp_hi, vp_lo, vp_hi, c_state, m_state = state
    c_state, m_state = list(c_state), list(m_state)
    lane = lax.broadcasted_iota(jnp.int32, (BLOCK, 128), 1)
    lo_half = lane < HEAD_DIM
    row_valid = None
    if pad_rows:
        row_valid = lax.broadcasted_iota(jnp.int32, (rows, 1), 0) >= pad_rows

    s_all, vmasks = [], []
    for blk in range(nb):
        rs = slice(blk * BLOCK, (blk + 1) * BLOCK)
        kb = zc[rs, ATT_W:ATT_W + KV_W].astype(BF16)
        vb = zc[rs, ATT_W + KV_W:ATT_W + 2 * KV_W].astype(BF16)
        zero_b = jnp.zeros_like(kb)
        k_lo, k_hi = jnp.where(lo_half, kb, zero_b), jnp.where(lo_half, zero_b, kb)
        v_lo, v_hi = jnp.where(lo_half, vb, zero_b), jnp.where(lo_half, zero_b, vb)
        kmask = jnp.concatenate([kp_lo, k_lo, kp_hi, k_hi], axis=0)
        vmasks.append(jnp.concatenate([vp_lo, v_lo, vp_hi, v_hi], axis=0))
        q4 = jnp.concatenate([zc[rs, r * 128:(r + 1) * 128] for r in range(4)], axis=0).astype(BF16)
        s_all.append(_dot_nt(q4, kmask))
        kp_lo, kp_hi, vp_lo, vp_hi = k_lo, k_hi, v_lo, v_hi
    hook("scores")

    c_in = zc[:, 768:1280]
    if row_valid is not None:
        c_in = jnp.where(row_valid, c_in, 0.0)
    cbuf[8:8 + rows, :] = c_in
    c_act = _conv_silu(lambda jj: cbuf[5 + jj:5 + jj + rows, :], wconv_ref, bconv_ref)
    cbuf[0:8, :] = cbuf[rows:rows + 8, :]
    hs = range(M_HEADS)
    ca_b = [c_act[:, h * 128:(h + 1) * 128].astype(BF16) for h in hs]
    qk = [_dot(ca_b[h], wqk_ref[h]) for h in hs]
    qm_b = [qk[h][:, :M_HEAD_DIM].astype(BF16) for h in hs]
    kmt_f = [(qk[h][:, M_HEAD_DIM:] * (M_HEAD_DIM ** -0.5)).T for h in hs]
    hook("front")

    ti = lax.broadcasted_iota(jnp.int32, (rows, rows), 0)
    si = lax.broadcasted_iota(jnp.int32, (rows, rows), 1)
    ltri = jnp.where((si <= ti) & ((si >> 7) == (ti >> 7)), 1.0, 0.0).astype(BF16)
    xg, _ = _gate_columns(zc[:, IN_MAIN:IN_PAD], gb_ref, row_valid, ltri)
    xgt = xg.T
    hook("gates")

    qi = lax.broadcasted_iota(jnp.int32, (BLOCK, BLOCK), 0)
    ks = lax.broadcasted_iota(jnp.int32, (BLOCK, BLOCK), 1)
    own = ks <= qi
    distf = jnp.where(own, qi - ks, WINDOW + qi - ks).astype(F32)
    zero_p = jnp.zeros((BLOCK, BLOCK), BF16)
    att_rows = []
    for blk in range(nb):
        live = None
        if blk == 0 and thr_first is not None:
            live = jnp.where(own, ks + BLOCK, ks) >= thr_first
        p_rows, inv = [], {}
        for r in range(4):
            pr = []
            for c in range(2):
                h = r + 4 * c
                s_blk = s_all[blk][r * 128:(r + 1) * 128, c * 256:(c + 1) * 256]
                sp = jnp.where(own, s_blk[:, BLOCK:], s_blk[:, :BLOCK]) * (HEAD_DIM ** -0.5) \
                    - ALIBI_SLOPES[h] * distf
                if live is not None:
                    sp = jnp.where(live, sp, -jnp.inf)
                sink = sinks_ref[h]
                mx = jnp.maximum(jnp.max(sp, axis=1, keepdims=True), sink)
                p = jnp.exp(sp - mx)
                inv[h] = 1.0 / (jnp.sum(p, axis=1, keepdims=True) + jnp.exp(sink - mx))
                p_b = p.astype(BF16)
                pr += [jnp.where(own, zero_p, p_b), jnp.where(own, p_b, zero_p)]
            p_rows.append(jnp.concatenate(pr, axis=1))
            hook("softmax", blk * 4 + r)
        o_all = _dot(jnp.concatenate(p_rows, axis=0), vmasks[blk])
        att_rows.append(jnp.concatenate(
            [o_all[r * 128:(r + 1) * 128] * jnp.where(lo_half, inv[r], inv[r + 4]) for r in range(4)], axis=1))
        hook("pv", blk)

    ones_col = jnp.where(lane == 0, 1.0, 0.0).astype(BF16)
    tb = lax.broadcasted_iota(jnp.int32, (BLOCK, BLOCK), 0)
    sb = lax.broadcasted_iota(jnp.int32, (BLOCK, BLOCK), 1)
    tri = sb <= tb
    hh_rows = [[] for _ in hs]
    for blk in range(nb):
        rs = slice(blk * BLOCK, (blk + 1) * BLOCK)
        b_c = [xg[rs, 4 + h:5 + h] for h in hs]
        b_r = [xgt[4 + h:5 + h, rs] for h in hs]
        li_r = [xgt[h:h + 1, rs] for h in hs]
        kmt = [kmt_f[h][:, rs] for h in hs]
        v_ext = [jnp.concatenate([zc[rs, 1280 + h * 128:1280 + (h + 1) * 128].astype(BF16), ones_col], axis=1)
                 for h in hs]
        s_qk = [_dot(qm_b[h][rs], kmt[h].astype(BF16)) for h in hs]
        inter = [_dot(qm_b[h][rs], c_state[h].astype(BF16)) for h in hs]
        hook("mlstm_a", blk)
        dmat = [jnp.where(tri, b_c[h] - b_r[h] + li_r[h], -jnp.inf) for h in hs]
        m_inter = [b_c[h] + m_state[h] for h in hs]
        m_t = [jnp.maximum(m_inter[h], jnp.max(dmat[h], axis=1, keepdims=True)) for h in hs]
        w_inter = [jnp.exp(m_inter[h] - m_t[h]) for h in hs]
        sc_b = [(s_qk[h] * jnp.exp(dmat[h] - m_t[h])).astype(BF16) for h in hs]
        hook("mlstm_b", blk)
        nd = [_dot(sc_b[h], v_ext[h]) + w_inter[h] * inter[h] for h in hs]
        for h in hs:
            m_end = m_t[h][BLOCK - 1:BLOCK, :]
            b_last = b_c[h][BLOCK - 1:BLOCK, :]
            decay = jnp.exp(b_last + m_state[h] - m_end)
            wk_r = jnp.exp(b_last - b_r[h] + li_r[h] - m_end)
            c_state[h] = decay * c_state[h] + _dot((kmt[h] * wk_r).astype(BF16), v_ext[h])
            m_state[h] = m_end
            hh_rows[h].append(nd[h][:, :128] / jnp.maximum(jnp.abs(nd[h][:, 128:129]), jnp.exp(-m_t[h])))
        hook("mlstm_c", blk)

    y_att = _attn_head_norm(jnp.concatenate(att_rows, axis=0), gattn_ref[...])
    hook("attnorm")
    y_m = []
    for h in hs:
        hm = _sigmoid(zc[:, 1792 + h * 128:1792 + (h + 1) * 128]) * jnp.concatenate(hh_rows[h], axis=0)
        y_m.append(_mlstm_head_norm(hm, gml_ref[:, h * 128:(h + 1) * 128]))
        hook("headnorm", h)
    ymix = jnp.concatenate([y_att] + y_m, axis=1).astype(BF16)
    return ymix, (kp_lo, kp_hi, vp_lo, vp_hi, c_state, m_state)


def _project(xb, w_in_refs, lo=0, hi=IN_PAD):
    bounds = (0, ATT_W, IN_MAIN, IN_PAD)
    parts = []
    for ref, start, stop in zip(w_in_refs, bounds[:-1], bounds[1:]):
        a, b = max(lo, start), min(hi, stop)
        if a < b:
            parts.append(_dot(xb, ref[:, a - start:b - start]))
    return parts[0] if len(parts) == 1 else jnp.concatenate(parts, axis=1)


def _meta_state(meta_ref, w_inq_ref, w_inr_ref, w_ing_ref, wconv_ref, bconv_ref, wqk_ref, gb_ref,
                sinks_ref, gattn_ref,
                gml_ref, kplo_o, kphi_o, vplo_o, vphi_o, cb_o, ctn_o, mst_o, zc, cbuf):
    zc[...] = _project(meta_ref[...].astype(BF16), (w_inq_ref, w_inr_ref, w_ing_ref))
    cbuf[0:8, :] = jnp.zeros((8, MLSTM_W), F32)
    zb = jnp.zeros((BLOCK, KV_W), BF16)
    state = (zb, zb, zb, zb, [jnp.zeros((M_HEAD_DIM, 2 * M_HEAD_DIM), F32)] * M_HEADS,
             [jnp.zeros((1, 1), F32)] * M_HEADS)
    _, (kp_lo, kp_hi, vp_lo, vp_hi, c_state, m_state) = _mix_blocks(
        zc, 1, META_PAD, BLOCK + META_PAD, state, cbuf, wconv_ref, bconv_ref, wqk_ref, gb_ref,
        sinks_ref, gattn_ref, gml_ref, lambda name, index=0: None)
    kplo_o[...] = kp_lo
    kphi_o[...] = kp_hi
    vplo_o[...] = vp_lo
    vphi_o[...] = vp_hi
    cb_o[...] = cbuf[0:8, :]
    for h in range(M_HEADS):
        ctn_o[h] = c_state[h]
        mst_o[h:h + 1, :] = jnp.broadcast_to(m_state[h], (1, 128))
    mst_o[M_HEADS:8, :] = jnp.zeros((8 - M_HEADS, 128), F32)


def _load_ffn_weights(wout_hbm, wg_hbm, wu_hbm, wd_hbm, wout_v, wg_v, wu_v, wd_v, stage, sems):
    jobs = []
    for src, dst, nrows in ((wg_hbm, wg_v, D_MODEL), (wu_hbm, wu_v, D_MODEL), (wd_hbm, wd_v, D_FF)):
        jobs += [(src, r, WEIGHT_ROWS, dst, r) for r in range(0, nrows, WEIGHT_ROWS)]
    for h in range(N_HEADS):
        jobs.append((wout_hbm, h * HEAD_DIM, HEAD_DIM, wout_v, ((h % 4) * 2 + h // 4) * HEAD_DIM))
    jobs += [(wout_hbm, r, WEIGHT_ROWS, wout_v, r) for r in range(ATT_W, D_MODEL, WEIGHT_ROWS)]

    nslot = stage.shape[0]

    def copy(i):
        src, r0, nrows, _, _ = jobs[i]
        cols = src.shape[1]
        return pltpu.make_async_copy(src.at[pl.ds(r0, nrows), :],
                                     stage.at[i % nslot, pl.ds(0, nrows), pl.ds(0, cols)], sems.at[i % nslot])

    for i in range(min(nslot - 1, len(jobs))):
        copy(i).start()
    for i, (src, _, nrows, dst, d0) in enumerate(jobs):
        if i + nslot - 1 < len(jobs):
            copy(i + nslot - 1).start()
        copy(i).wait()
        dst[d0:d0 + nrows, :] = stage[i % nslot, 0:nrows, 0:src.shape[1]].astype(BF16)


def _pair_kernel(xin_ref, xres_ref, meta_ref, xs_ref, ymix_s_ref,
                 w_inq_ref, w_inr_ref, w_ing_ref, wconv_ref, bconv_ref, wqk_ref, gb_ref, sinks_ref,
                 gattn_ref, gml_ref,
                 wout_hbm, ln1g_ref, ln1b_ref, wg_hbm, wu_hbm, wd_hbm, ln2g_ref, ln2b_ref,
                 y_ref, ys_ref, pk_ref, pv_ref, pconv_ref, pc_ref, pn_ref, pm_ref,
                 wout_v, wg_v, wu_v, wd_v, wstage, wsems,
                 kplo_i, kphi_i, vplo_i, vphi_i, cb_i, ctn_i, mst_i,
                 zcur, znext, kvlast, kplo, kphi, vplo, vphi, cbuf, ctn, mst, yprev, *, npair, n_pair_steps):
    t = pl.program_id(0)
    p = lax.rem(t + (npair - 1), npair)

    @pl.when(t == 0)
    def _init_pipeline():
        _load_ffn_weights(wout_hbm, wg_hbm, wu_hbm, wd_hbm, wout_v, wg_v, wu_v, wd_v, wstage, wsems)
        zcur[...] = jnp.zeros_like(zcur)
        yprev[...] = jnp.zeros_like(yprev)
        _meta_state(meta_ref, w_inq_ref, w_inr_ref, w_ing_ref, wconv_ref, bconv_ref, wqk_ref, gb_ref,
                    sinks_ref, gattn_ref, gml_ref, kplo_i, kphi_i, vplo_i, vphi_i, cb_i, ctn_i, mst_i,
                    znext.at[pl.ds(0, BLOCK)], cbuf)

    @pl.when((p == 0) | (t == 0))
    def _load_meta_state():
        kplo[...] = kplo_i[...]
        kphi[...] = kphi_i[...]
        vplo[...] = vplo_i[...]
        vphi[...] = vphi_i[...]
        cbuf[0:8, :] = cb_i[...]
        ctn[...] = ctn_i[...]
        mst[...] = mst_i[...]

    ffn_w = (wout_v, ln1g_ref, ln1b_ref, wg_v, wu_v, wd_v, ln2g_ref, ln2b_ref)

    @pl.when(t < n_pair_steps)
    def _pair_step():
        ffn = _FfnStages(xres_ref[...], yprev[...], *ffn_w)
        xb_b = xin_ref[...].astype(BF16)

        n_proj = -(-IN_PAD // PROJ_CHUNK)

        def proj(n):
            lo, hi = n * PROJ_CHUNK, min((n + 1) * PROJ_CHUNK, IN_PAD)
            znext[:, lo:hi] = _project(xb_b, (w_inq_ref, w_inr_ref, w_ing_ref), lo, hi)

        proj_it, ffn_it = iter(range(n_proj)), iter(range(ffn.N_CHUNKS))

        def fill_proj(n):
            for c in [c for _, c in zip(range(n), proj_it)]:
                proj(c)

        def fill_ffn(n):
            for c in [c for _, c in zip(range(n), ffn_it)]:
                ffn.chunk(c)

        out = {}

        def hook(name, index=0):
            if name == "scores":
                fill_proj(2)
                ffn.head()
            elif name == "front":
                fill_proj(1)
            elif name == "gates":
                fill_ffn(1)
            elif name == "softmax":
                if index % 2 == 1:
                    fill_ffn(1)
            elif name in ("pv", "mlstm_a", "mlstm_b"):
                fill_ffn(1)
            elif name == "mlstm_c":
                fill_proj(1)
            elif name == "attnorm":
                fill_ffn(ffn.N_CHUNKS)
                out["y"] = ffn.tail()
                fill_proj(1)
            elif name == "headnorm":
                fill_proj(1)

        state = (kplo[...], kphi[...], vplo[...], vphi[...], [ctn[h] for h in range(M_HEADS)],
                 [mst[h:h + 1, 0:1] for h in range(M_HEADS)])
        thr_first = jnp.where(p == 0, META_PAD, 0)
        kvlast[...] = zcur[(STEP_BLOCKS - 1) * BLOCK:STEP_BLOCKS * BLOCK, ATT_W:ATT_W + 2 * KV_W]
        ymix, (kp_lo, kp_hi, vp_lo, vp_hi, c_state, m_state) = _mix_blocks(
            zcur, STEP_BLOCKS, 0, thr_first, state, cbuf, wconv_ref, bconv_ref, wqk_ref, gb_ref,
            sinks_ref, gattn_ref, gml_ref, hook)
        fill_proj(n_proj)
        y_ref[...] = out["y"]
        kplo[...] = kp_lo
        kphi[...] = kp_hi
        vplo[...] = vp_lo
        vphi[...] = vp_hi
        for h in range(M_HEADS):
            ctn[h] = c_state[h]
            mst[h:h + 1, :] = jnp.broadcast_to(m_state[h], (1, 128))
        yprev[...] = ymix
        zcur[...] = znext[...]

    @pl.when(t >= n_pair_steps)
    def _sample_ffn_step():
        ys_ref[...] = _FfnStages(xs_ref[...], ymix_s_ref[...], *ffn_w).run()

    @pl.when((p == npair - 1) & (t < n_pair_steps))
    def _final():
        pk_ref[...] = kvlast[:, 0:KV_W].T
        pv_ref[...] = kvlast[:, KV_W:2 * KV_W].T
        pconv_ref[...] = cbuf[8 - (CONV_W - 1):8, :]
        pm_ref[...] = mst[...]
        for h in range(M_HEADS):
            c_n = ctn[h].T
            pc_ref[h] = c_n[0:M_HEAD_DIM, :]
            pn_ref[h:h + 1, :] = c_n[M_HEAD_DIM:M_HEAD_DIM + 1, :]


def _sample_kernel(xs_ref, ck_ref, cv_ref, sconv_ref, sc_ref, sn_ref, sm_ref,
                   w_inq_ref, w_inr_ref, w_ing_ref, wconv_ref, bconv_ref, wqk_ref, gb_ref, sinks_ref,
                   gattn_ref, gml_ref,
                   ymix_ref, sk_ref, sv_ref, scv_ref, sco_ref, sno_ref, smo_ref, cext):
    ns, L = SEQ_TILE, 8
    xs = xs_ref[...].reshape(ns * L, D_MODEL)
    z = _project(xs.astype(BF16), (w_inq_ref, w_inr_ref, w_ing_ref))

    lane = lax.broadcasted_iota(jnp.int32, (BLOCK, 128), 1)
    lo_half = lane < HEAD_DIM
    lo3 = lax.broadcasted_iota(jnp.int32, (ns, L, 128), 2) < HEAD_DIM
    ti = lax.broadcasted_iota(jnp.int32, (BLOCK, BLOCK), 0)
    si = lax.broadcasted_iota(jnp.int32, (BLOCK, BLOCK), 1)
    same_seq = (ti >> 3) == (si >> 3)
    tl, sl_ = ti & 7, si & 7
    causal_seq = same_seq & (sl_ <= tl)

    k_new = z[:, ATT_W:ATT_W + KV_W]
    v_new = z[:, ATT_W + KV_W:ATT_W + 2 * KV_W]
    kb, vb = k_new.astype(BF16), v_new.astype(BF16)
    zero_b = jnp.zeros_like(kb)
    kn_mask = jnp.concatenate([jnp.where(lo_half, kb, zero_b), jnp.where(lo_half, zero_b, kb)], axis=0)
    vn_mask = jnp.concatenate([jnp.where(lo_half, vb, zero_b), jnp.where(lo_half, zero_b, vb)], axis=0)
    qg = [z[:, r * 128:(r + 1) * 128] for r in range(4)]
    s_new = _dot_nt(jnp.concatenate(qg, axis=0).astype(BF16), kn_mask)
    qg3 = [q.reshape(ns, L, 128) for q in qg]
    q8 = jnp.concatenate([jnp.where(lo3, q, 0.0) for q in qg3] + [jnp.where(lo3, 0.0, q) for q in qg3],
                         axis=1).astype(BF16)
    s_cache = jnp.einsum('nqd,ndk->nqk', q8, ck_ref[...].astype(BF16), preferred_element_type=F32)

    c3 = z[:, 768:1280].reshape(ns, L, MLSTM_W)
    cext[:, 8 - (CONV_W - 1):8, :] = sconv_ref[...]
    cext[:, 8:16, :] = c3
    scv_ref[...] = c3[:, L - (CONV_W - 1):L, :]
    c_act = _conv_silu(lambda jj: cext[:, 5 + jj:5 + jj + L, :], wconv_ref, bconv_ref)
    c_act = c_act.reshape(BLOCK, MLSTM_W)
    lblk = jnp.where(causal_seq, 1.0, 0.0).astype(BF16)
    lseq = jnp.where(same_seq, 1.0, 0.0).astype(BF16)
    xg, blast = _gate_columns(z[:, IN_MAIN:IN_PAD], gb_ref, None, lblk, extra=lseq)
    xgt = xg.T
    m_rep = jnp.broadcast_to(sm_ref[...], (ns, L, 128)).reshape(BLOCK, 128)
    hs = range(M_HEADS)
    qk = [_dot(c_act[:, h * 128:(h + 1) * 128].astype(BF16), wqk_ref[h]) for h in hs]
    qm = [qk[h][:, :128] for h in hs]
    qm_b = [qm[h].astype(BF16) for h in hs]
    km = [qk[h][:, 128:] * (M_HEAD_DIM ** -0.5) for h in hs]
    km_b = [km[h].astype(BF16) for h in hs]
    vh_b = [z[:, 1280 + h * 128:1280 + (h + 1) * 128].astype(BF16) for h in hs]
    c_old = [sc_ref[:, h] for h in hs]
    s_qk = [_dot_nt(qm_b[h], km_b[h]) for h in hs]
    cq = [_dot_nt(qm_b[h], c_old[h].reshape(ns * 128, 128).astype(BF16)) for h in hs]

    mask_c = si > tl
    dist_c = (WINDOW + tl - si).astype(F32)
    dist_n = (tl - sl_).astype(F32)
    pc_parts, pn_rows, inv = {}, [], {}
    for r in range(4):
        pn = []
        for c in range(2):
            h = r + 4 * c
            sc_c = s_cache[:, c * 32 + r * L:c * 32 + (r + 1) * L, :].reshape(BLOCK, 128)
            sc_n = s_new[r * 128:(r + 1) * 128, c * 128:(c + 1) * 128]
            (p_c, p_n), inv[h] = _softmax_parts(
                [(sc_c, mask_c, -ALIBI_SLOPES[h] * dist_c), (sc_n, causal_seq, -ALIBI_SLOPES[h] * dist_n)],
                sinks_ref[h])
            pc_parts[(c, r)] = p_c.reshape(ns, L, 128)
            pn.append(p_n.astype(BF16))
        pn_rows.append(jnp.concatenate(pn, axis=1))
    p8 = jnp.concatenate([pc_parts[(c, r)] for c in range(2) for r in range(4)], axis=1).astype(BF16)
    o_cache = jnp.einsum('nqk,ndk->nqd', p8, cv_ref[...].astype(BF16), preferred_element_type=F32)
    o_new = _dot(jnp.concatenate(pn_rows, axis=0), vn_mask)
    groups = []
    for r in range(4):
        oa = o_cache[:, r * L:(r + 1) * L, :].reshape(BLOCK, 128)
        ob = o_cache[:, 32 + r * L:32 + (r + 1) * L, :].reshape(BLOCK, 128)
        o = jnp.where(lo_half, oa, ob) + o_new[r * 128:(r + 1) * 128]
        groups.append(o * jnp.where(lo_half, inv[r], inv[r + 4]))
    y_att = _attn_head_norm(jnp.concatenate(groups, axis=1), gattn_ref[...])

    sk_ref[:, 0:WINDOW - L, :] = jnp.swapaxes(ck_ref[...], 1, 2)[:, L:WINDOW, :]
    sk_ref[:, WINDOW - L:WINDOW, :] = k_new.reshape(ns, L, 128)
    sv_ref[:, 0:WINDOW - L, :] = jnp.swapaxes(cv_ref[...], 1, 2)[:, L:WINDOW, :]
    sv_ref[:, WINDOW - L:WINDOW, :] = v_new.reshape(ns, L, 128)

    b_c = [xg[:, 4 + h:5 + h] for h in hs]
    li_c = [xg[:, h:h + 1] for h in hs]
    b_r = [xgt[4 + h:5 + h, :] for h in hs]
    li_r = [xgt[h:h + 1, :] for h in hs]
    bl_c = [blast[:, 4 + h:5 + h] for h in hs]
    m_prev = [m_rep[:, h:h + 1] for h in hs]
    dmat = [jnp.where(causal_seq, b_c[h] - b_r[h] + li_r[h], -jnp.inf) for h in hs]
    m_inter = [b_c[h] + m_prev[h] for h in hs]
    m_t = [jnp.maximum(m_inter[h], jnp.max(dmat[h], axis=1, keepdims=True)) for h in hs]
    w_inter = [jnp.exp(m_inter[h] - m_t[h]) for h in hs]
    sc = [s_qk[h] * jnp.exp(dmat[h] - m_t[h]) for h in hs]
    num_intra = [_dot(sc[h].astype(BF16), vh_b[h]) for h in hs]

    m_end3 = [jnp.broadcast_to(m_t[h], (BLOCK, 128)).reshape(ns, L, 128)[:, L - 1:L, :] for h in hs]
    m_end = [jnp.broadcast_to(m_end3[h], (ns, L, 128)).reshape(BLOCK, 128)[:, 0:1] for h in hs]
    decay = [jnp.exp(bl_c[h] + m_prev[h] - m_end[h]) for h in hs]
    kw = [km[h] * jnp.exp(bl_c[h] - b_c[h] + li_c[h] - m_end[h]) for h in hs]
    seq_of_row = ti >> 3
    lane3 = lax.broadcasted_iota(jnp.int32, (ns, 1, 128), 2)
    m_out = jnp.zeros((ns, 1, 128), F32)
    for h in hs:
        kw_b = kw[h].astype(BF16)
        k_big = jnp.concatenate(
            [jnp.where(seq_of_row == g, kw_b, jnp.zeros_like(kw_b)) for g in range(ns)], axis=1)
        c_up = _dot_tn(vh_b[h], k_big)
        c_up3 = jnp.stack([c_up[:, g * 128:(g + 1) * 128] for g in range(ns)], axis=0)
        decay3 = jnp.broadcast_to(decay[h], (BLOCK, 128)).reshape(ns, L, 128)[:, 0:1, :]
        n_old = sn_ref[:, h:h + 1, :]
        sco_ref[:, h] = decay3 * c_old[h] + c_up3
        sno_ref[:, h:h + 1, :] = decay3 * n_old + jnp.sum(kw[h].reshape(ns, L, 128), axis=1, keepdims=True)
        m_out = jnp.where(lane3 == h, m_end3[h], m_out)
    smo_ref[...] = m_out

    y_m = []
    for h in hs:
        num_inter = jnp.concatenate([cq[h][g * L:(g + 1) * L, g * 128:(g + 1) * 128] for g in range(ns)], axis=0)
        n_rep = jnp.broadcast_to(sn_ref[:, h:h + 1, :], (ns, L, 128)).reshape(BLOCK, 128)
        num = num_intra[h] + w_inter[h] * num_inter
        den = jnp.sum(sc[h], axis=1, keepdims=True) + w_inter[h] * jnp.sum(qm[h] * n_rep, axis=1, keepdims=True)
        hh = num / jnp.maximum(jnp.abs(den), jnp.exp(-m_t[h]))
        hm = _sigmoid(z[:, 1792 + h * 128:1792 + (h + 1) * 128]) * hh
        y_m.append(_mlstm_head_norm(hm, gml_ref[:, h * 128:(h + 1) * 128]))

    ymix_ref[...] = jnp.concatenate([y_att] + y_m, axis=1).astype(BF16)


def _const_spec(shape):
    nd = len(shape)
    return pl.BlockSpec(shape, lambda *_: (0,) * nd, pipeline_mode=pl.Buffered(1))


def _mixer_weight_specs():
    return [
        _const_spec((D_MODEL, ATT_W)),
        _const_spec((D_MODEL, IN_MAIN - ATT_W)),
        _const_spec((D_MODEL, IN_PAD - IN_MAIN)),
        _const_spec((CONV_W, MLSTM_W)),
        _const_spec((1, MLSTM_W)),
        _const_spec((M_HEADS, M_HEAD_DIM, 2 * M_HEAD_DIM)),
        _const_spec((1, 128)),
        pl.BlockSpec(memory_space=pltpu.SMEM),
        _const_spec((1, ATT_W)),
        _const_spec((1, MLSTM_W)),
    ]


def _post_weight_specs(d):
    hbm = pl.BlockSpec(memory_space=pl.ANY)
    return [hbm, _const_spec((1, d)), _const_spec((1, d)), hbm, hbm, hbm, _const_spec((1, d)), _const_spec((1, d))]


def _prompt_layer(x_prompt, meta_blk, x_sample_rows, ymix_sample, mixer_w, post_w):
    b, s, d = x_prompt.shape
    state_shapes = [((BLOCK, KV_W), BF16)] * 4 + [
        ((8, MLSTM_W), F32), ((M_HEADS, M_HEAD_DIM, 2 * M_HEAD_DIM), F32), ((8, 128), F32)]
    rows = STEP_BLOCKS * BLOCK
    npair = s // rows
    total = b * npair
    n_pair_steps = total + 2
    n_sample_steps = x_sample_rows.shape[0] // rows
    steps = n_pair_steps + n_sample_steps

    def block_map(lag):
        def index(t):
            tb = jnp.clip(t - lag, 0, total - 1)
            return (tb // npair, tb % npair, 0)
        return index

    def smap(t):
        return (jnp.clip(t - 1, 0, total - 1) // npair, 0, 0)

    def sample_map(t):
        return (jnp.clip(t - n_pair_steps, 0, n_sample_steps - 1), 0)

    out_shape = (
        jax.ShapeDtypeStruct((b, s, d), F32),
        jax.ShapeDtypeStruct(x_sample_rows.shape, F32),
        jax.ShapeDtypeStruct((b, BLOCK, KV_W), F32),
        jax.ShapeDtypeStruct((b, BLOCK, KV_W), F32),
        jax.ShapeDtypeStruct((b, CONV_W - 1, MLSTM_W), F32),
        jax.ShapeDtypeStruct((b, M_HEADS, M_HEAD_DIM, M_HEAD_DIM), F32),
        jax.ShapeDtypeStruct((b, M_HEADS, M_HEAD_DIM), F32),
        jax.ShapeDtypeStruct((b, 8, 128), F32),
    )
    out_specs = (
        pl.BlockSpec((None, rows, d), block_map(2)),
        pl.BlockSpec((rows, d), sample_map),
        pl.BlockSpec((None, BLOCK, KV_W), smap),
        pl.BlockSpec((None, BLOCK, KV_W), smap),
        pl.BlockSpec((None, CONV_W - 1, MLSTM_W), smap),
        pl.BlockSpec((None, M_HEADS, M_HEAD_DIM, M_HEAD_DIM), lambda t: smap(t) + (0,)),
        pl.BlockSpec((None, M_HEADS, M_HEAD_DIM), smap),
        pl.BlockSpec((None, 8, 128), smap),
    )
    return pl.pallas_call(
        functools.partial(_pair_kernel, npair=npair, n_pair_steps=n_pair_steps),
        grid=(steps,),
        in_specs=[pl.BlockSpec((None, rows, d), block_map(0)), pl.BlockSpec((None, rows, d), block_map(2)),
                  _const_spec((BLOCK, d)), pl.BlockSpec((rows, d), sample_map), pl.BlockSpec((rows, d), sample_map)]
        + _mixer_weight_specs() + _post_weight_specs(d),
        out_specs=out_specs,
        out_shape=out_shape,
        scratch_shapes=[
            pltpu.VMEM((d, d), BF16), pltpu.VMEM((d, D_FF), BF16), pltpu.VMEM((d, D_FF), BF16),
            pltpu.VMEM((D_FF, d), BF16),
            pltpu.VMEM((WEIGHT_SLOTS, WEIGHT_ROWS, D_FF), F32),
            pltpu.SemaphoreType.DMA((WEIGHT_SLOTS,)),
        ] + [pltpu.VMEM(shape, dt) for shape, dt in state_shapes] + [
            pltpu.VMEM((rows, IN_PAD), F32),
            pltpu.VMEM((rows, IN_PAD), F32),
            pltpu.VMEM((BLOCK, 2 * KV_W), F32),
        ] + [pltpu.VMEM((BLOCK, KV_W), BF16)] * 4 + [
            pltpu.VMEM((rows + 8, MLSTM_W), F32),
            pltpu.VMEM((M_HEADS, M_HEAD_DIM, 2 * M_HEAD_DIM), F32),
            pltpu.VMEM((8, 128), F32),
            pltpu.VMEM((rows, d), BF16),
        ],
        compiler_params=pltpu.CompilerParams(
            dimension_semantics=("arbitrary",), vmem_limit_bytes=VMEM_LIMIT),
        name="prompt_layer",
    )(x_prompt, x_prompt, meta_blk, x_sample_rows, ymix_sample, *mixer_w, *post_w)


def _sample_mixer(x_sample, ck, cv, sconv8, s_c, s_n, s_m3, mixer_w):
    n, l, d = x_sample.shape
    t = SEQ_TILE
    m3 = lambda i: (i, 0, 0)
    m4 = lambda i: (i, 0, 0, 0)
    out_shape = (
        jax.ShapeDtypeStruct((n * l, d), BF16),
        jax.ShapeDtypeStruct((n, WINDOW, KV_W), F32),
        jax.ShapeDtypeStruct((n, WINDOW, KV_W), F32),
        jax.ShapeDtypeStruct((n, CONV_W - 1, MLSTM_W), F32),
        jax.ShapeDtypeStruct((n, M_HEADS, M_HEAD_DIM, M_HEAD_DIM), F32),
        jax.ShapeDtypeStruct((n, M_HEADS, M_HEAD_DIM), F32),
        jax.ShapeDtypeStruct((n, 1, 128), F32),
    )
    out_specs = (
        pl.BlockSpec((t * l, d), lambda i: (i, 0)),
        pl.BlockSpec((t, WINDOW, KV_W), m3),
        pl.BlockSpec((t, WINDOW, KV_W), m3),
        pl.BlockSpec((t, CONV_W - 1, MLSTM_W), m3),
        pl.BlockSpec((t, M_HEADS, M_HEAD_DIM, M_HEAD_DIM), m4),
        pl.BlockSpec((t, M_HEADS, M_HEAD_DIM), m3),
        pl.BlockSpec((t, 1, 128), m3),
    )
    in_specs = [
        pl.BlockSpec((t, l, d), m3),
        pl.BlockSpec((t, WINDOW, KV_W), m3),
        pl.BlockSpec((t, WINDOW, KV_W), m3),
        pl.BlockSpec((t, CONV_W - 1, MLSTM_W), m3),
        pl.BlockSpec((t, M_HEADS, M_HEAD_DIM, M_HEAD_DIM), m4),
        pl.BlockSpec((t, M_HEADS, M_HEAD_DIM), m3),
        pl.BlockSpec((t, 1, 128), m3),
    ] + _mixer_weight_specs()
    return pl.pallas_call(
        _sample_kernel,
        grid=(n // t,),
        in_specs=in_specs,
        out_specs=out_specs,
        out_shape=out_shape,
        scratch_shapes=[pltpu.VMEM((t, 16, MLSTM_W), F32)],
        compiler_params=pltpu.CompilerParams(
            dimension_semantics=("arbitrary",), vmem_limit_bytes=VMEM_LIMIT),
        name="sample_mixer",
    )(x_sample, ck, cv, sconv8, s_c, s_n, s_m3, *mixer_w)


def kernel(x_prompt, x_sample, cache_k, cache_v, state_conv, state_C, state_n, state_m, meta_tokens,
           w_in, w_conv, b_conv, w_mq, w_mk, b_i, b_f, attn_sinks, g_attn, g_mlstm, w_out,
           ln1_g, ln1_b, w_gate, w_up, w_down, ln2_g, ln2_b):
    b, s, d = x_prompt.shape
    n, l, _ = x_sample.shape

    w0 = w_in[0]
    w_in_parts = (_pair_heads(w0[:, :ATT_W], 1).astype(BF16), w0[:, ATT_W:IN_MAIN].astype(BF16),
                  jnp.pad(w0[:, IN_MAIN:], ((0, 0), (0, IN_PAD - IN_MAIN - IN_GATES))).astype(BF16))
    wqk = jnp.concatenate([w_mq[0], w_mk[0]], axis=-1).astype(BF16)
    gb = jnp.concatenate([b_i[0], b_f[0], jnp.zeros((128 - IN_GATES,), F32)]).reshape(1, 128)
    mixer_w = w_in_parts + (w_conv[0], b_conv[0].reshape(1, MLSTM_W), wqk, gb, attn_sinks[0],
                            _pair_heads(g_attn[0].reshape(1, ATT_W), 1), g_mlstm[0].reshape(1, MLSTM_W))
    post_w = (w_out[0], ln1_g[0].reshape(1, d), ln1_b[0].reshape(1, d), w_gate[0], w_up[0], w_down[0],
              ln2_g[0].reshape(1, d), ln2_b[0].reshape(1, d))

    key_minor = lambda c: c[0].transpose(0, 2, 3, 1).reshape(n, KV_W, WINDOW)
    s_m3 = jnp.pad(state_m[0], ((0, 0), (0, 128 - M_HEADS))).reshape(n, 1, 128)
    ymix_s, sk, sv, scv, s_c, s_n, s_mo = _sample_mixer(
        x_sample, key_minor(cache_k), key_minor(cache_v), state_conv[0], state_C[0], state_n[0], s_m3, mixer_w)

    meta_blk = jnp.concatenate([jnp.zeros((META_PAD, d), x_prompt.dtype), meta_tokens.astype(x_prompt.dtype)], axis=0)
    y_prompt, y_sample, pk, pv, pconv, p_c, p_n, pm8 = _prompt_layer(
        x_prompt, meta_blk, x_sample.reshape(n * l, d), ymix_s, mixer_w, post_w)

    kv5 = lambda a: a.reshape(1, a.shape[0], WINDOW, 2, HEAD_DIM)
    kv5_key_minor = lambda a: a.reshape(a.shape[0], 2, HEAD_DIM, WINDOW).transpose(0, 3, 1, 2)[None]
    return (y_prompt, y_sample.reshape(n, l, d),
            kv5_key_minor(pk), kv5_key_minor(pv), pconv[None], p_c[None], p_n[None], pm8[None, :, :M_HEADS, 0],
            kv5(sk), kv5(sv), scv[None], s_c[None], s_n[None], s_mo[None, :, 0, :M_HEADS])
```

```python
import functools

import jax
import jax.numpy as jnp
from jax import lax
from jax.experimental import pallas as pl
from jax.experimental.pallas import tpu as pltpu

F32 = jnp.float32
BF16 = jnp.bfloat16

D_MODEL = 1024
ATT_W = 512
MLSTM_W = 512
HEAD_DIM = 64
N_HEADS = 8
KV_W = 128
WINDOW = 128
BLOCK = 128
M_HEADS = 4
M_HEAD_DIM = 128
CONV_W = 4
N_META = 16
META_PAD = BLOCK - N_META
D_FF = 2816
IN_MAIN = 2304
IN_GATES = 2 * M_HEADS
IN_PAD = IN_MAIN + 128
DEPTH = 1
ALIBI_SLOPES = tuple(2.0 ** (-8.0 * (h + 1) / N_HEADS) for h in range(N_HEADS))
DEEPNORM_ALPHA = (2.0 * DEPTH) ** 0.25
EPS = 1e-5
SEQ_TILE = 16
STEP_BLOCKS = 2
FF_CHUNK = 256
FF_LOOKAHEAD = 3
PROJ_CHUNK = 256
WEIGHT_ROWS = 128
WEIGHT_SLOTS = 4
VMEM_LIMIT = 56 * 1024 * 1024


def _pair_heads(a, axis):
    shape = a.shape
    a = a.reshape(shape[:axis] + (2, 4, HEAD_DIM) + shape[axis + 1:])
    return jnp.swapaxes(a, axis, axis + 1).reshape(shape)


def _dot(a, b):
    return jnp.dot(a, b, preferred_element_type=F32)


def _dot_nt(a, b):
    return lax.dot_general(a, b, (((1,), (1,)), ((), ())), preferred_element_type=F32)


def _dot_tn(a, b):
    return lax.dot_general(a, b, (((0,), (0,)), ((), ())), preferred_element_type=F32)


def _split2(x):
    hi = x.astype(BF16)
    lo = (x - hi.astype(F32)).astype(BF16)
    return hi, lo


def _sigmoid(x):
    return 1.0 / (1.0 + jnp.exp(-x))


def _log_sigmoid(x):
    return -(jnp.maximum(-x, 0.0) + jnp.log1p(jnp.exp(-jnp.abs(x))))


def _layer_norm(x, g, b):
    mu = jnp.mean(x, axis=-1, keepdims=True)
    xc = x - mu
    var = jnp.mean(xc * xc, axis=-1, keepdims=True)
    return xc * lax.rsqrt(var + EPS) * g + b


def _attn_head_norm(att, g):
    lo_half = lax.broadcasted_iota(jnp.int32, (att.shape[0], 128), 1) < HEAD_DIM

    def seg_mean(x):
        s_all = jnp.sum(x, axis=1, keepdims=True)
        s_lo = jnp.sum(jnp.where(lo_half, x, 0.0), axis=1, keepdims=True)
        return jnp.where(lo_half, s_lo, s_all - s_lo) * (1.0 / HEAD_DIM)

    out = []
    for grp in range(att.shape[1] // 128):
        x = att[:, grp * 128:(grp + 1) * 128]
        xc = x - seg_mean(x)
        out.append(xc * lax.rsqrt(seg_mean(xc * xc) + EPS))
    return jnp.concatenate(out, axis=1) * g


def _mlstm_head_norm(hm, g):
    mu = jnp.mean(hm, axis=-1, keepdims=True)
    xc = hm - mu
    var = jnp.mean(xc * xc, axis=-1, keepdims=True)
    return xc * lax.rsqrt(var + EPS) * g


def _softmax_parts(parts, sink):
    sp = [jnp.where(m, s * (HEAD_DIM ** -0.5) + a, -jnp.inf) for s, m, a in parts]
    mx = sink
    for s in sp:
        mx = jnp.maximum(mx, jnp.max(s, axis=1, keepdims=True))
    ps = [jnp.exp(s - mx) for s in sp]
    den = jnp.exp(sink - mx)
    for p in ps:
        den = den + jnp.sum(p, axis=1, keepdims=True)
    return ps, 1.0 / den


def _conv_silu(window, wconv_ref, bconv_ref):
    acc = bconv_ref[...]
    for j in range(CONV_W):
        acc = acc + window(j) * wconv_ref[j:j + 1, :]
    return acc * _sigmoid(acc)


def _gate_columns(gates, gb_ref, row_valid, ltri, extra=None):
    lane = lax.broadcasted_iota(jnp.int32, gates.shape, 1)
    gb = gates + gb_ref[...]
    is_i = lane < M_HEADS
    is_f = (lane >= M_HEADS) & (lane < 2 * M_HEADS)
    logf = _log_sigmoid(gb)
    if row_valid is not None:
        lf = jnp.where(is_f & row_valid, logf, 0.0)
        li = jnp.where(row_valid, gb, -jnp.inf)
    else:
        lf = jnp.where(is_f, logf, 0.0)
        li = gb
    hi_lo = jnp.concatenate(_split2(lf), axis=1)
    bc = _dot(ltri, hi_lo)
    x = jnp.where(is_i, li, bc[:, :128] + bc[:, 128:])
    if extra is None:
        return x, None
    be = _dot(extra, hi_lo)
    return x, be[:, :128] + be[:, 128:]


class _FfnStages:
    N_CHUNKS = D_FF // FF_CHUNK

    def __init__(self, x, ymix, wout_ref, ln1g_ref, ln1b_ref, wg_ref, wu_ref, wd_ref, ln2g_ref, ln2b_ref):
        self.x, self.ymix = x, ymix
        self.wout_ref, self.ln1g_ref, self.ln1b_ref = wout_ref, ln1g_ref, ln1b_ref
        self.wg_ref, self.wu_ref, self.wd_ref = wg_ref, wu_ref, wd_ref
        self.ln2g_ref, self.ln2b_ref = ln2g_ref, ln2b_ref

    def head(self):
        self.x1 = _layer_norm(DEEPNORM_ALPHA * self.x + _dot(self.ymix, self.wout_ref[...]),
                              self.ln1g_ref[...], self.ln1b_ref[...])
        self.x1b = self.x1.astype(BF16)
        self.acc = DEEPNORM_ALPHA * self.x1

    def _gate_up(self, c):
        cs = slice(c * FF_CHUNK, (c + 1) * FF_CHUNK)
        return _dot(self.x1b, self.wg_ref[:, cs]), _dot(self.x1b, self.wu_ref[:, cs])

    def chunk(self, c):
        if c == 0:
            self.gu = {}
        for n in range(c, min(c + FF_LOOKAHEAD, self.N_CHUNKS - 1) + 1):
            if n not in self.gu:
                self.gu[n] = self._gate_up(n)
        g, u = self.gu.pop(c)
        hid = (g * _sigmoid(g) * u).astype(BF16)
        self.acc = self.acc + _dot(hid, self.wd_ref[c * FF_CHUNK:(c + 1) * FF_CHUNK, :])

    def tail(self):
        return _layer_norm(self.acc, self.ln2g_ref[...], self.ln2b_ref[...])

    def run(self):
        self.head()
        for c in range(self.N_CHUNKS):
            self.chunk(c)
        return self.tail()


def _mix_blocks(zc, nb, pad_rows, thr_first, state, cbuf, wconv_ref, bconv_ref, wqk_ref, gb_ref,
                sinks_ref, gattn_ref, gml_ref, hook):
    rows = nb * BLOCK
    kp_lo, kp_hi, vp_lo, vp_hi, c_state, m_state = state
    c_state, m_state = list(c_state), list(m_state)
    lane = lax.broadcasted_iota(jnp.int32, (BLOCK, 128), 1)
    lo_half = lane < HEAD_DIM
    row_valid = None
    if pad_rows:
        row_valid = lax.broadcasted_iota(jnp.int32, (rows, 1), 0) >= pad_rows

    s_all, vmasks = [], []
    for blk in range(nb):
        rs = slice(blk * BLOCK, (blk + 1) * BLOCK)
        kb = zc[rs, ATT_W:ATT_W + KV_W].astype(BF16)
        vb = zc[rs, ATT_W + KV_W:ATT_W + 2 * KV_W].astype(BF16)
        zero_b = jnp.zeros_like(kb)
        k_lo, k_hi = jnp.where(lo_half, kb, zero_b), jnp.where(lo_half, zero_b, kb)
        v_lo, v_hi = jnp.where(lo_half, vb, zero_b), jnp.where(lo_half, zero_b, vb)
        kmask = jnp.concatenate([kp_lo, k_lo, kp_hi, k_hi], axis=0)
        vmasks.append(jnp.concatenate([vp_lo, v_lo, vp_hi, v_hi], axis=0))
        q4 = jnp.concatenate([zc[rs, r * 128:(r + 1) * 128] for r in range(4)], axis=0).astype(BF16)
        s_all.append(_dot_nt(q4, kmask))
        kp_lo, kp_hi, vp_lo, vp_hi = k_lo, k_hi, v_lo, v_hi
    hook("scores")

    c_in = zc[:, 768:1280]
    if row_valid is not None:
        c_in = jnp.where(row_valid, c_in, 0.0)
    cbuf[8:8 + rows, :] = c_in
    c_act = _conv_silu(lambda jj: cbuf[5 + jj:5 + jj + rows, :], wconv_ref, bconv_ref)
    cbuf[0:8, :] = cbuf[rows:rows + 8, :]
    hs = range(M_HEADS)
    ca_b = [c_act[:, h * 128:(h + 1) * 128].astype(BF16) for h in hs]
    qk = [_dot(ca_b[h], wqk_ref[h]) for h in hs]
    qm_b = [qk[h][:, :M_HEAD_DIM].astype(BF16) for h in hs]
    kmt_f = [(qk[h][:, M_HEAD_DIM:] * (M_HEAD_DIM ** -0.5)).T for h in hs]
    hook("front")

    ti = lax.broadcasted_iota(jnp.int32, (rows, rows), 0)
    si = lax.broadcasted_iota(jnp.int32, (rows, rows), 1)
    ltri = jnp.where((si <= ti) & ((si >> 7) == (ti >> 7)), 1.0, 0.0).astype(BF16)
    xg, _ = _gate_columns(zc[:, IN_MAIN:IN_PAD], gb_ref, row_valid, ltri)
    xgt = xg.T
    hook("gates")

    qi = lax.broadcasted_iota(jnp.int32, (BLOCK, BLOCK), 0)
    ks = lax.broadcasted_iota(jnp.int32, (BLOCK, BLOCK), 1)
    own = ks <= qi
    distf = jnp.where(own, qi - ks, WINDOW + qi - ks).astype(F32)
    zero_p = jnp.zeros((BLOCK, BLOCK), BF16)
    att_rows = []
    for blk in range(nb):
        live = None
        if blk == 0 and thr_first is not None:
            live = jnp.where(own, ks + BLOCK, ks) >= thr_first
        p_rows, inv = [], {}
        for r in range(4):
            pr = []
            for c in range(2):
                h = r + 4 * c
                s_blk = s_all[blk][r * 128:(r + 1) * 128, c * 256:(c + 1) * 256]
                sp = jnp.where(own, s_blk[:, BLOCK:], s_blk[:, :BLOCK]) * (HEAD_DIM ** -0.5) \
                    - ALIBI_SLOPES[h] * distf
                if live is not None:
                    sp = jnp.where(live, sp, -jnp.inf)
                sink = sinks_ref[h]
                mx = jnp.maximum(jnp.max(sp, axis=1, keepdims=True), sink)
                p = jnp.exp(sp - mx)
                inv[h] = 1.0 / (jnp.sum(p, axis=1, keepdims=True) + jnp.exp(sink - mx))
                p_b = p.astype(BF16)
                pr += [jnp.where(own, zero_p, p_b), jnp.where(own, p_b, zero_p)]
            p_rows.append(jnp.concatenate(pr, axis=1))
            hook("softmax", blk * 4 + r)
        o_all = _dot(jnp.concatenate(p_rows, axis=0), vmasks[blk])
        att_rows.append(jnp.concatenate(
            [o_all[r * 128:(r + 1) * 128] * jnp.where(lo_half, inv[r], inv[r + 4]) for r in range(4)], axis=1))
        hook("pv", blk)

    ones_col = jnp.where(lane == 0, 1.0, 0.0).astype(BF16)
    tb = lax.broadcasted_iota(jnp.int32, (BLOCK, BLOCK), 0)
    sb = lax.broadcasted_iota(jnp.int32, (BLOCK, BLOCK), 1)
    tri = sb <= tb
    hh_rows = [[] for _ in hs]
    for blk in range(nb):
        rs = slice(blk * BLOCK, (blk + 1) * BLOCK)
        b_c = [xg[rs, 4 + h:5 + h] for h in hs]
        b_r = [xgt[4 + h:5 + h, rs] for h in hs]
        li_r = [xgt[h:h + 1, rs] for h in hs]
        kmt = [kmt_f[h][:, rs] for h in hs]
        v_ext = [jnp.concatenate([zc[rs, 1280 + h * 128:1280 + (h + 1) * 128].astype(BF16), ones_col], axis=1)
                 for h in hs]
        s_qk = [_dot(qm_b[h][rs], kmt[h].astype(BF16)) for h in hs]
        inter = [_dot(qm_b[h][rs], c_state[h].astype(BF16)) for h in hs]
        hook("mlstm_a", blk)
        dmat = [jnp.where(tri, b_c[h] - b_r[h] + li_r[h], -jnp.inf) for h in hs]
        m_inter = [b_c[h] + m_state[h] for h in hs]
        m_t = [jnp.maximum(m_inter[h], jnp.max(dmat[h], axis=1, keepdims=True)) for h in hs]
        w_inter = [jnp.exp(m_inter[h] - m_t[h]) for h in hs]
        sc_b = [(s_qk[h] * jnp.exp(dmat[h] - m_t[h])).astype(BF16) for h in hs]
        hook("mlstm_b", blk)
        nd = [_dot(sc_b[h], v_ext[h]) + w_inter[h] * inter[h] for h in hs]
        for h in hs:
            m_end = m_t[h][BLOCK - 1:BLOCK, :]
            b_last = b_c[h][BLOCK - 1:BLOCK, :]
            decay = jnp.exp(b_last + m_state[h] - m_end)
            wk_r = jnp.exp(b_last - b_r[h] + li_r[h] - m_end)
            c_state[h] = decay * c_state[h] + _dot((kmt[h] * wk_r).astype(BF16), v_ext[h])
            m_state[h] = m_end
            hh_rows[h].append(nd[h][:, :128] / jnp.maximum(jnp.abs(nd[h][:, 128:129]), jnp.exp(-m_t[h])))
        hook("mlstm_c", blk)

    y_att = _attn_head_norm(jnp.concatenate(att_rows, axis=0), gattn_ref[...])
    hook("attnorm")
    y_m = []
    for h in hs:
        hm = _sigmoid(zc[:, 1792 + h * 128:1792 + (h + 1) * 128]) * jnp.concatenate(hh_rows[h], axis=0)
        y_m.append(_mlstm_head_norm(hm, gml_ref[:, h * 128:(h + 1) * 128]))
        hook("headnorm", h)
    ymix = jnp.concatenate([y_att] + y_m, axis=1).astype(BF16)
    return ymix, (kp_lo, kp_hi, vp_lo, vp_hi, c_state, m_state)


def _project(xb, w_in_refs, lo=0, hi=IN_PAD):
    bounds = (0, ATT_W, IN_MAIN, IN_PAD)
    parts = []
    for ref, start, stop in zip(w_in_refs, bounds[:-1], bounds[1:]):
        a, b = max(lo, start), min(hi, stop)
        if a < b:
            parts.append(_dot(xb, ref[:, a - start:b - start]))
    return parts[0] if len(parts) == 1 else jnp.concatenate(parts, axis=1)


def _meta_state(meta_ref, w_inq_ref, w_inr_ref, w_ing_ref, wconv_ref, bconv_ref, wqk_ref, gb_ref,
                sinks_ref, gattn_ref,
                gml_ref, kplo_o, kphi_o, vplo_o, vphi_o, cb_o, ctn_o, mst_o, zc, cbuf):
    zc[...] = _project(meta_ref[...].astype(BF16), (w_inq_ref, w_inr_ref, w_ing_ref))
    cbuf[0:8, :] = jnp.zeros((8, MLSTM_W), F32)
    zb = jnp.zeros((BLOCK, KV_W), BF16)
    state = (zb, zb, zb, zb, [jnp.zeros((M_HEAD_DIM, 2 * M_HEAD_DIM), F32)] * M_HEADS,
             [jnp.zeros((1, 1), F32)] * M_HEADS)
    _, (kp_lo, kp_hi, vp_lo, vp_hi, c_state, m_state) = _mix_blocks(
        zc, 1, META_PAD, BLOCK + META_PAD, state, cbuf, wconv_ref, bconv_ref, wqk_ref, gb_ref,
        sinks_ref, gattn_ref, gml_ref, lambda name, index=0: None)
    kplo_o[...] = kp_lo
    kphi_o[...] = kp_hi
    vplo_o[...] = vp_lo
    vphi_o[...] = vp_hi
    cb_o[...] = cbuf[0:8, :]
    for h in range(M_HEADS):
        ctn_o[h] = c_state[h]
        mst_o[h:h + 1, :] = jnp.broadcast_to(m_state[h], (1, 128))
    mst_o[M_HEADS:8, :] = jnp.zeros((8 - M_HEADS, 128), F32)


def _load_ffn_weights(wout_hbm, wg_hbm, wu_hbm, wd_hbm, wout_v, wg_v, wu_v, wd_v, stage_bufs, sems):
    half = D_FF // 2
    jobs = []
    for src, dst in ((wg_hbm, wg_v), (wu_hbm, wu_v)):
        jobs += [(src, r, WEIGHT_ROWS, c, half, dst, r) for r in range(0, D_MODEL, WEIGHT_ROWS) for c in (0, half)]
    jobs += [(wd_hbm, r, WEIGHT_ROWS, 0, D_MODEL, wd_v, r) for r in range(0, D_FF, WEIGHT_ROWS)]
    for h in range(N_HEADS):
        jobs.append((wout_hbm, h * HEAD_DIM, HEAD_DIM, 0, D_MODEL, wout_v, ((h % 4) * 2 + h // 4) * HEAD_DIM))
    jobs += [(wout_hbm, r, WEIGHT_ROWS, 0, D_MODEL, wout_v, r) for r in range(ATT_W, D_MODEL, WEIGHT_ROWS)]

    windows = []
    for buf in stage_bufs:
        views = [buf.at[s] for s in range(buf.shape[0])] if len(buf.shape) == 3 else [buf]
        for v in views:
            windows += [(v, r, c) for r in range(0, v.shape[0] - WEIGHT_ROWS + 1, WEIGHT_ROWS)
                        for c in range(0, v.shape[1] - half + 1, half)]
    nslot = len(windows)
    assert nslot <= sems.shape[0]

    def window(i, nrows, ncols):
        v, r, c = windows[i % nslot]
        return v.at[pl.ds(r, nrows), pl.ds(c, ncols)]

    def copy(i):
        src, r0, nrows, c0, ncols, _, _ = jobs[i]
        return pltpu.make_async_copy(src.at[pl.ds(r0, nrows), pl.ds(c0, ncols)], window(i, nrows, ncols),
                                     sems.at[i % nslot])

    for i in range(min(nslot - 1, len(jobs))):
        copy(i).start()
    for i, (_, _, nrows, c0, ncols, dst, d0) in enumerate(jobs):
        if i + nslot - 1 < len(jobs):
            copy(i + nslot - 1).start()
        copy(i).wait()
        dst[d0:d0 + nrows, c0:c0 + ncols] = window(i, nrows, ncols)[...].astype(BF16)


def _pair_kernel(xin_ref, xres_ref, meta_ref, xs_ref, ymix_s_ref,
                 w_inq_ref, w_inr_ref, w_ing_ref, wconv_ref, bconv_ref, wqk_ref, gb_ref, sinks_ref,
                 gattn_ref, gml_ref,
                 wout_hbm, ln1g_ref, ln1b_ref, wg_hbm, wu_hbm, wd_hbm, ln2g_ref, ln2b_ref,
                 y_ref, ys_ref, pk_ref, pv_ref, pconv_ref, pc_ref, pn_ref, pm_ref,
                 wout_v, wg_v, wu_v, wd_v, wstage, wsems,
                 kplo_i, kphi_i, vplo_i, vphi_i, cb_i, ctn_i, mst_i,
                 zcur, znext, kvlast, kplo, kphi, vplo, vphi, cbuf, ctn, mst, yprev, *, npair, n_pair_steps):
    t = pl.program_id(0)
    p = lax.rem(t + (npair - 1), npair)

    @pl.when(t == 0)
    def _init_pipeline():
        _load_ffn_weights(wout_hbm, wg_hbm, wu_hbm, wd_hbm, wout_v, wg_v, wu_v, wd_v,
                          (wstage, zcur, znext), wsems)
        zcur[...] = jnp.zeros_like(zcur)
        yprev[...] = jnp.zeros_like(yprev)
        _meta_state(meta_ref, w_inq_ref, w_inr_ref, w_ing_ref, wconv_ref, bconv_ref, wqk_ref, gb_ref,
                    sinks_ref, gattn_ref, gml_ref, kplo_i, kphi_i, vplo_i, vphi_i, cb_i, ctn_i, mst_i,
                    znext.at[pl.ds(0, BLOCK)], cbuf)

    @pl.when((p == 0) | (t == 0))
    def _load_meta_state():
        kplo[...] = kplo_i[...]
        kphi[...] = kphi_i[...]
        vplo[...] = vplo_i[...]
        vphi[...] = vphi_i[...]
        cbuf[0:8, :] = cb_i[...]
        ctn[...] = ctn_i[...]
        mst[...] = mst_i[...]

    ffn_w = (wout_v, ln1g_ref, ln1b_ref, wg_v, wu_v, wd_v, ln2g_ref, ln2b_ref)

    @pl.when(t < n_pair_steps)
    def _pair_step():
        ffn = _FfnStages(xres_ref[...], yprev[...], *ffn_w)
        xb_b = xin_ref[...].astype(BF16)

        n_proj = -(-IN_PAD // PROJ_CHUNK)

        def proj(n):
            lo, hi = n * PROJ_CHUNK, min((n + 1) * PROJ_CHUNK, IN_PAD)
            znext[:, lo:hi] = _project(xb_b, (w_inq_ref, w_inr_ref, w_ing_ref), lo, hi)

        proj_it, ffn_it = iter(range(n_proj)), iter(range(ffn.N_CHUNKS))

        def fill_proj(n):
            for c in [c for _, c in zip(range(n), proj_it)]:
                proj(c)

        def fill_ffn(n):
            for c in [c for _, c in zip(range(n), ffn_it)]:
                ffn.chunk(c)

        out = {}

        def hook(name, index=0):
            if name == "scores":
                fill_proj(2)
                ffn.head()
            elif name == "front":
                fill_proj(1)
            elif name == "gates":
                fill_ffn(1)
            elif name == "softmax":
                if index % 2 == 1:
                    fill_ffn(1)
            elif name in ("pv", "mlstm_a", "mlstm_b"):
                fill_ffn(1)
            elif name == "mlstm_c":
                fill_proj(1)
            elif name == "attnorm":
                fill_ffn(ffn.N_CHUNKS)
                out["y"] = ffn.tail()
                fill_proj(1)
            elif name == "headnorm":
                fill_proj(1)

        state = (kplo[...], kphi[...], vplo[...], vphi[...], [ctn[h] for h in range(M_HEADS)],
                 [mst[h:h + 1, 0:1] for h in range(M_HEADS)])
        thr_first = jnp.where(p == 0, META_PAD, 0)
        kvlast[...] = zcur[(STEP_BLOCKS - 1) * BLOCK:STEP_BLOCKS * BLOCK, ATT_W:ATT_W + 2 * KV_W]
        ymix, (kp_lo, kp_hi, vp_lo, vp_hi, c_state, m_state) = _mix_blocks(
            zcur, STEP_BLOCKS, 0, thr_first, state, cbuf, wconv_ref, bconv_ref, wqk_ref, gb_ref,
            sinks_ref, gattn_ref, gml_ref, hook)
        fill_proj(n_proj)
        y_ref[...] = out["y"]
        kplo[...] = kp_lo
        kphi[...] = kp_hi
        vplo[...] = vp_lo
        vphi[...] = vp_hi
        for h in range(M_HEADS):
            ctn[h] = c_state[h]
            mst[h:h + 1, :] = jnp.broadcast_to(m_state[h], (1, 128))
        yprev[...] = ymix
        zcur[...] = znext[...]

    @pl.when(t >= n_pair_steps)
    def _sample_ffn_step():
        ys_ref[...] = _FfnStages(xs_ref[...], ymix_s_ref[...], *ffn_w).run()

    @pl.when((p == npair - 1) & (t < n_pair_steps))
    def _final():
        pk_ref[...] = kvlast[:, 0:KV_W].T
        pv_ref[...] = kvlast[:, KV_W:2 * KV_W].T
        pconv_ref[...] = cbuf[8 - (CONV_W - 1):8, :]
        pm_ref[...] = mst[...]
        for h in range(M_HEADS):
            c_n = ctn[h].T
            pc_ref[h] = c_n[0:M_HEAD_DIM, :]
            pn_ref[h:h + 1, :] = c_n[M_HEAD_DIM:M_HEAD_DIM + 1, :]


def _sample_kernel(xs_ref, ck_ref, cv_ref, sconv_ref, sc_ref, sn_ref, sm_ref,
                   w_inq_ref, w_inr_ref, w_ing_ref, wconv_ref, bconv_ref, wqk_ref, gb_ref, sinks_ref,
                   gattn_ref, gml_ref,
                   ymix_ref, sk_ref, sv_ref, scv_ref, sco_ref, sno_ref, smo_ref, cext):
    ns, L = SEQ_TILE, 8
    xs = xs_ref[...].reshape(ns * L, D_MODEL)
    z = _project(xs.astype(BF16), (w_inq_ref, w_inr_ref, w_ing_ref))

    lane = lax.broadcasted_iota(jnp.int32, (BLOCK, 128), 1)
    lo_half = lane < HEAD_DIM
    lo3 = lax.broadcasted_iota(jnp.int32, (ns, L, 128), 2) < HEAD_DIM
    ti = lax.broadcasted_iota(jnp.int32, (BLOCK, BLOCK), 0)
    si = lax.broadcasted_iota(jnp.int32, (BLOCK, BLOCK), 1)
    same_seq = (ti >> 3) == (si >> 3)
    tl, sl_ = ti & 7, si & 7
    causal_seq = same_seq & (sl_ <= tl)

    k_new = z[:, ATT_W:ATT_W + KV_W]
    v_new = z[:, ATT_W + KV_W:ATT_W + 2 * KV_W]
    kb, vb = k_new.astype(BF16), v_new.astype(BF16)
    zero_b = jnp.zeros_like(kb)
    kn_mask = jnp.concatenate([jnp.where(lo_half, kb, zero_b), jnp.where(lo_half, zero_b, kb)], axis=0)
    vn_mask = jnp.concatenate([jnp.where(lo_half, vb, zero_b), jnp.where(lo_half, zero_b, vb)], axis=0)
    qg = [z[:, r * 128:(r + 1) * 128] for r in range(4)]
    s_new = _dot_nt(jnp.concatenate(qg, axis=0).astype(BF16), kn_mask)
    qg3 = [q.reshape(ns, L, 128) for q in qg]
    q8 = jnp.concatenate([jnp.where(lo3, q, 0.0) for q in qg3] + [jnp.where(lo3, 0.0, q) for q in qg3],
                         axis=1).astype(BF16)
    s_cache = jnp.einsum('nqd,ndk->nqk', q8, ck_ref[...].astype(BF16), preferred_element_type=F32)

    c3 = z[:, 768:1280].reshape(ns, L, MLSTM_W)
    cext[:, 8 - (CONV_W - 1):8, :] = sconv_ref[...]
    cext[:, 8:16, :] = c3
    scv_ref[...] = c3[:, L - (CONV_W - 1):L, :]
    c_act = _conv_silu(lambda jj: cext[:, 5 + jj:5 + jj + L, :], wconv_ref, bconv_ref)
    c_act = c_act.reshape(BLOCK, MLSTM_W)
    lblk = jnp.where(causal_seq, 1.0, 0.0).astype(BF16)
    lseq = jnp.where(same_seq, 1.0, 0.0).astype(BF16)
    xg, blast = _gate_columns(z[:, IN_MAIN:IN_PAD], gb_ref, None, lblk, extra=lseq)
    xgt = xg.T
    m_rep = jnp.broadcast_to(sm_ref[...], (ns, L, 128)).reshape(BLOCK, 128)
    hs = range(M_HEADS)
    qk = [_dot(c_act[:, h * 128:(h + 1) * 128].astype(BF16), wqk_ref[h]) for h in hs]
    qm = [qk[h][:, :128] for h in hs]
    qm_b = [qm[h].astype(BF16) for h in hs]
    km = [qk[h][:, 128:] * (M_HEAD_DIM ** -0.5) for h in hs]
    km_b = [km[h].astype(BF16) for h in hs]
    vh_b = [z[:, 1280 + h * 128:1280 + (h + 1) * 128].astype(BF16) for h in hs]
    c_old = [sc_ref[:, h] for h in hs]
    s_qk = [_dot_nt(qm_b[h], km_b[h]) for h in hs]
    cq = [_dot_nt(qm_b[h], c_old[h].reshape(ns * 128, 128).astype(BF16)) for h in hs]

    mask_c = si > tl
    dist_c = (WINDOW + tl - si).astype(F32)
    dist_n = (tl - sl_).astype(F32)
    pc_parts, pn_rows, inv = {}, [], {}
    for r in range(4):
        pn = []
        for c in range(2):
            h = r + 4 * c
            sc_c = s_cache[:, c * 32 + r * L:c * 32 + (r + 1) * L, :].reshape(BLOCK, 128)
            sc_n = s_new[r * 128:(r + 1) * 128, c * 128:(c + 1) * 128]
            (p_c, p_n), inv[h] = _softmax_parts(
                [(sc_c, mask_c, -ALIBI_SLOPES[h] * dist_c), (sc_n, causal_seq, -ALIBI_SLOPES[h] * dist_n)],
                sinks_ref[h])
            pc_parts[(c, r)] = p_c.reshape(ns, L, 128)
            pn.append(p_n.astype(BF16))
        pn_rows.append(jnp.concatenate(pn, axis=1))
    p8 = jnp.concatenate([pc_parts[(c, r)] for c in range(2) for r in range(4)], axis=1).astype(BF16)
    o_cache = jnp.einsum('nqk,ndk->nqd', p8, cv_ref[...].astype(BF16), preferred_element_type=F32)
    o_new = _dot(jnp.concatenate(pn_rows, axis=0), vn_mask)
    groups = []
    for r in range(4):
        oa = o_cache[:, r * L:(r + 1) * L, :].reshape(BLOCK, 128)
        ob = o_cache[:, 32 + r * L:32 + (r + 1) * L, :].reshape(BLOCK, 128)
        o = jnp.where(lo_half, oa, ob) + o_new[r * 128:(r + 1) * 128]
        groups.append(o * jnp.where(lo_half, inv[r], inv[r + 4]))
    y_att = _attn_head_norm(jnp.concatenate(groups, axis=1), gattn_ref[...])

    sk_ref[:, 0:WINDOW - L, :] = jnp.swapaxes(ck_ref[...], 1, 2)[:, L:WINDOW, :]
    sk_ref[:, WINDOW - L:WINDOW, :] = k_new.reshape(ns, L, 128)
    sv_ref[:, 0:WINDOW - L, :] = jnp.swapaxes(cv_ref[...], 1, 2)[:, L:WINDOW, :]
    sv_ref[:, WINDOW - L:WINDOW, :] = v_new.reshape(ns, L, 128)

    b_c = [xg[:, 4 + h:5 + h] for h in hs]
    li_c = [xg[:, h:h + 1] for h in hs]
    b_r = [xgt[4 + h:5 + h, :] for h in hs]
    li_r = [xgt[h:h + 1, :] for h in hs]
    bl_c = [blast[:, 4 + h:5 + h] for h in hs]
    m_prev = [m_rep[:, h:h + 1] for h in hs]
    dmat = [jnp.where(causal_seq, b_c[h] - b_r[h] + li_r[h], -jnp.inf) for h in hs]
    m_inter = [b_c[h] + m_prev[h] for h in hs]
    m_t = [jnp.maximum(m_inter[h], jnp.max(dmat[h], axis=1, keepdims=True)) for h in hs]
    w_inter = [jnp.exp(m_inter[h] - m_t[h]) for h in hs]
    sc = [s_qk[h] * jnp.exp(dmat[h] - m_t[h]) for h in hs]
    num_intra = [_dot(sc[h].astype(BF16), vh_b[h]) for h in hs]

    m_end3 = [jnp.broadcast_to(m_t[h], (BLOCK, 128)).reshape(ns, L, 128)[:, L - 1:L, :] for h in hs]
    m_end = [jnp.broadcast_to(m_end3[h], (ns, L, 128)).reshape(BLOCK, 128)[:, 0:1] for h in hs]
    decay = [jnp.exp(bl_c[h] + m_prev[h] - m_end[h]) for h in hs]
    kw = [km[h] * jnp.exp(bl_c[h] - b_c[h] + li_c[h] - m_end[h]) for h in hs]
    seq_of_row = ti >> 3
    lane3 = lax.broadcasted_iota(jnp.int32, (ns, 1, 128), 2)
    m_out = jnp.zeros((ns, 1, 128), F32)
    for h in hs:
        kw_b = kw[h].astype(BF16)
        k_big = jnp.concatenate(
            [jnp.where(seq_of_row == g, kw_b, jnp.zeros_like(kw_b)) for g in range(ns)], axis=1)
        c_up = _dot_tn(vh_b[h], k_big)
        c_up3 = jnp.stack([c_up[:, g * 128:(g + 1) * 128] for g in range(ns)], axis=0)
        decay3 = jnp.broadcast_to(decay[h], (BLOCK, 128)).reshape(ns, L, 128)[:, 0:1, :]
        n_old = sn_ref[:, h:h + 1, :]
        sco_ref[:, h] = decay3 * c_old[h] + c_up3
        sno_ref[:, h:h + 1, :] = decay3 * n_old + jnp.sum(kw[h].reshape(ns, L, 128), axis=1, keepdims=True)
        m_out = jnp.where(lane3 == h, m_end3[h], m_out)
    smo_ref[...] = m_out

    y_m = []
    for h in hs:
        num_inter = jnp.concatenate([cq[h][g * L:(g + 1) * L, g * 128:(g + 1) * 128] for g in range(ns)], axis=0)
        n_rep = jnp.broadcast_to(sn_ref[:, h:h + 1, :], (ns, L, 128)).reshape(BLOCK, 128)
        num = num_intra[h] + w_inter[h] * num_inter
        den = jnp.sum(sc[h], axis=1, keepdims=True) + w_inter[h] * jnp.sum(qm[h] * n_rep, axis=1, keepdims=True)
        hh = num / jnp.maximum(jnp.abs(den), jnp.exp(-m_t[h]))
        hm = _sigmoid(z[:, 1792 + h * 128:1792 + (h + 1) * 128]) * hh
        y_m.append(_mlstm_head_norm(hm, gml_ref[:, h * 128:(h + 1) * 128]))

    ymix_ref[...] = jnp.concatenate([y_att] + y_m, axis=1).astype(BF16)


def _const_spec(shape):
    nd = len(shape)
    return pl.BlockSpec(shape, lambda *_: (0,) * nd, pipeline_mode=pl.Buffered(1))


def _mixer_weight_specs():
    return [
        _const_spec((D_MODEL, ATT_W)),
        _const_spec((D_MODEL, IN_MAIN - ATT_W)),
        _const_spec((D_MODEL, IN_PAD - IN_MAIN)),
        _const_spec((CONV_W, MLSTM_W)),
        _const_spec((1, MLSTM_W)),
        _const_spec((M_HEADS, M_HEAD_DIM, 2 * M_HEAD_DIM)),
        _const_spec((1, 128)),
        pl.BlockSpec(memory_space=pltpu.SMEM),
        _const_spec((1, ATT_W)),
        _const_spec((1, MLSTM_W)),
    ]


def _post_weight_specs(d):
    hbm = pl.BlockSpec(memory_space=pl.ANY)
    return [hbm, _const_spec((1, d)), _const_spec((1, d)), hbm, hbm, hbm, _const_spec((1, d)), _const_spec((1, d))]


def _prompt_layer(x_prompt, meta_blk, x_sample_rows, ymix_sample, mixer_w, post_w):
    b, s, d = x_prompt.shape
    state_shapes = [((BLOCK, KV_W), BF16)] * 4 + [
        ((8, MLSTM_W), F32), ((M_HEADS, M_HEAD_DIM, 2 * M_HEAD_DIM), F32), ((8, 128), F32)]
    rows = STEP_BLOCKS * BLOCK
    npair = s // rows
    total = b * npair
    n_pair_steps = total + 2
    n_sample_steps = x_sample_rows.shape[0] // rows
    steps = n_pair_steps + n_sample_steps

    def block_map(lag):
        def index(t):
            tb = jnp.clip(t - lag, 0, total - 1)
            return (tb // npair, tb % npair, 0)
        return index

    def smap(t):
        return (jnp.clip(t - 1, 0, total - 1) // npair, 0, 0)

    def sample_map(t):
        return (jnp.clip(t - n_pair_steps, 0, n_sample_steps - 1), 0)

    out_shape = (
        jax.ShapeDtypeStruct((b, s, d), F32),
        jax.ShapeDtypeStruct(x_sample_rows.shape, F32),
        jax.ShapeDtypeStruct((b, BLOCK, KV_W), F32),
        jax.ShapeDtypeStruct((b, BLOCK, KV_W), F32),
        jax.ShapeDtypeStruct((b, CONV_W - 1, MLSTM_W), F32),
        jax.ShapeDtypeStruct((b, M_HEADS, M_HEAD_DIM, M_HEAD_DIM), F32),
        jax.ShapeDtypeStruct((b, M_HEADS, M_HEAD_DIM), F32),
        jax.ShapeDtypeStruct((b, 8, 128), F32),
    )
    out_specs = (
        pl.BlockSpec((None, rows, d), block_map(2)),
        pl.BlockSpec((rows, d), sample_map),
        pl.BlockSpec((None, BLOCK, KV_W), smap),
        pl.BlockSpec((None, BLOCK, KV_W), smap),
        pl.BlockSpec((None, CONV_W - 1, MLSTM_W), smap),
        pl.BlockSpec((None, M_HEADS, M_HEAD_DIM, M_HEAD_DIM), lambda t: smap(t) + (0,)),
        pl.BlockSpec((None, M_HEADS, M_HEAD_DIM), smap),
        pl.BlockSpec((None, 8, 128), smap),
    )
    return pl.pallas_call(
        functools.partial(_pair_kernel, npair=npair, n_pair_steps=n_pair_steps),
        grid=(steps,),
        in_specs=[pl.BlockSpec((None, rows, d), block_map(0)), pl.BlockSpec((None, rows, d), block_map(2)),
                  _const_spec((BLOCK, d)), pl.BlockSpec((rows, d), sample_map), pl.BlockSpec((rows, d), sample_map)]
        + _mixer_weight_specs() + _post_weight_specs(d),
        out_specs=out_specs,
        out_shape=out_shape,
        scratch_shapes=[
            pltpu.VMEM((d, d), BF16), pltpu.VMEM((d, D_FF), BF16), pltpu.VMEM((d, D_FF), BF16),
            pltpu.VMEM((D_FF, d), BF16),
            pltpu.VMEM((WEIGHT_SLOTS, WEIGHT_ROWS, D_FF), F32),
            pltpu.SemaphoreType.DMA((2 * WEIGHT_SLOTS + 2 * STEP_BLOCKS,)),
        ] + [pltpu.VMEM(shape, dt) for shape, dt in state_shapes] + [
            pltpu.VMEM((rows, IN_PAD), F32),
            pltpu.VMEM((rows, IN_PAD), F32),
            pltpu.VMEM((BLOCK, 2 * KV_W), F32),
        ] + [pltpu.VMEM((BLOCK, KV_W), BF16)] * 4 + [
            pltpu.VMEM((rows + 8, MLSTM_W), F32),
            pltpu.VMEM((M_HEADS, M_HEAD_DIM, 2 * M_HEAD_DIM), F32),
            pltpu.VMEM((8, 128), F32),
            pltpu.VMEM((rows, d), BF16),
        ],
        compiler_params=pltpu.CompilerParams(
            dimension_semantics=("arbitrary",), vmem_limit_bytes=VMEM_LIMIT),
        name="prompt_layer",
    )(x_prompt, x_prompt, meta_blk, x_sample_rows, ymix_sample, *mixer_w, *post_w)


def _sample_mixer(x_sample, ck, cv, sconv8, s_c, s_n, s_m3, mixer_w):
    n, l, d = x_sample.shape
    t = SEQ_TILE
    m3 = lambda i: (i, 0, 0)
    m4 = lambda i: (i, 0, 0, 0)
    out_shape = (
        jax.ShapeDtypeStruct((n * l, d), BF16),
        jax.ShapeDtypeStruct((n, WINDOW, KV_W), F32),
        jax.ShapeDtypeStruct((n, WINDOW, KV_W), F32),
        jax.ShapeDtypeStruct((n, CONV_W - 1, MLSTM_W), F32),
        jax.ShapeDtypeStruct((n, M_HEADS, M_HEAD_DIM, M_HEAD_DIM), F32),
        jax.ShapeDtypeStruct((n, M_HEADS, M_HEAD_DIM), F32),
        jax.ShapeDtypeStruct((n, 1, 128), F32),
    )
    out_specs = (
        pl.BlockSpec((t * l, d), lambda i: (i, 0)),
        pl.BlockSpec((t, WINDOW, KV_W), m3),
        pl.BlockSpec((t, WINDOW, KV_W), m3),
        pl.BlockSpec((t, CONV_W - 1, MLSTM_W), m3),
        pl.BlockSpec((t, M_HEADS, M_HEAD_DIM, M_HEAD_DIM), m4),
        pl.BlockSpec((t, M_HEADS, M_HEAD_DIM), m3),
        pl.BlockSpec((t, 1, 128), m3),
    )
    in_specs = [
        pl.BlockSpec((t, l, d), m3),
        pl.BlockSpec((t, WINDOW, KV_W), m3),
        pl.BlockSpec((t, WINDOW, KV_W), m3),
        pl.BlockSpec((t, CONV_W - 1, MLSTM_W), m3),
        pl.BlockSpec((t, M_HEADS, M_HEAD_DIM, M_HEAD_DIM), m4),
        pl.BlockSpec((t, M_HEADS, M_HEAD_DIM), m3),
        pl.BlockSpec((t, 1, 128), m3),
    ] + _mixer_weight_specs()
    return pl.pallas_call(
        _sample_kernel,
        grid=(n // t,),
        in_specs=in_specs,
        out_specs=out_specs,
        out_shape=out_shape,
        scratch_shapes=[pltpu.VMEM((t, 16, MLSTM_W), F32)],
        compiler_params=pltpu.CompilerParams(
            dimension_semantics=("arbitrary",), vmem_limit_bytes=VMEM_LIMIT),
        name="sample_mixer",
    )(x_sample, ck, cv, sconv8, s_c, s_n, s_m3, *mixer_w)


def kernel(x_prompt, x_sample, cache_k, cache_v, state_conv, state_C, state_n, state_m, meta_tokens,
           w_in, w_conv, b_conv, w_mq, w_mk, b_i, b_f, attn_sinks, g_attn, g_mlstm, w_out,
           ln1_g, ln1_b, w_gate, w_up, w_down, ln2_g, ln2_b):
    b, s, d = x_prompt.shape
    n, l, _ = x_sample.shape

    w0 = w_in[0]
    w_in_parts = (_pair_heads(w0[:, :ATT_W], 1).astype(BF16), w0[:, ATT_W:IN_MAIN].astype(BF16),
                  jnp.pad(w0[:, IN_MAIN:], ((0, 0), (0, IN_PAD - IN_MAIN - IN_GATES))).astype(BF16))
    wqk = jnp.concatenate([w_mq[0], w_mk[0]], axis=-1).astype(BF16)
    gb = jnp.concatenate([b_i[0], b_f[0], jnp.zeros((128 - IN_GATES,), F32)]).reshape(1, 128)
    mixer_w = w_in_parts + (w_conv[0], b_conv[0].reshape(1, MLSTM_W), wqk, gb, attn_sinks[0],
                            _pair_heads(g_attn[0].reshape(1, ATT_W), 1), g_mlstm[0].reshape(1, MLSTM_W))
    post_w = (w_out[0], ln1_g[0].reshape(1, d), ln1_b[0].reshape(1, d), w_gate[0], w_up[0], w_down[0],
              ln2_g[0].reshape(1, d), ln2_b[0].reshape(1, d))

    key_minor = lambda c: c[0].transpose(0, 2, 3, 1).reshape(n, KV_W, WINDOW)
    s_m3 = jnp.pad(state_m[0], ((0, 0), (0, 128 - M_HEADS))).reshape(n, 1, 128)
    ymix_s, sk, sv, scv, s_c, s_n, s_mo = _sample_mixer(
        x_sample, key_minor(cache_k), key_minor(cache_v), state_conv[0], state_C[0], state_n[0], s_m3, mixer_w)

    meta_blk = jnp.concatenate([jnp.zeros((META_PAD, d), x_prompt.dtype), meta_tokens.astype(x_prompt.dtype)], axis=0)
    y_prompt, y_sample, pk, pv, pconv, p_c, p_n, pm8 = _prompt_layer(
        x_prompt, meta_blk, x_sample.reshape(n * l, d), ymix_s, mixer_w, post_w)

    kv5 = lambda a: a.reshape(1, a.shape[0], WINDOW, 2, HEAD_DIM)
    kv5_key_minor = lambda a: a.reshape(a.shape[0], 2, HEAD_DIM, WINDOW).transpose(0, 3, 1, 2)[None]
    return (y_prompt, y_sample.reshape(n, l, d),
            kv5_key_minor(pk), kv5_key_minor(pv), pconv[None], p_c[None], p_n[None], pm8[None, :, :M_HEADS, 0],
            kv5(sk), kv5(sv), scv[None], s_c[None], s_n[None], s_mo[None, :, 0, :M_HEADS])
```

```python
import functools

import jax
import jax.numpy as jnp
from jax import lax
from jax.experimental import pallas as pl
from jax.experimental.pallas import tpu as pltpu

F32 = jnp.float32
BF16 = jnp.bfloat16

D_MODEL = 1024
ATT_W = 512
MLSTM_W = 512
HEAD_DIM = 64
N_HEADS = 8
KV_W = 128
WINDOW = 128
BLOCK = 128
M_HEADS = 4
M_HEAD_DIM = 128
CONV_W = 4
N_META = 16
META_PAD = BLOCK - N_META
D_FF = 2816
LANES = 128
COL_CONV = ATT_W + 2 * KV_W
COL_VM = COL_CONV + MLSTM_W
COL_O = COL_VM + MLSTM_W
IN_MAIN = COL_O + MLSTM_W
IN_GATES = 2 * M_HEADS
IN_PAD = IN_MAIN + LANES
DEPTH = 1
ALIBI_SLOPES = tuple(2.0 ** (-8.0 * (h + 1) / N_HEADS) for h in range(N_HEADS))
DEEPNORM_ALPHA = (2.0 * DEPTH) ** 0.25
EPS = 1e-5
SEQ_TILE = 16
STEP_BLOCKS = 2
FF_CHUNK = 256
FF_LOOKAHEAD = 3
PROJ_CHUNK = 256
WEIGHT_ROWS = 128
WEIGHT_SLOTS = 4
VMEM_LIMIT = 56 * 1024 * 1024


def _pair_heads(a, axis):
    shape = a.shape
    a = a.reshape(shape[:axis] + (2, 4, HEAD_DIM) + shape[axis + 1:])
    return jnp.swapaxes(a, axis, axis + 1).reshape(shape)


def _dot(a, b):
    return jnp.dot(a, b, preferred_element_type=F32)


def _dot_nt(a, b):
    return lax.dot_general(a, b, (((1,), (1,)), ((), ())), preferred_element_type=F32)


def _dot_tn(a, b):
    return lax.dot_general(a, b, (((0,), (0,)), ((), ())), preferred_element_type=F32)


def _split2(x):
    hi = x.astype(BF16)
    lo = (x - hi.astype(F32)).astype(BF16)
    return hi, lo


def _sigmoid(x):
    return 1.0 / (1.0 + jnp.exp(-x))


def _log_sigmoid(x):
    return -(jnp.maximum(-x, 0.0) + jnp.log1p(jnp.exp(-jnp.abs(x))))


def _layer_norm(x, g, b):
    mu = jnp.mean(x, axis=-1, keepdims=True)
    xc = x - mu
    var = jnp.mean(xc * xc, axis=-1, keepdims=True)
    return xc * lax.rsqrt(var + EPS) * g + b


def _attn_head_norm(att, g):
    lo_half = lax.broadcasted_iota(jnp.int32, (att.shape[0], LANES), 1) < HEAD_DIM

    def seg_mean(x):
        s_all = jnp.sum(x, axis=1, keepdims=True)
        s_lo = jnp.sum(jnp.where(lo_half, x, 0.0), axis=1, keepdims=True)
        return jnp.where(lo_half, s_lo, s_all - s_lo) * (1.0 / HEAD_DIM)

    out = []
    for grp in range(att.shape[1] // LANES):
        x = att[:, grp * LANES:(grp + 1) * LANES]
        xc = x - seg_mean(x)
        out.append(xc * lax.rsqrt(seg_mean(xc * xc) + EPS))
    return jnp.concatenate(out, axis=1) * g


def _mlstm_head_norm(hm, g):
    mu = jnp.mean(hm, axis=-1, keepdims=True)
    xc = hm - mu
    var = jnp.mean(xc * xc, axis=-1, keepdims=True)
    return xc * lax.rsqrt(var + EPS) * g


def _softmax_parts(parts, sink):
    sp = [jnp.where(m, s * (HEAD_DIM ** -0.5) + a, -jnp.inf) for s, m, a in parts]
    mx = sink
    for s in sp:
        mx = jnp.maximum(mx, jnp.max(s, axis=1, keepdims=True))
    ps = [jnp.exp(s - mx) for s in sp]
    den = jnp.exp(sink - mx)
    for p in ps:
        den = den + jnp.sum(p, axis=1, keepdims=True)
    return ps, 1.0 / den


def _conv_silu(window, wconv_ref, bconv_ref):
    acc = bconv_ref[...]
    for j in range(CONV_W):
        acc = acc + window(j) * wconv_ref[j:j + 1, :]
    return acc * _sigmoid(acc)


def _gate_columns(gates, gb_ref, row_valid, ltri, extra=None):
    lane = lax.broadcasted_iota(jnp.int32, gates.shape, 1)
    gb = gates + gb_ref[...]
    is_i = lane < M_HEADS
    is_f = (lane >= M_HEADS) & (lane < 2 * M_HEADS)
    logf = _log_sigmoid(gb)
    if row_valid is not None:
        lf = jnp.where(is_f & row_valid, logf, 0.0)
        li = jnp.where(row_valid, gb, -jnp.inf)
    else:
        lf = jnp.where(is_f, logf, 0.0)
        li = gb
    hi_lo = jnp.concatenate(_split2(lf), axis=1)
    bc = _dot(ltri, hi_lo)
    x = jnp.where(is_i, li, bc[:, :LANES] + bc[:, LANES:])
    if extra is None:
        return x, None
    be = _dot(extra, hi_lo)
    return x, be[:, :LANES] + be[:, LANES:]


class _FfnStages:
    N_CHUNKS = D_FF // FF_CHUNK

    def __init__(self, x, ymix, wout_ref, ln1g_ref, ln1b_ref, wg_ref, wu_ref, wd_ref, ln2g_ref, ln2b_ref):
        self.x, self.ymix = x, ymix
        self.wout_ref, self.ln1g_ref, self.ln1b_ref = wout_ref, ln1g_ref, ln1b_ref
        self.wg_ref, self.wu_ref, self.wd_ref = wg_ref, wu_ref, wd_ref
        self.ln2g_ref, self.ln2b_ref = ln2g_ref, ln2b_ref

    def head(self):
        self.x1 = _layer_norm(DEEPNORM_ALPHA * self.x + _dot(self.ymix, self.wout_ref[...]),
                              self.ln1g_ref[...], self.ln1b_ref[...])
        self.x1b = self.x1.astype(BF16)
        self.acc = DEEPNORM_ALPHA * self.x1

    def _gate_up(self, c):
        cs = slice(c * FF_CHUNK, (c + 1) * FF_CHUNK)
        return _dot(self.x1b, self.wg_ref[:, cs]), _dot(self.x1b, self.wu_ref[:, cs])

    def chunk(self, c):
        if c == 0:
            self.gu = {}
        for n in range(c, min(c + FF_LOOKAHEAD, self.N_CHUNKS - 1) + 1):
            if n not in self.gu:
                self.gu[n] = self._gate_up(n)
        g, u = self.gu.pop(c)
        hid = (g * _sigmoid(g) * u).astype(BF16)
        self.acc = self.acc + _dot(hid, self.wd_ref[c * FF_CHUNK:(c + 1) * FF_CHUNK, :])

    def tail(self):
        return _layer_norm(self.acc, self.ln2g_ref[...], self.ln2b_ref[...])

    def run(self):
        self.head()
        for c in range(self.N_CHUNKS):
            self.chunk(c)
        return self.tail()


def _mix_blocks(zc, nb, pad_rows, thr_first, state, cbuf, wconv_ref, bconv_ref, wqk_ref, gb_ref,
                sinks_ref, gattn_ref, gml_ref, hook):
    rows = nb * BLOCK
    kp_lo, kp_hi, vp_lo, vp_hi, c_state, m_state = state
    c_state, m_state = list(c_state), list(m_state)
    lane = lax.broadcasted_iota(jnp.int32, (BLOCK, LANES), 1)
    lo_half = lane < HEAD_DIM
    row_valid = None
    if pad_rows:
        row_valid = lax.broadcasted_iota(jnp.int32, (rows, 1), 0) >= pad_rows

    s_all, vmasks = [], []
    for blk in range(nb):
        rs = slice(blk * BLOCK, (blk + 1) * BLOCK)
        kb = zc[rs, ATT_W:ATT_W + KV_W].astype(BF16)
        vb = zc[rs, ATT_W + KV_W:ATT_W + 2 * KV_W].astype(BF16)
        zero_b = jnp.zeros_like(kb)
        k_lo, k_hi = jnp.where(lo_half, kb, zero_b), jnp.where(lo_half, zero_b, kb)
        v_lo, v_hi = jnp.where(lo_half, vb, zero_b), jnp.where(lo_half, zero_b, vb)
        kmask = jnp.concatenate([kp_lo, k_lo, kp_hi, k_hi], axis=0)
        vmasks.append(jnp.concatenate([vp_lo, v_lo, vp_hi, v_hi], axis=0))
        q4 = jnp.concatenate([zc[rs, r * LANES:(r + 1) * LANES] for r in range(4)], axis=0).astype(BF16)
        s_all.append(_dot_nt(q4, kmask))
        kp_lo, kp_hi, vp_lo, vp_hi = k_lo, k_hi, v_lo, v_hi
    hook("scores")

    c_in = zc[:, COL_CONV:COL_VM]
    if row_valid is not None:
        c_in = jnp.where(row_valid, c_in, 0.0)
    cbuf[8:8 + rows, :] = c_in
    c_act = _conv_silu(lambda jj: cbuf[5 + jj:5 + jj + rows, :], wconv_ref, bconv_ref)
    cbuf[0:8, :] = cbuf[rows:rows + 8, :]
    hs = range(M_HEADS)
    ca_b = [c_act[:, h * M_HEAD_DIM:(h + 1) * M_HEAD_DIM].astype(BF16) for h in hs]
    qk = [_dot(ca_b[h], wqk_ref[h]) for h in hs]
    qm_b = [qk[h][:, :M_HEAD_DIM].astype(BF16) for h in hs]
    kmt_f = [(qk[h][:, M_HEAD_DIM:] * (M_HEAD_DIM ** -0.5)).T for h in hs]
    hook("front")

    ti = lax.broadcasted_iota(jnp.int32, (rows, rows), 0)
    si = lax.broadcasted_iota(jnp.int32, (rows, rows), 1)
    ltri = jnp.where((si <= ti) & ((si >> 7) == (ti >> 7)), 1.0, 0.0).astype(BF16)
    xg, _ = _gate_columns(zc[:, IN_MAIN:IN_PAD], gb_ref, row_valid, ltri)
    xgt = xg.T
    hook("gates")

    qi = lax.broadcasted_iota(jnp.int32, (BLOCK, BLOCK), 0)
    ks = lax.broadcasted_iota(jnp.int32, (BLOCK, BLOCK), 1)
    own = ks <= qi
    distf = jnp.where(own, qi - ks, WINDOW + qi - ks).astype(F32)
    zero_p = jnp.zeros((BLOCK, BLOCK), BF16)
    att_rows = []
    for blk in range(nb):
        live = None
        if blk == 0 and thr_first is not None:
            live = jnp.where(own, ks + BLOCK, ks) >= thr_first
        p_rows, inv = [], {}
        for r in range(4):
            pr = []
            for c in range(2):
                h = r + 4 * c
                s_blk = s_all[blk][r * LANES:(r + 1) * LANES, c * 256:(c + 1) * 256]
                sp = jnp.where(own, s_blk[:, BLOCK:], s_blk[:, :BLOCK]) * (HEAD_DIM ** -0.5) \
                    - ALIBI_SLOPES[h] * distf
                if live is not None:
                    sp = jnp.where(live, sp, -jnp.inf)
                sink = sinks_ref[h]
                mx = jnp.maximum(jnp.max(sp, axis=1, keepdims=True), sink)
                p = jnp.exp(sp - mx)
                inv[h] = 1.0 / (jnp.sum(p, axis=1, keepdims=True) + jnp.exp(sink - mx))
                p_b = p.astype(BF16)
                pr += [jnp.where(own, zero_p, p_b), jnp.where(own, p_b, zero_p)]
            p_rows.append(jnp.concatenate(pr, axis=1))
            hook("softmax", blk * 4 + r)
        o_all = _dot(jnp.concatenate(p_rows, axis=0), vmasks[blk])
        att_rows.append(jnp.concatenate(
            [o_all[r * LANES:(r + 1) * LANES] * jnp.where(lo_half, inv[r], inv[r + 4]) for r in range(4)], axis=1))
        hook("pv", blk)

    ones_col = jnp.where(lane == 0, 1.0, 0.0).astype(BF16)
    tb = lax.broadcasted_iota(jnp.int32, (BLOCK, BLOCK), 0)
    sb = lax.broadcasted_iota(jnp.int32, (BLOCK, BLOCK), 1)
    tri = sb <= tb
    hh_rows = [[] for _ in hs]
    for blk in range(nb):
        rs = slice(blk * BLOCK, (blk + 1) * BLOCK)
        b_c = [xg[rs, 4 + h:5 + h] for h in hs]
        b_r = [xgt[4 + h:5 + h, rs] for h in hs]
        li_r = [xgt[h:h + 1, rs] for h in hs]
        kmt = [kmt_f[h][:, rs] for h in hs]
        v_ext = [jnp.concatenate([zc[rs, COL_VM + h * M_HEAD_DIM:COL_VM + (h + 1) * M_HEAD_DIM].astype(BF16), ones_col], axis=1)
                 for h in hs]
        s_qk = [_dot(qm_b[h][rs], kmt[h].astype(BF16)) for h in hs]
        inter = [_dot(qm_b[h][rs], c_state[h].astype(BF16)) for h in hs]
        hook("mlstm_a", blk)
        dmat = [jnp.where(tri, b_c[h] - b_r[h] + li_r[h], -jnp.inf) for h in hs]
        m_inter = [b_c[h] + m_state[h] for h in hs]
        m_t = [jnp.maximum(m_inter[h], jnp.max(dmat[h], axis=1, keepdims=True)) for h in hs]
        w_inter = [jnp.exp(m_inter[h] - m_t[h]) for h in hs]
        sc_b = [(s_qk[h] * jnp.exp(dmat[h] - m_t[h])).astype(BF16) for h in hs]
        hook("mlstm_b", blk)
        nd = [_dot(sc_b[h], v_ext[h]) + w_inter[h] * inter[h] for h in hs]
        for h in hs:
            m_end = m_t[h][BLOCK - 1:BLOCK, :]
            b_last = b_c[h][BLOCK - 1:BLOCK, :]
            decay = jnp.exp(b_last + m_state[h] - m_end)
            wk_r = jnp.exp(b_last - b_r[h] + li_r[h] - m_end)
            c_state[h] = decay * c_state[h] + _dot((kmt[h] * wk_r).astype(BF16), v_ext[h])
            m_state[h] = m_end
            num, den = nd[h][:, :M_HEAD_DIM], nd[h][:, M_HEAD_DIM:M_HEAD_DIM + 1]
            hh_rows[h].append(num / jnp.maximum(jnp.abs(den), jnp.exp(-m_t[h])))
        hook("mlstm_c", blk)

    y_att = _attn_head_norm(jnp.concatenate(att_rows, axis=0), gattn_ref[...])
    hook("attnorm")
    y_m = []
    for h in hs:
        hm = _sigmoid(zc[:, COL_O + h * M_HEAD_DIM:COL_O + (h + 1) * M_HEAD_DIM]) * jnp.concatenate(hh_rows[h], axis=0)
        y_m.append(_mlstm_head_norm(hm, gml_ref[:, h * M_HEAD_DIM:(h + 1) * M_HEAD_DIM]))
        hook("headnorm", h)
    ymix = jnp.concatenate([y_att] + y_m, axis=1).astype(BF16)
    return ymix, (kp_lo, kp_hi, vp_lo, vp_hi, c_state, m_state)


def _project(xb, w_in_refs, lo=0, hi=IN_PAD):
    bounds = (0, ATT_W, IN_MAIN, IN_PAD)
    parts = []
    for ref, start, stop in zip(w_in_refs, bounds[:-1], bounds[1:]):
        a, b = max(lo, start), min(hi, stop)
        if a < b:
            parts.append(_dot(xb, ref[:, a - start:b - start]))
    return parts[0] if len(parts) == 1 else jnp.concatenate(parts, axis=1)


def _meta_state(meta_ref, w_inq_ref, w_inr_ref, w_ing_ref, wconv_ref, bconv_ref, wqk_ref, gb_ref,
                sinks_ref, gattn_ref,
                gml_ref, kplo_o, kphi_o, vplo_o, vphi_o, cb_o, ctn_o, mst_o, zc, cbuf):
    zc[...] = _project(meta_ref[...].astype(BF16), (w_inq_ref, w_inr_ref, w_ing_ref))
    cbuf[0:8, :] = jnp.zeros((8, MLSTM_W), F32)
    zb = jnp.zeros((BLOCK, KV_W), BF16)
    state = (zb, zb, zb, zb, [jnp.zeros((M_HEAD_DIM, 2 * M_HEAD_DIM), F32)] * M_HEADS,
             [jnp.zeros((1, 1), F32)] * M_HEADS)
    _, (kp_lo, kp_hi, vp_lo, vp_hi, c_state, m_state) = _mix_blocks(
        zc, 1, META_PAD, BLOCK + META_PAD, state, cbuf, wconv_ref, bconv_ref, wqk_ref, gb_ref,
        sinks_ref, gattn_ref, gml_ref, lambda name, index=0: None)
    kplo_o[...] = kp_lo
    kphi_o[...] = kp_hi
    vplo_o[...] = vp_lo
    vphi_o[...] = vp_hi
    cb_o[...] = cbuf[0:8, :]
    for h in range(M_HEADS):
        ctn_o[h] = c_state[h]
        mst_o[h:h + 1, :] = jnp.broadcast_to(m_state[h], (1, LANES))
    mst_o[M_HEADS:8, :] = jnp.zeros((8 - M_HEADS, LANES), F32)


def _load_ffn_weights(wout_hbm, wg_hbm, wu_hbm, wd_hbm, wout_v, wg_v, wu_v, wd_v, stage_bufs, sems):
    half = D_FF // 2
    jobs = []
    for src, dst in ((wg_hbm, wg_v), (wu_hbm, wu_v)):
        jobs += [(src, r, WEIGHT_ROWS, c, half, dst, r) for r in range(0, D_MODEL, WEIGHT_ROWS) for c in (0, half)]
    jobs += [(wd_hbm, r, WEIGHT_ROWS, 0, D_MODEL, wd_v, r) for r in range(0, D_FF, WEIGHT_ROWS)]
    for h in range(N_HEADS):
        jobs.append((wout_hbm, h * HEAD_DIM, HEAD_DIM, 0, D_MODEL, wout_v, ((h % 4) * 2 + h // 4) * HEAD_DIM))
    jobs += [(wout_hbm, r, WEIGHT_ROWS, 0, D_MODEL, wout_v, r) for r in range(ATT_W, D_MODEL, WEIGHT_ROWS)]

    windows = []
    for buf in stage_bufs:
        views = [buf.at[s] for s in range(buf.shape[0])] if len(buf.shape) == 3 else [buf]
        for v in views:
            windows += [(v, r, c) for r in range(0, v.shape[0] - WEIGHT_ROWS + 1, WEIGHT_ROWS)
                        for c in range(0, v.shape[1] - half + 1, half)]
    nslot = len(windows)
    assert nslot <= sems.shape[0]

    def window(i, nrows, ncols):
        v, r, c = windows[i % nslot]
        return v.at[pl.ds(r, nrows), pl.ds(c, ncols)]

    def copy(i):
        src, r0, nrows, c0, ncols, _, _ = jobs[i]
        return pltpu.make_async_copy(src.at[pl.ds(r0, nrows), pl.ds(c0, ncols)], window(i, nrows, ncols),
                                     sems.at[i % nslot])

    for i in range(min(nslot - 1, len(jobs))):
        copy(i).start()
    for i, (_, _, nrows, c0, ncols, dst, d0) in enumerate(jobs):
        if i + nslot - 1 < len(jobs):
            copy(i + nslot - 1).start()
        copy(i).wait()
        dst[d0:d0 + nrows, c0:c0 + ncols] = window(i, nrows, ncols)[...].astype(BF16)


def _pair_kernel(xin_ref, xres_ref, meta_ref, xs_ref, ymix_s_ref,
                 w_inq_ref, w_inr_ref, w_ing_ref, wconv_ref, bconv_ref, wqk_ref, gb_ref, sinks_ref,
                 gattn_ref, gml_ref,
                 wout_hbm, ln1g_ref, ln1b_ref, wg_hbm, wu_hbm, wd_hbm, ln2g_ref, ln2b_ref,
                 y_ref, ys_ref, pk_ref, pv_ref, pconv_ref, pc_ref, pn_ref, pm_ref,
                 wout_v, wg_v, wu_v, wd_v, wstage, wsems,
                 kplo_i, kphi_i, vplo_i, vphi_i, cb_i, ctn_i, mst_i,
                 zcur, znext, kvlast, kplo, kphi, vplo, vphi, cbuf, ctn, mst, yprev, *, npair, n_pair_steps):
    t = pl.program_id(0)
    p = lax.rem(t + (npair - 1), npair)

    @pl.when(t == 0)
    def _init_pipeline():
        _load_ffn_weights(wout_hbm, wg_hbm, wu_hbm, wd_hbm, wout_v, wg_v, wu_v, wd_v,
                          (wstage, zcur, znext), wsems)
        zcur[...] = jnp.zeros_like(zcur)
        yprev[...] = jnp.zeros_like(yprev)
        _meta_state(meta_ref, w_inq_ref, w_inr_ref, w_ing_ref, wconv_ref, bconv_ref, wqk_ref, gb_ref,
                    sinks_ref, gattn_ref, gml_ref, kplo_i, kphi_i, vplo_i, vphi_i, cb_i, ctn_i, mst_i,
                    znext.at[pl.ds(0, BLOCK)], cbuf)

    @pl.when((p == 0) | (t == 0))
    def _load_meta_state():
        kplo[...] = kplo_i[...]
        kphi[...] = kphi_i[...]
        vplo[...] = vplo_i[...]
        vphi[...] = vphi_i[...]
        cbuf[0:8, :] = cb_i[...]
        ctn[...] = ctn_i[...]
        mst[...] = mst_i[...]

    ffn_w = (wout_v, ln1g_ref, ln1b_ref, wg_v, wu_v, wd_v, ln2g_ref, ln2b_ref)

    @pl.when(t < n_pair_steps)
    def _pair_step():
        ffn = _FfnStages(xres_ref[...], yprev[...], *ffn_w)
        xb_b = xin_ref[...].astype(BF16)

        n_proj = -(-IN_PAD // PROJ_CHUNK)

        def proj(n):
            lo, hi = n * PROJ_CHUNK, min((n + 1) * PROJ_CHUNK, IN_PAD)
            znext[:, lo:hi] = _project(xb_b, (w_inq_ref, w_inr_ref, w_ing_ref), lo, hi)

        proj_it, ffn_it = iter(range(n_proj)), iter(range(ffn.N_CHUNKS))

        def fill_proj(n):
            for c in [c for _, c in zip(range(n), proj_it)]:
                proj(c)

        def fill_ffn(n):
            for c in [c for _, c in zip(range(n), ffn_it)]:
                ffn.chunk(c)

        out = {}

        def hook(name, index=0):
            if name == "scores":
                fill_proj(2)
                ffn.head()
            elif name == "front":
                fill_proj(1)
            elif name == "gates":
                fill_ffn(1)
            elif name == "softmax":
                if index % 2 == 1:
                    fill_ffn(1)
            elif name in ("pv", "mlstm_a", "mlstm_b"):
                fill_ffn(1)
            elif name == "mlstm_c":
                fill_proj(1)
            elif name == "attnorm":
                fill_ffn(ffn.N_CHUNKS)
                out["y"] = ffn.tail()
                fill_proj(1)
            elif name == "headnorm":
                fill_proj(1)

        state = (kplo[...], kphi[...], vplo[...], vphi[...], [ctn[h] for h in range(M_HEADS)],
                 [mst[h:h + 1, 0:1] for h in range(M_HEADS)])
        thr_first = jnp.where(p == 0, META_PAD, 0)
        kvlast[...] = zcur[(STEP_BLOCKS - 1) * BLOCK:STEP_BLOCKS * BLOCK, ATT_W:ATT_W + 2 * KV_W]
        ymix, (kp_lo, kp_hi, vp_lo, vp_hi, c_state, m_state) = _mix_blocks(
            zcur, STEP_BLOCKS, 0, thr_first, state, cbuf, wconv_ref, bconv_ref, wqk_ref, gb_ref,
            sinks_ref, gattn_ref, gml_ref, hook)
        fill_proj(n_proj)
        y_ref[...] = out["y"]
        kplo[...] = kp_lo
        kphi[...] = kp_hi
        vplo[...] = vp_lo
        vphi[...] = vp_hi
        for h in range(M_HEADS):
            ctn[h] = c_state[h]
            mst[h:h + 1, :] = jnp.broadcast_to(m_state[h], (1, LANES))
        yprev[...] = ymix
        zcur[...] = znext[...]

    @pl.when(t >= n_pair_steps)
    def _sample_ffn_step():
        ys_ref[...] = _FfnStages(xs_ref[...], ymix_s_ref[...], *ffn_w).run()

    @pl.when((p == npair - 1) & (t < n_pair_steps))
    def _final():
        pk_ref[...] = kvlast[:, 0:KV_W].T
        pv_ref[...] = kvlast[:, KV_W:2 * KV_W].T
        pconv_ref[...] = cbuf[8 - (CONV_W - 1):8, :]
        pm_ref[...] = mst[...]
        for h in range(M_HEADS):
            c_n = ctn[h].T
            pc_ref[h] = c_n[0:M_HEAD_DIM, :]
            pn_ref[h:h + 1, :] = c_n[M_HEAD_DIM:M_HEAD_DIM + 1, :]


def _sample_kernel(xs_ref, ck_ref, cv_ref, sconv_ref, sc_ref, sn_ref, sm_ref,
                   w_inq_ref, w_inr_ref, w_ing_ref, wconv_ref, bconv_ref, wqk_ref, gb_ref, sinks_ref,
                   gattn_ref, gml_ref,
                   ymix_ref, sk_ref, sv_ref, scv_ref, sco_ref, sno_ref, smo_ref, cext):
    ns, L = SEQ_TILE, 8
    xs = xs_ref[...].reshape(ns * L, D_MODEL)
    z = _project(xs.astype(BF16), (w_inq_ref, w_inr_ref, w_ing_ref))

    lane = lax.broadcasted_iota(jnp.int32, (BLOCK, LANES), 1)
    lo_half = lane < HEAD_DIM
    lo3 = lax.broadcasted_iota(jnp.int32, (ns, L, LANES), 2) < HEAD_DIM
    ti = lax.broadcasted_iota(jnp.int32, (BLOCK, BLOCK), 0)
    si = lax.broadcasted_iota(jnp.int32, (BLOCK, BLOCK), 1)
    same_seq = (ti >> 3) == (si >> 3)
    tl, sl_ = ti & 7, si & 7
    causal_seq = same_seq & (sl_ <= tl)

    k_new = z[:, ATT_W:ATT_W + KV_W]
    v_new = z[:, ATT_W + KV_W:ATT_W + 2 * KV_W]
    kb, vb = k_new.astype(BF16), v_new.astype(BF16)
    zero_b = jnp.zeros_like(kb)
    kn_mask = jnp.concatenate([jnp.where(lo_half, kb, zero_b), jnp.where(lo_half, zero_b, kb)], axis=0)
    vn_mask = jnp.concatenate([jnp.where(lo_half, vb, zero_b), jnp.where(lo_half, zero_b, vb)], axis=0)
    qg = [z[:, r * LANES:(r + 1) * LANES] for r in range(4)]
    s_new = _dot_nt(jnp.concatenate(qg, axis=0).astype(BF16), kn_mask)
    qg3 = [q.reshape(ns, L, LANES) for q in qg]
    q8 = jnp.concatenate([jnp.where(lo3, q, 0.0) for q in qg3] + [jnp.where(lo3, 0.0, q) for q in qg3],
                         axis=1).astype(BF16)
    s_cache = jnp.einsum('nqd,ndk->nqk', q8, ck_ref[...].astype(BF16), preferred_element_type=F32)

    c3 = z[:, COL_CONV:COL_VM].reshape(ns, L, MLSTM_W)
    cext[:, 8 - (CONV_W - 1):8, :] = sconv_ref[...]
    cext[:, 8:16, :] = c3
    scv_ref[...] = c3[:, L - (CONV_W - 1):L, :]
    c_act = _conv_silu(lambda jj: cext[:, 5 + jj:5 + jj + L, :], wconv_ref, bconv_ref)
    c_act = c_act.reshape(BLOCK, MLSTM_W)
    lblk = jnp.where(causal_seq, 1.0, 0.0).astype(BF16)
    lseq = jnp.where(same_seq, 1.0, 0.0).astype(BF16)
    xg, blast = _gate_columns(z[:, IN_MAIN:IN_PAD], gb_ref, None, lblk, extra=lseq)
    xgt = xg.T
    m_rep = jnp.broadcast_to(sm_ref[...], (ns, L, LANES)).reshape(BLOCK, LANES)
    hs = range(M_HEADS)
    qk = [_dot(c_act[:, h * M_HEAD_DIM:(h + 1) * M_HEAD_DIM].astype(BF16), wqk_ref[h]) for h in hs]
    qm = [qk[h][:, :M_HEAD_DIM] for h in hs]
    qm_b = [qm[h].astype(BF16) for h in hs]
    km = [qk[h][:, M_HEAD_DIM:] * (M_HEAD_DIM ** -0.5) for h in hs]
    km_b = [km[h].astype(BF16) for h in hs]
    vh_b = [z[:, COL_VM + h * M_HEAD_DIM:COL_VM + (h + 1) * M_HEAD_DIM].astype(BF16) for h in hs]
    c_old = [sc_ref[:, h] for h in hs]
    s_qk = [_dot_nt(qm_b[h], km_b[h]) for h in hs]
    cq = [_dot_nt(qm_b[h], c_old[h].reshape(ns * 128, 128).astype(BF16)) for h in hs]

    mask_c = si > tl
    dist_c = (WINDOW + tl - si).astype(F32)
    dist_n = (tl - sl_).astype(F32)
    pc_parts, pn_rows, inv = {}, [], {}
    for r in range(4):
        pn = []
        for c in range(2):
            h = r + 4 * c
            sc_c = s_cache[:, c * 32 + r * L:c * 32 + (r + 1) * L, :].reshape(BLOCK, LANES)
            sc_n = s_new[r * LANES:(r + 1) * LANES, c * BLOCK:(c + 1) * BLOCK]
            (p_c, p_n), inv[h] = _softmax_parts(
                [(sc_c, mask_c, -ALIBI_SLOPES[h] * dist_c), (sc_n, causal_seq, -ALIBI_SLOPES[h] * dist_n)],
                sinks_ref[h])
            pc_parts[(c, r)] = p_c.reshape(ns, L, LANES)
            pn.append(p_n.astype(BF16))
        pn_rows.append(jnp.concatenate(pn, axis=1))
    p8 = jnp.concatenate([pc_parts[(c, r)] for c in range(2) for r in range(4)], axis=1).astype(BF16)
    o_cache = jnp.einsum('nqk,ndk->nqd', p8, cv_ref[...].astype(BF16), preferred_element_type=F32)
    o_new = _dot(jnp.concatenate(pn_rows, axis=0), vn_mask)
    groups = []
    for r in range(4):
        oa = o_cache[:, r * L:(r + 1) * L, :].reshape(BLOCK, LANES)
        ob = o_cache[:, 32 + r * L:32 + (r + 1) * L, :].reshape(BLOCK, LANES)
        o = jnp.where(lo_half, oa, ob) + o_new[r * LANES:(r + 1) * LANES]
        groups.append(o * jnp.where(lo_half, inv[r], inv[r + 4]))
    y_att = _attn_head_norm(jnp.concatenate(groups, axis=1), gattn_ref[...])

    sk_ref[:, 0:WINDOW - L, :] = jnp.swapaxes(ck_ref[...], 1, 2)[:, L:WINDOW, :]
    sk_ref[:, WINDOW - L:WINDOW, :] = k_new.reshape(ns, L, LANES)
    sv_ref[:, 0:WINDOW - L, :] = jnp.swapaxes(cv_ref[...], 1, 2)[:, L:WINDOW, :]
    sv_ref[:, WINDOW - L:WINDOW, :] = v_new.reshape(ns, L, LANES)

    b_c = [xg[:, 4 + h:5 + h] for h in hs]
    li_c = [xg[:, h:h + 1] for h in hs]
    b_r = [xgt[4 + h:5 + h, :] for h in hs]
    li_r = [xgt[h:h + 1, :] for h in hs]
    bl_c = [blast[:, 4 + h:5 + h] for h in hs]
    m_prev = [m_rep[:, h:h + 1] for h in hs]
    dmat = [jnp.where(causal_seq, b_c[h] - b_r[h] + li_r[h], -jnp.inf) for h in hs]
    m_inter = [b_c[h] + m_prev[h] for h in hs]
    m_t = [jnp.maximum(m_inter[h], jnp.max(dmat[h], axis=1, keepdims=True)) for h in hs]
    w_inter = [jnp.exp(m_inter[h] - m_t[h]) for h in hs]
    sc = [s_qk[h] * jnp.exp(dmat[h] - m_t[h]) for h in hs]
    num_intra = [_dot(sc[h].astype(BF16), vh_b[h]) for h in hs]

    m_end3 = [jnp.broadcast_to(m_t[h], (BLOCK, LANES)).reshape(ns, L, LANES)[:, L - 1:L, :] for h in hs]
    m_end = [jnp.broadcast_to(m_end3[h], (ns, L, LANES)).reshape(BLOCK, LANES)[:, 0:1] for h in hs]
    decay = [jnp.exp(bl_c[h] + m_prev[h] - m_end[h]) for h in hs]
    kw = [km[h] * jnp.exp(bl_c[h] - b_c[h] + li_c[h] - m_end[h]) for h in hs]
    seq_of_row = ti >> 3
    lane3 = lax.broadcasted_iota(jnp.int32, (ns, 1, LANES), 2)
    m_out = jnp.zeros((ns, 1, LANES), F32)
    for h in hs:
        kw_b = kw[h].astype(BF16)
        k_big = jnp.concatenate(
            [jnp.where(seq_of_row == g, kw_b, jnp.zeros_like(kw_b)) for g in range(ns)], axis=1)
        c_up = _dot_tn(vh_b[h], k_big)
        c_up3 = jnp.stack([c_up[:, g * M_HEAD_DIM:(g + 1) * M_HEAD_DIM] for g in range(ns)], axis=0)
        decay3 = jnp.broadcast_to(decay[h], (BLOCK, LANES)).reshape(ns, L, LANES)[:, 0:1, :]
        n_old = sn_ref[:, h:h + 1, :]
        sco_ref[:, h] = decay3 * c_old[h] + c_up3
        sno_ref[:, h:h + 1, :] = decay3 * n_old + jnp.sum(kw[h].reshape(ns, L, LANES), axis=1, keepdims=True)
        m_out = jnp.where(lane3 == h, m_end3[h], m_out)
    smo_ref[...] = m_out

    y_m = []
    for h in hs:
        num_inter = jnp.concatenate([cq[h][g * L:(g + 1) * L, g * M_HEAD_DIM:(g + 1) * M_HEAD_DIM] for g in range(ns)], axis=0)
        n_rep = jnp.broadcast_to(sn_ref[:, h:h + 1, :], (ns, L, LANES)).reshape(BLOCK, LANES)
        num = num_intra[h] + w_inter[h] * num_inter
        den = jnp.sum(sc[h], axis=1, keepdims=True) + w_inter[h] * jnp.sum(qm[h] * n_rep, axis=1, keepdims=True)
        hh = num / jnp.maximum(jnp.abs(den), jnp.exp(-m_t[h]))
        hm = _sigmoid(z[:, COL_O + h * M_HEAD_DIM:COL_O + (h + 1) * M_HEAD_DIM]) * hh
        y_m.append(_mlstm_head_norm(hm, gml_ref[:, h * M_HEAD_DIM:(h + 1) * M_HEAD_DIM]))

    ymix_ref[...] = jnp.concatenate([y_att] + y_m, axis=1).astype(BF16)


def _const_spec(shape):
    nd = len(shape)
    return pl.BlockSpec(shape, lambda *_: (0,) * nd, pipeline_mode=pl.Buffered(1))


def _mixer_weight_specs():
    return [
        _const_spec((D_MODEL, ATT_W)),
        _const_spec((D_MODEL, IN_MAIN - ATT_W)),
        _const_spec((D_MODEL, IN_PAD - IN_MAIN)),
        _const_spec((CONV_W, MLSTM_W)),
        _const_spec((1, MLSTM_W)),
        _const_spec((M_HEADS, M_HEAD_DIM, 2 * M_HEAD_DIM)),
        _const_spec((1, LANES)),
        pl.BlockSpec(memory_space=pltpu.SMEM),
        _const_spec((1, ATT_W)),
        _const_spec((1, MLSTM_W)),
    ]


def _post_weight_specs(d):
    hbm = pl.BlockSpec(memory_space=pl.ANY)
    return [hbm, _const_spec((1, d)), _const_spec((1, d)), hbm, hbm, hbm, _const_spec((1, d)), _const_spec((1, d))]


def _prompt_layer(x_prompt, meta_blk, x_sample_rows, ymix_sample, mixer_w, post_w):
    b, s, d = x_prompt.shape
    state_shapes = [((BLOCK, KV_W), BF16)] * 4 + [
        ((8, MLSTM_W), F32), ((M_HEADS, M_HEAD_DIM, 2 * M_HEAD_DIM), F32), ((8, LANES), F32)]
    rows = STEP_BLOCKS * BLOCK
    npair = s // rows
    total = b * npair
    n_pair_steps = total + 2
    n_sample_steps = x_sample_rows.shape[0] // rows
    steps = n_pair_steps + n_sample_steps

    def block_map(lag):
        def index(t):
            tb = jnp.clip(t - lag, 0, total - 1)
            return (tb // npair, tb % npair, 0)
        return index

    def smap(t):
        return (jnp.clip(t - 1, 0, total - 1) // npair, 0, 0)

    def sample_map(t):
        return (jnp.clip(t - n_pair_steps, 0, n_sample_steps - 1), 0)

    out_shape = (
        jax.ShapeDtypeStruct((b, s, d), F32),
        jax.ShapeDtypeStruct(x_sample_rows.shape, F32),
        jax.ShapeDtypeStruct((b, BLOCK, KV_W), F32),
        jax.ShapeDtypeStruct((b, BLOCK, KV_W), F32),
        jax.ShapeDtypeStruct((b, CONV_W - 1, MLSTM_W), F32),
        jax.ShapeDtypeStruct((b, M_HEADS, M_HEAD_DIM, M_HEAD_DIM), F32),
        jax.ShapeDtypeStruct((b, M_HEADS, M_HEAD_DIM), F32),
        jax.ShapeDtypeStruct((b, 8, LANES), F32),
    )
    out_specs = (
        pl.BlockSpec((None, rows, d), block_map(2)),
        pl.BlockSpec((rows, d), sample_map),
        pl.BlockSpec((None, BLOCK, KV_W), smap),
        pl.BlockSpec((None, BLOCK, KV_W), smap),
        pl.BlockSpec((None, CONV_W - 1, MLSTM_W), smap),
        pl.BlockSpec((None, M_HEADS, M_HEAD_DIM, M_HEAD_DIM), lambda t: smap(t) + (0,)),
        pl.BlockSpec((None, M_HEADS, M_HEAD_DIM), smap),
        pl.BlockSpec((None, 8, LANES), smap),
    )
    return pl.pallas_call(
        functools.partial(_pair_kernel, npair=npair, n_pair_steps=n_pair_steps),
        grid=(steps,),
        in_specs=[pl.BlockSpec((None, rows, d), block_map(0)), pl.BlockSpec((None, rows, d), block_map(2)),
                  _const_spec((BLOCK, d)), pl.BlockSpec((rows, d), sample_map), pl.BlockSpec((rows, d), sample_map)]
        + _mixer_weight_specs() + _post_weight_specs(d),
        out_specs=out_specs,
        out_shape=out_shape,
        scratch_shapes=[
            pltpu.VMEM((d, d), BF16), pltpu.VMEM((d, D_FF), BF16), pltpu.VMEM((d, D_FF), BF16),
            pltpu.VMEM((D_FF, d), BF16),
            pltpu.VMEM((WEIGHT_SLOTS, WEIGHT_ROWS, D_FF), F32),
            pltpu.SemaphoreType.DMA((2 * WEIGHT_SLOTS + 2 * STEP_BLOCKS,)),
        ] + [pltpu.VMEM(shape, dt) for shape, dt in state_shapes] + [
            pltpu.VMEM((rows, IN_PAD), F32),
            pltpu.VMEM((rows, IN_PAD), F32),
            pltpu.VMEM((BLOCK, 2 * KV_W), F32),
        ] + [pltpu.VMEM((BLOCK, KV_W), BF16)] * 4 + [
            pltpu.VMEM((rows + 8, MLSTM_W), F32),
            pltpu.VMEM((M_HEADS, M_HEAD_DIM, 2 * M_HEAD_DIM), F32),
            pltpu.VMEM((8, LANES), F32),
            pltpu.VMEM((rows, d), BF16),
        ],
        compiler_params=pltpu.CompilerParams(
            dimension_semantics=("arbitrary",), vmem_limit_bytes=VMEM_LIMIT),
        name="prompt_layer",
    )(x_prompt, x_prompt, meta_blk, x_sample_rows, ymix_sample, *mixer_w, *post_w)


def _sample_mixer(x_sample, ck, cv, s_conv, s_c, s_n, s_m3, mixer_w):
    n, l, d = x_sample.shape
    t = SEQ_TILE
    m3 = lambda i: (i, 0, 0)
    m4 = lambda i: (i, 0, 0, 0)
    out_shape = (
        jax.ShapeDtypeStruct((n * l, d), BF16),
        jax.ShapeDtypeStruct((n, WINDOW, KV_W), F32),
        jax.ShapeDtypeStruct((n, WINDOW, KV_W), F32),
        jax.ShapeDtypeStruct((n, CONV_W - 1, MLSTM_W), F32),
        jax.ShapeDtypeStruct((n, M_HEADS, M_HEAD_DIM, M_HEAD_DIM), F32),
        jax.ShapeDtypeStruct((n, M_HEADS, M_HEAD_DIM), F32),
        jax.ShapeDtypeStruct((n, 1, LANES), F32),
    )
    out_specs = (
        pl.BlockSpec((t * l, d), lambda i: (i, 0)),
        pl.BlockSpec((t, WINDOW, KV_W), m3),
        pl.BlockSpec((t, WINDOW, KV_W), m3),
        pl.BlockSpec((t, CONV_W - 1, MLSTM_W), m3),
        pl.BlockSpec((t, M_HEADS, M_HEAD_DIM, M_HEAD_DIM), m4),
        pl.BlockSpec((t, M_HEADS, M_HEAD_DIM), m3),
        pl.BlockSpec((t, 1, LANES), m3),
    )
    in_specs = [
        pl.BlockSpec((t, l, d), m3),
        pl.BlockSpec((t, WINDOW, KV_W), m3),
        pl.BlockSpec((t, WINDOW, KV_W), m3),
        pl.BlockSpec((t, CONV_W - 1, MLSTM_W), m3),
        pl.BlockSpec((t, M_HEADS, M_HEAD_DIM, M_HEAD_DIM), m4),
        pl.BlockSpec((t, M_HEADS, M_HEAD_DIM), m3),
        pl.BlockSpec((t, 1, LANES), m3),
    ] + _mixer_weight_specs()
    return pl.pallas_call(
        _sample_kernel,
        grid=(n // t,),
        in_specs=in_specs,
        out_specs=out_specs,
        out_shape=out_shape,
        scratch_shapes=[pltpu.VMEM((t, 16, MLSTM_W), F32)],
        compiler_params=pltpu.CompilerParams(
            dimension_semantics=("arbitrary",), vmem_limit_bytes=VMEM_LIMIT),
        name="sample_mixer",
    )(x_sample, ck, cv, s_conv, s_c, s_n, s_m3, *mixer_w)


def kernel(x_prompt, x_sample, cache_k, cache_v, state_conv, state_C, state_n, state_m, meta_tokens,
           w_in, w_conv, b_conv, w_mq, w_mk, b_i, b_f, attn_sinks, g_attn, g_mlstm, w_out,
           ln1_g, ln1_b, w_gate, w_up, w_down, ln2_g, ln2_b):
    b, s, d = x_prompt.shape
    n, l, _ = x_sample.shape

    w0 = w_in[0]
    w_in_parts = (_pair_heads(w0[:, :ATT_W], 1).astype(BF16), w0[:, ATT_W:IN_MAIN].astype(BF16),
                  jnp.pad(w0[:, IN_MAIN:], ((0, 0), (0, IN_PAD - IN_MAIN - IN_GATES))).astype(BF16))
    wqk = jnp.concatenate([w_mq[0], w_mk[0]], axis=-1).astype(BF16)
    gb = jnp.concatenate([b_i[0], b_f[0], jnp.zeros((128 - IN_GATES,), F32)]).reshape(1, LANES)
    mixer_w = w_in_parts + (w_conv[0], b_conv[0].reshape(1, MLSTM_W), wqk, gb, attn_sinks[0],
                            _pair_heads(g_attn[0].reshape(1, ATT_W), 1), g_mlstm[0].reshape(1, MLSTM_W))
    post_w = (w_out[0], ln1_g[0].reshape(1, d), ln1_b[0].reshape(1, d), w_gate[0], w_up[0], w_down[0],
              ln2_g[0].reshape(1, d), ln2_b[0].reshape(1, d))

    key_minor = lambda c: c[0].transpose(0, 2, 3, 1).reshape(n, KV_W, WINDOW)
    s_m3 = jnp.pad(state_m[0], ((0, 0), (0, LANES - M_HEADS))).reshape(n, 1, LANES)
    ymix_s, sk, sv, scv, s_c, s_n, s_mo = _sample_mixer(
        x_sample, key_minor(cache_k), key_minor(cache_v), state_conv[0], state_C[0], state_n[0], s_m3, mixer_w)

    meta_blk = jnp.concatenate([jnp.zeros((META_PAD, d), x_prompt.dtype), meta_tokens.astype(x_prompt.dtype)], axis=0)
    y_prompt, y_sample, pk, pv, pconv, p_c, p_n, pm8 = _prompt_layer(
        x_prompt, meta_blk, x_sample.reshape(n * l, d), ymix_s, mixer_w, post_w)

    kv5 = lambda a: a.reshape(1, a.shape[0], WINDOW, 2, HEAD_DIM)
    kv5_key_minor = lambda a: a.reshape(a.shape[0], 2, HEAD_DIM, WINDOW).transpose(0, 3, 1, 2)[None]
    return (y_prompt, y_sample.reshape(n, l, d),
            kv5_key_minor(pk), kv5_key_minor(pv), pconv[None], p_c[None], p_n[None], pm8[None, :, :M_HEADS, 0],
            kv5(sk), kv5(sv), scv[None], s_c[None], s_n[None], s_mo[None, :, 0, :M_HEADS])
```

```python
import functools

import jax
import jax.numpy as jnp
from jax import lax
from jax.experimental import pallas as pl
from jax.experimental.pallas import tpu as pltpu

F32 = jnp.float32
BF16 = jnp.bfloat16

D_MODEL = 1024
ATT_W = 512
MLSTM_W = 512
HEAD_DIM = 64
N_HEADS = 8
KV_W = 128
WINDOW = 128
BLOCK = 128
M_HEADS = 4
M_HEAD_DIM = 128
CONV_W = 4
N_META = 16
META_PAD = BLOCK - N_META
D_FF = 2816
LANES = 128
COL_CONV = ATT_W + 2 * KV_W
COL_VM = COL_CONV + MLSTM_W
COL_O = COL_VM + MLSTM_W
IN_MAIN = COL_O + MLSTM_W
IN_GATES = 2 * M_HEADS
IN_PAD = IN_MAIN + LANES
DEPTH = 1
ALIBI_SLOPES = tuple(2.0 ** (-8.0 * (h + 1) / N_HEADS) for h in range(N_HEADS))
DEEPNORM_ALPHA = (2.0 * DEPTH) ** 0.25
EPS = 1e-5
SEQ_TILE = 16
STEP_BLOCKS = 2
FF_CHUNK = 256
FF_LOOKAHEAD = 3
PROJ_CHUNK = 256
WEIGHT_ROWS = 128
WEIGHT_SLOTS = 4
VMEM_LIMIT = 56 * 1024 * 1024


def _pair_heads(a, axis):
    shape = a.shape
    a = a.reshape(shape[:axis] + (2, 4, HEAD_DIM) + shape[axis + 1:])
    return jnp.swapaxes(a, axis, axis + 1).reshape(shape)


def _dot(a, b):
    return jnp.dot(a, b, preferred_element_type=F32)


def _dot_nt(a, b):
    return lax.dot_general(a, b, (((1,), (1,)), ((), ())), preferred_element_type=F32)


def _dot_tn(a, b):
    return lax.dot_general(a, b, (((0,), (0,)), ((), ())), preferred_element_type=F32)


def _split2(x):
    hi = x.astype(BF16)
    lo = (x - hi.astype(F32)).astype(BF16)
    return hi, lo


def _sigmoid(x):
    return 1.0 / (1.0 + jnp.exp(-x))


def _log_sigmoid(x):
    return -(jnp.maximum(-x, 0.0) + jnp.log1p(jnp.exp(-jnp.abs(x))))


def _layer_norm(x, g, b):
    mu = jnp.mean(x, axis=-1, keepdims=True)
    xc = x - mu
    var = jnp.mean(xc * xc, axis=-1, keepdims=True)
    return xc * lax.rsqrt(var + EPS) * g + b


def _attn_head_norm(att, g):
    lo_half = lax.broadcasted_iota(jnp.int32, (att.shape[0], LANES), 1) < HEAD_DIM

    def seg_mean(x):
        s_all = jnp.sum(x, axis=1, keepdims=True)
        s_lo = jnp.sum(jnp.where(lo_half, x, 0.0), axis=1, keepdims=True)
        return jnp.where(lo_half, s_lo, s_all - s_lo) * (1.0 / HEAD_DIM)

    out = []
    for grp in range(att.shape[1] // LANES):
        x = att[:, grp * LANES:(grp + 1) * LANES]
        xc = x - seg_mean(x)
        out.append(xc * lax.rsqrt(seg_mean(xc * xc) + EPS))
    return jnp.concatenate(out, axis=1) * g


def _mlstm_head_norm(hm, g):
    mu = jnp.mean(hm, axis=-1, keepdims=True)
    xc = hm - mu
    var = jnp.mean(xc * xc, axis=-1, keepdims=True)
    return xc * lax.rsqrt(var + EPS) * g


def _softmax_parts(parts, sink):
    sp = [jnp.where(m, s * (HEAD_DIM ** -0.5) + a, -jnp.inf) for s, m, a in parts]
    mx = sink
    for s in sp:
        mx = jnp.maximum(mx, jnp.max(s, axis=1, keepdims=True))
    ps = [jnp.exp(s - mx) for s in sp]
    den = jnp.exp(sink - mx)
    for p in ps:
        den = den + jnp.sum(p, axis=1, keepdims=True)
    return ps, 1.0 / den


def _conv_silu(window, wconv_ref, bconv_ref):
    acc = bconv_ref[...]
    for j in range(CONV_W):
        acc = acc + window(j) * wconv_ref[j:j + 1, :]
    return acc * _sigmoid(acc)


def _gate_columns(gates, gb_ref, row_valid, ltri, extra=None):
    lane = lax.broadcasted_iota(jnp.int32, gates.shape, 1)
    bi_ref, bf_ref = gb_ref
    lane1 = lax.broadcasted_iota(jnp.int32, (1, LANES), 1)
    bias = jnp.zeros((1, LANES), F32)
    for h in range(M_HEADS):
        bias = jnp.where(lane1 == h, bi_ref[h], jnp.where(lane1 == M_HEADS + h, bf_ref[h], bias))
    gb = gates + bias
    is_i = lane < M_HEADS
    is_f = (lane >= M_HEADS) & (lane < 2 * M_HEADS)
    logf = _log_sigmoid(gb)
    if row_valid is not None:
        lf = jnp.where(is_f & row_valid, logf, 0.0)
        li = jnp.where(row_valid, gb, -jnp.inf)
    else:
        lf = jnp.where(is_f, logf, 0.0)
        li = gb
    hi_lo = jnp.concatenate(_split2(lf), axis=1)
    bc = _dot(ltri, hi_lo)
    x = jnp.where(is_i, li, bc[:, :LANES] + bc[:, LANES:])
    if extra is None:
        return x, None
    be = _dot(extra, hi_lo)
    return x, be[:, :LANES] + be[:, LANES:]


class _FfnStages:
    N_CHUNKS = D_FF // FF_CHUNK

    def __init__(self, x, ymix, wout_ref, ln1g_ref, ln1b_ref, wg_ref, wu_ref, wd_ref, ln2g_ref, ln2b_ref):
        self.x, self.ymix = x, ymix
        self.wout_ref, self.ln1g_ref, self.ln1b_ref = wout_ref, ln1g_ref, ln1b_ref
        self.wg_ref, self.wu_ref, self.wd_ref = wg_ref, wu_ref, wd_ref
        self.ln2g_ref, self.ln2b_ref = ln2g_ref, ln2b_ref

    def head(self):
        self.x1 = _layer_norm(DEEPNORM_ALPHA * self.x + _dot(self.ymix, self.wout_ref[...]),
                              self.ln1g_ref[...], self.ln1b_ref[...])
        self.x1b = self.x1.astype(BF16)
        self.acc = DEEPNORM_ALPHA * self.x1

    def _gate_up(self, c):
        cs = slice(c * FF_CHUNK, (c + 1) * FF_CHUNK)
        return _dot(self.x1b, self.wg_ref[:, cs]), _dot(self.x1b, self.wu_ref[:, cs])

    def chunk(self, c):
        if c == 0:
            self.gu = {}
        for n in range(c, min(c + FF_LOOKAHEAD, self.N_CHUNKS - 1) + 1):
            if n not in self.gu:
                self.gu[n] = self._gate_up(n)
        g, u = self.gu.pop(c)
        hid = (g * _sigmoid(g) * u).astype(BF16)
        self.acc = self.acc + _dot(hid, self.wd_ref[c * FF_CHUNK:(c + 1) * FF_CHUNK, :])

    def tail(self):
        return _layer_norm(self.acc, self.ln2g_ref[...], self.ln2b_ref[...])

    def run(self):
        self.head()
        for c in range(self.N_CHUNKS):
            self.chunk(c)
        return self.tail()


def _mix_blocks(zc, nb, pad_rows, thr_first, state, cbuf, wconv_ref, bconv_ref, wqk_ref, gb_ref,
                sinks_ref, gattn_ref, gml_ref, hook):
    rows = nb * BLOCK
    kp_lo, kp_hi, vp_lo, vp_hi, c_state, m_state = state
    c_state, m_state = list(c_state), list(m_state)
    lane = lax.broadcasted_iota(jnp.int32, (BLOCK, LANES), 1)
    lo_half = lane < HEAD_DIM
    row_valid = None
    if pad_rows:
        row_valid = lax.broadcasted_iota(jnp.int32, (rows, 1), 0) >= pad_rows

    s_all, vmasks = [], []
    for blk in range(nb):
        rs = slice(blk * BLOCK, (blk + 1) * BLOCK)
        kb = zc[rs, ATT_W:ATT_W + KV_W].astype(BF16)
        vb = zc[rs, ATT_W + KV_W:ATT_W + 2 * KV_W].astype(BF16)
        zero_b = jnp.zeros_like(kb)
        k_lo, k_hi = jnp.where(lo_half, kb, zero_b), jnp.where(lo_half, zero_b, kb)
        v_lo, v_hi = jnp.where(lo_half, vb, zero_b), jnp.where(lo_half, zero_b, vb)
        kmask = jnp.concatenate([kp_lo, k_lo, kp_hi, k_hi], axis=0)
        vmasks.append(jnp.concatenate([vp_lo, v_lo, vp_hi, v_hi], axis=0))
        q4 = jnp.concatenate([zc[rs, r * LANES:(r + 1) * LANES] for r in range(4)], axis=0).astype(BF16)
        s_all.append(_dot_nt(q4, kmask))
        kp_lo, kp_hi, vp_lo, vp_hi = k_lo, k_hi, v_lo, v_hi
    hook("scores")

    c_in = zc[:, COL_CONV:COL_VM]
    if row_valid is not None:
        c_in = jnp.where(row_valid, c_in, 0.0)
    cbuf[8:8 + rows, :] = c_in
    c_act = _conv_silu(lambda jj: cbuf[5 + jj:5 + jj + rows, :], wconv_ref, bconv_ref)
    cbuf[0:8, :] = cbuf[rows:rows + 8, :]
    hs = range(M_HEADS)
    ca_b = [c_act[:, h * M_HEAD_DIM:(h + 1) * M_HEAD_DIM].astype(BF16) for h in hs]
    qk = [_dot(ca_b[h], wqk_ref[h]) for h in hs]
    qm_b = [qk[h][:, :M_HEAD_DIM].astype(BF16) for h in hs]
    kmt_f = [(qk[h][:, M_HEAD_DIM:] * (M_HEAD_DIM ** -0.5)).T for h in hs]
    hook("front")

    ti = lax.broadcasted_iota(jnp.int32, (rows, rows), 0)
    si = lax.broadcasted_iota(jnp.int32, (rows, rows), 1)
    ltri = jnp.where((si <= ti) & ((si >> 7) == (ti >> 7)), 1.0, 0.0).astype(BF16)
    xg, _ = _gate_columns(zc[:, IN_MAIN:IN_PAD], gb_ref, row_valid, ltri)
    xgt = xg.T
    hook("gates")

    qi = lax.broadcasted_iota(jnp.int32, (BLOCK, BLOCK), 0)
    ks = lax.broadcasted_iota(jnp.int32, (BLOCK, BLOCK), 1)
    own = ks <= qi
    distf = jnp.where(own, qi - ks, WINDOW + qi - ks).astype(F32)
    zero_p = jnp.zeros((BLOCK, BLOCK), BF16)
    att_rows = []
    for blk in range(nb):
        live = None
        if blk == 0 and thr_first is not None:
            live = jnp.where(own, ks + BLOCK, ks) >= thr_first
        p_rows, inv = [], {}
        for r in range(4):
            pr = []
            for c in range(2):
                h = r + 4 * c
                s_blk = s_all[blk][r * LANES:(r + 1) * LANES, c * 256:(c + 1) * 256]
                sp = jnp.where(own, s_blk[:, BLOCK:], s_blk[:, :BLOCK]) * (HEAD_DIM ** -0.5) \
                    - ALIBI_SLOPES[h] * distf
                if live is not None:
                    sp = jnp.where(live, sp, -jnp.inf)
                sink = sinks_ref[h]
                mx = jnp.maximum(jnp.max(sp, axis=1, keepdims=True), sink)
                p = jnp.exp(sp - mx)
                inv[h] = 1.0 / (jnp.sum(p, axis=1, keepdims=True) + jnp.exp(sink - mx))
                p_b = p.astype(BF16)
                pr += [jnp.where(own, zero_p, p_b), jnp.where(own, p_b, zero_p)]
            p_rows.append(jnp.concatenate(pr, axis=1))
            hook("softmax", blk * 4 + r)
        o_all = _dot(jnp.concatenate(p_rows, axis=0), vmasks[blk])
        att_rows.append(jnp.concatenate(
            [o_all[r * LANES:(r + 1) * LANES] * jnp.where(lo_half, inv[r], inv[r + 4]) for r in range(4)], axis=1))
        hook("pv", blk)

    ones_col = jnp.where(lane == 0, 1.0, 0.0).astype(BF16)
    tb = lax.broadcasted_iota(jnp.int32, (BLOCK, BLOCK), 0)
    sb = lax.broadcasted_iota(jnp.int32, (BLOCK, BLOCK), 1)
    tri = sb <= tb
    hh_rows = [[] for _ in hs]
    for blk in range(nb):
        rs = slice(blk * BLOCK, (blk + 1) * BLOCK)
        b_c = [xg[rs, 4 + h:5 + h] for h in hs]
        b_r = [xgt[4 + h:5 + h, rs] for h in hs]
        li_r = [xgt[h:h + 1, rs] for h in hs]
        kmt = [kmt_f[h][:, rs] for h in hs]
        v_ext = [jnp.concatenate([zc[rs, COL_VM + h * M_HEAD_DIM:COL_VM + (h + 1) * M_HEAD_DIM].astype(BF16), ones_col], axis=1)
                 for h in hs]
        s_qk = [_dot(qm_b[h][rs], kmt[h].astype(BF16)) for h in hs]
        inter = [_dot(qm_b[h][rs], c_state[h].astype(BF16)) for h in hs]
        hook("mlstm_a", blk)
        dmat = [jnp.where(tri, b_c[h] - b_r[h] + li_r[h], -jnp.inf) for h in hs]
        m_inter = [b_c[h] + m_state[h] for h in hs]
        m_t = [jnp.maximum(m_inter[h], jnp.max(dmat[h], axis=1, keepdims=True)) for h in hs]
        w_inter = [jnp.exp(m_inter[h] - m_t[h]) for h in hs]
        sc_b = [(s_qk[h] * jnp.exp(dmat[h] - m_t[h])).astype(BF16) for h in hs]
        hook("mlstm_b", blk)
        nd = [_dot(sc_b[h], v_ext[h]) + w_inter[h] * inter[h] for h in hs]
        for h in hs:
            m_end = m_t[h][BLOCK - 1:BLOCK, :]
            b_last = b_c[h][BLOCK - 1:BLOCK, :]
            decay = jnp.exp(b_last + m_state[h] - m_end)
            wk_r = jnp.exp(b_last - b_r[h] + li_r[h] - m_end)
            c_state[h] = decay * c_state[h] + _dot((kmt[h] * wk_r).astype(BF16), v_ext[h])
            m_state[h] = m_end
            num, den = nd[h][:, :M_HEAD_DIM], nd[h][:, M_HEAD_DIM:M_HEAD_DIM + 1]
            hh_rows[h].append(num / jnp.maximum(jnp.abs(den), jnp.exp(-m_t[h])))
        hook("mlstm_c", blk)

    y_att = _attn_head_norm(jnp.concatenate(att_rows, axis=0), gattn_ref[...])
    hook("attnorm")
    y_m = []
    for h in hs:
        hm = _sigmoid(zc[:, COL_O + h * M_HEAD_DIM:COL_O + (h + 1) * M_HEAD_DIM]) * jnp.concatenate(hh_rows[h], axis=0)
        y_m.append(_mlstm_head_norm(hm, gml_ref[:, h * M_HEAD_DIM:(h + 1) * M_HEAD_DIM]))
        hook("headnorm", h)
    ymix = jnp.concatenate([y_att] + y_m, axis=1).astype(BF16)
    return ymix, (kp_lo, kp_hi, vp_lo, vp_hi, c_state, m_state)


def _project(xb, w_in_refs, lo=0, hi=IN_PAD):
    bounds = (0, ATT_W, IN_MAIN, IN_PAD)
    parts = []
    for ref, start, stop in zip(w_in_refs, bounds[:-1], bounds[1:]):
        a, b = max(lo, start), min(hi, stop)
        if a < b:
            parts.append(_dot(xb, ref[:, a - start:b - start]))
    return parts[0] if len(parts) == 1 else jnp.concatenate(parts, axis=1)


def _meta_state(meta_ref, w_inq_ref, w_inr_ref, w_ing_ref, wconv_ref, bconv_ref, wqk_ref, gb_ref,
                sinks_ref, gattn_ref,
                gml_ref, kplo_o, kphi_o, vplo_o, vphi_o, cb_o, ctn_o, mst_o, zc, cbuf):
    xb = jnp.concatenate([jnp.zeros((META_PAD, D_MODEL), F32), meta_ref[...]], axis=0)
    zc[...] = _project(xb.astype(BF16), (w_inq_ref, w_inr_ref, w_ing_ref))
    cbuf[0:8, :] = jnp.zeros((8, MLSTM_W), F32)
    zb = jnp.zeros((BLOCK, KV_W), BF16)
    state = (zb, zb, zb, zb, [jnp.zeros((M_HEAD_DIM, 2 * M_HEAD_DIM), F32)] * M_HEADS,
             [jnp.zeros((1, 1), F32)] * M_HEADS)
    _, (kp_lo, kp_hi, vp_lo, vp_hi, c_state, m_state) = _mix_blocks(
        zc, 1, META_PAD, BLOCK + META_PAD, state, cbuf, wconv_ref, bconv_ref, wqk_ref, gb_ref,
        sinks_ref, gattn_ref, gml_ref, lambda name, index=0: None)
    kplo_o[...] = kp_lo
    kphi_o[...] = kp_hi
    vplo_o[...] = vp_lo
    vphi_o[...] = vp_hi
    cb_o[...] = cbuf[0:8, :]
    for h in range(M_HEADS):
        ctn_o[h] = c_state[h]
        mst_o[h:h + 1, :] = jnp.broadcast_to(m_state[h], (1, LANES))
    mst_o[M_HEADS:8, :] = jnp.zeros((8 - M_HEADS, LANES), F32)


def _load_ffn_weights(wout_hbm, wg_hbm, wu_hbm, wd_hbm, wout_v, wg_v, wu_v, wd_v, stage_bufs, sems):
    half = D_FF // 2
    jobs = []
    for src, dst in ((wg_hbm, wg_v), (wu_hbm, wu_v)):
        jobs += [(src, r, WEIGHT_ROWS, c, half, dst, r) for r in range(0, D_MODEL, WEIGHT_ROWS) for c in (0, half)]
    jobs += [(wd_hbm, r, WEIGHT_ROWS, 0, D_MODEL, wd_v, r) for r in range(0, D_FF, WEIGHT_ROWS)]
    for h in range(N_HEADS):
        jobs.append((wout_hbm, h * HEAD_DIM, HEAD_DIM, 0, D_MODEL, wout_v, ((h % 4) * 2 + h // 4) * HEAD_DIM))
    jobs += [(wout_hbm, r, WEIGHT_ROWS, 0, D_MODEL, wout_v, r) for r in range(ATT_W, D_MODEL, WEIGHT_ROWS)]

    windows = []
    for buf in stage_bufs:
        views = [buf.at[s] for s in range(buf.shape[0])] if len(buf.shape) == 3 else [buf]
        for v in views:
            windows += [(v, r, c) for r in range(0, v.shape[0] - WEIGHT_ROWS + 1, WEIGHT_ROWS)
                        for c in range(0, v.shape[1] - half + 1, half)]
    nslot = len(windows)
    assert nslot <= sems.shape[0]

    def window(i, nrows, ncols):
        v, r, c = windows[i % nslot]
        return v.at[pl.ds(r, nrows), pl.ds(c, ncols)]

    def copy(i):
        src, r0, nrows, c0, ncols, _, _ = jobs[i]
        return pltpu.make_async_copy(src.at[pl.ds(r0, nrows), pl.ds(c0, ncols)], window(i, nrows, ncols),
                                     sems.at[i % nslot])

    for i in range(min(nslot - 1, len(jobs))):
        copy(i).start()
    for i, (_, _, nrows, c0, ncols, dst, d0) in enumerate(jobs):
        if i + nslot - 1 < len(jobs):
            copy(i + nslot - 1).start()
        copy(i).wait()
        dst[d0:d0 + nrows, c0:c0 + ncols] = window(i, nrows, ncols)[...].astype(BF16)


def _pair_kernel(xin_ref, xres_ref, meta_ref, xs_ref, ymix_s_ref,
                 w_inq_ref, w_inr_ref, w_ing_ref, wconv_ref, bconv_ref, wqk_ref, bi_ref, bf_ref, sinks_ref,
                 gattn_ref, gml_ref,
                 wout_hbm, ln1g_ref, ln1b_ref, wg_hbm, wu_hbm, wd_hbm, ln2g_ref, ln2b_ref,
                 y_ref, ys_ref, pk_ref, pv_ref, pconv_ref, pc_ref, pn_ref, pm_ref,
                 wout_v, wg_v, wu_v, wd_v, wstage, wsems,
                 kplo_i, kphi_i, vplo_i, vphi_i, cb_i, ctn_i, mst_i,
                 zcur, znext, kvlast, kplo, kphi, vplo, vphi, cbuf, ctn, mst, yprev, *, npair, n_pair_steps):
    t = pl.program_id(0)
    p = lax.rem(t + (npair - 1), npair)
    gb_ref = (bi_ref, bf_ref)

    @pl.when(t == 0)
    def _init_pipeline():
        _load_ffn_weights(wout_hbm, wg_hbm, wu_hbm, wd_hbm, wout_v, wg_v, wu_v, wd_v,
                          (wstage, zcur, znext), wsems)
        zcur[...] = jnp.zeros_like(zcur)
        yprev[...] = jnp.zeros_like(yprev)
        _meta_state(meta_ref, w_inq_ref, w_inr_ref, w_ing_ref, wconv_ref, bconv_ref, wqk_ref, gb_ref,
                    sinks_ref, gattn_ref, gml_ref, kplo_i, kphi_i, vplo_i, vphi_i, cb_i, ctn_i, mst_i,
                    znext.at[pl.ds(0, BLOCK)], cbuf)

    @pl.when((p == 0) | (t == 0))
    def _load_meta_state():
        kplo[...] = kplo_i[...]
        kphi[...] = kphi_i[...]
        vplo[...] = vplo_i[...]
        vphi[...] = vphi_i[...]
        cbuf[0:8, :] = cb_i[...]
        ctn[...] = ctn_i[...]
        mst[...] = mst_i[...]

    ffn_w = (wout_v, ln1g_ref, ln1b_ref, wg_v, wu_v, wd_v, ln2g_ref, ln2b_ref)

    @pl.when(t < n_pair_steps)
    def _pair_step():
        ffn = _FfnStages(xres_ref[...], yprev[...], *ffn_w)
        xb_b = xin_ref[...].astype(BF16)

        n_proj = -(-IN_PAD // PROJ_CHUNK)

        def proj(n):
            lo, hi = n * PROJ_CHUNK, min((n + 1) * PROJ_CHUNK, IN_PAD)
            znext[:, lo:hi] = _project(xb_b, (w_inq_ref, w_inr_ref, w_ing_ref), lo, hi)

        proj_it, ffn_it = iter(range(n_proj)), iter(range(ffn.N_CHUNKS))

        def fill_proj(n):
            for c in [c for _, c in zip(range(n), proj_it)]:
                proj(c)

        def fill_ffn(n):
            for c in [c for _, c in zip(range(n), ffn_it)]:
                ffn.chunk(c)

        out = {}

        def hook(name, index=0):
            if name == "scores":
                fill_proj(2)
                ffn.head()
            elif name == "front":
                fill_proj(1)
            elif name == "gates":
                fill_ffn(1)
            elif name == "softmax":
                if index % 2 == 1:
                    fill_ffn(1)
            elif name in ("pv", "mlstm_a", "mlstm_b"):
                fill_ffn(1)
            elif name == "mlstm_c":
                fill_proj(1)
            elif name == "attnorm":
                fill_ffn(ffn.N_CHUNKS)
                out["y"] = ffn.tail()
                fill_proj(1)
            elif name == "headnorm":
                fill_proj(1)

        state = (kplo[...], kphi[...], vplo[...], vphi[...], [ctn[h] for h in range(M_HEADS)],
                 [mst[h:h + 1, 0:1] for h in range(M_HEADS)])
        thr_first = jnp.where(p == 0, META_PAD, 0)
        kvlast[...] = zcur[(STEP_BLOCKS - 1) * BLOCK:STEP_BLOCKS * BLOCK, ATT_W:ATT_W + 2 * KV_W]
        ymix, (kp_lo, kp_hi, vp_lo, vp_hi, c_state, m_state) = _mix_blocks(
            zcur, STEP_BLOCKS, 0, thr_first, state, cbuf, wconv_ref, bconv_ref, wqk_ref, gb_ref,
            sinks_ref, gattn_ref, gml_ref, hook)
        fill_proj(n_proj)
        y_ref[...] = out["y"]
        kplo[...] = kp_lo
        kphi[...] = kp_hi
        vplo[...] = vp_lo
        vphi[...] = vp_hi
        for h in range(M_HEADS):
            ctn[h] = c_state[h]
            mst[h:h + 1, :] = jnp.broadcast_to(m_state[h], (1, LANES))
        yprev[...] = ymix
        zcur[...] = znext[...]

    @pl.when(t >= n_pair_steps)
    def _sample_ffn_step():
        ys_ref[...] = _FfnStages(xs_ref[...], ymix_s_ref[...], *ffn_w).run()

    @pl.when((p == npair - 1) & (t < n_pair_steps))
    def _final():
        pk_ref[...] = kvlast[:, 0:KV_W].T
        pv_ref[...] = kvlast[:, KV_W:2 * KV_W].T
        pconv_ref[...] = cbuf[8 - (CONV_W - 1):8, :]
        pm_ref[...] = mst[...]
        for h in range(M_HEADS):
            c_n = ctn[h].T
            pc_ref[h] = c_n[0:M_HEAD_DIM, :]
            pn_ref[h:h + 1, :] = c_n[M_HEAD_DIM:M_HEAD_DIM + 1, :]


def _sample_kernel(xs_ref, ck_ref, cv_ref, sconv_ref, sc_ref, sn_ref, sm_ref,
                   w_inq_ref, w_inr_ref, w_ing_ref, wconv_ref, bconv_ref, wqk_ref, bi_ref, bf_ref, sinks_ref,
                   gattn_ref, gml_ref,
                   ymix_ref, sk_ref, sv_ref, scv_ref, sco_ref, sno_ref, smo_ref, cext):
    ns, L = SEQ_TILE, 8
    gb_ref = (bi_ref, bf_ref)
    xs = xs_ref[...].reshape(ns * L, D_MODEL)
    z = _project(xs.astype(BF16), (w_inq_ref, w_inr_ref, w_ing_ref))

    lane = lax.broadcasted_iota(jnp.int32, (BLOCK, LANES), 1)
    lo_half = lane < HEAD_DIM
    lo3 = lax.broadcasted_iota(jnp.int32, (ns, L, LANES), 2) < HEAD_DIM
    ti = lax.broadcasted_iota(jnp.int32, (BLOCK, BLOCK), 0)
    si = lax.broadcasted_iota(jnp.int32, (BLOCK, BLOCK), 1)
    same_seq = (ti >> 3) == (si >> 3)
    tl, sl_ = ti & 7, si & 7
    causal_seq = same_seq & (sl_ <= tl)

    k_new = z[:, ATT_W:ATT_W + KV_W]
    v_new = z[:, ATT_W + KV_W:ATT_W + 2 * KV_W]
    kb, vb = k_new.astype(BF16), v_new.astype(BF16)
    zero_b = jnp.zeros_like(kb)
    kn_mask = jnp.concatenate([jnp.where(lo_half, kb, zero_b), jnp.where(lo_half, zero_b, kb)], axis=0)
    vn_mask = jnp.concatenate([jnp.where(lo_half, vb, zero_b), jnp.where(lo_half, zero_b, vb)], axis=0)
    qg = [z[:, r * LANES:(r + 1) * LANES] for r in range(4)]
    s_new = _dot_nt(jnp.concatenate(qg, axis=0).astype(BF16), kn_mask)
    qg3 = [q.reshape(ns, L, LANES) for q in qg]
    q8 = jnp.concatenate([jnp.where(lo3, q, 0.0) for q in qg3] + [jnp.where(lo3, 0.0, q) for q in qg3],
                         axis=1).astype(BF16)
    s_cache = jnp.einsum('nqd,ndk->nqk', q8, ck_ref[...].astype(BF16), preferred_element_type=F32)

    c3 = z[:, COL_CONV:COL_VM].reshape(ns, L, MLSTM_W)
    cext[:, 8 - (CONV_W - 1):8, :] = sconv_ref[...]
    cext[:, 8:16, :] = c3
    scv_ref[...] = c3[:, L - (CONV_W - 1):L, :]
    c_act = _conv_silu(lambda jj: cext[:, 5 + jj:5 + jj + L, :], wconv_ref, bconv_ref)
    c_act = c_act.reshape(BLOCK, MLSTM_W)
    lblk = jnp.where(causal_seq, 1.0, 0.0).astype(BF16)
    lseq = jnp.where(same_seq, 1.0, 0.0).astype(BF16)
    xg, blast = _gate_columns(z[:, IN_MAIN:IN_PAD], gb_ref, None, lblk, extra=lseq)
    xgt = xg.T
    m_rep = jnp.broadcast_to(sm_ref[...], (ns, L, LANES)).reshape(BLOCK, LANES)
    hs = range(M_HEADS)
    qk = [_dot(c_act[:, h * M_HEAD_DIM:(h + 1) * M_HEAD_DIM].astype(BF16), wqk_ref[h]) for h in hs]
    qm = [qk[h][:, :M_HEAD_DIM] for h in hs]
    qm_b = [qm[h].astype(BF16) for h in hs]
    km = [qk[h][:, M_HEAD_DIM:] * (M_HEAD_DIM ** -0.5) for h in hs]
    km_b = [km[h].astype(BF16) for h in hs]
    vh_b = [z[:, COL_VM + h * M_HEAD_DIM:COL_VM + (h + 1) * M_HEAD_DIM].astype(BF16) for h in hs]
    c_old = [sc_ref[:, h] for h in hs]
    s_qk = [_dot_nt(qm_b[h], km_b[h]) for h in hs]
    cq = [_dot_nt(qm_b[h], c_old[h].reshape(ns * 128, 128).astype(BF16)) for h in hs]

    mask_c = si > tl
    dist_c = (WINDOW + tl - si).astype(F32)
    dist_n = (tl - sl_).astype(F32)
    pc_parts, pn_rows, inv = {}, [], {}
    for r in range(4):
        pn = []
        for c in range(2):
            h = r + 4 * c
            sc_c = s_cache[:, c * 32 + r * L:c * 32 + (r + 1) * L, :].reshape(BLOCK, LANES)
            sc_n = s_new[r * LANES:(r + 1) * LANES, c * BLOCK:(c + 1) * BLOCK]
            (p_c, p_n), inv[h] = _softmax_parts(
                [(sc_c, mask_c, -ALIBI_SLOPES[h] * dist_c), (sc_n, causal_seq, -ALIBI_SLOPES[h] * dist_n)],
                sinks_ref[h])
            pc_parts[(c, r)] = p_c.reshape(ns, L, LANES)
            pn.append(p_n.astype(BF16))
        pn_rows.append(jnp.concatenate(pn, axis=1))
    p8 = jnp.concatenate([pc_parts[(c, r)] for c in range(2) for r in range(4)], axis=1).astype(BF16)
    o_cache = jnp.einsum('nqk,ndk->nqd', p8, cv_ref[...].astype(BF16), preferred_element_type=F32)
    o_new = _dot(jnp.concatenate(pn_rows, axis=0), vn_mask)
    groups = []
    for r in range(4):
        oa = o_cache[:, r * L:(r + 1) * L, :].reshape(BLOCK, LANES)
        ob = o_cache[:, 32 + r * L:32 + (r + 1) * L, :].reshape(BLOCK, LANES)
        o = jnp.where(lo_half, oa, ob) + o_new[r * LANES:(r + 1) * LANES]
        groups.append(o * jnp.where(lo_half, inv[r], inv[r + 4]))
    y_att = _attn_head_norm(jnp.concatenate(groups, axis=1), gattn_ref[...])

    sk_ref[:, 0:WINDOW - L, :] = jnp.swapaxes(ck_ref[...], 1, 2)[:, L:WINDOW, :]
    sk_ref[:, WINDOW - L:WINDOW, :] = k_new.reshape(ns, L, LANES)
    sv_ref[:, 0:WINDOW - L, :] = jnp.swapaxes(cv_ref[...], 1, 2)[:, L:WINDOW, :]
    sv_ref[:, WINDOW - L:WINDOW, :] = v_new.reshape(ns, L, LANES)

    b_c = [xg[:, 4 + h:5 + h] for h in hs]
    li_c = [xg[:, h:h + 1] for h in hs]
    b_r = [xgt[4 + h:5 + h, :] for h in hs]
    li_r = [xgt[h:h + 1, :] for h in hs]
    bl_c = [blast[:, 4 + h:5 + h] for h in hs]
    m_prev = [m_rep[:, h:h + 1] for h in hs]
    dmat = [jnp.where(causal_seq, b_c[h] - b_r[h] + li_r[h], -jnp.inf) for h in hs]
    m_inter = [b_c[h] + m_prev[h] for h in hs]
    m_t = [jnp.maximum(m_inter[h], jnp.max(dmat[h], axis=1, keepdims=True)) for h in hs]
    w_inter = [jnp.exp(m_inter[h] - m_t[h]) for h in hs]
    sc = [s_qk[h] * jnp.exp(dmat[h] - m_t[h]) for h in hs]
    num_intra = [_dot(sc[h].astype(BF16), vh_b[h]) for h in hs]

    m_end3 = [jnp.broadcast_to(m_t[h], (BLOCK, LANES)).reshape(ns, L, LANES)[:, L - 1:L, :] for h in hs]
    m_end = [jnp.broadcast_to(m_end3[h], (ns, L, LANES)).reshape(BLOCK, LANES)[:, 0:1] for h in hs]
    decay = [jnp.exp(bl_c[h] + m_prev[h] - m_end[h]) for h in hs]
    kw = [km[h] * jnp.exp(bl_c[h] - b_c[h] + li_c[h] - m_end[h]) for h in hs]
    seq_of_row = ti >> 3
    lane3 = lax.broadcasted_iota(jnp.int32, (ns, 1, LANES), 2)
    m_out = jnp.zeros((ns, 1, LANES), F32)
    for h in hs:
        kw_b = kw[h].astype(BF16)
        k_big = jnp.concatenate(
            [jnp.where(seq_of_row == g, kw_b, jnp.zeros_like(kw_b)) for g in range(ns)], axis=1)
        c_up = _dot_tn(vh_b[h], k_big)
        c_up3 = jnp.stack([c_up[:, g * M_HEAD_DIM:(g + 1) * M_HEAD_DIM] for g in range(ns)], axis=0)
        decay3 = jnp.broadcast_to(decay[h], (BLOCK, LANES)).reshape(ns, L, LANES)[:, 0:1, :]
        n_old = sn_ref[:, h:h + 1, :]
        sco_ref[:, h] = decay3 * c_old[h] + c_up3
        sno_ref[:, h:h + 1, :] = decay3 * n_old + jnp.sum(kw[h].reshape(ns, L, LANES), axis=1, keepdims=True)
        m_out = jnp.where(lane3 == h, m_end3[h], m_out)
    smo_ref[...] = m_out

    y_m = []
    for h in hs:
        num_inter = jnp.concatenate([cq[h][g * L:(g + 1) * L, g * M_HEAD_DIM:(g + 1) * M_HEAD_DIM] for g in range(ns)], axis=0)
        n_rep = jnp.broadcast_to(sn_ref[:, h:h + 1, :], (ns, L, LANES)).reshape(BLOCK, LANES)
        num = num_intra[h] + w_inter[h] * num_inter
        den = jnp.sum(sc[h], axis=1, keepdims=True) + w_inter[h] * jnp.sum(qm[h] * n_rep, axis=1, keepdims=True)
        hh = num / jnp.maximum(jnp.abs(den), jnp.exp(-m_t[h]))
        hm = _sigmoid(z[:, COL_O + h * M_HEAD_DIM:COL_O + (h + 1) * M_HEAD_DIM]) * hh
        y_m.append(_mlstm_head_norm(hm, gml_ref[:, h * M_HEAD_DIM:(h + 1) * M_HEAD_DIM]))

    ymix_ref[...] = jnp.concatenate([y_att] + y_m, axis=1).astype(BF16)


def _const_spec(shape):
    nd = len(shape)
    return pl.BlockSpec(shape, lambda *_: (0,) * nd, pipeline_mode=pl.Buffered(1))


def _mixer_weight_specs():
    return [
        _const_spec((D_MODEL, ATT_W)),
        _const_spec((D_MODEL, IN_MAIN - ATT_W)),
        _const_spec((D_MODEL, IN_PAD - IN_MAIN)),
        _const_spec((CONV_W, MLSTM_W)),
        _const_spec((1, MLSTM_W)),
        _const_spec((M_HEADS, M_HEAD_DIM, 2 * M_HEAD_DIM)),
        pl.BlockSpec(memory_space=pltpu.SMEM),
        pl.BlockSpec(memory_space=pltpu.SMEM),
        pl.BlockSpec(memory_space=pltpu.SMEM),
        _const_spec((1, ATT_W)),
        _const_spec((1, MLSTM_W)),
    ]


def _post_weight_specs(d):
    hbm = pl.BlockSpec(memory_space=pl.ANY)
    return [hbm, _const_spec((1, d)), _const_spec((1, d)), hbm, hbm, hbm, _const_spec((1, d)), _const_spec((1, d))]


def _prompt_layer(x_prompt, meta_tokens, x_sample_rows, ymix_sample, mixer_w, post_w):
    b, s, d = x_prompt.shape
    state_shapes = [((BLOCK, KV_W), BF16)] * 4 + [
        ((8, MLSTM_W), F32), ((M_HEADS, M_HEAD_DIM, 2 * M_HEAD_DIM), F32), ((8, LANES), F32)]
    rows = STEP_BLOCKS * BLOCK
    npair = s // rows
    total = b * npair
    n_pair_steps = total + 2
    n_sample_steps = x_sample_rows.shape[0] // rows
    steps = n_pair_steps + n_sample_steps

    def block_map(lag):
        def index(t):
            tb = jnp.clip(t - lag, 0, total - 1)
            return (tb // npair, tb % npair, 0)
        return index

    def smap(t):
        return (jnp.clip(t - 1, 0, total - 1) // npair, 0, 0)

    def sample_map(t):
        return (jnp.clip(t - n_pair_steps, 0, n_sample_steps - 1), 0)

    out_shape = (
        jax.ShapeDtypeStruct((b, s, d), F32),
        jax.ShapeDtypeStruct(x_sample_rows.shape, F32),
        jax.ShapeDtypeStruct((b, BLOCK, KV_W), F32),
        jax.ShapeDtypeStruct((b, BLOCK, KV_W), F32),
        jax.ShapeDtypeStruct((b, CONV_W - 1, MLSTM_W), F32),
        jax.ShapeDtypeStruct((b, M_HEADS, M_HEAD_DIM, M_HEAD_DIM), F32),
        jax.ShapeDtypeStruct((b, M_HEADS, M_HEAD_DIM), F32),
        jax.ShapeDtypeStruct((b, 8, LANES), F32),
    )
    out_specs = (
        pl.BlockSpec((None, rows, d), block_map(2)),
        pl.BlockSpec((rows, d), sample_map),
        pl.BlockSpec((None, BLOCK, KV_W), smap),
        pl.BlockSpec((None, BLOCK, KV_W), smap),
        pl.BlockSpec((None, CONV_W - 1, MLSTM_W), smap),
        pl.BlockSpec((None, M_HEADS, M_HEAD_DIM, M_HEAD_DIM), lambda t: smap(t) + (0,)),
        pl.BlockSpec((None, M_HEADS, M_HEAD_DIM), smap),
        pl.BlockSpec((None, 8, LANES), smap),
    )
    return pl.pallas_call(
        functools.partial(_pair_kernel, npair=npair, n_pair_steps=n_pair_steps),
        grid=(steps,),
        in_specs=[pl.BlockSpec((None, rows, d), block_map(0)), pl.BlockSpec((None, rows, d), block_map(2)),
                  _const_spec((N_META, d)), pl.BlockSpec((rows, d), sample_map), pl.BlockSpec((rows, d), sample_map)]
        + _mixer_weight_specs() + _post_weight_specs(d),
        out_specs=out_specs,
        out_shape=out_shape,
        scratch_shapes=[
            pltpu.VMEM((d, d), BF16), pltpu.VMEM((d, D_FF), BF16), pltpu.VMEM((d, D_FF), BF16),
            pltpu.VMEM((D_FF, d), BF16),
            pltpu.VMEM((WEIGHT_SLOTS, WEIGHT_ROWS, D_FF), F32),
            pltpu.SemaphoreType.DMA((2 * WEIGHT_SLOTS + 2 * STEP_BLOCKS,)),
        ] + [pltpu.VMEM(shape, dt) for shape, dt in state_shapes] + [
            pltpu.VMEM((rows, IN_PAD), F32),
            pltpu.VMEM((rows, IN_PAD), F32),
            pltpu.VMEM((BLOCK, 2 * KV_W), F32),
        ] + [pltpu.VMEM((BLOCK, KV_W), BF16)] * 4 + [
            pltpu.VMEM((rows + 8, MLSTM_W), F32),
            pltpu.VMEM((M_HEADS, M_HEAD_DIM, 2 * M_HEAD_DIM), F32),
            pltpu.VMEM((8, LANES), F32),
            pltpu.VMEM((rows, d), BF16),
        ],
        compiler_params=pltpu.CompilerParams(
            dimension_semantics=("arbitrary",), vmem_limit_bytes=VMEM_LIMIT),
        name="prompt_layer",
    )(x_prompt, x_prompt, meta_tokens, x_sample_rows, ymix_sample, *mixer_w, *post_w)


def _sample_mixer(x_sample, ck, cv, s_conv, s_c, s_n, s_m3, mixer_w):
    n, l, d = x_sample.shape
    t = SEQ_TILE
    m3 = lambda i: (i, 0, 0)
    m4 = lambda i: (i, 0, 0, 0)
    out_shape = (
        jax.ShapeDtypeStruct((n * l, d), BF16),
        jax.ShapeDtypeStruct((n, WINDOW, KV_W), F32),
        jax.ShapeDtypeStruct((n, WINDOW, KV_W), F32),
        jax.ShapeDtypeStruct((n, CONV_W - 1, MLSTM_W), F32),
        jax.ShapeDtypeStruct((n, M_HEADS, M_HEAD_DIM, M_HEAD_DIM), F32),
        jax.ShapeDtypeStruct((n, M_HEADS, M_HEAD_DIM), F32),
        jax.ShapeDtypeStruct((n, 1, LANES), F32),
    )
    out_specs = (
        pl.BlockSpec((t * l, d), lambda i: (i, 0)),
        pl.BlockSpec((t, WINDOW, KV_W), m3),
        pl.BlockSpec((t, WINDOW, KV_W), m3),
        pl.BlockSpec((t, CONV_W - 1, MLSTM_W), m3),
        pl.BlockSpec((t, M_HEADS, M_HEAD_DIM, M_HEAD_DIM), m4),
        pl.BlockSpec((t, M_HEADS, M_HEAD_DIM), m3),
        pl.BlockSpec((t, 1, LANES), m3),
    )
    in_specs = [
        pl.BlockSpec((t, l, d), m3),
        pl.BlockSpec((t, WINDOW, KV_W), m3),
        pl.BlockSpec((t, WINDOW, KV_W), m3),
        pl.BlockSpec((t, CONV_W - 1, MLSTM_W), m3),
        pl.BlockSpec((t, M_HEADS, M_HEAD_DIM, M_HEAD_DIM), m4),
        pl.BlockSpec((t, M_HEADS, M_HEAD_DIM), m3),
        pl.BlockSpec((t, 1, LANES), m3),
    ] + _mixer_weight_specs()
    return pl.pallas_call(
        _sample_kernel,
        grid=(n // t,),
        in_specs=in_specs,
        out_specs=out_specs,
        out_shape=out_shape,
        scratch_shapes=[pltpu.VMEM((t, 16, MLSTM_W), F32)],
        compiler_params=pltpu.CompilerParams(
            dimension_semantics=("arbitrary",), vmem_limit_bytes=VMEM_LIMIT),
        name="sample_mixer",
    )(x_sample, ck, cv, s_conv, s_c, s_n, s_m3, *mixer_w)


def kernel(x_prompt, x_sample, cache_k, cache_v, state_conv, state_C, state_n, state_m, meta_tokens,
           w_in, w_conv, b_conv, w_mq, w_mk, b_i, b_f, attn_sinks, g_attn, g_mlstm, w_out,
           ln1_g, ln1_b, w_gate, w_up, w_down, ln2_g, ln2_b):
    b, s, d = x_prompt.shape
    n, l, _ = x_sample.shape

    w0 = w_in[0]
    w_in_parts = (_pair_heads(w0[:, :ATT_W], 1).astype(BF16), w0[:, ATT_W:IN_MAIN].astype(BF16),
                  jnp.pad(w0[:, IN_MAIN:], ((0, 0), (0, IN_PAD - IN_MAIN - IN_GATES))).astype(BF16))
    wqk = jnp.concatenate([w_mq[0], w_mk[0]], axis=-1).astype(BF16)
    mixer_w = w_in_parts + (w_conv[0], b_conv[0].reshape(1, MLSTM_W), wqk, b_i[0], b_f[0], attn_sinks[0],
                            _pair_heads(g_attn[0].reshape(1, ATT_W), 1), g_mlstm[0].reshape(1, MLSTM_W))
    post_w = (w_out[0], ln1_g[0].reshape(1, d), ln1_b[0].reshape(1, d), w_gate[0], w_up[0], w_down[0],
              ln2_g[0].reshape(1, d), ln2_b[0].reshape(1, d))

    key_minor = lambda c: c[0].transpose(0, 2, 3, 1).reshape(n, KV_W, WINDOW)
    s_m3 = jnp.pad(state_m[0], ((0, 0), (0, LANES - M_HEADS))).reshape(n, 1, LANES)
    ymix_s, sk, sv, scv, s_c, s_n, s_mo = _sample_mixer(
        x_sample, key_minor(cache_k), key_minor(cache_v), state_conv[0], state_C[0], state_n[0], s_m3, mixer_w)

    y_prompt, y_sample, pk, pv, pconv, p_c, p_n, pm8 = _prompt_layer(
        x_prompt, meta_tokens, x_sample.reshape(n * l, d), ymix_s, mixer_w, post_w)

    kv5 = lambda a: a.reshape(1, a.shape[0], WINDOW, 2, HEAD_DIM)
    kv5_key_minor = lambda a: a.reshape(a.shape[0], 2, HEAD_DIM, WINDOW).transpose(0, 3, 1, 2)[None]
    return (y_prompt, y_sample.reshape(n, l, d),
            kv5_key_minor(pk), kv5_key_minor(pv), pconv[None], p_c[None], p_n[None], pm8[None, :, :M_HEADS, 0],
            kv5(sk), kv5(sv), scv[None], s_c[None], s_n[None], s_mo[None, :, 0, :M_HEADS])
```

```python
import functools

import jax
import jax.numpy as jnp
from jax import lax
from jax.experimental import pallas as pl
from jax.experimental.pallas import tpu as pltpu

F32 = jnp.float32
BF16 = jnp.bfloat16

D_MODEL = 1024
ATT_W = 512
MLSTM_W = 512
HEAD_DIM = 64
N_HEADS = 8
KV_W = 128
WINDOW = 128
BLOCK = 128
M_HEADS = 4
M_HEAD_DIM = 128
CONV_W = 4
N_META = 16
META_PAD = BLOCK - N_META
D_FF = 2816
LANES = 128
COL_CONV = ATT_W + 2 * KV_W
COL_VM = COL_CONV + MLSTM_W
COL_O = COL_VM + MLSTM_W
IN_MAIN = COL_O + MLSTM_W
IN_GATES = 2 * M_HEADS
IN_PAD = IN_MAIN + LANES
DEPTH = 1
ALIBI_SLOPES = tuple(2.0 ** (-8.0 * (h + 1) / N_HEADS) for h in range(N_HEADS))
DEEPNORM_ALPHA = (2.0 * DEPTH) ** 0.25
EPS = 1e-5
SEQ_TILE = 16
STEP_BLOCKS = 2
FF_CHUNK = 256
FF_LOOKAHEAD = 3
PROJ_CHUNK = 256
WEIGHT_ROWS = 128
WEIGHT_SLOTS = 4
VMEM_LIMIT = 56 * 1024 * 1024


def _pair_heads(a, axis):
    half = N_HEADS // 2
    heads = [lax.slice_in_dim(a, h * HEAD_DIM, (h + 1) * HEAD_DIM, axis=axis)
             for r in range(half) for h in (r, r + half)]
    return jnp.concatenate(heads, axis=axis)


def _dot(a, b):
    return jnp.dot(a, b, preferred_element_type=F32)


def _dot_nt(a, b):
    return lax.dot_general(a, b, (((1,), (1,)), ((), ())), preferred_element_type=F32)


def _dot_tn(a, b):
    return lax.dot_general(a, b, (((0,), (0,)), ((), ())), preferred_element_type=F32)


def _split2(x):
    hi = x.astype(BF16)
    lo = (x - hi.astype(F32)).astype(BF16)
    return hi, lo


def _sigmoid(x):
    return 1.0 / (1.0 + jnp.exp(-x))


def _log_sigmoid(x):
    return -(jnp.maximum(-x, 0.0) + jnp.log1p(jnp.exp(-jnp.abs(x))))


def _layer_norm(x, g, b):
    mu = jnp.mean(x, axis=-1, keepdims=True)
    xc = x - mu
    var = jnp.mean(xc * xc, axis=-1, keepdims=True)
    return xc * lax.rsqrt(var + EPS) * g + b


def _attn_head_norm(att, g):
    lo_half = lax.broadcasted_iota(jnp.int32, (att.shape[0], LANES), 1) < HEAD_DIM

    def seg_mean(x):
        s_all = jnp.sum(x, axis=1, keepdims=True)
        s_lo = jnp.sum(jnp.where(lo_half, x, 0.0), axis=1, keepdims=True)
        return jnp.where(lo_half, s_lo, s_all - s_lo) * (1.0 / HEAD_DIM)

    out = []
    for grp in range(att.shape[1] // LANES):
        x = att[:, grp * LANES:(grp + 1) * LANES]
        xc = x - seg_mean(x)
        out.append(xc * lax.rsqrt(seg_mean(xc * xc) + EPS))
    return jnp.concatenate(out, axis=1) * g


def _mlstm_head_norm(hm, g):
    mu = jnp.mean(hm, axis=-1, keepdims=True)
    xc = hm - mu
    var = jnp.mean(xc * xc, axis=-1, keepdims=True)
    return xc * lax.rsqrt(var + EPS) * g


def _softmax_parts(parts, sink):
    sp = [jnp.where(m, s * (HEAD_DIM ** -0.5) + a, -jnp.inf) for s, m, a in parts]
    mx = sink
    for s in sp:
        mx = jnp.maximum(mx, jnp.max(s, axis=1, keepdims=True))
    ps = [jnp.exp(s - mx) for s in sp]
    den = jnp.exp(sink - mx)
    for p in ps:
        den = den + jnp.sum(p, axis=1, keepdims=True)
    return ps, 1.0 / den


def _conv_silu(window, wconv_ref, bconv_ref):
    acc = bconv_ref[...]
    for j in range(CONV_W):
        acc = acc + window(j) * wconv_ref[j:j + 1, :]
    return acc * _sigmoid(acc)


def _gate_columns(gates, gb_ref, row_valid, ltri, extra=None):
    lane = lax.broadcasted_iota(jnp.int32, gates.shape, 1)
    bi_ref, bf_ref = gb_ref
    lane1 = lax.broadcasted_iota(jnp.int32, (1, LANES), 1)
    bias = jnp.zeros((1, LANES), F32)
    for h in range(M_HEADS):
        bias = jnp.where(lane1 == h, bi_ref[h], jnp.where(lane1 == M_HEADS + h, bf_ref[h], bias))
    gb = gates + bias
    is_i = lane < M_HEADS
    is_f = (lane >= M_HEADS) & (lane < 2 * M_HEADS)
    logf = _log_sigmoid(gb)
    if row_valid is not None:
        lf = jnp.where(is_f & row_valid, logf, 0.0)
        li = jnp.where(row_valid, gb, -jnp.inf)
    else:
        lf = jnp.where(is_f, logf, 0.0)
        li = gb
    hi_lo = jnp.concatenate(_split2(lf), axis=1)
    bc = _dot(ltri, hi_lo)
    x = jnp.where(is_i, li, bc[:, :LANES] + bc[:, LANES:])
    if extra is None:
        return x, None
    be = _dot(extra, hi_lo)
    return x, be[:, :LANES] + be[:, LANES:]


class _FfnStages:
    N_CHUNKS = D_FF // FF_CHUNK

    def __init__(self, x, ymix, wout_ref, ln1g_ref, ln1b_ref, wg_ref, wu_ref, wd_ref, ln2g_ref, ln2b_ref):
        self.x, self.ymix = x, ymix
        self.wout_ref, self.ln1g_ref, self.ln1b_ref = wout_ref, ln1g_ref, ln1b_ref
        self.wg_ref, self.wu_ref, self.wd_ref = wg_ref, wu_ref, wd_ref
        self.ln2g_ref, self.ln2b_ref = ln2g_ref, ln2b_ref

    def head(self):
        self.x1 = _layer_norm(DEEPNORM_ALPHA * self.x + _dot(self.ymix, self.wout_ref[...]),
                              self.ln1g_ref[...], self.ln1b_ref[...])
        self.x1b = self.x1.astype(BF16)
        self.acc = DEEPNORM_ALPHA * self.x1

    def _gate_up(self, c):
        cs = slice(c * FF_CHUNK, (c + 1) * FF_CHUNK)
        return _dot(self.x1b, self.wg_ref[:, cs]), _dot(self.x1b, self.wu_ref[:, cs])

    def chunk(self, c):
        if c == 0:
            self.gu = {}
        for n in range(c, min(c + FF_LOOKAHEAD, self.N_CHUNKS - 1) + 1):
            if n not in self.gu:
                self.gu[n] = self._gate_up(n)
        g, u = self.gu.pop(c)
        hid = (g * _sigmoid(g) * u).astype(BF16)
        self.acc = self.acc + _dot(hid, self.wd_ref[c * FF_CHUNK:(c + 1) * FF_CHUNK, :])

    def tail(self):
        return _layer_norm(self.acc, self.ln2g_ref[...], self.ln2b_ref[...])

    def run(self):
        self.head()
        for c in range(self.N_CHUNKS):
            self.chunk(c)
        return self.tail()


def _mix_blocks(zc, nb, pad_rows, thr_first, state, cbuf, wconv_ref, bconv_ref, wqk_ref, gb_ref,
                sinks_ref, gattn_ref, gml_ref, hook):
    rows = nb * BLOCK
    kp_lo, kp_hi, vp_lo, vp_hi, c_state, m_state = state
    c_state, m_state = list(c_state), list(m_state)
    lane = lax.broadcasted_iota(jnp.int32, (BLOCK, LANES), 1)
    lo_half = lane < HEAD_DIM
    row_valid = None
    if pad_rows:
        row_valid = lax.broadcasted_iota(jnp.int32, (rows, 1), 0) >= pad_rows

    s_all, vmasks = [], []
    for blk in range(nb):
        rs = slice(blk * BLOCK, (blk + 1) * BLOCK)
        kb = zc[rs, ATT_W:ATT_W + KV_W].astype(BF16)
        vb = zc[rs, ATT_W + KV_W:ATT_W + 2 * KV_W].astype(BF16)
        zero_b = jnp.zeros_like(kb)
        k_lo, k_hi = jnp.where(lo_half, kb, zero_b), jnp.where(lo_half, zero_b, kb)
        v_lo, v_hi = jnp.where(lo_half, vb, zero_b), jnp.where(lo_half, zero_b, vb)
        kmask = jnp.concatenate([kp_lo, k_lo, kp_hi, k_hi], axis=0)
        vmasks.append(jnp.concatenate([vp_lo, v_lo, vp_hi, v_hi], axis=0))
        q4 = jnp.concatenate([zc[rs, r * LANES:(r + 1) * LANES] for r in range(4)], axis=0).astype(BF16)
        s_all.append(_dot_nt(q4, kmask))
        kp_lo, kp_hi, vp_lo, vp_hi = k_lo, k_hi, v_lo, v_hi
    hook("scores")

    c_in = zc[:, COL_CONV:COL_VM]
    if row_valid is not None:
        c_in = jnp.where(row_valid, c_in, 0.0)
    cbuf[8:8 + rows, :] = c_in
    c_act = _conv_silu(lambda jj: cbuf[5 + jj:5 + jj + rows, :], wconv_ref, bconv_ref)
    cbuf[0:8, :] = cbuf[rows:rows + 8, :]
    hs = range(M_HEADS)
    ca_b = [c_act[:, h * M_HEAD_DIM:(h + 1) * M_HEAD_DIM].astype(BF16) for h in hs]
    qk = [_dot(ca_b[h], wqk_ref[h]) for h in hs]
    qm_b = [qk[h][:, :M_HEAD_DIM].astype(BF16) for h in hs]
    kmt_f = [(qk[h][:, M_HEAD_DIM:] * (M_HEAD_DIM ** -0.5)).T for h in hs]
    hook("front")

    ti = lax.broadcasted_iota(jnp.int32, (rows, rows), 0)
    si = lax.broadcasted_iota(jnp.int32, (rows, rows), 1)
    ltri = jnp.where((si <= ti) & ((si >> 7) == (ti >> 7)), 1.0, 0.0).astype(BF16)
    xg, _ = _gate_columns(zc[:, IN_MAIN:IN_PAD], gb_ref, row_valid, ltri)
    xgt = xg.T
    hook("gates")

    qi = lax.broadcasted_iota(jnp.int32, (BLOCK, BLOCK), 0)
    ks = lax.broadcasted_iota(jnp.int32, (BLOCK, BLOCK), 1)
    own = ks <= qi
    distf = jnp.where(own, qi - ks, WINDOW + qi - ks).astype(F32)
    zero_p = jnp.zeros((BLOCK, BLOCK), BF16)
    att_rows = []
    for blk in range(nb):
        live = None
        if blk == 0 and thr_first is not None:
            live = jnp.where(own, ks + BLOCK, ks) >= thr_first
        p_rows, inv = [], {}
        for r in range(4):
            pr = []
            for c in range(2):
                h = r + 4 * c
                s_blk = s_all[blk][r * LANES:(r + 1) * LANES, c * 256:(c + 1) * 256]
                sp = jnp.where(own, s_blk[:, BLOCK:], s_blk[:, :BLOCK]) * (HEAD_DIM ** -0.5) \
                    - ALIBI_SLOPES[h] * distf
                if live is not None:
                    sp = jnp.where(live, sp, -jnp.inf)
                sink = sinks_ref[h]
                mx = jnp.maximum(jnp.max(sp, axis=1, keepdims=True), sink)
                p = jnp.exp(sp - mx)
                inv[h] = 1.0 / (jnp.sum(p, axis=1, keepdims=True) + jnp.exp(sink - mx))
                p_b = p.astype(BF16)
                pr += [jnp.where(own, zero_p, p_b), jnp.where(own, p_b, zero_p)]
            p_rows.append(jnp.concatenate(pr, axis=1))
            hook("softmax", blk * 4 + r)
        o_all = _dot(jnp.concatenate(p_rows, axis=0), vmasks[blk])
        att_rows.append(jnp.concatenate(
            [o_all[r * LANES:(r + 1) * LANES] * jnp.where(lo_half, inv[r], inv[r + 4]) for r in range(4)], axis=1))
        hook("pv", blk)

    ones_col = jnp.where(lane == 0, 1.0, 0.0).astype(BF16)
    tb = lax.broadcasted_iota(jnp.int32, (BLOCK, BLOCK), 0)
    sb = lax.broadcasted_iota(jnp.int32, (BLOCK, BLOCK), 1)
    tri = sb <= tb
    hh_rows = [[] for _ in hs]
    for blk in range(nb):
        rs = slice(blk * BLOCK, (blk + 1) * BLOCK)
        b_c = [xg[rs, 4 + h:5 + h] for h in hs]
        b_r = [xgt[4 + h:5 + h, rs] for h in hs]
        li_r = [xgt[h:h + 1, rs] for h in hs]
        kmt = [kmt_f[h][:, rs] for h in hs]
        v_ext = [jnp.concatenate([zc[rs, COL_VM + h * M_HEAD_DIM:COL_VM + (h + 1) * M_HEAD_DIM].astype(BF16), ones_col], axis=1)
                 for h in hs]
        s_qk = [_dot(qm_b[h][rs], kmt[h].astype(BF16)) for h in hs]
        inter = [_dot(qm_b[h][rs], c_state[h].astype(BF16)) for h in hs]
        hook("mlstm_a", blk)
        dmat = [jnp.where(tri, b_c[h] - b_r[h] + li_r[h], -jnp.inf) for h in hs]
        m_inter = [b_c[h] + m_state[h] for h in hs]
        m_t = [jnp.maximum(m_inter[h], jnp.max(dmat[h], axis=1, keepdims=True)) for h in hs]
        w_inter = [jnp.exp(m_inter[h] - m_t[h]) for h in hs]
        sc_b = [(s_qk[h] * jnp.exp(dmat[h] - m_t[h])).astype(BF16) for h in hs]
        hook("mlstm_b", blk)
        nd = [_dot(sc_b[h], v_ext[h]) + w_inter[h] * inter[h] for h in hs]
        for h in hs:
            m_end = m_t[h][BLOCK - 1:BLOCK, :]
            b_last = b_c[h][BLOCK - 1:BLOCK, :]
            decay = jnp.exp(b_last + m_state[h] - m_end)
            wk_r = jnp.exp(b_last - b_r[h] + li_r[h] - m_end)
            c_state[h] = decay * c_state[h] + _dot((kmt[h] * wk_r).astype(BF16), v_ext[h])
            m_state[h] = m_end
            num, den = nd[h][:, :M_HEAD_DIM], nd[h][:, M_HEAD_DIM:M_HEAD_DIM + 1]
            hh_rows[h].append(num / jnp.maximum(jnp.abs(den), jnp.exp(-m_t[h])))
        hook("mlstm_c", blk)

    y_att = _attn_head_norm(jnp.concatenate(att_rows, axis=0), gattn_ref[...])
    hook("attnorm")
    y_m = []
    for h in hs:
        hm = _sigmoid(zc[:, COL_O + h * M_HEAD_DIM:COL_O + (h + 1) * M_HEAD_DIM]) * jnp.concatenate(hh_rows[h], axis=0)
        y_m.append(_mlstm_head_norm(hm, gml_ref[:, h * M_HEAD_DIM:(h + 1) * M_HEAD_DIM]))
        hook("headnorm", h)
    ymix = jnp.concatenate([y_att] + y_m, axis=1).astype(BF16)
    return ymix, (kp_lo, kp_hi, vp_lo, vp_hi, c_state, m_state)


def _project(xb, w_in_refs, lo=0, hi=IN_PAD):
    bounds = (0, ATT_W, IN_MAIN, IN_PAD)
    parts = []
    for ref, start, stop in zip(w_in_refs, bounds[:-1], bounds[1:]):
        a, b = max(lo, start), min(hi, stop)
        if a < b:
            parts.append(_dot(xb, ref[:, a - start:b - start]))
    return parts[0] if len(parts) == 1 else jnp.concatenate(parts, axis=1)


def _meta_state(meta_ref, w_inq_ref, w_inr_ref, w_ing_ref, wconv_ref, bconv_ref, wqk_ref, gb_ref,
                sinks_ref, gattn_ref,
                gml_ref, kplo_o, kphi_o, vplo_o, vphi_o, cb_o, ctn_o, mst_o, zc, cbuf):
    xb = jnp.concatenate([jnp.zeros((META_PAD, D_MODEL), F32), meta_ref[...]], axis=0)
    zc[...] = _project(xb.astype(BF16), (w_inq_ref, w_inr_ref, w_ing_ref))
    cbuf[0:8, :] = jnp.zeros((8, MLSTM_W), F32)
    zb = jnp.zeros((BLOCK, KV_W), BF16)
    state = (zb, zb, zb, zb, [jnp.zeros((M_HEAD_DIM, 2 * M_HEAD_DIM), F32)] * M_HEADS,
             [jnp.zeros((1, 1), F32)] * M_HEADS)
    _, (kp_lo, kp_hi, vp_lo, vp_hi, c_state, m_state) = _mix_blocks(
        zc, 1, META_PAD, BLOCK + META_PAD, state, cbuf, wconv_ref, bconv_ref, wqk_ref, gb_ref,
        sinks_ref, gattn_ref, gml_ref, lambda name, index=0: None)
    kplo_o[...] = kp_lo
    kphi_o[...] = kp_hi
    vplo_o[...] = vp_lo
    vphi_o[...] = vp_hi
    cb_o[...] = cbuf[0:8, :]
    for h in range(M_HEADS):
        ctn_o[h] = c_state[h]
        mst_o[h:h + 1, :] = jnp.broadcast_to(m_state[h], (1, LANES))
    mst_o[M_HEADS:8, :] = jnp.zeros((8 - M_HEADS, LANES), F32)


def _load_ffn_weights(wout_hbm, wg_hbm, wu_hbm, wd_hbm, wout_v, wg_v, wu_v, wd_v, stage_bufs, sems):
    half = D_FF // 2
    jobs = []
    for src, dst in ((wg_hbm, wg_v), (wu_hbm, wu_v)):
        jobs += [(src, r, WEIGHT_ROWS, c, half, dst, r) for r in range(0, D_MODEL, WEIGHT_ROWS) for c in (0, half)]
    jobs += [(wd_hbm, r, WEIGHT_ROWS, 0, D_MODEL, wd_v, r) for r in range(0, D_FF, WEIGHT_ROWS)]
    for h in range(N_HEADS):
        jobs.append((wout_hbm, h * HEAD_DIM, HEAD_DIM, 0, D_MODEL, wout_v, ((h % 4) * 2 + h // 4) * HEAD_DIM))
    jobs += [(wout_hbm, r, WEIGHT_ROWS, 0, D_MODEL, wout_v, r) for r in range(ATT_W, D_MODEL, WEIGHT_ROWS)]

    windows = []
    for buf in stage_bufs:
        views = [buf.at[s] for s in range(buf.shape[0])] if len(buf.shape) == 3 else [buf]
        for v in views:
            windows += [(v, r, c) for r in range(0, v.shape[0] - WEIGHT_ROWS + 1, WEIGHT_ROWS)
                        for c in range(0, v.shape[1] - half + 1, half)]
    nslot = len(windows)
    assert nslot <= sems.shape[0]

    def window(i, nrows, ncols):
        v, r, c = windows[i % nslot]
        return v.at[pl.ds(r, nrows), pl.ds(c, ncols)]

    def copy(i):
        src, r0, nrows, c0, ncols, _, _ = jobs[i]
        return pltpu.make_async_copy(src.at[pl.ds(r0, nrows), pl.ds(c0, ncols)], window(i, nrows, ncols),
                                     sems.at[i % nslot])

    for i in range(min(nslot - 1, len(jobs))):
        copy(i).start()
    for i, (_, _, nrows, c0, ncols, dst, d0) in enumerate(jobs):
        if i + nslot - 1 < len(jobs):
            copy(i + nslot - 1).start()
        copy(i).wait()
        dst[d0:d0 + nrows, c0:c0 + ncols] = window(i, nrows, ncols)[...].astype(BF16)


def _pair_kernel(xin_ref, xres_ref, meta_ref, xs_ref, ymix_s_ref,
                 w_inq_ref, w_inr_ref, w_ing_ref, wconv_ref, bconv_ref, wqk_ref, bi_ref, bf_ref, sinks_ref,
                 gattn_ref, gml_ref,
                 wout_hbm, ln1g_ref, ln1b_ref, wg_hbm, wu_hbm, wd_hbm, ln2g_ref, ln2b_ref,
                 y_ref, ys_ref, pk_ref, pv_ref, pconv_ref, pc_ref, pn_ref, pm_ref,
                 wout_v, wg_v, wu_v, wd_v, wstage, wsems,
                 kplo_i, kphi_i, vplo_i, vphi_i, cb_i, ctn_i, mst_i,
                 zcur, znext, kvlast, kplo, kphi, vplo, vphi, cbuf, ctn, mst, yprev, *, npair, n_pair_steps):
    t = pl.program_id(0)
    p = lax.rem(t + (npair - 1), npair)
    gb_ref = (bi_ref, bf_ref)

    @pl.when(t == 0)
    def _init_pipeline():
        _load_ffn_weights(wout_hbm, wg_hbm, wu_hbm, wd_hbm, wout_v, wg_v, wu_v, wd_v,
                          (wstage, zcur, znext), wsems)
        zcur[...] = jnp.zeros_like(zcur)
        yprev[...] = jnp.zeros_like(yprev)
        _meta_state(meta_ref, w_inq_ref, w_inr_ref, w_ing_ref, wconv_ref, bconv_ref, wqk_ref, gb_ref,
                    sinks_ref, gattn_ref, gml_ref, kplo_i, kphi_i, vplo_i, vphi_i, cb_i, ctn_i, mst_i,
                    znext.at[pl.ds(0, BLOCK)], cbuf)

    @pl.when((p == 0) | (t == 0))
    def _load_meta_state():
        kplo[...] = kplo_i[...]
        kphi[...] = kphi_i[...]
        vplo[...] = vplo_i[...]
        vphi[...] = vphi_i[...]
        cbuf[0:8, :] = cb_i[...]
        ctn[...] = ctn_i[...]
        mst[...] = mst_i[...]

    ffn_w = (wout_v, ln1g_ref, ln1b_ref, wg_v, wu_v, wd_v, ln2g_ref, ln2b_ref)

    @pl.when(t < n_pair_steps)
    def _pair_step():
        ffn = _FfnStages(xres_ref[...], yprev[...], *ffn_w)
        xb_b = xin_ref[...].astype(BF16)

        n_proj = -(-IN_PAD // PROJ_CHUNK)

        def proj(n):
            lo, hi = n * PROJ_CHUNK, min((n + 1) * PROJ_CHUNK, IN_PAD)
            znext[:, lo:hi] = _project(xb_b, (w_inq_ref, w_inr_ref, w_ing_ref), lo, hi)

        proj_it, ffn_it = iter(range(n_proj)), iter(range(ffn.N_CHUNKS))

        def fill_proj(n):
            for c in [c for _, c in zip(range(n), proj_it)]:
                proj(c)

        def fill_ffn(n):
            for c in [c for _, c in zip(range(n), ffn_it)]:
                ffn.chunk(c)

        out = {}

        def hook(name, index=0):
            if name == "scores":
                fill_proj(2)
                ffn.head()
            elif name == "front":
                fill_proj(1)
            elif name == "gates":
                fill_ffn(1)
            elif name == "softmax":
                if index % 2 == 1:
                    fill_ffn(1)
            elif name in ("pv", "mlstm_a", "mlstm_b"):
                fill_ffn(1)
            elif name == "mlstm_c":
                fill_proj(1)
            elif name == "attnorm":
                fill_ffn(ffn.N_CHUNKS)
                out["y"] = ffn.tail()
                fill_proj(1)
            elif name == "headnorm":
                fill_proj(1)

        state = (kplo[...], kphi[...], vplo[...], vphi[...], [ctn[h] for h in range(M_HEADS)],
                 [mst[h:h + 1, 0:1] for h in range(M_HEADS)])
        thr_first = jnp.where(p == 0, META_PAD, 0)
        kvlast[...] = zcur[(STEP_BLOCKS - 1) * BLOCK:STEP_BLOCKS * BLOCK, ATT_W:ATT_W + 2 * KV_W]
        ymix, (kp_lo, kp_hi, vp_lo, vp_hi, c_state, m_state) = _mix_blocks(
            zcur, STEP_BLOCKS, 0, thr_first, state, cbuf, wconv_ref, bconv_ref, wqk_ref, gb_ref,
            sinks_ref, gattn_ref, gml_ref, hook)
        fill_proj(n_proj)
        y_ref[...] = out["y"]
        kplo[...] = kp_lo
        kphi[...] = kp_hi
        vplo[...] = vp_lo
        vphi[...] = vp_hi
        for h in range(M_HEADS):
            ctn[h] = c_state[h]
            mst[h:h + 1, :] = jnp.broadcast_to(m_state[h], (1, LANES))
        yprev[...] = ymix
        zcur[...] = znext[...]

    @pl.when(t >= n_pair_steps)
    def _sample_ffn_step():
        ys_ref[...] = _FfnStages(xs_ref[...], ymix_s_ref[...], *ffn_w).run()

    @pl.when((p == npair - 1) & (t > 0) & (t < n_pair_steps))
    def _final():
        pk_ref[...] = kvlast[:, 0:KV_W].T
        pv_ref[...] = kvlast[:, KV_W:2 * KV_W].T
        seq = lax.div(t - 1, npair)
        for j in range(CONV_W - 1):
            pconv_ref[j, pl.ds(seq, 1), :] = cbuf[8 - (CONV_W - 1) + j:8 - (CONV_W - 1) + j + 1, :]
        head_lane = lax.broadcasted_iota(jnp.int32, (1, LANES), 1)
        m_row = jnp.zeros((1, LANES), F32)
        for h in range(M_HEADS):
            m_row = jnp.where(head_lane == h, mst[h:h + 1, :], m_row)
        pm_ref[pl.ds(seq, 1), :] = m_row[:, 0:M_HEADS]
        for h in range(M_HEADS):
            c_n = ctn[h].T
            pc_ref[h] = c_n[0:M_HEAD_DIM, :]
            pn_ref[h:h + 1, :] = c_n[M_HEAD_DIM:M_HEAD_DIM + 1, :]


def _sample_kernel(xs_ref, ck_ref, cv_ref, sconv_ref, sc_ref, sn_ref, sm_ref,
                   w_inq_ref, w_inr_ref, w_ing_ref, wconv_ref, bconv_ref, wqk_ref, bi_ref, bf_ref, sinks_ref,
                   gattn_ref, gml_ref,
                   ymix_ref, sk_ref, sv_ref, scv_ref, sco_ref, sno_ref, smo_ref, cext):
    ns, L = SEQ_TILE, 8
    gb_ref = (bi_ref, bf_ref)
    xs = xs_ref[...].reshape(ns * L, D_MODEL)
    z = _project(xs.astype(BF16), (w_inq_ref, w_inr_ref, w_ing_ref))

    lane = lax.broadcasted_iota(jnp.int32, (BLOCK, LANES), 1)
    lo_half = lane < HEAD_DIM
    lo3 = lax.broadcasted_iota(jnp.int32, (ns, L, LANES), 2) < HEAD_DIM
    ti = lax.broadcasted_iota(jnp.int32, (BLOCK, BLOCK), 0)
    si = lax.broadcasted_iota(jnp.int32, (BLOCK, BLOCK), 1)
    same_seq = (ti >> 3) == (si >> 3)
    tl, sl_ = ti & 7, si & 7
    causal_seq = same_seq & (sl_ <= tl)

    k_new = z[:, ATT_W:ATT_W + KV_W]
    v_new = z[:, ATT_W + KV_W:ATT_W + 2 * KV_W]
    kb, vb = k_new.astype(BF16), v_new.astype(BF16)
    zero_b = jnp.zeros_like(kb)
    kn_mask = jnp.concatenate([jnp.where(lo_half, kb, zero_b), jnp.where(lo_half, zero_b, kb)], axis=0)
    vn_mask = jnp.concatenate([jnp.where(lo_half, vb, zero_b), jnp.where(lo_half, zero_b, vb)], axis=0)
    qg = [z[:, r * LANES:(r + 1) * LANES] for r in range(4)]
    s_new = _dot_nt(jnp.concatenate(qg, axis=0).astype(BF16), kn_mask)
    qg3 = [q.reshape(ns, L, LANES) for q in qg]
    q8 = jnp.concatenate([jnp.where(lo3, q, 0.0) for q in qg3] + [jnp.where(lo3, 0.0, q) for q in qg3],
                         axis=1).astype(BF16)
    s_cache = jnp.einsum('nqd,ndk->nqk', q8, ck_ref[...].astype(BF16), preferred_element_type=F32)

    c3 = z[:, COL_CONV:COL_VM].reshape(ns, L, MLSTM_W)
    cext[:, 8:16, :] = c3
    for j in range(CONV_W - 1):
        cext[:, 8 - (CONV_W - 1) + j, :] = sconv_ref[j]
        scv_ref[j] = cext[:, 16 - (CONV_W - 1) + j, :]
    c_act = _conv_silu(lambda jj: cext[:, 5 + jj:5 + jj + L, :], wconv_ref, bconv_ref)
    c_act = c_act.reshape(BLOCK, MLSTM_W)
    lblk = jnp.where(causal_seq, 1.0, 0.0).astype(BF16)
    lseq = jnp.where(same_seq, 1.0, 0.0).astype(BF16)
    xg, blast = _gate_columns(z[:, IN_MAIN:IN_PAD], gb_ref, None, lblk, extra=lseq)
    xgt = xg.T
    m_rep = jnp.broadcast_to(sm_ref[...], (ns, L, LANES)).reshape(BLOCK, LANES)
    hs = range(M_HEADS)
    qk = [_dot(c_act[:, h * M_HEAD_DIM:(h + 1) * M_HEAD_DIM].astype(BF16), wqk_ref[h]) for h in hs]
    qm = [qk[h][:, :M_HEAD_DIM] for h in hs]
    qm_b = [qm[h].astype(BF16) for h in hs]
    km = [qk[h][:, M_HEAD_DIM:] * (M_HEAD_DIM ** -0.5) for h in hs]
    km_b = [km[h].astype(BF16) for h in hs]
    vh_b = [z[:, COL_VM + h * M_HEAD_DIM:COL_VM + (h + 1) * M_HEAD_DIM].astype(BF16) for h in hs]
    c_old = [sc_ref[:, h] for h in hs]
    s_qk = [_dot_nt(qm_b[h], km_b[h]) for h in hs]
    cq = [_dot_nt(qm_b[h], c_old[h].reshape(ns * 128, 128).astype(BF16)) for h in hs]

    mask_c = si > tl
    dist_c = (WINDOW + tl - si).astype(F32)
    dist_n = (tl - sl_).astype(F32)
    pc_parts, pn_rows, inv = {}, [], {}
    for r in range(4):
        pn = []
        for c in range(2):
            h = r + 4 * c
            sc_c = s_cache[:, c * 32 + r * L:c * 32 + (r + 1) * L, :].reshape(BLOCK, LANES)
            sc_n = s_new[r * LANES:(r + 1) * LANES, c * BLOCK:(c + 1) * BLOCK]
            (p_c, p_n), inv[h] = _softmax_parts(
                [(sc_c, mask_c, -ALIBI_SLOPES[h] * dist_c), (sc_n, causal_seq, -ALIBI_SLOPES[h] * dist_n)],
                sinks_ref[h])
            pc_parts[(c, r)] = p_c.reshape(ns, L, LANES)
            pn.append(p_n.astype(BF16))
        pn_rows.append(jnp.concatenate(pn, axis=1))
    p8 = jnp.concatenate([pc_parts[(c, r)] for c in range(2) for r in range(4)], axis=1).astype(BF16)
    o_cache = jnp.einsum('nqk,ndk->nqd', p8, cv_ref[...].astype(BF16), preferred_element_type=F32)
    o_new = _dot(jnp.concatenate(pn_rows, axis=0), vn_mask)
    groups = []
    for r in range(4):
        oa = o_cache[:, r * L:(r + 1) * L, :].reshape(BLOCK, LANES)
        ob = o_cache[:, 32 + r * L:32 + (r + 1) * L, :].reshape(BLOCK, LANES)
        o = jnp.where(lo_half, oa, ob) + o_new[r * LANES:(r + 1) * LANES]
        groups.append(o * jnp.where(lo_half, inv[r], inv[r + 4]))
    y_att = _attn_head_norm(jnp.concatenate(groups, axis=1), gattn_ref[...])

    sk_ref[:, 0:WINDOW - L, :] = jnp.swapaxes(ck_ref[...], 1, 2)[:, L:WINDOW, :]
    sk_ref[:, WINDOW - L:WINDOW, :] = k_new.reshape(ns, L, LANES)
    sv_ref[:, 0:WINDOW - L, :] = jnp.swapaxes(cv_ref[...], 1, 2)[:, L:WINDOW, :]
    sv_ref[:, WINDOW - L:WINDOW, :] = v_new.reshape(ns, L, LANES)

    b_c = [xg[:, 4 + h:5 + h] for h in hs]
    li_c = [xg[:, h:h + 1] for h in hs]
    b_r = [xgt[4 + h:5 + h, :] for h in hs]
    li_r = [xgt[h:h + 1, :] for h in hs]
    bl_c = [blast[:, 4 + h:5 + h] for h in hs]
    m_prev = [m_rep[:, h:h + 1] for h in hs]
    dmat = [jnp.where(causal_seq, b_c[h] - b_r[h] + li_r[h], -jnp.inf) for h in hs]
    m_inter = [b_c[h] + m_prev[h] for h in hs]
    m_t = [jnp.maximum(m_inter[h], jnp.max(dmat[h], axis=1, keepdims=True)) for h in hs]
    w_inter = [jnp.exp(m_inter[h] - m_t[h]) for h in hs]
    sc = [s_qk[h] * jnp.exp(dmat[h] - m_t[h]) for h in hs]
    num_intra = [_dot(sc[h].astype(BF16), vh_b[h]) for h in hs]

    m_end3 = [jnp.broadcast_to(m_t[h], (BLOCK, LANES)).reshape(ns, L, LANES)[:, L - 1:L, :] for h in hs]
    m_end = [jnp.broadcast_to(m_end3[h], (ns, L, LANES)).reshape(BLOCK, LANES)[:, 0:1] for h in hs]
    decay = [jnp.exp(bl_c[h] + m_prev[h] - m_end[h]) for h in hs]
    kw = [km[h] * jnp.exp(bl_c[h] - b_c[h] + li_c[h] - m_end[h]) for h in hs]
    seq_of_row = ti >> 3
    lane3 = lax.broadcasted_iota(jnp.int32, (ns, 1, LANES), 2)
    m_out = jnp.zeros((ns, 1, LANES), F32)
    for h in hs:
        kw_b = kw[h].astype(BF16)
        k_big = jnp.concatenate(
            [jnp.where(seq_of_row == g, kw_b, jnp.zeros_like(kw_b)) for g in range(ns)], axis=1)
        c_up = _dot_tn(vh_b[h], k_big)
        c_up3 = jnp.stack([c_up[:, g * M_HEAD_DIM:(g + 1) * M_HEAD_DIM] for g in range(ns)], axis=0)
        decay3 = jnp.broadcast_to(decay[h], (BLOCK, LANES)).reshape(ns, L, LANES)[:, 0:1, :]
        n_old = sn_ref[:, h:h + 1, :]
        sco_ref[:, h] = decay3 * c_old[h] + c_up3
        sno_ref[:, h:h + 1, :] = decay3 * n_old + jnp.sum(kw[h].reshape(ns, L, LANES), axis=1, keepdims=True)
        m_out = jnp.where(lane3 == h, m_end3[h], m_out)
    smo_ref[...] = m_out

    y_m = []
    for h in hs:
        num_inter = jnp.concatenate([cq[h][g * L:(g + 1) * L, g * M_HEAD_DIM:(g + 1) * M_HEAD_DIM] for g in range(ns)], axis=0)
        n_rep = jnp.broadcast_to(sn_ref[:, h:h + 1, :], (ns, L, LANES)).reshape(BLOCK, LANES)
        num = num_intra[h] + w_inter[h] * num_inter
        den = jnp.sum(sc[h], axis=1, keepdims=True) + w_inter[h] * jnp.sum(qm[h] * n_rep, axis=1, keepdims=True)
        hh = num / jnp.maximum(jnp.abs(den), jnp.exp(-m_t[h]))
        hm = _sigmoid(z[:, COL_O + h * M_HEAD_DIM:COL_O + (h + 1) * M_HEAD_DIM]) * hh
        y_m.append(_mlstm_head_norm(hm, gml_ref[:, h * M_HEAD_DIM:(h + 1) * M_HEAD_DIM]))

    ymix_ref[...] = jnp.concatenate([y_att] + y_m, axis=1).astype(BF16)


def _const_spec(shape):
    nd = len(shape)
    return pl.BlockSpec(shape, lambda *_: (0,) * nd, pipeline_mode=pl.Buffered(1))


def _mixer_weight_specs():
    return [
        _const_spec((D_MODEL, ATT_W)),
        _const_spec((D_MODEL, IN_MAIN - ATT_W)),
        _const_spec((D_MODEL, IN_PAD - IN_MAIN)),
        _const_spec((CONV_W, MLSTM_W)),
        _const_spec((1, MLSTM_W)),
        _const_spec((M_HEADS, M_HEAD_DIM, 2 * M_HEAD_DIM)),
        pl.BlockSpec(memory_space=pltpu.SMEM),
        pl.BlockSpec(memory_space=pltpu.SMEM),
        pl.BlockSpec(memory_space=pltpu.SMEM),
        _const_spec((1, ATT_W)),
        _const_spec((1, MLSTM_W)),
    ]


def _post_weight_specs(d):
    hbm = pl.BlockSpec(memory_space=pl.ANY)
    return [hbm, _const_spec((1, d)), _const_spec((1, d)), hbm, hbm, hbm, _const_spec((1, d)), _const_spec((1, d))]


def _prompt_layer(x_prompt, meta_tokens, x_sample_rows, ymix_sample, mixer_w, post_w):
    b, s, d = x_prompt.shape
    state_shapes = [((BLOCK, KV_W), BF16)] * 4 + [
        ((8, MLSTM_W), F32), ((M_HEADS, M_HEAD_DIM, 2 * M_HEAD_DIM), F32), ((8, LANES), F32)]
    rows = STEP_BLOCKS * BLOCK
    npair = s // rows
    total = b * npair
    n_pair_steps = total + 2
    n_sample_steps = x_sample_rows.shape[0] // rows
    steps = n_pair_steps + n_sample_steps

    def block_map(lag):
        def index(t):
            tb = jnp.clip(t - lag, 0, total - 1)
            return (tb // npair, tb % npair, 0)
        return index

    def smap(t):
        return (jnp.clip(t - 1, 0, total - 1) // npair, 0, 0)

    def sample_map(t):
        return (jnp.clip(t - n_pair_steps, 0, n_sample_steps - 1), 0)

    out_shape = (
        jax.ShapeDtypeStruct((b, s, d), F32),
        jax.ShapeDtypeStruct(x_sample_rows.shape, F32),
        jax.ShapeDtypeStruct((b, BLOCK, KV_W), F32),
        jax.ShapeDtypeStruct((b, BLOCK, KV_W), F32),
        jax.ShapeDtypeStruct((CONV_W - 1, b, MLSTM_W), F32),
        jax.ShapeDtypeStruct((b, M_HEADS, M_HEAD_DIM, M_HEAD_DIM), F32),
        jax.ShapeDtypeStruct((b, M_HEADS, M_HEAD_DIM), F32),
        jax.ShapeDtypeStruct((b, M_HEADS), F32),
    )
    out_specs = (
        pl.BlockSpec((None, rows, d), block_map(2)),
        pl.BlockSpec((rows, d), sample_map),
        pl.BlockSpec((None, BLOCK, KV_W), smap),
        pl.BlockSpec((None, BLOCK, KV_W), smap),
        pl.BlockSpec((CONV_W - 1, b, MLSTM_W), lambda t: (0, 0, 0)),
        pl.BlockSpec((None, M_HEADS, M_HEAD_DIM, M_HEAD_DIM), lambda t: smap(t) + (0,)),
        pl.BlockSpec((None, M_HEADS, M_HEAD_DIM), smap),
        pl.BlockSpec((b, M_HEADS), lambda t: (0, 0)),
    )
    return pl.pallas_call(
        functools.partial(_pair_kernel, npair=npair, n_pair_steps=n_pair_steps),
        grid=(steps,),
        in_specs=[pl.BlockSpec((None, rows, d), block_map(0)), pl.BlockSpec((None, rows, d), block_map(2)),
                  _const_spec((N_META, d)), pl.BlockSpec((rows, d), sample_map), pl.BlockSpec((rows, d), sample_map)]
        + _mixer_weight_specs() + _post_weight_specs(d),
        out_specs=out_specs,
        out_shape=out_shape,
        scratch_shapes=[
            pltpu.VMEM((d, d), BF16), pltpu.VMEM((d, D_FF), BF16), pltpu.VMEM((d, D_FF), BF16),
            pltpu.VMEM((D_FF, d), BF16),
            pltpu.VMEM((WEIGHT_SLOTS, WEIGHT_ROWS, D_FF), F32),
            pltpu.SemaphoreType.DMA((2 * WEIGHT_SLOTS + 2 * STEP_BLOCKS,)),
        ] + [pltpu.VMEM(shape, dt) for shape, dt in state_shapes] + [
            pltpu.VMEM((rows, IN_PAD), F32),
            pltpu.VMEM((rows, IN_PAD), F32),
            pltpu.VMEM((BLOCK, 2 * KV_W), F32),
        ] + [pltpu.VMEM((BLOCK, KV_W), BF16)] * 4 + [
            pltpu.VMEM((rows + 8, MLSTM_W), F32),
            pltpu.VMEM((M_HEADS, M_HEAD_DIM, 2 * M_HEAD_DIM), F32),
            pltpu.VMEM((8, LANES), F32),
            pltpu.VMEM((rows, d), BF16),
        ],
        compiler_params=pltpu.CompilerParams(
            dimension_semantics=("arbitrary",), vmem_limit_bytes=VMEM_LIMIT),
        name="prompt_layer",
    )(x_prompt, x_prompt, meta_tokens, x_sample_rows, ymix_sample, *mixer_w, *post_w)


def _sample_mixer(x_sample, ck, cv, s_conv, s_c, s_n, s_m3, mixer_w):
    n, l, d = x_sample.shape
    t = SEQ_TILE
    m3 = lambda i: (i, 0, 0)
    m4 = lambda i: (i, 0, 0, 0)
    out_shape = (
        jax.ShapeDtypeStruct((n * l, d), BF16),
        jax.ShapeDtypeStruct((n, WINDOW, KV_W), F32),
        jax.ShapeDtypeStruct((n, WINDOW, KV_W), F32),
        jax.ShapeDtypeStruct((CONV_W - 1, n, MLSTM_W), F32),
        jax.ShapeDtypeStruct((n, M_HEADS, M_HEAD_DIM, M_HEAD_DIM), F32),
        jax.ShapeDtypeStruct((n, M_HEADS, M_HEAD_DIM), F32),
        jax.ShapeDtypeStruct((n, 1, LANES), F32),
    )
    out_specs = (
        pl.BlockSpec((t * l, d), lambda i: (i, 0)),
        pl.BlockSpec((t, WINDOW, KV_W), m3),
        pl.BlockSpec((t, WINDOW, KV_W), m3),
        pl.BlockSpec((CONV_W - 1, t, MLSTM_W), lambda i: (0, i, 0)),
        pl.BlockSpec((t, M_HEADS, M_HEAD_DIM, M_HEAD_DIM), m4),
        pl.BlockSpec((t, M_HEADS, M_HEAD_DIM), m3),
        pl.BlockSpec((t, 1, LANES), m3),
    )
    in_specs = [
        pl.BlockSpec((t, l, d), m3),
        pl.BlockSpec((t, WINDOW, KV_W), m3),
        pl.BlockSpec((t, WINDOW, KV_W), m3),
        pl.BlockSpec((CONV_W - 1, t, MLSTM_W), lambda i: (0, i, 0)),
        pl.BlockSpec((t, M_HEADS, M_HEAD_DIM, M_HEAD_DIM), m4),
        pl.BlockSpec((t, M_HEADS, M_HEAD_DIM), m3),
        pl.BlockSpec((t, 1, LANES), m3),
    ] + _mixer_weight_specs()
    return pl.pallas_call(
        _sample_kernel,
        grid=(n // t,),
        in_specs=in_specs,
        out_specs=out_specs,
        out_shape=out_shape,
        scratch_shapes=[pltpu.VMEM((t, 16, MLSTM_W), F32)],
        compiler_params=pltpu.CompilerParams(
            dimension_semantics=("arbitrary",), vmem_limit_bytes=VMEM_LIMIT),
        name="sample_mixer",
    )(x_sample, ck, cv, s_conv, s_c, s_n, s_m3, *mixer_w)


def kernel(x_prompt, x_sample, cache_k, cache_v, state_conv, state_C, state_n, state_m, meta_tokens,
           w_in, w_conv, b_conv, w_mq, w_mk, b_i, b_f, attn_sinks, g_attn, g_mlstm, w_out,
           ln1_g, ln1_b, w_gate, w_up, w_down, ln2_g, ln2_b):
    b, s, d = x_prompt.shape
    n, l, _ = x_sample.shape

    w0 = w_in[0]
    w_in_parts = (_pair_heads(w0[:, :ATT_W], 1).astype(BF16), w0[:, ATT_W:IN_MAIN].astype(BF16),
                  jnp.pad(w0[:, IN_MAIN:], ((0, 0), (0, IN_PAD - IN_MAIN - IN_GATES))).astype(BF16))
    wqk = jnp.concatenate([w_mq[0], w_mk[0]], axis=-1).astype(BF16)
    mixer_w = w_in_parts + (w_conv[0], b_conv[0].reshape(1, MLSTM_W), wqk, b_i[0], b_f[0], attn_sinks[0],
                            _pair_heads(g_attn[0].reshape(1, ATT_W), 1), g_mlstm[0].reshape(1, MLSTM_W))
    post_w = (w_out[0], ln1_g[0].reshape(1, d), ln1_b[0].reshape(1, d), w_gate[0], w_up[0], w_down[0],
              ln2_g[0].reshape(1, d), ln2_b[0].reshape(1, d))

    key_minor = lambda c: c[0].transpose(0, 2, 3, 1).reshape(n, KV_W, WINDOW)
    s_m3 = jnp.pad(state_m[0], ((0, 0), (0, LANES - M_HEADS))).reshape(n, 1, LANES)
    ymix_s, sk, sv, scv, s_c, s_n, s_mo = _sample_mixer(
        x_sample, key_minor(cache_k), key_minor(cache_v), state_conv[0].swapaxes(0, 1), state_C[0], state_n[0],
        s_m3, mixer_w)

    y_prompt, y_sample, pk, pv, pconv, p_c, p_n, p_m = _prompt_layer(
        x_prompt, meta_tokens, x_sample.reshape(n * l, d), ymix_s, mixer_w, post_w)

    kv5 = lambda a: a.reshape(1, a.shape[0], WINDOW, 2, HEAD_DIM)
    kv5_key_minor = lambda a: a.reshape(a.shape[0], 2, HEAD_DIM, WINDOW).transpose(0, 3, 1, 2)[None]
    return (y_prompt, y_sample.reshape(n, l, d),
            kv5_key_minor(pk), kv5_key_minor(pv), pconv.swapaxes(0, 1)[None], p_c[None], p_n[None], p_m[None],
            kv5(sk), kv5(sv), scv.swapaxes(0, 1)[None], s_c[None], s_n[None], s_mo[None, :, 0, :M_HEADS])
```

```python
import functools

import jax
import jax.numpy as jnp
from jax import lax
from jax.experimental import pallas as pl
from jax.experimental.pallas import tpu as pltpu

F32 = jnp.float32
BF16 = jnp.bfloat16

D_MODEL = 1024
ATT_W = 512
MLSTM_W = 512
HEAD_DIM = 64
N_HEADS = 8
KV_W = 128
WINDOW = 128
BLOCK = 128
M_HEADS = 4
M_HEAD_DIM = 128
CONV_W = 4
N_META = 16
META_PAD = BLOCK - N_META
D_FF = 2816
LANES = 128
COL_CONV = ATT_W + 2 * KV_W
COL_VM = COL_CONV + MLSTM_W
COL_O = COL_VM + MLSTM_W
IN_MAIN = COL_O + MLSTM_W
IN_GATES = 2 * M_HEADS
IN_PAD = IN_MAIN + LANES
DEPTH = 1
ALIBI_SLOPES = tuple(2.0 ** (-8.0 * (h + 1) / N_HEADS) for h in range(N_HEADS))
DEEPNORM_ALPHA = (2.0 * DEPTH) ** 0.25
EPS = 1e-5
SEQ_TILE = 16
STEP_BLOCKS = 2
FF_CHUNK = 256
FF_LOOKAHEAD = 3
PROJ_CHUNK = 256
WEIGHT_ROWS = 128
WEIGHT_SLOTS = 4
VMEM_LIMIT = 56 * 1024 * 1024


def _pair_heads(a, axis):
    half = N_HEADS // 2
    heads = [lax.slice_in_dim(a, h * HEAD_DIM, (h + 1) * HEAD_DIM, axis=axis)
             for r in range(half) for h in (r, r + half)]
    return jnp.concatenate(heads, axis=axis)


def _paired_q_group(q_group, r):
    a, b = q_group(r // 2), q_group(r // 2 + N_HEADS // 4)
    lo_half = lax.broadcasted_iota(jnp.int32, a.shape, 1) < HEAD_DIM
    if r % 2 == 0:
        return jnp.where(lo_half, a, pltpu.roll(b, HEAD_DIM, 1))
    return jnp.where(lo_half, pltpu.roll(a, HEAD_DIM, 1), b)


def _dot(a, b):
    return jnp.dot(a, b, preferred_element_type=F32)


def _dot_nt(a, b):
    return lax.dot_general(a, b, (((1,), (1,)), ((), ())), preferred_element_type=F32)


def _dot_tn(a, b):
    return lax.dot_general(a, b, (((0,), (0,)), ((), ())), preferred_element_type=F32)


def _split2(x):
    hi = x.astype(BF16)
    lo = (x - hi.astype(F32)).astype(BF16)
    return hi, lo


def _sigmoid(x):
    return 1.0 / (1.0 + jnp.exp(-x))


def _log_sigmoid(x):
    return -(jnp.maximum(-x, 0.0) + jnp.log1p(jnp.exp(-jnp.abs(x))))


def _layer_norm(x, g, b):
    mu = jnp.mean(x, axis=-1, keepdims=True)
    xc = x - mu
    var = jnp.mean(xc * xc, axis=-1, keepdims=True)
    return xc * lax.rsqrt(var + EPS) * g + b


def _attn_head_norm(att, g):
    lo_half = lax.broadcasted_iota(jnp.int32, (att.shape[0], LANES), 1) < HEAD_DIM

    def seg_mean(x):
        s_all = jnp.sum(x, axis=1, keepdims=True)
        s_lo = jnp.sum(jnp.where(lo_half, x, 0.0), axis=1, keepdims=True)
        return jnp.where(lo_half, s_lo, s_all - s_lo) * (1.0 / HEAD_DIM)

    out = []
    for grp in range(att.shape[1] // LANES):
        x = att[:, grp * LANES:(grp + 1) * LANES]
        xc = x - seg_mean(x)
        out.append(xc * lax.rsqrt(seg_mean(xc * xc) + EPS))
    return jnp.concatenate(out, axis=1) * g


def _mlstm_head_norm(hm, g):
    mu = jnp.mean(hm, axis=-1, keepdims=True)
    xc = hm - mu
    var = jnp.mean(xc * xc, axis=-1, keepdims=True)
    return xc * lax.rsqrt(var + EPS) * g


def _softmax_parts(parts, sink):
    sp = [jnp.where(m, s * (HEAD_DIM ** -0.5) + a, -jnp.inf) for s, m, a in parts]
    mx = sink
    for s in sp:
        mx = jnp.maximum(mx, jnp.max(s, axis=1, keepdims=True))
    ps = [jnp.exp(s - mx) for s in sp]
    den = jnp.exp(sink - mx)
    for p in ps:
        den = den + jnp.sum(p, axis=1, keepdims=True)
    return ps, 1.0 / den


def _conv_silu(window, wconv_ref, bconv_ref):
    acc = bconv_ref[...]
    for j in range(CONV_W):
        acc = acc + window(j) * wconv_ref[j:j + 1, :]
    return acc * _sigmoid(acc)


def _gate_columns(gates, gb_ref, row_valid, ltri, extra=None):
    lane = lax.broadcasted_iota(jnp.int32, gates.shape, 1)
    bi_ref, bf_ref = gb_ref
    lane1 = lax.broadcasted_iota(jnp.int32, (1, LANES), 1)
    bias = jnp.zeros((1, LANES), F32)
    for h in range(M_HEADS):
        bias = jnp.where(lane1 == h, bi_ref[h], jnp.where(lane1 == M_HEADS + h, bf_ref[h], bias))
    gb = gates + bias
    is_i = lane < M_HEADS
    is_f = (lane >= M_HEADS) & (lane < 2 * M_HEADS)
    logf = _log_sigmoid(gb)
    if row_valid is not None:
        lf = jnp.where(is_f & row_valid, logf, 0.0)
        li = jnp.where(row_valid, gb, -jnp.inf)
    else:
        lf = jnp.where(is_f, logf, 0.0)
        li = gb
    hi_lo = jnp.concatenate(_split2(lf), axis=1)
    bc = _dot(ltri, hi_lo)
    x = jnp.where(is_i, li, bc[:, :LANES] + bc[:, LANES:])
    if extra is None:
        return x, None
    be = _dot(extra, hi_lo)
    return x, be[:, :LANES] + be[:, LANES:]


class _FfnStages:
    N_CHUNKS = D_FF // FF_CHUNK

    def __init__(self, x, ymix, wout_ref, ln1g_ref, ln1b_ref, wg_ref, wu_ref, wd_ref, ln2g_ref, ln2b_ref):
        self.x, self.ymix = x, ymix
        self.wout_ref, self.ln1g_ref, self.ln1b_ref = wout_ref, ln1g_ref, ln1b_ref
        self.wg_ref, self.wu_ref, self.wd_ref = wg_ref, wu_ref, wd_ref
        self.ln2g_ref, self.ln2b_ref = ln2g_ref, ln2b_ref

    def head(self):
        self.x1 = _layer_norm(DEEPNORM_ALPHA * self.x + _dot(self.ymix, self.wout_ref[...]),
                              self.ln1g_ref[...], self.ln1b_ref[...])
        self.x1b = self.x1.astype(BF16)
        self.acc = DEEPNORM_ALPHA * self.x1

    def _gate_up(self, c):
        cs = slice(c * FF_CHUNK, (c + 1) * FF_CHUNK)
        return _dot(self.x1b, self.wg_ref[:, cs]), _dot(self.x1b, self.wu_ref[:, cs])

    def chunk(self, c):
        if c == 0:
            self.gu = {}
        for n in range(c, min(c + FF_LOOKAHEAD, self.N_CHUNKS - 1) + 1):
            if n not in self.gu:
                self.gu[n] = self._gate_up(n)
        g, u = self.gu.pop(c)
        hid = (g * _sigmoid(g) * u).astype(BF16)
        self.acc = self.acc + _dot(hid, self.wd_ref[c * FF_CHUNK:(c + 1) * FF_CHUNK, :])

    def tail(self):
        return _layer_norm(self.acc, self.ln2g_ref[...], self.ln2b_ref[...])

    def run(self):
        self.head()
        for c in range(self.N_CHUNKS):
            self.chunk(c)
        return self.tail()


def _mix_blocks(zc, nb, pad_rows, thr_first, state, cbuf, wconv_ref, bconv_ref, wqk_ref, gb_ref,
                sinks_ref, gattn_ref, gml_ref, hook):
    rows = nb * BLOCK
    kp_lo, kp_hi, vp_lo, vp_hi, c_state, m_state = state
    c_state, m_state = list(c_state), list(m_state)
    lane = lax.broadcasted_iota(jnp.int32, (BLOCK, LANES), 1)
    lo_half = lane < HEAD_DIM
    row_valid = None
    if pad_rows:
        row_valid = lax.broadcasted_iota(jnp.int32, (rows, 1), 0) >= pad_rows

    s_all, vmasks = [], []
    for blk in range(nb):
        rs = slice(blk * BLOCK, (blk + 1) * BLOCK)
        kb = zc[rs, ATT_W:ATT_W + KV_W].astype(BF16)
        vb = zc[rs, ATT_W + KV_W:ATT_W + 2 * KV_W].astype(BF16)
        zero_b = jnp.zeros_like(kb)
        k_lo, k_hi = jnp.where(lo_half, kb, zero_b), jnp.where(lo_half, zero_b, kb)
        v_lo, v_hi = jnp.where(lo_half, vb, zero_b), jnp.where(lo_half, zero_b, vb)
        kmask = jnp.concatenate([kp_lo, k_lo, kp_hi, k_hi], axis=0)
        vmasks.append(jnp.concatenate([vp_lo, v_lo, vp_hi, v_hi], axis=0))
        q_group = lambda g, rs=rs: zc[rs, g * LANES:(g + 1) * LANES]
        q4 = jnp.concatenate([_paired_q_group(q_group, r) for r in range(4)], axis=0).astype(BF16)
        s_all.append(_dot_nt(q4, kmask))
        kp_lo, kp_hi, vp_lo, vp_hi = k_lo, k_hi, v_lo, v_hi
    hook("scores")

    c_in = zc[:, COL_CONV:COL_VM]
    if row_valid is not None:
        c_in = jnp.where(row_valid, c_in, 0.0)
    cbuf[8:8 + rows, :] = c_in
    c_act = _conv_silu(lambda jj: cbuf[5 + jj:5 + jj + rows, :], wconv_ref, bconv_ref)
    cbuf[0:8, :] = cbuf[rows:rows + 8, :]
    hs = range(M_HEADS)
    ca_b = [c_act[:, h * M_HEAD_DIM:(h + 1) * M_HEAD_DIM].astype(BF16) for h in hs]
    qk = [_dot(ca_b[h], wqk_ref[h]) for h in hs]
    qm_b = [qk[h][:, :M_HEAD_DIM].astype(BF16) for h in hs]
    kmt_f = [(qk[h][:, M_HEAD_DIM:] * (M_HEAD_DIM ** -0.5)).T for h in hs]
    hook("front")

    ti = lax.broadcasted_iota(jnp.int32, (rows, rows), 0)
    si = lax.broadcasted_iota(jnp.int32, (rows, rows), 1)
    ltri = jnp.where((si <= ti) & ((si >> 7) == (ti >> 7)), 1.0, 0.0).astype(BF16)
    xg, _ = _gate_columns(zc[:, IN_MAIN:IN_PAD], gb_ref, row_valid, ltri)
    xgt = xg.T
    hook("gates")

    qi = lax.broadcasted_iota(jnp.int32, (BLOCK, BLOCK), 0)
    ks = lax.broadcasted_iota(jnp.int32, (BLOCK, BLOCK), 1)
    own = ks <= qi
    distf = jnp.where(own, qi - ks, WINDOW + qi - ks).astype(F32)
    zero_p = jnp.zeros((BLOCK, BLOCK), BF16)
    att_rows = []
    for blk in range(nb):
        live = None
        if blk == 0 and thr_first is not None:
            live = jnp.where(own, ks + BLOCK, ks) >= thr_first
        p_rows, inv = [], {}
        for r in range(4):
            pr = []
            for c in range(2):
                h = r + 4 * c
                s_blk = s_all[blk][r * LANES:(r + 1) * LANES, c * 256:(c + 1) * 256]
                sp = jnp.where(own, s_blk[:, BLOCK:], s_blk[:, :BLOCK]) * (HEAD_DIM ** -0.5) \
                    - ALIBI_SLOPES[h] * distf
                if live is not None:
                    sp = jnp.where(live, sp, -jnp.inf)
                sink = sinks_ref[h]
                mx = jnp.maximum(jnp.max(sp, axis=1, keepdims=True), sink)
                p = jnp.exp(sp - mx)
                inv[h] = 1.0 / (jnp.sum(p, axis=1, keepdims=True) + jnp.exp(sink - mx))
                p_b = p.astype(BF16)
                pr += [jnp.where(own, zero_p, p_b), jnp.where(own, p_b, zero_p)]
            p_rows.append(jnp.concatenate(pr, axis=1))
            hook("softmax", blk * 4 + r)
        o_all = _dot(jnp.concatenate(p_rows, axis=0), vmasks[blk])
        att_rows.append(jnp.concatenate(
            [o_all[r * LANES:(r + 1) * LANES] * jnp.where(lo_half, inv[r], inv[r + 4]) for r in range(4)], axis=1))
        hook("pv", blk)

    ones_col = jnp.where(lane == 0, 1.0, 0.0).astype(BF16)
    tb = lax.broadcasted_iota(jnp.int32, (BLOCK, BLOCK), 0)
    sb = lax.broadcasted_iota(jnp.int32, (BLOCK, BLOCK), 1)
    tri = sb <= tb
    hh_rows = [[] for _ in hs]
    for blk in range(nb):
        rs = slice(blk * BLOCK, (blk + 1) * BLOCK)
        b_c = [xg[rs, 4 + h:5 + h] for h in hs]
        b_r = [xgt[4 + h:5 + h, rs] for h in hs]
        li_r = [xgt[h:h + 1, rs] for h in hs]
        kmt = [kmt_f[h][:, rs] for h in hs]
        v_ext = [jnp.concatenate([zc[rs, COL_VM + h * M_HEAD_DIM:COL_VM + (h + 1) * M_HEAD_DIM].astype(BF16), ones_col], axis=1)
                 for h in hs]
        s_qk = [_dot(qm_b[h][rs], kmt[h].astype(BF16)) for h in hs]
        inter = [_dot(qm_b[h][rs], c_state[h].astype(BF16)) for h in hs]
        hook("mlstm_a", blk)
        dmat = [jnp.where(tri, b_c[h] - b_r[h] + li_r[h], -jnp.inf) for h in hs]
        m_inter = [b_c[h] + m_state[h] for h in hs]
        m_t = [jnp.maximum(m_inter[h], jnp.max(dmat[h], axis=1, keepdims=True)) for h in hs]
        w_inter = [jnp.exp(m_inter[h] - m_t[h]) for h in hs]
        sc_b = [(s_qk[h] * jnp.exp(dmat[h] - m_t[h])).astype(BF16) for h in hs]
        hook("mlstm_b", blk)
        nd = [_dot(sc_b[h], v_ext[h]) + w_inter[h] * inter[h] for h in hs]
        for h in hs:
            m_end = m_t[h][BLOCK - 1:BLOCK, :]
            b_last = b_c[h][BLOCK - 1:BLOCK, :]
            decay = jnp.exp(b_last + m_state[h] - m_end)
            wk_r = jnp.exp(b_last - b_r[h] + li_r[h] - m_end)
            c_state[h] = decay * c_state[h] + _dot((kmt[h] * wk_r).astype(BF16), v_ext[h])
            m_state[h] = m_end
            num, den = nd[h][:, :M_HEAD_DIM], nd[h][:, M_HEAD_DIM:M_HEAD_DIM + 1]
            hh_rows[h].append(num / jnp.maximum(jnp.abs(den), jnp.exp(-m_t[h])))
        hook("mlstm_c", blk)

    y_att = _attn_head_norm(jnp.concatenate(att_rows, axis=0), gattn_ref[...])
    hook("attnorm")
    y_m = []
    for h in hs:
        hm = _sigmoid(zc[:, COL_O + h * M_HEAD_DIM:COL_O + (h + 1) * M_HEAD_DIM]) * jnp.concatenate(hh_rows[h], axis=0)
        y_m.append(_mlstm_head_norm(hm, gml_ref[:, h * M_HEAD_DIM:(h + 1) * M_HEAD_DIM]))
        hook("headnorm", h)
    ymix = jnp.concatenate([y_att] + y_m, axis=1).astype(BF16)
    return ymix, (kp_lo, kp_hi, vp_lo, vp_hi, c_state, m_state)


def _project(xb, w_in_refs, lo=0, hi=IN_PAD):
    bounds = (0, IN_MAIN, IN_PAD)
    parts = []
    for ref, start, stop in zip(w_in_refs, bounds[:-1], bounds[1:]):
        a, b = max(lo, start), min(hi, stop)
        if a < b:
            parts.append(_dot(xb, ref[:, a - start:b - start]))
    return parts[0] if len(parts) == 1 else jnp.concatenate(parts, axis=1)


def _meta_state(meta_ref, w_inm_ref, w_ing_ref, wconv_ref, bconv_ref, wqk_ref, gb_ref,
                sinks_ref, gattn_ref,
                gml_ref, kplo_o, kphi_o, vplo_o, vphi_o, cb_o, ctn_o, mst_o, zc, cbuf):
    xb = jnp.concatenate([jnp.zeros((META_PAD, D_MODEL), F32), meta_ref[...]], axis=0)
    zc[...] = _project(xb.astype(BF16), (w_inm_ref, w_ing_ref))
    cbuf[0:8, :] = jnp.zeros((8, MLSTM_W), F32)
    zb = jnp.zeros((BLOCK, KV_W), BF16)
    state = (zb, zb, zb, zb, [jnp.zeros((M_HEAD_DIM, 2 * M_HEAD_DIM), F32)] * M_HEADS,
             [jnp.zeros((1, 1), F32)] * M_HEADS)
    _, (kp_lo, kp_hi, vp_lo, vp_hi, c_state, m_state) = _mix_blocks(
        zc, 1, META_PAD, BLOCK + META_PAD, state, cbuf, wconv_ref, bconv_ref, wqk_ref, gb_ref,
        sinks_ref, gattn_ref, gml_ref, lambda name, index=0: None)
    kplo_o[...] = kp_lo
    kphi_o[...] = kp_hi
    vplo_o[...] = vp_lo
    vphi_o[...] = vp_hi
    cb_o[...] = cbuf[0:8, :]
    for h in range(M_HEADS):
        ctn_o[h] = c_state[h]
        mst_o[h:h + 1, :] = jnp.broadcast_to(m_state[h], (1, LANES))
    mst_o[M_HEADS:8, :] = jnp.zeros((8 - M_HEADS, LANES), F32)


def _load_ffn_weights(wout_hbm, wg_hbm, wu_hbm, wd_hbm, wout_v, wg_v, wu_v, wd_v, stage_bufs, sems):
    half = D_FF // 2
    jobs = []
    for src, dst in ((wg_hbm, wg_v), (wu_hbm, wu_v)):
        jobs += [(src, r, WEIGHT_ROWS, c, half, dst, r) for r in range(0, D_MODEL, WEIGHT_ROWS) for c in (0, half)]
    jobs += [(wd_hbm, r, WEIGHT_ROWS, 0, D_MODEL, wd_v, r) for r in range(0, D_FF, WEIGHT_ROWS)]
    for h in range(N_HEADS):
        jobs.append((wout_hbm, h * HEAD_DIM, HEAD_DIM, 0, D_MODEL, wout_v, ((h % 4) * 2 + h // 4) * HEAD_DIM))
    jobs += [(wout_hbm, r, WEIGHT_ROWS, 0, D_MODEL, wout_v, r) for r in range(ATT_W, D_MODEL, WEIGHT_ROWS)]

    windows = []
    for buf in stage_bufs:
        views = [buf.at[s] for s in range(buf.shape[0])] if len(buf.shape) == 3 else [buf]
        for v in views:
            windows += [(v, r, c) for r in range(0, v.shape[0] - WEIGHT_ROWS + 1, WEIGHT_ROWS)
                        for c in range(0, v.shape[1] - half + 1, half)]
    nslot = len(windows)
    assert nslot <= sems.shape[0]

    def window(i, nrows, ncols):
        v, r, c = windows[i % nslot]
        return v.at[pl.ds(r, nrows), pl.ds(c, ncols)]

    def copy(i):
        src, r0, nrows, c0, ncols, _, _ = jobs[i]
        return pltpu.make_async_copy(src.at[pl.ds(r0, nrows), pl.ds(c0, ncols)], window(i, nrows, ncols),
                                     sems.at[i % nslot])

    for i in range(min(nslot - 1, len(jobs))):
        copy(i).start()
    for i, (_, _, nrows, c0, ncols, dst, d0) in enumerate(jobs):
        if i + nslot - 1 < len(jobs):
            copy(i + nslot - 1).start()
        copy(i).wait()
        dst[d0:d0 + nrows, c0:c0 + ncols] = window(i, nrows, ncols)[...].astype(BF16)


def _pair_kernel(xin_ref, xres_ref, meta_ref, xs_ref, ymix_s_ref,
                 w_inm_ref, w_ing_ref, wconv_ref, bconv_ref, wqk_ref, bi_ref, bf_ref, sinks_ref,
                 gattn_ref, gml_ref,
                 wout_hbm, ln1g_ref, ln1b_ref, wg_hbm, wu_hbm, wd_hbm, ln2g_ref, ln2b_ref,
                 y_ref, ys_ref, pk_ref, pv_ref, pconv_ref, pc_ref, pn_ref, pm_ref,
                 wout_v, wg_v, wu_v, wd_v, wstage, wsems,
                 kplo_i, kphi_i, vplo_i, vphi_i, cb_i, ctn_i, mst_i,
                 zcur, znext, kvlast, kplo, kphi, vplo, vphi, cbuf, ctn, mst, yprev, *, npair, n_pair_steps):
    t = pl.program_id(0)
    p = lax.rem(t + (npair - 1), npair)
    gb_ref = (bi_ref, bf_ref)

    @pl.when(t == 0)
    def _init_pipeline():
        _load_ffn_weights(wout_hbm, wg_hbm, wu_hbm, wd_hbm, wout_v, wg_v, wu_v, wd_v,
                          (wstage, zcur, znext), wsems)
        zcur[...] = jnp.zeros_like(zcur)
        yprev[...] = jnp.zeros_like(yprev)
        _meta_state(meta_ref, w_inm_ref, w_ing_ref, wconv_ref, bconv_ref, wqk_ref, gb_ref,
                    sinks_ref, gattn_ref, gml_ref, kplo_i, kphi_i, vplo_i, vphi_i, cb_i, ctn_i, mst_i,
                    znext.at[pl.ds(0, BLOCK)], cbuf)

    @pl.when((p == 0) | (t == 0))
    def _load_meta_state():
        kplo[...] = kplo_i[...]
        kphi[...] = kphi_i[...]
        vplo[...] = vplo_i[...]
        vphi[...] = vphi_i[...]
        cbuf[0:8, :] = cb_i[...]
        ctn[...] = ctn_i[...]
        mst[...] = mst_i[...]

    ffn_w = (wout_v, ln1g_ref, ln1b_ref, wg_v, wu_v, wd_v, ln2g_ref, ln2b_ref)

    @pl.when(t < n_pair_steps)
    def _pair_step():
        ffn = _FfnStages(xres_ref[...], yprev[...], *ffn_w)
        xb_b = xin_ref[...].astype(BF16)

        n_proj = -(-IN_PAD // PROJ_CHUNK)

        def proj(n):
            lo, hi = n * PROJ_CHUNK, min((n + 1) * PROJ_CHUNK, IN_PAD)
            znext[:, lo:hi] = _project(xb_b, (w_inm_ref, w_ing_ref), lo, hi)

        proj_it, ffn_it = iter(range(n_proj)), iter(range(ffn.N_CHUNKS))

        def fill_proj(n):
            for c in [c for _, c in zip(range(n), proj_it)]:
                proj(c)

        def fill_ffn(n):
            for c in [c for _, c in zip(range(n), ffn_it)]:
                ffn.chunk(c)

        out = {}

        def hook(name, index=0):
            if name == "scores":
                fill_proj(2)
                ffn.head()
            elif name == "front":
                fill_proj(1)
            elif name == "gates":
                fill_ffn(1)
            elif name == "softmax":
                if index % 2 == 1:
                    fill_ffn(1)
            elif name in ("pv", "mlstm_a", "mlstm_b"):
                fill_ffn(1)
            elif name == "mlstm_c":
                fill_proj(1)
            elif name == "attnorm":
                fill_ffn(ffn.N_CHUNKS)
                out["y"] = ffn.tail()
                fill_proj(1)
            elif name == "headnorm":
                fill_proj(1)

        state = (kplo[...], kphi[...], vplo[...], vphi[...], [ctn[h] for h in range(M_HEADS)],
                 [mst[h:h + 1, 0:1] for h in range(M_HEADS)])
        thr_first = jnp.where(p == 0, META_PAD, 0)
        kvlast[...] = zcur[(STEP_BLOCKS - 1) * BLOCK:STEP_BLOCKS * BLOCK, ATT_W:ATT_W + 2 * KV_W]
        ymix, (kp_lo, kp_hi, vp_lo, vp_hi, c_state, m_state) = _mix_blocks(
            zcur, STEP_BLOCKS, 0, thr_first, state, cbuf, wconv_ref, bconv_ref, wqk_ref, gb_ref,
            sinks_ref, gattn_ref, gml_ref, hook)
        fill_proj(n_proj)
        y_ref[...] = out["y"]
        kplo[...] = kp_lo
        kphi[...] = kp_hi
        vplo[...] = vp_lo
        vphi[...] = vp_hi
        for h in range(M_HEADS):
            ctn[h] = c_state[h]
            mst[h:h + 1, :] = jnp.broadcast_to(m_state[h], (1, LANES))
        yprev[...] = ymix
        zcur[...] = znext[...]

    @pl.when(t >= n_pair_steps)
    def _sample_ffn_step():
        ys_ref[...] = _FfnStages(xs_ref[...], ymix_s_ref[...], *ffn_w).run()

    @pl.when((p == npair - 1) & (t > 0) & (t < n_pair_steps))
    def _final():
        pk_ref[...] = kvlast[:, 0:KV_W].T
        pv_ref[...] = kvlast[:, KV_W:2 * KV_W].T
        seq = lax.div(t - 1, npair)
        for j in range(CONV_W - 1):
            pconv_ref[j, pl.ds(seq, 1), :] = cbuf[8 - (CONV_W - 1) + j:8 - (CONV_W - 1) + j + 1, :]
        head_lane = lax.broadcasted_iota(jnp.int32, (1, LANES), 1)
        m_row = jnp.zeros((1, LANES), F32)
        for h in range(M_HEADS):
            m_row = jnp.where(head_lane == h, mst[h:h + 1, :], m_row)
        pm_ref[pl.ds(seq, 1), :] = m_row[:, 0:M_HEADS]
        for h in range(M_HEADS):
            c_n = ctn[h].T
            pc_ref[h] = c_n[0:M_HEAD_DIM, :]
            pn_ref[h:h + 1, :] = c_n[M_HEAD_DIM:M_HEAD_DIM + 1, :]


def _sample_kernel(xs_ref, ck_ref, cv_ref, sconv_ref, sc_ref, sn_ref, sm_ref,
                   w_inm_ref, w_ing_ref, wconv_ref, bconv_ref, wqk_ref, bi_ref, bf_ref, sinks_ref,
                   gattn_ref, gml_ref,
                   ymix_ref, sk_ref, sv_ref, scv_ref, sco_ref, sno_ref, smo_ref, cext):
    ns, L = SEQ_TILE, 8
    gb_ref = (bi_ref, bf_ref)
    xs = xs_ref[...].reshape(ns * L, D_MODEL)
    z = _project(xs.astype(BF16), (w_inm_ref, w_ing_ref))

    lane = lax.broadcasted_iota(jnp.int32, (BLOCK, LANES), 1)
    lo_half = lane < HEAD_DIM
    lo3 = lax.broadcasted_iota(jnp.int32, (ns, L, LANES), 2) < HEAD_DIM
    ti = lax.broadcasted_iota(jnp.int32, (BLOCK, BLOCK), 0)
    si = lax.broadcasted_iota(jnp.int32, (BLOCK, BLOCK), 1)
    same_seq = (ti >> 3) == (si >> 3)
    tl, sl_ = ti & 7, si & 7
    causal_seq = same_seq & (sl_ <= tl)

    k_new = z[:, ATT_W:ATT_W + KV_W]
    v_new = z[:, ATT_W + KV_W:ATT_W + 2 * KV_W]
    kb, vb = k_new.astype(BF16), v_new.astype(BF16)
    zero_b = jnp.zeros_like(kb)
    kn_mask = jnp.concatenate([jnp.where(lo_half, kb, zero_b), jnp.where(lo_half, zero_b, kb)], axis=0)
    vn_mask = jnp.concatenate([jnp.where(lo_half, vb, zero_b), jnp.where(lo_half, zero_b, vb)], axis=0)
    qg = [_paired_q_group(lambda g: z[:, g * LANES:(g + 1) * LANES], r) for r in range(4)]
    s_new = _dot_nt(jnp.concatenate(qg, axis=0).astype(BF16), kn_mask)
    qg3 = [q.reshape(ns, L, LANES) for q in qg]
    q8 = jnp.concatenate([jnp.where(lo3, q, 0.0) for q in qg3] + [jnp.where(lo3, 0.0, q) for q in qg3],
                         axis=1).astype(BF16)
    s_cache = jnp.einsum('nqd,ndk->nqk', q8, ck_ref[...].astype(BF16), preferred_element_type=F32)

    c3 = z[:, COL_CONV:COL_VM].reshape(ns, L, MLSTM_W)
    cext[:, 8:16, :] = c3
    for j in range(CONV_W - 1):
        cext[:, 8 - (CONV_W - 1) + j, :] = sconv_ref[j]
        scv_ref[j] = cext[:, 16 - (CONV_W - 1) + j, :]
    c_act = _conv_silu(lambda jj: cext[:, 5 + jj:5 + jj + L, :], wconv_ref, bconv_ref)
    c_act = c_act.reshape(BLOCK, MLSTM_W)
    lblk = jnp.where(causal_seq, 1.0, 0.0).astype(BF16)
    lseq = jnp.where(same_seq, 1.0, 0.0).astype(BF16)
    xg, blast = _gate_columns(z[:, IN_MAIN:IN_PAD], gb_ref, None, lblk, extra=lseq)
    xgt = xg.T
    m_rep = jnp.broadcast_to(sm_ref[...], (ns, L, LANES)).reshape(BLOCK, LANES)
    hs = range(M_HEADS)
    qk = [_dot(c_act[:, h * M_HEAD_DIM:(h + 1) * M_HEAD_DIM].astype(BF16), wqk_ref[h]) for h in hs]
    qm = [qk[h][:, :M_HEAD_DIM] for h in hs]
    qm_b = [qm[h].astype(BF16) for h in hs]
    km = [qk[h][:, M_HEAD_DIM:] * (M_HEAD_DIM ** -0.5) for h in hs]
    km_b = [km[h].astype(BF16) for h in hs]
    vh_b = [z[:, COL_VM + h * M_HEAD_DIM:COL_VM + (h + 1) * M_HEAD_DIM].astype(BF16) for h in hs]
    c_old = [sc_ref[:, h] for h in hs]
    s_qk = [_dot_nt(qm_b[h], km_b[h]) for h in hs]
    cq = [_dot_nt(qm_b[h], c_old[h].reshape(ns * 128, 128).astype(BF16)) for h in hs]

    mask_c = si > tl
    dist_c = (WINDOW + tl - si).astype(F32)
    dist_n = (tl - sl_).astype(F32)
    pc_parts, pn_rows, inv = {}, [], {}
    for r in range(4):
        pn = []
        for c in range(2):
            h = r + 4 * c
            sc_c = s_cache[:, c * 32 + r * L:c * 32 + (r + 1) * L, :].reshape(BLOCK, LANES)
            sc_n = s_new[r * LANES:(r + 1) * LANES, c * BLOCK:(c + 1) * BLOCK]
            (p_c, p_n), inv[h] = _softmax_parts(
                [(sc_c, mask_c, -ALIBI_SLOPES[h] * dist_c), (sc_n, causal_seq, -ALIBI_SLOPES[h] * dist_n)],
                sinks_ref[h])
            pc_parts[(c, r)] = p_c.reshape(ns, L, LANES)
            pn.append(p_n.astype(BF16))
        pn_rows.append(jnp.concatenate(pn, axis=1))
    p8 = jnp.concatenate([pc_parts[(c, r)] for c in range(2) for r in range(4)], axis=1).astype(BF16)
    o_cache = jnp.einsum('nqk,ndk->nqd', p8, cv_ref[...].astype(BF16), preferred_element_type=F32)
    o_new = _dot(jnp.concatenate(pn_rows, axis=0), vn_mask)
    groups = []
    for r in range(4):
        oa = o_cache[:, r * L:(r + 1) * L, :].reshape(BLOCK, LANES)
        ob = o_cache[:, 32 + r * L:32 + (r + 1) * L, :].reshape(BLOCK, LANES)
        o = jnp.where(lo_half, oa, ob) + o_new[r * LANES:(r + 1) * LANES]
        groups.append(o * jnp.where(lo_half, inv[r], inv[r + 4]))
    y_att = _attn_head_norm(jnp.concatenate(groups, axis=1), gattn_ref[...])

    sk_ref[:, 0:WINDOW - L, :] = jnp.swapaxes(ck_ref[...], 1, 2)[:, L:WINDOW, :]
    sk_ref[:, WINDOW - L:WINDOW, :] = k_new.reshape(ns, L, LANES)
    sv_ref[:, 0:WINDOW - L, :] = jnp.swapaxes(cv_ref[...], 1, 2)[:, L:WINDOW, :]
    sv_ref[:, WINDOW - L:WINDOW, :] = v_new.reshape(ns, L, LANES)

    b_c = [xg[:, 4 + h:5 + h] for h in hs]
    li_c = [xg[:, h:h + 1] for h in hs]
    b_r = [xgt[4 + h:5 + h, :] for h in hs]
    li_r = [xgt[h:h + 1, :] for h in hs]
    bl_c = [blast[:, 4 + h:5 + h] for h in hs]
    m_prev = [m_rep[:, h:h + 1] for h in hs]
    dmat = [jnp.where(causal_seq, b_c[h] - b_r[h] + li_r[h], -jnp.inf) for h in hs]
    m_inter = [b_c[h] + m_prev[h] for h in hs]
    m_t = [jnp.maximum(m_inter[h], jnp.max(dmat[h], axis=1, keepdims=True)) for h in hs]
    w_inter = [jnp.exp(m_inter[h] - m_t[h]) for h in hs]
    sc = [s_qk[h] * jnp.exp(dmat[h] - m_t[h]) for h in hs]
    num_intra = [_dot(sc[h].astype(BF16), vh_b[h]) for h in hs]

    m_end3 = [jnp.broadcast_to(m_t[h], (BLOCK, LANES)).reshape(ns, L, LANES)[:, L - 1:L, :] for h in hs]
    m_end = [jnp.broadcast_to(m_end3[h], (ns, L, LANES)).reshape(BLOCK, LANES)[:, 0:1] for h in hs]
    decay = [jnp.exp(bl_c[h] + m_prev[h] - m_end[h]) for h in hs]
    kw = [km[h] * jnp.exp(bl_c[h] - b_c[h] + li_c[h] - m_end[h]) for h in hs]
    seq_of_row = ti >> 3
    lane3 = lax.broadcasted_iota(jnp.int32, (ns, 1, LANES), 2)
    m_out = jnp.zeros((ns, 1, LANES), F32)
    for h in hs:
        kw_b = kw[h].astype(BF16)
        k_big = jnp.concatenate(
            [jnp.where(seq_of_row == g, kw_b, jnp.zeros_like(kw_b)) for g in range(ns)], axis=1)
        c_up = _dot_tn(vh_b[h], k_big)
        c_up3 = jnp.stack([c_up[:, g * M_HEAD_DIM:(g + 1) * M_HEAD_DIM] for g in range(ns)], axis=0)
        decay3 = jnp.broadcast_to(decay[h], (BLOCK, LANES)).reshape(ns, L, LANES)[:, 0:1, :]
        n_old = sn_ref[:, h:h + 1, :]
        sco_ref[:, h] = decay3 * c_old[h] + c_up3
        sno_ref[:, h:h + 1, :] = decay3 * n_old + jnp.sum(kw[h].reshape(ns, L, LANES), axis=1, keepdims=True)
        m_out = jnp.where(lane3 == h, m_end3[h], m_out)
    smo_ref[...] = m_out

    y_m = []
    for h in hs:
        num_inter = jnp.concatenate([cq[h][g * L:(g + 1) * L, g * M_HEAD_DIM:(g + 1) * M_HEAD_DIM] for g in range(ns)], axis=0)
        n_rep = jnp.broadcast_to(sn_ref[:, h:h + 1, :], (ns, L, LANES)).reshape(BLOCK, LANES)
        num = num_intra[h] + w_inter[h] * num_inter
        den = jnp.sum(sc[h], axis=1, keepdims=True) + w_inter[h] * jnp.sum(qm[h] * n_rep, axis=1, keepdims=True)
        hh = num / jnp.maximum(jnp.abs(den), jnp.exp(-m_t[h]))
        hm = _sigmoid(z[:, COL_O + h * M_HEAD_DIM:COL_O + (h + 1) * M_HEAD_DIM]) * hh
        y_m.append(_mlstm_head_norm(hm, gml_ref[:, h * M_HEAD_DIM:(h + 1) * M_HEAD_DIM]))

    ymix_ref[...] = jnp.concatenate([y_att] + y_m, axis=1).astype(BF16)


def _const_spec(shape):
    nd = len(shape)
    return pl.BlockSpec(shape, lambda *_: (0,) * nd, pipeline_mode=pl.Buffered(1))


def _mixer_weight_specs():
    return [
        _const_spec((D_MODEL, IN_MAIN + IN_GATES)),
        _const_spec((D_MODEL, IN_PAD - IN_MAIN)),
        _const_spec((CONV_W, MLSTM_W)),
        _const_spec((1, MLSTM_W)),
        _const_spec((M_HEADS, M_HEAD_DIM, 2 * M_HEAD_DIM)),
        pl.BlockSpec(memory_space=pltpu.SMEM),
        pl.BlockSpec(memory_space=pltpu.SMEM),
        pl.BlockSpec(memory_space=pltpu.SMEM),
        _const_spec((1, ATT_W)),
        _const_spec((1, MLSTM_W)),
    ]


def _post_weight_specs(d):
    hbm = pl.BlockSpec(memory_space=pl.ANY)
    return [hbm, _const_spec((1, d)), _const_spec((1, d)), hbm, hbm, hbm, _const_spec((1, d)), _const_spec((1, d))]


def _prompt_layer(x_prompt, meta_tokens, x_sample_rows, ymix_sample, mixer_w, post_w):
    b, s, d = x_prompt.shape
    state_shapes = [((BLOCK, KV_W), BF16)] * 4 + [
        ((8, MLSTM_W), F32), ((M_HEADS, M_HEAD_DIM, 2 * M_HEAD_DIM), F32), ((8, LANES), F32)]
    rows = STEP_BLOCKS * BLOCK
    npair = s // rows
    total = b * npair
    n_pair_steps = total + 2
    n_sample_steps = x_sample_rows.shape[0] // rows
    steps = n_pair_steps + n_sample_steps

    def block_map(lag):
        def index(t):
            tb = jnp.clip(t - lag, 0, total - 1)
            return (tb // npair, tb % npair, 0)
        return index

    def smap(t):
        return (jnp.clip(t - 1, 0, total - 1) // npair, 0, 0)

    def sample_map(t):
        return (jnp.clip(t - n_pair_steps, 0, n_sample_steps - 1), 0)

    out_shape = (
        jax.ShapeDtypeStruct((b, s, d), F32),
        jax.ShapeDtypeStruct(x_sample_rows.shape, F32),
        jax.ShapeDtypeStruct((b, BLOCK, KV_W), F32),
        jax.ShapeDtypeStruct((b, BLOCK, KV_W), F32),
        jax.ShapeDtypeStruct((CONV_W - 1, b, MLSTM_W), F32),
        jax.ShapeDtypeStruct((b, M_HEADS, M_HEAD_DIM, M_HEAD_DIM), F32),
        jax.ShapeDtypeStruct((b, M_HEADS, M_HEAD_DIM), F32),
        jax.ShapeDtypeStruct((b, M_HEADS), F32),
    )
    out_specs = (
        pl.BlockSpec((None, rows, d), block_map(2)),
        pl.BlockSpec((rows, d), sample_map),
        pl.BlockSpec((None, BLOCK, KV_W), smap),
        pl.BlockSpec((None, BLOCK, KV_W), smap),
        pl.BlockSpec((CONV_W - 1, b, MLSTM_W), lambda t: (0, 0, 0)),
        pl.BlockSpec((None, M_HEADS, M_HEAD_DIM, M_HEAD_DIM), lambda t: smap(t) + (0,)),
        pl.BlockSpec((None, M_HEADS, M_HEAD_DIM), smap),
        pl.BlockSpec((b, M_HEADS), lambda t: (0, 0)),
    )
    return pl.pallas_call(
        functools.partial(_pair_kernel, npair=npair, n_pair_steps=n_pair_steps),
        grid=(steps,),
        in_specs=[pl.BlockSpec((None, rows, d), block_map(0)), pl.BlockSpec((None, rows, d), block_map(2)),
                  _const_spec((N_META, d)), pl.BlockSpec((rows, d), sample_map), pl.BlockSpec((rows, d), sample_map)]
        + _mixer_weight_specs() + _post_weight_specs(d),
        out_specs=out_specs,
        out_shape=out_shape,
        scratch_shapes=[
            pltpu.VMEM((d, d), BF16), pltpu.VMEM((d, D_FF), BF16), pltpu.VMEM((d, D_FF), BF16),
            pltpu.VMEM((D_FF, d), BF16),
            pltpu.VMEM((WEIGHT_SLOTS, WEIGHT_ROWS, D_FF), F32),
            pltpu.SemaphoreType.DMA((2 * WEIGHT_SLOTS + 2 * STEP_BLOCKS,)),
        ] + [pltpu.VMEM(shape, dt) for shape, dt in state_shapes] + [
            pltpu.VMEM((rows, IN_PAD), F32),
            pltpu.VMEM((rows, IN_PAD), F32),
            pltpu.VMEM((BLOCK, 2 * KV_W), F32),
        ] + [pltpu.VMEM((BLOCK, KV_W), BF16)] * 4 + [
            pltpu.VMEM((rows + 8, MLSTM_W), F32),
            pltpu.VMEM((M_HEADS, M_HEAD_DIM, 2 * M_HEAD_DIM), F32),
            pltpu.VMEM((8, LANES), F32),
            pltpu.VMEM((rows, d), BF16),
        ],
        compiler_params=pltpu.CompilerParams(
            dimension_semantics=("arbitrary",), vmem_limit_bytes=VMEM_LIMIT),
        name="prompt_layer",
    )(x_prompt, x_prompt, meta_tokens, x_sample_rows, ymix_sample, *mixer_w, *post_w)


def _sample_mixer(x_sample, ck, cv, s_conv, s_c, s_n, s_m3, mixer_w):
    n, l, d = x_sample.shape
    t = SEQ_TILE
    m3 = lambda i: (i, 0, 0)
    m4 = lambda i: (i, 0, 0, 0)
    out_shape = (
        jax.ShapeDtypeStruct((n * l, d), BF16),
        jax.ShapeDtypeStruct((n, WINDOW, KV_W), F32),
        jax.ShapeDtypeStruct((n, WINDOW, KV_W), F32),
        jax.ShapeDtypeStruct((CONV_W - 1, n, MLSTM_W), F32),
        jax.ShapeDtypeStruct((n, M_HEADS, M_HEAD_DIM, M_HEAD_DIM), F32),
        jax.ShapeDtypeStruct((n, M_HEADS, M_HEAD_DIM), F32),
        jax.ShapeDtypeStruct((n, 1, LANES), F32),
    )
    out_specs = (
        pl.BlockSpec((t * l, d), lambda i: (i, 0)),
        pl.BlockSpec((t, WINDOW, KV_W), m3),
        pl.BlockSpec((t, WINDOW, KV_W), m3),
        pl.BlockSpec((CONV_W - 1, t, MLSTM_W), lambda i: (0, i, 0)),
        pl.BlockSpec((t, M_HEADS, M_HEAD_DIM, M_HEAD_DIM), m4),
        pl.BlockSpec((t, M_HEADS, M_HEAD_DIM), m3),
        pl.BlockSpec((t, 1, LANES), m3),
    )
    in_specs = [
        pl.BlockSpec((t, l, d), m3),
        pl.BlockSpec((t, WINDOW, KV_W), m3),
        pl.BlockSpec((t, WINDOW, KV_W), m3),
        pl.BlockSpec((CONV_W - 1, t, MLSTM_W), lambda i: (0, i, 0)),
        pl.BlockSpec((t, M_HEADS, M_HEAD_DIM, M_HEAD_DIM), m4),
        pl.BlockSpec((t, M_HEADS, M_HEAD_DIM), m3),
        pl.BlockSpec((t, 1, LANES), m3),
    ] + _mixer_weight_specs()
    return pl.pallas_call(
        _sample_kernel,
        grid=(n // t,),
        in_specs=in_specs,
        out_specs=out_specs,
        out_shape=out_shape,
        scratch_shapes=[pltpu.VMEM((t, 16, MLSTM_W), F32)],
        compiler_params=pltpu.CompilerParams(
            dimension_semantics=("arbitrary",), vmem_limit_bytes=VMEM_LIMIT),
        name="sample_mixer",
    )(x_sample, ck, cv, s_conv, s_c, s_n, s_m3, *mixer_w)


def kernel(x_prompt, x_sample, cache_k, cache_v, state_conv, state_C, state_n, state_m, meta_tokens,
           w_in, w_conv, b_conv, w_mq, w_mk, b_i, b_f, attn_sinks, g_attn, g_mlstm, w_out,
           ln1_g, ln1_b, w_gate, w_up, w_down, ln2_g, ln2_b):
    b, s, d = x_prompt.shape
    n, l, _ = x_sample.shape

    w0 = w_in[0].astype(BF16)
    w_in_parts = (w0, jnp.pad(w0[:, IN_MAIN:], ((0, 0), (0, IN_PAD - IN_MAIN - IN_GATES))))
    wqk = jnp.concatenate([w_mq[0], w_mk[0]], axis=-1).astype(BF16)
    mixer_w = w_in_parts + (w_conv[0], b_conv[0].reshape(1, MLSTM_W), wqk, b_i[0], b_f[0], attn_sinks[0],
                            _pair_heads(g_attn[0].reshape(1, ATT_W), 1), g_mlstm[0].reshape(1, MLSTM_W))
    post_w = (w_out[0], ln1_g[0].reshape(1, d), ln1_b[0].reshape(1, d), w_gate[0], w_up[0], w_down[0],
              ln2_g[0].reshape(1, d), ln2_b[0].reshape(1, d))

    key_minor = lambda c: c[0].transpose(0, 2, 3, 1).reshape(n, KV_W, WINDOW)
    s_m3 = jnp.pad(state_m[0], ((0, 0), (0, LANES - M_HEADS))).reshape(n, 1, LANES)
    ymix_s, sk, sv, scv, s_c, s_n, s_mo = _sample_mixer(
        x_sample, key_minor(cache_k), key_minor(cache_v), state_conv[0].swapaxes(0, 1), state_C[0], state_n[0],
        s_m3, mixer_w)

    y_prompt, y_sample, pk, pv, pconv, p_c, p_n, p_m = _prompt_layer(
        x_prompt, meta_tokens, x_sample.reshape(n * l, d), ymix_s, mixer_w, post_w)

    kv5 = lambda a: a.reshape(1, a.shape[0], WINDOW, 2, HEAD_DIM)
    kv5_key_minor = lambda a: a.reshape(a.shape[0], 2, HEAD_DIM, WINDOW).transpose(0, 3, 1, 2)[None]
    return (y_prompt, y_sample.reshape(n, l, d),
            kv5_key_minor(pk), kv5_key_minor(pv), pconv.swapaxes(0, 1)[None], p_c[None], p_n[None], p_m[None],
            kv5(sk), kv5(sv), scv.swapaxes(0, 1)[None], s_c[None], s_n[None], s_mo[None, :, 0, :M_HEADS])
```

```python
import functools

import jax
import jax.numpy as jnp
from jax import lax
from jax.experimental import pallas as pl
from jax.experimental.pallas import tpu as pltpu

F32 = jnp.float32
BF16 = jnp.bfloat16

D_MODEL = 1024
ATT_W = 512
MLSTM_W = 512
HEAD_DIM = 64
N_HEADS = 8
KV_W = 128
WINDOW = 128
BLOCK = 128
M_HEADS = 4
M_HEAD_DIM = 128
CONV_W = 4
N_META = 16
META_PAD = BLOCK - N_META
D_FF = 2816
LANES = 128
COL_CONV = ATT_W + 2 * KV_W
COL_VM = COL_CONV + MLSTM_W
COL_O = COL_VM + MLSTM_W
IN_MAIN = COL_O + MLSTM_W
IN_GATES = 2 * M_HEADS
IN_PAD = IN_MAIN + LANES
DEPTH = 1
ALIBI_SLOPES = tuple(2.0 ** (-8.0 * (h + 1) / N_HEADS) for h in range(N_HEADS))
DEEPNORM_ALPHA = (2.0 * DEPTH) ** 0.25
EPS = 1e-5
SEQ_TILE = 16
STEP_BLOCKS = 2
FF_CHUNK = 256
FF_LOOKAHEAD = 3
PROJ_CHUNK = 256
WEIGHT_ROWS = 128
WEIGHT_SLOTS = 4
VMEM_LIMIT = 56 * 1024 * 1024


def _pair_heads(a, axis):
    half = N_HEADS // 2
    heads = [lax.slice_in_dim(a, h * HEAD_DIM, (h + 1) * HEAD_DIM, axis=axis)
             for r in range(half) for h in (r, r + half)]
    return jnp.concatenate(heads, axis=axis)


def _paired_q_group(q_group, r):
    a, b = q_group(r // 2), q_group(r // 2 + N_HEADS // 4)
    lo_half = lax.broadcasted_iota(jnp.int32, a.shape, 1) < HEAD_DIM
    if r % 2 == 0:
        return jnp.where(lo_half, a, pltpu.roll(b, HEAD_DIM, 1))
    return jnp.where(lo_half, pltpu.roll(a, HEAD_DIM, 1), b)


def _dot(a, b):
    return jnp.dot(a, b, preferred_element_type=F32)


def _dot_nt(a, b):
    return lax.dot_general(a, b, (((1,), (1,)), ((), ())), preferred_element_type=F32)


def _dot_tn(a, b):
    return lax.dot_general(a, b, (((0,), (0,)), ((), ())), preferred_element_type=F32)


def _split2(x):
    hi = x.astype(BF16)
    lo = (x - hi.astype(F32)).astype(BF16)
    return hi, lo


def _sigmoid(x):
    return 1.0 / (1.0 + jnp.exp(-x))


def _log_sigmoid(x):
    return -(jnp.maximum(-x, 0.0) + jnp.log1p(jnp.exp(-jnp.abs(x))))


def _layer_norm(x, g, b):
    mu = jnp.mean(x, axis=-1, keepdims=True)
    xc = x - mu
    var = jnp.mean(xc * xc, axis=-1, keepdims=True)
    return xc * lax.rsqrt(var + EPS) * g + b


def _attn_head_norm(att, g):
    lo_half = lax.broadcasted_iota(jnp.int32, (att.shape[0], LANES), 1) < HEAD_DIM

    def seg_mean(x):
        s_all = jnp.sum(x, axis=1, keepdims=True)
        s_lo = jnp.sum(jnp.where(lo_half, x, 0.0), axis=1, keepdims=True)
        return jnp.where(lo_half, s_lo, s_all - s_lo) * (1.0 / HEAD_DIM)

    out = []
    for grp in range(att.shape[1] // LANES):
        x = att[:, grp * LANES:(grp + 1) * LANES]
        xc = x - seg_mean(x)
        out.append(xc * lax.rsqrt(seg_mean(xc * xc) + EPS))
    return jnp.concatenate(out, axis=1) * g


def _mlstm_head_norm(hm, g):
    mu = jnp.mean(hm, axis=-1, keepdims=True)
    xc = hm - mu
    var = jnp.mean(xc * xc, axis=-1, keepdims=True)
    return xc * lax.rsqrt(var + EPS) * g


def _softmax_parts(parts, sink):
    sp = [jnp.where(m, s * (HEAD_DIM ** -0.5) + a, -jnp.inf) for s, m, a in parts]
    mx = sink
    for s in sp:
        mx = jnp.maximum(mx, jnp.max(s, axis=1, keepdims=True))
    ps = [jnp.exp(s - mx) for s in sp]
    den = jnp.exp(sink - mx)
    for p in ps:
        den = den + jnp.sum(p, axis=1, keepdims=True)
    return ps, 1.0 / den


def _conv_silu(window, wconv_ref, bconv_ref):
    acc = bconv_ref[...]
    for j in range(CONV_W):
        acc = acc + window(j) * wconv_ref[j:j + 1, :]
    return acc * _sigmoid(acc)


def _gate_columns(gates, gb_ref, row_valid, ltri, extra=None):
    lane = lax.broadcasted_iota(jnp.int32, gates.shape, 1)
    bi_ref, bf_ref = gb_ref
    lane1 = lax.broadcasted_iota(jnp.int32, (1, LANES), 1)
    bias = jnp.zeros((1, LANES), F32)
    for h in range(M_HEADS):
        bias = jnp.where(lane1 == h, bi_ref[h], jnp.where(lane1 == M_HEADS + h, bf_ref[h], bias))
    gb = gates + bias
    is_i = lane < M_HEADS
    is_f = (lane >= M_HEADS) & (lane < 2 * M_HEADS)
    logf = _log_sigmoid(gb)
    if row_valid is not None:
        lf = jnp.where(is_f & row_valid, logf, 0.0)
        li = jnp.where(row_valid, gb, -jnp.inf)
    else:
        lf = jnp.where(is_f, logf, 0.0)
        li = gb
    hi_lo = jnp.concatenate(_split2(lf), axis=1)
    bc = _dot(ltri, hi_lo)
    x = jnp.where(is_i, li, bc[:, :LANES] + bc[:, LANES:])
    if extra is None:
        return x, None
    be = _dot(extra, hi_lo)
    return x, be[:, :LANES] + be[:, LANES:]


class _FfnStages:
    N_CHUNKS = D_FF // FF_CHUNK

    def __init__(self, x, ymix, wout_ref, ln1g_ref, ln1b_ref, wg_ref, wu_ref, wd_ref, ln2g_ref, ln2b_ref):
        self.x, self.ymix = x, ymix
        self.wout_ref, self.ln1g_ref, self.ln1b_ref = wout_ref, ln1g_ref, ln1b_ref
        self.wg_ref, self.wu_ref, self.wd_ref = wg_ref, wu_ref, wd_ref
        self.ln2g_ref, self.ln2b_ref = ln2g_ref, ln2b_ref

    def head(self):
        self.x1 = _layer_norm(DEEPNORM_ALPHA * self.x + _dot(self.ymix, self.wout_ref[...]),
                              self.ln1g_ref[...], self.ln1b_ref[...])
        self.x1b = self.x1.astype(BF16)
        self.acc = DEEPNORM_ALPHA * self.x1

    def _gate_up(self, c):
        cs = slice(c * FF_CHUNK, (c + 1) * FF_CHUNK)
        return _dot(self.x1b, self.wg_ref[:, cs]), _dot(self.x1b, self.wu_ref[:, cs])

    def chunk(self, c):
        if c == 0:
            self.gu = {}
        for n in range(c, min(c + FF_LOOKAHEAD, self.N_CHUNKS - 1) + 1):
            if n not in self.gu:
                self.gu[n] = self._gate_up(n)
        g, u = self.gu.pop(c)
        hid = (g * _sigmoid(g) * u).astype(BF16)
        self.acc = self.acc + _dot(hid, self.wd_ref[c * FF_CHUNK:(c + 1) * FF_CHUNK, :])

    def tail(self):
        return _layer_norm(self.acc, self.ln2g_ref[...], self.ln2b_ref[...])

    def run(self):
        self.head()
        for c in range(self.N_CHUNKS):
            self.chunk(c)
        return self.tail()


def _mix_blocks(zc, nb, pad_rows, thr_first, state, cbuf, wconv_ref, bconv_ref, wqk_ref, gb_ref,
                sinks_ref, gattn_ref, gml_ref, hook):
    rows = nb * BLOCK
    kp_lo, kp_hi, vp_lo, vp_hi, c_state, m_state = state
    c_state, m_state = list(c_state), list(m_state)
    lane = lax.broadcasted_iota(jnp.int32, (BLOCK, LANES), 1)
    lo_half = lane < HEAD_DIM
    row_valid = None
    if pad_rows:
        row_valid = lax.broadcasted_iota(jnp.int32, (rows, 1), 0) >= pad_rows

    s_all, vmasks = [], []
    for blk in range(nb):
        rs = slice(blk * BLOCK, (blk + 1) * BLOCK)
        kb = zc[rs, ATT_W:ATT_W + KV_W].astype(BF16)
        vb = zc[rs, ATT_W + KV_W:ATT_W + 2 * KV_W].astype(BF16)
        zero_b = jnp.zeros_like(kb)
        k_lo, k_hi = jnp.where(lo_half, kb, zero_b), jnp.where(lo_half, zero_b, kb)
        v_lo, v_hi = jnp.where(lo_half, vb, zero_b), jnp.where(lo_half, zero_b, vb)
        kmask = jnp.concatenate([kp_lo, k_lo, kp_hi, k_hi], axis=0)
        vmasks.append(jnp.concatenate([vp_lo, v_lo, vp_hi, v_hi], axis=0))
        q_group = lambda g, rs=rs: zc[rs, g * LANES:(g + 1) * LANES]
        q4 = jnp.concatenate([_paired_q_group(q_group, r) for r in range(4)], axis=0).astype(BF16)
        s_all.append(_dot_nt(q4, kmask))
        kp_lo, kp_hi, vp_lo, vp_hi = k_lo, k_hi, v_lo, v_hi
    hook("scores")

    c_in = zc[:, COL_CONV:COL_VM]
    if row_valid is not None:
        c_in = jnp.where(row_valid, c_in, 0.0)
    cbuf[8:8 + rows, :] = c_in
    c_act = _conv_silu(lambda jj: cbuf[5 + jj:5 + jj + rows, :], wconv_ref, bconv_ref)
    cbuf[0:8, :] = cbuf[rows:rows + 8, :]
    hs = range(M_HEADS)
    ca_b = [c_act[:, h * M_HEAD_DIM:(h + 1) * M_HEAD_DIM].astype(BF16) for h in hs]
    qk = [_dot(ca_b[h], wqk_ref[h]) for h in hs]
    qm_b = [qk[h][:, :M_HEAD_DIM].astype(BF16) for h in hs]
    kmt_f = [(qk[h][:, M_HEAD_DIM:] * (M_HEAD_DIM ** -0.5)).T for h in hs]
    hook("front")

    ti = lax.broadcasted_iota(jnp.int32, (rows, rows), 0)
    si = lax.broadcasted_iota(jnp.int32, (rows, rows), 1)
    ltri = jnp.where((si <= ti) & ((si >> 7) == (ti >> 7)), 1.0, 0.0).astype(BF16)
    xg, _ = _gate_columns(zc[:, IN_MAIN:IN_PAD], gb_ref, row_valid, ltri)
    xgt = xg.T
    hook("gates")

    qi = lax.broadcasted_iota(jnp.int32, (BLOCK, BLOCK), 0)
    ks = lax.broadcasted_iota(jnp.int32, (BLOCK, BLOCK), 1)
    own = ks <= qi
    distf = jnp.where(own, qi - ks, WINDOW + qi - ks).astype(F32)
    zero_p = jnp.zeros((BLOCK, BLOCK), BF16)
    att_rows = []
    for blk in range(nb):
        live = None
        if blk == 0 and thr_first is not None:
            live = jnp.where(own, ks + BLOCK, ks) >= thr_first
        p_rows, inv = [], {}
        for r in range(4):
            pr = []
            for c in range(2):
                h = r + 4 * c
                s_blk = s_all[blk][r * LANES:(r + 1) * LANES, c * 256:(c + 1) * 256]
                sp = jnp.where(own, s_blk[:, BLOCK:], s_blk[:, :BLOCK]) * (HEAD_DIM ** -0.5) \
                    - ALIBI_SLOPES[h] * distf
                if live is not None:
                    sp = jnp.where(live, sp, -jnp.inf)
                sink = sinks_ref[h]
                mx = jnp.maximum(jnp.max(sp, axis=1, keepdims=True), sink)
                p = jnp.exp(sp - mx)
                inv[h] = 1.0 / (jnp.sum(p, axis=1, keepdims=True) + jnp.exp(sink - mx))
                p_b = p.astype(BF16)
                pr += [jnp.where(own, zero_p, p_b), jnp.where(own, p_b, zero_p)]
            p_rows.append(jnp.concatenate(pr, axis=1))
            hook("softmax", blk * 4 + r)
        o_all = _dot(jnp.concatenate(p_rows, axis=0), vmasks[blk])
        att_rows.append(jnp.concatenate(
            [o_all[r * LANES:(r + 1) * LANES] * jnp.where(lo_half, inv[r], inv[r + 4]) for r in range(4)], axis=1))
        hook("pv", blk)

    ones_col = jnp.where(lane == 0, 1.0, 0.0).astype(BF16)
    tb = lax.broadcasted_iota(jnp.int32, (BLOCK, BLOCK), 0)
    sb = lax.broadcasted_iota(jnp.int32, (BLOCK, BLOCK), 1)
    tri = sb <= tb
    hh_rows = [[] for _ in hs]
    for blk in range(nb):
        rs = slice(blk * BLOCK, (blk + 1) * BLOCK)
        b_c = [xg[rs, 4 + h:5 + h] for h in hs]
        b_r = [xgt[4 + h:5 + h, rs] for h in hs]
        li_r = [xgt[h:h + 1, rs] for h in hs]
        kmt = [kmt_f[h][:, rs] for h in hs]
        v_ext = [jnp.concatenate([zc[rs, COL_VM + h * M_HEAD_DIM:COL_VM + (h + 1) * M_HEAD_DIM].astype(BF16), ones_col], axis=1)
                 for h in hs]
        s_qk = [_dot(qm_b[h][rs], kmt[h].astype(BF16)) for h in hs]
        inter = [_dot(qm_b[h][rs], c_state[h].astype(BF16)) for h in hs]
        hook("mlstm_a", blk)
        dmat = [jnp.where(tri, b_c[h] - b_r[h] + li_r[h], -jnp.inf) for h in hs]
        m_inter = [b_c[h] + m_state[h] for h in hs]
        m_t = [jnp.maximum(m_inter[h], jnp.max(dmat[h], axis=1, keepdims=True)) for h in hs]
        w_inter = [jnp.exp(m_inter[h] - m_t[h]) for h in hs]
        sc_b = [(s_qk[h] * jnp.exp(dmat[h] - m_t[h])).astype(BF16) for h in hs]
        hook("mlstm_b", blk)
        nd = [_dot(sc_b[h], v_ext[h]) + w_inter[h] * inter[h] for h in hs]
        for h in hs:
            m_end = m_t[h][BLOCK - 1:BLOCK, :]
            b_last = b_c[h][BLOCK - 1:BLOCK, :]
            decay = jnp.exp(b_last + m_state[h] - m_end)
            wk_r = jnp.exp(b_last - b_r[h] + li_r[h] - m_end)
            c_state[h] = decay * c_state[h] + _dot((kmt[h] * wk_r).astype(BF16), v_ext[h])
            m_state[h] = m_end
            num, den = nd[h][:, :M_HEAD_DIM], nd[h][:, M_HEAD_DIM:M_HEAD_DIM + 1]
            hh_rows[h].append(num / jnp.maximum(jnp.abs(den), jnp.exp(-m_t[h])))
        hook("mlstm_c", blk)

    y_att = _attn_head_norm(jnp.concatenate(att_rows, axis=0), gattn_ref[...])
    hook("attnorm")
    y_m = []
    for h in hs:
        hm = _sigmoid(zc[:, COL_O + h * M_HEAD_DIM:COL_O + (h + 1) * M_HEAD_DIM]) * jnp.concatenate(hh_rows[h], axis=0)
        y_m.append(_mlstm_head_norm(hm, gml_ref[:, h * M_HEAD_DIM:(h + 1) * M_HEAD_DIM]))
        hook("headnorm", h)
    ymix = jnp.concatenate([y_att] + y_m, axis=1).astype(BF16)
    return ymix, (kp_lo, kp_hi, vp_lo, vp_hi, c_state, m_state)


def _project(xb, w_in_refs, lo=0, hi=IN_PAD):
    bounds = (0, IN_MAIN, IN_PAD)
    parts = []
    for ref, start, stop in zip(w_in_refs, bounds[:-1], bounds[1:]):
        a, b = max(lo, start), min(hi, stop)
        if a < b:
            parts.append(_dot(xb, ref[:, a - start:b - start]))
    return parts[0] if len(parts) == 1 else jnp.concatenate(parts, axis=1)


def _meta_state(meta_ref, w_inm_ref, w_ing_ref, wconv_ref, bconv_ref, wqk_ref, gb_ref,
                sinks_ref, gattn_ref,
                gml_ref, kplo_o, kphi_o, vplo_o, vphi_o, cb_o, ctn_o, mst_o, zc, cbuf):
    xb = jnp.concatenate([jnp.zeros((META_PAD, D_MODEL), F32), meta_ref[...]], axis=0)
    zc[...] = _project(xb.astype(BF16), (w_inm_ref, w_ing_ref))
    cbuf[0:8, :] = jnp.zeros((8, MLSTM_W), F32)
    zb = jnp.zeros((BLOCK, KV_W), BF16)
    state = (zb, zb, zb, zb, [jnp.zeros((M_HEAD_DIM, 2 * M_HEAD_DIM), F32)] * M_HEADS,
             [jnp.zeros((1, 1), F32)] * M_HEADS)
    _, (kp_lo, kp_hi, vp_lo, vp_hi, c_state, m_state) = _mix_blocks(
        zc, 1, META_PAD, BLOCK + META_PAD, state, cbuf, wconv_ref, bconv_ref, wqk_ref, gb_ref,
        sinks_ref, gattn_ref, gml_ref, lambda name, index=0: None)
    kplo_o[...] = kp_lo
    kphi_o[...] = kp_hi
    vplo_o[...] = vp_lo
    vphi_o[...] = vp_hi
    cb_o[...] = cbuf[0:8, :]
    for h in range(M_HEADS):
        ctn_o[h] = c_state[h]
        mst_o[h:h + 1, :] = jnp.broadcast_to(m_state[h], (1, LANES))
    mst_o[M_HEADS:8, :] = jnp.zeros((8 - M_HEADS, LANES), F32)


def _load_ffn_weights(wout_hbm, wg_hbm, wu_hbm, wd_hbm, wout_v, wg_v, wu_v, wd_v, stage_bufs, sems):
    half = D_FF // 2
    jobs = []
    for src, dst in ((wg_hbm, wg_v), (wu_hbm, wu_v)):
        jobs += [(src, r, WEIGHT_ROWS, c, half, dst, r) for r in range(0, D_MODEL, WEIGHT_ROWS) for c in (0, half)]
    jobs += [(wd_hbm, r, WEIGHT_ROWS, 0, D_MODEL, wd_v, r) for r in range(0, D_FF, WEIGHT_ROWS)]
    for h in range(N_HEADS):
        jobs.append((wout_hbm, h * HEAD_DIM, HEAD_DIM, 0, D_MODEL, wout_v, ((h % 4) * 2 + h // 4) * HEAD_DIM))
    jobs += [(wout_hbm, r, WEIGHT_ROWS, 0, D_MODEL, wout_v, r) for r in range(ATT_W, D_MODEL, WEIGHT_ROWS)]

    windows = []
    for buf in stage_bufs:
        views = [buf.at[s] for s in range(buf.shape[0])] if len(buf.shape) == 3 else [buf]
        for v in views:
            windows += [(v, r, c) for r in range(0, v.shape[0] - WEIGHT_ROWS + 1, WEIGHT_ROWS)
                        for c in range(0, v.shape[1] - half + 1, half)]
    nslot = len(windows)
    assert nslot <= sems.shape[0]

    def window(i, nrows, ncols):
        v, r, c = windows[i % nslot]
        return v.at[pl.ds(r, nrows), pl.ds(c, ncols)]

    def copy(i):
        src, r0, nrows, c0, ncols, _, _ = jobs[i]
        return pltpu.make_async_copy(src.at[pl.ds(r0, nrows), pl.ds(c0, ncols)], window(i, nrows, ncols),
                                     sems.at[i % nslot])

    for i in range(min(nslot - 1, len(jobs))):
        copy(i).start()
    for i, (_, _, nrows, c0, ncols, dst, d0) in enumerate(jobs):
        if i + nslot - 1 < len(jobs):
            copy(i + nslot - 1).start()
        copy(i).wait()
        dst[d0:d0 + nrows, c0:c0 + ncols] = window(i, nrows, ncols)[...].astype(BF16)


def _pair_kernel(xin_ref, xres_ref, meta_ref, xs_ref, ymix_s_ref,
                 w_inm_ref, w_ing_ref, wconv_ref, bconv_ref, wqk_ref, bi_ref, bf_ref, sinks_ref,
                 gattn_ref, gml_ref,
                 wout_hbm, ln1g_ref, ln1b_ref, wg_hbm, wu_hbm, wd_hbm, ln2g_ref, ln2b_ref,
                 y_ref, ys_ref, pk_ref, pv_ref, pconv_ref, pc_ref, pn_ref, pm_ref,
                 wout_v, wg_v, wu_v, wd_v, wstage, wsems,
                 kplo_i, kphi_i, vplo_i, vphi_i, cb_i, ctn_i, mst_i,
                 zcur, znext, kvlast, kplo, kphi, vplo, vphi, cbuf, ctn, mst, yprev, *, npair, n_pair_steps):
    t = pl.program_id(0)
    p = lax.rem(t + (npair - 1), npair)
    gb_ref = (bi_ref, bf_ref)

    @pl.when(t == 0)
    def _init_pipeline():
        _load_ffn_weights(wout_hbm, wg_hbm, wu_hbm, wd_hbm, wout_v, wg_v, wu_v, wd_v,
                          (wstage, zcur, znext), wsems)
        zcur[...] = jnp.zeros_like(zcur)
        yprev[...] = jnp.zeros_like(yprev)
        _meta_state(meta_ref, w_inm_ref, w_ing_ref, wconv_ref, bconv_ref, wqk_ref, gb_ref,
                    sinks_ref, gattn_ref, gml_ref, kplo_i, kphi_i, vplo_i, vphi_i, cb_i, ctn_i, mst_i,
                    znext.at[pl.ds(0, BLOCK)], cbuf)

    @pl.when((p == 0) | (t == 0))
    def _load_meta_state():
        kplo[...] = kplo_i[...]
        kphi[...] = kphi_i[...]
        vplo[...] = vplo_i[...]
        vphi[...] = vphi_i[...]
        cbuf[0:8, :] = cb_i[...]
        ctn[...] = ctn_i[...]
        mst[...] = mst_i[...]

    ffn_w = (wout_v, ln1g_ref, ln1b_ref, wg_v, wu_v, wd_v, ln2g_ref, ln2b_ref)

    @pl.when(t < n_pair_steps)
    def _pair_step():
        ffn = _FfnStages(xres_ref[...], yprev[...], *ffn_w)
        xb_b = xin_ref[...].astype(BF16)

        n_proj = -(-IN_PAD // PROJ_CHUNK)

        def proj(n):
            lo, hi = n * PROJ_CHUNK, min((n + 1) * PROJ_CHUNK, IN_PAD)
            znext[:, lo:hi] = _project(xb_b, (w_inm_ref, w_ing_ref), lo, hi)

        proj_it, ffn_it = iter(range(n_proj)), iter(range(ffn.N_CHUNKS))

        def fill_proj(n):
            for c in [c for _, c in zip(range(n), proj_it)]:
                proj(c)

        def fill_ffn(n):
            for c in [c for _, c in zip(range(n), ffn_it)]:
                ffn.chunk(c)

        out = {}

        def hook(name, index=0):
            if name == "scores":
                fill_proj(2)
                ffn.head()
            elif name == "front":
                fill_proj(1)
            elif name == "gates":
                fill_ffn(1)
            elif name == "softmax":
                if index % 2 == 1:
                    fill_ffn(1)
            elif name in ("pv", "mlstm_a", "mlstm_b"):
                fill_ffn(1)
            elif name == "mlstm_c":
                fill_proj(1)
            elif name == "attnorm":
                fill_ffn(ffn.N_CHUNKS)
                out["y"] = ffn.tail()
                fill_proj(1)
            elif name == "headnorm":
                fill_proj(1)

        state = (kplo[...], kphi[...], vplo[...], vphi[...], [ctn[h] for h in range(M_HEADS)],
                 [mst[h:h + 1, 0:1] for h in range(M_HEADS)])
        thr_first = jnp.where(p == 0, META_PAD, 0)
        kvlast[...] = zcur[(STEP_BLOCKS - 1) * BLOCK:STEP_BLOCKS * BLOCK, ATT_W:ATT_W + 2 * KV_W]
        ymix, (kp_lo, kp_hi, vp_lo, vp_hi, c_state, m_state) = _mix_blocks(
            zcur, STEP_BLOCKS, 0, thr_first, state, cbuf, wconv_ref, bconv_ref, wqk_ref, gb_ref,
            sinks_ref, gattn_ref, gml_ref, hook)
        fill_proj(n_proj)
        y_ref[...] = out["y"]
        kplo[...] = kp_lo
        kphi[...] = kp_hi
        vplo[...] = vp_lo
        vphi[...] = vp_hi
        for h in range(M_HEADS):
            ctn[h] = c_state[h]
            mst[h:h + 1, :] = jnp.broadcast_to(m_state[h], (1, LANES))
        yprev[...] = ymix
        zcur[...] = znext[...]

    @pl.when(t >= n_pair_steps)
    def _sample_ffn_step():
        ys_ref[...] = _FfnStages(xs_ref[...], ymix_s_ref[...], *ffn_w).run()

    @pl.when((p == npair - 1) & (t > 0) & (t < n_pair_steps))
    def _final():
        pk_ref[...] = kvlast[:, 0:KV_W].T
        pv_ref[...] = kvlast[:, KV_W:2 * KV_W].T
        seq = lax.div(t - 1, npair)
        for j in range(CONV_W - 1):
            pconv_ref[j, pl.ds(seq, 1), :] = cbuf[8 - (CONV_W - 1) + j:8 - (CONV_W - 1) + j + 1, :]
        head_lane = lax.broadcasted_iota(jnp.int32, (1, LANES), 1)
        m_row = jnp.zeros((1, LANES), F32)
        for h in range(M_HEADS):
            m_row = jnp.where(head_lane == h, mst[h:h + 1, :], m_row)
        pm_ref[pl.ds(seq, 1), :] = m_row[:, 0:M_HEADS]
        for h in range(M_HEADS):
            c_n = ctn[h].T
            pc_ref[h] = c_n[0:M_HEAD_DIM, :]
            pn_ref[h:h + 1, :] = c_n[M_HEAD_DIM:M_HEAD_DIM + 1, :]


def _sample_kernel(xs_ref, ck_ref, cv_ref, sconv_ref, sc_ref, sn_ref, sm_ref,
                   w_inm_ref, w_ing_ref, wconv_ref, bconv_ref, wqk_ref, bi_ref, bf_ref, sinks_ref,
                   gattn_ref, gml_ref,
                   ymix_ref, sk_ref, sv_ref, scv_ref, sco_ref, sno_ref, smo_ref, cext, m_in, m_acc, m_rows):
    ns, L = SEQ_TILE, 8
    gb_ref = (bi_ref, bf_ref)
    step = pl.program_id(0)

    @pl.when(step == 0)
    def _transpose_m_state():
        m_in[...] = jnp.zeros_like(m_in)
        m_in[0:M_HEADS, :] = sm_ref[...]
        m_in[...] = m_in[...].T

    xs = xs_ref[...].reshape(ns * L, D_MODEL)
    z = _project(xs.astype(BF16), (w_inm_ref, w_ing_ref))

    lane = lax.broadcasted_iota(jnp.int32, (BLOCK, LANES), 1)
    lo_half = lane < HEAD_DIM
    lo3 = lax.broadcasted_iota(jnp.int32, (ns, L, LANES), 2) < HEAD_DIM
    ti = lax.broadcasted_iota(jnp.int32, (BLOCK, BLOCK), 0)
    si = lax.broadcasted_iota(jnp.int32, (BLOCK, BLOCK), 1)
    same_seq = (ti >> 3) == (si >> 3)
    tl, sl_ = ti & 7, si & 7
    causal_seq = same_seq & (sl_ <= tl)

    k_new = z[:, ATT_W:ATT_W + KV_W]
    v_new = z[:, ATT_W + KV_W:ATT_W + 2 * KV_W]
    kb, vb = k_new.astype(BF16), v_new.astype(BF16)
    zero_b = jnp.zeros_like(kb)
    kn_mask = jnp.concatenate([jnp.where(lo_half, kb, zero_b), jnp.where(lo_half, zero_b, kb)], axis=0)
    vn_mask = jnp.concatenate([jnp.where(lo_half, vb, zero_b), jnp.where(lo_half, zero_b, vb)], axis=0)
    qg = [_paired_q_group(lambda g: z[:, g * LANES:(g + 1) * LANES], r) for r in range(4)]
    s_new = _dot_nt(jnp.concatenate(qg, axis=0).astype(BF16), kn_mask)
    qg3 = [q.reshape(ns, L, LANES) for q in qg]
    q8 = jnp.concatenate([jnp.where(lo3, q, 0.0) for q in qg3] + [jnp.where(lo3, 0.0, q) for q in qg3],
                         axis=1).astype(BF16)
    s_cache = jnp.einsum('nqd,ndk->nqk', q8, ck_ref[...].astype(BF16), preferred_element_type=F32)

    c3 = z[:, COL_CONV:COL_VM].reshape(ns, L, MLSTM_W)
    cext[:, 8:16, :] = c3
    for j in range(CONV_W - 1):
        cext[:, 8 - (CONV_W - 1) + j, :] = sconv_ref[j]
        scv_ref[j] = cext[:, 16 - (CONV_W - 1) + j, :]
    c_act = _conv_silu(lambda jj: cext[:, 5 + jj:5 + jj + L, :], wconv_ref, bconv_ref)
    c_act = c_act.reshape(BLOCK, MLSTM_W)
    lblk = jnp.where(causal_seq, 1.0, 0.0).astype(BF16)
    lseq = jnp.where(same_seq, 1.0, 0.0).astype(BF16)
    xg, blast = _gate_columns(z[:, IN_MAIN:IN_PAD], gb_ref, None, lblk, extra=lseq)
    xgt = xg.T
    m_tile = m_in[pl.ds(pl.multiple_of(step * ns, ns), ns), :]
    m_rep = jnp.concatenate([jnp.broadcast_to(m_tile[g:g + 1, :], (L, LANES)) for g in range(ns)], axis=0)
    hs = range(M_HEADS)
    qk = [_dot(c_act[:, h * M_HEAD_DIM:(h + 1) * M_HEAD_DIM].astype(BF16), wqk_ref[h]) for h in hs]
    qm = [qk[h][:, :M_HEAD_DIM] for h in hs]
    qm_b = [qm[h].astype(BF16) for h in hs]
    km = [qk[h][:, M_HEAD_DIM:] * (M_HEAD_DIM ** -0.5) for h in hs]
    km_b = [km[h].astype(BF16) for h in hs]
    vh_b = [z[:, COL_VM + h * M_HEAD_DIM:COL_VM + (h + 1) * M_HEAD_DIM].astype(BF16) for h in hs]
    c_old = [sc_ref[:, h] for h in hs]
    s_qk = [_dot_nt(qm_b[h], km_b[h]) for h in hs]
    cq = [_dot_nt(qm_b[h], c_old[h].reshape(ns * 128, 128).astype(BF16)) for h in hs]

    mask_c = si > tl
    dist_c = (WINDOW + tl - si).astype(F32)
    dist_n = (tl - sl_).astype(F32)
    pc_parts, pn_rows, inv = {}, [], {}
    for r in range(4):
        pn = []
        for c in range(2):
            h = r + 4 * c
            sc_c = s_cache[:, c * 32 + r * L:c * 32 + (r + 1) * L, :].reshape(BLOCK, LANES)
            sc_n = s_new[r * LANES:(r + 1) * LANES, c * BLOCK:(c + 1) * BLOCK]
            (p_c, p_n), inv[h] = _softmax_parts(
                [(sc_c, mask_c, -ALIBI_SLOPES[h] * dist_c), (sc_n, causal_seq, -ALIBI_SLOPES[h] * dist_n)],
                sinks_ref[h])
            pc_parts[(c, r)] = p_c.reshape(ns, L, LANES)
            pn.append(p_n.astype(BF16))
        pn_rows.append(jnp.concatenate(pn, axis=1))
    p8 = jnp.concatenate([pc_parts[(c, r)] for c in range(2) for r in range(4)], axis=1).astype(BF16)
    o_cache = jnp.einsum('nqk,ndk->nqd', p8, cv_ref[...].astype(BF16), preferred_element_type=F32)
    o_new = _dot(jnp.concatenate(pn_rows, axis=0), vn_mask)
    groups = []
    for r in range(4):
        oa = o_cache[:, r * L:(r + 1) * L, :].reshape(BLOCK, LANES)
        ob = o_cache[:, 32 + r * L:32 + (r + 1) * L, :].reshape(BLOCK, LANES)
        o = jnp.where(lo_half, oa, ob) + o_new[r * LANES:(r + 1) * LANES]
        groups.append(o * jnp.where(lo_half, inv[r], inv[r + 4]))
    y_att = _attn_head_norm(jnp.concatenate(groups, axis=1), gattn_ref[...])

    sk_ref[:, 0:WINDOW - L, :] = jnp.swapaxes(ck_ref[...], 1, 2)[:, L:WINDOW, :]
    sk_ref[:, WINDOW - L:WINDOW, :] = k_new.reshape(ns, L, LANES)
    sv_ref[:, 0:WINDOW - L, :] = jnp.swapaxes(cv_ref[...], 1, 2)[:, L:WINDOW, :]
    sv_ref[:, WINDOW - L:WINDOW, :] = v_new.reshape(ns, L, LANES)

    b_c = [xg[:, 4 + h:5 + h] for h in hs]
    li_c = [xg[:, h:h + 1] for h in hs]
    b_r = [xgt[4 + h:5 + h, :] for h in hs]
    li_r = [xgt[h:h + 1, :] for h in hs]
    bl_c = [blast[:, 4 + h:5 + h] for h in hs]
    m_prev = [m_rep[:, h:h + 1] for h in hs]
    dmat = [jnp.where(causal_seq, b_c[h] - b_r[h] + li_r[h], -jnp.inf) for h in hs]
    m_inter = [b_c[h] + m_prev[h] for h in hs]
    m_t = [jnp.maximum(m_inter[h], jnp.max(dmat[h], axis=1, keepdims=True)) for h in hs]
    w_inter = [jnp.exp(m_inter[h] - m_t[h]) for h in hs]
    sc = [s_qk[h] * jnp.exp(dmat[h] - m_t[h]) for h in hs]
    num_intra = [_dot(sc[h].astype(BF16), vh_b[h]) for h in hs]

    m_end3 = [jnp.broadcast_to(m_t[h], (BLOCK, LANES)).reshape(ns, L, LANES)[:, L - 1:L, :] for h in hs]
    m_end = [jnp.broadcast_to(m_end3[h], (ns, L, LANES)).reshape(BLOCK, LANES)[:, 0:1] for h in hs]
    decay = [jnp.exp(bl_c[h] + m_prev[h] - m_end[h]) for h in hs]
    kw = [km[h] * jnp.exp(bl_c[h] - b_c[h] + li_c[h] - m_end[h]) for h in hs]
    seq_of_row = ti >> 3
    m_all = jnp.zeros((BLOCK, LANES), F32)
    for h in hs:
        kw_b = kw[h].astype(BF16)
        k_big = jnp.concatenate(
            [jnp.where(seq_of_row == g, kw_b, jnp.zeros_like(kw_b)) for g in range(ns)], axis=1)
        c_up = _dot_tn(vh_b[h], k_big)
        c_up3 = jnp.stack([c_up[:, g * M_HEAD_DIM:(g + 1) * M_HEAD_DIM] for g in range(ns)], axis=0)
        decay3 = jnp.broadcast_to(decay[h], (BLOCK, LANES)).reshape(ns, L, LANES)[:, 0:1, :]
        n_old = sn_ref[:, h:h + 1, :]
        sco_ref[:, h] = decay3 * c_old[h] + c_up3
        sno_ref[:, h:h + 1, :] = decay3 * n_old + jnp.sum(kw[h].reshape(ns, L, LANES), axis=1, keepdims=True)
        m_all = jnp.where(lane == h, m_t[h], m_all)
    m_rows[...] = m_all.reshape(ns, L, LANES)
    m_acc[pl.ds(pl.multiple_of(step * ns, ns), ns), :] = m_rows[:, L - 1, :]

    y_m = []
    for h in hs:
        num_inter = jnp.concatenate([cq[h][g * L:(g + 1) * L, g * M_HEAD_DIM:(g + 1) * M_HEAD_DIM] for g in range(ns)], axis=0)
        n_rep = jnp.broadcast_to(sn_ref[:, h:h + 1, :], (ns, L, LANES)).reshape(BLOCK, LANES)
        num = num_intra[h] + w_inter[h] * num_inter
        den = jnp.sum(sc[h], axis=1, keepdims=True) + w_inter[h] * jnp.sum(qm[h] * n_rep, axis=1, keepdims=True)
        hh = num / jnp.maximum(jnp.abs(den), jnp.exp(-m_t[h]))
        hm = _sigmoid(z[:, COL_O + h * M_HEAD_DIM:COL_O + (h + 1) * M_HEAD_DIM]) * hh
        y_m.append(_mlstm_head_norm(hm, gml_ref[:, h * M_HEAD_DIM:(h + 1) * M_HEAD_DIM]))

    ymix_ref[...] = jnp.concatenate([y_att] + y_m, axis=1).astype(BF16)

    @pl.when(step == pl.num_programs(0) - 1)
    def _write_m_state():
        smo_ref[...] = m_acc[...].T[0:M_HEADS, :]


def _const_spec(shape):
    nd = len(shape)
    return pl.BlockSpec(shape, lambda *_: (0,) * nd, pipeline_mode=pl.Buffered(1))


def _mixer_weight_specs():
    return [
        _const_spec((D_MODEL, IN_MAIN + IN_GATES)),
        _const_spec((D_MODEL, IN_PAD - IN_MAIN)),
        _const_spec((CONV_W, MLSTM_W)),
        _const_spec((1, MLSTM_W)),
        _const_spec((M_HEADS, M_HEAD_DIM, 2 * M_HEAD_DIM)),
        pl.BlockSpec(memory_space=pltpu.SMEM),
        pl.BlockSpec(memory_space=pltpu.SMEM),
        pl.BlockSpec(memory_space=pltpu.SMEM),
        _const_spec((1, ATT_W)),
        _const_spec((1, MLSTM_W)),
    ]


def _post_weight_specs(d):
    hbm = pl.BlockSpec(memory_space=pl.ANY)
    return [hbm, _const_spec((1, d)), _const_spec((1, d)), hbm, hbm, hbm, _const_spec((1, d)), _const_spec((1, d))]


def _prompt_layer(x_prompt, meta_tokens, x_sample_rows, ymix_sample, mixer_w, post_w):
    b, s, d = x_prompt.shape
    state_shapes = [((BLOCK, KV_W), BF16)] * 4 + [
        ((8, MLSTM_W), F32), ((M_HEADS, M_HEAD_DIM, 2 * M_HEAD_DIM), F32), ((8, LANES), F32)]
    rows = STEP_BLOCKS * BLOCK
    npair = s // rows
    total = b * npair
    n_pair_steps = total + 2
    n_sample_steps = x_sample_rows.shape[0] // rows
    steps = n_pair_steps + n_sample_steps

    def block_map(lag):
        def index(t):
            tb = jnp.clip(t - lag, 0, total - 1)
            return (tb // npair, tb % npair, 0)
        return index

    def smap(t):
        return (jnp.clip(t - 1, 0, total - 1) // npair, 0, 0)

    def sample_map(t):
        return (jnp.clip(t - n_pair_steps, 0, n_sample_steps - 1), 0)

    out_shape = (
        jax.ShapeDtypeStruct((b, s, d), F32),
        jax.ShapeDtypeStruct(x_sample_rows.shape, F32),
        jax.ShapeDtypeStruct((b, BLOCK, KV_W), F32),
        jax.ShapeDtypeStruct((b, BLOCK, KV_W), F32),
        jax.ShapeDtypeStruct((CONV_W - 1, b, MLSTM_W), F32),
        jax.ShapeDtypeStruct((b, M_HEADS, M_HEAD_DIM, M_HEAD_DIM), F32),
        jax.ShapeDtypeStruct((b, M_HEADS, M_HEAD_DIM), F32),
        jax.ShapeDtypeStruct((b, M_HEADS), F32),
    )
    out_specs = (
        pl.BlockSpec((None, rows, d), block_map(2)),
        pl.BlockSpec((rows, d), sample_map),
        pl.BlockSpec((None, BLOCK, KV_W), smap),
        pl.BlockSpec((None, BLOCK, KV_W), smap),
        pl.BlockSpec((CONV_W - 1, b, MLSTM_W), lambda t: (0, 0, 0)),
        pl.BlockSpec((None, M_HEADS, M_HEAD_DIM, M_HEAD_DIM), lambda t: smap(t) + (0,)),
        pl.BlockSpec((None, M_HEADS, M_HEAD_DIM), smap),
        pl.BlockSpec((b, M_HEADS), lambda t: (0, 0)),
    )
    return pl.pallas_call(
        functools.partial(_pair_kernel, npair=npair, n_pair_steps=n_pair_steps),
        grid=(steps,),
        in_specs=[pl.BlockSpec((None, rows, d), block_map(0)), pl.BlockSpec((None, rows, d), block_map(2)),
                  _const_spec((N_META, d)), pl.BlockSpec((rows, d), sample_map), pl.BlockSpec((rows, d), sample_map)]
        + _mixer_weight_specs() + _post_weight_specs(d),
        out_specs=out_specs,
        out_shape=out_shape,
        scratch_shapes=[
            pltpu.VMEM((d, d), BF16), pltpu.VMEM((d, D_FF), BF16), pltpu.VMEM((d, D_FF), BF16),
            pltpu.VMEM((D_FF, d), BF16),
            pltpu.VMEM((WEIGHT_SLOTS, WEIGHT_ROWS, D_FF), F32),
            pltpu.SemaphoreType.DMA((2 * WEIGHT_SLOTS + 2 * STEP_BLOCKS,)),
        ] + [pltpu.VMEM(shape, dt) for shape, dt in state_shapes] + [
            pltpu.VMEM((rows, IN_PAD), F32),
            pltpu.VMEM((rows, IN_PAD), F32),
            pltpu.VMEM((BLOCK, 2 * KV_W), F32),
        ] + [pltpu.VMEM((BLOCK, KV_W), BF16)] * 4 + [
            pltpu.VMEM((rows + 8, MLSTM_W), F32),
            pltpu.VMEM((M_HEADS, M_HEAD_DIM, 2 * M_HEAD_DIM), F32),
            pltpu.VMEM((8, LANES), F32),
            pltpu.VMEM((rows, d), BF16),
        ],
        compiler_params=pltpu.CompilerParams(
            dimension_semantics=("arbitrary",), vmem_limit_bytes=VMEM_LIMIT),
        name="prompt_layer",
    )(x_prompt, x_prompt, meta_tokens, x_sample_rows, ymix_sample, *mixer_w, *post_w)


def _sample_mixer(x_sample, ck, cv, s_conv, s_c, s_n, s_m, mixer_w):
    n, l, d = x_sample.shape
    assert n == LANES, "the m-state transposes assume one lane per sequence"
    t = SEQ_TILE
    m3 = lambda i: (i, 0, 0)
    m4 = lambda i: (i, 0, 0, 0)
    out_shape = (
        jax.ShapeDtypeStruct((n * l, d), BF16),
        jax.ShapeDtypeStruct((n, WINDOW, KV_W), F32),
        jax.ShapeDtypeStruct((n, WINDOW, KV_W), F32),
        jax.ShapeDtypeStruct((CONV_W - 1, n, MLSTM_W), F32),
        jax.ShapeDtypeStruct((n, M_HEADS, M_HEAD_DIM, M_HEAD_DIM), F32),
        jax.ShapeDtypeStruct((n, M_HEADS, M_HEAD_DIM), F32),
        jax.ShapeDtypeStruct((M_HEADS, n), F32),
    )
    m_state_spec = pl.BlockSpec((M_HEADS, n), lambda i: (0, 0))
    out_specs = (
        pl.BlockSpec((t * l, d), lambda i: (i, 0)),
        pl.BlockSpec((t, WINDOW, KV_W), m3),
        pl.BlockSpec((t, WINDOW, KV_W), m3),
        pl.BlockSpec((CONV_W - 1, t, MLSTM_W), lambda i: (0, i, 0)),
        pl.BlockSpec((t, M_HEADS, M_HEAD_DIM, M_HEAD_DIM), m4),
        pl.BlockSpec((t, M_HEADS, M_HEAD_DIM), m3),
        m_state_spec,
    )
    in_specs = [
        pl.BlockSpec((t, l, d), m3),
        pl.BlockSpec((t, WINDOW, KV_W), m3),
        pl.BlockSpec((t, WINDOW, KV_W), m3),
        pl.BlockSpec((CONV_W - 1, t, MLSTM_W), lambda i: (0, i, 0)),
        pl.BlockSpec((t, M_HEADS, M_HEAD_DIM, M_HEAD_DIM), m4),
        pl.BlockSpec((t, M_HEADS, M_HEAD_DIM), m3),
        m_state_spec,
    ] + _mixer_weight_specs()
    return pl.pallas_call(
        _sample_kernel,
        grid=(n // t,),
        in_specs=in_specs,
        out_specs=out_specs,
        out_shape=out_shape,
        scratch_shapes=[pltpu.VMEM((t, 16, MLSTM_W), F32),
                        pltpu.VMEM((LANES, LANES), F32),
                        pltpu.VMEM((LANES, LANES), F32),
                        pltpu.VMEM((t, l, LANES), F32)],
        compiler_params=pltpu.CompilerParams(
            dimension_semantics=("arbitrary",), vmem_limit_bytes=VMEM_LIMIT),
        name="sample_mixer",
    )(x_sample, ck, cv, s_conv, s_c, s_n, s_m, *mixer_w)


def kernel(x_prompt, x_sample, cache_k, cache_v, state_conv, state_C, state_n, state_m, meta_tokens,
           w_in, w_conv, b_conv, w_mq, w_mk, b_i, b_f, attn_sinks, g_attn, g_mlstm, w_out,
           ln1_g, ln1_b, w_gate, w_up, w_down, ln2_g, ln2_b):
    b, s, d = x_prompt.shape
    n, l, _ = x_sample.shape

    w0 = w_in[0].astype(BF16)
    w_in_parts = (w0, jnp.pad(w0[:, IN_MAIN:], ((0, 0), (0, IN_PAD - IN_MAIN - IN_GATES))))
    wqk = jnp.concatenate([w_mq[0], w_mk[0]], axis=-1).astype(BF16)
    mixer_w = w_in_parts + (w_conv[0], b_conv[0].reshape(1, MLSTM_W), wqk, b_i[0], b_f[0], attn_sinks[0],
                            _pair_heads(g_attn[0].reshape(1, ATT_W), 1), g_mlstm[0].reshape(1, MLSTM_W))
    post_w = (w_out[0], ln1_g[0].reshape(1, d), ln1_b[0].reshape(1, d), w_gate[0], w_up[0], w_down[0],
              ln2_g[0].reshape(1, d), ln2_b[0].reshape(1, d))

    key_minor = lambda c: c[0].transpose(0, 2, 3, 1).reshape(n, KV_W, WINDOW)
    ymix_s, sk, sv, scv, s_c, s_n, s_mo = _sample_mixer(
        x_sample, key_minor(cache_k), key_minor(cache_v), state_conv[0].swapaxes(0, 1), state_C[0], state_n[0],
        state_m[0].T, mixer_w)

    y_prompt, y_sample, pk, pv, pconv, p_c, p_n, p_m = _prompt_layer(
        x_prompt, meta_tokens, x_sample.reshape(n * l, d), ymix_s, mixer_w, post_w)

    kv5 = lambda a: a.reshape(1, a.shape[0], WINDOW, 2, HEAD_DIM)
    kv5_key_minor = lambda a: a.reshape(a.shape[0], 2, HEAD_DIM, WINDOW).transpose(0, 3, 1, 2)[None]
    return (y_prompt, y_sample.reshape(n, l, d),
            kv5_key_minor(pk), kv5_key_minor(pv), pconv.swapaxes(0, 1)[None], p_c[None], p_n[None], p_m[None],
            kv5(sk), kv5(sv), scv.swapaxes(0, 1)[None], s_c[None], s_n[None], s_mo.T[None])
```

```python
import functools

import jax
import jax.numpy as jnp
from jax import lax
from jax.experimental import pallas as pl
from jax.experimental.pallas import tpu as pltpu

F32 = jnp.float32
BF16 = jnp.bfloat16

D_MODEL = 1024
ATT_W = 512
MLSTM_W = 512
HEAD_DIM = 64
N_HEADS = 8
KV_W = 128
WINDOW = 128
BLOCK = 128
M_HEADS = 4
M_HEAD_DIM = 128
CONV_W = 4
N_META = 16
META_PAD = BLOCK - N_META
D_FF = 2816
LANES = 128
COL_CONV = ATT_W + 2 * KV_W
COL_VM = COL_CONV + MLSTM_W
COL_O = COL_VM + MLSTM_W
IN_MAIN = COL_O + MLSTM_W
IN_GATES = 2 * M_HEADS
IN_PAD = IN_MAIN + LANES
DEPTH = 1
ALIBI_SLOPES = tuple(2.0 ** (-8.0 * (h + 1) / N_HEADS) for h in range(N_HEADS))
DEEPNORM_ALPHA = (2.0 * DEPTH) ** 0.25
EPS = 1e-5
SEQ_TILE = 16
STEP_BLOCKS = 2
FF_CHUNK = 256
FF_LOOKAHEAD = 3
PROJ_CHUNK = 256
WEIGHT_ROWS = 128
WEIGHT_SLOTS = 4
VMEM_LIMIT = 56 * 1024 * 1024


def _paired_q_group(q_group, r):
    a, b = q_group(r // 2), q_group(r // 2 + N_HEADS // 4)
    lo_half = lax.broadcasted_iota(jnp.int32, a.shape, 1) < HEAD_DIM
    if r % 2 == 0:
        return jnp.where(lo_half, a, pltpu.roll(b, HEAD_DIM, 1))
    return jnp.where(lo_half, pltpu.roll(a, HEAD_DIM, 1), b)


def _dot(a, b):
    return jnp.dot(a, b, preferred_element_type=F32)


def _dot_nt(a, b):
    return lax.dot_general(a, b, (((1,), (1,)), ((), ())), preferred_element_type=F32)


def _dot_tn(a, b):
    return lax.dot_general(a, b, (((0,), (0,)), ((), ())), preferred_element_type=F32)


def _split2(x):
    hi = x.astype(BF16)
    lo = (x - hi.astype(F32)).astype(BF16)
    return hi, lo


def _sigmoid(x):
    return 1.0 / (1.0 + jnp.exp(-x))


def _log_sigmoid(x):
    return -(jnp.maximum(-x, 0.0) + jnp.log1p(jnp.exp(-jnp.abs(x))))


def _layer_norm(x, g, b):
    mu = jnp.mean(x, axis=-1, keepdims=True)
    xc = x - mu
    var = jnp.mean(xc * xc, axis=-1, keepdims=True)
    return xc * lax.rsqrt(var + EPS) * g + b


def _attn_head_norm(att, g):
    lo_half = lax.broadcasted_iota(jnp.int32, (att.shape[0], LANES), 1) < HEAD_DIM
    g8 = jnp.broadcast_to(g, (8, ATT_W))
    g_group = lambda i: g8[:, i * LANES:(i + 1) * LANES]

    def seg_mean(x):
        s_all = jnp.sum(x, axis=1, keepdims=True)
        s_lo = jnp.sum(jnp.where(lo_half, x, 0.0), axis=1, keepdims=True)
        return jnp.where(lo_half, s_lo, s_all - s_lo) * (1.0 / HEAD_DIM)

    out = []
    for grp in range(att.shape[1] // LANES):
        x = att[:, grp * LANES:(grp + 1) * LANES]
        xc = x - seg_mean(x)
        out.append(xc * lax.rsqrt(seg_mean(xc * xc) + EPS) * _paired_q_group(g_group, grp)[0:1, :])
    return jnp.concatenate(out, axis=1)


def _mlstm_head_norm(hm, g):
    mu = jnp.mean(hm, axis=-1, keepdims=True)
    xc = hm - mu
    var = jnp.mean(xc * xc, axis=-1, keepdims=True)
    return xc * lax.rsqrt(var + EPS) * g


def _softmax_parts(parts, sink):
    sp = [jnp.where(m, s * (HEAD_DIM ** -0.5) + a, -jnp.inf) for s, m, a in parts]
    mx = sink
    for s in sp:
        mx = jnp.maximum(mx, jnp.max(s, axis=1, keepdims=True))
    ps = [jnp.exp(s - mx) for s in sp]
    den = jnp.exp(sink - mx)
    for p in ps:
        den = den + jnp.sum(p, axis=1, keepdims=True)
    return ps, 1.0 / den


def _conv_silu(window, wconv_ref, bconv_ref):
    acc = bconv_ref[...]
    for j in range(CONV_W):
        acc = acc + window(j) * wconv_ref[j:j + 1, :]
    return acc * _sigmoid(acc)


def _gate_columns(gates, gb_ref, row_valid, ltri, extra=None):
    lane = lax.broadcasted_iota(jnp.int32, gates.shape, 1)
    bi_ref, bf_ref = gb_ref
    lane1 = lax.broadcasted_iota(jnp.int32, (1, LANES), 1)
    bias = jnp.zeros((1, LANES), F32)
    for h in range(M_HEADS):
        bias = jnp.where(lane1 == h, bi_ref[h], jnp.where(lane1 == M_HEADS + h, bf_ref[h], bias))
    gb = gates + bias
    is_i = lane < M_HEADS
    is_f = (lane >= M_HEADS) & (lane < 2 * M_HEADS)
    logf = _log_sigmoid(gb)
    if row_valid is not None:
        lf = jnp.where(is_f & row_valid, logf, 0.0)
        li = jnp.where(row_valid, gb, -jnp.inf)
    else:
        lf = jnp.where(is_f, logf, 0.0)
        li = gb
    hi_lo = jnp.concatenate(_split2(lf), axis=1)
    bc = _dot(ltri, hi_lo)
    x = jnp.where(is_i, li, bc[:, :LANES] + bc[:, LANES:])
    if extra is None:
        return x, None
    be = _dot(extra, hi_lo)
    return x, be[:, :LANES] + be[:, LANES:]


class _FfnStages:
    N_CHUNKS = D_FF // FF_CHUNK

    def __init__(self, x, ymix, wout_ref, ln1g_ref, ln1b_ref, wg_ref, wu_ref, wd_ref, ln2g_ref, ln2b_ref):
        self.x, self.ymix = x, ymix
        self.wout_ref, self.ln1g_ref, self.ln1b_ref = wout_ref, ln1g_ref, ln1b_ref
        self.wg_ref, self.wu_ref, self.wd_ref = wg_ref, wu_ref, wd_ref
        self.ln2g_ref, self.ln2b_ref = ln2g_ref, ln2b_ref

    def head(self):
        self.x1 = _layer_norm(DEEPNORM_ALPHA * self.x + _dot(self.ymix, self.wout_ref[...]),
                              self.ln1g_ref[...], self.ln1b_ref[...])
        self.x1b = self.x1.astype(BF16)
        self.acc = DEEPNORM_ALPHA * self.x1

    def _gate_up(self, c):
        cs = slice(c * FF_CHUNK, (c + 1) * FF_CHUNK)
        return _dot(self.x1b, self.wg_ref[:, cs]), _dot(self.x1b, self.wu_ref[:, cs])

    def chunk(self, c):
        if c == 0:
            self.gu = {}
        for n in range(c, min(c + FF_LOOKAHEAD, self.N_CHUNKS - 1) + 1):
            if n not in self.gu:
                self.gu[n] = self._gate_up(n)
        g, u = self.gu.pop(c)
        hid = (g * _sigmoid(g) * u).astype(BF16)
        self.acc = self.acc + _dot(hid, self.wd_ref[c * FF_CHUNK:(c + 1) * FF_CHUNK, :])

    def tail(self):
        return _layer_norm(self.acc, self.ln2g_ref[...], self.ln2b_ref[...])

    def run(self):
        self.head()
        for c in range(self.N_CHUNKS):
            self.chunk(c)
        return self.tail()


def _mix_blocks(zc, nb, pad_rows, thr_first, state, cbuf, wconv_ref, bconv_ref, wqk_ref, gb_ref,
                sinks_ref, gattn_ref, gml_ref, hook):
    rows = nb * BLOCK
    kp_lo, kp_hi, vp_lo, vp_hi, c_state, m_state = state
    c_state, m_state = list(c_state), list(m_state)
    lane = lax.broadcasted_iota(jnp.int32, (BLOCK, LANES), 1)
    lo_half = lane < HEAD_DIM
    row_valid = None
    if pad_rows:
        row_valid = lax.broadcasted_iota(jnp.int32, (rows, 1), 0) >= pad_rows

    s_all, vmasks = [], []
    for blk in range(nb):
        rs = slice(blk * BLOCK, (blk + 1) * BLOCK)
        kb = zc[rs, ATT_W:ATT_W + KV_W].astype(BF16)
        vb = zc[rs, ATT_W + KV_W:ATT_W + 2 * KV_W].astype(BF16)
        zero_b = jnp.zeros_like(kb)
        k_lo, k_hi = jnp.where(lo_half, kb, zero_b), jnp.where(lo_half, zero_b, kb)
        v_lo, v_hi = jnp.where(lo_half, vb, zero_b), jnp.where(lo_half, zero_b, vb)
        kmask = jnp.concatenate([kp_lo, k_lo, kp_hi, k_hi], axis=0)
        vmasks.append(jnp.concatenate([vp_lo, v_lo, vp_hi, v_hi], axis=0))
        q_group = lambda g, rs=rs: zc[rs, g * LANES:(g + 1) * LANES]
        q4 = jnp.concatenate([_paired_q_group(q_group, r) for r in range(4)], axis=0).astype(BF16)
        s_all.append(_dot_nt(q4, kmask))
        kp_lo, kp_hi, vp_lo, vp_hi = k_lo, k_hi, v_lo, v_hi
    hook("scores")

    c_in = zc[:, COL_CONV:COL_VM]
    if row_valid is not None:
        c_in = jnp.where(row_valid, c_in, 0.0)
    cbuf[8:8 + rows, :] = c_in
    c_act = _conv_silu(lambda jj: cbuf[5 + jj:5 + jj + rows, :], wconv_ref, bconv_ref)
    cbuf[0:8, :] = cbuf[rows:rows + 8, :]
    hs = range(M_HEADS)
    ca_b = [c_act[:, h * M_HEAD_DIM:(h + 1) * M_HEAD_DIM].astype(BF16) for h in hs]
    qk = [_dot(ca_b[h], wqk_ref[h]) for h in hs]
    qm_b = [qk[h][:, :M_HEAD_DIM].astype(BF16) for h in hs]
    kmt_f = [(qk[h][:, M_HEAD_DIM:] * (M_HEAD_DIM ** -0.5)).T for h in hs]
    hook("front")

    ti = lax.broadcasted_iota(jnp.int32, (rows, rows), 0)
    si = lax.broadcasted_iota(jnp.int32, (rows, rows), 1)
    ltri = jnp.where((si <= ti) & ((si >> 7) == (ti >> 7)), 1.0, 0.0).astype(BF16)
    xg, _ = _gate_columns(zc[:, IN_MAIN:IN_PAD], gb_ref, row_valid, ltri)
    xgt = xg.T
    hook("gates")

    qi = lax.broadcasted_iota(jnp.int32, (BLOCK, BLOCK), 0)
    ks = lax.broadcasted_iota(jnp.int32, (BLOCK, BLOCK), 1)
    own = ks <= qi
    distf = jnp.where(own, qi - ks, WINDOW + qi - ks).astype(F32)
    zero_p = jnp.zeros((BLOCK, BLOCK), BF16)
    att_rows = []
    for blk in range(nb):
        live = None
        if blk == 0 and thr_first is not None:
            live = jnp.where(own, ks + BLOCK, ks) >= thr_first
        p_rows, inv = [], {}
        for r in range(4):
            pr = []
            for c in range(2):
                h = r + 4 * c
                s_blk = s_all[blk][r * LANES:(r + 1) * LANES, c * 256:(c + 1) * 256]
                sp = jnp.where(own, s_blk[:, BLOCK:], s_blk[:, :BLOCK]) * (HEAD_DIM ** -0.5) \
                    - ALIBI_SLOPES[h] * distf
                if live is not None:
                    sp = jnp.where(live, sp, -jnp.inf)
                sink = sinks_ref[h]
                mx = jnp.maximum(jnp.max(sp, axis=1, keepdims=True), sink)
                p = jnp.exp(sp - mx)
                inv[h] = 1.0 / (jnp.sum(p, axis=1, keepdims=True) + jnp.exp(sink - mx))
                p_b = p.astype(BF16)
                pr += [jnp.where(own, zero_p, p_b), jnp.where(own, p_b, zero_p)]
            p_rows.append(jnp.concatenate(pr, axis=1))
            hook("softmax", blk * 4 + r)
        o_all = _dot(jnp.concatenate(p_rows, axis=0), vmasks[blk])
        att_rows.append(jnp.concatenate(
            [o_all[r * LANES:(r + 1) * LANES] * jnp.where(lo_half, inv[r], inv[r + 4]) for r in range(4)], axis=1))
        hook("pv", blk)

    ones_col = jnp.where(lane == 0, 1.0, 0.0).astype(BF16)
    tb = lax.broadcasted_iota(jnp.int32, (BLOCK, BLOCK), 0)
    sb = lax.broadcasted_iota(jnp.int32, (BLOCK, BLOCK), 1)
    tri = sb <= tb
    hh_rows = [[] for _ in hs]
    for blk in range(nb):
        rs = slice(blk * BLOCK, (blk + 1) * BLOCK)
        b_c = [xg[rs, 4 + h:5 + h] for h in hs]
        b_r = [xgt[4 + h:5 + h, rs] for h in hs]
        li_r = [xgt[h:h + 1, rs] for h in hs]
        kmt = [kmt_f[h][:, rs] for h in hs]
        v_ext = [jnp.concatenate([zc[rs, COL_VM + h * M_HEAD_DIM:COL_VM + (h + 1) * M_HEAD_DIM].astype(BF16), ones_col], axis=1)
                 for h in hs]
        s_qk = [_dot(qm_b[h][rs], kmt[h].astype(BF16)) for h in hs]
        inter = [_dot(qm_b[h][rs], c_state[h].astype(BF16)) for h in hs]
        hook("mlstm_a", blk)
        dmat = [jnp.where(tri, b_c[h] - b_r[h] + li_r[h], -jnp.inf) for h in hs]
        m_inter = [b_c[h] + m_state[h] for h in hs]
        m_t = [jnp.maximum(m_inter[h], jnp.max(dmat[h], axis=1, keepdims=True)) for h in hs]
        w_inter = [jnp.exp(m_inter[h] - m_t[h]) for h in hs]
        sc_b = [(s_qk[h] * jnp.exp(dmat[h] - m_t[h])).astype(BF16) for h in hs]
        hook("mlstm_b", blk)
        nd = [_dot(sc_b[h], v_ext[h]) + w_inter[h] * inter[h] for h in hs]
        for h in hs:
            m_end = m_t[h][BLOCK - 1:BLOCK, :]
            b_last = b_c[h][BLOCK - 1:BLOCK, :]
            decay = jnp.exp(b_last + m_state[h] - m_end)
            wk_r = jnp.exp(b_last - b_r[h] + li_r[h] - m_end)
            c_state[h] = decay * c_state[h] + _dot((kmt[h] * wk_r).astype(BF16), v_ext[h])
            m_state[h] = m_end
            num, den = nd[h][:, :M_HEAD_DIM], nd[h][:, M_HEAD_DIM:M_HEAD_DIM + 1]
            hh_rows[h].append(num / jnp.maximum(jnp.abs(den), jnp.exp(-m_t[h])))
        hook("mlstm_c", blk)

    y_att = _attn_head_norm(jnp.concatenate(att_rows, axis=0), gattn_ref[...])
    hook("attnorm")
    y_m = []
    for h in hs:
        hm = _sigmoid(zc[:, COL_O + h * M_HEAD_DIM:COL_O + (h + 1) * M_HEAD_DIM]) * jnp.concatenate(hh_rows[h], axis=0)
        y_m.append(_mlstm_head_norm(hm, gml_ref[:, h * M_HEAD_DIM:(h + 1) * M_HEAD_DIM]))
        hook("headnorm", h)
    ymix = jnp.concatenate([y_att] + y_m, axis=1).astype(BF16)
    return ymix, (kp_lo, kp_hi, vp_lo, vp_hi, c_state, m_state)


def _project(xb, w_in_refs, lo=0, hi=IN_PAD):
    bounds = (0, IN_MAIN, IN_PAD)
    parts = []
    for ref, start, stop in zip(w_in_refs, bounds[:-1], bounds[1:]):
        a, b = max(lo, start), min(hi, stop)
        if a < b:
            parts.append(_dot(xb, ref[:, a - start:b - start]))
    return parts[0] if len(parts) == 1 else jnp.concatenate(parts, axis=1)


def _meta_state(meta_ref, w_inm_ref, w_ing_ref, wconv_ref, bconv_ref, wqk_ref, gb_ref,
                sinks_ref, gattn_ref,
                gml_ref, kplo_o, kphi_o, vplo_o, vphi_o, cb_o, ctn_o, mst_o, zc, cbuf):
    xb = jnp.concatenate([jnp.zeros((META_PAD, D_MODEL), F32), meta_ref[...]], axis=0)
    zc[...] = _project(xb.astype(BF16), (w_inm_ref, w_ing_ref))
    cbuf[0:8, :] = jnp.zeros((8, MLSTM_W), F32)
    zb = jnp.zeros((BLOCK, KV_W), BF16)
    state = (zb, zb, zb, zb, [jnp.zeros((M_HEAD_DIM, 2 * M_HEAD_DIM), F32)] * M_HEADS,
             [jnp.zeros((1, 1), F32)] * M_HEADS)
    _, (kp_lo, kp_hi, vp_lo, vp_hi, c_state, m_state) = _mix_blocks(
        zc, 1, META_PAD, BLOCK + META_PAD, state, cbuf, wconv_ref, bconv_ref, wqk_ref, gb_ref,
        sinks_ref, gattn_ref, gml_ref, lambda name, index=0: None)
    kplo_o[...] = kp_lo
    kphi_o[...] = kp_hi
    vplo_o[...] = vp_lo
    vphi_o[...] = vp_hi
    cb_o[...] = cbuf[0:8, :]
    for h in range(M_HEADS):
        ctn_o[h] = c_state[h]
        mst_o[h:h + 1, :] = jnp.broadcast_to(m_state[h], (1, LANES))
    mst_o[M_HEADS:8, :] = jnp.zeros((8 - M_HEADS, LANES), F32)


def _load_ffn_weights(wout_hbm, wg_hbm, wu_hbm, wd_hbm, wout_v, wg_v, wu_v, wd_v, stage_bufs, sems):
    half = D_FF // 2
    jobs = []
    for src, dst in ((wg_hbm, wg_v), (wu_hbm, wu_v)):
        jobs += [(src, r, WEIGHT_ROWS, c, half, dst, r) for r in range(0, D_MODEL, WEIGHT_ROWS) for c in (0, half)]
    jobs += [(wd_hbm, r, WEIGHT_ROWS, 0, D_MODEL, wd_v, r) for r in range(0, D_FF, WEIGHT_ROWS)]
    for h in range(N_HEADS):
        jobs.append((wout_hbm, h * HEAD_DIM, HEAD_DIM, 0, D_MODEL, wout_v, ((h % 4) * 2 + h // 4) * HEAD_DIM))
    jobs += [(wout_hbm, r, WEIGHT_ROWS, 0, D_MODEL, wout_v, r) for r in range(ATT_W, D_MODEL, WEIGHT_ROWS)]

    windows = []
    for buf in stage_bufs:
        views = [buf.at[s] for s in range(buf.shape[0])] if len(buf.shape) == 3 else [buf]
        for v in views:
            windows += [(v, r, c) for r in range(0, v.shape[0] - WEIGHT_ROWS + 1, WEIGHT_ROWS)
                        for c in range(0, v.shape[1] - half + 1, half)]
    nslot = len(windows)
    assert nslot <= sems.shape[0]

    def window(i, nrows, ncols):
        v, r, c = windows[i % nslot]
        return v.at[pl.ds(r, nrows), pl.ds(c, ncols)]

    def copy(i):
        src, r0, nrows, c0, ncols, _, _ = jobs[i]
        return pltpu.make_async_copy(src.at[pl.ds(r0, nrows), pl.ds(c0, ncols)], window(i, nrows, ncols),
                                     sems.at[i % nslot])

    for i in range(min(nslot - 1, len(jobs))):
        copy(i).start()
    for i, (_, _, nrows, c0, ncols, dst, d0) in enumerate(jobs):
        if i + nslot - 1 < len(jobs):
            copy(i + nslot - 1).start()
        copy(i).wait()
        dst[d0:d0 + nrows, c0:c0 + ncols] = window(i, nrows, ncols)[...].astype(BF16)


def _pair_kernel(xin_ref, xres_ref, meta_ref, xs_ref, ymix_s_ref,
                 w_inm_ref, w_ing_ref, wconv_ref, bconv_ref, wqk_ref, bi_ref, bf_ref, sinks_ref,
                 gattn_ref, gml_ref,
                 wout_hbm, ln1g_ref, ln1b_ref, wg_hbm, wu_hbm, wd_hbm, ln2g_ref, ln2b_ref,
                 y_ref, ys_ref, pk_ref, pv_ref, pconv_ref, pc_ref, pn_ref, pm_ref,
                 wout_v, wg_v, wu_v, wd_v, wstage, wsems,
                 kplo_i, kphi_i, vplo_i, vphi_i, cb_i, ctn_i, mst_i,
                 zcur, znext, kvlast, kplo, kphi, vplo, vphi, cbuf, ctn, mst, yprev, *, npair, n_pair_steps):
    t = pl.program_id(0)
    p = lax.rem(t + (npair - 1), npair)
    gb_ref = (bi_ref, bf_ref)

    @pl.when(t == 0)
    def _init_pipeline():
        _load_ffn_weights(wout_hbm, wg_hbm, wu_hbm, wd_hbm, wout_v, wg_v, wu_v, wd_v,
                          (wstage, zcur, znext), wsems)
        zcur[...] = jnp.zeros_like(zcur)
        yprev[...] = jnp.zeros_like(yprev)
        _meta_state(meta_ref, w_inm_ref, w_ing_ref, wconv_ref, bconv_ref, wqk_ref, gb_ref,
                    sinks_ref, gattn_ref, gml_ref, kplo_i, kphi_i, vplo_i, vphi_i, cb_i, ctn_i, mst_i,
                    znext.at[pl.ds(0, BLOCK)], cbuf)

    @pl.when((p == 0) | (t == 0))
    def _load_meta_state():
        kplo[...] = kplo_i[...]
        kphi[...] = kphi_i[...]
        vplo[...] = vplo_i[...]
        vphi[...] = vphi_i[...]
        cbuf[0:8, :] = cb_i[...]
        ctn[...] = ctn_i[...]
        mst[...] = mst_i[...]

    ffn_w = (wout_v, ln1g_ref, ln1b_ref, wg_v, wu_v, wd_v, ln2g_ref, ln2b_ref)

    @pl.when(t < n_pair_steps)
    def _pair_step():
        ffn = _FfnStages(xres_ref[...], yprev[...], *ffn_w)
        xb_b = xin_ref[...].astype(BF16)

        n_proj = -(-IN_PAD // PROJ_CHUNK)

        def proj(n):
            lo, hi = n * PROJ_CHUNK, min((n + 1) * PROJ_CHUNK, IN_PAD)
            znext[:, lo:hi] = _project(xb_b, (w_inm_ref, w_ing_ref), lo, hi)

        proj_it, ffn_it = iter(range(n_proj)), iter(range(ffn.N_CHUNKS))

        def fill_proj(n):
            for c in [c for _, c in zip(range(n), proj_it)]:
                proj(c)

        def fill_ffn(n):
            for c in [c for _, c in zip(range(n), ffn_it)]:
                ffn.chunk(c)

        out = {}

        def hook(name, index=0):
            if name == "scores":
                fill_proj(2)
                ffn.head()
            elif name == "front":
                fill_proj(1)
            elif name == "gates":
                fill_ffn(1)
            elif name == "softmax":
                if index % 2 == 1:
                    fill_ffn(1)
            elif name in ("pv", "mlstm_a", "mlstm_b"):
                fill_ffn(1)
            elif name == "mlstm_c":
                fill_proj(1)
            elif name == "attnorm":
                fill_ffn(ffn.N_CHUNKS)
                out["y"] = ffn.tail()
                fill_proj(1)
            elif name == "headnorm":
                fill_proj(1)

        state = (kplo[...], kphi[...], vplo[...], vphi[...], [ctn[h] for h in range(M_HEADS)],
                 [mst[h:h + 1, 0:1] for h in range(M_HEADS)])
        thr_first = jnp.where(p == 0, META_PAD, 0)
        kvlast[...] = zcur[(STEP_BLOCKS - 1) * BLOCK:STEP_BLOCKS * BLOCK, ATT_W:ATT_W + 2 * KV_W]
        ymix, (kp_lo, kp_hi, vp_lo, vp_hi, c_state, m_state) = _mix_blocks(
            zcur, STEP_BLOCKS, 0, thr_first, state, cbuf, wconv_ref, bconv_ref, wqk_ref, gb_ref,
            sinks_ref, gattn_ref, gml_ref, hook)
        fill_proj(n_proj)
        y_ref[...] = out["y"]
        kplo[...] = kp_lo
        kphi[...] = kp_hi
        vplo[...] = vp_lo
        vphi[...] = vp_hi
        for h in range(M_HEADS):
            ctn[h] = c_state[h]
            mst[h:h + 1, :] = jnp.broadcast_to(m_state[h], (1, LANES))
        yprev[...] = ymix
        zcur[...] = znext[...]

    @pl.when(t >= n_pair_steps)
    def _sample_ffn_step():
        ys_ref[...] = _FfnStages(xs_ref[...], ymix_s_ref[...], *ffn_w).run()

    @pl.when((p == npair - 1) & (t > 0) & (t < n_pair_steps))
    def _final():
        pk_ref[...] = kvlast[:, 0:KV_W].T
        pv_ref[...] = kvlast[:, KV_W:2 * KV_W].T
        seq = lax.div(t - 1, npair)
        for j in range(CONV_W - 1):
            pconv_ref[j, pl.ds(seq, 1), :] = cbuf[8 - (CONV_W - 1) + j:8 - (CONV_W - 1) + j + 1, :]
        head_lane = lax.broadcasted_iota(jnp.int32, (1, LANES), 1)
        m_row = jnp.zeros((1, LANES), F32)
        for h in range(M_HEADS):
            m_row = jnp.where(head_lane == h, mst[h:h + 1, :], m_row)
        pm_ref[pl.ds(seq, 1), :] = m_row[:, 0:M_HEADS]
        for h in range(M_HEADS):
            c_n = ctn[h].T
            pc_ref[h] = c_n[0:M_HEAD_DIM, :]
            pn_ref[h:h + 1, :] = c_n[M_HEAD_DIM:M_HEAD_DIM + 1, :]


def _sample_kernel(xs_ref, ck_ref, cv_ref, sconv_ref, sc_ref, sn_ref, sm_ref,
                   w_inm_ref, w_ing_ref, wconv_ref, bconv_ref, wqk_ref, bi_ref, bf_ref, sinks_ref,
                   gattn_ref, gml_ref,
                   ymix_ref, sk_ref, sv_ref, scv_ref, sco_ref, sno_ref, smo_ref, cext, m_in, m_acc, m_rows):
    ns, L = SEQ_TILE, 8
    gb_ref = (bi_ref, bf_ref)
    step = pl.program_id(0)

    @pl.when(step == 0)
    def _transpose_m_state():
        m_in[...] = jnp.zeros_like(m_in)
        m_in[0:M_HEADS, :] = sm_ref[...]
        m_in[...] = m_in[...].T

    xs = xs_ref[...].reshape(ns * L, D_MODEL)
    z = _project(xs.astype(BF16), (w_inm_ref, w_ing_ref))

    lane = lax.broadcasted_iota(jnp.int32, (BLOCK, LANES), 1)
    lo_half = lane < HEAD_DIM
    lo3 = lax.broadcasted_iota(jnp.int32, (ns, L, LANES), 2) < HEAD_DIM
    ti = lax.broadcasted_iota(jnp.int32, (BLOCK, BLOCK), 0)
    si = lax.broadcasted_iota(jnp.int32, (BLOCK, BLOCK), 1)
    same_seq = (ti >> 3) == (si >> 3)
    tl, sl_ = ti & 7, si & 7
    causal_seq = same_seq & (sl_ <= tl)

    k_new = z[:, ATT_W:ATT_W + KV_W]
    v_new = z[:, ATT_W + KV_W:ATT_W + 2 * KV_W]
    kb, vb = k_new.astype(BF16), v_new.astype(BF16)
    zero_b = jnp.zeros_like(kb)
    kn_mask = jnp.concatenate([jnp.where(lo_half, kb, zero_b), jnp.where(lo_half, zero_b, kb)], axis=0)
    vn_mask = jnp.concatenate([jnp.where(lo_half, vb, zero_b), jnp.where(lo_half, zero_b, vb)], axis=0)
    qg = [_paired_q_group(lambda g: z[:, g * LANES:(g + 1) * LANES], r) for r in range(4)]
    s_new = _dot_nt(jnp.concatenate(qg, axis=0).astype(BF16), kn_mask)
    qg3 = [q.reshape(ns, L, LANES) for q in qg]
    q8 = jnp.concatenate([jnp.where(lo3, q, 0.0) for q in qg3] + [jnp.where(lo3, 0.0, q) for q in qg3],
                         axis=1).astype(BF16)
    s_cache = jnp.einsum('nqd,ndk->nqk', q8, ck_ref[...].astype(BF16), preferred_element_type=F32)

    c3 = z[:, COL_CONV:COL_VM].reshape(ns, L, MLSTM_W)
    cext[:, 8:16, :] = c3
    for j in range(CONV_W - 1):
        cext[:, 8 - (CONV_W - 1) + j, :] = sconv_ref[j]
        scv_ref[j] = cext[:, 16 - (CONV_W - 1) + j, :]
    c_act = _conv_silu(lambda jj: cext[:, 5 + jj:5 + jj + L, :], wconv_ref, bconv_ref)
    c_act = c_act.reshape(BLOCK, MLSTM_W)
    lblk = jnp.where(causal_seq, 1.0, 0.0).astype(BF16)
    lseq = jnp.where(same_seq, 1.0, 0.0).astype(BF16)
    xg, blast = _gate_columns(z[:, IN_MAIN:IN_PAD], gb_ref, None, lblk, extra=lseq)
    xgt = xg.T
    m_tile = m_in[pl.ds(pl.multiple_of(step * ns, ns), ns), :]
    m_rep = jnp.concatenate([jnp.broadcast_to(m_tile[g:g + 1, :], (L, LANES)) for g in range(ns)], axis=0)
    hs = range(M_HEADS)
    qk = [_dot(c_act[:, h * M_HEAD_DIM:(h + 1) * M_HEAD_DIM].astype(BF16), wqk_ref[h]) for h in hs]
    qm = [qk[h][:, :M_HEAD_DIM] for h in hs]
    qm_b = [qm[h].astype(BF16) for h in hs]
    km = [qk[h][:, M_HEAD_DIM:] * (M_HEAD_DIM ** -0.5) for h in hs]
    km_b = [km[h].astype(BF16) for h in hs]
    vh_b = [z[:, COL_VM + h * M_HEAD_DIM:COL_VM + (h + 1) * M_HEAD_DIM].astype(BF16) for h in hs]
    c_old = [sc_ref[:, h] for h in hs]
    s_qk = [_dot_nt(qm_b[h], km_b[h]) for h in hs]
    cq = [_dot_nt(qm_b[h], c_old[h].reshape(ns * 128, 128).astype(BF16)) for h in hs]

    mask_c = si > tl
    dist_c = (WINDOW + tl - si).astype(F32)
    dist_n = (tl - sl_).astype(F32)
    pc_parts, pn_rows, inv = {}, [], {}
    for r in range(4):
        pn = []
        for c in range(2):
            h = r + 4 * c
            sc_c = s_cache[:, c * 32 + r * L:c * 32 + (r + 1) * L, :].reshape(BLOCK, LANES)
            sc_n = s_new[r * LANES:(r + 1) * LANES, c * BLOCK:(c + 1) * BLOCK]
            (p_c, p_n), inv[h] = _softmax_parts(
                [(sc_c, mask_c, -ALIBI_SLOPES[h] * dist_c), (sc_n, causal_seq, -ALIBI_SLOPES[h] * dist_n)],
                sinks_ref[h])
            pc_parts[(c, r)] = p_c.reshape(ns, L, LANES)
            pn.append(p_n.astype(BF16))
        pn_rows.append(jnp.concatenate(pn, axis=1))
    p8 = jnp.concatenate([pc_parts[(c, r)] for c in range(2) for r in range(4)], axis=1).astype(BF16)
    o_cache = jnp.einsum('nqk,ndk->nqd', p8, cv_ref[...].astype(BF16), preferred_element_type=F32)
    o_new = _dot(jnp.concatenate(pn_rows, axis=0), vn_mask)
    groups = []
    for r in range(4):
        oa = o_cache[:, r * L:(r + 1) * L, :].reshape(BLOCK, LANES)
        ob = o_cache[:, 32 + r * L:32 + (r + 1) * L, :].reshape(BLOCK, LANES)
        o = jnp.where(lo_half, oa, ob) + o_new[r * LANES:(r + 1) * LANES]
        groups.append(o * jnp.where(lo_half, inv[r], inv[r + 4]))
    y_att = _attn_head_norm(jnp.concatenate(groups, axis=1), gattn_ref[...])

    sk_ref[:, 0:WINDOW - L, :] = jnp.swapaxes(ck_ref[...], 1, 2)[:, L:WINDOW, :]
    sk_ref[:, WINDOW - L:WINDOW, :] = k_new.reshape(ns, L, LANES)
    sv_ref[:, 0:WINDOW - L, :] = jnp.swapaxes(cv_ref[...], 1, 2)[:, L:WINDOW, :]
    sv_ref[:, WINDOW - L:WINDOW, :] = v_new.reshape(ns, L, LANES)

    b_c = [xg[:, 4 + h:5 + h] for h in hs]
    li_c = [xg[:, h:h + 1] for h in hs]
    b_r = [xgt[4 + h:5 + h, :] for h in hs]
    li_r = [xgt[h:h + 1, :] for h in hs]
    bl_c = [blast[:, 4 + h:5 + h] for h in hs]
    m_prev = [m_rep[:, h:h + 1] for h in hs]
    dmat = [jnp.where(causal_seq, b_c[h] - b_r[h] + li_r[h], -jnp.inf) for h in hs]
    m_inter = [b_c[h] + m_prev[h] for h in hs]
    m_t = [jnp.maximum(m_inter[h], jnp.max(dmat[h], axis=1, keepdims=True)) for h in hs]
    w_inter = [jnp.exp(m_inter[h] - m_t[h]) for h in hs]
    sc = [s_qk[h] * jnp.exp(dmat[h] - m_t[h]) for h in hs]
    num_intra = [_dot(sc[h].astype(BF16), vh_b[h]) for h in hs]

    m_end3 = [jnp.broadcast_to(m_t[h], (BLOCK, LANES)).reshape(ns, L, LANES)[:, L - 1:L, :] for h in hs]
    m_end = [jnp.broadcast_to(m_end3[h], (ns, L, LANES)).reshape(BLOCK, LANES)[:, 0:1] for h in hs]
    decay = [jnp.exp(bl_c[h] + m_prev[h] - m_end[h]) for h in hs]
    kw = [km[h] * jnp.exp(bl_c[h] - b_c[h] + li_c[h] - m_end[h]) for h in hs]
    seq_of_row = ti >> 3
    m_all = jnp.zeros((BLOCK, LANES), F32)
    for h in hs:
        kw_b = kw[h].astype(BF16)
        k_big = jnp.concatenate(
            [jnp.where(seq_of_row == g, kw_b, jnp.zeros_like(kw_b)) for g in range(ns)], axis=1)
        c_up = _dot_tn(vh_b[h], k_big)
        c_up3 = jnp.stack([c_up[:, g * M_HEAD_DIM:(g + 1) * M_HEAD_DIM] for g in range(ns)], axis=0)
        decay3 = jnp.broadcast_to(decay[h], (BLOCK, LANES)).reshape(ns, L, LANES)[:, 0:1, :]
        n_old = sn_ref[:, h:h + 1, :]
        sco_ref[:, h] = decay3 * c_old[h] + c_up3
        sno_ref[:, h:h + 1, :] = decay3 * n_old + jnp.sum(kw[h].reshape(ns, L, LANES), axis=1, keepdims=True)
        m_all = jnp.where(lane == h, m_t[h], m_all)
    m_rows[...] = m_all.reshape(ns, L, LANES)
    m_acc[pl.ds(pl.multiple_of(step * ns, ns), ns), :] = m_rows[:, L - 1, :]

    y_m = []
    for h in hs:
        num_inter = jnp.concatenate([cq[h][g * L:(g + 1) * L, g * M_HEAD_DIM:(g + 1) * M_HEAD_DIM] for g in range(ns)], axis=0)
        n_rep = jnp.broadcast_to(sn_ref[:, h:h + 1, :], (ns, L, LANES)).reshape(BLOCK, LANES)
        num = num_intra[h] + w_inter[h] * num_inter
        den = jnp.sum(sc[h], axis=1, keepdims=True) + w_inter[h] * jnp.sum(qm[h] * n_rep, axis=1, keepdims=True)
        hh = num / jnp.maximum(jnp.abs(den), jnp.exp(-m_t[h]))
        hm = _sigmoid(z[:, COL_O + h * M_HEAD_DIM:COL_O + (h + 1) * M_HEAD_DIM]) * hh
        y_m.append(_mlstm_head_norm(hm, gml_ref[:, h * M_HEAD_DIM:(h + 1) * M_HEAD_DIM]))

    ymix_ref[...] = jnp.concatenate([y_att] + y_m, axis=1).astype(BF16)

    @pl.when(step == pl.num_programs(0) - 1)
    def _write_m_state():
        smo_ref[...] = m_acc[...].T[0:M_HEADS, :]


def _const_spec(shape):
    nd = len(shape)
    return pl.BlockSpec(shape, lambda *_: (0,) * nd, pipeline_mode=pl.Buffered(1))


def _mixer_weight_specs():
    return [
        _const_spec((D_MODEL, IN_MAIN + IN_GATES)),
        _const_spec((D_MODEL, IN_PAD - IN_MAIN)),
        _const_spec((CONV_W, MLSTM_W)),
        _const_spec((1, MLSTM_W)),
        _const_spec((M_HEADS, M_HEAD_DIM, 2 * M_HEAD_DIM)),
        pl.BlockSpec(memory_space=pltpu.SMEM),
        pl.BlockSpec(memory_space=pltpu.SMEM),
        pl.BlockSpec(memory_space=pltpu.SMEM),
        _const_spec((1, ATT_W)),
        _const_spec((1, MLSTM_W)),
    ]


def _post_weight_specs(d):
    hbm = pl.BlockSpec(memory_space=pl.ANY)
    return [hbm, _const_spec((1, d)), _const_spec((1, d)), hbm, hbm, hbm, _const_spec((1, d)), _const_spec((1, d))]


def _prompt_layer(x_prompt, meta_tokens, x_sample_rows, ymix_sample, mixer_w, post_w):
    b, s, d = x_prompt.shape
    state_shapes = [((BLOCK, KV_W), BF16)] * 4 + [
        ((8, MLSTM_W), F32), ((M_HEADS, M_HEAD_DIM, 2 * M_HEAD_DIM), F32), ((8, LANES), F32)]
    rows = STEP_BLOCKS * BLOCK
    npair = s // rows
    total = b * npair
    n_pair_steps = total + 2
    n_sample_steps = x_sample_rows.shape[0] // rows
    steps = n_pair_steps + n_sample_steps

    def block_map(lag):
        def index(t):
            tb = jnp.clip(t - lag, 0, total - 1)
            return (tb // npair, tb % npair, 0)
        return index

    def smap(t):
        return (jnp.clip(t - 1, 0, total - 1) // npair, 0, 0)

    def sample_map(t):
        return (jnp.clip(t - n_pair_steps, 0, n_sample_steps - 1), 0)

    out_shape = (
        jax.ShapeDtypeStruct((b, s, d), F32),
        jax.ShapeDtypeStruct(x_sample_rows.shape, F32),
        jax.ShapeDtypeStruct((b, BLOCK, KV_W), F32),
        jax.ShapeDtypeStruct((b, BLOCK, KV_W), F32),
        jax.ShapeDtypeStruct((CONV_W - 1, b, MLSTM_W), F32),
        jax.ShapeDtypeStruct((b, M_HEADS, M_HEAD_DIM, M_HEAD_DIM), F32),
        jax.ShapeDtypeStruct((b, M_HEADS, M_HEAD_DIM), F32),
        jax.ShapeDtypeStruct((b, M_HEADS), F32),
    )
    out_specs = (
        pl.BlockSpec((None, rows, d), block_map(2)),
        pl.BlockSpec((rows, d), sample_map),
        pl.BlockSpec((None, BLOCK, KV_W), smap),
        pl.BlockSpec((None, BLOCK, KV_W), smap),
        pl.BlockSpec((CONV_W - 1, b, MLSTM_W), lambda t: (0, 0, 0)),
        pl.BlockSpec((None, M_HEADS, M_HEAD_DIM, M_HEAD_DIM), lambda t: smap(t) + (0,)),
        pl.BlockSpec((None, M_HEADS, M_HEAD_DIM), smap),
        pl.BlockSpec((b, M_HEADS), lambda t: (0, 0)),
    )
    return pl.pallas_call(
        functools.partial(_pair_kernel, npair=npair, n_pair_steps=n_pair_steps),
        grid=(steps,),
        in_specs=[pl.BlockSpec((None, rows, d), block_map(0)), pl.BlockSpec((None, rows, d), block_map(2)),
                  _const_spec((N_META, d)), pl.BlockSpec((rows, d), sample_map), pl.BlockSpec((rows, d), sample_map)]
        + _mixer_weight_specs() + _post_weight_specs(d),
        out_specs=out_specs,
        out_shape=out_shape,
        scratch_shapes=[
            pltpu.VMEM((d, d), BF16), pltpu.VMEM((d, D_FF), BF16), pltpu.VMEM((d, D_FF), BF16),
            pltpu.VMEM((D_FF, d), BF16),
            pltpu.VMEM((WEIGHT_SLOTS, WEIGHT_ROWS, D_FF), F32),
            pltpu.SemaphoreType.DMA((2 * WEIGHT_SLOTS + 2 * STEP_BLOCKS,)),
        ] + [pltpu.VMEM(shape, dt) for shape, dt in state_shapes] + [
            pltpu.VMEM((rows, IN_PAD), F32),
            pltpu.VMEM((rows, IN_PAD), F32),
            pltpu.VMEM((BLOCK, 2 * KV_W), F32),
        ] + [pltpu.VMEM((BLOCK, KV_W), BF16)] * 4 + [
            pltpu.VMEM((rows + 8, MLSTM_W), F32),
            pltpu.VMEM((M_HEADS, M_HEAD_DIM, 2 * M_HEAD_DIM), F32),
            pltpu.VMEM((8, LANES), F32),
            pltpu.VMEM((rows, d), BF16),
        ],
        compiler_params=pltpu.CompilerParams(
            dimension_semantics=("arbitrary",), vmem_limit_bytes=VMEM_LIMIT),
        name="prompt_layer",
    )(x_prompt, x_prompt, meta_tokens, x_sample_rows, ymix_sample, *mixer_w, *post_w)


def _sample_mixer(x_sample, ck, cv, s_conv, s_c, s_n, s_m, mixer_w):
    n, l, d = x_sample.shape
    assert n == LANES, "the m-state transposes assume one lane per sequence"
    t = SEQ_TILE
    m3 = lambda i: (i, 0, 0)
    m4 = lambda i: (i, 0, 0, 0)
    out_shape = (
        jax.ShapeDtypeStruct((n * l, d), BF16),
        jax.ShapeDtypeStruct((n, WINDOW, KV_W), F32),
        jax.ShapeDtypeStruct((n, WINDOW, KV_W), F32),
        jax.ShapeDtypeStruct((CONV_W - 1, n, MLSTM_W), F32),
        jax.ShapeDtypeStruct((n, M_HEADS, M_HEAD_DIM, M_HEAD_DIM), F32),
        jax.ShapeDtypeStruct((n, M_HEADS, M_HEAD_DIM), F32),
        jax.ShapeDtypeStruct((M_HEADS, n), F32),
    )
    m_state_spec = pl.BlockSpec((M_HEADS, n), lambda i: (0, 0))
    out_specs = (
        pl.BlockSpec((t * l, d), lambda i: (i, 0)),
        pl.BlockSpec((t, WINDOW, KV_W), m3),
        pl.BlockSpec((t, WINDOW, KV_W), m3),
        pl.BlockSpec((CONV_W - 1, t, MLSTM_W), lambda i: (0, i, 0)),
        pl.BlockSpec((t, M_HEADS, M_HEAD_DIM, M_HEAD_DIM), m4),
        pl.BlockSpec((t, M_HEADS, M_HEAD_DIM), m3),
        m_state_spec,
    )
    in_specs = [
        pl.BlockSpec((t, l, d), m3),
        pl.BlockSpec((t, WINDOW, KV_W), m3),
        pl.BlockSpec((t, WINDOW, KV_W), m3),
        pl.BlockSpec((CONV_W - 1, t, MLSTM_W), lambda i: (0, i, 0)),
        pl.BlockSpec((t, M_HEADS, M_HEAD_DIM, M_HEAD_DIM), m4),
        pl.BlockSpec((t, M_HEADS, M_HEAD_DIM), m3),
        m_state_spec,
    ] + _mixer_weight_specs()
    return pl.pallas_call(
        _sample_kernel,
        grid=(n // t,),
        in_specs=in_specs,
        out_specs=out_specs,
        out_shape=out_shape,
        scratch_shapes=[pltpu.VMEM((t, 16, MLSTM_W), F32),
                        pltpu.VMEM((LANES, LANES), F32),
                        pltpu.VMEM((LANES, LANES), F32),
                        pltpu.VMEM((t, l, LANES), F32)],
        compiler_params=pltpu.CompilerParams(
            dimension_semantics=("arbitrary",), vmem_limit_bytes=VMEM_LIMIT),
        name="sample_mixer",
    )(x_sample, ck, cv, s_conv, s_c, s_n, s_m, *mixer_w)


def kernel(x_prompt, x_sample, cache_k, cache_v, state_conv, state_C, state_n, state_m, meta_tokens,
           w_in, w_conv, b_conv, w_mq, w_mk, b_i, b_f, attn_sinks, g_attn, g_mlstm, w_out,
           ln1_g, ln1_b, w_gate, w_up, w_down, ln2_g, ln2_b):
    b, s, d = x_prompt.shape
    n, l, _ = x_sample.shape

    w0 = w_in[0].astype(BF16)
    w_in_parts = (w0, jnp.pad(w0[:, IN_MAIN:], ((0, 0), (0, IN_PAD - IN_MAIN - IN_GATES))))
    wqk = jnp.concatenate([w_mq[0], w_mk[0]], axis=-1).astype(BF16)
    mixer_w = w_in_parts + (w_conv[0], b_conv[0].reshape(1, MLSTM_W), wqk, b_i[0], b_f[0], attn_sinks[0],
                            g_attn[0].reshape(1, ATT_W), g_mlstm[0].reshape(1, MLSTM_W))
    post_w = (w_out[0], ln1_g[0].reshape(1, d), ln1_b[0].reshape(1, d), w_gate[0], w_up[0], w_down[0],
              ln2_g[0].reshape(1, d), ln2_b[0].reshape(1, d))

    key_minor = lambda c: c[0].transpose(0, 2, 3, 1).reshape(n, KV_W, WINDOW)
    ymix_s, sk, sv, scv, s_c, s_n, s_mo = _sample_mixer(
        x_sample, key_minor(cache_k), key_minor(cache_v), state_conv[0].swapaxes(0, 1), state_C[0], state_n[0],
        state_m[0].T, mixer_w)

    y_prompt, y_sample, pk, pv, pconv, p_c, p_n, p_m = _prompt_layer(
        x_prompt, meta_tokens, x_sample.reshape(n * l, d), ymix_s, mixer_w, post_w)

    kv5 = lambda a: a.reshape(1, a.shape[0], WINDOW, 2, HEAD_DIM)
    kv5_key_minor = lambda a: a.reshape(a.shape[0], 2, HEAD_DIM, WINDOW).transpose(0, 3, 1, 2)[None]
    return (y_prompt, y_sample.reshape(n, l, d),
            kv5_key_minor(pk), kv5_key_minor(pv), pconv.swapaxes(0, 1)[None], p_c[None], p_n[None], p_m[None],
            kv5(sk), kv5(sv), scv.swapaxes(0, 1)[None], s_c[None], s_n[None], s_mo.T[None])
```

```python
import functools

import jax
import jax.numpy as jnp
from jax import lax
from jax.experimental import pallas as pl
from jax.experimental.pallas import tpu as pltpu

F32 = jnp.float32
BF16 = jnp.bfloat16

D_MODEL = 1024
ATT_W = 512
MLSTM_W = 512
HEAD_DIM = 64
N_HEADS = 8
KV_W = 128
WINDOW = 128
BLOCK = 128
M_HEADS = 4
M_HEAD_DIM = 128
CONV_W = 4
N_META = 16
META_PAD = BLOCK - N_META
D_FF = 2816
LANES = 128
COL_CONV = ATT_W + 2 * KV_W
COL_VM = COL_CONV + MLSTM_W
COL_O = COL_VM + MLSTM_W
IN_MAIN = COL_O + MLSTM_W
IN_GATES = 2 * M_HEADS
IN_PAD = IN_MAIN + LANES
DEPTH = 1
ALIBI_SLOPES = tuple(2.0 ** (-8.0 * (h + 1) / N_HEADS) for h in range(N_HEADS))
DEEPNORM_ALPHA = (2.0 * DEPTH) ** 0.25
EPS = 1e-5
SEQ_TILE = 16
STEP_BLOCKS = 2
FF_CHUNK = 256
FF_LOOKAHEAD = 3
PROJ_CHUNK = 256
WEIGHT_ROWS = 128
WEIGHT_SLOTS = 4
VMEM_LIMIT = 56 * 1024 * 1024


def _paired_q_group(q_group, r):
    a, b = q_group(r // 2), q_group(r // 2 + N_HEADS // 4)
    lo_half = lax.broadcasted_iota(jnp.int32, a.shape, 1) < HEAD_DIM
    if r % 2 == 0:
        return jnp.where(lo_half, a, pltpu.roll(b, HEAD_DIM, 1))
    return jnp.where(lo_half, pltpu.roll(a, HEAD_DIM, 1), b)


def _dot(a, b):
    return jnp.dot(a, b, preferred_element_type=F32)


def _dot_nt(a, b):
    return lax.dot_general(a, b, (((1,), (1,)), ((), ())), preferred_element_type=F32)


def _dot_tn(a, b):
    return lax.dot_general(a, b, (((0,), (0,)), ((), ())), preferred_element_type=F32)


def _split2(x):
    hi = x.astype(BF16)
    lo = (x - hi.astype(F32)).astype(BF16)
    return hi, lo


def _sigmoid(x):
    return 1.0 / (1.0 + jnp.exp(-x))


def _log_sigmoid(x):
    return -(jnp.maximum(-x, 0.0) + jnp.log1p(jnp.exp(-jnp.abs(x))))


def _layer_norm(x, g, b):
    mu = jnp.mean(x, axis=-1, keepdims=True)
    xc = x - mu
    var = jnp.mean(xc * xc, axis=-1, keepdims=True)
    return xc * lax.rsqrt(var + EPS) * g + b


def _attn_head_norm(att, g):
    lo_half = lax.broadcasted_iota(jnp.int32, (att.shape[0], LANES), 1) < HEAD_DIM
    g8 = jnp.broadcast_to(g, (8, ATT_W))
    g_group = lambda i: g8[:, i * LANES:(i + 1) * LANES]

    def seg_mean(x):
        s_all = jnp.sum(x, axis=1, keepdims=True)
        s_lo = jnp.sum(jnp.where(lo_half, x, 0.0), axis=1, keepdims=True)
        return jnp.where(lo_half, s_lo, s_all - s_lo) * (1.0 / HEAD_DIM)

    out = []
    for grp in range(att.shape[1] // LANES):
        x = att[:, grp * LANES:(grp + 1) * LANES]
        xc = x - seg_mean(x)
        out.append(xc * lax.rsqrt(seg_mean(xc * xc) + EPS) * _paired_q_group(g_group, grp)[0:1, :])
    return jnp.concatenate(out, axis=1)


def _mlstm_head_norm(hm, g):
    mu = jnp.mean(hm, axis=-1, keepdims=True)
    xc = hm - mu
    var = jnp.mean(xc * xc, axis=-1, keepdims=True)
    return xc * lax.rsqrt(var + EPS) * g


def _softmax_parts(parts, sink):
    sp = [jnp.where(m, s * (HEAD_DIM ** -0.5) + a, -jnp.inf) for s, m, a in parts]
    mx = sink
    for s in sp:
        mx = jnp.maximum(mx, jnp.max(s, axis=1, keepdims=True))
    ps = [jnp.exp(s - mx) for s in sp]
    den = jnp.exp(sink - mx)
    for p in ps:
        den = den + jnp.sum(p, axis=1, keepdims=True)
    return ps, 1.0 / den


def _conv_silu(window, wconv_ref, bconv_ref):
    acc = bconv_ref[...]
    for j in range(CONV_W):
        acc = acc + window(j) * wconv_ref[j:j + 1, :]
    return acc * _sigmoid(acc)


def _gate_columns(gates, gb_ref, row_valid, ltri, extra=None):
    lane = lax.broadcasted_iota(jnp.int32, gates.shape, 1)
    bi_ref, bf_ref = gb_ref
    lane1 = lax.broadcasted_iota(jnp.int32, (1, LANES), 1)
    bias = jnp.zeros((1, LANES), F32)
    for h in range(M_HEADS):
        bias = jnp.where(lane1 == h, bi_ref[h], jnp.where(lane1 == M_HEADS + h, bf_ref[h], bias))
    gb = gates + bias
    is_i = lane < M_HEADS
    is_f = (lane >= M_HEADS) & (lane < 2 * M_HEADS)
    logf = _log_sigmoid(gb)
    if row_valid is not None:
        lf = jnp.where(is_f & row_valid, logf, 0.0)
        li = jnp.where(row_valid, gb, -jnp.inf)
    else:
        lf = jnp.where(is_f, logf, 0.0)
        li = gb
    hi_lo = jnp.concatenate(_split2(lf), axis=1)
    bc = _dot(ltri, hi_lo)
    x = jnp.where(is_i, li, bc[:, :LANES] + bc[:, LANES:])
    if extra is None:
        return x, None
    be = _dot(extra, hi_lo)
    return x, be[:, :LANES] + be[:, LANES:]


class _FfnStages:
    N_CHUNKS = D_FF // FF_CHUNK

    def __init__(self, x, ymix, wout_ref, ln1g_ref, ln1b_ref, wg_ref, wu_ref, wd_ref, ln2g_ref, ln2b_ref):
        self.x, self.ymix = x, ymix
        self.wout_ref, self.ln1g_ref, self.ln1b_ref = wout_ref, ln1g_ref, ln1b_ref
        self.wg_ref, self.wu_ref, self.wd_ref = wg_ref, wu_ref, wd_ref
        self.ln2g_ref, self.ln2b_ref = ln2g_ref, ln2b_ref

    def head(self):
        self.x1 = _layer_norm(DEEPNORM_ALPHA * self.x + _dot(self.ymix, self.wout_ref[...]),
                              self.ln1g_ref[...], self.ln1b_ref[...])
        self.x1b = self.x1.astype(BF16)
        self.acc = DEEPNORM_ALPHA * self.x1

    def _gate_up(self, c):
        cs = slice(c * FF_CHUNK, (c + 1) * FF_CHUNK)
        return _dot(self.x1b, self.wg_ref[:, cs]), _dot(self.x1b, self.wu_ref[:, cs])

    def chunk(self, c):
        if c == 0:
            self.gu = {}
        for n in range(c, min(c + FF_LOOKAHEAD, self.N_CHUNKS - 1) + 1):
            if n not in self.gu:
                self.gu[n] = self._gate_up(n)
        g, u = self.gu.pop(c)
        hid = (g * _sigmoid(g) * u).astype(BF16)
        self.acc = self.acc + _dot(hid, self.wd_ref[c * FF_CHUNK:(c + 1) * FF_CHUNK, :])

    def tail(self):
        return _layer_norm(self.acc, self.ln2g_ref[...], self.ln2b_ref[...])

    def run(self):
        self.head()
        for c in range(self.N_CHUNKS):
            self.chunk(c)
        return self.tail()


def _mix_blocks(zc, nb, pad_rows, thr_first, state, cbuf, wconv_ref, bconv_ref, wqk_ref, gb_ref,
                sinks_ref, gattn_ref, gml_ref, hook):
    rows = nb * BLOCK
    kp_lo, kp_hi, vp_lo, vp_hi, c_state, m_state = state
    c_state, m_state = list(c_state), list(m_state)
    lane = lax.broadcasted_iota(jnp.int32, (BLOCK, LANES), 1)
    lo_half = lane < HEAD_DIM
    row_valid = None
    if pad_rows:
        row_valid = lax.broadcasted_iota(jnp.int32, (rows, 1), 0) >= pad_rows

    s_all, vmasks = [], []
    for blk in range(nb):
        rs = slice(blk * BLOCK, (blk + 1) * BLOCK)
        kb = zc[rs, ATT_W:ATT_W + KV_W].astype(BF16)
        vb = zc[rs, ATT_W + KV_W:ATT_W + 2 * KV_W].astype(BF16)
        zero_b = jnp.zeros_like(kb)
        k_lo, k_hi = jnp.where(lo_half, kb, zero_b), jnp.where(lo_half, zero_b, kb)
        v_lo, v_hi = jnp.where(lo_half, vb, zero_b), jnp.where(lo_half, zero_b, vb)
        kmask = jnp.concatenate([kp_lo, k_lo, kp_hi, k_hi], axis=0)
        vmasks.append(jnp.concatenate([vp_lo, v_lo, vp_hi, v_hi], axis=0))
        q_group = lambda g, rs=rs: zc[rs, g * LANES:(g + 1) * LANES]
        q4 = jnp.concatenate([_paired_q_group(q_group, r) for r in range(4)], axis=0).astype(BF16)
        s_all.append(_dot_nt(q4, kmask))
        kp_lo, kp_hi, vp_lo, vp_hi = k_lo, k_hi, v_lo, v_hi
    hook("scores")

    c_in = zc[:, COL_CONV:COL_VM]
    if row_valid is not None:
        c_in = jnp.where(row_valid, c_in, 0.0)
    cbuf[8:8 + rows, :] = c_in
    c_act = _conv_silu(lambda jj: cbuf[5 + jj:5 + jj + rows, :], wconv_ref, bconv_ref)
    cbuf[0:8, :] = cbuf[rows:rows + 8, :]
    hs = range(M_HEADS)
    ca_b = [c_act[:, h * M_HEAD_DIM:(h + 1) * M_HEAD_DIM].astype(BF16) for h in hs]
    qk = [_dot(ca_b[h], wqk_ref[h]) for h in hs]
    qm_b = [qk[h][:, :M_HEAD_DIM].astype(BF16) for h in hs]
    kmt_f = [(qk[h][:, M_HEAD_DIM:] * (M_HEAD_DIM ** -0.5)).T for h in hs]
    hook("front")

    ti = lax.broadcasted_iota(jnp.int32, (rows, rows), 0)
    si = lax.broadcasted_iota(jnp.int32, (rows, rows), 1)
    ltri = jnp.where((si <= ti) & ((si >> 7) == (ti >> 7)), 1.0, 0.0).astype(BF16)
    xg, _ = _gate_columns(zc[:, IN_MAIN:IN_PAD], gb_ref, row_valid, ltri)
    xgt = xg.T
    hook("gates")

    qi = lax.broadcasted_iota(jnp.int32, (BLOCK, BLOCK), 0)
    ks = lax.broadcasted_iota(jnp.int32, (BLOCK, BLOCK), 1)
    own = ks <= qi
    distf = jnp.where(own, qi - ks, WINDOW + qi - ks).astype(F32)
    zero_p = jnp.zeros((BLOCK, BLOCK), BF16)
    att_rows = []
    for blk in range(nb):
        live = None
        if blk == 0 and thr_first is not None:
            live = jnp.where(own, ks + BLOCK, ks) >= thr_first
        p_rows, inv = [], {}
        for r in range(4):
            pr = []
            for c in range(2):
                h = r + 4 * c
                s_blk = s_all[blk][r * LANES:(r + 1) * LANES, c * 256:(c + 1) * 256]
                sp = jnp.where(own, s_blk[:, BLOCK:], s_blk[:, :BLOCK]) * (HEAD_DIM ** -0.5) \
                    - ALIBI_SLOPES[h] * distf
                if live is not None:
                    sp = jnp.where(live, sp, -jnp.inf)
                sink = sinks_ref[h]
                mx = jnp.maximum(jnp.max(sp, axis=1, keepdims=True), sink)
                p = jnp.exp(sp - mx)
                inv[h] = 1.0 / (jnp.sum(p, axis=1, keepdims=True) + jnp.exp(sink - mx))
                p_b = p.astype(BF16)
                pr += [jnp.where(own, zero_p, p_b), jnp.where(own, p_b, zero_p)]
            p_rows.append(jnp.concatenate(pr, axis=1))
            hook("softmax", blk * 4 + r)
        o_all = _dot(jnp.concatenate(p_rows, axis=0), vmasks[blk])
        att_rows.append(jnp.concatenate(
            [o_all[r * LANES:(r + 1) * LANES] * jnp.where(lo_half, inv[r], inv[r + 4]) for r in range(4)], axis=1))
        hook("pv", blk)

    ones_col = jnp.where(lane == 0, 1.0, 0.0).astype(BF16)
    tb = lax.broadcasted_iota(jnp.int32, (BLOCK, BLOCK), 0)
    sb = lax.broadcasted_iota(jnp.int32, (BLOCK, BLOCK), 1)
    tri = sb <= tb
    hh_rows = [[] for _ in hs]
    for blk in range(nb):
        rs = slice(blk * BLOCK, (blk + 1) * BLOCK)
        b_c = [xg[rs, 4 + h:5 + h] for h in hs]
        b_r = [xgt[4 + h:5 + h, rs] for h in hs]
        li_r = [xgt[h:h + 1, rs] for h in hs]
        kmt = [kmt_f[h][:, rs] for h in hs]
        v_ext = [jnp.concatenate([zc[rs, COL_VM + h * M_HEAD_DIM:COL_VM + (h + 1) * M_HEAD_DIM].astype(BF16), ones_col], axis=1)
                 for h in hs]
        s_qk = [_dot(qm_b[h][rs], kmt[h].astype(BF16)) for h in hs]
        inter = [_dot(qm_b[h][rs], c_state[h].astype(BF16)) for h in hs]
        hook("mlstm_a", blk)
        dmat = [jnp.where(tri, b_c[h] - b_r[h] + li_r[h], -jnp.inf) for h in hs]
        m_inter = [b_c[h] + m_state[h] for h in hs]
        m_t = [jnp.maximum(m_inter[h], jnp.max(dmat[h], axis=1, keepdims=True)) for h in hs]
        w_inter = [jnp.exp(m_inter[h] - m_t[h]) for h in hs]
        sc_b = [(s_qk[h] * jnp.exp(dmat[h] - m_t[h])).astype(BF16) for h in hs]
        hook("mlstm_b", blk)
        nd = [_dot(sc_b[h], v_ext[h]) + w_inter[h] * inter[h] for h in hs]
        for h in hs:
            m_end = m_t[h][BLOCK - 1:BLOCK, :]
            b_last = b_c[h][BLOCK - 1:BLOCK, :]
            decay = jnp.exp(b_last + m_state[h] - m_end)
            wk_r = jnp.exp(b_last - b_r[h] + li_r[h] - m_end)
            c_state[h] = decay * c_state[h] + _dot((kmt[h] * wk_r).astype(BF16), v_ext[h])
            m_state[h] = m_end
            num, den = nd[h][:, :M_HEAD_DIM], nd[h][:, M_HEAD_DIM:M_HEAD_DIM + 1]
            hh_rows[h].append(num / jnp.maximum(jnp.abs(den), jnp.exp(-m_t[h])))
        hook("mlstm_c", blk)

    y_att = _attn_head_norm(jnp.concatenate(att_rows, axis=0), gattn_ref[...])
    hook("attnorm")
    y_m = []
    for h in hs:
        hm = _sigmoid(zc[:, COL_O + h * M_HEAD_DIM:COL_O + (h + 1) * M_HEAD_DIM]) * jnp.concatenate(hh_rows[h], axis=0)
        y_m.append(_mlstm_head_norm(hm, gml_ref[:, h * M_HEAD_DIM:(h + 1) * M_HEAD_DIM]))
        hook("headnorm", h)
    ymix = jnp.concatenate([y_att] + y_m, axis=1).astype(BF16)
    return ymix, (kp_lo, kp_hi, vp_lo, vp_hi, c_state, m_state)


def _project(xb, w_in_ref, lo=0, hi=IN_PAD):
    parts = []
    if lo < IN_MAIN:
        parts.append(_dot(xb, w_in_ref[:, lo:min(hi, IN_MAIN)]))
    if hi > IN_MAIN:
        assert lo <= IN_MAIN and hi == IN_PAD
        gates = _dot(xb, w_in_ref[:, IN_MAIN:IN_MAIN + IN_GATES])
        parts.append(jnp.concatenate(
            [gates, jnp.zeros((xb.shape[0], IN_PAD - IN_MAIN - IN_GATES), F32)], axis=1))
    return parts[0] if len(parts) == 1 else jnp.concatenate(parts, axis=1)


def _meta_state(meta_ref, w_inm_ref, wconv_ref, bconv_ref, wqk_ref, gb_ref,
                sinks_ref, gattn_ref,
                gml_ref, kplo_o, kphi_o, vplo_o, vphi_o, cb_o, ctn_o, mst_o, zc, cbuf):
    xb = jnp.concatenate([jnp.zeros((META_PAD, D_MODEL), F32), meta_ref[...]], axis=0)
    zc[...] = _project(xb.astype(BF16), w_inm_ref)
    cbuf[0:8, :] = jnp.zeros((8, MLSTM_W), F32)
    zb = jnp.zeros((BLOCK, KV_W), BF16)
    state = (zb, zb, zb, zb, [jnp.zeros((M_HEAD_DIM, 2 * M_HEAD_DIM), F32)] * M_HEADS,
             [jnp.zeros((1, 1), F32)] * M_HEADS)
    _, (kp_lo, kp_hi, vp_lo, vp_hi, c_state, m_state) = _mix_blocks(
        zc, 1, META_PAD, BLOCK + META_PAD, state, cbuf, wconv_ref, bconv_ref, wqk_ref, gb_ref,
        sinks_ref, gattn_ref, gml_ref, lambda name, index=0: None)
    kplo_o[...] = kp_lo
    kphi_o[...] = kp_hi
    vplo_o[...] = vp_lo
    vphi_o[...] = vp_hi
    cb_o[...] = cbuf[0:8, :]
    for h in range(M_HEADS):
        ctn_o[h] = c_state[h]
        mst_o[h:h + 1, :] = jnp.broadcast_to(m_state[h], (1, LANES))
    mst_o[M_HEADS:8, :] = jnp.zeros((8 - M_HEADS, LANES), F32)


def _load_ffn_weights(wout_hbm, wg_hbm, wu_hbm, wd_hbm, wout_v, wg_v, wu_v, wd_v, stage_bufs, sems):
    half = D_FF // 2
    jobs = []
    for src, dst in ((wg_hbm, wg_v), (wu_hbm, wu_v)):
        jobs += [(src, r, WEIGHT_ROWS, c, half, dst, r) for r in range(0, D_MODEL, WEIGHT_ROWS) for c in (0, half)]
    jobs += [(wd_hbm, r, WEIGHT_ROWS, 0, D_MODEL, wd_v, r) for r in range(0, D_FF, WEIGHT_ROWS)]
    for h in range(N_HEADS):
        jobs.append((wout_hbm, h * HEAD_DIM, HEAD_DIM, 0, D_MODEL, wout_v, ((h % 4) * 2 + h // 4) * HEAD_DIM))
    jobs += [(wout_hbm, r, WEIGHT_ROWS, 0, D_MODEL, wout_v, r) for r in range(ATT_W, D_MODEL, WEIGHT_ROWS)]

    windows = []
    for buf in stage_bufs:
        views = [buf.at[s] for s in range(buf.shape[0])] if len(buf.shape) == 3 else [buf]
        for v in views:
            windows += [(v, r, c) for r in range(0, v.shape[0] - WEIGHT_ROWS + 1, WEIGHT_ROWS)
                        for c in range(0, v.shape[1] - half + 1, half)]
    nslot = len(windows)
    assert nslot <= sems.shape[0]

    def window(i, nrows, ncols):
        v, r, c = windows[i % nslot]
        return v.at[pl.ds(r, nrows), pl.ds(c, ncols)]

    def copy(i):
        src, r0, nrows, c0, ncols, _, _ = jobs[i]
        return pltpu.make_async_copy(src.at[pl.ds(r0, nrows), pl.ds(c0, ncols)], window(i, nrows, ncols),
                                     sems.at[i % nslot])

    for i in range(min(nslot - 1, len(jobs))):
        copy(i).start()
    for i, (_, _, nrows, c0, ncols, dst, d0) in enumerate(jobs):
        if i + nslot - 1 < len(jobs):
            copy(i + nslot - 1).start()
        copy(i).wait()
        dst[d0:d0 + nrows, c0:c0 + ncols] = window(i, nrows, ncols)[...].astype(BF16)


def _pair_kernel(xin_ref, xres_ref, meta_ref, xs_ref, ymix_s_ref,
                 w_inm_ref, wconv_ref, bconv_ref, wqk_ref, bi_ref, bf_ref, sinks_ref,
                 gattn_ref, gml_ref,
                 wout_hbm, ln1g_ref, ln1b_ref, wg_hbm, wu_hbm, wd_hbm, ln2g_ref, ln2b_ref,
                 y_ref, ys_ref, pk_ref, pv_ref, pconv_ref, pc_ref, pn_ref, pm_ref,
                 wout_v, wg_v, wu_v, wd_v, wstage, wsems,
                 kplo_i, kphi_i, vplo_i, vphi_i, cb_i, ctn_i, mst_i,
                 zcur, znext, kvlast, kplo, kphi, vplo, vphi, cbuf, ctn, mst, yprev, *, npair, n_pair_steps):
    t = pl.program_id(0)
    p = lax.rem(t + (npair - 1), npair)
    gb_ref = (bi_ref, bf_ref)

    @pl.when(t == 0)
    def _init_pipeline():
        _load_ffn_weights(wout_hbm, wg_hbm, wu_hbm, wd_hbm, wout_v, wg_v, wu_v, wd_v,
                          (wstage, zcur, znext), wsems)
        zcur[...] = jnp.zeros_like(zcur)
        yprev[...] = jnp.zeros_like(yprev)
        _meta_state(meta_ref, w_inm_ref, wconv_ref, bconv_ref, wqk_ref, gb_ref,
                    sinks_ref, gattn_ref, gml_ref, kplo_i, kphi_i, vplo_i, vphi_i, cb_i, ctn_i, mst_i,
                    znext.at[pl.ds(0, BLOCK)], cbuf)

    @pl.when((p == 0) | (t == 0))
    def _load_meta_state():
        kplo[...] = kplo_i[...]
        kphi[...] = kphi_i[...]
        vplo[...] = vplo_i[...]
        vphi[...] = vphi_i[...]
        cbuf[0:8, :] = cb_i[...]
        ctn[...] = ctn_i[...]
        mst[...] = mst_i[...]

    ffn_w = (wout_v, ln1g_ref, ln1b_ref, wg_v, wu_v, wd_v, ln2g_ref, ln2b_ref)

    @pl.when(t < n_pair_steps)
    def _pair_step():
        ffn = _FfnStages(xres_ref[...], yprev[...], *ffn_w)
        xb_b = xin_ref[...].astype(BF16)

        n_proj = -(-IN_PAD // PROJ_CHUNK)

        def proj(n):
            lo, hi = n * PROJ_CHUNK, min((n + 1) * PROJ_CHUNK, IN_PAD)
            znext[:, lo:hi] = _project(xb_b, w_inm_ref, lo, hi)

        proj_it, ffn_it = iter(range(n_proj)), iter(range(ffn.N_CHUNKS))

        def fill_proj(n):
            for c in [c for _, c in zip(range(n), proj_it)]:
                proj(c)

        def fill_ffn(n):
            for c in [c for _, c in zip(range(n), ffn_it)]:
                ffn.chunk(c)

        out = {}

        def hook(name, index=0):
            if name == "scores":
                fill_proj(2)
                ffn.head()
            elif name == "front":
                fill_proj(1)
            elif name == "gates":
                fill_ffn(1)
            elif name == "softmax":
                if index % 2 == 1:
                    fill_ffn(1)
            elif name in ("pv", "mlstm_a", "mlstm_b"):
                fill_ffn(1)
            elif name == "mlstm_c":
                fill_proj(1)
            elif name == "attnorm":
                fill_ffn(ffn.N_CHUNKS)
                out["y"] = ffn.tail()
                fill_proj(1)
            elif name == "headnorm":
                fill_proj(1)

        state = (kplo[...], kphi[...], vplo[...], vphi[...], [ctn[h] for h in range(M_HEADS)],
                 [mst[h:h + 1, 0:1] for h in range(M_HEADS)])
        thr_first = jnp.where(p == 0, META_PAD, 0)
        kvlast[...] = zcur[(STEP_BLOCKS - 1) * BLOCK:STEP_BLOCKS * BLOCK, ATT_W:ATT_W + 2 * KV_W]
        ymix, (kp_lo, kp_hi, vp_lo, vp_hi, c_state, m_state) = _mix_blocks(
            zcur, STEP_BLOCKS, 0, thr_first, state, cbuf, wconv_ref, bconv_ref, wqk_ref, gb_ref,
            sinks_ref, gattn_ref, gml_ref, hook)
        fill_proj(n_proj)
        y_ref[...] = out["y"]
        kplo[...] = kp_lo
        kphi[...] = kp_hi
        vplo[...] = vp_lo
        vphi[...] = vp_hi
        for h in range(M_HEADS):
            ctn[h] = c_state[h]
            mst[h:h + 1, :] = jnp.broadcast_to(m_state[h], (1, LANES))
        yprev[...] = ymix
        zcur[...] = znext[...]

    @pl.when(t >= n_pair_steps)
    def _sample_ffn_step():
        ys_ref[...] = _FfnStages(xs_ref[...], ymix_s_ref[...], *ffn_w).run()

    @pl.when((p == npair - 1) & (t > 0) & (t < n_pair_steps))
    def _final():
        pk_ref[...] = kvlast[:, 0:KV_W].T
        pv_ref[...] = kvlast[:, KV_W:2 * KV_W].T
        seq = lax.div(t - 1, npair)
        for j in range(CONV_W - 1):
            pconv_ref[j, pl.ds(seq, 1), :] = cbuf[8 - (CONV_W - 1) + j:8 - (CONV_W - 1) + j + 1, :]
        head_lane = lax.broadcasted_iota(jnp.int32, (1, LANES), 1)
        m_row = jnp.zeros((1, LANES), F32)
        for h in range(M_HEADS):
            m_row = jnp.where(head_lane == h, mst[h:h + 1, :], m_row)
        pm_ref[pl.ds(seq, 1), :] = m_row[:, 0:M_HEADS]
        for h in range(M_HEADS):
            c_n = ctn[h].T
            pc_ref[h] = c_n[0:M_HEAD_DIM, :]
            pn_ref[h:h + 1, :] = c_n[M_HEAD_DIM:M_HEAD_DIM + 1, :]


def _sample_kernel(xs_ref, ck_ref, cv_ref, sconv_ref, sc_ref, sn_ref, sm_ref,
                   w_inm_ref, wconv_ref, bconv_ref, wqk_ref, bi_ref, bf_ref, sinks_ref,
                   gattn_ref, gml_ref,
                   ymix_ref, sk_ref, sv_ref, scv_ref, sco_ref, sno_ref, smo_ref, cext, m_in, m_acc, m_rows):
    ns, L = SEQ_TILE, 8
    gb_ref = (bi_ref, bf_ref)
    step = pl.program_id(0)

    @pl.when(step == 0)
    def _transpose_m_state():
        m_in[...] = jnp.zeros_like(m_in)
        m_in[0:M_HEADS, :] = sm_ref[...]
        m_in[...] = m_in[...].T

    xs = xs_ref[...].reshape(ns * L, D_MODEL)
    z = _project(xs.astype(BF16), w_inm_ref)

    lane = lax.broadcasted_iota(jnp.int32, (BLOCK, LANES), 1)
    lo_half = lane < HEAD_DIM
    lo3 = lax.broadcasted_iota(jnp.int32, (ns, L, LANES), 2) < HEAD_DIM
    ti = lax.broadcasted_iota(jnp.int32, (BLOCK, BLOCK), 0)
    si = lax.broadcasted_iota(jnp.int32, (BLOCK, BLOCK), 1)
    same_seq = (ti >> 3) == (si >> 3)
    tl, sl_ = ti & 7, si & 7
    causal_seq = same_seq & (sl_ <= tl)

    k_new = z[:, ATT_W:ATT_W + KV_W]
    v_new = z[:, ATT_W + KV_W:ATT_W + 2 * KV_W]
    kb, vb = k_new.astype(BF16), v_new.astype(BF16)
    zero_b = jnp.zeros_like(kb)
    kn_mask = jnp.concatenate([jnp.where(lo_half, kb, zero_b), jnp.where(lo_half, zero_b, kb)], axis=0)
    vn_mask = jnp.concatenate([jnp.where(lo_half, vb, zero_b), jnp.where(lo_half, zero_b, vb)], axis=0)
    qg = [_paired_q_group(lambda g: z[:, g * LANES:(g + 1) * LANES], r) for r in range(4)]
    s_new = _dot_nt(jnp.concatenate(qg, axis=0).astype(BF16), kn_mask)
    qg3 = [q.reshape(ns, L, LANES) for q in qg]
    q8 = jnp.concatenate([jnp.where(lo3, q, 0.0) for q in qg3] + [jnp.where(lo3, 0.0, q) for q in qg3],
                         axis=1).astype(BF16)
    s_cache = jnp.einsum('nqd,ndk->nqk', q8, ck_ref[...].astype(BF16), preferred_element_type=F32)

    c3 = z[:, COL_CONV:COL_VM].reshape(ns, L, MLSTM_W)
    cext[:, 8:16, :] = c3
    for j in range(CONV_W - 1):
        cext[:, 8 - (CONV_W - 1) + j, :] = sconv_ref[j]
        scv_ref[j] = cext[:, 16 - (CONV_W - 1) + j, :]
    c_act = _conv_silu(lambda jj: cext[:, 5 + jj:5 + jj + L, :], wconv_ref, bconv_ref)
    c_act = c_act.reshape(BLOCK, MLSTM_W)
    lblk = jnp.where(causal_seq, 1.0, 0.0).astype(BF16)
    lseq = jnp.where(same_seq, 1.0, 0.0).astype(BF16)
    xg, blast = _gate_columns(z[:, IN_MAIN:IN_PAD], gb_ref, None, lblk, extra=lseq)
    xgt = xg.T
    m_tile = m_in[pl.ds(pl.multiple_of(step * ns, ns), ns), :]
    m_rep = jnp.concatenate([jnp.broadcast_to(m_tile[g:g + 1, :], (L, LANES)) for g in range(ns)], axis=0)
    hs = range(M_HEADS)
    qk = [_dot(c_act[:, h * M_HEAD_DIM:(h + 1) * M_HEAD_DIM].astype(BF16), wqk_ref[h]) for h in hs]
    qm = [qk[h][:, :M_HEAD_DIM] for h in hs]
    qm_b = [qm[h].astype(BF16) for h in hs]
    km = [qk[h][:, M_HEAD_DIM:] * (M_HEAD_DIM ** -0.5) for h in hs]
    km_b = [km[h].astype(BF16) for h in hs]
    vh_b = [z[:, COL_VM + h * M_HEAD_DIM:COL_VM + (h + 1) * M_HEAD_DIM].astype(BF16) for h in hs]
    c_old = [sc_ref[:, h] for h in hs]
    s_qk = [_dot_nt(qm_b[h], km_b[h]) for h in hs]
    cq = [_dot_nt(qm_b[h], c_old[h].reshape(ns * 128, 128).astype(BF16)) for h in hs]

    mask_c = si > tl
    dist_c = (WINDOW + tl - si).astype(F32)
    dist_n = (tl - sl_).astype(F32)
    pc_parts, pn_rows, inv = {}, [], {}
    for r in range(4):
        pn = []
        for c in range(2):
            h = r + 4 * c
            sc_c = s_cache[:, c * 32 + r * L:c * 32 + (r + 1) * L, :].reshape(BLOCK, LANES)
            sc_n = s_new[r * LANES:(r + 1) * LANES, c * BLOCK:(c + 1) * BLOCK]
            (p_c, p_n), inv[h] = _softmax_parts(
                [(sc_c, mask_c, -ALIBI_SLOPES[h] * dist_c), (sc_n, causal_seq, -ALIBI_SLOPES[h] * dist_n)],
                sinks_ref[h])
            pc_parts[(c, r)] = p_c.reshape(ns, L, LANES)
            pn.append(p_n.astype(BF16))
        pn_rows.append(jnp.concatenate(pn, axis=1))
    p8 = jnp.concatenate([pc_parts[(c, r)] for c in range(2) for r in range(4)], axis=1).astype(BF16)
    o_cache = jnp.einsum('nqk,ndk->nqd', p8, cv_ref[...].astype(BF16), preferred_element_type=F32)
    o_new = _dot(jnp.concatenate(pn_rows, axis=0), vn_mask)
    groups = []
    for r in range(4):
        oa = o_cache[:, r * L:(r + 1) * L, :].reshape(BLOCK, LANES)
        ob = o_cache[:, 32 + r * L:32 + (r + 1) * L, :].reshape(BLOCK, LANES)
        o = jnp.where(lo_half, oa, ob) + o_new[r * LANES:(r + 1) * LANES]
        groups.append(o * jnp.where(lo_half, inv[r], inv[r + 4]))
    y_att = _attn_head_norm(jnp.concatenate(groups, axis=1), gattn_ref[...])

    sk_ref[:, 0:WINDOW - L, :] = jnp.swapaxes(ck_ref[...], 1, 2)[:, L:WINDOW, :]
    sk_ref[:, WINDOW - L:WINDOW, :] = k_new.reshape(ns, L, LANES)
    sv_ref[:, 0:WINDOW - L, :] = jnp.swapaxes(cv_ref[...], 1, 2)[:, L:WINDOW, :]
    sv_ref[:, WINDOW - L:WINDOW, :] = v_new.reshape(ns, L, LANES)

    b_c = [xg[:, 4 + h:5 + h] for h in hs]
    li_c = [xg[:, h:h + 1] for h in hs]
    b_r = [xgt[4 + h:5 + h, :] for h in hs]
    li_r = [xgt[h:h + 1, :] for h in hs]
    bl_c = [blast[:, 4 + h:5 + h] for h in hs]
    m_prev = [m_rep[:, h:h + 1] for h in hs]
    dmat = [jnp.where(causal_seq, b_c[h] - b_r[h] + li_r[h], -jnp.inf) for h in hs]
    m_inter = [b_c[h] + m_prev[h] for h in hs]
    m_t = [jnp.maximum(m_inter[h], jnp.max(dmat[h], axis=1, keepdims=True)) for h in hs]
    w_inter = [jnp.exp(m_inter[h] - m_t[h]) for h in hs]
    sc = [s_qk[h] * jnp.exp(dmat[h] - m_t[h]) for h in hs]
    num_intra = [_dot(sc[h].astype(BF16), vh_b[h]) for h in hs]

    m_end3 = [jnp.broadcast_to(m_t[h], (BLOCK, LANES)).reshape(ns, L, LANES)[:, L - 1:L, :] for h in hs]
    m_end = [jnp.broadcast_to(m_end3[h], (ns, L, LANES)).reshape(BLOCK, LANES)[:, 0:1] for h in hs]
    decay = [jnp.exp(bl_c[h] + m_prev[h] - m_end[h]) for h in hs]
    kw = [km[h] * jnp.exp(bl_c[h] - b_c[h] + li_c[h] - m_end[h]) for h in hs]
    seq_of_row = ti >> 3
    m_all = jnp.zeros((BLOCK, LANES), F32)
    for h in hs:
        kw_b = kw[h].astype(BF16)
        k_big = jnp.concatenate(
            [jnp.where(seq_of_row == g, kw_b, jnp.zeros_like(kw_b)) for g in range(ns)], axis=1)
        c_up = _dot_tn(vh_b[h], k_big)
        c_up3 = jnp.stack([c_up[:, g * M_HEAD_DIM:(g + 1) * M_HEAD_DIM] for g in range(ns)], axis=0)
        decay3 = jnp.broadcast_to(decay[h], (BLOCK, LANES)).reshape(ns, L, LANES)[:, 0:1, :]
        n_old = sn_ref[:, h:h + 1, :]
        sco_ref[:, h] = decay3 * c_old[h] + c_up3
        sno_ref[:, h:h + 1, :] = decay3 * n_old + jnp.sum(kw[h].reshape(ns, L, LANES), axis=1, keepdims=True)
        m_all = jnp.where(lane == h, m_t[h], m_all)
    m_rows[...] = m_all.reshape(ns, L, LANES)
    m_acc[pl.ds(pl.multiple_of(step * ns, ns), ns), :] = m_rows[:, L - 1, :]

    y_m = []
    for h in hs:
        num_inter = jnp.concatenate([cq[h][g * L:(g + 1) * L, g * M_HEAD_DIM:(g + 1) * M_HEAD_DIM] for g in range(ns)], axis=0)
        n_rep = jnp.broadcast_to(sn_ref[:, h:h + 1, :], (ns, L, LANES)).reshape(BLOCK, LANES)
        num = num_intra[h] + w_inter[h] * num_inter
        den = jnp.sum(sc[h], axis=1, keepdims=True) + w_inter[h] * jnp.sum(qm[h] * n_rep, axis=1, keepdims=True)
        hh = num / jnp.maximum(jnp.abs(den), jnp.exp(-m_t[h]))
        hm = _sigmoid(z[:, COL_O + h * M_HEAD_DIM:COL_O + (h + 1) * M_HEAD_DIM]) * hh
        y_m.append(_mlstm_head_norm(hm, gml_ref[:, h * M_HEAD_DIM:(h + 1) * M_HEAD_DIM]))

    ymix_ref[...] = jnp.concatenate([y_att] + y_m, axis=1).astype(BF16)

    @pl.when(step == pl.num_programs(0) - 1)
    def _write_m_state():
        smo_ref[...] = m_acc[...].T[0:M_HEADS, :]


def _const_spec(shape):
    nd = len(shape)
    return pl.BlockSpec(shape, lambda *_: (0,) * nd, pipeline_mode=pl.Buffered(1))


def _mixer_weight_specs():
    return [
        _const_spec((D_MODEL, IN_MAIN + IN_GATES)),
        _const_spec((CONV_W, MLSTM_W)),
        _const_spec((1, MLSTM_W)),
        _const_spec((M_HEADS, M_HEAD_DIM, 2 * M_HEAD_DIM)),
        pl.BlockSpec(memory_space=pltpu.SMEM),
        pl.BlockSpec(memory_space=pltpu.SMEM),
        pl.BlockSpec(memory_space=pltpu.SMEM),
        _const_spec((1, ATT_W)),
        _const_spec((1, MLSTM_W)),
    ]


def _post_weight_specs(d):
    hbm = pl.BlockSpec(memory_space=pl.ANY)
    return [hbm, _const_spec((1, d)), _const_spec((1, d)), hbm, hbm, hbm, _const_spec((1, d)), _const_spec((1, d))]


def _prompt_layer(x_prompt, meta_tokens, x_sample_rows, ymix_sample, mixer_w, post_w):
    b, s, d = x_prompt.shape
    state_shapes = [((BLOCK, KV_W), BF16)] * 4 + [
        ((8, MLSTM_W), F32), ((M_HEADS, M_HEAD_DIM, 2 * M_HEAD_DIM), F32), ((8, LANES), F32)]
    rows = STEP_BLOCKS * BLOCK
    npair = s // rows
    total = b * npair
    n_pair_steps = total + 2
    n_sample_steps = x_sample_rows.shape[0] // rows
    steps = n_pair_steps + n_sample_steps

    def block_map(lag):
        def index(t):
            tb = jnp.clip(t - lag, 0, total - 1)
            return (tb // npair, tb % npair, 0)
        return index

    def smap(t):
        return (jnp.clip(t - 1, 0, total - 1) // npair, 0, 0)

    def sample_map(t):
        return (jnp.clip(t - n_pair_steps, 0, n_sample_steps - 1), 0)

    out_shape = (
        jax.ShapeDtypeStruct((b, s, d), F32),
        jax.ShapeDtypeStruct(x_sample_rows.shape, F32),
        jax.ShapeDtypeStruct((b, BLOCK, KV_W), F32),
        jax.ShapeDtypeStruct((b, BLOCK, KV_W), F32),
        jax.ShapeDtypeStruct((CONV_W - 1, b, MLSTM_W), F32),
        jax.ShapeDtypeStruct((b, M_HEADS, M_HEAD_DIM, M_HEAD_DIM), F32),
        jax.ShapeDtypeStruct((b, M_HEADS, M_HEAD_DIM), F32),
        jax.ShapeDtypeStruct((b, M_HEADS), F32),
    )
    out_specs = (
        pl.BlockSpec((None, rows, d), block_map(2)),
        pl.BlockSpec((rows, d), sample_map),
        pl.BlockSpec((None, BLOCK, KV_W), smap),
        pl.BlockSpec((None, BLOCK, KV_W), smap),
        pl.BlockSpec((CONV_W - 1, b, MLSTM_W), lambda t: (0, 0, 0)),
        pl.BlockSpec((None, M_HEADS, M_HEAD_DIM, M_HEAD_DIM), lambda t: smap(t) + (0,)),
        pl.BlockSpec((None, M_HEADS, M_HEAD_DIM), smap),
        pl.BlockSpec((b, M_HEADS), lambda t: (0, 0)),
    )
    return pl.pallas_call(
        functools.partial(_pair_kernel, npair=npair, n_pair_steps=n_pair_steps),
        grid=(steps,),
        in_specs=[pl.BlockSpec((None, rows, d), block_map(0)), pl.BlockSpec((None, rows, d), block_map(2)),
                  _const_spec((N_META, d)), pl.BlockSpec((rows, d), sample_map), pl.BlockSpec((rows, d), sample_map)]
        + _mixer_weight_specs() + _post_weight_specs(d),
        out_specs=out_specs,
        out_shape=out_shape,
        scratch_shapes=[
            pltpu.VMEM((d, d), BF16), pltpu.VMEM((d, D_FF), BF16), pltpu.VMEM((d, D_FF), BF16),
            pltpu.VMEM((D_FF, d), BF16),
            pltpu.VMEM((WEIGHT_SLOTS, WEIGHT_ROWS, D_FF), F32),
            pltpu.SemaphoreType.DMA((2 * WEIGHT_SLOTS + 2 * STEP_BLOCKS,)),
        ] + [pltpu.VMEM(shape, dt) for shape, dt in state_shapes] + [
            pltpu.VMEM((rows, IN_PAD), F32),
            pltpu.VMEM((rows, IN_PAD), F32),
            pltpu.VMEM((BLOCK, 2 * KV_W), F32),
        ] + [pltpu.VMEM((BLOCK, KV_W), BF16)] * 4 + [
            pltpu.VMEM((rows + 8, MLSTM_W), F32),
            pltpu.VMEM((M_HEADS, M_HEAD_DIM, 2 * M_HEAD_DIM), F32),
            pltpu.VMEM((8, LANES), F32),
            pltpu.VMEM((rows, d), BF16),
        ],
        compiler_params=pltpu.CompilerParams(
            dimension_semantics=("arbitrary",), vmem_limit_bytes=VMEM_LIMIT),
        name="prompt_layer",
    )(x_prompt, x_prompt, meta_tokens, x_sample_rows, ymix_sample, *mixer_w, *post_w)


def _sample_mixer(x_sample, ck, cv, s_conv, s_c, s_n, s_m, mixer_w):
    n, l, d = x_sample.shape
    assert n == LANES, "the m-state transposes assume one lane per sequence"
    t = SEQ_TILE
    m3 = lambda i: (i, 0, 0)
    m4 = lambda i: (i, 0, 0, 0)
    out_shape = (
        jax.ShapeDtypeStruct((n * l, d), BF16),
        jax.ShapeDtypeStruct((n, WINDOW, KV_W), F32),
        jax.ShapeDtypeStruct((n, WINDOW, KV_W), F32),
        jax.ShapeDtypeStruct((CONV_W - 1, n, MLSTM_W), F32),
        jax.ShapeDtypeStruct((n, M_HEADS, M_HEAD_DIM, M_HEAD_DIM), F32),
        jax.ShapeDtypeStruct((n, M_HEADS, M_HEAD_DIM), F32),
        jax.ShapeDtypeStruct((M_HEADS, n), F32),
    )
    m_state_spec = pl.BlockSpec((M_HEADS, n), lambda i: (0, 0))
    out_specs = (
        pl.BlockSpec((t * l, d), lambda i: (i, 0)),
        pl.BlockSpec((t, WINDOW, KV_W), m3),
        pl.BlockSpec((t, WINDOW, KV_W), m3),
        pl.BlockSpec((CONV_W - 1, t, MLSTM_W), lambda i: (0, i, 0)),
        pl.BlockSpec((t, M_HEADS, M_HEAD_DIM, M_HEAD_DIM), m4),
        pl.BlockSpec((t, M_HEADS, M_HEAD_DIM), m3),
        m_state_spec,
    )
    in_specs = [
        pl.BlockSpec((t, l, d), m3),
        pl.BlockSpec((t, WINDOW, KV_W), m3),
        pl.BlockSpec((t, WINDOW, KV_W), m3),
        pl.BlockSpec((CONV_W - 1, t, MLSTM_W), lambda i: (0, i, 0)),
        pl.BlockSpec((t, M_HEADS, M_HEAD_DIM, M_HEAD_DIM), m4),
        pl.BlockSpec((t, M_HEADS, M_HEAD_DIM), m3),
        m_state_spec,
    ] + _mixer_weight_specs()
    return pl.pallas_call(
        _sample_kernel,
        grid=(n // t,),
        in_specs=in_specs,
        out_specs=out_specs,
        out_shape=out_shape,
        scratch_shapes=[pltpu.VMEM((t, 16, MLSTM_W), F32),
                        pltpu.VMEM((LANES, LANES), F32),
                        pltpu.VMEM((LANES, LANES), F32),
                        pltpu.VMEM((t, l, LANES), F32)],
        compiler_params=pltpu.CompilerParams(
            dimension_semantics=("arbitrary",), vmem_limit_bytes=VMEM_LIMIT),
        name="sample_mixer",
    )(x_sample, ck, cv, s_conv, s_c, s_n, s_m, *mixer_w)


def kernel(x_prompt, x_sample, cache_k, cache_v, state_conv, state_C, state_n, state_m, meta_tokens,
           w_in, w_conv, b_conv, w_mq, w_mk, b_i, b_f, attn_sinks, g_attn, g_mlstm, w_out,
           ln1_g, ln1_b, w_gate, w_up, w_down, ln2_g, ln2_b):
    b, s, d = x_prompt.shape
    n, l, _ = x_sample.shape

    wqk = jnp.concatenate([w_mq[0], w_mk[0]], axis=-1).astype(BF16)
    mixer_w = (w_in[0].astype(BF16), w_conv[0], b_conv[0].reshape(1, MLSTM_W), wqk, b_i[0], b_f[0], attn_sinks[0],
               g_attn[0].reshape(1, ATT_W), g_mlstm[0].reshape(1, MLSTM_W))
    post_w = (w_out[0], ln1_g[0].reshape(1, d), ln1_b[0].reshape(1, d), w_gate[0], w_up[0], w_down[0],
              ln2_g[0].reshape(1, d), ln2_b[0].reshape(1, d))

    key_minor = lambda c: c[0].transpose(0, 2, 3, 1).reshape(n, KV_W, WINDOW)
    ymix_s, sk, sv, scv, s_c, s_n, s_mo = _sample_mixer(
        x_sample, key_minor(cache_k), key_minor(cache_v), state_conv[0].swapaxes(0, 1), state_C[0], state_n[0],
        state_m[0].T, mixer_w)

    y_prompt, y_sample, pk, pv, pconv, p_c, p_n, p_m = _prompt_layer(
        x_prompt, meta_tokens, x_sample.reshape(n * l, d), ymix_s, mixer_w, post_w)

    kv5 = lambda a: a.reshape(1, a.shape[0], WINDOW, 2, HEAD_DIM)
    kv5_key_minor = lambda a: a.reshape(a.shape[0], 2, HEAD_DIM, WINDOW).transpose(0, 3, 1, 2)[None]
    return (y_prompt, y_sample.reshape(n, l, d),
            kv5_key_minor(pk), kv5_key_minor(pv), pconv.swapaxes(0, 1)[None], p_c[None], p_n[None], p_m[None],
            kv5(sk), kv5(sv), scv.swapaxes(0, 1)[None], s_c[None], s_n[None], s_mo.T[None])
```

```python
import functools

import jax
import jax.numpy as jnp
from jax import lax
from jax.experimental import pallas as pl
from jax.experimental.pallas import tpu as pltpu

F32 = jnp.float32
BF16 = jnp.bfloat16

D_MODEL = 1024
ATT_W = 512
MLSTM_W = 512
HEAD_DIM = 64
N_HEADS = 8
KV_W = 128
WINDOW = 128
BLOCK = 128
M_HEADS = 4
M_HEAD_DIM = 128
CONV_W = 4
N_META = 16
META_PAD = BLOCK - N_META
D_FF = 2816
LANES = 128
COL_CONV = ATT_W + 2 * KV_W
COL_VM = COL_CONV + MLSTM_W
COL_O = COL_VM + MLSTM_W
IN_MAIN = COL_O + MLSTM_W
IN_GATES = 2 * M_HEADS
IN_PAD = IN_MAIN + LANES
DEPTH = 1
ALIBI_SLOPES = tuple(2.0 ** (-8.0 * (h + 1) / N_HEADS) for h in range(N_HEADS))
DEEPNORM_ALPHA = (2.0 * DEPTH) ** 0.25
EPS = 1e-5
SEQ_TILE = 16
STEP_BLOCKS = 2
FF_CHUNK = 256
FF_LOOKAHEAD = 3
PROJ_CHUNK = 256
WEIGHT_ROWS = 128
WEIGHT_SLOTS = 4
VMEM_LIMIT = 56 * 1024 * 1024


def _paired_q_group(q_group, r):
    a, b = q_group(r // 2), q_group(r // 2 + N_HEADS // 4)
    lo_half = lax.broadcasted_iota(jnp.int32, a.shape, 1) < HEAD_DIM
    if r % 2 == 0:
        return jnp.where(lo_half, a, pltpu.roll(b, HEAD_DIM, 1))
    return jnp.where(lo_half, pltpu.roll(a, HEAD_DIM, 1), b)


def _dot(a, b):
    return jnp.dot(a, b, preferred_element_type=F32)


def _dot_nt(a, b):
    return lax.dot_general(a, b, (((1,), (1,)), ((), ())), preferred_element_type=F32)


def _dot_tn(a, b):
    return lax.dot_general(a, b, (((0,), (0,)), ((), ())), preferred_element_type=F32)


def _split2(x):
    hi = x.astype(BF16)
    lo = (x - hi.astype(F32)).astype(BF16)
    return hi, lo


def _sigmoid(x):
    return 1.0 / (1.0 + jnp.exp(-x))


def _log_sigmoid(x):
    return -(jnp.maximum(-x, 0.0) + jnp.log1p(jnp.exp(-jnp.abs(x))))


def _layer_norm(x, g, b):
    mu = jnp.mean(x, axis=-1, keepdims=True)
    xc = x - mu
    var = jnp.mean(xc * xc, axis=-1, keepdims=True)
    return xc * lax.rsqrt(var + EPS) * g + b


def _attn_head_norm(att, g):
    lo_half = lax.broadcasted_iota(jnp.int32, (att.shape[0], LANES), 1) < HEAD_DIM
    g8 = jnp.broadcast_to(g, (8, ATT_W))
    g_group = lambda i: g8[:, i * LANES:(i + 1) * LANES]

    def seg_mean(x):
        s_all = jnp.sum(x, axis=1, keepdims=True)
        s_lo = jnp.sum(jnp.where(lo_half, x, 0.0), axis=1, keepdims=True)
        return jnp.where(lo_half, s_lo, s_all - s_lo) * (1.0 / HEAD_DIM)

    out = []
    for grp in range(att.shape[1] // LANES):
        x = att[:, grp * LANES:(grp + 1) * LANES]
        xc = x - seg_mean(x)
        out.append(xc * lax.rsqrt(seg_mean(xc * xc) + EPS) * _paired_q_group(g_group, grp)[0:1, :])
    return jnp.concatenate(out, axis=1)


def _mlstm_head_norm(hm, g):
    mu = jnp.mean(hm, axis=-1, keepdims=True)
    xc = hm - mu
    var = jnp.mean(xc * xc, axis=-1, keepdims=True)
    return xc * lax.rsqrt(var + EPS) * g


def _softmax_parts(parts, sink):
    sp = [jnp.where(m, s * (HEAD_DIM ** -0.5) + a, -jnp.inf) for s, m, a in parts]
    mx = sink
    for s in sp:
        mx = jnp.maximum(mx, jnp.max(s, axis=1, keepdims=True))
    ps = [jnp.exp(s - mx) for s in sp]
    den = jnp.exp(sink - mx)
    for p in ps:
        den = den + jnp.sum(p, axis=1, keepdims=True)
    return ps, 1.0 / den


def _conv_silu(window, wconv_ref, bconv_ref):
    acc = bconv_ref[...]
    for j in range(CONV_W):
        acc = acc + window(j) * wconv_ref[j:j + 1, :]
    return acc * _sigmoid(acc)


def _gate_columns(gates, gb_ref, row_valid, ltri, extra=None):
    lane = lax.broadcasted_iota(jnp.int32, gates.shape, 1)
    bi_ref, bf_ref = gb_ref
    lane1 = lax.broadcasted_iota(jnp.int32, (1, LANES), 1)
    bias = jnp.zeros((1, LANES), F32)
    for h in range(M_HEADS):
        bias = jnp.where(lane1 == h, bi_ref[h], jnp.where(lane1 == M_HEADS + h, bf_ref[h], bias))
    gb = gates + bias
    is_i = lane < M_HEADS
    is_f = (lane >= M_HEADS) & (lane < 2 * M_HEADS)
    logf = _log_sigmoid(gb)
    if row_valid is not None:
        lf = jnp.where(is_f & row_valid, logf, 0.0)
        li = jnp.where(row_valid, gb, -jnp.inf)
    else:
        lf = jnp.where(is_f, logf, 0.0)
        li = gb
    hi_lo = jnp.concatenate(_split2(lf), axis=1)
    bc = _dot(ltri, hi_lo)
    x = jnp.where(is_i, li, bc[:, :LANES] + bc[:, LANES:])
    if extra is None:
        return x, None
    be = _dot(extra, hi_lo)
    return x, be[:, :LANES] + be[:, LANES:]


class _FfnStages:
    N_CHUNKS = D_FF // FF_CHUNK

    def __init__(self, x, ymix, wout_ref, ln1g_ref, ln1b_ref, wg_ref, wu_ref, wd_ref, ln2g_ref, ln2b_ref):
        self.x, self.ymix = x, ymix
        self.wout_ref, self.ln1g_ref, self.ln1b_ref = wout_ref, ln1g_ref, ln1b_ref
        self.wg_ref, self.wu_ref, self.wd_ref = wg_ref, wu_ref, wd_ref
        self.ln2g_ref, self.ln2b_ref = ln2g_ref, ln2b_ref

    def head(self):
        self.x1 = _layer_norm(DEEPNORM_ALPHA * self.x + _dot(self.ymix, self.wout_ref[...]),
                              self.ln1g_ref[...], self.ln1b_ref[...])
        self.x1b = self.x1.astype(BF16)
        self.acc = DEEPNORM_ALPHA * self.x1

    def _gate_up(self, c):
        cs = slice(c * FF_CHUNK, (c + 1) * FF_CHUNK)
        return _dot(self.x1b, self.wg_ref[:, cs]), _dot(self.x1b, self.wu_ref[:, cs])

    def chunk(self, c):
        if c == 0:
            self.gu = {}
        for n in range(c, min(c + FF_LOOKAHEAD, self.N_CHUNKS - 1) + 1):
            if n not in self.gu:
                self.gu[n] = self._gate_up(n)
        g, u = self.gu.pop(c)
        hid = (g * _sigmoid(g) * u).astype(BF16)
        self.acc = self.acc + _dot(hid, self.wd_ref[c * FF_CHUNK:(c + 1) * FF_CHUNK, :])

    def tail(self):
        return _layer_norm(self.acc, self.ln2g_ref[...], self.ln2b_ref[...])

    def run(self):
        self.head()
        for c in range(self.N_CHUNKS):
            self.chunk(c)
        return self.tail()


def _mix_blocks(zc, nb, pad_rows, thr_first, state, cbuf, wconv_ref, bconv_ref, wqk_ref, gb_ref,
                sinks_ref, gattn_ref, gml_ref, hook):
    rows = nb * BLOCK
    kp_lo, kp_hi, vp_lo, vp_hi, c_state, m_state = state
    c_state, m_state = list(c_state), list(m_state)
    lane = lax.broadcasted_iota(jnp.int32, (BLOCK, LANES), 1)
    lo_half = lane < HEAD_DIM
    row_valid = None
    if pad_rows:
        row_valid = lax.broadcasted_iota(jnp.int32, (rows, 1), 0) >= pad_rows

    s_all, vmasks = [], []
    for blk in range(nb):
        rs = slice(blk * BLOCK, (blk + 1) * BLOCK)
        kb = zc[rs, ATT_W:ATT_W + KV_W].astype(BF16)
        vb = zc[rs, ATT_W + KV_W:ATT_W + 2 * KV_W].astype(BF16)
        zero_b = jnp.zeros_like(kb)
        k_lo, k_hi = jnp.where(lo_half, kb, zero_b), jnp.where(lo_half, zero_b, kb)
        v_lo, v_hi = jnp.where(lo_half, vb, zero_b), jnp.where(lo_half, zero_b, vb)
        kmask = jnp.concatenate([kp_lo, k_lo, kp_hi, k_hi], axis=0)
        vmasks.append(jnp.concatenate([vp_lo, v_lo, vp_hi, v_hi], axis=0))
        q_group = lambda g, rs=rs: zc[rs, g * LANES:(g + 1) * LANES]
        q4 = jnp.concatenate([_paired_q_group(q_group, r) for r in range(4)], axis=0).astype(BF16)
        s_all.append(_dot_nt(q4, kmask))
        kp_lo, kp_hi, vp_lo, vp_hi = k_lo, k_hi, v_lo, v_hi
    hook("scores")

    c_in = zc[:, COL_CONV:COL_VM]
    if row_valid is not None:
        c_in = jnp.where(row_valid, c_in, 0.0)
    cbuf[8:8 + rows, :] = c_in
    c_act = _conv_silu(lambda jj: cbuf[5 + jj:5 + jj + rows, :], wconv_ref, bconv_ref)
    cbuf[0:8, :] = cbuf[rows:rows + 8, :]
    hs = range(M_HEADS)
    ca_b = [c_act[:, h * M_HEAD_DIM:(h + 1) * M_HEAD_DIM].astype(BF16) for h in hs]
    qk = [_dot(ca_b[h], wqk_ref[h]) for h in hs]
    qm_b = [qk[h][:, :M_HEAD_DIM].astype(BF16) for h in hs]
    kmt_f = [(qk[h][:, M_HEAD_DIM:] * (M_HEAD_DIM ** -0.5)).T for h in hs]
    hook("front")

    ti = lax.broadcasted_iota(jnp.int32, (rows, rows), 0)
    si = lax.broadcasted_iota(jnp.int32, (rows, rows), 1)
    ltri = jnp.where((si <= ti) & ((si >> 7) == (ti >> 7)), 1.0, 0.0).astype(BF16)
    xg, _ = _gate_columns(zc[:, IN_MAIN:IN_PAD], gb_ref, row_valid, ltri)
    xgt = xg.T
    hook("gates")

    qi = lax.broadcasted_iota(jnp.int32, (BLOCK, BLOCK), 0)
    ks = lax.broadcasted_iota(jnp.int32, (BLOCK, BLOCK), 1)
    own = ks <= qi
    distf = jnp.where(own, qi - ks, WINDOW + qi - ks).astype(F32)
    zero_p = jnp.zeros((BLOCK, BLOCK), BF16)
    att_rows = []
    for blk in range(nb):
        live = None
        if blk == 0 and thr_first is not None:
            live = jnp.where(own, ks + BLOCK, ks) >= thr_first
        p_rows, inv = [], {}
        for r in range(4):
            pr = []
            for c in range(2):
                h = r + 4 * c
                s_blk = s_all[blk][r * LANES:(r + 1) * LANES, c * 256:(c + 1) * 256]
                sp = jnp.where(own, s_blk[:, BLOCK:], s_blk[:, :BLOCK]) * (HEAD_DIM ** -0.5) \
                    - ALIBI_SLOPES[h] * distf
                if live is not None:
                    sp = jnp.where(live, sp, -jnp.inf)
                sink = sinks_ref[h]
                mx = jnp.maximum(jnp.max(sp, axis=1, keepdims=True), sink)
                p = jnp.exp(sp - mx)
                inv[h] = 1.0 / (jnp.sum(p, axis=1, keepdims=True) + jnp.exp(sink - mx))
                p_b = p.astype(BF16)
                pr += [jnp.where(own, zero_p, p_b), jnp.where(own, p_b, zero_p)]
            p_rows.append(jnp.concatenate(pr, axis=1))
            hook("softmax", blk * 4 + r)
        o_all = _dot(jnp.concatenate(p_rows, axis=0), vmasks[blk])
        att_rows.append(jnp.concatenate(
            [o_all[r * LANES:(r + 1) * LANES] * jnp.where(lo_half, inv[r], inv[r + 4]) for r in range(4)], axis=1))
        hook("pv", blk)

    ones_col = jnp.where(lane == 0, 1.0, 0.0).astype(BF16)
    tb = lax.broadcasted_iota(jnp.int32, (BLOCK, BLOCK), 0)
    sb = lax.broadcasted_iota(jnp.int32, (BLOCK, BLOCK), 1)
    tri = sb <= tb
    hh_rows = [[] for _ in hs]
    for blk in range(nb):
        rs = slice(blk * BLOCK, (blk + 1) * BLOCK)
        b_c = [xg[rs, 4 + h:5 + h] for h in hs]
        b_r = [xgt[4 + h:5 + h, rs] for h in hs]
        li_r = [xgt[h:h + 1, rs] for h in hs]
        kmt = [kmt_f[h][:, rs] for h in hs]
        v_ext = [jnp.concatenate([zc[rs, COL_VM + h * M_HEAD_DIM:COL_VM + (h + 1) * M_HEAD_DIM].astype(BF16), ones_col], axis=1)
                 for h in hs]
        s_qk = [_dot(qm_b[h][rs], kmt[h].astype(BF16)) for h in hs]
        inter = [_dot(qm_b[h][rs], c_state[h].astype(BF16)) for h in hs]
        hook("mlstm_a", blk)
        dmat = [jnp.where(tri, b_c[h] - b_r[h] + li_r[h], -jnp.inf) for h in hs]
        m_inter = [b_c[h] + m_state[h] for h in hs]
        m_t = [jnp.maximum(m_inter[h], jnp.max(dmat[h], axis=1, keepdims=True)) for h in hs]
        w_inter = [jnp.exp(m_inter[h] - m_t[h]) for h in hs]
        sc_b = [(s_qk[h] * jnp.exp(dmat[h] - m_t[h])).astype(BF16) for h in hs]
        hook("mlstm_b", blk)
        nd = [_dot(sc_b[h], v_ext[h]) + w_inter[h] * inter[h] for h in hs]
        for h in hs:
            m_end = m_t[h][BLOCK - 1:BLOCK, :]
            b_last = b_c[h][BLOCK - 1:BLOCK, :]
            decay = jnp.exp(b_last + m_state[h] - m_end)
            wk_r = jnp.exp(b_last - b_r[h] + li_r[h] - m_end)
            c_state[h] = decay * c_state[h] + _dot((kmt[h] * wk_r).astype(BF16), v_ext[h])
            m_state[h] = m_end
            num, den = nd[h][:, :M_HEAD_DIM], nd[h][:, M_HEAD_DIM:M_HEAD_DIM + 1]
            hh_rows[h].append(num / jnp.maximum(jnp.abs(den), jnp.exp(-m_t[h])))
        hook("mlstm_c", blk)

    y_att = _attn_head_norm(jnp.concatenate(att_rows, axis=0), gattn_ref[...])
    hook("attnorm")
    y_m = []
    for h in hs:
        hm = _sigmoid(zc[:, COL_O + h * M_HEAD_DIM:COL_O + (h + 1) * M_HEAD_DIM]) * jnp.concatenate(hh_rows[h], axis=0)
        y_m.append(_mlstm_head_norm(hm, gml_ref[:, h * M_HEAD_DIM:(h + 1) * M_HEAD_DIM]))
        hook("headnorm", h)
    ymix = jnp.concatenate([y_att] + y_m, axis=1).astype(BF16)
    return ymix, (kp_lo, kp_hi, vp_lo, vp_hi, c_state, m_state)


def _project(xb, w_in_refs, lo=0, hi=IN_PAD):
    bounds = (0, IN_MAIN, IN_PAD)
    parts = []
    for ref, start, stop in zip(w_in_refs, bounds[:-1], bounds[1:]):
        a, b = max(lo, start), min(hi, stop)
        if a < b:
            parts.append(_dot(xb, ref[:, a - start:b - start]))
    return parts[0] if len(parts) == 1 else jnp.concatenate(parts, axis=1)


def _meta_state(meta_ref, w_inm_ref, w_ing_ref, wconv_ref, bconv_ref, wqk_ref, gb_ref,
                sinks_ref, gattn_ref,
                gml_ref, kplo_o, kphi_o, vplo_o, vphi_o, cb_o, ctn_o, mst_o, zc, cbuf):
    xb = jnp.concatenate([jnp.zeros((META_PAD, D_MODEL), F32), meta_ref[...]], axis=0)
    zc[...] = _project(xb.astype(BF16), (w_inm_ref, w_ing_ref))
    cbuf[0:8, :] = jnp.zeros((8, MLSTM_W), F32)
    zb = jnp.zeros((BLOCK, KV_W), BF16)
    state = (zb, zb, zb, zb, [jnp.zeros((M_HEAD_DIM, 2 * M_HEAD_DIM), F32)] * M_HEADS,
             [jnp.zeros((1, 1), F32)] * M_HEADS)
    _, (kp_lo, kp_hi, vp_lo, vp_hi, c_state, m_state) = _mix_blocks(
        zc, 1, META_PAD, BLOCK + META_PAD, state, cbuf, wconv_ref, bconv_ref, wqk_ref, gb_ref,
        sinks_ref, gattn_ref, gml_ref, lambda name, index=0: None)
    kplo_o[...] = kp_lo
    kphi_o[...] = kp_hi
    vplo_o[...] = vp_lo
    vphi_o[...] = vp_hi
    cb_o[...] = cbuf[0:8, :]
    for h in range(M_HEADS):
        ctn_o[h] = c_state[h]
        mst_o[h:h + 1, :] = jnp.broadcast_to(m_state[h], (1, LANES))
    mst_o[M_HEADS:8, :] = jnp.zeros((8 - M_HEADS, LANES), F32)


def _load_ffn_weights(wout_hbm, wg_hbm, wu_hbm, wd_hbm, wout_v, wg_v, wu_v, wd_v, stage_bufs, sems, overlap=None):
    half = D_FF // 2
    jobs = []
    for src, dst in ((wg_hbm, wg_v), (wu_hbm, wu_v)):
        jobs += [(src, r, WEIGHT_ROWS, c, half, dst, r) for r in range(0, D_MODEL, WEIGHT_ROWS) for c in (0, half)]
    jobs += [(wd_hbm, r, WEIGHT_ROWS, 0, D_MODEL, wd_v, r) for r in range(0, D_FF, WEIGHT_ROWS)]
    for h in range(N_HEADS):
        jobs.append((wout_hbm, h * HEAD_DIM, HEAD_DIM, 0, D_MODEL, wout_v, ((h % 4) * 2 + h // 4) * HEAD_DIM))
    jobs += [(wout_hbm, r, WEIGHT_ROWS, 0, D_MODEL, wout_v, r) for r in range(ATT_W, D_MODEL, WEIGHT_ROWS)]

    windows = []
    for buf in stage_bufs:
        views = [buf.at[s] for s in range(buf.shape[0])] if len(buf.shape) == 3 else [buf]
        for v in views:
            windows += [(v, r, c) for r in range(0, v.shape[0] - WEIGHT_ROWS + 1, WEIGHT_ROWS)
                        for c in range(0, v.shape[1] - half + 1, half)]
    nslot = len(windows)
    assert nslot <= sems.shape[0]

    def window(i, nrows, ncols):
        v, r, c = windows[i % nslot]
        return v.at[pl.ds(r, nrows), pl.ds(c, ncols)]

    def copy(i):
        src, r0, nrows, c0, ncols, _, _ = jobs[i]
        return pltpu.make_async_copy(src.at[pl.ds(r0, nrows), pl.ds(c0, ncols)], window(i, nrows, ncols),
                                     sems.at[i % nslot])

    for i in range(min(nslot - 1, len(jobs))):
        copy(i).start()
    if overlap is not None:
        overlap()
    for i, (_, _, nrows, c0, ncols, dst, d0) in enumerate(jobs):
        if i + nslot - 1 < len(jobs):
            copy(i + nslot - 1).start()
        copy(i).wait()
        dst[d0:d0 + nrows, c0:c0 + ncols] = window(i, nrows, ncols)[...].astype(BF16)


def _pair_kernel(xin_ref, xres_ref, meta_ref, xs_ref, ymix_s_ref,
                 w_inm_ref, w_ing_ref, wconv_ref, bconv_ref, wqk_ref, bi_ref, bf_ref, sinks_ref,
                 gattn_ref, gml_ref,
                 wout_hbm, ln1g_ref, ln1b_ref, wg_hbm, wu_hbm, wd_hbm, ln2g_ref, ln2b_ref,
                 y_ref, ys_ref, pk_ref, pv_ref, pconv_ref, pc_ref, pn_ref, pm_ref,
                 wout_v, wg_v, wu_v, wd_v, wstage, wsems,
                 kplo_i, kphi_i, vplo_i, vphi_i, cb_i, ctn_i, mst_i,
                 zcur, znext, kvlast, kplo, kphi, vplo, vphi, cbuf, ctn, mst, yprev, *, npair, n_pair_steps):
    t = pl.program_id(0)
    p = lax.rem(t + (npair - 1), npair)
    gb_ref = (bi_ref, bf_ref)

    @pl.when(t == 0)
    def _init_pipeline():
        def meta_state():
            _meta_state(meta_ref, w_inm_ref, w_ing_ref, wconv_ref, bconv_ref, wqk_ref, gb_ref,
                        sinks_ref, gattn_ref, gml_ref, kplo_i, kphi_i, vplo_i, vphi_i, cb_i, ctn_i, mst_i,
                        znext.at[pl.ds(0, BLOCK)], cbuf)

        _load_ffn_weights(wout_hbm, wg_hbm, wu_hbm, wd_hbm, wout_v, wg_v, wu_v, wd_v,
                          (wstage, zcur, znext.at[pl.ds(BLOCK, BLOCK)]), wsems, overlap=meta_state)
        zcur[...] = jnp.zeros_like(zcur)
        yprev[...] = jnp.zeros_like(yprev)

    @pl.when((p == 0) | (t == 0))
    def _load_meta_state():
        kplo[...] = kplo_i[...]
        kphi[...] = kphi_i[...]
        vplo[...] = vplo_i[...]
        vphi[...] = vphi_i[...]
        cbuf[0:8, :] = cb_i[...]
        ctn[...] = ctn_i[...]
        mst[...] = mst_i[...]

    ffn_w = (wout_v, ln1g_ref, ln1b_ref, wg_v, wu_v, wd_v, ln2g_ref, ln2b_ref)

    @pl.when(t < n_pair_steps)
    def _pair_step():
        ffn = _FfnStages(xres_ref[...], yprev[...], *ffn_w)
        xb_b = xin_ref[...].astype(BF16)

        n_proj = -(-IN_PAD // PROJ_CHUNK)

        def proj(n):
            lo, hi = n * PROJ_CHUNK, min((n + 1) * PROJ_CHUNK, IN_PAD)
            znext[:, lo:hi] = _project(xb_b, (w_inm_ref, w_ing_ref), lo, hi)

        proj_it, ffn_it = iter(range(n_proj)), iter(range(ffn.N_CHUNKS))

        def fill_proj(n):
            for c in [c for _, c in zip(range(n), proj_it)]:
                proj(c)

        def fill_ffn(n):
            for c in [c for _, c in zip(range(n), ffn_it)]:
                ffn.chunk(c)

        out = {}

        def hook(name, index=0):
            if name == "scores":
                fill_proj(2)
                ffn.head()
            elif name == "front":
                fill_proj(1)
            elif name == "gates":
                fill_ffn(1)
            elif name == "softmax":
                if index % 2 == 1:
                    fill_ffn(1)
            elif name in ("pv", "mlstm_a", "mlstm_b"):
                fill_ffn(1)
            elif name == "mlstm_c":
                fill_proj(1)
            elif name == "attnorm":
                fill_ffn(ffn.N_CHUNKS)
                out["y"] = ffn.tail()
                fill_proj(1)
            elif name == "headnorm":
                fill_proj(1)

        state = (kplo[...], kphi[...], vplo[...], vphi[...], [ctn[h] for h in range(M_HEADS)],
                 [mst[h:h + 1, 0:1] for h in range(M_HEADS)])
        thr_first = jnp.where(p == 0, META_PAD, 0)
        kvlast[...] = zcur[(STEP_BLOCKS - 1) * BLOCK:STEP_BLOCKS * BLOCK, ATT_W:ATT_W + 2 * KV_W]
        ymix, (kp_lo, kp_hi, vp_lo, vp_hi, c_state, m_state) = _mix_blocks(
            zcur, STEP_BLOCKS, 0, thr_first, state, cbuf, wconv_ref, bconv_ref, wqk_ref, gb_ref,
            sinks_ref, gattn_ref, gml_ref, hook)
        fill_proj(n_proj)
        y_ref[...] = out["y"]
        kplo[...] = kp_lo
        kphi[...] = kp_hi
        vplo[...] = vp_lo
        vphi[...] = vp_hi
        for h in range(M_HEADS):
            ctn[h] = c_state[h]
            mst[h:h + 1, :] = jnp.broadcast_to(m_state[h], (1, LANES))
        yprev[...] = ymix
        zcur[...] = znext[...]

    @pl.when(t >= n_pair_steps)
    def _sample_ffn_step():
        ys_ref[...] = _FfnStages(xs_ref[...], ymix_s_ref[...], *ffn_w).run()

    @pl.when((p == npair - 1) & (t > 0) & (t < n_pair_steps))
    def _final():
        pk_ref[...] = kvlast[:, 0:KV_W].T
        pv_ref[...] = kvlast[:, KV_W:2 * KV_W].T
        seq = lax.div(t - 1, npair)
        for j in range(CONV_W - 1):
            pconv_ref[j, pl.ds(seq, 1), :] = cbuf[8 - (CONV_W - 1) + j:8 - (CONV_W - 1) + j + 1, :]
        head_lane = lax.broadcasted_iota(jnp.int32, (1, LANES), 1)
        m_row = jnp.zeros((1, LANES), F32)
        for h in range(M_HEADS):
            m_row = jnp.where(head_lane == h, mst[h:h + 1, :], m_row)
        pm_ref[pl.ds(seq, 1), :] = m_row[:, 0:M_HEADS]
        for h in range(M_HEADS):
            c_n = ctn[h].T
            pc_ref[h] = c_n[0:M_HEAD_DIM, :]
            pn_ref[h:h + 1, :] = c_n[M_HEAD_DIM:M_HEAD_DIM + 1, :]


def _sample_kernel(xs_ref, ck_ref, cv_ref, sconv_ref, sc_ref, sn_ref, sm_ref,
                   w_inm_ref, w_ing_ref, wconv_ref, bconv_ref, wqk_ref, bi_ref, bf_ref, sinks_ref,
                   gattn_ref, gml_ref,
                   ymix_ref, sk_ref, sv_ref, scv_ref, sco_ref, sno_ref, smo_ref, cext, m_in, m_acc, m_rows):
    ns, L = SEQ_TILE, 8
    gb_ref = (bi_ref, bf_ref)
    step = pl.program_id(0)

    @pl.when(step == 0)
    def _transpose_m_state():
        m_in[...] = jnp.zeros_like(m_in)
        m_in[0:M_HEADS, :] = sm_ref[...]
        m_in[...] = m_in[...].T

    xs = xs_ref[...].reshape(ns * L, D_MODEL)
    z = _project(xs.astype(BF16), (w_inm_ref, w_ing_ref))

    lane = lax.broadcasted_iota(jnp.int32, (BLOCK, LANES), 1)
    lo_half = lane < HEAD_DIM
    lo3 = lax.broadcasted_iota(jnp.int32, (ns, L, LANES), 2) < HEAD_DIM
    ti = lax.broadcasted_iota(jnp.int32, (BLOCK, BLOCK), 0)
    si = lax.broadcasted_iota(jnp.int32, (BLOCK, BLOCK), 1)
    same_seq = (ti >> 3) == (si >> 3)
    tl, sl_ = ti & 7, si & 7
    causal_seq = same_seq & (sl_ <= tl)

    k_new = z[:, ATT_W:ATT_W + KV_W]
    v_new = z[:, ATT_W + KV_W:ATT_W + 2 * KV_W]
    kb, vb = k_new.astype(BF16), v_new.astype(BF16)
    zero_b = jnp.zeros_like(kb)
    kn_mask = jnp.concatenate([jnp.where(lo_half, kb, zero_b), jnp.where(lo_half, zero_b, kb)], axis=0)
    vn_mask = jnp.concatenate([jnp.where(lo_half, vb, zero_b), jnp.where(lo_half, zero_b, vb)], axis=0)
    qg = [_paired_q_group(lambda g: z[:, g * LANES:(g + 1) * LANES], r) for r in range(4)]
    s_new = _dot_nt(jnp.concatenate(qg, axis=0).astype(BF16), kn_mask)
    qg3 = [q.reshape(ns, L, LANES) for q in qg]
    q8 = jnp.concatenate([jnp.where(lo3, q, 0.0) for q in qg3] + [jnp.where(lo3, 0.0, q) for q in qg3],
                         axis=1).astype(BF16)
    s_cache = jnp.einsum('nqd,ndk->nqk', q8, ck_ref[...].astype(BF16), preferred_element_type=F32)

    c3 = z[:, COL_CONV:COL_VM].reshape(ns, L, MLSTM_W)
    cext[:, 8:16, :] = c3
    for j in range(CONV_W - 1):
        cext[:, 8 - (CONV_W - 1) + j, :] = sconv_ref[j]
        scv_ref[j] = cext[:, 16 - (CONV_W - 1) + j, :]
    c_act = _conv_silu(lambda jj: cext[:, 5 + jj:5 + jj + L, :], wconv_ref, bconv_ref)
    c_act = c_act.reshape(BLOCK, MLSTM_W)
    lblk = jnp.where(causal_seq, 1.0, 0.0).astype(BF16)
    lseq = jnp.where(same_seq, 1.0, 0.0).astype(BF16)
    xg, blast = _gate_columns(z[:, IN_MAIN:IN_PAD], gb_ref, None, lblk, extra=lseq)
    xgt = xg.T
    m_tile = m_in[pl.ds(pl.multiple_of(step * ns, ns), ns), :]
    m_rep = jnp.concatenate([jnp.broadcast_to(m_tile[g:g + 1, :], (L, LANES)) for g in range(ns)], axis=0)
    hs = range(M_HEADS)
    qk = [_dot(c_act[:, h * M_HEAD_DIM:(h + 1) * M_HEAD_DIM].astype(BF16), wqk_ref[h]) for h in hs]
    qm = [qk[h][:, :M_HEAD_DIM] for h in hs]
    qm_b = [qm[h].astype(BF16) for h in hs]
    km = [qk[h][:, M_HEAD_DIM:] * (M_HEAD_DIM ** -0.5) for h in hs]
    km_b = [km[h].astype(BF16) for h in hs]
    vh_b = [z[:, COL_VM + h * M_HEAD_DIM:COL_VM + (h + 1) * M_HEAD_DIM].astype(BF16) for h in hs]
    c_old = [sc_ref[:, h] for h in hs]
    s_qk = [_dot_nt(qm_b[h], km_b[h]) for h in hs]
    cq = [_dot_nt(qm_b[h], c_old[h].reshape(ns * 128, 128).astype(BF16)) for h in hs]

    mask_c = si > tl
    dist_c = (WINDOW + tl - si).astype(F32)
    dist_n = (tl - sl_).astype(F32)
    pc_parts, pn_rows, inv = {}, [], {}
    for r in range(4):
        pn = []
        for c in range(2):
            h = r + 4 * c
            sc_c = s_cache[:, c * 32 + r * L:c * 32 + (r + 1) * L, :].reshape(BLOCK, LANES)
            sc_n = s_new[r * LANES:(r + 1) * LANES, c * BLOCK:(c + 1) * BLOCK]
            (p_c, p_n), inv[h] = _softmax_parts(
                [(sc_c, mask_c, -ALIBI_SLOPES[h] * dist_c), (sc_n, causal_seq, -ALIBI_SLOPES[h] * dist_n)],
                sinks_ref[h])
            pc_parts[(c, r)] = p_c.reshape(ns, L, LANES)
            pn.append(p_n.astype(BF16))
        pn_rows.append(jnp.concatenate(pn, axis=1))
    p8 = jnp.concatenate([pc_parts[(c, r)] for c in range(2) for r in range(4)], axis=1).astype(BF16)
    o_cache = jnp.einsum('nqk,ndk->nqd', p8, cv_ref[...].astype(BF16), preferred_element_type=F32)
    o_new = _dot(jnp.concatenate(pn_rows, axis=0), vn_mask)
    groups = []
    for r in range(4):
        oa = o_cache[:, r * L:(r + 1) * L, :].reshape(BLOCK, LANES)
        ob = o_cache[:, 32 + r * L:32 + (r + 1) * L, :].reshape(BLOCK, LANES)
        o = jnp.where(lo_half, oa, ob) + o_new[r * LANES:(r + 1) * LANES]
        groups.append(o * jnp.where(lo_half, inv[r], inv[r + 4]))
    y_att = _attn_head_norm(jnp.concatenate(groups, axis=1), gattn_ref[...])

    sk_ref[:, 0:WINDOW - L, :] = jnp.swapaxes(ck_ref[...], 1, 2)[:, L:WINDOW, :]
    sk_ref[:, WINDOW - L:WINDOW, :] = k_new.reshape(ns, L, LANES)
    sv_ref[:, 0:WINDOW - L, :] = jnp.swapaxes(cv_ref[...], 1, 2)[:, L:WINDOW, :]
    sv_ref[:, WINDOW - L:WINDOW, :] = v_new.reshape(ns, L, LANES)

    b_c = [xg[:, 4 + h:5 + h] for h in hs]
    li_c = [xg[:, h:h + 1] for h in hs]
    b_r = [xgt[4 + h:5 + h, :] for h in hs]
    li_r = [xgt[h:h + 1, :] for h in hs]
    bl_c = [blast[:, 4 + h:5 + h] for h in hs]
    m_prev = [m_rep[:, h:h + 1] for h in hs]
    dmat = [jnp.where(causal_seq, b_c[h] - b_r[h] + li_r[h], -jnp.inf) for h in hs]
    m_inter = [b_c[h] + m_prev[h] for h in hs]
    m_t = [jnp.maximum(m_inter[h], jnp.max(dmat[h], axis=1, keepdims=True)) for h in hs]
    w_inter = [jnp.exp(m_inter[h] - m_t[h]) for h in hs]
    sc = [s_qk[h] * jnp.exp(dmat[h] - m_t[h]) for h in hs]
    num_intra = [_dot(sc[h].astype(BF16), vh_b[h]) for h in hs]

    m_end3 = [jnp.broadcast_to(m_t[h], (BLOCK, LANES)).reshape(ns, L, LANES)[:, L - 1:L, :] for h in hs]
    m_end = [jnp.broadcast_to(m_end3[h], (ns, L, LANES)).reshape(BLOCK, LANES)[:, 0:1] for h in hs]
    decay = [jnp.exp(bl_c[h] + m_prev[h] - m_end[h]) for h in hs]
    kw = [km[h] * jnp.exp(bl_c[h] - b_c[h] + li_c[h] - m_end[h]) for h in hs]
    seq_of_row = ti >> 3
    m_all = jnp.zeros((BLOCK, LANES), F32)
    for h in hs:
        kw_b = kw[h].astype(BF16)
        k_big = jnp.concatenate(
            [jnp.where(seq_of_row == g, kw_b, jnp.zeros_like(kw_b)) for g in range(ns)], axis=1)
        c_up = _dot_tn(vh_b[h], k_big)
        c_up3 = jnp.stack([c_up[:, g * M_HEAD_DIM:(g + 1) * M_HEAD_DIM] for g in range(ns)], axis=0)
        decay3 = jnp.broadcast_to(decay[h], (BLOCK, LANES)).reshape(ns, L, LANES)[:, 0:1, :]
        n_old = sn_ref[:, h:h + 1, :]
        sco_ref[:, h] = decay3 * c_old[h] + c_up3
        sno_ref[:, h:h + 1, :] = decay3 * n_old + jnp.sum(kw[h].reshape(ns, L, LANES), axis=1, keepdims=True)
        m_all = jnp.where(lane == h, m_t[h], m_all)
    m_rows[...] = m_all.reshape(ns, L, LANES)
    m_acc[pl.ds(pl.multiple_of(step * ns, ns), ns), :] = m_rows[:, L - 1, :]

    y_m = []
    for h in hs:
        num_inter = jnp.concatenate([cq[h][g * L:(g + 1) * L, g * M_HEAD_DIM:(g + 1) * M_HEAD_DIM] for g in range(ns)], axis=0)
        n_rep = jnp.broadcast_to(sn_ref[:, h:h + 1, :], (ns, L, LANES)).reshape(BLOCK, LANES)
        num = num_intra[h] + w_inter[h] * num_inter
        den = jnp.sum(sc[h], axis=1, keepdims=True) + w_inter[h] * jnp.sum(qm[h] * n_rep, axis=1, keepdims=True)
        hh = num / jnp.maximum(jnp.abs(den), jnp.exp(-m_t[h]))
        hm = _sigmoid(z[:, COL_O + h * M_HEAD_DIM:COL_O + (h + 1) * M_HEAD_DIM]) * hh
        y_m.append(_mlstm_head_norm(hm, gml_ref[:, h * M_HEAD_DIM:(h + 1) * M_HEAD_DIM]))

    ymix_ref[...] = jnp.concatenate([y_att] + y_m, axis=1).astype(BF16)

    @pl.when(step == pl.num_programs(0) - 1)
    def _write_m_state():
        smo_ref[...] = m_acc[...].T[0:M_HEADS, :]


def _const_spec(shape):
    nd = len(shape)
    return pl.BlockSpec(shape, lambda *_: (0,) * nd, pipeline_mode=pl.Buffered(1))


def _mixer_weight_specs():
    return [
        _const_spec((D_MODEL, IN_MAIN + IN_GATES)),
        _const_spec((D_MODEL, IN_PAD - IN_MAIN)),
        _const_spec((CONV_W, MLSTM_W)),
        _const_spec((1, MLSTM_W)),
        _const_spec((M_HEADS, M_HEAD_DIM, 2 * M_HEAD_DIM)),
        pl.BlockSpec(memory_space=pltpu.SMEM),
        pl.BlockSpec(memory_space=pltpu.SMEM),
        pl.BlockSpec(memory_space=pltpu.SMEM),
        _const_spec((1, ATT_W)),
        _const_spec((1, MLSTM_W)),
    ]


def _post_weight_specs(d):
    hbm = pl.BlockSpec(memory_space=pl.ANY)
    return [hbm, _const_spec((1, d)), _const_spec((1, d)), hbm, hbm, hbm, _const_spec((1, d)), _const_spec((1, d))]


def _prompt_layer(x_prompt, meta_tokens, x_sample_rows, ymix_sample, mixer_w, post_w):
    b, s, d = x_prompt.shape
    state_shapes = [((BLOCK, KV_W), BF16)] * 4 + [
        ((8, MLSTM_W), F32), ((M_HEADS, M_HEAD_DIM, 2 * M_HEAD_DIM), F32), ((8, LANES), F32)]
    rows = STEP_BLOCKS * BLOCK
    npair = s // rows
    total = b * npair
    n_pair_steps = total + 2
    n_sample_steps = x_sample_rows.shape[0] // rows
    steps = n_pair_steps + n_sample_steps

    def block_map(lag):
        def index(t):
            tb = jnp.clip(t - lag, 0, total - 1)
            return (tb // npair, tb % npair, 0)
        return index

    def smap(t):
        return (jnp.clip(t - 1, 0, total - 1) // npair, 0, 0)

    def sample_map(t):
        return (jnp.clip(t - n_pair_steps, 0, n_sample_steps - 1), 0)

    out_shape = (
        jax.ShapeDtypeStruct((b, s, d), F32),
        jax.ShapeDtypeStruct(x_sample_rows.shape, F32),
        jax.ShapeDtypeStruct((b, BLOCK, KV_W), F32),
        jax.ShapeDtypeStruct((b, BLOCK, KV_W), F32),
        jax.ShapeDtypeStruct((CONV_W - 1, b, MLSTM_W), F32),
        jax.ShapeDtypeStruct((b, M_HEADS, M_HEAD_DIM, M_HEAD_DIM), F32),
        jax.ShapeDtypeStruct((b, M_HEADS, M_HEAD_DIM), F32),
        jax.ShapeDtypeStruct((b, M_HEADS), F32),
    )
    out_specs = (
        pl.BlockSpec((None, rows, d), block_map(2)),
        pl.BlockSpec((rows, d), sample_map),
        pl.BlockSpec((None, BLOCK, KV_W), smap),
        pl.BlockSpec((None, BLOCK, KV_W), smap),
        pl.BlockSpec((CONV_W - 1, b, MLSTM_W), lambda t: (0, 0, 0)),
        pl.BlockSpec((None, M_HEADS, M_HEAD_DIM, M_HEAD_DIM), lambda t: smap(t) + (0,)),
        pl.BlockSpec((None, M_HEADS, M_HEAD_DIM), smap),
        pl.BlockSpec((b, M_HEADS), lambda t: (0, 0)),
    )
    return pl.pallas_call(
        functools.partial(_pair_kernel, npair=npair, n_pair_steps=n_pair_steps),
        grid=(steps,),
        in_specs=[pl.BlockSpec((None, rows, d), block_map(0)), pl.BlockSpec((None, rows, d), block_map(2)),
                  _const_spec((N_META, d)), pl.BlockSpec((rows, d), sample_map), pl.BlockSpec((rows, d), sample_map)]
        + _mixer_weight_specs() + _post_weight_specs(d),
        out_specs=out_specs,
        out_shape=out_shape,
        scratch_shapes=[
            pltpu.VMEM((d, d), BF16), pltpu.VMEM((d, D_FF), BF16), pltpu.VMEM((d, D_FF), BF16),
            pltpu.VMEM((D_FF, d), BF16),
            pltpu.VMEM((WEIGHT_SLOTS, WEIGHT_ROWS, D_FF), F32),
            pltpu.SemaphoreType.DMA((2 * WEIGHT_SLOTS + 2 * STEP_BLOCKS,)),
        ] + [pltpu.VMEM(shape, dt) for shape, dt in state_shapes] + [
            pltpu.VMEM((rows, IN_PAD), F32),
            pltpu.VMEM((rows, IN_PAD), F32),
            pltpu.VMEM((BLOCK, 2 * KV_W), F32),
        ] + [pltpu.VMEM((BLOCK, KV_W), BF16)] * 4 + [
            pltpu.VMEM((rows + 8, MLSTM_W), F32),
            pltpu.VMEM((M_HEADS, M_HEAD_DIM, 2 * M_HEAD_DIM), F32),
            pltpu.VMEM((8, LANES), F32),
            pltpu.VMEM((rows, d), BF16),
        ],
        compiler_params=pltpu.CompilerParams(
            dimension_semantics=("arbitrary",), vmem_limit_bytes=VMEM_LIMIT),
        name="prompt_layer",
    )(x_prompt, x_prompt, meta_tokens, x_sample_rows, ymix_sample, *mixer_w, *post_w)


def _sample_mixer(x_sample, ck, cv, s_conv, s_c, s_n, s_m, mixer_w):
    n, l, d = x_sample.shape
    assert n == LANES, "the m-state transposes assume one lane per sequence"
    t = SEQ_TILE
    m3 = lambda i: (i, 0, 0)
    m4 = lambda i: (i, 0, 0, 0)
    out_shape = (
        jax.ShapeDtypeStruct((n * l, d), BF16),
        jax.ShapeDtypeStruct((n, WINDOW, KV_W), F32),
        jax.ShapeDtypeStruct((n, WINDOW, KV_W), F32),
        jax.ShapeDtypeStruct((CONV_W - 1, n, MLSTM_W), F32),
        jax.ShapeDtypeStruct((n, M_HEADS, M_HEAD_DIM, M_HEAD_DIM), F32),
        jax.ShapeDtypeStruct((n, M_HEADS, M_HEAD_DIM), F32),
        jax.ShapeDtypeStruct((M_HEADS, n), F32),
    )
    m_state_spec = pl.BlockSpec((M_HEADS, n), lambda i: (0, 0))
    out_specs = (
        pl.BlockSpec((t * l, d), lambda i: (i, 0)),
        pl.BlockSpec((t, WINDOW, KV_W), m3),
        pl.BlockSpec((t, WINDOW, KV_W), m3),
        pl.BlockSpec((CONV_W - 1, t, MLSTM_W), lambda i: (0, i, 0)),
        pl.BlockSpec((t, M_HEADS, M_HEAD_DIM, M_HEAD_DIM), m4),
        pl.BlockSpec((t, M_HEADS, M_HEAD_DIM), m3),
        m_state_spec,
    )
    in_specs = [
        pl.BlockSpec((t, l, d), m3),
        pl.BlockSpec((t, WINDOW, KV_W), m3),
        pl.BlockSpec((t, WINDOW, KV_W), m3),
        pl.BlockSpec((CONV_W - 1, t, MLSTM_W), lambda i: (0, i, 0)),
        pl.BlockSpec((t, M_HEADS, M_HEAD_DIM, M_HEAD_DIM), m4),
        pl.BlockSpec((t, M_HEADS, M_HEAD_DIM), m3),
        m_state_spec,
    ] + _mixer_weight_specs()
    return pl.pallas_call(
        _sample_kernel,
        grid=(n // t,),
        in_specs=in_specs,
        out_specs=out_specs,
        out_shape=out_shape,
        scratch_shapes=[pltpu.VMEM((t, 16, MLSTM_W), F32),
                        pltpu.VMEM((LANES, LANES), F32),
                        pltpu.VMEM((LANES, LANES), F32),
                        pltpu.VMEM((t, l, LANES), F32)],
        compiler_params=pltpu.CompilerParams(
            dimension_semantics=("arbitrary",), vmem_limit_bytes=VMEM_LIMIT),
        name="sample_mixer",
    )(x_sample, ck, cv, s_conv, s_c, s_n, s_m, *mixer_w)


def kernel(x_prompt, x_sample, cache_k, cache_v, state_conv, state_C, state_n, state_m, meta_tokens,
           w_in, w_conv, b_conv, w_mq, w_mk, b_i, b_f, attn_sinks, g_attn, g_mlstm, w_out,
           ln1_g, ln1_b, w_gate, w_up, w_down, ln2_g, ln2_b):
    b, s, d = x_prompt.shape
    n, l, _ = x_sample.shape

    w0 = w_in[0].astype(BF16)
    w_in_parts = (w0, jnp.pad(w0[:, IN_MAIN:], ((0, 0), (0, IN_PAD - IN_MAIN - IN_GATES))))
    wqk = jnp.concatenate([w_mq[0], w_mk[0]], axis=-1).astype(BF16)
    mixer_w = w_in_parts + (w_conv[0], b_conv[0].reshape(1, MLSTM_W), wqk, b_i[0], b_f[0], attn_sinks[0],
                            g_attn[0].reshape(1, ATT_W), g_mlstm[0].reshape(1, MLSTM_W))
    post_w = (w_out[0], ln1_g[0].reshape(1, d), ln1_b[0].reshape(1, d), w_gate[0], w_up[0], w_down[0],
              ln2_g[0].reshape(1, d), ln2_b[0].reshape(1, d))

    key_minor = lambda c: c[0].transpose(0, 2, 3, 1).reshape(n, KV_W, WINDOW)
    ymix_s, sk, sv, scv, s_c, s_n, s_mo = _sample_mixer(
        x_sample, key_minor(cache_k), key_minor(cache_v), state_conv[0].swapaxes(0, 1), state_C[0], state_n[0],
        state_m[0].T, mixer_w)

    y_prompt, y_sample, pk, pv, pconv, p_c, p_n, p_m = _prompt_layer(
        x_prompt, meta_tokens, x_sample.reshape(n * l, d), ymix_s, mixer_w, post_w)

    kv5 = lambda a: a.reshape(1, a.shape[0], WINDOW, 2, HEAD_DIM)
    kv5_key_minor = lambda a: a.reshape(a.shape[0], 2, HEAD_DIM, WINDOW).transpose(0, 3, 1, 2)[None]
    return (y_prompt, y_sample.reshape(n, l, d),
            kv5_key_minor(pk), kv5_key_minor(pv), pconv.swapaxes(0, 1)[None], p_c[None], p_n[None], p_m[None],
            kv5(sk), kv5(sv), scv.swapaxes(0, 1)[None], s_c[None], s_n[None], s_mo.T[None])
```
